```python
import jax, jax.numpy as jnp
from jax import lax
import numpy as np

D_MODEL = 1024
BATCH = 4
SEQ = 4096
DEPTH = 1
DEC_BATCH = 32
DEC_SEQ = 1
PAST_LEN = 16384
PAGE_SIZE = 128

HEAD_DIM = 64
MIX_WIDTH = D_MODEL
ATTN_WIDTH = MIX_WIDTH // 2
RWKV_WIDTH = MIX_WIDTH - ATTN_WIDTH
N_ATTN_HEADS = ATTN_WIDTH // HEAD_DIM
N_RWKV_HEADS = RWKV_WIDTH // HEAD_DIM

DILATED_PATTERNS = ((128, 1), (512, 4), (2048, 16))
MAX_WINDOW = max(w for w, _ in DILATED_PATTERNS)
ATTN_QBLOCK = 128
ROPE_THETA = 10000.0

DECAY_LORA = 64
AAA_LORA = 64
GATE_LORA = 128
RWKV_COLS = 3 * RWKV_WIDTH + DECAY_LORA + AAA_LORA + GATE_LORA
RWKV_SPLITS = (RWKV_WIDTH, 2 * RWKV_WIDTH, 3 * RWKV_WIDTH,
               3 * RWKV_WIDTH + DECAY_LORA, 3 * RWKV_WIDTH + DECAY_LORA + AAA_LORA)
IN_COLS = 3 * ATTN_WIDTH + RWKV_COLS

N_EXPERTS = 32
TOP_K = 4
D_EXPERT = D_MODEL
SWIGLU_ALPHA = 1.702
SWIGLU_LIMIT = 7.0
MOE_BLOCK = 128

NORM_EPS = 1e-6
GN_EPS = 64e-5

kernel_name = 'hymba_rwkv7_dilated_moe_step'


def rms_norm(x, g):
    xf = x.astype(jnp.float32)
    y = xf * lax.rsqrt(jnp.mean(xf * xf, axis=-1, keepdims=True) + NORM_EPS)
    return y * g.astype(jnp.float32)


def rope(x, pos):
    half = HEAD_DIM // 2
    inv_freq = 1.0 / (ROPE_THETA ** (jnp.arange(0, HEAD_DIM, 2, dtype=jnp.float32) / HEAD_DIM))
    ang = pos.astype(jnp.float32)[:, None] * inv_freq[None, :]
    cos = jnp.cos(ang)[None, :, None, :]
    sin = jnp.sin(ang)[None, :, None, :]
    xf = x.astype(jnp.float32)
    x1, x2 = xf[..., :half], xf[..., half:]
    return jnp.concatenate([x1 * cos - x2 * sin, x2 * cos + x1 * sin], axis=-1)


def dilated_attention(q, k_all, v_all, q_off):
    B, Tq, H, Dh = q.shape
    qb = ATTN_QBLOCK if Tq % ATTN_QBLOCK == 0 else Tq
    n_blk = Tq // qb
    scale = 1.0 / np.sqrt(HEAD_DIM).astype(np.float32)
    q_blocks = jnp.moveaxis(q.astype(jnp.float32).reshape(B, n_blk, qb, H, Dh), 1, 0)

    def one_block(args):
        blk, q_blk = args
        q_idx = q_off + blk * qb + jnp.arange(qb)
        outs, lses = [], []
        for (w, d) in DILATED_PATTERNS:
            dist = jnp.arange(w // d + 1) * d
            idx = q_idx[:, None] - dist[None, :]
            valid = idx >= 0
            idx = jnp.maximum(idx, 0)
            kg = k_all[:, idx].astype(jnp.float32)
            vg = v_all[:, idx].astype(jnp.float32)
            s = jnp.einsum('bqhd,bqkhd->bhqk', q_blk, kg) * scale
            s = jnp.where(valid[None, None], s, -jnp.inf)
            lse = jax.nn.logsumexp(s, axis=-1)
            p = jnp.exp(s - lse[..., None])
            outs.append(jnp.einsum('bhqk,bqkhd->bqhd', p, vg))
            lses.append(jnp.transpose(lse, (0, 2, 1)))
        wts = jax.nn.softmax(jnp.stack(lses, axis=0), axis=0)
        return jnp.sum(jnp.stack(outs, axis=0) * wts[..., None], axis=0)

    out = lax.map(one_block, (jnp.arange(n_blk), q_blocks))
    return jnp.moveaxis(out, 0, 1).reshape(B, Tq, H, Dh)


def rwkv7_group(p, shift0, wkv0, mu, w0, w2, a0, a2, g2, k_k, k_a, r_k, ln_w, ln_b):
    B, T, _ = p.shape
    p = p.astype(jnp.float32)
    prev = jnp.concatenate([shift0.astype(jnp.float32)[:, None, :], p[:, :-1]], axis=1)
    xs = p + mu * (prev - p)
    r, k, v, xw, xa, xg = jnp.split(xs, RWKV_SPLITS, axis=-1)
    w = -jax.nn.softplus(-(w0 + jnp.tanh(xw) @ w2)) - 0.5
    decay = jnp.exp(-jnp.exp(w))
    a = jax.nn.sigmoid(a0 + xa @ a2)
    g = jax.nn.sigmoid(xg) @ g2

    def heads(t):
        return t.reshape(B, T, N_RWKV_HEADS, HEAD_DIM)

    kk = heads(k * k_k)
    kk = kk / jnp.maximum(jnp.linalg.norm(kk, axis=-1, keepdims=True), 1e-12)
    k = k * (1.0 + (a - 1.0) * k_a)
    rh, dh, kh, vh, ah = heads(r), heads(decay), heads(k), heads(v), heads(a)

    def step(S, inp):
        r_t, d_t, k_t, v_t, a_t, b_t = inp
        sa = jnp.einsum('bhvk,bhk->bhv', S, a_t)
        S = S * d_t[:, :, None, :] + sa[..., None] * b_t[:, :, None, :] + v_t[..., None] * k_t[:, :, None, :]
        return S, jnp.einsum('bhvk,bhk->bhv', S, r_t)

    seq = tuple(jnp.moveaxis(t, 1, 0) for t in (rh, dh, kh, vh, -kk, kk * ah))
    S_final, ys = lax.scan(step, wkv0.astype(jnp.float32), seq)
    y = jnp.moveaxis(ys, 0, 1)
    mean = jnp.mean(y, axis=-1, keepdims=True)
    var = jnp.mean(jnp.square(y - mean), axis=-1, keepdims=True)
    y = ((y - mean) * lax.rsqrt(var + GN_EPS)).reshape(B, T, RWKV_WIDTH) * ln_w + ln_b
    bonus = jnp.sum(rh * kh * r_k, axis=-1, keepdims=True) * vh
    out = (y + bonus.reshape(B, T, RWKV_WIDTH)) * g
    return out, p[:, -1], S_final


def moe_ffn(h, router_w, router_b, w_gu, b_gu, w_down, b_down):
    B, T, D = h.shape
    xt = h.reshape(-1, D).astype(jnp.float32)
    n_tok = xt.shape[0]
    logits = xt @ router_w + router_b
    top_val, top_idx = lax.top_k(logits, TOP_K)
    top_w = jax.nn.softmax(top_val, axis=-1)
    n_assign = n_tok * TOP_K
    blk = max(1, min(MOE_BLOCK, n_assign // N_EXPERTS))
    n_blocks = -(-n_assign // blk) + N_EXPERTS
    flat_e = top_idx.reshape(-1)
    flat_tok = jnp.repeat(jnp.arange(n_tok, dtype=jnp.int32), TOP_K)
    flat_w = top_w.reshape(-1)
    order = jnp.argsort(flat_e)
    se = flat_e[order]
    counts = jnp.bincount(flat_e, length=N_EXPERTS)
    padded = (counts + blk - 1) // blk * blk
    pad_end = jnp.cumsum(padded)
    pad_start = pad_end - padded
    start = jnp.cumsum(counts) - counts
    dest = pad_start[se] + (jnp.arange(n_assign) - start[se])
    slot_tok = jnp.full((n_blocks * blk,), n_tok, jnp.int32).at[dest].set(flat_tok[order])
    slot_w = jnp.zeros((n_blocks * blk,), jnp.float32).at[dest].set(flat_w[order])
    blk_e = jnp.minimum(jnp.searchsorted(pad_end, jnp.arange(n_blocks) * blk, side='right'), N_EXPERTS - 1)
    x_pad = jnp.concatenate([xt, jnp.zeros((1, D), jnp.float32)], axis=0)

    def expert_block(args):
        e, toks = args
        gu = x_pad[toks] @ w_gu[e] + b_gu[e]
        gate = jnp.minimum(gu[:, :D_EXPERT], SWIGLU_LIMIT)
        up = jnp.clip(gu[:, D_EXPERT:], -SWIGLU_LIMIT, SWIGLU_LIMIT)
        act = (up + 1.0) * gate * jax.nn.sigmoid(gate * SWIGLU_ALPHA)
        return act @ w_down[e] + b_down[e]

    outs = lax.map(expert_block, (blk_e, slot_tok.reshape(n_blocks, blk)))
    y = jnp.zeros((n_tok + 1, D), jnp.float32).at[slot_tok].add(outs.reshape(-1, D) * slot_w[:, None])
    return y[:n_tok].reshape(B, T, D)


def layer_forward(x, c, pos0, k_past, v_past, wkv0, shift0,
                  w_ada, b_ada, norm1_g, norm2_g, w_in, q_norm_g, k_norm_g,
                  rwkv_mu, rwkv_w0, rwkv_w2, rwkv_a0, rwkv_a2, rwkv_g2,
                  rwkv_k_k, rwkv_k_a, rwkv_r_k, rwkv_ln_w, rwkv_ln_b, w_out,
                  router_w, router_b, moe_w_gu, moe_b_gu, moe_w_down, moe_b_down):
    B, T, _ = x.shape
    mod = jax.nn.silu(c.astype(jnp.float32)) @ w_ada + b_ada
    sh1, sc1, gt1, sh2, sc2, gt2 = jnp.split(mod[:, None, :], 6, axis=-1)
    h = rms_norm(x, norm1_g) * (1.0 + sc1) + sh1
    proj = h @ w_in
    q, k, v = jnp.split(proj[..., :3 * ATTN_WIDTH], 3, axis=-1)
    q = q.reshape(B, T, N_ATTN_HEADS, HEAD_DIM)
    k = k.reshape(B, T, N_ATTN_HEADS, HEAD_DIM)
    v = v.reshape(B, T, N_ATTN_HEADS, HEAD_DIM)
    pos = pos0 + jnp.arange(T)
    q = rope(rms_norm(q, q_norm_g), pos)
    k = rope(rms_norm(k, k_norm_g), pos)
    k_all = jnp.concatenate([k_past.astype(k.dtype), k], axis=1)
    v_all = jnp.concatenate([v_past.astype(v.dtype), v], axis=1)
    attn = dilated_attention(q, k_all, v_all, k_past.shape[1]).reshape(B, T, ATTN_WIDTH)
    rw, shift_new, wkv_new = rwkv7_group(proj[..., 3 * ATTN_WIDTH:], shift0, wkv0,
                                         rwkv_mu, rwkv_w0, rwkv_w2, rwkv_a0, rwkv_a2, rwkv_g2,
                                         rwkv_k_k, rwkv_k_a, rwkv_r_k, rwkv_ln_w, rwkv_ln_b)
    x = x + gt1 * (jnp.concatenate([attn, rw], axis=-1) @ w_out)
    h2 = rms_norm(x, norm2_g) * (1.0 + sc2) + sh2
    x = x + gt2 * moe_ffn(h2, router_w, router_b, moe_w_gu, moe_b_gu, moe_w_down, moe_b_down)
    return x, k, v, wkv_new, shift_new


def setup_inputs(seed: int = 0) -> dict:
    key = jax.random.key(seed)
    ks = jax.random.split(key, 40)

    def nrm(i, shape, s):
        return jax.random.normal(ks[i], shape, jnp.float32) * s

    buf = min(MAX_WINDOW, PAST_LEN)
    L, D, E, F = DEPTH, D_MODEL, N_EXPERTS, D_EXPERT
    return {
        'x_prompt': nrm(0, (BATCH, SEQ, D), 1.0),
        'x_sample': nrm(1, (DEC_BATCH, DEC_SEQ, D), 1.0),
        'cache_k': nrm(2, (L, DEC_BATCH, buf, N_ATTN_HEADS, HEAD_DIM), 1.0),
        'cache_v': nrm(3, (L, DEC_BATCH, buf, N_ATTN_HEADS, HEAD_DIM), 1.0),
        'state_wkv': nrm(4, (L, DEC_BATCH, N_RWKV_HEADS, HEAD_DIM, HEAD_DIM), 0.3),
        'state_shift': nrm(5, (L, DEC_BATCH, RWKV_COLS), 1.0),
        'c_prompt': nrm(6, (BATCH, D), 1.0),
        'c_sample': nrm(7, (DEC_BATCH, D), 1.0),
        'w_ada': nrm(8, (L, D, 6 * D), 0.5 * D ** -0.5),
        'b_ada': nrm(9, (L, 6 * D), 0.02),
        'norm1_g': 1.0 + nrm(10, (L, D), 0.02),
        'norm2_g': 1.0 + nrm(11, (L, D), 0.02),
        'w_in': nrm(12, (L, D, IN_COLS), D ** -0.5),
        'q_norm_g': 1.0 + nrm(13, (L, HEAD_DIM), 0.02),
        'k_norm_g': 1.0 + nrm(14, (L, HEAD_DIM), 0.02),
        'rwkv_mu': jax.random.uniform(ks[15], (L, RWKV_COLS), jnp.float32),
        'rwkv_w0': nrm(16, (L, RWKV_WIDTH), 0.5),
        'rwkv_w2': nrm(17, (L, DECAY_LORA, RWKV_WIDTH), DECAY_LORA ** -0.5),
        'rwkv_a0': nrm(18, (L, RWKV_WIDTH), 0.1),
        'rwkv_a2': nrm(19, (L, AAA_LORA, RWKV_WIDTH), AAA_LORA ** -0.5),
        'rwkv_g2': nrm(20, (L, GATE_LORA, RWKV_WIDTH), GATE_LORA ** -0.5),
        'rwkv_k_k': 0.85 + nrm(21, (L, RWKV_WIDTH), 0.05),
        'rwkv_k_a': 1.0 + nrm(22, (L, RWKV_WIDTH), 0.05),
        'rwkv_r_k': nrm(23, (L, N_RWKV_HEADS, HEAD_DIM), 0.1),
        'rwkv_ln_w': 1.0 + nrm(24, (L, RWKV_WIDTH), 0.02),
        'rwkv_ln_b': nrm(25, (L, RWKV_WIDTH), 0.02),
        'w_out': nrm(26, (L, MIX_WIDTH, D), MIX_WIDTH ** -0.5),
        'router_w': nrm(27, (L, D, E), D ** -0.5),
        'router_b': nrm(28, (L, E), 0.01),
        'moe_w_gu': nrm(29, (L, E, D, 2 * F), D ** -0.5),
        'moe_b_gu': nrm(30, (L, E, 2 * F), 0.02),
        'moe_w_down': nrm(31, (L, E, F, D), F ** -0.5),
        'moe_b_down': nrm(32, (L, E, D), 0.02),
    }


def reference(x_prompt, x_sample, cache_k, cache_v, state_wkv, state_shift, c_prompt, c_sample,
              w_ada, b_ada, norm1_g, norm2_g, w_in, q_norm_g, k_norm_g,
              rwkv_mu, rwkv_w0, rwkv_w2, rwkv_a0, rwkv_a2, rwkv_g2,
              rwkv_k_k, rwkv_k_a, rwkv_r_k, rwkv_ln_w, rwkv_ln_b, w_out,
              router_w, router_b, moe_w_gu, moe_b_gu, moe_w_down, moe_b_down):
    B, T, _ = x_prompt.shape
    keep = min(MAX_WINDOW, T)
    empty_kv = jnp.zeros((B, 0, N_ATTN_HEADS, HEAD_DIM), jnp.float32)
    zero_wkv = jnp.zeros((B, N_RWKV_HEADS, HEAD_DIM, HEAD_DIM), jnp.float32)
    zero_shift = jnp.zeros((B, RWKV_COLS), jnp.float32)
    yp, ys = x_prompt, x_sample
    kp_l, vp_l, sp_l, hp_l, ks_l, vs_l, ss_l, hs_l = [], [], [], [], [], [], [], []
    for l in range(DEPTH):
        lw = (w_ada[l], b_ada[l], norm1_g[l], norm2_g[l], w_in[l], q_norm_g[l], k_norm_g[l],
              rwkv_mu[l], rwkv_w0[l], rwkv_w2[l], rwkv_a0[l], rwkv_a2[l], rwkv_g2[l],
              rwkv_k_k[l], rwkv_k_a[l], rwkv_r_k[l], rwkv_ln_w[l], rwkv_ln_b[l], w_out[l],
              router_w[l], router_b[l], moe_w_gu[l], moe_b_gu[l], moe_w_down[l], moe_b_down[l])
        yp, kp, vp, sp, hp = layer_forward(yp, c_prompt, 0, empty_kv, empty_kv, zero_wkv, zero_shift, *lw)
        ys, kn, vn, sn, hn = layer_forward(ys, c_sample, PAST_LEN, cache_k[l], cache_v[l],
                                           state_wkv[l], state_shift[l], *lw)
        kp_l.append(kp[:, T - keep:])
        vp_l.append(vp[:, T - keep:])
        sp_l.append(sp)
        hp_l.append(hp)
        ks_l.append(kn)
        vs_l.append(vn)
        ss_l.append(sn)
        hs_l.append(hn)
    return (yp, ys, jnp.stack(kp_l), jnp.stack(vp_l), jnp.stack(sp_l), jnp.stack(hp_l),
            jnp.stack(ks_l), jnp.stack(vs_l), jnp.stack(ss_l), jnp.stack(hs_l))
```

```python
import functools

import numpy as np
import jax
import jax.numpy as jnp
from jax import lax
from jax.experimental import pallas as pl
from jax.experimental.pallas import tpu as pltpu

F32 = jnp.float32
BF16 = jnp.bfloat16
I32 = jnp.int32

HEAD_DIM = 64
LANES = 128
DILATED_PATTERNS = ((128, 1), (512, 4), (2048, 16))
WINDOW_STEPS = 128
MAX_WINDOW = 2048
PAST_LEN = 16384
ROPE_THETA = 10000.0
NORM_EPS = 1e-6
GN_EPS = 64e-5
TOP_K = 4
SWIGLU_ALPHA = 1.702
SWIGLU_LIMIT = 7.0
RW_CHUNK = 128
MOE_BLOCK = 256
ROW_TILE = 512
TOK_TILE = 256
ROUTE_TILE = 512
VMEM_LIMIT = 56 * 1024 * 1024
NEG_BIG = -1e30


def _cparams(*sem):
    return pltpu.CompilerParams(dimension_semantics=sem, vmem_limit_bytes=VMEM_LIMIT)


def _dot(a, b):
    return jnp.dot(a.astype(BF16), b.astype(BF16), preferred_element_type=F32)


def _dot_nt(a, b):
    return lax.dot_general(a.astype(BF16), b.astype(BF16), (((1,), (1,)), ((), ())),
                           preferred_element_type=F32)


def _split2(a):
    hi = a.astype(BF16)
    lo = (a - hi.astype(F32)).astype(BF16)
    return hi, lo


def _split3(a):
    hi = a.astype(BF16)
    r1 = a - hi.astype(F32)
    mid = r1.astype(BF16)
    lo = (r1 - mid.astype(F32)).astype(BF16)
    return hi, mid, lo


def _dot_x(a, e):
    e = e.astype(BF16)
    hi, mid, lo = _split3(a)
    return (jnp.dot(hi, e, preferred_element_type=F32) + jnp.dot(mid, e, preferred_element_type=F32)
            + jnp.dot(lo, e, preferred_element_type=F32))


def _xdot(e, a):
    e = e.astype(BF16)
    hi, mid, lo = _split3(a)
    return (jnp.dot(e, hi, preferred_element_type=F32) + jnp.dot(e, mid, preferred_element_type=F32)
            + jnp.dot(e, lo, preferred_element_type=F32))


def _dot3(a, b):
    ah, al = _split2(a)
    bh, bl = _split2(b)
    return (jnp.dot(ah, bh, preferred_element_type=F32) + jnp.dot(ah, bl, preferred_element_type=F32)
            + jnp.dot(al, bh, preferred_element_type=F32))


def _dot3_nt(a, b):
    ah, al = _split2(a)
    bh, bl = _split2(b)
    dn = (((1,), (1,)), ((), ()))
    return (lax.dot_general(ah, bh, dn, preferred_element_type=F32)
            + lax.dot_general(ah, bl, dn, preferred_element_type=F32)
            + lax.dot_general(al, bh, dn, preferred_element_type=F32))


def _sigmoid(x):
    return 1.0 / (1.0 + jnp.exp(-x))


def _ada_kernel(c_ref, w_ref, b_ref, o_ref):
    c = c_ref[...]
    o_ref[...] = _dot3(c * _sigmoid(c), w_ref[...]) + b_ref[...]


def _ada(c_all, w_ada, b_ada):
    rows, d = c_all.shape
    n = w_ada.shape[1]
    tn = n // 4
    return pl.pallas_call(
        _ada_kernel,
        grid=(n // tn,),
        in_specs=[pl.BlockSpec((rows, d), lambda j: (0, 0)),
                  pl.BlockSpec((d, tn), lambda j: (0, j)),
                  pl.BlockSpec((1, tn), lambda j: (0, j))],
        out_specs=pl.BlockSpec((rows, tn), lambda j: (0, j)),
        out_shape=jax.ShapeDtypeStruct((rows, n), F32),
        compiler_params=_cparams("arbitrary"),
        name="ada",
    )(c_all, w_ada, b_ada.reshape(1, n))


def _inproj_kernel(x_ref, sh_ref, sc_ref, g_ref, w_ref, cos_ref, sin_ref, qg_ref, kg_ref, bd_ref,
                   q_ref, k_ref, v_ref, rw_ref, *, aw):
    x = x_ref[...]
    ms = jnp.mean(x * x, axis=-1, keepdims=True)
    h = x * lax.rsqrt(ms + NORM_EPS) * g_ref[...] * (1.0 + sc_ref[...]) + sh_ref[...]
    proj = jnp.dot(h.astype(BF16), w_ref[...], preferred_element_type=F32)
    cos = cos_ref[...]
    sin = sin_ref[...]
    lane = lax.broadcasted_iota(I32, (1, aw), 1)
    first_half = (lane % HEAD_DIM) < (HEAD_DIM // 2)
    bd = bd_ref[...]

    def norm_rope(t, g):
        tn = t * lax.rsqrt(_dot(t * t, bd) + NORM_EPS) * g
        rot = jnp.where(first_half, pltpu.roll(tn, aw - HEAD_DIM // 2, 1), pltpu.roll(tn, HEAD_DIM // 2, 1))
        return tn * cos + rot * sin

    scale = 1.0 / np.sqrt(HEAD_DIM).astype(np.float32)
    q_ref[...] = norm_rope(proj[:, :aw], qg_ref[...]) * scale
    k_ref[...] = norm_rope(proj[:, aw:2 * aw], kg_ref[...])
    v_ref[...] = proj[:, 2 * aw:3 * aw]
    rw_ref[...] = proj[:, 3 * aw:]


def _inproj(x3, mod3, g1, w_in_bf, cos_t, sin_t, qg, kg, bd_mean, tm):
    nb, t, d = x3.shape
    r = mod3.shape[1]
    ncol = w_in_bf.shape[1]
    aw = cos_t.shape[1]
    rwc = ncol - 3 * aw
    grid = (nb, t // tm)
    row = lambda b, i: (b, i, 0)
    const = lambda b, i: (0, 0)
    mod_spec = lambda s: pl.BlockSpec((None, r, d), (lambda b, i: (b, 0, s)) if r == 1 else (lambda b, i: (b, i, s)))
    outs = pl.pallas_call(
        functools.partial(_inproj_kernel, aw=aw),
        grid=grid,
        in_specs=[pl.BlockSpec((None, tm, d), row), mod_spec(0), mod_spec(1),
                  pl.BlockSpec((1, d), const), pl.BlockSpec((d, ncol), const),
                  pl.BlockSpec((tm, aw), lambda b, i: (i, 0)), pl.BlockSpec((tm, aw), lambda b, i: (i, 0)),
                  pl.BlockSpec((1, aw), const), pl.BlockSpec((1, aw), const), pl.BlockSpec((aw, aw), const)],
        out_specs=[pl.BlockSpec((None, tm, aw), row)] * 3 + [pl.BlockSpec((None, tm, rwc), row)],
        out_shape=[jax.ShapeDtypeStruct((nb, t, aw), F32)] * 3 + [jax.ShapeDtypeStruct((nb, t, rwc), F32)],
        compiler_params=_cparams("parallel", "parallel"),
        name="inproj",
    )(x3, mod3, mod3, g1, w_in_bf, cos_t, sin_t, qg, kg, bd_mean)
    return outs


def _attn_prompt_kernel(q_ref, k_ref, v_ref, o_ref, kp_ref, vp_ref, m_ref, l_ref, acc_ref, *, t, pad):
    nq = WINDOW_STEPS
    nk = 2 * WINDOW_STEPS
    kp_ref[pl.ds(0, pad), :] = jnp.zeros((pad, LANES), F32)
    vp_ref[pl.ds(0, pad), :] = jnp.zeros((pad, LANES), F32)
    kp_ref[pl.ds(pad, t), :] = k_ref[...]
    vp_ref[pl.ds(pad, t), :] = v_ref[...]
    head0 = lax.broadcasted_iota(I32, (1, LANES), 1) < HEAD_DIM
    qi = lax.broadcasted_iota(I32, (nq, nk), 0)
    kj = lax.broadcasted_iota(I32, (nq, nk), 1)
    steps_back = qi + nq - kj
    band = (steps_back >= 0) & (steps_back <= WINDOW_STEPS)
    has_past = kj >= nq

    for p, (_, d) in enumerate(DILATED_PATTERNS):
        def unit(u, carry, p=p, d=d):
            res = u % d
            blk = u // d
            q_start = res + d * nq * blk
            k_start = pad + q_start - d * nq
            if d == 1:
                rows_q = pl.ds(q_start, nq)
                rows_k = pl.ds(k_start, nk)
            else:
                rows_q = pl.ds(q_start, nq, stride=d)
                rows_k = pl.ds(k_start, nk, stride=d)
            q = q_ref[rows_q, :]
            kb = kp_ref[rows_k, :].astype(BF16)
            vb = vp_ref[rows_k, :].astype(BF16)
            valid = band & (has_past | (blk > 0))
            res_h = []
            for hsel in (head0, jnp.logical_not(head0)):
                qh = jnp.where(hsel, q, 0.0).astype(BF16)
                s = lax.dot_general(qh, kb, (((1,), (1,)), ((), ())), preferred_element_type=F32)
                s = jnp.where(valid, s, NEG_BIG)
                mx = jnp.max(s, axis=-1, keepdims=True)
                e = jnp.exp(s - mx)
                den = jnp.sum(e, axis=-1, keepdims=True)
                o = jnp.dot(e.astype(BF16), vb, preferred_element_type=F32)
                res_h.append((mx, den, o))
            m_ref[p, rows_q, :] = jnp.where(head0, res_h[0][0], res_h[1][0])
            l_ref[p, rows_q, :] = jnp.where(head0, res_h[0][1], res_h[1][1])
            acc_ref[p, rows_q, :] = jnp.where(head0, res_h[0][2], res_h[1][2])
            return carry

        lax.fori_loop(0, t // nq, unit, 0)

    rows = 256

    def merge(i, carry):
        sl = pl.ds(pl.multiple_of(i * rows, rows), rows)
        m0, m1, m2 = m_ref[0, sl, :], m_ref[1, sl, :], m_ref[2, sl, :]
        mm = jnp.maximum(jnp.maximum(m0, m1), m2)
        w0, w1, w2 = jnp.exp(m0 - mm), jnp.exp(m1 - mm), jnp.exp(m2 - mm)
        num = w0 * acc_ref[0, sl, :] + w1 * acc_ref[1, sl, :] + w2 * acc_ref[2, sl, :]
        den = w0 * l_ref[0, sl, :] + w1 * l_ref[1, sl, :] + w2 * l_ref[2, sl, :]
        o_ref[sl, :] = num / den
        return carry

    lax.fori_loop(0, t // rows, merge, 0)


def _attn_prompt(q, k, v):
    nb, t, aw = q.shape
    pad = MAX_WINDOW
    assert t % MAX_WINDOW == 0
    spec = pl.BlockSpec((None, t, LANES), lambda b, hp: (b, 0, hp))
    return pl.pallas_call(
        functools.partial(_attn_prompt_kernel, t=t, pad=pad),
        grid=(nb, aw // LANES),
        in_specs=[spec, spec, spec],
        out_specs=spec,
        out_shape=jax.ShapeDtypeStruct((nb, t, aw), F32),
        scratch_shapes=[pltpu.VMEM((pad + t, LANES), F32), pltpu.VMEM((pad + t, LANES), F32),
                        pltpu.VMEM((3, t, LANES), F32), pltpu.VMEM((3, t, LANES), F32),
                        pltpu.VMEM((3, t, LANES), F32)],
        compiler_params=_cparams("parallel", "parallel"),
        name="attn_prompt",
    )(q, k, v)


def _attn_sample_kernel(q_ref, kn_ref, vn_ref, k1_ref, k4_ref, k16_ref, v1_ref, v4_ref, v16_ref,
                        e_ref, et_ref, o_ref, *, bb):
    e = e_ref[...]
    et = et_ref[...]
    n_pat = len(DILATED_PATTERNS)

    def one(b, carry):
        q = q_ref[pl.ds(b, 1), :]
        kn = kn_ref[pl.ds(b, 1), :]
        vn = vn_ref[pl.ds(b, 1), :]
        s_self = _dot_x(jnp.broadcast_to(q * kn, (8, q.shape[1])), e)[0:1]
        scores = [_dot_x(kr[b] * q, e) for kr in (k1_ref, k4_ref, k16_ref)]
        mx = s_self
        for s in scores:
            mx = jnp.maximum(mx, jnp.max(s, axis=0, keepdims=True))
        p_self = n_pat * jnp.exp(s_self - mx)
        den = p_self
        num = _dot_x(jnp.broadcast_to(p_self, (8, LANES)), et)[0:1] * vn
        for s, vr in zip(scores, (v1_ref, v4_ref, v16_ref)):
            pr = jnp.exp(s - mx)
            den = den + jnp.sum(pr, axis=0, keepdims=True)
            num = num + jnp.sum(_dot_x(pr, et) * vr[b], axis=0, keepdims=True)
        den_f = _dot_x(jnp.broadcast_to(den, (8, LANES)), et)[0:1]
        o_ref[pl.ds(b, 1), :] = num / den_f
        return carry

    lax.fori_loop(0, bb, one, 0)


def _attn_sample(q, kn, vn, cache_k, cache_v, e_mat, et_mat):
    nb, aw = q.shape
    w = cache_k.shape[1]
    assert w == MAX_WINDOW
    bb = 8
    ws = WINDOW_STEPS
    views, specs = [], []
    for cache in (cache_k, cache_v):
        for (_, d) in DILATED_PATTERNS:
            views.append(cache.reshape(nb, w // d, d * aw))
            specs.append(pl.BlockSpec((bb, ws, aw), functools.partial(lambda i, blk: (i, blk, 0), blk=w // d // ws - 1)))
    row = pl.BlockSpec((bb, aw), lambda i: (i, 0))
    return pl.pallas_call(
        functools.partial(_attn_sample_kernel, bb=bb),
        grid=(nb // bb,),
        in_specs=[row, row, row] + specs + [pl.BlockSpec((aw, LANES), lambda i: (0, 0)),
                                             pl.BlockSpec((LANES, aw), lambda i: (0, 0))],
        out_specs=row,
        out_shape=jax.ShapeDtypeStruct((nb, aw), F32),
        compiler_params=_cparams("parallel"),
        name="attn_sample",
    )(q, kn, vn, *views, e_mat, et_mat)


def _rwprep_kernel(p_ref, prev_ref, first_ref, mu_ref, w0_ref, w2_ref, a0_ref, a2_ref, g2_ref, kk_ref, ka_ref,
                   bd_ref, r_ref, ld_ref, k_ref, v_ref, al_ref, be_ref, g_ref, *, rwid, whole_prev):
    p = p_ref[...]
    if whole_prev:
        prev = first_ref[...]
    else:
        tm = p.shape[0]
        before = jnp.where(pl.program_id(1) == 0, first_ref[...], prev_ref[7:8, :])
        rowi = lax.broadcasted_iota(I32, (tm, 1), 0)
        prev = jnp.where(rowi == 0, before, pltpu.roll(p, 1, 0))
    xs = p + mu_ref[...] * (prev - p)
    r = xs[:, :rwid]
    k = xs[:, rwid:2 * rwid]
    v = xs[:, 2 * rwid:3 * rwid]
    xwa = xs[:, 3 * rwid:3 * rwid + LANES]
    xg = xs[:, 3 * rwid + LANES:]
    z = w0_ref[...] + _dot3(jnp.tanh(xwa), w2_ref[...])
    softplus_neg = jnp.maximum(-z, 0.0) + jnp.log(1.0 + jnp.exp(-jnp.abs(z)))
    w = -softplus_neg - 0.5
    a = _sigmoid(a0_ref[...] + _dot3(xwa, a2_ref[...]))
    g = _dot3(_sigmoid(xg), g2_ref[...])
    kk = k * kk_ref[...]
    norm = jnp.sqrt(_dot_x(kk * kk, bd_ref[...]))
    kk = kk / jnp.maximum(norm, 1e-12)
    r_ref[...] = r
    ld_ref[...] = -jnp.exp(w)
    k_ref[...] = k * (1.0 + (a - 1.0) * ka_ref[...])
    v_ref[...] = v
    al_ref[...] = -kk
    be_ref[...] = kk * a
    g_ref[...] = g


def _rwprep(rw, first, mu, w0, w2p, a0, a2p, g2, k_k, k_a, bd_ones, tm, whole_prev):
    nb, t, rwc = rw.shape
    rwid = w0.shape[1]
    grid = (nb, t // tm)
    row = lambda b, i: (b, i, 0)
    const = lambda b, i: (0, 0)
    if whole_prev:
        prev_spec = pl.BlockSpec((None, tm, rwc), row)
        first_spec = pl.BlockSpec((None, tm, rwc), row)
    else:
        prev_spec = pl.BlockSpec((None, 8, rwc), lambda b, i: (b, jnp.maximum(i * (tm // 8) - 1, 0), 0))
        first_spec = pl.BlockSpec((None, 1, rwc), lambda b, i: (b, 0, 0))
    vec = pl.BlockSpec((1, rwid), const)
    out = pl.BlockSpec((None, tm, rwid), row)
    return pl.pallas_call(
        functools.partial(_rwprep_kernel, rwid=rwid, whole_prev=whole_prev),
        grid=grid,
        in_specs=[pl.BlockSpec((None, tm, rwc), row), prev_spec, first_spec,
                  pl.BlockSpec((1, rwc), const), vec, pl.BlockSpec((LANES, rwid), const),
                  vec, pl.BlockSpec((LANES, rwid), const), pl.BlockSpec((LANES, rwid), const), vec, vec,
                  pl.BlockSpec((rwid, rwid), const)],
        out_specs=[out] * 7,
        out_shape=[jax.ShapeDtypeStruct((nb, t, rwid), F32)] * 7,
        compiler_params=_cparams("parallel", "parallel"),
        name="rwprep",
    )(rw, rw, first, mu, w0, w2p, a0, a2p, g2, k_k, k_a, bd_ones)


def _rwchunk_kernel(r_ref, ld_ref, k_ref, v_ref, al_ref, be_ref, z0_ref, y_ref, zt_ref, z_ref, *, nb, unroll):
    c = RW_CHUNK
    ci = pl.program_id(1)

    @pl.when(ci == 0)
    def _():
        z_ref[...] = z0_ref[...]

    ti = lax.broadcasted_iota(I32, (c, c), 0)
    si = lax.broadcasted_iota(I32, (c, c), 1)
    low_incl = si <= ti
    low_strict = si < ti
    tri = jnp.where(low_incl, 1.0, 0.0).astype(BF16)
    eye = jnp.where(si == ti, 1.0, 0.0)
    lane = lax.broadcasted_iota(I32, (1, LANES), 1)
    head0 = lane < HEAD_DIM
    same_head = (ti < HEAD_DIM) == (si < HEAD_DIM)

    def one(b):
        ld = ld_ref[b]
        cum = _xdot(tri, ld)
        tot = cum[c - 1:c, :]
        e_in = jnp.exp(cum)
        e_ex = jnp.exp(cum - ld)
        e_neg = jnp.exp(-cum)
        e_end = jnp.exp(tot - cum)
        r, k, v, al, be = r_ref[b], k_ref[b], v_ref[b], al_ref[b], be_ref[b]
        at = al * e_ex
        rt = r * e_in
        rhs_t = jnp.concatenate([be * e_neg, k * e_neg], axis=0).astype(BF16)
        vb = v.astype(BF16)
        u0s, ats, y0s, rts = [], [], [], []
        for hsel in (head0, jnp.logical_not(head0)):
            at_h = jnp.where(hsel, at, 0.0)
            rt_h = jnp.where(hsel, rt, 0.0)
            lhs = jnp.concatenate([at_h, rt_h], axis=0).astype(BF16)
            a4 = lax.dot_general(lhs, rhs_t, (((1,), (1,)), ((), ())), preferred_element_type=F32)
            a_ab = jnp.where(low_strict, a4[:c, :c], 0.0)
            a_ak = jnp.where(low_strict, a4[:c, c:], 0.0)
            a_rb = jnp.where(low_incl, a4[c:, :c], 0.0)
            a_rk = jnp.where(low_incl, a4[c:, c:], 0.0)
            pw = a_ab
            inv = eye + a_ab
            n_sq = int(np.log2(c)) - 1
            for _ in range(n_sq):
                pw = _dot(pw, pw)
                inv = inv + _dot(inv, pw)
            akv = jnp.dot(a_ak.astype(BF16), vb, preferred_element_type=F32)
            x = _dot(inv, jnp.concatenate([akv, at_h], axis=1))
            u0_h, at2_h = x[:, :LANES], x[:, LANES:]
            y0_h = _dot(jnp.concatenate([a_rb, a_rk], axis=1), jnp.concatenate([u0_h, v], axis=0))
            rt2_h = rt_h + _dot(a_rb, at2_h)
            u0s.append(u0_h)
            ats.append(at2_h)
            y0s.append(y0_h)
            rts.append(rt2_h)
        u0 = jnp.where(head0, u0s[0], u0s[1])
        y0 = jnp.where(head0, y0s[0], y0s[1])
        z = z_ref[b]
        uy = _dot(jnp.concatenate([ats[0] + ats[1], rts[0] + rts[1]], axis=0), z)
        u = uy[:c] + u0
        y_ref[b] = uy[c:] + y0
        lhs_t = jnp.concatenate([be * e_end, k * e_end], axis=0)
        zadd = _dot(lhs_t.T, jnp.concatenate([u, v], axis=0))
        dcol = jnp.sum(jnp.where(si == ti, jnp.broadcast_to(jnp.exp(tot), (c, c)), 0.0), axis=1, keepdims=True)
        z_ref[b] = dcol * z + jnp.where(same_head, zadd, 0.0)

    if unroll:
        for b in range(nb):
            one(b)
    else:
        def body(b, carry):
            one(b)
            return carry
        lax.fori_loop(0, nb, body, 0)

    @pl.when(ci == pl.num_programs(1) - 1)
    def _():
        zt_ref[...] = z_ref[...]


def _rwchunk(r, ld, k, v, al, be, z0):
    nb, t, rwid = r.shape
    c = RW_CHUNK
    assert t % c == 0 and c == LANES
    npair = rwid // LANES
    seq = pl.BlockSpec((nb, c, LANES), lambda hp, ci: (0, ci, hp))
    st = pl.BlockSpec((nb, None, LANES, LANES), lambda hp, ci: (0, hp, 0, 0))
    return pl.pallas_call(
        functools.partial(_rwchunk_kernel, nb=nb, unroll=nb <= 4),
        grid=(npair, t // c),
        in_specs=[seq] * 6 + [st],
        out_specs=[seq, st],
        out_shape=[jax.ShapeDtypeStruct((nb, t, rwid), F32),
                   jax.ShapeDtypeStruct((nb, npair, LANES, LANES), F32)],
        scratch_shapes=[pltpu.VMEM((nb, LANES, LANES), F32)],
        compiler_params=_cparams("parallel", "arbitrary"),
        name="rwchunk",
    )(r, ld, k, v, al, be, z0)


def _outproj_kernel(x_ref, at_ref, y_ref, r_ref, k_ref, v_ref, g_ref, gt_ref, sh_ref, sc_ref, g2_ref,
                    wt_ref, wb_ref, rk_ref, lw_ref, lb_ref, bdm_ref, bd1_ref, rwt_ref, rb_ref,
                    x1_ref, h2_ref, lg_ref):
    y = y_ref[...]
    bdm = bdm_ref[...]
    mean = _dot_x(y, bdm)
    yc = y - mean
    var = _dot_x(yc * yc, bdm)
    yn = yc * lax.rsqrt(var + GN_EPS) * lw_ref[...] + lb_ref[...]
    v = v_ref[...]
    bonus = _dot_x(r_ref[...] * k_ref[...] * rk_ref[...], bd1_ref[...]) * v
    rw = (yn + bonus) * g_ref[...]
    mix = (jnp.dot(at_ref[...].astype(BF16), wt_ref[...], preferred_element_type=F32)
           + jnp.dot(rw.astype(BF16), wb_ref[...], preferred_element_type=F32))
    x1 = x_ref[...] + gt_ref[...] * mix
    x1_ref[...] = x1
    ms = jnp.mean(x1 * x1, axis=-1, keepdims=True)
    h2 = x1 * lax.rsqrt(ms + NORM_EPS) * g2_ref[...] * (1.0 + sc_ref[...]) + sh_ref[...]
    h2_ref[...] = h2
    lg_ref[...] = _dot3_nt(rwt_ref[...], h2) + rb_ref[...]


def _outproj(x3, attn, y, r, k, v, g, mod3, g2, wt_bf, wb_bf, r_k, ln_w, ln_b, bd_mean, bd_ones, rwt, rb, tm):
    nb, t, d = x3.shape
    rmod = mod3.shape[1]
    aw = attn.shape[2]
    ne = rwt.shape[0]
    nt = t // tm
    grid = (nb, nt)
    row = lambda b, i: (b, i, 0)
    const = lambda b, i: (0, 0)
    mod_spec = lambda s: pl.BlockSpec((None, rmod, d), (lambda b, i: (b, 0, s)) if rmod == 1 else (lambda b, i: (b, i, s)))
    half = pl.BlockSpec((None, tm, aw), row)
    vec = pl.BlockSpec((1, aw), const)
    return pl.pallas_call(
        _outproj_kernel,
        grid=grid,
        in_specs=[pl.BlockSpec((None, tm, d), row)] + [half] * 6 + [mod_spec(2), mod_spec(3), mod_spec(4),
                  pl.BlockSpec((1, d), const), pl.BlockSpec((aw, d), const), pl.BlockSpec((aw, d), const),
                  vec, vec, vec, pl.BlockSpec((aw, aw), const), pl.BlockSpec((aw, aw), const),
                  pl.BlockSpec((ne, d), const), pl.BlockSpec((ne, 1), const)],
        out_specs=[pl.BlockSpec((None, tm, d), row), pl.BlockSpec((None, tm, d), row),
                   pl.BlockSpec((ne, tm), lambda b, i: (0, b * nt + i))],
        out_shape=[jax.ShapeDtypeStruct((nb, t, d), F32), jax.ShapeDtypeStruct((nb, t, d), F32),
                   jax.ShapeDtypeStruct((ne, nb * t), F32)],
        compiler_params=_cparams("parallel", "parallel"),
        name="outproj",
    )(x3, attn, y, r, k, v, g, mod3, mod3, mod3, g2, wt_bf, wb_bf, r_k, ln_w, ln_b, bd_mean, bd_ones, rwt, rb)


def _route_kernel(lg_ref, idx_ref, wt_ref, rank_ref, cnt_ref, carry_ref, *, n_valid):
    i = pl.program_id(0)
    ne, tn = lg_ref.shape

    @pl.when(i == 0)
    def _():
        carry_ref[...] = jnp.zeros_like(carry_ref)

    lg = lg_ref[...]
    eidx = lax.broadcasted_iota(I32, (ne, tn), 0).astype(F32)
    tok = i * tn + lax.broadcasted_iota(I32, (1, tn), 1)
    live = tok < n_valid
    vals, hots = [], []
    for _ in range(TOP_K):
        mx = jnp.max(lg, axis=0, keepdims=True)
        pick = jnp.min(jnp.where(lg == mx, eidx, float(ne)), axis=0, keepdims=True)
        hot = eidx == pick
        vals.append(mx)
        hots.append(hot)
        idx_ref[pl.ds(len(vals) - 1, 1), :] = pick.astype(I32)
        lg = jnp.where(hot, -jnp.inf, lg)
    ex = [jnp.exp(vv - vals[0]) for vv in vals]
    den = ex[0] + ex[1] + ex[2] + ex[3]
    for kk in range(TOP_K):
        wt_ref[pl.ds(kk, 1), :] = ex[kk] / den
    hot_all = jnp.zeros((ne, tn), F32)
    for hot in hots:
        hot_all = hot_all + jnp.where(hot & live, 1.0, 0.0)
    ri = lax.broadcasted_iota(I32, (tn, tn), 0)
    cj = lax.broadcasted_iota(I32, (tn, tn), 1)
    upper = jnp.where(ri <= cj, 1.0, 0.0).astype(BF16)
    incl = jnp.dot(hot_all.astype(BF16), upper, preferred_element_type=F32)
    before = carry_ref[:, 0:1] + incl - hot_all
    for kk, hot in enumerate(hots):
        rank_ref[pl.ds(kk, 1), :] = jnp.sum(jnp.where(hot, before, 0.0), axis=0, keepdims=True).astype(I32)
    carry_ref[...] = carry_ref[...] + jnp.sum(hot_all, axis=1, keepdims=True)
    cnt_ref[...] = carry_ref[...]


def _route(logits_t, n_valid):
    ne, npad = logits_t.shape
    tn = ROUTE_TILE
    tile = lambda rows: pl.BlockSpec((rows, tn), lambda i: (0, i))
    return pl.pallas_call(
        functools.partial(_route_kernel, n_valid=n_valid),
        grid=(npad // tn,),
        in_specs=[tile(ne)],
        out_specs=[tile(TOP_K), tile(TOP_K), tile(TOP_K), pl.BlockSpec((ne, LANES), lambda i: (0, 0))],
        out_shape=[jax.ShapeDtypeStruct((TOP_K, npad), I32), jax.ShapeDtypeStruct((TOP_K, npad), F32),
                   jax.ShapeDtypeStruct((TOP_K, npad), I32), jax.ShapeDtypeStruct((ne, LANES), F32)],
        scratch_shapes=[pltpu.VMEM((ne, LANES), F32)],
        compiler_params=_cparams("arbitrary"),
        name="route",
    )(logits_t)


def _dest_kernel(idx_ref, rank_ref, start_ref, dest_ref, *, n_valid, trash):
    i = pl.program_id(0)
    ne = start_ref.shape[0]
    tn = idx_ref.shape[1]
    eidx = lax.broadcasted_iota(I32, (ne, tn), 0)
    tok = i * tn + lax.broadcasted_iota(I32, (1, tn), 1)
    start = start_ref[:, 0:1]
    for kk in range(TOP_K):
        base = jnp.sum(jnp.where(eidx == idx_ref[pl.ds(kk, 1), :], start, 0.0), axis=0, keepdims=True)
        dest_ref[pl.ds(kk, 1), :] = jnp.where(tok < n_valid, base.astype(I32) + rank_ref[pl.ds(kk, 1), :], trash)


def _dest(idx, rank, start, n_valid, trash):
    _, npad = idx.shape
    ne = start.shape[0]
    tn = ROUTE_TILE
    tile = pl.BlockSpec((TOP_K, tn), lambda i: (0, i))
    return pl.pallas_call(
        functools.partial(_dest_kernel, n_valid=n_valid, trash=trash),
        grid=(npad // tn,),
        in_specs=[tile, tile, pl.BlockSpec((ne, LANES), lambda i: (0, 0))],
        out_specs=tile,
        out_shape=jax.ShapeDtypeStruct((TOP_K, npad), I32),
        compiler_params=_cparams("parallel"),
        name="dest",
    )(idx, rank, jnp.broadcast_to(start.astype(F32)[:, None], (ne, LANES)))


def _row_copy_out(h_ref, xs_ref, sem, dest_ref, i, kk):
    return pltpu.make_async_copy(h_ref.at[pl.ds(i, 1), :], xs_ref.at[pl.ds(dest_ref[0, kk, i], 1), :], sem)


def _dispatch_kernel(dest_ref, h_ref, xs_in_ref, xs_ref, sem, *, rows):
    del xs_in_ref

    def start(i, carry):
        for kk in range(TOP_K):
            _row_copy_out(h_ref, xs_ref, sem, dest_ref, i, kk).start()
        return carry

    def wait(i, carry):
        for kk in range(TOP_K):
            _row_copy_out(h_ref, xs_ref, sem, dest_ref, i, kk).wait()
        return carry

    lax.fori_loop(0, rows, start, 0)
    lax.fori_loop(0, rows, wait, 0)


def _dispatch(dest3, h2, xs, tile0, rows):
    n, d = h2.shape
    return pl.pallas_call(
        functools.partial(_dispatch_kernel, rows=rows),
        grid=(n // rows,),
        in_specs=[pl.BlockSpec((1, TOP_K, TOK_TILE), lambda i: (tile0 + i, 0, 0), memory_space=pltpu.SMEM),
                  pl.BlockSpec((rows, d), lambda i: (i, 0)),
                  pl.BlockSpec(memory_space=pl.ANY)],
        out_specs=pl.BlockSpec(memory_space=pl.ANY),
        out_shape=jax.ShapeDtypeStruct(xs.shape, xs.dtype),
        scratch_shapes=[pltpu.SemaphoreType.DMA(())],
        input_output_aliases={2: 0},
        compiler_params=_cparams("arbitrary"),
        name="dispatch",
    )(dest3, h2, xs)


def _moe_kernel(be_ref, nact_ref, x_ref, wgu_ref, bgu_ref, wd_ref, bd_ref, o_ref, wgu_bf, wd_bf, *, dff):
    i = pl.program_id(0)

    @pl.when(i < nact_ref[0])
    def _():
        prev = be_ref[jnp.maximum(i - 1, 0)]

        @pl.when((i == 0) | (be_ref[i] != prev))
        def _():
            wgu_bf[...] = wgu_ref[...].astype(BF16)
            wd_bf[...] = wd_ref[...].astype(BF16)

        gu = jnp.dot(x_ref[...].astype(BF16), wgu_bf[...], preferred_element_type=F32) + bgu_ref[...]
        gate = jnp.minimum(gu[:, :dff], SWIGLU_LIMIT)
        up = jnp.clip(gu[:, dff:], -SWIGLU_LIMIT, SWIGLU_LIMIT)
        act = (up + 1.0) * gate * _sigmoid(gate * SWIGLU_ALPHA)
        o_ref[...] = jnp.dot(act.astype(BF16), wd_bf[...], preferred_element_type=F32) + bd_ref[...]

    @pl.when(i >= nact_ref[0])
    def _():
        o_ref[...] = jnp.zeros_like(o_ref)


def _moe(blk_e, nact, xs, w_gu, b_gu, w_down, b_down, n_blocks):
    ne, d, dff2 = w_gu.shape
    dff = dff2 // 2
    blk = MOE_BLOCK
    grid_spec = pltpu.PrefetchScalarGridSpec(
        num_scalar_prefetch=2,
        grid=(n_blocks,),
        in_specs=[pl.BlockSpec((blk, d), lambda i, be, na: (i, 0)),
                  pl.BlockSpec((None, d, dff2), lambda i, be, na: (be[i], 0, 0)),
                  pl.BlockSpec((None, 1, dff2), lambda i, be, na: (be[i], 0, 0)),
                  pl.BlockSpec((None, dff, d), lambda i, be, na: (be[i], 0, 0)),
                  pl.BlockSpec((None, 1, d), lambda i, be, na: (be[i], 0, 0))],
        out_specs=pl.BlockSpec((blk, d), lambda i, be, na: (i, 0)),
        scratch_shapes=[pltpu.VMEM((d, dff2), BF16), pltpu.VMEM((dff, d), BF16)],
    )
    return pl.pallas_call(
        functools.partial(_moe_kernel, dff=dff),
        grid_spec=grid_spec,
        out_shape=jax.ShapeDtypeStruct((n_blocks * blk, d), F32),
        compiler_params=_cparams("arbitrary"),
        name="moe",
    )(blk_e, nact, xs, w_gu, b_gu.reshape(ne, 1, dff2), w_down, b_down.reshape(ne, 1, d))


def _row_copy_in(ys_ref, buf_ref, sem, dest_ref, i, kk):
    return pltpu.make_async_copy(ys_ref.at[pl.ds(dest_ref[0, kk, i], 1), :], buf_ref.at[kk, pl.ds(i, 1), :], sem)


def _combine_kernel(dest_ref, ys_ref, w_ref, x1_ref, gt_ref, o_ref, buf_ref, sem, *, rows):
    def start(i, carry):
        for kk in range(TOP_K):
            _row_copy_in(ys_ref, buf_ref, sem, dest_ref, i, kk).start()
        return carry

    def wait(i, carry):
        for kk in range(TOP_K):
            _row_copy_in(ys_ref, buf_ref, sem, dest_ref, i, kk).wait()
        return carry

    lax.fori_loop(0, rows, start, 0)
    lax.fori_loop(0, rows, wait, 0)
    w = w_ref[...]
    y = w[:rows, 0:1] * buf_ref[0]
    for kk in range(1, TOP_K):
        y = y + w[:rows, kk:kk + 1] * buf_ref[kk]
    o_ref[...] = x1_ref[...] + gt_ref[...] * y


def _combine(dest3, ys, wt3, x1, mod3, tile0, rows):
    nb, t, d = x1.shape
    rmod = mod3.shape[1]
    nt = t // rows
    row = lambda b, i: (b, i, 0)
    gate = pl.BlockSpec((None, rmod, d), (lambda b, i: (b, 0, 5)) if rmod == 1 else (lambda b, i: (b, i, 5)))
    return pl.pallas_call(
        functools.partial(_combine_kernel, rows=rows),
        grid=(nb, nt),
        in_specs=[pl.BlockSpec((1, TOP_K, TOK_TILE), lambda b, i: (tile0 + b * nt + i, 0, 0), memory_space=pltpu.SMEM),
                  pl.BlockSpec(memory_space=pl.ANY),
                  pl.BlockSpec((None, TOK_TILE, TOP_K), lambda b, i: (tile0 + b * nt + i, 0, 0)),
                  pl.BlockSpec((None, rows, d), row), gate],
        out_specs=pl.BlockSpec((None, rows, d), row),
        out_shape=jax.ShapeDtypeStruct((nb, t, d), F32),
        scratch_shapes=[pltpu.VMEM((TOP_K, rows, d), F32), pltpu.SemaphoreType.DMA(())],
        compiler_params=_cparams("arbitrary", "arbitrary"),
        name="combine",
    )(dest3, ys, wt3, x1, mod3)


def _rope_tables(pos, n_heads):
    half = HEAD_DIM // 2
    inv_freq = 1.0 / (ROPE_THETA ** (jnp.arange(0, HEAD_DIM, 2, dtype=F32) / HEAD_DIM))
    ang = pos.astype(F32)[:, None] * inv_freq[None, :]
    cos, sin = jnp.cos(ang), jnp.sin(ang)
    del half
    return (jnp.tile(jnp.concatenate([cos, cos], axis=-1), (1, n_heads)),
            jnp.tile(jnp.concatenate([-sin, sin], axis=-1), (1, n_heads)))


def _block_diag(width, value):
    h = np.arange(width) // HEAD_DIM
    return jnp.asarray(np.where(h[:, None] == h[None, :], value, 0.0), F32)


def _state_to_pairs(s):
    nb, nh, n, _ = s.shape
    zt = jnp.swapaxes(s, 2, 3).reshape(nb, nh // 2, 2, n, n)
    z = jnp.zeros((nb, nh // 2, 2, n, 2, n), s.dtype)
    z = z.at[:, :, 0, :, 0, :].set(zt[:, :, 0]).at[:, :, 1, :, 1, :].set(zt[:, :, 1])
    return z.reshape(nb, nh // 2, 2 * n, 2 * n)


def _pairs_to_state(z, nh):
    nb = z.shape[0]
    n = HEAD_DIM
    z = z.reshape(nb, nh // 2, 2, n, 2, n)
    zt = jnp.stack([z[:, :, 0, :, 0, :], z[:, :, 1, :, 1, :]], axis=2).reshape(nb, nh, n, n)
    return jnp.swapaxes(zt, 2, 3)


def kernel(x_prompt, x_sample, cache_k, cache_v, state_wkv, state_shift, c_prompt, c_sample, w_ada, b_ada, norm1_g, norm2_g, w_in, q_norm_g, k_norm_g, rwkv_mu, rwkv_w0, rwkv_w2, rwkv_a0, rwkv_a2, rwkv_g2, rwkv_k_k, rwkv_k_a, rwkv_r_k, rwkv_ln_w, rwkv_ln_b, w_out, router_w, router_b, moe_w_gu, moe_b_gu, moe_w_down, moe_b_down):
    nbp, t, d = x_prompt.shape
    nbs, ts, _ = x_sample.shape
    depth = w_ada.shape[0]
    assert depth == 1 and ts == 1
    n_heads = cache_k.shape[3]
    aw = n_heads * HEAD_DIM
    rwid = rwkv_w0.shape[1]
    rw_heads = rwid // HEAD_DIM
    rwc = rwkv_mu.shape[1]
    past = cache_k.shape[2]
    ne = router_w.shape[2]
    keep = min(MAX_WINDOW, t)
    lyr = 0

    w_in_bf = w_in[lyr].astype(BF16)
    wt_bf = w_out[lyr][:aw].astype(BF16)
    wb_bf = w_out[lyr][aw:].astype(BF16)
    g1 = norm1_g[lyr].reshape(1, d)
    g2 = norm2_g[lyr].reshape(1, d)
    qg = jnp.tile(q_norm_g[lyr], n_heads).reshape(1, aw)
    kg = jnp.tile(k_norm_g[lyr], n_heads).reshape(1, aw)
    bd_mean_a = _block_diag(aw, 1.0 / HEAD_DIM)
    bd_mean_r = _block_diag(rwid, 1.0 / HEAD_DIM)
    bd_ones_r = _block_diag(rwid, 1.0)
    dl = rwkv_w2.shape[1]
    w2p = jnp.zeros((LANES, rwid), F32).at[:dl].set(rwkv_w2[lyr])
    a2p = jnp.zeros((LANES, rwid), F32).at[dl:dl + rwkv_a2.shape[1]].set(rwkv_a2[lyr])
    vec = lambda a: a[lyr].reshape(1, -1)
    rwt = router_w[lyr].T
    rb = router_b[lyr].reshape(ne, 1)
    head_of_lane = np.arange(aw) // HEAD_DIM
    e_mat = jnp.asarray(head_of_lane[:, None] == np.arange(LANES)[None, :], F32)
    et_mat = e_mat.T

    rows_c = nbp + nbs
    rows_pad = -(-rows_c // 8) * 8
    c_all = jnp.zeros((rows_pad, d), F32).at[:nbp].set(c_prompt).at[nbp:rows_c].set(c_sample)
    mod = _ada(c_all, w_ada[lyr], b_ada[lyr])
    mod_p = mod[:nbp].reshape(nbp, 1, 6 * d)
    mod_s = mod[nbp:rows_c].reshape(1, nbs, 6 * d)
    xs3 = x_sample.reshape(1, nbs, d)

    cos_p, sin_p = _rope_tables(jnp.arange(t), n_heads)
    cos_s, sin_s = _rope_tables(jnp.full((nbs,), PAST_LEN), n_heads)
    qp, kp, vp, rwp = _inproj(x_prompt, mod_p, g1, w_in_bf, cos_p, sin_p, qg, kg, bd_mean_a, ROW_TILE)
    qs, ks, vs, rws = _inproj(xs3, mod_s, g1, w_in_bf, cos_s, sin_s, qg, kg, bd_mean_a, nbs)

    attn_p = _attn_prompt(qp, kp, vp)
    attn_s = _attn_sample(qs[0], ks[0], vs[0], cache_k[lyr].reshape(nbs, past, aw),
                          cache_v[lyr].reshape(nbs, past, aw), e_mat, et_mat)

    rw_args = (vec(rwkv_mu), vec(rwkv_w0), w2p, vec(rwkv_a0), a2p, rwkv_g2[lyr], vec(rwkv_k_k), vec(rwkv_k_a), bd_ones_r)
    pre_p = _rwprep(rwp, jnp.zeros((nbp, 1, rwc), F32), *rw_args, tm=ROW_TILE, whole_prev=False)
    pre_s = _rwprep(rws, state_shift[lyr].reshape(1, nbs, rwc), *rw_args, tm=nbs, whole_prev=True)
    r_p, ld_p, k_p, v_p, al_p, be_p, g_p = pre_p
    y_p, zt_p = _rwchunk(r_p, ld_p, k_p, v_p, al_p, be_p, jnp.zeros((nbp, rwid // LANES, LANES, LANES), F32))
    padc = lambda a: jnp.zeros((nbs, RW_CHUNK, rwid), F32).at[:, 0].set(a[0])
    r_s, ld_s, k_s, v_s, al_s, be_s, g_s = pre_s
    y_s, zt_s = _rwchunk(padc(r_s), padc(ld_s), padc(k_s), padc(v_s), padc(al_s), padc(be_s),
                         _state_to_pairs(state_wkv[lyr]))
    y_s = y_s[:, 0].reshape(1, nbs, rwid)

    op_args = (g2, wt_bf, wb_bf, rwkv_r_k[lyr].reshape(1, rwid), vec(rwkv_ln_w), vec(rwkv_ln_b),
               bd_mean_r, bd_ones_r, rwt, rb)
    x1_p, h2_p, lg_p = _outproj(x_prompt, attn_p, y_p, r_p, k_p, v_p, g_p, mod_p, *op_args, tm=ROW_TILE)
    x1_s, h2_s, lg_s = _outproj(xs3, attn_s.reshape(1, nbs, aw), y_s, r_s, k_s, v_s, g_s, mod_s, *op_args, tm=nbs)

    n_p = nbp * t
    n_valid = n_p + nbs
    assert n_p % ROUTE_TILE == 0 and ROUTE_TILE % TOK_TILE == 0
    npad = -(-n_valid // ROUTE_TILE) * ROUTE_TILE
    logits_t = jnp.zeros((ne, npad), F32).at[:, :n_p].set(lg_p).at[:, n_p:n_valid].set(lg_s)
    idx, wts, rank, cnt = _route(logits_t, n_valid)
    counts = cnt[:, 0].astype(I32)
    blk = MOE_BLOCK
    n_blocks = -(-(n_valid * TOP_K + ne * (blk - 1)) // blk)
    padded = (counts + blk - 1) // blk * blk
    pad_end = jnp.cumsum(padded)
    start = pad_end - padded
    blk_e = jnp.minimum(jnp.searchsorted(pad_end, jnp.arange(n_blocks) * blk, side='right'), ne - 1).astype(I32)
    nact = (pad_end[-1] // blk).reshape(1).astype(I32)
    trash = n_blocks * blk
    dest = _dest(idx, rank, start.astype(I32), n_valid, trash)
    n_tiles = npad // TOK_TILE
    dest3 = dest.reshape(TOP_K, n_tiles, TOK_TILE).transpose(1, 0, 2)
    wt3 = wts.reshape(TOP_K, n_tiles, TOK_TILE).transpose(1, 2, 0)
    xs_buf = jnp.zeros((trash + 8, d), F32)
    xs_buf = _dispatch(dest3, h2_p.reshape(n_p, d), xs_buf, 0, TOK_TILE)
    xs_buf = _dispatch(dest3, h2_s.reshape(nbs, d), xs_buf, n_p // TOK_TILE, nbs)
    ys = _moe(blk_e, nact, xs_buf, moe_w_gu[lyr], moe_b_gu[lyr], moe_w_down[lyr], moe_b_down[lyr], n_blocks)
    y_prompt = _combine(dest3, ys, wt3, x1_p, mod_p, 0, TOK_TILE)
    y_sample = _combine(dest3, ys, wt3, x1_s, mod_s, n_p // TOK_TILE, nbs)

    hk = (nbp, keep, n_heads, HEAD_DIM)
    return (y_prompt, y_sample.reshape(nbs, ts, d),
            kp[:, t - keep:].reshape(hk)[None], vp[:, t - keep:].reshape(hk)[None],
            _pairs_to_state(zt_p, rw_heads)[None], rwp[:, t - 1][None],
            ks.reshape(nbs, ts, n_heads, HEAD_DIM)[None], vs.reshape(nbs, ts, n_heads, HEAD_DIM)[None],
            _pairs_to_state(zt_s, rw_heads)[None], rws.reshape(nbs, rwc)[None])
```

```python
import functools

import numpy as np
import jax
import jax.numpy as jnp
from jax import lax
from jax.experimental import pallas as pl
from jax.experimental.pallas import tpu as pltpu

F32 = jnp.float32
BF16 = jnp.bfloat16
I32 = jnp.int32

HEAD_DIM = 64
LANES = 128
DILATED_PATTERNS = ((128, 1), (512, 4), (2048, 16))
WINDOW_STEPS = 128
MAX_WINDOW = 2048
PAST_LEN = 16384
ROPE_THETA = 10000.0
NORM_EPS = 1e-6
GN_EPS = 64e-5
TOP_K = 4
SWIGLU_ALPHA = 1.702
SWIGLU_LIMIT = 7.0
RW_CHUNK = 128
MOE_BLOCK = 256
ATTN_UNITS = 4
ROW_TILE = 512
TOK_TILE = 256
ROUTE_TILE = 512
VMEM_LIMIT = 56 * 1024 * 1024
NEG_BIG = -1e30


def _cparams(*sem):
    return pltpu.CompilerParams(dimension_semantics=sem, vmem_limit_bytes=VMEM_LIMIT)


def _dot(a, b):
    return jnp.dot(a.astype(BF16), b.astype(BF16), preferred_element_type=F32)


def _dot_nt(a, b):
    return lax.dot_general(a.astype(BF16), b.astype(BF16), (((1,), (1,)), ((), ())),
                           preferred_element_type=F32)


def _split2(a):
    hi = a.astype(BF16)
    lo = (a - hi.astype(F32)).astype(BF16)
    return hi, lo


def _split3(a):
    hi = a.astype(BF16)
    r1 = a - hi.astype(F32)
    mid = r1.astype(BF16)
    lo = (r1 - mid.astype(F32)).astype(BF16)
    return hi, mid, lo


def _dot_x(a, e):
    e = e.astype(BF16)
    hi, mid, lo = _split3(a)
    return (jnp.dot(hi, e, preferred_element_type=F32) + jnp.dot(mid, e, preferred_element_type=F32)
            + jnp.dot(lo, e, preferred_element_type=F32))


def _xdot(e, a):
    e = e.astype(BF16)
    hi, mid, lo = _split3(a)
    return (jnp.dot(e, hi, preferred_element_type=F32) + jnp.dot(e, mid, preferred_element_type=F32)
            + jnp.dot(e, lo, preferred_element_type=F32))


def _dot3(a, b):
    ah, al = _split2(a)
    bh, bl = _split2(b)
    return (jnp.dot(ah, bh, preferred_element_type=F32) + jnp.dot(ah, bl, preferred_element_type=F32)
            + jnp.dot(al, bh, preferred_element_type=F32))


def _dot3_nt(a, b):
    ah, al = _split2(a)
    bh, bl = _split2(b)
    dn = (((1,), (1,)), ((), ()))
    return (lax.dot_general(ah, bh, dn, preferred_element_type=F32)
            + lax.dot_general(ah, bl, dn, preferred_element_type=F32)
            + lax.dot_general(al, bh, dn, preferred_element_type=F32))


def _mm(a, w):
    if w.dtype == BF16:
        return jnp.dot(a.astype(BF16), w, preferred_element_type=F32)
    return _dot3(a, w)


def _sigmoid(x):
    return 1.0 / (1.0 + jnp.exp(-x))


def _ada_kernel(c_ref, w_ref, b_ref, o_ref):
    c = c_ref[...]
    o_ref[...] = _dot3(c * _sigmoid(c), w_ref[...]) + b_ref[...]


def _ada(c_all, w_ada, b_ada):
    rows, d = c_all.shape
    n = w_ada.shape[1]
    tn = n // 4
    return pl.pallas_call(
        _ada_kernel,
        grid=(n // tn,),
        in_specs=[pl.BlockSpec((rows, d), lambda j: (0, 0)),
                  pl.BlockSpec((d, tn), lambda j: (0, j)),
                  pl.BlockSpec((1, tn), lambda j: (0, j))],
        out_specs=pl.BlockSpec((rows, tn), lambda j: (0, j)),
        out_shape=jax.ShapeDtypeStruct((rows, n), F32),
        compiler_params=_cparams("arbitrary"),
        name="ada",
    )(c_all, w_ada, b_ada.reshape(1, n))


def _inproj_kernel(x_ref, sh_ref, sc_ref, g_ref, w_ref, cos_ref, sin_ref, qg_ref, kg_ref, bd_ref,
                   q_ref, k_ref, v_ref, rw_ref, *maybe_kv_t, aw, first_kept):
    x = x_ref[...]
    ms = jnp.mean(x * x, axis=-1, keepdims=True)
    h = x * lax.rsqrt(ms + NORM_EPS) * g_ref[...] * (1.0 + sc_ref[...]) + sh_ref[...]
    proj = _mm(h, w_ref[...])
    cos = cos_ref[...]
    sin = sin_ref[...]
    lane = lax.broadcasted_iota(I32, (1, aw), 1)
    first_half = (lane % HEAD_DIM) < (HEAD_DIM // 2)
    bd = bd_ref[...]

    def norm_rope(t, g):
        tn = t * lax.rsqrt((_dot if w_ref.dtype == BF16 else _dot_x)(t * t, bd) + NORM_EPS) * g
        rot = jnp.where(first_half, pltpu.roll(tn, aw - HEAD_DIM // 2, 1), pltpu.roll(tn, HEAD_DIM // 2, 1))
        return tn * cos + rot * sin

    scale = 1.0 / np.sqrt(HEAD_DIM).astype(np.float32)
    q_ref[...] = norm_rope(proj[:, :aw], qg_ref[...]) * scale
    k = norm_rope(proj[:, aw:2 * aw], kg_ref[...])
    v = proj[:, 2 * aw:3 * aw]
    k_ref[...] = k
    v_ref[...] = v
    rw_ref[...] = proj[:, 3 * aw:]
    if maybe_kv_t:
        kt_ref, vt_ref = maybe_kv_t
        kept = pl.program_id(1) >= first_kept

        @pl.when(kept)
        def _():
            kt_ref[...] = k.T
            vt_ref[...] = v.T

        @pl.when(jnp.logical_not(kept))
        def _():
            kt_ref[...] = jnp.zeros_like(kt_ref)
            vt_ref[...] = jnp.zeros_like(vt_ref)


def _inproj(x3, mod3, g1, w_in_bf, cos_t, sin_t, qg, kg, bd_mean, tm, keep=0):
    nb, t, d = x3.shape
    r = mod3.shape[1]
    ncol = w_in_bf.shape[1]
    aw = cos_t.shape[1]
    rwc = ncol - 3 * aw
    grid = (nb, t // tm)
    row = lambda b, i: (b, i, 0)
    const = lambda b, i: (0, 0)
    mod_spec = lambda s: pl.BlockSpec((None, r, d), (lambda b, i: (b, 0, s)) if r == 1 else (lambda b, i: (b, i, s)))
    out_specs = [pl.BlockSpec((None, tm, aw), row)] * 3 + [pl.BlockSpec((None, tm, rwc), row)]
    out_shape = [jax.ShapeDtypeStruct((nb, t, aw), F32)] * 3 + [jax.ShapeDtypeStruct((nb, t, rwc), F32)]
    first_kept = (t - keep) // tm
    if keep:
        assert keep % tm == 0 and (t - keep) % tm == 0
        kept_spec = pl.BlockSpec((None, aw, tm), lambda b, i: (b, 0, jnp.maximum(i - first_kept, 0)))
        out_specs += [kept_spec, kept_spec]
        out_shape += [jax.ShapeDtypeStruct((nb, aw, keep), F32)] * 2
    outs = pl.pallas_call(
        functools.partial(_inproj_kernel, aw=aw, first_kept=first_kept),
        grid=grid,
        in_specs=[pl.BlockSpec((None, tm, d), row), mod_spec(0), mod_spec(1),
                  pl.BlockSpec((1, d), const), pl.BlockSpec((d, ncol), const),
                  pl.BlockSpec((tm, aw), lambda b, i: (i, 0)), pl.BlockSpec((tm, aw), lambda b, i: (i, 0)),
                  pl.BlockSpec((1, aw), const), pl.BlockSpec((1, aw), const), pl.BlockSpec((aw, aw), const)],
        out_specs=out_specs,
        out_shape=out_shape,
        compiler_params=_cparams("parallel", "arbitrary"),
        name="inproj",
    )(x3, mod3, mod3, g1, w_in_bf, cos_t, sin_t, qg, kg, bd_mean)
    return outs


def _attn_prompt_kernel(q_ref, k_ref, v_ref, o_ref, kp_ref, vp_ref, m_ref, l_ref, acc_ref, *, t, pad):
    nq = WINDOW_STEPS
    nk = 2 * WINDOW_STEPS
    kp_ref[pl.ds(0, pad), :] = jnp.zeros((pad, LANES), F32)
    vp_ref[pl.ds(0, pad), :] = jnp.zeros((pad, LANES), F32)
    kp_ref[pl.ds(pad, t), :] = k_ref[...]
    vp_ref[pl.ds(pad, t), :] = v_ref[...]
    head0 = lax.broadcasted_iota(I32, (1, LANES), 1) < HEAD_DIM
    qi = lax.broadcasted_iota(I32, (nq, nk), 0)
    kj = lax.broadcasted_iota(I32, (nq, nk), 1)
    steps_back = qi + nq - kj
    band = (steps_back >= 0) & (steps_back <= WINDOW_STEPS)
    has_past = kj >= nq

    hsels = (head0, jnp.logical_not(head0))
    nt = (((1,), (1,)), ((), ()))

    for p, (_, d) in enumerate(DILATED_PATTERNS):
        def units(g, carry, p=p, d=d):
            rows_q, kb, vb, valid, qh = [], [], [], [], []
            for j in range(ATTN_UNITS):
                u = g * ATTN_UNITS + j
                res = u % d
                blk = u // d
                q_start = res + d * nq * blk
                k_start = pad + q_start - d * nq
                if d == 1:
                    rows_q.append(pl.ds(q_start, nq))
                    rows_k = pl.ds(k_start, nk)
                else:
                    rows_q.append(pl.ds(q_start, nq, stride=d))
                    rows_k = pl.ds(k_start, nk, stride=d)
                q = q_ref[rows_q[j], :]
                kb.append(kp_ref[rows_k, :].astype(BF16))
                vb.append(vp_ref[rows_k, :].astype(BF16))
                valid.append(band & (has_past | (blk > 0)))
                qh.append([jnp.where(hsel, q, 0.0).astype(BF16) for hsel in hsels])
            chains = [(j, h) for j in range(ATTN_UNITS) for h in range(2)]
            s = [lax.dot_general(qh[j][h], kb[j], nt, preferred_element_type=F32) for j, h in chains]
            s = [jnp.where(valid[j], sc, NEG_BIG) for (j, h), sc in zip(chains, s)]
            mx = [jnp.max(sc, axis=-1, keepdims=True) for sc in s]
            e = [jnp.exp(sc - m) for sc, m in zip(s, mx)]
            den = [jnp.sum(ec, axis=-1, keepdims=True) for ec in e]
            o = [jnp.dot(ec.astype(BF16), vb[j], preferred_element_type=F32) for (j, h), ec in zip(chains, e)]
            for j in range(ATTN_UNITS):
                m_ref[p, rows_q[j], :] = jnp.where(head0, mx[2 * j], mx[2 * j + 1])
                l_ref[p, rows_q[j], :] = jnp.where(head0, den[2 * j], den[2 * j + 1])
                acc_ref[p, rows_q[j], :] = jnp.where(head0, o[2 * j], o[2 * j + 1])
            return carry

        lax.fori_loop(0, t // nq // ATTN_UNITS, units, 0)

    rows = 256

    def merge(i, carry):
        sl = pl.ds(pl.multiple_of(i * rows, rows), rows)
        m0, m1, m2 = m_ref[0, sl, :], m_ref[1, sl, :], m_ref[2, sl, :]
        mm = jnp.maximum(jnp.maximum(m0, m1), m2)
        w0, w1, w2 = jnp.exp(m0 - mm), jnp.exp(m1 - mm), jnp.exp(m2 - mm)
        num = w0 * acc_ref[0, sl, :] + w1 * acc_ref[1, sl, :] + w2 * acc_ref[2, sl, :]
        den = w0 * l_ref[0, sl, :] + w1 * l_ref[1, sl, :] + w2 * l_ref[2, sl, :]
        o_ref[sl, :] = num / den
        return carry

    lax.fori_loop(0, t // rows, merge, 0)


def _attn_prompt(q, k, v):
    nb, t, aw = q.shape
    pad = MAX_WINDOW
    assert t % MAX_WINDOW == 0
    spec = pl.BlockSpec((None, t, LANES), lambda b, hp: (b, 0, hp))
    return pl.pallas_call(
        functools.partial(_attn_prompt_kernel, t=t, pad=pad),
        grid=(nb, aw // LANES),
        in_specs=[spec, spec, spec],
        out_specs=spec,
        out_shape=jax.ShapeDtypeStruct((nb, t, aw), F32),
        scratch_shapes=[pltpu.VMEM((pad + t, LANES), F32), pltpu.VMEM((pad + t, LANES), F32),
                        pltpu.VMEM((3, t, LANES), F32), pltpu.VMEM((3, t, LANES), F32),
                        pltpu.VMEM((3, t, LANES), F32)],
        compiler_params=_cparams("parallel", "parallel"),
        name="attn_prompt",
    )(q, k, v)


def _attn_sample_kernel(q_ref, kn_ref, vn_ref, kt_ref, vt_ref, o_ref, *, nh, w):
    aw = nh * HEAD_DIM
    dist = w - lax.broadcasted_iota(I32, (1, w), 1)
    mult = jnp.zeros((1, w), F32)
    for win, d in DILATED_PATTERNS:
        mult = mult + jnp.where((dist % d == 0) & (dist <= win), 1.0, 0.0)
    n_pat = float(len(DILATED_PATTERNS))
    q_col = _col(q_ref[...], aw)
    kn_col = _col(kn_ref[...], aw)
    vn_col = _col(vn_ref[...], aw)
    outs = []
    for h in range(nh):
        hs = pl.ds(h * HEAD_DIM, HEAD_DIM)
        qh = q_col[h * HEAD_DIM:(h + 1) * HEAD_DIM]
        s = jnp.sum(kt_ref[hs, :] * qh, axis=0, keepdims=True)
        s_self = jnp.sum(qh * kn_col[h * HEAD_DIM:(h + 1) * HEAD_DIM], axis=0, keepdims=True)
        s = jnp.where(mult > 0.0, s, NEG_BIG)
        mx = jnp.maximum(jnp.max(s, axis=1, keepdims=True), s_self)
        pr = mult * jnp.exp(s - mx)
        p_self = n_pat * jnp.exp(s_self - mx)
        den = jnp.sum(pr, axis=1, keepdims=True) + p_self
        num = jnp.sum(vt_ref[hs, :] * pr, axis=1, keepdims=True) + p_self * vn_col[h * HEAD_DIM:(h + 1) * HEAD_DIM]
        outs.append(num / den)
    o_ref[...] = _row(jnp.concatenate(outs, axis=0), aw)


def _attn_sample(q, kn, vn, cache_kt, cache_vt):
    nb, _, aw = q.shape
    w = cache_kt.shape[2]
    assert w == MAX_WINDOW
    row = pl.BlockSpec((None, 1, aw), lambda i: (i, 0, 0))
    mat = pl.BlockSpec((None, aw, w), lambda i: (i, 0, 0))
    return pl.pallas_call(
        functools.partial(_attn_sample_kernel, nh=aw // HEAD_DIM, w=w),
        grid=(nb,),
        in_specs=[row, row, row, mat, mat],
        out_specs=row,
        out_shape=jax.ShapeDtypeStruct((nb, 1, aw), F32),
        compiler_params=_cparams("parallel"),
        name="attn_sample",
    )(q, kn, vn, cache_kt, cache_vt)


def _rwprep_kernel(p_ref, prev_ref, first_ref, mu_ref, w0_ref, w2_ref, a0_ref, a2_ref, g2_ref, kk_ref, ka_ref,
                   bd_ref, r_ref, ld_ref, k_ref, v_ref, al_ref, be_ref, g_ref, *, rwid, whole_prev):
    p = p_ref[...]
    if whole_prev:
        prev = first_ref[...]
    else:
        tm = p.shape[0]
        before = jnp.where(pl.program_id(1) == 0, first_ref[...], prev_ref[7:8, :])
        rowi = lax.broadcasted_iota(I32, (tm, 1), 0)
        prev = jnp.where(rowi == 0, before, pltpu.roll(p, 1, 0))
    xs = p + mu_ref[...] * (prev - p)
    r = xs[:, :rwid]
    k = xs[:, rwid:2 * rwid]
    v = xs[:, 2 * rwid:3 * rwid]
    xwa = xs[:, 3 * rwid:3 * rwid + LANES]
    xg = xs[:, 3 * rwid + LANES:]
    z = w0_ref[...] + _dot3(jnp.tanh(xwa), w2_ref[...])
    softplus_neg = jnp.maximum(-z, 0.0) + jnp.log(1.0 + jnp.exp(-jnp.abs(z)))
    w = -softplus_neg - 0.5
    a = _sigmoid(a0_ref[...] + _dot3(xwa, a2_ref[...]))
    g = _dot3(_sigmoid(xg), g2_ref[...])
    kk = k * kk_ref[...]
    norm = jnp.sqrt(_dot_x(kk * kk, bd_ref[...]))
    kk = kk / jnp.maximum(norm, 1e-12)
    r_ref[...] = r
    ld_ref[...] = -jnp.exp(w)
    k_ref[...] = k * (1.0 + (a - 1.0) * ka_ref[...])
    v_ref[...] = v
    al_ref[...] = -kk
    be_ref[...] = kk * a
    g_ref[...] = g


def _rwprep(rw, first, mu, w0, w2p, a0, a2p, g2, k_k, k_a, bd_ones, tm, whole_prev):
    nb, t, rwc = rw.shape
    rwid = w0.shape[1]
    grid = (nb, t // tm)
    row = lambda b, i: (b, i, 0)
    const = lambda b, i: (0, 0)
    if whole_prev:
        prev_spec = pl.BlockSpec((None, tm, rwc), row)
        first_spec = pl.BlockSpec((None, tm, rwc), row)
    else:
        prev_spec = pl.BlockSpec((None, 8, rwc), lambda b, i: (b, jnp.maximum(i * (tm // 8) - 1, 0), 0))
        first_spec = pl.BlockSpec((None, 1, rwc), lambda b, i: (b, 0, 0))
    vec = pl.BlockSpec((1, rwid), const)
    out = pl.BlockSpec((None, tm, rwid), row)
    return pl.pallas_call(
        functools.partial(_rwprep_kernel, rwid=rwid, whole_prev=whole_prev),
        grid=grid,
        in_specs=[pl.BlockSpec((None, tm, rwc), row), prev_spec, first_spec,
                  pl.BlockSpec((1, rwc), const), vec, pl.BlockSpec((LANES, rwid), const),
                  vec, pl.BlockSpec((LANES, rwid), const), pl.BlockSpec((LANES, rwid), const), vec, vec,
                  pl.BlockSpec((rwid, rwid), const)],
        out_specs=[out] * 7,
        out_shape=[jax.ShapeDtypeStruct((nb, t, rwid), F32)] * 7,
        compiler_params=_cparams("parallel", "parallel"),
        name="rwprep",
    )(rw, rw, first, mu, w0, w2p, a0, a2p, g2, k_k, k_a, bd_ones)


def _rwchunk_kernel(r_ref, ld_ref, k_ref, v_ref, al_ref, be_ref, y_ref, st_ref, z_ref, *, nb):
    c = RW_CHUNK
    ci = pl.program_id(1)

    @pl.when(ci == 0)
    def _():
        z_ref[...] = jnp.zeros_like(z_ref)

    ti = lax.broadcasted_iota(I32, (c, c), 0)
    si = lax.broadcasted_iota(I32, (c, c), 1)
    low_incl = si <= ti
    low_strict = si < ti
    diag = si == ti
    tri = jnp.where(low_incl, 1.0, 0.0).astype(BF16)
    eye = jnp.where(diag, 1.0, 0.0)
    head0 = lax.broadcasted_iota(I32, (1, LANES), 1) < HEAD_DIM
    hsels = (head0, jnp.logical_not(head0))
    same_head = (ti < HEAD_DIM) == (si < HEAD_DIM)
    nt = (((1,), (1,)), ((), ()))
    batches = range(nb)
    chains = [(b, h) for b in batches for h in range(2)]

    cum = [_xdot(tri, ld_ref[b]) for b in batches]
    tot = [cm[c - 1:c, :] for cm in cum]
    e_neg = [jnp.exp(-cm) for cm in cum]
    at = [al_ref[b] * jnp.exp(cum[b] - ld_ref[b]) for b in batches]
    rt = [r_ref[b] * jnp.exp(cum[b]) for b in batches]
    rhs_t = [jnp.concatenate([be_ref[b] * e_neg[b], k_ref[b] * e_neg[b]], axis=0).astype(BF16) for b in batches]
    vb = [v_ref[b].astype(BF16) for b in batches]
    at_h = [jnp.where(hsels[h], at[b], 0.0) for b, h in chains]
    rt_h = [jnp.where(hsels[h], rt[b], 0.0) for b, h in chains]
    a4 = [lax.dot_general(jnp.concatenate([a, r], axis=0).astype(BF16), rhs_t[b], nt, preferred_element_type=F32)
          for (b, h), a, r in zip(chains, at_h, rt_h)]
    a_ab = [jnp.where(low_strict, m[:c, :c], 0.0) for m in a4]
    a_ak = [jnp.where(low_strict, m[:c, c:], 0.0).astype(BF16) for m in a4]
    a_r = [jnp.concatenate([jnp.where(low_incl, m[c:, :c], 0.0), jnp.where(low_incl, m[c:, c:], 0.0)],
                           axis=1).astype(BF16) for m in a4]
    pw = [m.astype(BF16) for m in a_ab]
    inv = [eye + m for m in a_ab]
    for _ in range(int(np.log2(c)) - 1):
        pw = [jnp.dot(m, m, preferred_element_type=F32).astype(BF16) for m in pw]
        inv = [i + jnp.dot(i.astype(BF16), m, preferred_element_type=F32) for i, m in zip(inv, pw)]
    akv = [jnp.dot(m, vb[b], preferred_element_type=F32) for (b, h), m in zip(chains, a_ak)]
    x = [_dot(i, jnp.concatenate([kv, a], axis=1)) for i, kv, a in zip(inv, akv, at_h)]
    u0_h = [m[:, :LANES] for m in x]
    at2_h = [m[:, LANES:] for m in x]
    y0_h = [jnp.dot(ar, jnp.concatenate([u0, v_ref[b]], axis=0).astype(BF16), preferred_element_type=F32)
            for (b, h), ar, u0 in zip(chains, a_r, u0_h)]
    rt2_h = [r + jnp.dot(ar[:, :c], a2.astype(BF16), preferred_element_type=F32)
             for r, ar, a2 in zip(rt_h, a_r, at2_h)]
    z = [z_ref[b] for b in batches]
    uy = [_dot(jnp.concatenate([at2_h[2 * b] + at2_h[2 * b + 1], rt2_h[2 * b] + rt2_h[2 * b + 1]], axis=0), z[b])
          for b in batches]
    u = [uy[b][:c] + jnp.where(head0, u0_h[2 * b], u0_h[2 * b + 1]) for b in batches]
    for b in batches:
        y_ref[b] = uy[b][c:] + jnp.where(head0, y0_h[2 * b], y0_h[2 * b + 1])
    e_end = [jnp.exp(tot[b] - cum[b]) for b in batches]
    lhs_t = [jnp.concatenate([be_ref[b] * e_end[b], k_ref[b] * e_end[b]], axis=0) for b in batches]
    zadd = [_dot(lhs_t[b].T, jnp.concatenate([u[b], v_ref[b]], axis=0)) for b in batches]
    for b in batches:
        dcol = jnp.sum(jnp.where(diag, jnp.broadcast_to(jnp.exp(tot[b]), (c, c)), 0.0), axis=1, keepdims=True)
        z_ref[b] = dcol * z[b] + jnp.where(same_head, zadd[b], 0.0)

    @pl.when(ci == pl.num_programs(1) - 1)
    def _():
        for b in batches:
            s = z_ref[b].T
            st_ref[b, 0] = s[:HEAD_DIM, :HEAD_DIM]
            st_ref[b, 1] = s[HEAD_DIM:, HEAD_DIM:]


def _rwchunk(r, ld, k, v, al, be):
    nb, t, rwid = r.shape
    c = RW_CHUNK
    assert t % c == 0 and c == LANES
    npair = rwid // LANES
    seq = pl.BlockSpec((nb, c, LANES), lambda hp, ci: (0, ci, hp))
    return pl.pallas_call(
        functools.partial(_rwchunk_kernel, nb=nb),
        grid=(npair, t // c),
        in_specs=[seq] * 6,
        out_specs=[seq, pl.BlockSpec((nb, 2, HEAD_DIM, HEAD_DIM), lambda hp, ci: (0, hp, 0, 0))],
        out_shape=[jax.ShapeDtypeStruct((nb, t, rwid), F32),
                   jax.ShapeDtypeStruct((nb, 2 * npair, HEAD_DIM, HEAD_DIM), F32)],
        scratch_shapes=[pltpu.VMEM((nb, LANES, LANES), F32)],
        compiler_params=_cparams("parallel", "arbitrary"),
        name="rwchunk",
    )(r, ld, k, v, al, be)


def _col(row, width):
    return jnp.broadcast_to(row, (LANES, width)).T[:, 0:1]


def _row(col, width):
    return jnp.broadcast_to(col, (width, LANES)).T[0:1, :]


def _rwstep_kernel(r_ref, ld_ref, k_ref, v_ref, al_ref, be_ref, s_ref, y_ref, so_ref, *, bb, nh):
    rwid = nh * HEAD_DIM

    def one(b, carry):
        r, dcy, k, al, be = (ref[b] for ref in (r_ref, ld_ref, k_ref, al_ref, be_ref))
        dcy = jnp.exp(dcy)
        v_col = _col(v_ref[b], rwid)
        ys = []
        for h in range(nh):
            hs = slice(h * HEAD_DIM, (h + 1) * HEAD_DIM)
            s = s_ref[b, h]
            sa = jnp.sum(s * al[:, hs], axis=1, keepdims=True)
            s = s * dcy[:, hs] + sa * be[:, hs] + v_col[hs, :] * k[:, hs]
            so_ref[b, h] = s
            ys.append(jnp.sum(s * r[:, hs], axis=1, keepdims=True))
        y_ref[b] = _row(jnp.concatenate(ys, axis=0), rwid)
        return carry

    lax.fori_loop(0, bb, one, 0)


def _rwstep(r, ld, k, v, al, be, state):
    nb, _, rwid = r.shape
    nh = state.shape[1]
    bb = 8
    row = pl.BlockSpec((bb, 1, rwid), lambda i: (i, 0, 0))
    st = pl.BlockSpec((bb, nh, HEAD_DIM, HEAD_DIM), lambda i: (i, 0, 0, 0))
    return pl.pallas_call(
        functools.partial(_rwstep_kernel, bb=bb, nh=nh),
        grid=(nb // bb,),
        in_specs=[row] * 6 + [st],
        out_specs=[row, st],
        out_shape=[jax.ShapeDtypeStruct((nb, 1, rwid), F32), jax.ShapeDtypeStruct(state.shape, F32)],
        compiler_params=_cparams("parallel"),
        name="rwstep",
    )(r, ld, k, v, al, be, state)


def _outproj_kernel(x_ref, at_ref, y_ref, r_ref, k_ref, v_ref, g_ref, gt_ref, sh_ref, sc_ref, g2_ref,
                    wt_ref, wb_ref, rk_ref, lw_ref, lb_ref, bdm_ref, bd1_ref, rwt_ref, rb_ref,
                    x1_ref, h2_ref, lg_ref):
    y = y_ref[...]
    bdm = bdm_ref[...]
    mean = _dot_x(y, bdm)
    yc = y - mean
    var = _dot_x(yc * yc, bdm)
    yn = yc * lax.rsqrt(var + GN_EPS) * lw_ref[...] + lb_ref[...]
    v = v_ref[...]
    bonus = _dot_x(r_ref[...] * k_ref[...] * rk_ref[...], bd1_ref[...]) * v
    rw = (yn + bonus) * g_ref[...]
    mix = _mm(at_ref[...], wt_ref[...]) + _mm(rw, wb_ref[...])
    x1 = x_ref[...] + gt_ref[...] * mix
    x1_ref[...] = x1
    ms = jnp.mean(x1 * x1, axis=-1, keepdims=True)
    h2 = x1 * lax.rsqrt(ms + NORM_EPS) * g2_ref[...] * (1.0 + sc_ref[...]) + sh_ref[...]
    h2_ref[...] = h2
    lg_ref[...] = _dot3_nt(rwt_ref[...], h2) + rb_ref[...]


def _outproj(x3, attn, y, r, k, v, g, mod3, g2, wt_bf, wb_bf, r_k, ln_w, ln_b, bd_mean, bd_ones, rwt, rb, tm):
    nb, t, d = x3.shape
    rmod = mod3.shape[1]
    aw = attn.shape[2]
    ne = rwt.shape[0]
    nt = t // tm
    grid = (nb, nt)
    row = lambda b, i: (b, i, 0)
    const = lambda b, i: (0, 0)
    mod_spec = lambda s: pl.BlockSpec((None, rmod, d), (lambda b, i: (b, 0, s)) if rmod == 1 else (lambda b, i: (b, i, s)))
    half = pl.BlockSpec((None, tm, aw), row)
    vec = pl.BlockSpec((1, aw), const)
    return pl.pallas_call(
        _outproj_kernel,
        grid=grid,
        in_specs=[pl.BlockSpec((None, tm, d), row)] + [half] * 6 + [mod_spec(2), mod_spec(3), mod_spec(4),
                  pl.BlockSpec((1, d), const), pl.BlockSpec((aw, d), const), pl.BlockSpec((aw, d), const),
                  vec, vec, vec, pl.BlockSpec((aw, aw), const), pl.BlockSpec((aw, aw), const),
                  pl.BlockSpec((ne, d), const), pl.BlockSpec((ne, 1), const)],
        out_specs=[pl.BlockSpec((None, tm, d), row), pl.BlockSpec((None, tm, d), row),
                   pl.BlockSpec((ne, tm), lambda b, i: (0, b * nt + i))],
        out_shape=[jax.ShapeDtypeStruct((nb, t, d), F32), jax.ShapeDtypeStruct((nb, t, d), F32),
                   jax.ShapeDtypeStruct((ne, nb * t), F32)],
        compiler_params=_cparams("parallel", "parallel"),
        name="outproj",
    )(x3, attn, y, r, k, v, g, mod3, mod3, mod3, g2, wt_bf, wb_bf, r_k, ln_w, ln_b, bd_mean, bd_ones, rwt, rb)


def _route_kernel(lg_ref, idx_ref, wt_ref, rank_ref, cnt_ref, carry_ref, *, n_valid):
    i = pl.program_id(0)
    ne, tn = lg_ref.shape

    @pl.when(i == 0)
    def _():
        carry_ref[...] = jnp.zeros_like(carry_ref)

    lg = lg_ref[...]
    eidx = lax.broadcasted_iota(I32, (ne, tn), 0).astype(F32)
    tok = i * tn + lax.broadcasted_iota(I32, (1, tn), 1)
    live = tok < n_valid
    vals, hots = [], []
    for _ in range(TOP_K):
        mx = jnp.max(lg, axis=0, keepdims=True)
        pick = jnp.min(jnp.where(lg == mx, eidx, float(ne)), axis=0, keepdims=True)
        hot = eidx == pick
        vals.append(mx)
        hots.append(hot)
        idx_ref[pl.ds(len(vals) - 1, 1), :] = pick.astype(I32)
        lg = jnp.where(hot, -jnp.inf, lg)
    ex = [jnp.exp(vv - vals[0]) for vv in vals]
    den = ex[0] + ex[1] + ex[2] + ex[3]
    for kk in range(TOP_K):
        wt_ref[pl.ds(kk, 1), :] = ex[kk] / den
    hot_all = jnp.zeros((ne, tn), F32)
    for hot in hots:
        hot_all = hot_all + jnp.where(hot & live, 1.0, 0.0)
    ri = lax.broadcasted_iota(I32, (tn, tn), 0)
    cj = lax.broadcasted_iota(I32, (tn, tn), 1)
    upper = jnp.where(ri <= cj, 1.0, 0.0).astype(BF16)
    incl = jnp.dot(hot_all.astype(BF16), upper, preferred_element_type=F32)
    before = carry_ref[:, 0:1] + incl - hot_all
    for kk, hot in enumerate(hots):
        rank_ref[pl.ds(kk, 1), :] = jnp.sum(jnp.where(hot, before, 0.0), axis=0, keepdims=True).astype(I32)
    carry_ref[...] = carry_ref[...] + jnp.sum(hot_all, axis=1, keepdims=True)
    cnt_ref[...] = carry_ref[...]


def _route(logits_t, n_valid):
    ne, npad = logits_t.shape
    tn = ROUTE_TILE
    tile = lambda rows: pl.BlockSpec((rows, tn), lambda i: (0, i))
    return pl.pallas_call(
        functools.partial(_route_kernel, n_valid=n_valid),
        grid=(npad // tn,),
        in_specs=[tile(ne)],
        out_specs=[tile(TOP_K), tile(TOP_K), tile(TOP_K), pl.BlockSpec((ne, LANES), lambda i: (0, 0))],
        out_shape=[jax.ShapeDtypeStruct((TOP_K, npad), I32), jax.ShapeDtypeStruct((TOP_K, npad), F32),
                   jax.ShapeDtypeStruct((TOP_K, npad), I32), jax.ShapeDtypeStruct((ne, LANES), F32)],
        scratch_shapes=[pltpu.VMEM((ne, LANES), F32)],
        compiler_params=_cparams("arbitrary"),
        name="route",
    )(logits_t)


def _dest_kernel(idx_ref, rank_ref, start_ref, dest_ref, *, n_valid, trash):
    i = pl.program_id(0)
    ne = start_ref.shape[0]
    tn = idx_ref.shape[1]
    eidx = lax.broadcasted_iota(I32, (ne, tn), 0)
    tok = i * tn + lax.broadcasted_iota(I32, (1, tn), 1)
    start = start_ref[:, 0:1]
    for kk in range(TOP_K):
        base = jnp.sum(jnp.where(eidx == idx_ref[pl.ds(kk, 1), :], start, 0.0), axis=0, keepdims=True)
        dest_ref[pl.ds(kk, 1), :] = jnp.where(tok < n_valid, base.astype(I32) + rank_ref[pl.ds(kk, 1), :], trash)


def _dest(idx, rank, start, n_valid, trash):
    _, npad = idx.shape
    ne = start.shape[0]
    tn = ROUTE_TILE
    tile = pl.BlockSpec((TOP_K, tn), lambda i: (0, i))
    return pl.pallas_call(
        functools.partial(_dest_kernel, n_valid=n_valid, trash=trash),
        grid=(npad // tn,),
        in_specs=[tile, tile, pl.BlockSpec((ne, LANES), lambda i: (0, 0))],
        out_specs=tile,
        out_shape=jax.ShapeDtypeStruct((TOP_K, npad), I32),
        compiler_params=_cparams("parallel"),
        name="dest",
    )(idx, rank, jnp.broadcast_to(start.astype(F32)[:, None], (ne, LANES)))


def _row_copy_out(h_ref, xs_ref, sem, dest_ref, i, kk):
    return pltpu.make_async_copy(h_ref.at[pl.ds(i, 1), :], xs_ref.at[pl.ds(dest_ref[0, kk, i], 1), :], sem)


def _dispatch_kernel(dest_ref, h_ref, xs_in_ref, xs_ref, sem, *, rows):
    del xs_in_ref

    def start(i, carry):
        for kk in range(TOP_K):
            _row_copy_out(h_ref, xs_ref, sem, dest_ref, i, kk).start()
        return carry

    def wait(i, carry):
        for kk in range(TOP_K):
            _row_copy_out(h_ref, xs_ref, sem, dest_ref, i, kk).wait()
        return carry

    lax.fori_loop(0, rows, start, 0)
    lax.fori_loop(0, rows, wait, 0)


def _dispatch(dest3, h2, xs, tile0, rows):
    n, d = h2.shape
    return pl.pallas_call(
        functools.partial(_dispatch_kernel, rows=rows),
        grid=(n // rows,),
        in_specs=[pl.BlockSpec((1, TOP_K, TOK_TILE), lambda i: (tile0 + i, 0, 0), memory_space=pltpu.SMEM),
                  pl.BlockSpec((rows, d), lambda i: (i, 0)),
                  pl.BlockSpec(memory_space=pl.ANY)],
        out_specs=pl.BlockSpec(memory_space=pl.ANY),
        out_shape=jax.ShapeDtypeStruct(xs.shape, xs.dtype),
        scratch_shapes=[pltpu.SemaphoreType.DMA(())],
        input_output_aliases={2: 0},
        compiler_params=_cparams("arbitrary"),
        name="dispatch",
    )(dest3, h2, xs)


def _moe_kernel(be_ref, nact_ref, x_ref, wgu_ref, bgu_ref, wd_ref, bd_ref, o_ref, wgu_bf, wd_bf, *, dff):
    i = pl.program_id(0)

    @pl.when(i < nact_ref[0])
    def _():
        prev = be_ref[jnp.maximum(i - 1, 0)]

        @pl.when((i == 0) | (be_ref[i] != prev))
        def _():
            wgu_bf[...] = wgu_ref[...].astype(BF16)
            wd_bf[...] = wd_ref[...].astype(BF16)

        gu = jnp.dot(x_ref[...].astype(BF16), wgu_bf[...], preferred_element_type=F32) + bgu_ref[...]
        gate = jnp.minimum(gu[:, :dff], SWIGLU_LIMIT)
        up = jnp.clip(gu[:, dff:], -SWIGLU_LIMIT, SWIGLU_LIMIT)
        act = (up + 1.0) * gate * _sigmoid(gate * SWIGLU_ALPHA)
        o_ref[...] = jnp.dot(act.astype(BF16), wd_bf[...], preferred_element_type=F32) + bd_ref[...]

    @pl.when(i >= nact_ref[0])
    def _():
        o_ref[...] = jnp.zeros_like(o_ref)


def _moe(blk_e, nact, xs, w_gu, b_gu, w_down, b_down, n_blocks):
    ne, d, dff2 = w_gu.shape
    dff = dff2 // 2
    blk = MOE_BLOCK
    grid_spec = pltpu.PrefetchScalarGridSpec(
        num_scalar_prefetch=2,
        grid=(n_blocks,),
        in_specs=[pl.BlockSpec((blk, d), lambda i, be, na: (i, 0)),
                  pl.BlockSpec((None, d, dff2), lambda i, be, na: (be[i], 0, 0)),
                  pl.BlockSpec((None, 1, dff2), lambda i, be, na: (be[i], 0, 0)),
                  pl.BlockSpec((None, dff, d), lambda i, be, na: (be[i], 0, 0)),
                  pl.BlockSpec((None, 1, d), lambda i, be, na: (be[i], 0, 0))],
        out_specs=pl.BlockSpec((blk, d), lambda i, be, na: (i, 0)),
        scratch_shapes=[pltpu.VMEM((d, dff2), BF16), pltpu.VMEM((dff, d), BF16)],
    )
    return pl.pallas_call(
        functools.partial(_moe_kernel, dff=dff),
        grid_spec=grid_spec,
        out_shape=jax.ShapeDtypeStruct((n_blocks * blk, d), F32),
        compiler_params=_cparams("arbitrary"),
        name="moe",
    )(blk_e, nact, xs, w_gu, b_gu.reshape(ne, 1, dff2), w_down, b_down.reshape(ne, 1, d))


def _row_copy_in(ys_ref, buf_ref, sem, dest_ref, i, kk):
    return pltpu.make_async_copy(ys_ref.at[pl.ds(dest_ref[0, kk, i], 1), :], buf_ref.at[kk, pl.ds(i, 1), :], sem)


def _combine_kernel(dest_ref, ys_ref, w_ref, x1_ref, gt_ref, o_ref, buf_ref, sem, *, rows):
    def start(i, carry):
        for kk in range(TOP_K):
            _row_copy_in(ys_ref, buf_ref, sem, dest_ref, i, kk).start()
        return carry

    def wait(i, carry):
        for kk in range(TOP_K):
            _row_copy_in(ys_ref, buf_ref, sem, dest_ref, i, kk).wait()
        return carry

    lax.fori_loop(0, rows, start, 0)
    lax.fori_loop(0, rows, wait, 0)
    w = w_ref[...]
    y = w[:rows, 0:1] * buf_ref[0]
    for kk in range(1, TOP_K):
        y = y + w[:rows, kk:kk + 1] * buf_ref[kk]
    o_ref[...] = x1_ref[...] + gt_ref[...] * y


def _combine(dest3, ys, wt3, x1, mod3, tile0, rows):
    nb, t, d = x1.shape
    rmod = mod3.shape[1]
    nt = t // rows
    row = lambda b, i: (b, i, 0)
    gate = pl.BlockSpec((None, rmod, d), (lambda b, i: (b, 0, 5)) if rmod == 1 else (lambda b, i: (b, i, 5)))
    return pl.pallas_call(
        functools.partial(_combine_kernel, rows=rows),
        grid=(nb, nt),
        in_specs=[pl.BlockSpec((1, TOP_K, TOK_TILE), lambda b, i: (tile0 + b * nt + i, 0, 0), memory_space=pltpu.SMEM),
                  pl.BlockSpec(memory_space=pl.ANY),
                  pl.BlockSpec((None, TOK_TILE, TOP_K), lambda b, i: (tile0 + b * nt + i, 0, 0)),
                  pl.BlockSpec((None, rows, d), row), gate],
        out_specs=pl.BlockSpec((None, rows, d), row),
        out_shape=jax.ShapeDtypeStruct((nb, t, d), F32),
        scratch_shapes=[pltpu.VMEM((TOP_K, rows, d), F32), pltpu.SemaphoreType.DMA(())],
        compiler_params=_cparams("arbitrary", "arbitrary"),
        name="combine",
    )(dest3, ys, wt3, x1, mod3)


def _rope_tables(pos, n_heads):
    half = HEAD_DIM // 2
    inv_freq = 1.0 / (ROPE_THETA ** (jnp.arange(0, HEAD_DIM, 2, dtype=F32) / HEAD_DIM))
    ang = pos.astype(F32)[:, None] * inv_freq[None, :]
    cos, sin = jnp.cos(ang), jnp.sin(ang)
    del half
    return (jnp.tile(jnp.concatenate([cos, cos], axis=-1), (1, n_heads)),
            jnp.tile(jnp.concatenate([-sin, sin], axis=-1), (1, n_heads)))


def _block_diag(width, value):
    h = np.arange(width) // HEAD_DIM
    return jnp.asarray(np.where(h[:, None] == h[None, :], value, 0.0), F32)


def kernel(x_prompt, x_sample, cache_k, cache_v, state_wkv, state_shift, c_prompt, c_sample, w_ada, b_ada, norm1_g, norm2_g, w_in, q_norm_g, k_norm_g, rwkv_mu, rwkv_w0, rwkv_w2, rwkv_a0, rwkv_a2, rwkv_g2, rwkv_k_k, rwkv_k_a, rwkv_r_k, rwkv_ln_w, rwkv_ln_b, w_out, router_w, router_b, moe_w_gu, moe_b_gu, moe_w_down, moe_b_down):
    nbp, t, d = x_prompt.shape
    nbs, ts, _ = x_sample.shape
    depth = w_ada.shape[0]
    assert depth == 1 and ts == 1
    n_heads = cache_k.shape[3]
    aw = n_heads * HEAD_DIM
    rwid = rwkv_w0.shape[1]
    rw_heads = rwid // HEAD_DIM
    rwc = rwkv_mu.shape[1]
    past = cache_k.shape[2]
    ne = router_w.shape[2]
    keep = min(MAX_WINDOW, t)
    lyr = 0

    w_in_bf = w_in[lyr].astype(BF16)
    wt_bf = w_out[lyr][:aw].astype(BF16)
    wb_bf = w_out[lyr][aw:].astype(BF16)
    g1 = norm1_g[lyr].reshape(1, d)
    g2 = norm2_g[lyr].reshape(1, d)
    qg = jnp.tile(q_norm_g[lyr], n_heads).reshape(1, aw)
    kg = jnp.tile(k_norm_g[lyr], n_heads).reshape(1, aw)
    bd_mean_a = _block_diag(aw, 1.0 / HEAD_DIM)
    bd_mean_r = _block_diag(rwid, 1.0 / HEAD_DIM)
    bd_ones_r = _block_diag(rwid, 1.0)
    dl = rwkv_w2.shape[1]
    w2p = jnp.zeros((LANES, rwid), F32).at[:dl].set(rwkv_w2[lyr])
    a2p = jnp.zeros((LANES, rwid), F32).at[dl:dl + rwkv_a2.shape[1]].set(rwkv_a2[lyr])
    vec = lambda a: a[lyr].reshape(1, -1)
    rwt = router_w[lyr].T
    rb = router_b[lyr].reshape(ne, 1)

    rows_c = nbp + nbs
    rows_pad = -(-rows_c // 8) * 8
    c_all = jnp.zeros((rows_pad, d), F32).at[:nbp].set(c_prompt).at[nbp:rows_c].set(c_sample)
    mod = _ada(c_all, w_ada[lyr], b_ada[lyr])
    mod_p = mod[:nbp].reshape(nbp, 1, 6 * d)
    mod_s = mod[nbp:rows_c].reshape(1, nbs, 6 * d)
    xs3 = x_sample.reshape(1, nbs, d)

    cos_p, sin_p = _rope_tables(jnp.arange(t), n_heads)
    cos_s, sin_s = _rope_tables(jnp.full((nbs,), PAST_LEN), n_heads)
    qp, kp, vp, rwp, kt_p, vt_p = _inproj(x_prompt, mod_p, g1, w_in_bf, cos_p, sin_p, qg, kg, bd_mean_a,
                                          ROW_TILE, keep=keep)
    qs, ks, vs, rws = _inproj(xs3, mod_s, g1, w_in[lyr], cos_s, sin_s, qg, kg, bd_mean_a, nbs)

    attn_p = _attn_prompt(qp, kp, vp)
    as_rows = lambda a: a.reshape(nbs, 1, -1)
    cache_t = lambda cch: jnp.transpose(cch[lyr], (0, 2, 3, 1)).reshape(nbs, aw, past)
    attn_s = _attn_sample(as_rows(qs), as_rows(ks), as_rows(vs), cache_t(cache_k), cache_t(cache_v))

    rw_args = (vec(rwkv_mu), vec(rwkv_w0), w2p, vec(rwkv_a0), a2p, rwkv_g2[lyr], vec(rwkv_k_k), vec(rwkv_k_a), bd_ones_r)
    pre_p = _rwprep(rwp, jnp.zeros((nbp, 1, rwc), F32), *rw_args, tm=ROW_TILE, whole_prev=False)
    pre_s = _rwprep(rws, state_shift[lyr].reshape(1, nbs, rwc), *rw_args, tm=nbs, whole_prev=True)
    r_p, ld_p, k_p, v_p, al_p, be_p, g_p = pre_p
    y_p, wkv_p = _rwchunk(r_p, ld_p, k_p, v_p, al_p, be_p)
    r_s, ld_s, k_s, v_s, al_s, be_s, g_s = pre_s
    y_s, wkv_s = _rwstep(*(as_rows(a) for a in (r_s, ld_s, k_s, v_s, al_s, be_s)), state_wkv[lyr])
    y_s = y_s.reshape(1, nbs, rwid)

    op_args = (rwkv_r_k[lyr].reshape(1, rwid), vec(rwkv_ln_w), vec(rwkv_ln_b), bd_mean_r, bd_ones_r, rwt, rb)
    x1_p, h2_p, lg_p = _outproj(x_prompt, attn_p, y_p, r_p, k_p, v_p, g_p, mod_p, g2, wt_bf, wb_bf, *op_args,
                                tm=ROW_TILE)
    x1_s, h2_s, lg_s = _outproj(xs3, attn_s.reshape(1, nbs, aw), y_s, r_s, k_s, v_s, g_s, mod_s, g2,
                                w_out[lyr][:aw], w_out[lyr][aw:], *op_args, tm=nbs)

    n_p = nbp * t
    n_valid = n_p + nbs
    assert n_p % ROUTE_TILE == 0 and ROUTE_TILE % TOK_TILE == 0
    npad = -(-n_valid // ROUTE_TILE) * ROUTE_TILE
    logits_t = jnp.zeros((ne, npad), F32).at[:, :n_p].set(lg_p).at[:, n_p:n_valid].set(lg_s)
    idx, wts, rank, cnt = _route(logits_t, n_valid)
    counts = cnt[:, 0].astype(I32)
    blk = MOE_BLOCK
    n_blocks = -(-(n_valid * TOP_K + ne * (blk - 1)) // blk)
    padded = (counts + blk - 1) // blk * blk
    pad_end = jnp.cumsum(padded)
    start = pad_end - padded
    blk_e = jnp.minimum(jnp.sum(pad_end[None, :] <= (jnp.arange(n_blocks) * blk)[:, None], axis=1), ne - 1).astype(I32)
    nact = (pad_end[-1] // blk).reshape(1).astype(I32)
    trash = n_blocks * blk
    dest = _dest(idx, rank, start.astype(I32), n_valid, trash)
    n_tiles = npad // TOK_TILE
    dest3 = dest.reshape(TOP_K, n_tiles, TOK_TILE).transpose(1, 0, 2)
    wt3 = wts.reshape(TOP_K, n_tiles, TOK_TILE).transpose(1, 2, 0)
    xs_buf = jnp.zeros((trash + 8, d), F32)
    xs_buf = _dispatch(dest3, h2_p.reshape(n_p, d), xs_buf, 0, TOK_TILE)
    xs_buf = _dispatch(dest3, h2_s.reshape(nbs, d), xs_buf, n_p // TOK_TILE, nbs)
    ys = _moe(blk_e, nact, xs_buf, moe_w_gu[lyr], moe_b_gu[lyr], moe_w_down[lyr], moe_b_down[lyr], n_blocks)
    y_prompt = _combine(dest3, ys, wt3, x1_p, mod_p, 0, TOK_TILE)
    y_sample = _combine(dest3, ys, wt3, x1_s, mod_s, n_p // TOK_TILE, nbs)

    kept = lambda a: jnp.transpose(a.reshape(nbp, n_heads, HEAD_DIM, keep), (0, 3, 1, 2))[None]
    return (y_prompt, y_sample.reshape(nbs, ts, d), kept(kt_p), kept(vt_p), wkv_p[None], rwp[:, t - 1][None],
            ks.reshape(nbs, ts, n_heads, HEAD_DIM)[None], vs.reshape(nbs, ts, n_heads, HEAD_DIM)[None],
            wkv_s[None], rws.reshape(nbs, rwc)[None])
```

```python
import functools

import numpy as np
import jax
import jax.numpy as jnp
from jax import lax
from jax.experimental import pallas as pl
from jax.experimental.pallas import tpu as pltpu

F32 = jnp.float32
BF16 = jnp.bfloat16
I32 = jnp.int32

HEAD_DIM = 64
LANES = 128
DILATED_PATTERNS = ((128, 1), (512, 4), (2048, 16))
WINDOW_STEPS = 128
MAX_WINDOW = 2048
PAST_LEN = 16384
ROPE_THETA = 10000.0
NORM_EPS = 1e-6
GN_EPS = 64e-5
TOP_K = 4
SWIGLU_ALPHA = 1.702
SWIGLU_LIMIT = 7.0
RW_CHUNK = 128
MOE_BLOCK = 256
ATTN_UNITS = 4
ROW_TILE = 512
TOK_TILE = 256
ROUTE_TILE = 512
VMEM_LIMIT = 56 * 1024 * 1024
NEG_BIG = -1e30


def _cparams(*sem):
    return pltpu.CompilerParams(dimension_semantics=sem, vmem_limit_bytes=VMEM_LIMIT)


def _dot(a, b):
    return jnp.dot(a.astype(BF16), b.astype(BF16), preferred_element_type=F32)


def _dot_nt(a, b):
    return lax.dot_general(a.astype(BF16), b.astype(BF16), (((1,), (1,)), ((), ())),
                           preferred_element_type=F32)


def _split2(a):
    hi = a.astype(BF16)
    lo = (a - hi.astype(F32)).astype(BF16)
    return hi, lo


def _split3(a):
    hi = a.astype(BF16)
    r1 = a - hi.astype(F32)
    mid = r1.astype(BF16)
    lo = (r1 - mid.astype(F32)).astype(BF16)
    return hi, mid, lo


def _dot_x(a, e):
    e = e.astype(BF16)
    hi, mid, lo = _split3(a)
    return (jnp.dot(hi, e, preferred_element_type=F32) + jnp.dot(mid, e, preferred_element_type=F32)
            + jnp.dot(lo, e, preferred_element_type=F32))


def _xdot(e, a):
    e = e.astype(BF16)
    hi, mid, lo = _split3(a)
    return (jnp.dot(e, hi, preferred_element_type=F32) + jnp.dot(e, mid, preferred_element_type=F32)
            + jnp.dot(e, lo, preferred_element_type=F32))


def _dot3(a, b):
    ah, al = _split2(a)
    bh, bl = _split2(b)
    return (jnp.dot(ah, bh, preferred_element_type=F32) + jnp.dot(ah, bl, preferred_element_type=F32)
            + jnp.dot(al, bh, preferred_element_type=F32))


def _dot3_nt(a, b):
    ah, al = _split2(a)
    bh, bl = _split2(b)
    dn = (((1,), (1,)), ((), ()))
    return (lax.dot_general(ah, bh, dn, preferred_element_type=F32)
            + lax.dot_general(ah, bl, dn, preferred_element_type=F32)
            + lax.dot_general(al, bh, dn, preferred_element_type=F32))


def _mm(a, w):
    if w.dtype == BF16:
        return jnp.dot(a.astype(BF16), w, preferred_element_type=F32)
    return _dot3(a, w)


def _sigmoid(x):
    return 1.0 / (1.0 + jnp.exp(-x))


def _ada_kernel(c_ref, w_ref, b_ref, o_ref):
    c = c_ref[...]
    o_ref[...] = _dot3(c * _sigmoid(c), w_ref[...]) + b_ref[...]


def _ada(c_all, w_ada, b_ada):
    rows, d = c_all.shape
    n = w_ada.shape[1]
    tn = n // 4
    return pl.pallas_call(
        _ada_kernel,
        grid=(n // tn,),
        in_specs=[pl.BlockSpec((rows, d), lambda j: (0, 0)),
                  pl.BlockSpec((d, tn), lambda j: (0, j)),
                  pl.BlockSpec((1, tn), lambda j: (0, j))],
        out_specs=pl.BlockSpec((rows, tn), lambda j: (0, j)),
        out_shape=jax.ShapeDtypeStruct((rows, n), F32),
        compiler_params=_cparams("arbitrary"),
        name="ada",
    )(c_all, w_ada, b_ada.reshape(1, n))


def _inproj_kernel(x_ref, sh_ref, sc_ref, g_ref, w_ref, cos_ref, sin_ref, qg_ref, kg_ref, bd_ref,
                   q_ref, k_ref, v_ref, rw_ref, *maybe_kv_t, aw, first_kept):
    x = x_ref[...]
    ms = jnp.mean(x * x, axis=-1, keepdims=True)
    h = x * lax.rsqrt(ms + NORM_EPS) * g_ref[...] * (1.0 + sc_ref[...]) + sh_ref[...]
    proj = _mm(h, w_ref[...])
    cos = cos_ref[...]
    sin = sin_ref[...]
    lane = lax.broadcasted_iota(I32, (1, aw), 1)
    first_half = (lane % HEAD_DIM) < (HEAD_DIM // 2)
    bd = bd_ref[...]

    def norm_rope(t, g):
        tn = t * lax.rsqrt((_dot if w_ref.dtype == BF16 else _dot_x)(t * t, bd) + NORM_EPS) * g
        rot = jnp.where(first_half, pltpu.roll(tn, aw - HEAD_DIM // 2, 1), pltpu.roll(tn, HEAD_DIM // 2, 1))
        return tn * cos + rot * sin

    scale = 1.0 / np.sqrt(HEAD_DIM).astype(np.float32)
    q_ref[...] = norm_rope(proj[:, :aw], qg_ref[...]) * scale
    k = norm_rope(proj[:, aw:2 * aw], kg_ref[...])
    v = proj[:, 2 * aw:3 * aw]
    k_ref[...] = k
    v_ref[...] = v
    rw_ref[...] = proj[:, 3 * aw:]
    if maybe_kv_t:
        kt_ref, vt_ref = maybe_kv_t
        kept = pl.program_id(1) >= first_kept

        @pl.when(kept)
        def _():
            kt_ref[...] = k.T
            vt_ref[...] = v.T

        @pl.when(jnp.logical_not(kept))
        def _():
            kt_ref[...] = jnp.zeros_like(kt_ref)
            vt_ref[...] = jnp.zeros_like(vt_ref)


def _inproj(x3, mod3, g1, w_in_bf, cos_t, sin_t, qg, kg, bd_mean, tm, keep=0):
    nb, t, d = x3.shape
    r = mod3.shape[1]
    ncol = w_in_bf.shape[1]
    aw = cos_t.shape[1]
    rwc = ncol - 3 * aw
    grid = (nb, t // tm)
    row = lambda b, i: (b, i, 0)
    const = lambda b, i: (0, 0)
    mod_spec = lambda s: pl.BlockSpec((None, r, d), (lambda b, i: (b, 0, s)) if r == 1 else (lambda b, i: (b, i, s)))
    out_specs = [pl.BlockSpec((None, tm, aw), row)] * 3 + [pl.BlockSpec((None, tm, rwc), row)]
    out_shape = [jax.ShapeDtypeStruct((nb, t, aw), F32)] * 3 + [jax.ShapeDtypeStruct((nb, t, rwc), F32)]
    first_kept = (t - keep) // tm
    if keep:
        assert keep % tm == 0 and (t - keep) % tm == 0
        kept_spec = pl.BlockSpec((None, aw, tm), lambda b, i: (b, 0, jnp.maximum(i - first_kept, 0)))
        out_specs += [kept_spec, kept_spec]
        out_shape += [jax.ShapeDtypeStruct((nb, aw, keep), F32)] * 2
    outs = pl.pallas_call(
        functools.partial(_inproj_kernel, aw=aw, first_kept=first_kept),
        grid=grid,
        in_specs=[pl.BlockSpec((None, tm, d), row), mod_spec(0), mod_spec(1),
                  pl.BlockSpec((1, d), const), pl.BlockSpec((d, ncol), const),
                  pl.BlockSpec((tm, aw), lambda b, i: (i, 0)), pl.BlockSpec((tm, aw), lambda b, i: (i, 0)),
                  pl.BlockSpec((1, aw), const), pl.BlockSpec((1, aw), const), pl.BlockSpec((aw, aw), const)],
        out_specs=out_specs,
        out_shape=out_shape,
        compiler_params=_cparams("parallel", "arbitrary"),
        name="inproj",
    )(x3, mod3, mod3, g1, w_in_bf, cos_t, sin_t, qg, kg, bd_mean)
    return outs


def _attn_prompt_kernel(q_ref, k_ref, v_ref, o_ref, kp_ref, vp_ref, m_ref, l_ref, acc_ref, *, t, pad):
    nq = WINDOW_STEPS
    nk = 2 * WINDOW_STEPS
    kp_ref[pl.ds(0, pad), :] = jnp.zeros((pad, LANES), F32)
    vp_ref[pl.ds(0, pad), :] = jnp.zeros((pad, LANES), F32)
    kp_ref[pl.ds(pad, t), :] = k_ref[...]
    vp_ref[pl.ds(pad, t), :] = v_ref[...]
    head0 = lax.broadcasted_iota(I32, (1, LANES), 1) < HEAD_DIM
    qi = lax.broadcasted_iota(I32, (nq, nk), 0)
    kj = lax.broadcasted_iota(I32, (nq, nk), 1)
    steps_back = qi + nq - kj
    band = (steps_back >= 0) & (steps_back <= WINDOW_STEPS)
    has_past = kj >= nq

    hsels = (head0, jnp.logical_not(head0))
    nt = (((1,), (1,)), ((), ()))

    for p, (_, d) in enumerate(DILATED_PATTERNS):
        def units(g, carry, p=p, d=d):
            rows_q, kb, vb, valid, qh = [], [], [], [], []
            for j in range(ATTN_UNITS):
                u = g * ATTN_UNITS + j
                res = u % d
                blk = u // d
                q_start = res + d * nq * blk
                k_start = pad + q_start - d * nq
                if d == 1:
                    rows_q.append(pl.ds(q_start, nq))
                    rows_k = pl.ds(k_start, nk)
                else:
                    rows_q.append(pl.ds(q_start, nq, stride=d))
                    rows_k = pl.ds(k_start, nk, stride=d)
                q = q_ref[rows_q[j], :]
                kb.append(kp_ref[rows_k, :].astype(BF16))
                vb.append(vp_ref[rows_k, :].astype(BF16))
                valid.append(band & (has_past | (blk > 0)))
                qh.append([jnp.where(hsel, q, 0.0).astype(BF16) for hsel in hsels])
            chains = [(j, h) for j in range(ATTN_UNITS) for h in range(2)]
            s = [lax.dot_general(qh[j][h], kb[j], nt, preferred_element_type=F32) for j, h in chains]
            s = [jnp.where(valid[j], sc, NEG_BIG) for (j, h), sc in zip(chains, s)]
            mx = [jnp.max(sc, axis=-1, keepdims=True) for sc in s]
            e = [jnp.exp(sc - m) for sc, m in zip(s, mx)]
            den = [jnp.sum(ec, axis=-1, keepdims=True) for ec in e]
            o = [jnp.dot(ec.astype(BF16), vb[j], preferred_element_type=F32) for (j, h), ec in zip(chains, e)]
            for j in range(ATTN_UNITS):
                m_ref[p, rows_q[j], :] = jnp.where(head0, mx[2 * j], mx[2 * j + 1])
                l_ref[p, rows_q[j], :] = jnp.where(head0, den[2 * j], den[2 * j + 1])
                acc_ref[p, rows_q[j], :] = jnp.where(head0, o[2 * j], o[2 * j + 1])
            return carry

        lax.fori_loop(0, t // nq // ATTN_UNITS, units, 0)

    rows = 256

    def merge(i, carry):
        sl = pl.ds(pl.multiple_of(i * rows, rows), rows)
        m0, m1, m2 = m_ref[0, sl, :], m_ref[1, sl, :], m_ref[2, sl, :]
        mm = jnp.maximum(jnp.maximum(m0, m1), m2)
        w0, w1, w2 = jnp.exp(m0 - mm), jnp.exp(m1 - mm), jnp.exp(m2 - mm)
        num = w0 * acc_ref[0, sl, :] + w1 * acc_ref[1, sl, :] + w2 * acc_ref[2, sl, :]
        den = w0 * l_ref[0, sl, :] + w1 * l_ref[1, sl, :] + w2 * l_ref[2, sl, :]
        o_ref[sl, :] = num / den
        return carry

    lax.fori_loop(0, t // rows, merge, 0)


def _attn_prompt(q, k, v):
    nb, t, aw = q.shape
    pad = MAX_WINDOW
    assert t % MAX_WINDOW == 0
    spec = pl.BlockSpec((None, t, LANES), lambda b, hp: (b, 0, hp))
    return pl.pallas_call(
        functools.partial(_attn_prompt_kernel, t=t, pad=pad),
        grid=(nb, aw // LANES),
        in_specs=[spec, spec, spec],
        out_specs=spec,
        out_shape=jax.ShapeDtypeStruct((nb, t, aw), F32),
        scratch_shapes=[pltpu.VMEM((pad + t, LANES), F32), pltpu.VMEM((pad + t, LANES), F32),
                        pltpu.VMEM((3, t, LANES), F32), pltpu.VMEM((3, t, LANES), F32),
                        pltpu.VMEM((3, t, LANES), F32)],
        compiler_params=_cparams("parallel", "parallel"),
        name="attn_prompt",
    )(q, k, v)


def _attn_sample_kernel(q_ref, kn_ref, vn_ref, kt_ref, vt_ref, o_ref, *, nh, w):
    aw = nh * HEAD_DIM
    dist = w - lax.broadcasted_iota(I32, (1, w), 1)
    mult = jnp.zeros((1, w), F32)
    for win, d in DILATED_PATTERNS:
        mult = mult + jnp.where((dist % d == 0) & (dist <= win), 1.0, 0.0)
    n_pat = float(len(DILATED_PATTERNS))
    q_col = _col(q_ref[...], aw)
    kn_col = _col(kn_ref[...], aw)
    vn_col = _col(vn_ref[...], aw)
    outs = []
    for h in range(nh):
        hs = pl.ds(h * HEAD_DIM, HEAD_DIM)
        qh = q_col[h * HEAD_DIM:(h + 1) * HEAD_DIM]
        s = jnp.sum(kt_ref[hs, :] * qh, axis=0, keepdims=True)
        s_self = jnp.sum(qh * kn_col[h * HEAD_DIM:(h + 1) * HEAD_DIM], axis=0, keepdims=True)
        s = jnp.where(mult > 0.0, s, NEG_BIG)
        mx = jnp.maximum(jnp.max(s, axis=1, keepdims=True), s_self)
        pr = mult * jnp.exp(s - mx)
        p_self = n_pat * jnp.exp(s_self - mx)
        den = jnp.sum(pr, axis=1, keepdims=True) + p_self
        num = jnp.sum(vt_ref[hs, :] * pr, axis=1, keepdims=True) + p_self * vn_col[h * HEAD_DIM:(h + 1) * HEAD_DIM]
        outs.append(num / den)
    o_ref[...] = _row(jnp.concatenate(outs, axis=0), aw)


def _attn_sample(q, kn, vn, cache_kt, cache_vt):
    nb, _, aw = q.shape
    w = cache_kt.shape[2]
    assert w == MAX_WINDOW
    row = pl.BlockSpec((None, 1, aw), lambda i: (i, 0, 0))
    mat = pl.BlockSpec((None, aw, w), lambda i: (i, 0, 0))
    return pl.pallas_call(
        functools.partial(_attn_sample_kernel, nh=aw // HEAD_DIM, w=w),
        grid=(nb,),
        in_specs=[row, row, row, mat, mat],
        out_specs=row,
        out_shape=jax.ShapeDtypeStruct((nb, 1, aw), F32),
        compiler_params=_cparams("parallel"),
        name="attn_sample",
    )(q, kn, vn, cache_kt, cache_vt)


def _rwprep_kernel(p_ref, prev_ref, first_ref, mu_ref, w0_ref, w2_ref, a0_ref, a2_ref, g2_ref, kk_ref, ka_ref,
                   bd_ref, r_ref, ld_ref, k_ref, v_ref, al_ref, be_ref, g_ref, *, rwid, whole_prev):
    p = p_ref[...]
    if whole_prev:
        prev = first_ref[...]
    else:
        tm = p.shape[0]
        before = jnp.where(pl.program_id(1) == 0, first_ref[...], prev_ref[7:8, :])
        rowi = lax.broadcasted_iota(I32, (tm, 1), 0)
        prev = jnp.where(rowi == 0, before, pltpu.roll(p, 1, 0))
    xs = p + mu_ref[...] * (prev - p)
    r = xs[:, :rwid]
    k = xs[:, rwid:2 * rwid]
    v = xs[:, 2 * rwid:3 * rwid]
    xwa = xs[:, 3 * rwid:3 * rwid + LANES]
    xg = xs[:, 3 * rwid + LANES:]
    z = w0_ref[...] + _dot3(jnp.tanh(xwa), w2_ref[...])
    softplus_neg = jnp.maximum(-z, 0.0) + jnp.log(1.0 + jnp.exp(-jnp.abs(z)))
    w = -softplus_neg - 0.5
    a = _sigmoid(a0_ref[...] + _dot3(xwa, a2_ref[...]))
    g = _dot3(_sigmoid(xg), g2_ref[...])
    kk = k * kk_ref[...]
    norm = jnp.sqrt(_dot_x(kk * kk, bd_ref[...]))
    kk = kk / jnp.maximum(norm, 1e-12)
    r_ref[...] = r
    ld_ref[...] = -jnp.exp(w)
    k_ref[...] = k * (1.0 + (a - 1.0) * ka_ref[...])
    v_ref[...] = v
    al_ref[...] = -kk
    be_ref[...] = kk * a
    g_ref[...] = g


def _rwprep(rw, first, mu, w0, w2p, a0, a2p, g2, k_k, k_a, bd_ones, tm, whole_prev):
    nb, t, rwc = rw.shape
    rwid = w0.shape[1]
    grid = (nb, t // tm)
    row = lambda b, i: (b, i, 0)
    const = lambda b, i: (0, 0)
    if whole_prev:
        prev_spec = pl.BlockSpec((None, tm, rwc), row)
        first_spec = pl.BlockSpec((None, tm, rwc), row)
    else:
        prev_spec = pl.BlockSpec((None, 8, rwc), lambda b, i: (b, jnp.maximum(i * (tm // 8) - 1, 0), 0))
        first_spec = pl.BlockSpec((None, 1, rwc), lambda b, i: (b, 0, 0))
    vec = pl.BlockSpec((1, rwid), const)
    out = pl.BlockSpec((None, tm, rwid), row)
    return pl.pallas_call(
        functools.partial(_rwprep_kernel, rwid=rwid, whole_prev=whole_prev),
        grid=grid,
        in_specs=[pl.BlockSpec((None, tm, rwc), row), prev_spec, first_spec,
                  pl.BlockSpec((1, rwc), const), vec, pl.BlockSpec((LANES, rwid), const),
                  vec, pl.BlockSpec((LANES, rwid), const), pl.BlockSpec((LANES, rwid), const), vec, vec,
                  pl.BlockSpec((rwid, rwid), const)],
        out_specs=[out] * 7,
        out_shape=[jax.ShapeDtypeStruct((nb, t, rwid), F32)] * 7,
        compiler_params=_cparams("parallel", "parallel"),
        name="rwprep",
    )(rw, rw, first, mu, w0, w2p, a0, a2p, g2, k_k, k_a, bd_ones)


def _rwchunk_kernel(r_ref, ld_ref, k_ref, v_ref, al_ref, be_ref, y_ref, st_ref, z_ref, *, nb):
    c = RW_CHUNK
    ci = pl.program_id(1)

    @pl.when(ci == 0)
    def _():
        z_ref[...] = jnp.zeros_like(z_ref)

    ti = lax.broadcasted_iota(I32, (c, c), 0)
    si = lax.broadcasted_iota(I32, (c, c), 1)
    low_incl = si <= ti
    low_strict = si < ti
    diag = si == ti
    tri = jnp.where(low_incl, 1.0, 0.0).astype(BF16)
    eye = jnp.where(diag, 1.0, 0.0)
    head0 = lax.broadcasted_iota(I32, (1, LANES), 1) < HEAD_DIM
    hsels = (head0, jnp.logical_not(head0))
    same_head = (ti < HEAD_DIM) == (si < HEAD_DIM)
    nt = (((1,), (1,)), ((), ()))
    batches = range(nb)
    chains = [(b, h) for b in batches for h in range(2)]

    cum = [_xdot(tri, ld_ref[b]) for b in batches]
    tot = [cm[c - 1:c, :] for cm in cum]
    e_neg = [jnp.exp(-cm) for cm in cum]
    at = [al_ref[b] * jnp.exp(cum[b] - ld_ref[b]) for b in batches]
    rt = [r_ref[b] * jnp.exp(cum[b]) for b in batches]
    rhs_t = [jnp.concatenate([be_ref[b] * e_neg[b], k_ref[b] * e_neg[b]], axis=0).astype(BF16) for b in batches]
    vb = [v_ref[b].astype(BF16) for b in batches]
    at_h = [jnp.where(hsels[h], at[b], 0.0) for b, h in chains]
    rt_h = [jnp.where(hsels[h], rt[b], 0.0) for b, h in chains]
    a4 = [lax.dot_general(jnp.concatenate([a, r], axis=0).astype(BF16), rhs_t[b], nt, preferred_element_type=F32)
          for (b, h), a, r in zip(chains, at_h, rt_h)]
    a_ab = [jnp.where(low_strict, m[:c, :c], 0.0) for m in a4]
    a_ak = [jnp.where(low_strict, m[:c, c:], 0.0).astype(BF16) for m in a4]
    a_r = [jnp.concatenate([jnp.where(low_incl, m[c:, :c], 0.0), jnp.where(low_incl, m[c:, c:], 0.0)],
                           axis=1).astype(BF16) for m in a4]
    pw = [m.astype(BF16) for m in a_ab]
    inv = [eye + m for m in a_ab]
    for _ in range(int(np.log2(c)) - 1):
        pw = [jnp.dot(m, m, preferred_element_type=F32).astype(BF16) for m in pw]
        inv = [i + jnp.dot(i.astype(BF16), m, preferred_element_type=F32) for i, m in zip(inv, pw)]
    akv = [jnp.dot(m, vb[b], preferred_element_type=F32) for (b, h), m in zip(chains, a_ak)]
    x = [_dot(i, jnp.concatenate([kv, a], axis=1)) for i, kv, a in zip(inv, akv, at_h)]
    u0_h = [m[:, :LANES] for m in x]
    at2_h = [m[:, LANES:] for m in x]
    y0_h = [jnp.dot(ar, jnp.concatenate([u0, v_ref[b]], axis=0).astype(BF16), preferred_element_type=F32)
            for (b, h), ar, u0 in zip(chains, a_r, u0_h)]
    rt2_h = [r + jnp.dot(ar[:, :c], a2.astype(BF16), preferred_element_type=F32)
             for r, ar, a2 in zip(rt_h, a_r, at2_h)]
    z = [z_ref[b] for b in batches]
    uy = [_dot(jnp.concatenate([at2_h[2 * b] + at2_h[2 * b + 1], rt2_h[2 * b] + rt2_h[2 * b + 1]], axis=0), z[b])
          for b in batches]
    u = [uy[b][:c] + jnp.where(head0, u0_h[2 * b], u0_h[2 * b + 1]) for b in batches]
    for b in batches:
        y_ref[b] = uy[b][c:] + jnp.where(head0, y0_h[2 * b], y0_h[2 * b + 1])
    e_end = [jnp.exp(tot[b] - cum[b]) for b in batches]
    lhs_t = [jnp.concatenate([be_ref[b] * e_end[b], k_ref[b] * e_end[b]], axis=0) for b in batches]
    zadd = [_dot(lhs_t[b].T, jnp.concatenate([u[b], v_ref[b]], axis=0)) for b in batches]
    for b in batches:
        dcol = jnp.sum(jnp.where(diag, jnp.broadcast_to(jnp.exp(tot[b]), (c, c)), 0.0), axis=1, keepdims=True)
        z_ref[b] = dcol * z[b] + jnp.where(same_head, zadd[b], 0.0)

    @pl.when(ci == pl.num_programs(1) - 1)
    def _():
        for b in batches:
            s = z_ref[b].T
            st_ref[b, 0] = s[:HEAD_DIM, :HEAD_DIM]
            st_ref[b, 1] = s[HEAD_DIM:, HEAD_DIM:]


def _rwchunk(r, ld, k, v, al, be):
    nb, t, rwid = r.shape
    c = RW_CHUNK
    assert t % c == 0 and c == LANES
    npair = rwid // LANES
    seq = pl.BlockSpec((nb, c, LANES), lambda hp, ci: (0, ci, hp))
    return pl.pallas_call(
        functools.partial(_rwchunk_kernel, nb=nb),
        grid=(npair, t // c),
        in_specs=[seq] * 6,
        out_specs=[seq, pl.BlockSpec((nb, 2, HEAD_DIM, HEAD_DIM), lambda hp, ci: (0, hp, 0, 0))],
        out_shape=[jax.ShapeDtypeStruct((nb, t, rwid), F32),
                   jax.ShapeDtypeStruct((nb, 2 * npair, HEAD_DIM, HEAD_DIM), F32)],
        scratch_shapes=[pltpu.VMEM((nb, LANES, LANES), F32)],
        compiler_params=_cparams("parallel", "arbitrary"),
        name="rwchunk",
    )(r, ld, k, v, al, be)


def _col(row, width):
    return jnp.broadcast_to(row, (LANES, width)).T[:, 0:1]


def _row(col, width):
    return jnp.broadcast_to(col, (width, LANES)).T[0:1, :]


def _rwstep_kernel(r_ref, ld_ref, k_ref, v_ref, al_ref, be_ref, s_ref, y_ref, so_ref, *, bb, nh):
    rwid = nh * HEAD_DIM

    def one(b, carry):
        r, dcy, k, al, be = (ref[b] for ref in (r_ref, ld_ref, k_ref, al_ref, be_ref))
        dcy = jnp.exp(dcy)
        v_col = _col(v_ref[b], rwid)
        ys = []
        for h in range(nh):
            hs = slice(h * HEAD_DIM, (h + 1) * HEAD_DIM)
            s = s_ref[b, h]
            sa = jnp.sum(s * al[:, hs], axis=1, keepdims=True)
            s = s * dcy[:, hs] + sa * be[:, hs] + v_col[hs, :] * k[:, hs]
            so_ref[b, h] = s
            ys.append(jnp.sum(s * r[:, hs], axis=1, keepdims=True))
        y_ref[b] = _row(jnp.concatenate(ys, axis=0), rwid)
        return carry

    lax.fori_loop(0, bb, one, 0)


def _rwstep(r, ld, k, v, al, be, state):
    nb, _, rwid = r.shape
    nh = state.shape[1]
    bb = 8
    row = pl.BlockSpec((bb, 1, rwid), lambda i: (i, 0, 0))
    st = pl.BlockSpec((bb, nh, HEAD_DIM, HEAD_DIM), lambda i: (i, 0, 0, 0))
    return pl.pallas_call(
        functools.partial(_rwstep_kernel, bb=bb, nh=nh),
        grid=(nb // bb,),
        in_specs=[row] * 6 + [st],
        out_specs=[row, st],
        out_shape=[jax.ShapeDtypeStruct((nb, 1, rwid), F32), jax.ShapeDtypeStruct(state.shape, F32)],
        compiler_params=_cparams("parallel"),
        name="rwstep",
    )(r, ld, k, v, al, be, state)


def _outproj_kernel(x_ref, at_ref, y_ref, r_ref, k_ref, v_ref, g_ref, gt_ref, sh_ref, sc_ref, g2_ref,
                    wt_ref, wb_ref, rk_ref, lw_ref, lb_ref, bdm_ref, bd1_ref, rwt_ref, rb_ref, *rest):
    x1_ref, h2_ref, lg_ref = rest[-3:]
    y = y_ref[...]
    bdm = bdm_ref[...]
    mean = _dot_x(y, bdm)
    yc = y - mean
    var = _dot_x(yc * yc, bdm)
    yn = yc * lax.rsqrt(var + GN_EPS) * lw_ref[...] + lb_ref[...]
    v = v_ref[...]
    bonus = _dot_x(r_ref[...] * k_ref[...] * rk_ref[...], bd1_ref[...]) * v
    rw = (yn + bonus) * g_ref[...]
    mix = _mm(at_ref[...], wt_ref[...]) + _mm(rw, wb_ref[...])
    x1 = x_ref[...] + gt_ref[...] * mix
    x1_ref[...] = x1
    ms = jnp.mean(x1 * x1, axis=-1, keepdims=True)
    h2 = x1 * lax.rsqrt(ms + NORM_EPS) * g2_ref[...] * (1.0 + sc_ref[...]) + sh_ref[...]
    h2_ref[...] = h2
    lg_ref[...] = _dot3_nt(rwt_ref[...], h2) + rb_ref[...]


def _outproj(x3, attn, y, r, k, v, g, mod3, g2, wt_bf, wb_bf, r_k, ln_w, ln_b, bd_mean, bd_ones, rwt, rb, tm,
             tok_rows, tok_row0=0, tok_buf=None):
    nb, t, d = x3.shape
    rmod = mod3.shape[1]
    aw = attn.shape[2]
    ne = rwt.shape[0]
    nt = t // tm
    assert tok_row0 % tm == 0
    grid = (nb, nt)
    row = lambda b, i: (b, i, 0)
    const = lambda b, i: (0, 0)
    mod_spec = lambda s: pl.BlockSpec((None, rmod, d), (lambda b, i: (b, 0, s)) if rmod == 1 else (lambda b, i: (b, i, s)))
    half = pl.BlockSpec((None, tm, aw), row)
    vec = pl.BlockSpec((1, aw), const)
    aliased = [] if tok_buf is None else [tok_buf]
    return pl.pallas_call(
        _outproj_kernel,
        grid=grid,
        in_specs=[pl.BlockSpec((None, tm, d), row)] + [half] * 6 + [mod_spec(2), mod_spec(3), mod_spec(4),
                  pl.BlockSpec((1, d), const), pl.BlockSpec((aw, d), const), pl.BlockSpec((aw, d), const),
                  vec, vec, vec, pl.BlockSpec((aw, aw), const), pl.BlockSpec((aw, aw), const),
                  pl.BlockSpec((ne, d), const), pl.BlockSpec((ne, 1), const)]
                 + [pl.BlockSpec(memory_space=pl.ANY)] * len(aliased),
        out_specs=[pl.BlockSpec((None, tm, d), row),
                   pl.BlockSpec((tm, d), lambda b, i: (tok_row0 // tm + b * nt + i, 0)),
                   pl.BlockSpec((ne, tm), lambda b, i: (0, b * nt + i))],
        out_shape=[jax.ShapeDtypeStruct((nb, t, d), F32), jax.ShapeDtypeStruct((tok_rows, d), F32),
                   jax.ShapeDtypeStruct((ne, nb * t), F32)],
        input_output_aliases={20: 1} if aliased else {},
        compiler_params=_cparams("parallel", "parallel"),
        name="outproj",
    )(x3, attn, y, r, k, v, g, mod3, mod3, mod3, g2, wt_bf, wb_bf, r_k, ln_w, ln_b, bd_mean, bd_ones, rwt, rb,
      *aliased)


def _route_kernel(lg_ref, idx_ref, wt_ref, rank_ref, cnt_ref, carry_ref, *, n_valid):
    i = pl.program_id(0)
    ne, tn = lg_ref.shape

    @pl.when(i == 0)
    def _():
        carry_ref[...] = jnp.zeros_like(carry_ref)

    lg = lg_ref[...]
    eidx = lax.broadcasted_iota(I32, (ne, tn), 0).astype(F32)
    tok = i * tn + lax.broadcasted_iota(I32, (1, tn), 1)
    live = tok < n_valid
    vals, hots = [], []
    for _ in range(TOP_K):
        mx = jnp.max(lg, axis=0, keepdims=True)
        pick = jnp.min(jnp.where(lg == mx, eidx, float(ne)), axis=0, keepdims=True)
        hot = eidx == pick
        vals.append(mx)
        hots.append(hot)
        idx_ref[pl.ds(len(vals) - 1, 1), :] = pick.astype(I32)
        lg = jnp.where(hot, -jnp.inf, lg)
    ex = [jnp.exp(vv - vals[0]) for vv in vals]
    den = ex[0] + ex[1] + ex[2] + ex[3]
    for kk in range(TOP_K):
        wt_ref[pl.ds(kk, 1), :] = ex[kk] / den
    hot_all = jnp.zeros((ne, tn), F32)
    for hot in hots:
        hot_all = hot_all + jnp.where(hot & live, 1.0, 0.0)
    ri = lax.broadcasted_iota(I32, (tn, tn), 0)
    cj = lax.broadcasted_iota(I32, (tn, tn), 1)
    upper = jnp.where(ri <= cj, 1.0, 0.0).astype(BF16)
    incl = jnp.dot(hot_all.astype(BF16), upper, preferred_element_type=F32)
    before = carry_ref[:, 0:1] + incl - hot_all
    for kk, hot in enumerate(hots):
        rank_ref[pl.ds(kk, 1), :] = jnp.sum(jnp.where(hot, before, 0.0), axis=0, keepdims=True).astype(I32)
    carry_ref[...] = carry_ref[...] + jnp.sum(hot_all, axis=1, keepdims=True)
    cnt_ref[...] = carry_ref[...]


def _route(logits_t, n_valid):
    ne, npad = logits_t.shape
    tn = ROUTE_TILE
    tile = lambda rows: pl.BlockSpec((rows, tn), lambda i: (0, i))
    return pl.pallas_call(
        functools.partial(_route_kernel, n_valid=n_valid),
        grid=(npad // tn,),
        in_specs=[tile(ne)],
        out_specs=[tile(TOP_K), tile(TOP_K), tile(TOP_K), pl.BlockSpec((ne, LANES), lambda i: (0, 0))],
        out_shape=[jax.ShapeDtypeStruct((TOP_K, npad), I32), jax.ShapeDtypeStruct((TOP_K, npad), F32),
                   jax.ShapeDtypeStruct((TOP_K, npad), I32), jax.ShapeDtypeStruct((ne, LANES), F32)],
        scratch_shapes=[pltpu.VMEM((ne, LANES), F32)],
        compiler_params=_cparams("arbitrary"),
        name="route",
    )(logits_t)


def _dest_kernel(idx_ref, rank_ref, start_ref, dest_ref, *, n_valid, trash):
    i = pl.program_id(0)
    ne = start_ref.shape[0]
    tn = idx_ref.shape[1]
    eidx = lax.broadcasted_iota(I32, (ne, tn), 0)
    tok = i * tn + lax.broadcasted_iota(I32, (1, tn), 1)
    start = start_ref[:, 0:1]
    for kk in range(TOP_K):
        base = jnp.sum(jnp.where(eidx == idx_ref[pl.ds(kk, 1), :], start, 0.0), axis=0, keepdims=True)
        dest_ref[pl.ds(kk, 1), :] = jnp.where(tok < n_valid, base.astype(I32) + rank_ref[pl.ds(kk, 1), :], trash)


def _dest(idx, rank, start, n_valid, trash):
    _, npad = idx.shape
    ne = start.shape[0]
    tn = ROUTE_TILE
    tile = pl.BlockSpec((TOP_K, tn), lambda i: (0, i))
    return pl.pallas_call(
        functools.partial(_dest_kernel, n_valid=n_valid, trash=trash),
        grid=(npad // tn,),
        in_specs=[tile, tile, pl.BlockSpec((ne, LANES), lambda i: (0, 0))],
        out_specs=tile,
        out_shape=jax.ShapeDtypeStruct((TOP_K, npad), I32),
        compiler_params=_cparams("parallel"),
        name="dest",
    )(idx, rank, jnp.broadcast_to(start.astype(F32)[:, None], (ne, LANES)))


def _slotmap_kernel(start_ref, count_ref, dest_ref, map_ref, *, ne, trash_code):
    i = pl.program_id(0)

    @pl.when(i == 0)
    def _():
        def fill(s, carry):
            map_ref[s] = trash_code
            return carry

        for e in range(ne):
            first = start_ref[e] + count_ref[e]
            last = start_ref[e + 1] if e + 1 < ne else map_ref.shape[0]
            lax.fori_loop(first, last, fill, 0)

    def put(j, carry):
        for kk in range(TOP_K):
            map_ref[dest_ref[0, kk, j]] = (i * TOK_TILE + j) * TOP_K + kk
        return carry

    lax.fori_loop(0, TOK_TILE, put, 0, unroll=8)


def _slotmap(start, counts, dest3, n_slots, trash_code):
    n_tiles = dest3.shape[0]
    grid_spec = pltpu.PrefetchScalarGridSpec(
        num_scalar_prefetch=2,
        grid=(n_tiles,),
        in_specs=[pl.BlockSpec((1, TOP_K, TOK_TILE), lambda i, st, ct: (i, 0, 0), memory_space=pltpu.SMEM)],
        out_specs=pl.BlockSpec(memory_space=pltpu.SMEM),
    )
    return pl.pallas_call(
        functools.partial(_slotmap_kernel, ne=start.shape[0], trash_code=trash_code),
        grid_spec=grid_spec,
        out_shape=jax.ShapeDtypeStruct((n_slots,), I32),
        compiler_params=_cparams("arbitrary"),
        name="slotmap",
    )(start, counts, dest3)


def _decode_kernel(code_ref, tok_ref, row_ref, *, trash_code, y_stride):
    code = code_ref[...]
    nblk, blk = code.shape
    k_bits = TOP_K.bit_length() - 1
    assert TOP_K == 1 << k_bits
    blk_i = lax.broadcasted_iota(I32, (nblk, blk), 0)
    parity = jnp.where(blk_i == nblk - 1, 1, blk_i & 1)
    trash_row = TOP_K * y_stride + parity * blk + lax.broadcasted_iota(I32, (nblk, blk), 1)
    pad = code >= trash_code
    tok = code >> k_bits
    tok_ref[...] = jnp.where(pad, 0, tok)
    row_ref[...] = jnp.where(pad, trash_row, (code & (TOP_K - 1)) * y_stride + tok)


def _decode(codes, trash_code, y_stride):
    shp = jax.ShapeDtypeStruct(codes.shape, I32)
    return pl.pallas_call(
        functools.partial(_decode_kernel, trash_code=trash_code, y_stride=y_stride),
        out_shape=[shp, shp],
        compiler_params=pltpu.CompilerParams(vmem_limit_bytes=VMEM_LIMIT),
        name="decode",
    )(codes)


def _moe_kernel(be_ref, nact_ref, tcur_ref, tnext_ref, rprev_ref, rcur_ref, h_ref, wgu_ref, bgu_ref, wd_ref,
                bd_ref, y_ref, xbuf, xb_ref, obuf, gsem, ssem, wgu_bf, wd_bf, *, dff):
    i = pl.program_id(0)
    nact = nact_ref[0]
    blk = MOE_BLOCK

    def gather(t_ref, j):
        return pltpu.make_async_copy(h_ref.at[pl.ds(t_ref[0, 0, j], 1), :], xbuf.at[pl.ds(j, 1), :], gsem)

    def scatter(r_ref, j, slot):
        return pltpu.make_async_copy(obuf.at[slot, pl.ds(j, 1), :], y_ref.at[pl.ds(r_ref[0, 0, j], 1), :],
                                     ssem.at[slot])

    def wait_gathers():
        pltpu.make_async_copy(h_ref.at[pl.ds(0, blk), :], xbuf, gsem).wait()

    def wait_scatters(slot):
        pltpu.make_async_copy(obuf.at[slot], y_ref.at[pl.ds(0, blk), :], ssem.at[slot]).wait()

    @pl.when(i == 0)
    def _():
        obuf[...] = jnp.zeros_like(obuf)
        for j in range(blk):
            gather(tcur_ref, j).start()

    @pl.when(i < nact)
    def _():
        wait_gathers()

        @pl.when(i >= 1)
        def _():
            wait_scatters(i % 2)

        @pl.when((i == 0) | (be_ref[i] != be_ref[jnp.maximum(i - 1, 0)]))
        def _():
            wgu_bf[...] = wgu_ref[...].astype(BF16)
            wd_bf[...] = wd_ref[...].astype(BF16)

        xb_ref[...] = xbuf[...].astype(BF16)
        for j in range(blk):
            gather(tnext_ref, j).start()
        for j in range(blk):
            scatter(rprev_ref, j, (i + 1) % 2).start()
        gu = jnp.dot(xb_ref[...], wgu_bf[...], preferred_element_type=F32) + bgu_ref[...]
        gate = jnp.minimum(gu[:, :dff], SWIGLU_LIMIT)
        up = jnp.clip(gu[:, dff:], -SWIGLU_LIMIT, SWIGLU_LIMIT)
        act = (up + 1.0) * gate * _sigmoid(gate * SWIGLU_ALPHA)
        obuf[i % 2] = jnp.dot(act.astype(BF16), wd_bf[...], preferred_element_type=F32) + bd_ref[...]

        @pl.when(i == nact - 1)
        def _():
            for j in range(blk):
                scatter(rcur_ref, j, i % 2).start()
            wait_gathers()
            wait_scatters(0)
            wait_scatters(1)


def _moe(blk_e, nact, slot_map, tok_buf, w_gu, b_gu, w_down, b_down, n_blocks, y_stride):
    ne, d, dff2 = w_gu.shape
    dff = dff2 // 2
    blk = MOE_BLOCK
    trash_code = TOP_K * y_stride
    codes = jnp.concatenate([slot_map[:n_blocks * blk].reshape(n_blocks, blk), jnp.full((1, blk), trash_code, I32)])
    tok_tab, row_tab = (a.reshape(n_blocks + 1, 1, blk) for a in _decode(codes, trash_code, y_stride))
    smem = lambda f: pl.BlockSpec((1, 1, blk), f, memory_space=pltpu.SMEM)
    nxt = lambda i, be, na: (jnp.where(i + 1 < na[0], i + 1, jnp.minimum(i, na[0] - 1)), 0, 0)
    grid_spec = pltpu.PrefetchScalarGridSpec(
        num_scalar_prefetch=2,
        grid=(n_blocks,),
        in_specs=[smem(lambda i, be, na: (i, 0, 0)), smem(nxt),
                  smem(lambda i, be, na: (jnp.where(i == 0, n_blocks, i - 1), 0, 0)),
                  smem(lambda i, be, na: (i, 0, 0)),
                  pl.BlockSpec(memory_space=pl.ANY),
                  pl.BlockSpec((None, d, dff2), lambda i, be, na: (be[i], 0, 0)),
                  pl.BlockSpec((None, 1, dff2), lambda i, be, na: (be[i], 0, 0)),
                  pl.BlockSpec((None, dff, d), lambda i, be, na: (be[i], 0, 0)),
                  pl.BlockSpec((None, 1, d), lambda i, be, na: (be[i], 0, 0))],
        out_specs=pl.BlockSpec(memory_space=pl.ANY),
        scratch_shapes=[pltpu.VMEM((blk, d), F32), pltpu.VMEM((blk, d), BF16), pltpu.VMEM((2, blk, d), F32),
                        pltpu.SemaphoreType.DMA(()), pltpu.SemaphoreType.DMA((2,)),
                        pltpu.VMEM((d, dff2), BF16), pltpu.VMEM((dff, d), BF16)],
    )
    return pl.pallas_call(
        functools.partial(_moe_kernel, dff=dff),
        grid_spec=grid_spec,
        out_shape=jax.ShapeDtypeStruct((TOP_K * y_stride + 2 * blk, d), F32),
        compiler_params=_cparams("arbitrary"),
        name="moe",
    )(blk_e, nact, tok_tab, tok_tab, row_tab, row_tab, tok_buf, w_gu, b_gu.reshape(ne, 1, dff2), w_down,
      b_down.reshape(ne, 1, d))


def _combine_kernel(y0_ref, y1_ref, y2_ref, y3_ref, w_ref, x1_ref, gt_ref, o_ref, *, rows):
    w = w_ref[...]
    y = w[:rows, 0:1] * y0_ref[...]
    for kk, y_ref in enumerate((y1_ref, y2_ref, y3_ref), start=1):
        y = y + w[:rows, kk:kk + 1] * y_ref[...]
    o_ref[...] = x1_ref[...] + gt_ref[...] * y


def _combine(ys, wt3, x1, mod3, tok_rows, tile0, rows):
    nb, t, d = x1.shape
    rmod = mod3.shape[1]
    nt = t // rows
    row0 = tile0 * TOK_TILE
    assert tok_rows % rows == 0 and row0 % rows == 0
    row = lambda b, i: (b, i, 0)
    gate = pl.BlockSpec((None, rmod, d), (lambda b, i: (b, 0, 5)) if rmod == 1 else (lambda b, i: (b, i, 5)))
    expert_out = lambda kk: pl.BlockSpec((rows, d), lambda b, i: ((kk * tok_rows + row0) // rows + b * nt + i, 0))
    return pl.pallas_call(
        functools.partial(_combine_kernel, rows=rows),
        grid=(nb, nt),
        in_specs=[expert_out(kk) for kk in range(TOP_K)]
                 + [pl.BlockSpec((None, TOK_TILE, TOP_K), lambda b, i: (tile0 + (b * nt + i) * rows // TOK_TILE, 0, 0)),
                    pl.BlockSpec((None, rows, d), row), gate],
        out_specs=pl.BlockSpec((None, rows, d), row),
        out_shape=jax.ShapeDtypeStruct((nb, t, d), F32),
        compiler_params=_cparams("parallel", "parallel"),
        name="combine",
    )(ys, ys, ys, ys, wt3, x1, mod3)


def _rope_tables(pos, n_heads):
    half = HEAD_DIM // 2
    inv_freq = 1.0 / (ROPE_THETA ** (jnp.arange(0, HEAD_DIM, 2, dtype=F32) / HEAD_DIM))
    ang = pos.astype(F32)[:, None] * inv_freq[None, :]
    cos, sin = jnp.cos(ang), jnp.sin(ang)
    del half
    return (jnp.tile(jnp.concatenate([cos, cos], axis=-1), (1, n_heads)),
            jnp.tile(jnp.concatenate([-sin, sin], axis=-1), (1, n_heads)))


def _block_diag(width, value):
    h = np.arange(width) // HEAD_DIM
    return jnp.asarray(np.where(h[:, None] == h[None, :], value, 0.0), F32)


def kernel(x_prompt, x_sample, cache_k, cache_v, state_wkv, state_shift, c_prompt, c_sample, w_ada, b_ada, norm1_g, norm2_g, w_in, q_norm_g, k_norm_g, rwkv_mu, rwkv_w0, rwkv_w2, rwkv_a0, rwkv_a2, rwkv_g2, rwkv_k_k, rwkv_k_a, rwkv_r_k, rwkv_ln_w, rwkv_ln_b, w_out, router_w, router_b, moe_w_gu, moe_b_gu, moe_w_down, moe_b_down):
    nbp, t, d = x_prompt.shape
    nbs, ts, _ = x_sample.shape
    depth = w_ada.shape[0]
    assert depth == 1 and ts == 1
    n_heads = cache_k.shape[3]
    aw = n_heads * HEAD_DIM
    rwid = rwkv_w0.shape[1]
    rw_heads = rwid // HEAD_DIM
    rwc = rwkv_mu.shape[1]
    past = cache_k.shape[2]
    ne = router_w.shape[2]
    keep = min(MAX_WINDOW, t)
    lyr = 0

    w_in_bf = w_in[lyr].astype(BF16)
    wt_bf = w_out[lyr][:aw].astype(BF16)
    wb_bf = w_out[lyr][aw:].astype(BF16)
    g1 = norm1_g[lyr].reshape(1, d)
    g2 = norm2_g[lyr].reshape(1, d)
    qg = jnp.tile(q_norm_g[lyr], n_heads).reshape(1, aw)
    kg = jnp.tile(k_norm_g[lyr], n_heads).reshape(1, aw)
    bd_mean_a = _block_diag(aw, 1.0 / HEAD_DIM)
    bd_mean_r = _block_diag(rwid, 1.0 / HEAD_DIM)
    bd_ones_r = _block_diag(rwid, 1.0)
    dl = rwkv_w2.shape[1]
    w2p = jnp.zeros((LANES, rwid), F32).at[:dl].set(rwkv_w2[lyr])
    a2p = jnp.zeros((LANES, rwid), F32).at[dl:dl + rwkv_a2.shape[1]].set(rwkv_a2[lyr])
    vec = lambda a: a[lyr].reshape(1, -1)
    rwt = router_w[lyr].T
    rb = router_b[lyr].reshape(ne, 1)

    rows_c = nbp + nbs
    rows_pad = -(-rows_c // 8) * 8
    c_all = jnp.zeros((rows_pad, d), F32).at[:nbp].set(c_prompt).at[nbp:rows_c].set(c_sample)
    mod = _ada(c_all, w_ada[lyr], b_ada[lyr])
    mod_p = mod[:nbp].reshape(nbp, 1, 6 * d)
    mod_s = mod[nbp:rows_c].reshape(1, nbs, 6 * d)
    xs3 = x_sample.reshape(1, nbs, d)

    cos_p, sin_p = _rope_tables(jnp.arange(t), n_heads)
    cos_s, sin_s = _rope_tables(jnp.full((nbs,), PAST_LEN), n_heads)
    qp, kp, vp, rwp, kt_p, vt_p = _inproj(x_prompt, mod_p, g1, w_in_bf, cos_p, sin_p, qg, kg, bd_mean_a,
                                          ROW_TILE, keep=keep)
    qs, ks, vs, rws = _inproj(xs3, mod_s, g1, w_in[lyr], cos_s, sin_s, qg, kg, bd_mean_a, nbs)

    attn_p = _attn_prompt(qp, kp, vp)
    as_rows = lambda a: a.reshape(nbs, 1, -1)
    cache_t = lambda cch: jnp.transpose(cch[lyr], (0, 2, 3, 1)).reshape(nbs, aw, past)
    attn_s = _attn_sample(as_rows(qs), as_rows(ks), as_rows(vs), cache_t(cache_k), cache_t(cache_v))

    rw_args = (vec(rwkv_mu), vec(rwkv_w0), w2p, vec(rwkv_a0), a2p, rwkv_g2[lyr], vec(rwkv_k_k), vec(rwkv_k_a), bd_ones_r)
    pre_p = _rwprep(rwp, jnp.zeros((nbp, 1, rwc), F32), *rw_args, tm=ROW_TILE, whole_prev=False)
    pre_s = _rwprep(rws, state_shift[lyr].reshape(1, nbs, rwc), *rw_args, tm=nbs, whole_prev=True)
    r_p, ld_p, k_p, v_p, al_p, be_p, g_p = pre_p
    y_p, wkv_p = _rwchunk(r_p, ld_p, k_p, v_p, al_p, be_p)
    r_s, ld_s, k_s, v_s, al_s, be_s, g_s = pre_s
    y_s, wkv_s = _rwstep(*(as_rows(a) for a in (r_s, ld_s, k_s, v_s, al_s, be_s)), state_wkv[lyr])
    y_s = y_s.reshape(1, nbs, rwid)

    op_args = (rwkv_r_k[lyr].reshape(1, rwid), vec(rwkv_ln_w), vec(rwkv_ln_b), bd_mean_r, bd_ones_r, rwt, rb)
    n_p = nbp * t
    n_valid = n_p + nbs
    assert n_p % ROUTE_TILE == 0 and ROUTE_TILE % TOK_TILE == 0 and n_p % nbs == 0
    npad = -(-n_valid // ROUTE_TILE) * ROUTE_TILE
    x1_p, tok_buf, lg_p = _outproj(x_prompt, attn_p, y_p, r_p, k_p, v_p, g_p, mod_p, g2, wt_bf, wb_bf, *op_args,
                                   tm=ROW_TILE, tok_rows=npad)
    x1_s, tok_buf, lg_s = _outproj(xs3, attn_s.reshape(1, nbs, aw), y_s, r_s, k_s, v_s, g_s, mod_s, g2,
                                   w_out[lyr][:aw], w_out[lyr][aw:], *op_args, tm=nbs, tok_rows=npad,
                                   tok_row0=n_p, tok_buf=tok_buf)

    logits_t = jnp.zeros((ne, npad), F32).at[:, :n_p].set(lg_p).at[:, n_p:n_valid].set(lg_s)
    idx, wts, rank, cnt = _route(logits_t, n_valid)
    counts = cnt[:, 0].astype(I32)
    blk = MOE_BLOCK
    n_blocks = -(-(n_valid * TOP_K + ne * (blk - 1)) // blk)
    padded = (counts + blk - 1) // blk * blk
    pad_end = jnp.cumsum(padded)
    start = pad_end - padded
    blk_e = jnp.minimum(jnp.sum(pad_end[None, :] <= (jnp.arange(n_blocks) * blk)[:, None], axis=1), ne - 1).astype(I32)
    nact = (pad_end[-1] // blk).reshape(1).astype(I32)
    trash = n_blocks * blk
    dest = _dest(idx, rank, start.astype(I32), n_valid, trash)
    n_tiles = npad // TOK_TILE
    dest3 = dest.reshape(TOP_K, n_tiles, TOK_TILE).transpose(1, 0, 2)
    wt3 = wts.reshape(TOP_K, n_tiles, TOK_TILE).transpose(1, 2, 0)
    slot_map = _slotmap(start.astype(I32), counts, dest3, trash + 8, npad * TOP_K)
    ys = _moe(blk_e, nact, slot_map, tok_buf, moe_w_gu[lyr], moe_b_gu[lyr], moe_w_down[lyr], moe_b_down[lyr],
              n_blocks, npad)
    y_prompt = _combine(ys, wt3, x1_p, mod_p, npad, 0, TOK_TILE)
    y_sample = _combine(ys, wt3, x1_s, mod_s, npad, n_p // TOK_TILE, nbs)

    kept = lambda a: jnp.transpose(a.reshape(nbp, n_heads, HEAD_DIM, keep), (0, 3, 1, 2))[None]
    return (y_prompt, y_sample.reshape(nbs, ts, d), kept(kt_p), kept(vt_p), wkv_p[None], rwp[:, t - 1][None],
            ks.reshape(nbs, ts, n_heads, HEAD_DIM)[None], vs.reshape(nbs, ts, n_heads, HEAD_DIM)[None],
            wkv_s[None], rws.reshape(nbs, rwc)[None])
```

```python
import functools

import numpy as np
import jax
import jax.numpy as jnp
from jax import lax
from jax.experimental import pallas as pl
from jax.experimental.pallas import tpu as pltpu

F32 = jnp.float32
BF16 = jnp.bfloat16
I32 = jnp.int32

HEAD_DIM = 64
LANES = 128
DILATED_PATTERNS = ((128, 1), (512, 4), (2048, 16))
WINDOW_STEPS = 128
MAX_WINDOW = 2048
PAST_LEN = 16384
ROPE_THETA = 10000.0
NORM_EPS = 1e-6
GN_EPS = 64e-5
TOP_K = 4
SWIGLU_ALPHA = 1.702
SWIGLU_LIMIT = 7.0
RW_CHUNK = 128
MOE_BLOCK = 256
ATTN_UNITS = 4
ROW_TILE = 512
TOK_TILE = 256
ROUTE_TILE = 512
VMEM_LIMIT = 56 * 1024 * 1024
NEG_BIG = -1e30


def _cparams(*sem):
    return pltpu.CompilerParams(dimension_semantics=sem, vmem_limit_bytes=VMEM_LIMIT)


def _dot(a, b):
    return jnp.dot(a.astype(BF16), b.astype(BF16), preferred_element_type=F32)


def _dot_nt(a, b):
    return lax.dot_general(a.astype(BF16), b.astype(BF16), (((1,), (1,)), ((), ())),
                           preferred_element_type=F32)


def _split2(a):
    hi = a.astype(BF16)
    lo = (a - hi.astype(F32)).astype(BF16)
    return hi, lo


def _split3(a):
    hi = a.astype(BF16)
    r1 = a - hi.astype(F32)
    mid = r1.astype(BF16)
    lo = (r1 - mid.astype(F32)).astype(BF16)
    return hi, mid, lo


def _dot_x(a, e):
    e = e.astype(BF16)
    hi, mid, lo = _split3(a)
    return (jnp.dot(hi, e, preferred_element_type=F32) + jnp.dot(mid, e, preferred_element_type=F32)
            + jnp.dot(lo, e, preferred_element_type=F32))


def _xdot(e, a):
    e = e.astype(BF16)
    hi, mid, lo = _split3(a)
    return (jnp.dot(e, hi, preferred_element_type=F32) + jnp.dot(e, mid, preferred_element_type=F32)
            + jnp.dot(e, lo, preferred_element_type=F32))


def _dot3(a, b):
    ah, al = _split2(a)
    bh, bl = _split2(b)
    return (jnp.dot(ah, bh, preferred_element_type=F32) + jnp.dot(ah, bl, preferred_element_type=F32)
            + jnp.dot(al, bh, preferred_element_type=F32))


def _dot3_nt(a, b):
    ah, al = _split2(a)
    bh, bl = _split2(b)
    dn = (((1,), (1,)), ((), ()))
    return (lax.dot_general(ah, bh, dn, preferred_element_type=F32)
            + lax.dot_general(ah, bl, dn, preferred_element_type=F32)
            + lax.dot_general(al, bh, dn, preferred_element_type=F32))


def _mm(a, w):
    if w.dtype == BF16:
        return jnp.dot(a.astype(BF16), w, preferred_element_type=F32)
    return _dot3(a, w)


def _sigmoid(x):
    return 1.0 / (1.0 + jnp.exp(-x))


def _ada_kernel(c_ref, w_ref, b_ref, o_ref):
    c = c_ref[...]
    o_ref[...] = _dot3(c * _sigmoid(c), w_ref[...]) + b_ref[...]


def _ada(c_all, w_ada, b_ada):
    rows, d = c_all.shape
    n = w_ada.shape[1]
    tn = n // 4
    return pl.pallas_call(
        _ada_kernel,
        grid=(n // tn,),
        in_specs=[pl.BlockSpec((rows, d), lambda j: (0, 0)),
                  pl.BlockSpec((d, tn), lambda j: (0, j)),
                  pl.BlockSpec((1, tn), lambda j: (0, j))],
        out_specs=pl.BlockSpec((rows, tn), lambda j: (0, j)),
        out_shape=jax.ShapeDtypeStruct((rows, n), F32),
        compiler_params=_cparams("arbitrary"),
        name="ada",
    )(c_all, w_ada, b_ada.reshape(1, n))


def _inproj_kernel(x_ref, sh_ref, sc_ref, g_ref, w_ref, cos_ref, sin_ref, qg_ref, kg_ref, bd_ref,
                   q_ref, k_ref, v_ref, rw_ref, *maybe_kv_t, aw, first_kept):
    x = x_ref[...]
    ms = jnp.mean(x * x, axis=-1, keepdims=True)
    h = x * lax.rsqrt(ms + NORM_EPS) * g_ref[...] * (1.0 + sc_ref[...]) + sh_ref[...]
    proj = _mm(h, w_ref[...])
    cos = cos_ref[...]
    sin = sin_ref[...]
    lane = lax.broadcasted_iota(I32, (1, aw), 1)
    first_half = (lane % HEAD_DIM) < (HEAD_DIM // 2)
    bd = bd_ref[...]

    def norm_rope(t, g):
        tn = t * lax.rsqrt((_dot if w_ref.dtype == BF16 else _dot_x)(t * t, bd) + NORM_EPS) * g
        rot = jnp.where(first_half, pltpu.roll(tn, aw - HEAD_DIM // 2, 1), pltpu.roll(tn, HEAD_DIM // 2, 1))
        return tn * cos + rot * sin

    scale = 1.0 / np.sqrt(HEAD_DIM).astype(np.float32)
    q_ref[...] = norm_rope(proj[:, :aw], qg_ref[...]) * scale
    k = norm_rope(proj[:, aw:2 * aw], kg_ref[...])
    v = proj[:, 2 * aw:3 * aw]
    k_ref[...] = k
    v_ref[...] = v
    rw_ref[...] = proj[:, 3 * aw:]
    if maybe_kv_t:
        kt_ref, vt_ref = maybe_kv_t
        kept = pl.program_id(1) >= first_kept

        @pl.when(kept)
        def _():
            kt_ref[...] = k.T
            vt_ref[...] = v.T

        @pl.when(jnp.logical_not(kept))
        def _():
            kt_ref[...] = jnp.zeros_like(kt_ref)
            vt_ref[...] = jnp.zeros_like(vt_ref)


def _inproj(x3, mod3, g1, w_in_bf, cos_t, sin_t, qg, kg, bd_mean, tm, keep=0):
    nb, t, d = x3.shape
    r = mod3.shape[1]
    ncol = w_in_bf.shape[1]
    aw = cos_t.shape[1]
    rwc = ncol - 3 * aw
    grid = (nb, t // tm)
    row = lambda b, i: (b, i, 0)
    const = lambda b, i: (0, 0)
    mod_spec = lambda s: pl.BlockSpec((None, r, d), (lambda b, i: (b, 0, s)) if r == 1 else (lambda b, i: (b, i, s)))
    out_specs = [pl.BlockSpec((None, tm, aw), row)] * 3 + [pl.BlockSpec((None, tm, rwc), row)]
    out_shape = [jax.ShapeDtypeStruct((nb, t, aw), F32)] * 3 + [jax.ShapeDtypeStruct((nb, t, rwc), F32)]
    first_kept = (t - keep) // tm
    if keep:
        assert keep % tm == 0 and (t - keep) % tm == 0
        kept_spec = pl.BlockSpec((None, aw, tm), lambda b, i: (b, 0, jnp.maximum(i - first_kept, 0)))
        out_specs += [kept_spec, kept_spec]
        out_shape += [jax.ShapeDtypeStruct((nb, aw, keep), F32)] * 2
    outs = pl.pallas_call(
        functools.partial(_inproj_kernel, aw=aw, first_kept=first_kept),
        grid=grid,
        in_specs=[pl.BlockSpec((None, tm, d), row), mod_spec(0), mod_spec(1),
                  pl.BlockSpec((1, d), const), pl.BlockSpec((d, ncol), const),
                  pl.BlockSpec((tm, aw), lambda b, i: (i, 0)), pl.BlockSpec((tm, aw), lambda b, i: (i, 0)),
                  pl.BlockSpec((1, aw), const), pl.BlockSpec((1, aw), const), pl.BlockSpec((aw, aw), const)],
        out_specs=out_specs,
        out_shape=out_shape,
        compiler_params=_cparams("parallel", "arbitrary"),
        name="inproj",
    )(x3, mod3, mod3, g1, w_in_bf, cos_t, sin_t, qg, kg, bd_mean)
    return outs


def _attn_prompt_kernel(q_ref, k_ref, v_ref, o_ref, kp_ref, vp_ref, m_ref, l_ref, acc_ref, *, t, pad):
    nq = WINDOW_STEPS
    nk = 2 * WINDOW_STEPS
    kp_ref[pl.ds(0, pad), :] = jnp.zeros((pad, LANES), F32)
    vp_ref[pl.ds(0, pad), :] = jnp.zeros((pad, LANES), F32)
    kp_ref[pl.ds(pad, t), :] = k_ref[...]
    vp_ref[pl.ds(pad, t), :] = v_ref[...]
    head0 = lax.broadcasted_iota(I32, (1, LANES), 1) < HEAD_DIM
    qi = lax.broadcasted_iota(I32, (nq, nk), 0)
    kj = lax.broadcasted_iota(I32, (nq, nk), 1)
    steps_back = qi + nq - kj
    band = (steps_back >= 0) & (steps_back <= WINDOW_STEPS)
    has_past = kj >= nq

    hsels = (head0, jnp.logical_not(head0))
    nt = (((1,), (1,)), ((), ()))

    for p, (_, d) in enumerate(DILATED_PATTERNS):
        def units(g, carry, p=p, d=d):
            rows_q, kb, vb, valid, qh = [], [], [], [], []
            for j in range(ATTN_UNITS):
                u = g * ATTN_UNITS + j
                res = u % d
                blk = u // d
                q_start = res + d * nq * blk
                k_start = pad + q_start - d * nq
                if d == 1:
                    rows_q.append(pl.ds(q_start, nq))
                    rows_k = pl.ds(k_start, nk)
                else:
                    rows_q.append(pl.ds(q_start, nq, stride=d))
                    rows_k = pl.ds(k_start, nk, stride=d)
                q = q_ref[rows_q[j], :]
                kb.append(kp_ref[rows_k, :].astype(BF16))
                vb.append(vp_ref[rows_k, :].astype(BF16))
                valid.append(band & (has_past | (blk > 0)))
                qh.append([jnp.where(hsel, q, 0.0).astype(BF16) for hsel in hsels])
            chains = [(j, h) for j in range(ATTN_UNITS) for h in range(2)]
            s = [lax.dot_general(qh[j][h], kb[j], nt, preferred_element_type=F32) for j, h in chains]
            s = [jnp.where(valid[j], sc, NEG_BIG) for (j, h), sc in zip(chains, s)]
            mx = [jnp.max(sc, axis=-1, keepdims=True) for sc in s]
            e = [jnp.exp(sc - m) for sc, m in zip(s, mx)]
            den = [jnp.sum(ec, axis=-1, keepdims=True) for ec in e]
            o = [jnp.dot(ec.astype(BF16), vb[j], preferred_element_type=F32) for (j, h), ec in zip(chains, e)]
            for j in range(ATTN_UNITS):
                m_ref[p, rows_q[j], :] = jnp.where(head0, mx[2 * j], mx[2 * j + 1])
                l_ref[p, rows_q[j], :] = jnp.where(head0, den[2 * j], den[2 * j + 1])
                acc_ref[p, rows_q[j], :] = jnp.where(head0, o[2 * j], o[2 * j + 1])
            return carry

        lax.fori_loop(0, t // nq // ATTN_UNITS, units, 0)

    rows = 256

    def merge(i, carry):
        sl = pl.ds(pl.multiple_of(i * rows, rows), rows)
        m0, m1, m2 = m_ref[0, sl, :], m_ref[1, sl, :], m_ref[2, sl, :]
        mm = jnp.maximum(jnp.maximum(m0, m1), m2)
        w0, w1, w2 = jnp.exp(m0 - mm), jnp.exp(m1 - mm), jnp.exp(m2 - mm)
        num = w0 * acc_ref[0, sl, :] + w1 * acc_ref[1, sl, :] + w2 * acc_ref[2, sl, :]
        den = w0 * l_ref[0, sl, :] + w1 * l_ref[1, sl, :] + w2 * l_ref[2, sl, :]
        o_ref[sl, :] = num / den
        return carry

    lax.fori_loop(0, t // rows, merge, 0)


def _attn_prompt(q, k, v):
    nb, t, aw = q.shape
    pad = MAX_WINDOW
    assert t % MAX_WINDOW == 0
    spec = pl.BlockSpec((None, t, LANES), lambda b, hp: (b, 0, hp))
    return pl.pallas_call(
        functools.partial(_attn_prompt_kernel, t=t, pad=pad),
        grid=(nb, aw // LANES),
        in_specs=[spec, spec, spec],
        out_specs=spec,
        out_shape=jax.ShapeDtypeStruct((nb, t, aw), F32),
        scratch_shapes=[pltpu.VMEM((pad + t, LANES), F32), pltpu.VMEM((pad + t, LANES), F32),
                        pltpu.VMEM((3, t, LANES), F32), pltpu.VMEM((3, t, LANES), F32),
                        pltpu.VMEM((3, t, LANES), F32)],
        compiler_params=_cparams("parallel", "parallel"),
        name="attn_prompt",
    )(q, k, v)


def _attn_sample_kernel(q_ref, kn_ref, vn_ref, kt_ref, vt_ref, o_ref, *, nh, w):
    aw = nh * HEAD_DIM
    dist = w - lax.broadcasted_iota(I32, (1, w), 1)
    mult = jnp.zeros((1, w), F32)
    for win, d in DILATED_PATTERNS:
        mult = mult + jnp.where((dist % d == 0) & (dist <= win), 1.0, 0.0)
    n_pat = float(len(DILATED_PATTERNS))
    q_col = _col(q_ref[...], aw)
    kn_col = _col(kn_ref[...], aw)
    vn_col = _col(vn_ref[...], aw)
    outs = []
    for h in range(nh):
        hs = pl.ds(h * HEAD_DIM, HEAD_DIM)
        qh = q_col[h * HEAD_DIM:(h + 1) * HEAD_DIM]
        s = jnp.sum(kt_ref[hs, :] * qh, axis=0, keepdims=True)
        s_self = jnp.sum(qh * kn_col[h * HEAD_DIM:(h + 1) * HEAD_DIM], axis=0, keepdims=True)
        s = jnp.where(mult > 0.0, s, NEG_BIG)
        mx = jnp.maximum(jnp.max(s, axis=1, keepdims=True), s_self)
        pr = mult * jnp.exp(s - mx)
        p_self = n_pat * jnp.exp(s_self - mx)
        den = jnp.sum(pr, axis=1, keepdims=True) + p_self
        num = jnp.sum(vt_ref[hs, :] * pr, axis=1, keepdims=True) + p_self * vn_col[h * HEAD_DIM:(h + 1) * HEAD_DIM]
        outs.append(num / den)
    o_ref[...] = _row(jnp.concatenate(outs, axis=0), aw)


def _attn_sample(q, kn, vn, cache_kt, cache_vt):
    nb, _, aw = q.shape
    w = cache_kt.shape[2]
    assert w == MAX_WINDOW
    row = pl.BlockSpec((None, 1, aw), lambda i: (i, 0, 0))
    mat = pl.BlockSpec((None, aw, w), lambda i: (i, 0, 0))
    return pl.pallas_call(
        functools.partial(_attn_sample_kernel, nh=aw // HEAD_DIM, w=w),
        grid=(nb,),
        in_specs=[row, row, row, mat, mat],
        out_specs=row,
        out_shape=jax.ShapeDtypeStruct((nb, 1, aw), F32),
        compiler_params=_cparams("parallel"),
        name="attn_sample",
    )(q, kn, vn, cache_kt, cache_vt)


def _rwprep_kernel(p_ref, prev_ref, first_ref, mu_ref, w0_ref, w2_ref, a0_ref, a2_ref, g2_ref, kk_ref, ka_ref,
                   bd_ref, r_ref, ld_ref, k_ref, v_ref, al_ref, be_ref, g_ref, *, rwid, whole_prev):
    p = p_ref[...]
    if whole_prev:
        prev = first_ref[...]
    else:
        tm = p.shape[0]
        before = jnp.where(pl.program_id(1) == 0, first_ref[...], prev_ref[7:8, :])
        rowi = lax.broadcasted_iota(I32, (tm, 1), 0)
        prev = jnp.where(rowi == 0, before, pltpu.roll(p, 1, 0))
    xs = p + mu_ref[...] * (prev - p)
    r = xs[:, :rwid]
    k = xs[:, rwid:2 * rwid]
    v = xs[:, 2 * rwid:3 * rwid]
    xwa = xs[:, 3 * rwid:3 * rwid + LANES]
    xg = xs[:, 3 * rwid + LANES:]
    z = w0_ref[...] + _dot3(jnp.tanh(xwa), w2_ref[...])
    softplus_neg = jnp.maximum(-z, 0.0) + jnp.log(1.0 + jnp.exp(-jnp.abs(z)))
    w = -softplus_neg - 0.5
    a = _sigmoid(a0_ref[...] + _dot3(xwa, a2_ref[...]))
    g = _dot3(_sigmoid(xg), g2_ref[...])
    kk = k * kk_ref[...]
    norm = jnp.sqrt(_dot_x(kk * kk, bd_ref[...]))
    kk = kk / jnp.maximum(norm, 1e-12)
    r_ref[...] = r
    ld_ref[...] = -jnp.exp(w)
    k_ref[...] = k * (1.0 + (a - 1.0) * ka_ref[...])
    v_ref[...] = v
    al_ref[...] = -kk
    be_ref[...] = kk * a
    g_ref[...] = g


def _rwprep(rw, first, mu, w0, w2p, a0, a2p, g2, k_k, k_a, bd_ones, tm, whole_prev):
    nb, t, rwc = rw.shape
    rwid = w0.shape[1]
    grid = (nb, t // tm)
    row = lambda b, i: (b, i, 0)
    const = lambda b, i: (0, 0)
    if whole_prev:
        prev_spec = pl.BlockSpec((None, tm, rwc), row)
        first_spec = pl.BlockSpec((None, tm, rwc), row)
    else:
        prev_spec = pl.BlockSpec((None, 8, rwc), lambda b, i: (b, jnp.maximum(i * (tm // 8) - 1, 0), 0))
        first_spec = pl.BlockSpec((None, 1, rwc), lambda b, i: (b, 0, 0))
    vec = pl.BlockSpec((1, rwid), const)
    out = pl.BlockSpec((None, tm, rwid), row)
    return pl.pallas_call(
        functools.partial(_rwprep_kernel, rwid=rwid, whole_prev=whole_prev),
        grid=grid,
        in_specs=[pl.BlockSpec((None, tm, rwc), row), prev_spec, first_spec,
                  pl.BlockSpec((1, rwc), const), vec, pl.BlockSpec((LANES, rwid), const),
                  vec, pl.BlockSpec((LANES, rwid), const), pl.BlockSpec((LANES, rwid), const), vec, vec,
                  pl.BlockSpec((rwid, rwid), const)],
        out_specs=[out] * 7,
        out_shape=[jax.ShapeDtypeStruct((nb, t, rwid), F32)] * 7,
        compiler_params=_cparams("parallel", "parallel"),
        name="rwprep",
    )(rw, rw, first, mu, w0, w2p, a0, a2p, g2, k_k, k_a, bd_ones)


def _rwchunk_kernel(r_ref, ld_ref, k_ref, v_ref, al_ref, be_ref, y_ref, st_ref, z_ref, *, nb):
    c = RW_CHUNK
    ci = pl.program_id(1)

    @pl.when(ci == 0)
    def _():
        z_ref[...] = jnp.zeros_like(z_ref)

    ti = lax.broadcasted_iota(I32, (c, c), 0)
    si = lax.broadcasted_iota(I32, (c, c), 1)
    low_incl = si <= ti
    low_strict = si < ti
    diag = si == ti
    tri = jnp.where(low_incl, 1.0, 0.0).astype(BF16)
    eye = jnp.where(diag, 1.0, 0.0)
    head0 = lax.broadcasted_iota(I32, (1, LANES), 1) < HEAD_DIM
    hsels = (head0, jnp.logical_not(head0))
    same_head = (ti < HEAD_DIM) == (si < HEAD_DIM)
    nt = (((1,), (1,)), ((), ()))
    batches = range(nb)
    chains = [(b, h) for b in batches for h in range(2)]

    cum = [_xdot(tri, ld_ref[b]) for b in batches]
    tot = [cm[c - 1:c, :] for cm in cum]
    e_neg = [jnp.exp(-cm) for cm in cum]
    at = [al_ref[b] * jnp.exp(cum[b] - ld_ref[b]) for b in batches]
    rt = [r_ref[b] * jnp.exp(cum[b]) for b in batches]
    rhs_t = [jnp.concatenate([be_ref[b] * e_neg[b], k_ref[b] * e_neg[b]], axis=0).astype(BF16) for b in batches]
    vb = [v_ref[b].astype(BF16) for b in batches]
    at_h = [jnp.where(hsels[h], at[b], 0.0) for b, h in chains]
    rt_h = [jnp.where(hsels[h], rt[b], 0.0) for b, h in chains]
    a4 = [lax.dot_general(jnp.concatenate([a, r], axis=0).astype(BF16), rhs_t[b], nt, preferred_element_type=F32)
          for (b, h), a, r in zip(chains, at_h, rt_h)]
    a_ab = [jnp.where(low_strict, m[:c, :c], 0.0) for m in a4]
    a_ak = [jnp.where(low_strict, m[:c, c:], 0.0).astype(BF16) for m in a4]
    a_r = [jnp.concatenate([jnp.where(low_incl, m[c:, :c], 0.0), jnp.where(low_incl, m[c:, c:], 0.0)],
                           axis=1).astype(BF16) for m in a4]
    pw = [m.astype(BF16) for m in a_ab]
    inv = [eye + m for m in a_ab]
    for _ in range(int(np.log2(c)) - 1):
        pw = [jnp.dot(m, m, preferred_element_type=F32).astype(BF16) for m in pw]
        inv = [i + jnp.dot(i.astype(BF16), m, preferred_element_type=F32) for i, m in zip(inv, pw)]
    akv = [jnp.dot(m, vb[b], preferred_element_type=F32) for (b, h), m in zip(chains, a_ak)]
    x = [_dot(i, jnp.concatenate([kv, a], axis=1)) for i, kv, a in zip(inv, akv, at_h)]
    u0_h = [m[:, :LANES] for m in x]
    at2_h = [m[:, LANES:] for m in x]
    y0_h = [jnp.dot(ar, jnp.concatenate([u0, v_ref[b]], axis=0).astype(BF16), preferred_element_type=F32)
            for (b, h), ar, u0 in zip(chains, a_r, u0_h)]
    rt2_h = [r + jnp.dot(ar[:, :c], a2.astype(BF16), preferred_element_type=F32)
             for r, ar, a2 in zip(rt_h, a_r, at2_h)]
    z = [z_ref[b] for b in batches]
    uy = [_dot(jnp.concatenate([at2_h[2 * b] + at2_h[2 * b + 1], rt2_h[2 * b] + rt2_h[2 * b + 1]], axis=0), z[b])
          for b in batches]
    u = [uy[b][:c] + jnp.where(head0, u0_h[2 * b], u0_h[2 * b + 1]) for b in batches]
    for b in batches:
        y_ref[b] = uy[b][c:] + jnp.where(head0, y0_h[2 * b], y0_h[2 * b + 1])
    e_end = [jnp.exp(tot[b] - cum[b]) for b in batches]
    lhs_t = [jnp.concatenate([be_ref[b] * e_end[b], k_ref[b] * e_end[b]], axis=0) for b in batches]
    zadd = [_dot(lhs_t[b].T, jnp.concatenate([u[b], v_ref[b]], axis=0)) for b in batches]
    for b in batches:
        dcol = jnp.sum(jnp.where(diag, jnp.broadcast_to(jnp.exp(tot[b]), (c, c)), 0.0), axis=1, keepdims=True)
        z_ref[b] = dcol * z[b] + jnp.where(same_head, zadd[b], 0.0)

    @pl.when(ci == pl.num_programs(1) - 1)
    def _():
        for b in batches:
            s = z_ref[b].T
            st_ref[b, 0] = s[:HEAD_DIM, :HEAD_DIM]
            st_ref[b, 1] = s[HEAD_DIM:, HEAD_DIM:]


def _rwchunk(r, ld, k, v, al, be):
    nb, t, rwid = r.shape
    c = RW_CHUNK
    assert t % c == 0 and c == LANES
    npair = rwid // LANES
    seq = pl.BlockSpec((nb, c, LANES), lambda hp, ci: (0, ci, hp))
    return pl.pallas_call(
        functools.partial(_rwchunk_kernel, nb=nb),
        grid=(npair, t // c),
        in_specs=[seq] * 6,
        out_specs=[seq, pl.BlockSpec((nb, 2, HEAD_DIM, HEAD_DIM), lambda hp, ci: (0, hp, 0, 0))],
        out_shape=[jax.ShapeDtypeStruct((nb, t, rwid), F32),
                   jax.ShapeDtypeStruct((nb, 2 * npair, HEAD_DIM, HEAD_DIM), F32)],
        scratch_shapes=[pltpu.VMEM((nb, LANES, LANES), F32)],
        compiler_params=_cparams("parallel", "arbitrary"),
        name="rwchunk",
    )(r, ld, k, v, al, be)


def _col(row, width):
    return jnp.broadcast_to(row, (LANES, width)).T[:, 0:1]


def _row(col, width):
    return jnp.broadcast_to(col, (width, LANES)).T[0:1, :]


def _rwstep_kernel(r_ref, ld_ref, k_ref, v_ref, al_ref, be_ref, s_ref, y_ref, so_ref, *, bb, nh):
    rwid = nh * HEAD_DIM

    def one(b, carry):
        r, dcy, k, al, be = (ref[b] for ref in (r_ref, ld_ref, k_ref, al_ref, be_ref))
        dcy = jnp.exp(dcy)
        v_col = _col(v_ref[b], rwid)
        ys = []
        for h in range(nh):
            hs = slice(h * HEAD_DIM, (h + 1) * HEAD_DIM)
            s = s_ref[b, h]
            sa = jnp.sum(s * al[:, hs], axis=1, keepdims=True)
            s = s * dcy[:, hs] + sa * be[:, hs] + v_col[hs, :] * k[:, hs]
            so_ref[b, h] = s
            ys.append(jnp.sum(s * r[:, hs], axis=1, keepdims=True))
        y_ref[b] = _row(jnp.concatenate(ys, axis=0), rwid)
        return carry

    lax.fori_loop(0, bb, one, 0)


def _rwstep(r, ld, k, v, al, be, state):
    nb, _, rwid = r.shape
    nh = state.shape[1]
    bb = 8
    row = pl.BlockSpec((bb, 1, rwid), lambda i: (i, 0, 0))
    st = pl.BlockSpec((bb, nh, HEAD_DIM, HEAD_DIM), lambda i: (i, 0, 0, 0))
    return pl.pallas_call(
        functools.partial(_rwstep_kernel, bb=bb, nh=nh),
        grid=(nb // bb,),
        in_specs=[row] * 6 + [st],
        out_specs=[row, st],
        out_shape=[jax.ShapeDtypeStruct((nb, 1, rwid), F32), jax.ShapeDtypeStruct(state.shape, F32)],
        compiler_params=_cparams("parallel"),
        name="rwstep",
    )(r, ld, k, v, al, be, state)


def _outproj_kernel(x_ref, at_ref, y_ref, r_ref, k_ref, v_ref, g_ref, gt_ref, sh_ref, sc_ref, g2_ref,
                    wt_ref, wb_ref, rk_ref, lw_ref, lb_ref, bdm_ref, bd1_ref, rwt_ref, rb_ref, *rest):
    x1_ref, h2_ref, lg_ref = rest[-3:]
    y = y_ref[...]
    bdm = bdm_ref[...]
    mean = _dot_x(y, bdm)
    yc = y - mean
    var = _dot_x(yc * yc, bdm)
    yn = yc * lax.rsqrt(var + GN_EPS) * lw_ref[...] + lb_ref[...]
    v = v_ref[...]
    bonus = _dot_x(r_ref[...] * k_ref[...] * rk_ref[...], bd1_ref[...]) * v
    rw = (yn + bonus) * g_ref[...]
    mix = _mm(at_ref[...], wt_ref[...]) + _mm(rw, wb_ref[...])
    x1 = x_ref[...] + gt_ref[...] * mix
    x1_ref[...] = x1
    ms = jnp.mean(x1 * x1, axis=-1, keepdims=True)
    h2 = x1 * lax.rsqrt(ms + NORM_EPS) * g2_ref[...] * (1.0 + sc_ref[...]) + sh_ref[...]
    h2_ref[...] = h2
    lg_ref[...] = _dot3_nt(rwt_ref[...], h2) + rb_ref[...]


def _outproj(x3, attn, y, r, k, v, g, mod3, g2, wt_bf, wb_bf, r_k, ln_w, ln_b, bd_mean, bd_ones, rwt, rb, tm,
             tok_rows, tok_row0=0, tok_buf=None):
    nb, t, d = x3.shape
    rmod = mod3.shape[1]
    aw = attn.shape[2]
    ne = rwt.shape[0]
    nt = t // tm
    assert tok_row0 % tm == 0
    grid = (nb, nt)
    row = lambda b, i: (b, i, 0)
    const = lambda b, i: (0, 0)
    mod_spec = lambda s: pl.BlockSpec((None, rmod, d), (lambda b, i: (b, 0, s)) if rmod == 1 else (lambda b, i: (b, i, s)))
    half = pl.BlockSpec((None, tm, aw), row)
    vec = pl.BlockSpec((1, aw), const)
    aliased = [] if tok_buf is None else [tok_buf]
    return pl.pallas_call(
        _outproj_kernel,
        grid=grid,
        in_specs=[pl.BlockSpec((None, tm, d), row)] + [half] * 6 + [mod_spec(2), mod_spec(3), mod_spec(4),
                  pl.BlockSpec((1, d), const), pl.BlockSpec((aw, d), const), pl.BlockSpec((aw, d), const),
                  vec, vec, vec, pl.BlockSpec((aw, aw), const), pl.BlockSpec((aw, aw), const),
                  pl.BlockSpec((ne, d), const), pl.BlockSpec((ne, 1), const)]
                 + [pl.BlockSpec(memory_space=pl.ANY)] * len(aliased),
        out_specs=[pl.BlockSpec((None, tm, d), row),
                   pl.BlockSpec((tm, d), lambda b, i: (tok_row0 // tm + b * nt + i, 0)),
                   pl.BlockSpec((ne, tm), lambda b, i: (0, b * nt + i))],
        out_shape=[jax.ShapeDtypeStruct((nb, t, d), F32), jax.ShapeDtypeStruct((tok_rows, d), F32),
                   jax.ShapeDtypeStruct((ne, nb * t), F32)],
        input_output_aliases={20: 1} if aliased else {},
        compiler_params=_cparams("parallel", "parallel"),
        name="outproj",
    )(x3, attn, y, r, k, v, g, mod3, mod3, mod3, g2, wt_bf, wb_bf, r_k, ln_w, ln_b, bd_mean, bd_ones, rwt, rb,
      *aliased)


def _route_kernel(lg_ref, idx_ref, wt_ref, rank_ref, cnt_ref, carry_ref, *, n_valid):
    i = pl.program_id(0)
    ne, tn = lg_ref.shape

    @pl.when(i == 0)
    def _():
        carry_ref[...] = jnp.zeros_like(carry_ref)

    lg = lg_ref[...]
    eidx = lax.broadcasted_iota(I32, (ne, tn), 0).astype(F32)
    tok = i * tn + lax.broadcasted_iota(I32, (1, tn), 1)
    live = tok < n_valid
    vals, hots = [], []
    for _ in range(TOP_K):
        mx = jnp.max(lg, axis=0, keepdims=True)
        pick = jnp.min(jnp.where(lg == mx, eidx, float(ne)), axis=0, keepdims=True)
        hot = eidx == pick
        vals.append(mx)
        hots.append(hot)
        idx_ref[pl.ds(len(vals) - 1, 1), :] = pick.astype(I32)
        lg = jnp.where(hot, -jnp.inf, lg)
    ex = [jnp.exp(vv - vals[0]) for vv in vals]
    den = ex[0] + ex[1] + ex[2] + ex[3]
    for kk in range(TOP_K):
        wt_ref[pl.ds(kk, 1), :] = ex[kk] / den
    hot_all = jnp.zeros((ne, tn), F32)
    for hot in hots:
        hot_all = hot_all + jnp.where(hot & live, 1.0, 0.0)
    ri = lax.broadcasted_iota(I32, (tn, tn), 0)
    cj = lax.broadcasted_iota(I32, (tn, tn), 1)
    upper = jnp.where(ri <= cj, 1.0, 0.0).astype(BF16)
    incl = jnp.dot(hot_all.astype(BF16), upper, preferred_element_type=F32)
    before = carry_ref[:, 0:1] + incl - hot_all
    for kk, hot in enumerate(hots):
        rank_ref[pl.ds(kk, 1), :] = jnp.sum(jnp.where(hot, before, 0.0), axis=0, keepdims=True).astype(I32)
    carry_ref[...] = carry_ref[...] + jnp.sum(hot_all, axis=1, keepdims=True)
    cnt_ref[...] = carry_ref[...]


def _route(logits_t, n_valid):
    ne, npad = logits_t.shape
    tn = ROUTE_TILE
    tile = lambda rows: pl.BlockSpec((rows, tn), lambda i: (0, i))
    return pl.pallas_call(
        functools.partial(_route_kernel, n_valid=n_valid),
        grid=(npad // tn,),
        in_specs=[tile(ne)],
        out_specs=[tile(TOP_K), tile(TOP_K), tile(TOP_K), pl.BlockSpec((ne, LANES), lambda i: (0, 0))],
        out_shape=[jax.ShapeDtypeStruct((TOP_K, npad), I32), jax.ShapeDtypeStruct((TOP_K, npad), F32),
                   jax.ShapeDtypeStruct((TOP_K, npad), I32), jax.ShapeDtypeStruct((ne, LANES), F32)],
        scratch_shapes=[pltpu.VMEM((ne, LANES), F32)],
        compiler_params=_cparams("arbitrary"),
        name="route",
    )(logits_t)


def _dest_kernel(idx_ref, rank_ref, start_ref, dest_ref, *, n_valid, trash):
    i = pl.program_id(0)
    ne = start_ref.shape[0]
    tn = idx_ref.shape[1]
    eidx = lax.broadcasted_iota(I32, (ne, tn), 0)
    tok = i * tn + lax.broadcasted_iota(I32, (1, tn), 1)
    start = start_ref[:, 0:1]
    for kk in range(TOP_K):
        base = jnp.sum(jnp.where(eidx == idx_ref[pl.ds(kk, 1), :], start, 0.0), axis=0, keepdims=True)
        dest_ref[pl.ds(kk, 1), :] = jnp.where(tok < n_valid, base.astype(I32) + rank_ref[pl.ds(kk, 1), :], trash)


def _dest(idx, rank, start, n_valid, trash):
    _, npad = idx.shape
    ne = start.shape[0]
    tn = ROUTE_TILE
    tile = pl.BlockSpec((TOP_K, tn), lambda i: (0, i))
    return pl.pallas_call(
        functools.partial(_dest_kernel, n_valid=n_valid, trash=trash),
        grid=(npad // tn,),
        in_specs=[tile, tile, pl.BlockSpec((ne, LANES), lambda i: (0, 0))],
        out_specs=tile,
        out_shape=jax.ShapeDtypeStruct((TOP_K, npad), I32),
        compiler_params=_cparams("parallel"),
        name="dest",
    )(idx, rank, jnp.broadcast_to(start.astype(F32)[:, None], (ne, LANES)))


def _slotmap_kernel(start_ref, count_ref, dest_ref, map_ref, *, ne, trash_code):
    i = pl.program_id(0)

    @pl.when(i == 0)
    def _():
        def fill(s, carry):
            map_ref[s] = trash_code
            return carry

        for e in range(ne):
            first = start_ref[e] + count_ref[e]
            last = start_ref[e + 1] if e + 1 < ne else map_ref.shape[0]
            lax.fori_loop(first, last, fill, 0)

    def put(j, carry):
        for kk in range(TOP_K):
            map_ref[dest_ref[0, kk, j]] = (i * TOK_TILE + j) * TOP_K + kk
        return carry

    lax.fori_loop(0, TOK_TILE, put, 0, unroll=8)


def _slotmap(start, counts, dest3, n_slots, trash_code):
    n_tiles = dest3.shape[0]
    grid_spec = pltpu.PrefetchScalarGridSpec(
        num_scalar_prefetch=2,
        grid=(n_tiles,),
        in_specs=[pl.BlockSpec((1, TOP_K, TOK_TILE), lambda i, st, ct: (i, 0, 0), memory_space=pltpu.SMEM)],
        out_specs=pl.BlockSpec(memory_space=pltpu.SMEM),
    )
    return pl.pallas_call(
        functools.partial(_slotmap_kernel, ne=start.shape[0], trash_code=trash_code),
        grid_spec=grid_spec,
        out_shape=jax.ShapeDtypeStruct((n_slots,), I32),
        compiler_params=_cparams("arbitrary"),
        name="slotmap",
    )(start, counts, dest3)


def _decode_kernel(code_ref, tok_ref, row_ref, *, trash_code, y_stride):
    code = code_ref[...]
    nblk, blk = code.shape
    k_bits = TOP_K.bit_length() - 1
    assert TOP_K == 1 << k_bits
    blk_i = lax.broadcasted_iota(I32, (nblk, blk), 0)
    parity = jnp.where(blk_i == nblk - 1, 1, blk_i & 1)
    trash_row = TOP_K * y_stride + parity * blk + lax.broadcasted_iota(I32, (nblk, blk), 1)
    pad = code >= trash_code
    tok = code >> k_bits
    tok_ref[...] = jnp.where(pad, 0, tok)
    row_ref[...] = jnp.where(pad, trash_row, (code & (TOP_K - 1)) * y_stride + tok)


def _decode(codes, trash_code, y_stride):
    shp = jax.ShapeDtypeStruct(codes.shape, I32)
    return pl.pallas_call(
        functools.partial(_decode_kernel, trash_code=trash_code, y_stride=y_stride),
        out_shape=[shp, shp],
        compiler_params=pltpu.CompilerParams(vmem_limit_bytes=VMEM_LIMIT),
        name="decode",
    )(codes)


def _moe_kernel(be_ref, nact_ref, tcur_ref, tnext_ref, rprev_ref, rcur_ref, h_ref, wgu_ref, bgu_ref, wd_ref,
                bd_ref, y_ref, xbuf, xb_ref, obuf, gsem, ssem, wgu_bf, wd_bf, *, dff):
    i = pl.program_id(0)
    nact = nact_ref[0]
    blk = MOE_BLOCK

    def gather(t_ref, j):
        return pltpu.make_async_copy(h_ref.at[pl.ds(t_ref[0, 0, j], 1), :], xbuf.at[pl.ds(j, 1), :], gsem)

    def scatter(r_ref, j, slot):
        return pltpu.make_async_copy(obuf.at[slot, pl.ds(j, 1), :], y_ref.at[pl.ds(r_ref[0, 0, j], 1), :],
                                     ssem.at[slot])

    def wait_gathers():
        pltpu.make_async_copy(h_ref.at[pl.ds(0, blk), :], xbuf, gsem).wait()

    def wait_scatters(slot):
        pltpu.make_async_copy(obuf.at[slot], y_ref.at[pl.ds(0, blk), :], ssem.at[slot]).wait()

    @pl.when(i == 0)
    def _():
        obuf[...] = jnp.zeros_like(obuf)
        for j in range(blk):
            gather(tcur_ref, j).start(priority=j % 2)

    @pl.when(i < nact)
    def _():
        wait_gathers()

        @pl.when(i >= 1)
        def _():
            wait_scatters(i % 2)

        @pl.when((i == 0) | (be_ref[i] != be_ref[jnp.maximum(i - 1, 0)]))
        def _():
            wgu_bf[...] = wgu_ref[...].astype(BF16)
            wd_bf[...] = wd_ref[...].astype(BF16)

        xb_ref[...] = xbuf[...].astype(BF16)
        for j in range(blk):
            gather(tnext_ref, j).start(priority=j % 2)
        for j in range(blk):
            scatter(rprev_ref, j, (i + 1) % 2).start(priority=j % 2)
        gu = jnp.dot(xb_ref[...], wgu_bf[...], preferred_element_type=F32) + bgu_ref[...]
        gate = jnp.minimum(gu[:, :dff], SWIGLU_LIMIT)
        up = jnp.clip(gu[:, dff:], -SWIGLU_LIMIT, SWIGLU_LIMIT)
        act = (up + 1.0) * gate * _sigmoid(gate * SWIGLU_ALPHA)
        obuf[i % 2] = jnp.dot(act.astype(BF16), wd_bf[...], preferred_element_type=F32) + bd_ref[...]

        @pl.when(i == nact - 1)
        def _():
            for j in range(blk):
                scatter(rcur_ref, j, i % 2).start(priority=j % 2)
            wait_gathers()
            wait_scatters(0)
            wait_scatters(1)


def _moe(blk_e, nact, slot_map, tok_buf, w_gu, b_gu, w_down, b_down, n_blocks, y_stride):
    ne, d, dff2 = w_gu.shape
    dff = dff2 // 2
    blk = MOE_BLOCK
    trash_code = TOP_K * y_stride
    codes = jnp.concatenate([slot_map[:n_blocks * blk].reshape(n_blocks, blk), jnp.full((1, blk), trash_code, I32)])
    tok_tab, row_tab = (a.reshape(n_blocks + 1, 1, blk) for a in _decode(codes, trash_code, y_stride))
    smem = lambda f: pl.BlockSpec((1, 1, blk), f, memory_space=pltpu.SMEM)
    nxt = lambda i, be, na: (jnp.where(i + 1 < na[0], i + 1, jnp.minimum(i, na[0] - 1)), 0, 0)
    grid_spec = pltpu.PrefetchScalarGridSpec(
        num_scalar_prefetch=2,
        grid=(n_blocks,),
        in_specs=[smem(lambda i, be, na: (i, 0, 0)), smem(nxt),
                  smem(lambda i, be, na: (jnp.where(i == 0, n_blocks, i - 1), 0, 0)),
                  smem(lambda i, be, na: (i, 0, 0)),
                  pl.BlockSpec(memory_space=pl.ANY),
                  pl.BlockSpec((None, d, dff2), lambda i, be, na: (be[i], 0, 0)),
                  pl.BlockSpec((None, 1, dff2), lambda i, be, na: (be[i], 0, 0)),
                  pl.BlockSpec((None, dff, d), lambda i, be, na: (be[i], 0, 0)),
                  pl.BlockSpec((None, 1, d), lambda i, be, na: (be[i], 0, 0))],
        out_specs=pl.BlockSpec(memory_space=pl.ANY),
        scratch_shapes=[pltpu.VMEM((blk, d), F32), pltpu.VMEM((blk, d), BF16), pltpu.VMEM((2, blk, d), F32),
                        pltpu.SemaphoreType.DMA(()), pltpu.SemaphoreType.DMA((2,)),
                        pltpu.VMEM((d, dff2), BF16), pltpu.VMEM((dff, d), BF16)],
    )
    return pl.pallas_call(
        functools.partial(_moe_kernel, dff=dff),
        grid_spec=grid_spec,
        out_shape=jax.ShapeDtypeStruct((TOP_K * y_stride + 2 * blk, d), F32),
        compiler_params=_cparams("arbitrary"),
        name="moe",
    )(blk_e, nact, tok_tab, tok_tab, row_tab, row_tab, tok_buf, w_gu, b_gu.reshape(ne, 1, dff2), w_down,
      b_down.reshape(ne, 1, d))


def _combine_kernel(y0_ref, y1_ref, y2_ref, y3_ref, w_ref, x1_ref, gt_ref, o_ref, *, rows):
    w = w_ref[...]
    y = w[:rows, 0:1] * y0_ref[...]
    for kk, y_ref in enumerate((y1_ref, y2_ref, y3_ref), start=1):
        y = y + w[:rows, kk:kk + 1] * y_ref[...]
    o_ref[...] = x1_ref[...] + gt_ref[...] * y


def _combine(ys, wt3, x1, mod3, tok_rows, tile0, rows):
    nb, t, d = x1.shape
    rmod = mod3.shape[1]
    nt = t // rows
    row0 = tile0 * TOK_TILE
    assert tok_rows % rows == 0 and row0 % rows == 0
    row = lambda b, i: (b, i, 0)
    gate = pl.BlockSpec((None, rmod, d), (lambda b, i: (b, 0, 5)) if rmod == 1 else (lambda b, i: (b, i, 5)))
    expert_out = lambda kk: pl.BlockSpec((rows, d), lambda b, i: ((kk * tok_rows + row0) // rows + b * nt + i, 0))
    return pl.pallas_call(
        functools.partial(_combine_kernel, rows=rows),
        grid=(nb, nt),
        in_specs=[expert_out(kk) for kk in range(TOP_K)]
                 + [pl.BlockSpec((None, TOK_TILE, TOP_K), lambda b, i: (tile0 + (b * nt + i) * rows // TOK_TILE, 0, 0)),
                    pl.BlockSpec((None, rows, d), row), gate],
        out_specs=pl.BlockSpec((None, rows, d), row),
        out_shape=jax.ShapeDtypeStruct((nb, t, d), F32),
        compiler_params=_cparams("parallel", "parallel"),
        name="combine",
    )(ys, ys, ys, ys, wt3, x1, mod3)


def _rope_tables(pos, n_heads):
    half = HEAD_DIM // 2
    inv_freq = 1.0 / (ROPE_THETA ** (jnp.arange(0, HEAD_DIM, 2, dtype=F32) / HEAD_DIM))
    ang = pos.astype(F32)[:, None] * inv_freq[None, :]
    cos, sin = jnp.cos(ang), jnp.sin(ang)
    del half
    return (jnp.tile(jnp.concatenate([cos, cos], axis=-1), (1, n_heads)),
            jnp.tile(jnp.concatenate([-sin, sin], axis=-1), (1, n_heads)))


def _block_diag(width, value):
    h = np.arange(width) // HEAD_DIM
    return jnp.asarray(np.where(h[:, None] == h[None, :], value, 0.0), F32)


def kernel(x_prompt, x_sample, cache_k, cache_v, state_wkv, state_shift, c_prompt, c_sample, w_ada, b_ada, norm1_g, norm2_g, w_in, q_norm_g, k_norm_g, rwkv_mu, rwkv_w0, rwkv_w2, rwkv_a0, rwkv_a2, rwkv_g2, rwkv_k_k, rwkv_k_a, rwkv_r_k, rwkv_ln_w, rwkv_ln_b, w_out, router_w, router_b, moe_w_gu, moe_b_gu, moe_w_down, moe_b_down):
    nbp, t, d = x_prompt.shape
    nbs, ts, _ = x_sample.shape
    depth = w_ada.shape[0]
    assert depth == 1 and ts == 1
    n_heads = cache_k.shape[3]
    aw = n_heads * HEAD_DIM
    rwid = rwkv_w0.shape[1]
    rw_heads = rwid // HEAD_DIM
    rwc = rwkv_mu.shape[1]
    past = cache_k.shape[2]
    ne = router_w.shape[2]
    keep = min(MAX_WINDOW, t)
    lyr = 0

    w_in_bf = w_in[lyr].astype(BF16)
    wt_bf = w_out[lyr][:aw].astype(BF16)
    wb_bf = w_out[lyr][aw:].astype(BF16)
    g1 = norm1_g[lyr].reshape(1, d)
    g2 = norm2_g[lyr].reshape(1, d)
    qg = jnp.tile(q_norm_g[lyr], n_heads).reshape(1, aw)
    kg = jnp.tile(k_norm_g[lyr], n_heads).reshape(1, aw)
    bd_mean_a = _block_diag(aw, 1.0 / HEAD_DIM)
    bd_mean_r = _block_diag(rwid, 1.0 / HEAD_DIM)
    bd_ones_r = _block_diag(rwid, 1.0)
    dl = rwkv_w2.shape[1]
    w2p = jnp.zeros((LANES, rwid), F32).at[:dl].set(rwkv_w2[lyr])
    a2p = jnp.zeros((LANES, rwid), F32).at[dl:dl + rwkv_a2.shape[1]].set(rwkv_a2[lyr])
    vec = lambda a: a[lyr].reshape(1, -1)
    rwt = router_w[lyr].T
    rb = router_b[lyr].reshape(ne, 1)

    rows_c = nbp + nbs
    rows_pad = -(-rows_c // 8) * 8
    c_all = jnp.zeros((rows_pad, d), F32).at[:nbp].set(c_prompt).at[nbp:rows_c].set(c_sample)
    mod = _ada(c_all, w_ada[lyr], b_ada[lyr])
    mod_p = mod[:nbp].reshape(nbp, 1, 6 * d)
    mod_s = mod[nbp:rows_c].reshape(1, nbs, 6 * d)
    xs3 = x_sample.reshape(1, nbs, d)

    cos_p, sin_p = _rope_tables(jnp.arange(t), n_heads)
    cos_s, sin_s = _rope_tables(jnp.full((nbs,), PAST_LEN), n_heads)
    qp, kp, vp, rwp, kt_p, vt_p = _inproj(x_prompt, mod_p, g1, w_in_bf, cos_p, sin_p, qg, kg, bd_mean_a,
                                          ROW_TILE, keep=keep)
    qs, ks, vs, rws = _inproj(xs3, mod_s, g1, w_in[lyr], cos_s, sin_s, qg, kg, bd_mean_a, nbs)

    attn_p = _attn_prompt(qp, kp, vp)
    as_rows = lambda a: a.reshape(nbs, 1, -1)
    cache_t = lambda cch: jnp.transpose(cch[lyr], (0, 2, 3, 1)).reshape(nbs, aw, past)
    attn_s = _attn_sample(as_rows(qs), as_rows(ks), as_rows(vs), cache_t(cache_k), cache_t(cache_v))

    rw_args = (vec(rwkv_mu), vec(rwkv_w0), w2p, vec(rwkv_a0), a2p, rwkv_g2[lyr], vec(rwkv_k_k), vec(rwkv_k_a), bd_ones_r)
    pre_p = _rwprep(rwp, jnp.zeros((nbp, 1, rwc), F32), *rw_args, tm=ROW_TILE, whole_prev=False)
    pre_s = _rwprep(rws, state_shift[lyr].reshape(1, nbs, rwc), *rw_args, tm=nbs, whole_prev=True)
    r_p, ld_p, k_p, v_p, al_p, be_p, g_p = pre_p
    y_p, wkv_p = _rwchunk(r_p, ld_p, k_p, v_p, al_p, be_p)
    r_s, ld_s, k_s, v_s, al_s, be_s, g_s = pre_s
    y_s, wkv_s = _rwstep(*(as_rows(a) for a in (r_s, ld_s, k_s, v_s, al_s, be_s)), state_wkv[lyr])
    y_s = y_s.reshape(1, nbs, rwid)

    op_args = (rwkv_r_k[lyr].reshape(1, rwid), vec(rwkv_ln_w), vec(rwkv_ln_b), bd_mean_r, bd_ones_r, rwt, rb)
    n_p = nbp * t
    n_valid = n_p + nbs
    assert n_p % ROUTE_TILE == 0 and ROUTE_TILE % TOK_TILE == 0 and n_p % nbs == 0
    npad = -(-n_valid // ROUTE_TILE) * ROUTE_TILE
    x1_p, tok_buf, lg_p = _outproj(x_prompt, attn_p, y_p, r_p, k_p, v_p, g_p, mod_p, g2, wt_bf, wb_bf, *op_args,
                                   tm=ROW_TILE, tok_rows=npad)
    x1_s, tok_buf, lg_s = _outproj(xs3, attn_s.reshape(1, nbs, aw), y_s, r_s, k_s, v_s, g_s, mod_s, g2,
                                   w_out[lyr][:aw], w_out[lyr][aw:], *op_args, tm=nbs, tok_rows=npad,
                                   tok_row0=n_p, tok_buf=tok_buf)

    logits_t = jnp.zeros((ne, npad), F32).at[:, :n_p].set(lg_p).at[:, n_p:n_valid].set(lg_s)
    idx, wts, rank, cnt = _route(logits_t, n_valid)
    counts = cnt[:, 0].astype(I32)
    blk = MOE_BLOCK
    n_blocks = -(-(n_valid * TOP_K + ne * (blk - 1)) // blk)
    padded = (counts + blk - 1) // blk * blk
    pad_end = jnp.cumsum(padded)
    start = pad_end - padded
    blk_e = jnp.minimum(jnp.sum(pad_end[None, :] <= (jnp.arange(n_blocks) * blk)[:, None], axis=1), ne - 1).astype(I32)
    nact = (pad_end[-1] // blk).reshape(1).astype(I32)
    trash = n_blocks * blk
    dest = _dest(idx, rank, start.astype(I32), n_valid, trash)
    n_tiles = npad // TOK_TILE
    dest3 = dest.reshape(TOP_K, n_tiles, TOK_TILE).transpose(1, 0, 2)
    wt3 = wts.reshape(TOP_K, n_tiles, TOK_TILE).transpose(1, 2, 0)
    slot_map = _slotmap(start.astype(I32), counts, dest3, trash + 8, npad * TOP_K)
    ys = _moe(blk_e, nact, slot_map, tok_buf, moe_w_gu[lyr], moe_b_gu[lyr], moe_w_down[lyr], moe_b_down[lyr],
              n_blocks, npad)
    y_prompt = _combine(ys, wt3, x1_p, mod_p, npad, 0, TOK_TILE)
    y_sample = _combine(ys, wt3, x1_s, mod_s, npad, n_p // TOK_TILE, nbs)

    kept = lambda a: jnp.transpose(a.reshape(nbp, n_heads, HEAD_DIM, keep), (0, 3, 1, 2))[None]
    return (y_prompt, y_sample.reshape(nbs, ts, d), kept(kt_p), kept(vt_p), wkv_p[None], rwp[:, t - 1][None],
            ks.reshape(nbs, ts, n_heads, HEAD_DIM)[None], vs.reshape(nbs, ts, n_heads, HEAD_DIM)[None],
            wkv_s[None], rws.reshape(nbs, rwc)[None])
```

```python
import functools

import numpy as np
import jax
import jax.numpy as jnp
from jax import lax
from jax.experimental import pallas as pl
from jax.experimental.pallas import tpu as pltpu

F32 = jnp.float32
BF16 = jnp.bfloat16
I32 = jnp.int32

HEAD_DIM = 64
LANES = 128
DILATED_PATTERNS = ((128, 1), (512, 4), (2048, 16))
WINDOW_STEPS = 128
MAX_WINDOW = 2048
PAST_LEN = 16384
ROPE_THETA = 10000.0
NORM_EPS = 1e-6
GN_EPS = 64e-5
TOP_K = 4
SWIGLU_ALPHA = 1.702
SWIGLU_LIMIT = 7.0
RW_CHUNK = 128
MOE_BLOCK = 256
ATTN_UNITS = 4
ROW_TILE = 512
TOK_TILE = 256
ROUTE_TILE = 512
VMEM_LIMIT = 56 * 1024 * 1024
NEG_BIG = -1e30


def _cparams(*sem):
    return pltpu.CompilerParams(dimension_semantics=sem, vmem_limit_bytes=VMEM_LIMIT)


def _dot(a, b):
    return jnp.dot(a.astype(BF16), b.astype(BF16), preferred_element_type=F32)


def _dot_nt(a, b):
    return lax.dot_general(a.astype(BF16), b.astype(BF16), (((1,), (1,)), ((), ())),
                           preferred_element_type=F32)


def _split2(a):
    hi = a.astype(BF16)
    lo = (a - hi.astype(F32)).astype(BF16)
    return hi, lo


def _split3(a):
    hi = a.astype(BF16)
    r1 = a - hi.astype(F32)
    mid = r1.astype(BF16)
    lo = (r1 - mid.astype(F32)).astype(BF16)
    return hi, mid, lo


def _dot_x(a, e):
    e = e.astype(BF16)
    hi, mid, lo = _split3(a)
    return (jnp.dot(hi, e, preferred_element_type=F32) + jnp.dot(mid, e, preferred_element_type=F32)
            + jnp.dot(lo, e, preferred_element_type=F32))


def _xdot(e, a):
    e = e.astype(BF16)
    hi, mid, lo = _split3(a)
    return (jnp.dot(e, hi, preferred_element_type=F32) + jnp.dot(e, mid, preferred_element_type=F32)
            + jnp.dot(e, lo, preferred_element_type=F32))


def _dot3(a, b):
    ah, al = _split2(a)
    bh, bl = _split2(b)
    return (jnp.dot(ah, bh, preferred_element_type=F32) + jnp.dot(ah, bl, preferred_element_type=F32)
            + jnp.dot(al, bh, preferred_element_type=F32))


def _dot3_nt(a, b):
    ah, al = _split2(a)
    bh, bl = _split2(b)
    dn = (((1,), (1,)), ((), ()))
    return (lax.dot_general(ah, bh, dn, preferred_element_type=F32)
            + lax.dot_general(ah, bl, dn, preferred_element_type=F32)
            + lax.dot_general(al, bh, dn, preferred_element_type=F32))


def _mm(a, w):
    if w.dtype == BF16:
        return jnp.dot(a.astype(BF16), w, preferred_element_type=F32)
    return _dot3(a, w)


def _sigmoid(x):
    return 1.0 / (1.0 + jnp.exp(-x))


def _ada_kernel(c_ref, w_ref, b_ref, o_ref):
    c = c_ref[...]
    o_ref[...] = _dot3(c * _sigmoid(c), w_ref[...]) + b_ref[...]


def _ada(c_all, w_ada, b_ada):
    rows, d = c_all.shape
    n = w_ada.shape[1]
    tn = n // 4
    return pl.pallas_call(
        _ada_kernel,
        grid=(n // tn,),
        in_specs=[pl.BlockSpec((rows, d), lambda j: (0, 0)),
                  pl.BlockSpec((d, tn), lambda j: (0, j)),
                  pl.BlockSpec((1, tn), lambda j: (0, j))],
        out_specs=pl.BlockSpec((rows, tn), lambda j: (0, j)),
        out_shape=jax.ShapeDtypeStruct((rows, n), F32),
        compiler_params=_cparams("arbitrary"),
        name="ada",
    )(c_all, w_ada, b_ada.reshape(1, n))


def _inproj_kernel(x_ref, sh_ref, sc_ref, g_ref, w_ref, cos_ref, sin_ref, qg_ref, kg_ref, bd_ref,
                   q_ref, k_ref, v_ref, rw_ref, *maybe_kv_t, aw, first_kept):
    x = x_ref[...]
    ms = jnp.mean(x * x, axis=-1, keepdims=True)
    h = x * lax.rsqrt(ms + NORM_EPS) * g_ref[...] * (1.0 + sc_ref[...]) + sh_ref[...]
    proj = _mm(h, w_ref[...])
    cos = cos_ref[...]
    sin = sin_ref[...]
    lane = lax.broadcasted_iota(I32, (1, aw), 1)
    first_half = (lane % HEAD_DIM) < (HEAD_DIM // 2)
    bd = bd_ref[...]

    def norm_rope(t, g):
        tn = t * lax.rsqrt((_dot if w_ref.dtype == BF16 else _dot_x)(t * t, bd) + NORM_EPS) * g
        rot = jnp.where(first_half, pltpu.roll(tn, aw - HEAD_DIM // 2, 1), pltpu.roll(tn, HEAD_DIM // 2, 1))
        return tn * cos + rot * sin

    scale = 1.0 / np.sqrt(HEAD_DIM).astype(np.float32)
    q_ref[...] = norm_rope(proj[:, :aw], qg_ref[...]) * scale
    k = norm_rope(proj[:, aw:2 * aw], kg_ref[...])
    v = proj[:, 2 * aw:3 * aw]
    k_ref[...] = k
    v_ref[...] = v
    rw_ref[...] = proj[:, 3 * aw:]
    if maybe_kv_t:
        kt_ref, vt_ref = maybe_kv_t
        kept = pl.program_id(1) >= first_kept

        @pl.when(kept)
        def _():
            kt_ref[...] = k.T
            vt_ref[...] = v.T

        @pl.when(jnp.logical_not(kept))
        def _():
            kt_ref[...] = jnp.zeros_like(kt_ref)
            vt_ref[...] = jnp.zeros_like(vt_ref)


def _inproj(x3, mod3, g1, w_in_bf, cos_t, sin_t, qg, kg, bd_mean, tm, keep=0):
    nb, t, d = x3.shape
    r = mod3.shape[1]
    ncol = w_in_bf.shape[1]
    aw = cos_t.shape[1]
    rwc = ncol - 3 * aw
    grid = (nb, t // tm)
    row = lambda b, i: (b, i, 0)
    const = lambda b, i: (0, 0)
    mod_spec = lambda s: pl.BlockSpec((None, r, d), (lambda b, i: (b, 0, s)) if r == 1 else (lambda b, i: (b, i, s)))
    out_specs = [pl.BlockSpec((None, tm, aw), row)] * 3 + [pl.BlockSpec((None, tm, rwc), row)]
    out_shape = [jax.ShapeDtypeStruct((nb, t, aw), F32)] * 3 + [jax.ShapeDtypeStruct((nb, t, rwc), F32)]
    first_kept = (t - keep) // tm
    if keep:
        assert keep % tm == 0 and (t - keep) % tm == 0
        kept_spec = pl.BlockSpec((None, aw, tm), lambda b, i: (b, 0, jnp.maximum(i - first_kept, 0)))
        out_specs += [kept_spec, kept_spec]
        out_shape += [jax.ShapeDtypeStruct((nb, aw, keep), F32)] * 2
    outs = pl.pallas_call(
        functools.partial(_inproj_kernel, aw=aw, first_kept=first_kept),
        grid=grid,
        in_specs=[pl.BlockSpec((None, tm, d), row), mod_spec(0), mod_spec(1),
                  pl.BlockSpec((1, d), const), pl.BlockSpec((d, ncol), const),
                  pl.BlockSpec((tm, aw), lambda b, i: (i, 0)), pl.BlockSpec((tm, aw), lambda b, i: (i, 0)),
                  pl.BlockSpec((1, aw), const), pl.BlockSpec((1, aw), const), pl.BlockSpec((aw, aw), const)],
        out_specs=out_specs,
        out_shape=out_shape,
        compiler_params=_cparams("parallel", "arbitrary"),
        name="inproj",
    )(x3, mod3, mod3, g1, w_in_bf, cos_t, sin_t, qg, kg, bd_mean)
    return outs


def _attn_prompt_kernel(q_ref, k_ref, v_ref, o_ref, kp_ref, vp_ref, m_ref, l_ref, acc_ref, *, t, pad):
    nq = WINDOW_STEPS
    nk = 2 * WINDOW_STEPS
    kp_ref[pl.ds(0, pad), :] = jnp.zeros((pad, LANES), F32)
    vp_ref[pl.ds(0, pad), :] = jnp.zeros((pad, LANES), F32)
    kp_ref[pl.ds(pad, t), :] = k_ref[...]
    vp_ref[pl.ds(pad, t), :] = v_ref[...]
    head0 = lax.broadcasted_iota(I32, (1, LANES), 1) < HEAD_DIM
    qi = lax.broadcasted_iota(I32, (nq, nk), 0)
    kj = lax.broadcasted_iota(I32, (nq, nk), 1)
    steps_back = qi + nq - kj
    band = (steps_back >= 0) & (steps_back <= WINDOW_STEPS)
    has_past = kj >= nq

    hsels = (head0, jnp.logical_not(head0))
    nt = (((1,), (1,)), ((), ()))

    for p, (_, d) in enumerate(DILATED_PATTERNS):
        def units(g, carry, p=p, d=d):
            rows_q, kb, vb, valid, qh = [], [], [], [], []
            for j in range(ATTN_UNITS):
                u = g * ATTN_UNITS + j
                res = u % d
                blk = u // d
                q_start = res + d * nq * blk
                k_start = pad + q_start - d * nq
                if d == 1:
                    rows_q.append(pl.ds(q_start, nq))
                    rows_k = pl.ds(k_start, nk)
                else:
                    rows_q.append(pl.ds(q_start, nq, stride=d))
                    rows_k = pl.ds(k_start, nk, stride=d)
                q = q_ref[rows_q[j], :]
                kb.append(kp_ref[rows_k, :].astype(BF16))
                vb.append(vp_ref[rows_k, :].astype(BF16))
                valid.append(band & (has_past | (blk > 0)))
                qh.append([jnp.where(hsel, q, 0.0).astype(BF16) for hsel in hsels])
            chains = [(j, h) for j in range(ATTN_UNITS) for h in range(2)]
            s = [lax.dot_general(qh[j][h], kb[j], nt, preferred_element_type=F32) for j, h in chains]
            s = [jnp.where(valid[j], sc, NEG_BIG) for (j, h), sc in zip(chains, s)]
            mx = [jnp.max(sc, axis=-1, keepdims=True) for sc in s]
            e = [jnp.exp(sc - m) for sc, m in zip(s, mx)]
            den = [jnp.sum(ec, axis=-1, keepdims=True) for ec in e]
            o = [jnp.dot(ec.astype(BF16), vb[j], preferred_element_type=F32) for (j, h), ec in zip(chains, e)]
            for j in range(ATTN_UNITS):
                m_ref[p, rows_q[j], :] = jnp.where(head0, mx[2 * j], mx[2 * j + 1])
                l_ref[p, rows_q[j], :] = jnp.where(head0, den[2 * j], den[2 * j + 1])
                acc_ref[p, rows_q[j], :] = jnp.where(head0, o[2 * j], o[2 * j + 1])
            return carry

        lax.fori_loop(0, t // nq // ATTN_UNITS, units, 0)

    rows = 256

    def merge(i, carry):
        sl = pl.ds(pl.multiple_of(i * rows, rows), rows)
        m0, m1, m2 = m_ref[0, sl, :], m_ref[1, sl, :], m_ref[2, sl, :]
        mm = jnp.maximum(jnp.maximum(m0, m1), m2)
        w0, w1, w2 = jnp.exp(m0 - mm), jnp.exp(m1 - mm), jnp.exp(m2 - mm)
        num = w0 * acc_ref[0, sl, :] + w1 * acc_ref[1, sl, :] + w2 * acc_ref[2, sl, :]
        den = w0 * l_ref[0, sl, :] + w1 * l_ref[1, sl, :] + w2 * l_ref[2, sl, :]
        o_ref[sl, :] = num / den
        return carry

    lax.fori_loop(0, t // rows, merge, 0)


def _attn_prompt(q, k, v):
    nb, t, aw = q.shape
    pad = MAX_WINDOW
    assert t % MAX_WINDOW == 0
    spec = pl.BlockSpec((None, t, LANES), lambda b, hp: (b, 0, hp))
    return pl.pallas_call(
        functools.partial(_attn_prompt_kernel, t=t, pad=pad),
        grid=(nb, aw // LANES),
        in_specs=[spec, spec, spec],
        out_specs=spec,
        out_shape=jax.ShapeDtypeStruct((nb, t, aw), F32),
        scratch_shapes=[pltpu.VMEM((pad + t, LANES), F32), pltpu.VMEM((pad + t, LANES), F32),
                        pltpu.VMEM((3, t, LANES), F32), pltpu.VMEM((3, t, LANES), F32),
                        pltpu.VMEM((3, t, LANES), F32)],
        compiler_params=_cparams("parallel", "parallel"),
        name="attn_prompt",
    )(q, k, v)


def _attn_sample_kernel(q_ref, kn_ref, vn_ref, kt_ref, vt_ref, o_ref, *, nh, w):
    aw = nh * HEAD_DIM
    dist = w - lax.broadcasted_iota(I32, (1, w), 1)
    mult = jnp.zeros((1, w), F32)
    for win, d in DILATED_PATTERNS:
        mult = mult + jnp.where((dist % d == 0) & (dist <= win), 1.0, 0.0)
    n_pat = float(len(DILATED_PATTERNS))
    q_col = _col(q_ref[...], aw)
    kn_col = _col(kn_ref[...], aw)
    vn_col = _col(vn_ref[...], aw)
    outs = []
    for h in range(nh):
        hs = pl.ds(h * HEAD_DIM, HEAD_DIM)
        qh = q_col[h * HEAD_DIM:(h + 1) * HEAD_DIM]
        s = jnp.sum(kt_ref[hs, :] * qh, axis=0, keepdims=True)
        s_self = jnp.sum(qh * kn_col[h * HEAD_DIM:(h + 1) * HEAD_DIM], axis=0, keepdims=True)
        s = jnp.where(mult > 0.0, s, NEG_BIG)
        mx = jnp.maximum(jnp.max(s, axis=1, keepdims=True), s_self)
        pr = mult * jnp.exp(s - mx)
        p_self = n_pat * jnp.exp(s_self - mx)
        den = jnp.sum(pr, axis=1, keepdims=True) + p_self
        num = jnp.sum(vt_ref[hs, :] * pr, axis=1, keepdims=True) + p_self * vn_col[h * HEAD_DIM:(h + 1) * HEAD_DIM]
        outs.append(num / den)
    o_ref[...] = _row(jnp.concatenate(outs, axis=0), aw)


def _attn_sample(q, kn, vn, cache_kt, cache_vt):
    nb, _, aw = q.shape
    w = cache_kt.shape[2]
    assert w == MAX_WINDOW
    row = pl.BlockSpec((None, 1, aw), lambda i: (i, 0, 0))
    mat = pl.BlockSpec((None, aw, w), lambda i: (i, 0, 0))
    return pl.pallas_call(
        functools.partial(_attn_sample_kernel, nh=aw // HEAD_DIM, w=w),
        grid=(nb,),
        in_specs=[row, row, row, mat, mat],
        out_specs=row,
        out_shape=jax.ShapeDtypeStruct((nb, 1, aw), F32),
        compiler_params=_cparams("parallel"),
        name="attn_sample",
    )(q, kn, vn, cache_kt, cache_vt)


def _rwprep_kernel(p_ref, prev_ref, first_ref, mu_ref, w0_ref, w2_ref, a0_ref, a2_ref, g2_ref, kk_ref, ka_ref,
                   bd_ref, r_ref, ld_ref, k_ref, v_ref, al_ref, be_ref, g_ref, *, rwid, whole_prev):
    p = p_ref[...]
    if whole_prev:
        prev = first_ref[...]
    else:
        tm = p.shape[0]
        before = jnp.where(pl.program_id(1) == 0, first_ref[...], prev_ref[7:8, :])
        rowi = lax.broadcasted_iota(I32, (tm, 1), 0)
        prev = jnp.where(rowi == 0, before, pltpu.roll(p, 1, 0))
    xs = p + mu_ref[...] * (prev - p)
    r = xs[:, :rwid]
    k = xs[:, rwid:2 * rwid]
    v = xs[:, 2 * rwid:3 * rwid]
    xwa = xs[:, 3 * rwid:3 * rwid + LANES]
    xg = xs[:, 3 * rwid + LANES:]
    z = w0_ref[...] + _dot3(jnp.tanh(xwa), w2_ref[...])
    softplus_neg = jnp.maximum(-z, 0.0) + jnp.log(1.0 + jnp.exp(-jnp.abs(z)))
    w = -softplus_neg - 0.5
    a = _sigmoid(a0_ref[...] + _dot3(xwa, a2_ref[...]))
    g = _dot3(_sigmoid(xg), g2_ref[...])
    kk = k * kk_ref[...]
    norm = jnp.sqrt(_dot_x(kk * kk, bd_ref[...]))
    kk = kk / jnp.maximum(norm, 1e-12)
    r_ref[...] = r
    ld_ref[...] = -jnp.exp(w)
    k_ref[...] = k * (1.0 + (a - 1.0) * ka_ref[...])
    v_ref[...] = v
    al_ref[...] = -kk
    be_ref[...] = kk * a
    g_ref[...] = g


def _rwprep(rw, first, mu, w0, w2p, a0, a2p, g2, k_k, k_a, bd_ones, tm, whole_prev):
    nb, t, rwc = rw.shape
    rwid = w0.shape[1]
    grid = (nb, t // tm)
    row = lambda b, i: (b, i, 0)
    const = lambda b, i: (0, 0)
    if whole_prev:
        prev_spec = pl.BlockSpec((None, tm, rwc), row)
        first_spec = pl.BlockSpec((None, tm, rwc), row)
    else:
        prev_spec = pl.BlockSpec((None, 8, rwc), lambda b, i: (b, jnp.maximum(i * (tm // 8) - 1, 0), 0))
        first_spec = pl.BlockSpec((None, 1, rwc), lambda b, i: (b, 0, 0))
    vec = pl.BlockSpec((1, rwid), const)
    out = pl.BlockSpec((None, tm, rwid), row)
    return pl.pallas_call(
        functools.partial(_rwprep_kernel, rwid=rwid, whole_prev=whole_prev),
        grid=grid,
        in_specs=[pl.BlockSpec((None, tm, rwc), row), prev_spec, first_spec,
                  pl.BlockSpec((1, rwc), const), vec, pl.BlockSpec((LANES, rwid), const),
                  vec, pl.BlockSpec((LANES, rwid), const), pl.BlockSpec((LANES, rwid), const), vec, vec,
                  pl.BlockSpec((rwid, rwid), const)],
        out_specs=[out] * 7,
        out_shape=[jax.ShapeDtypeStruct((nb, t, rwid), F32)] * 7,
        compiler_params=_cparams("parallel", "parallel"),
        name="rwprep",
    )(rw, rw, first, mu, w0, w2p, a0, a2p, g2, k_k, k_a, bd_ones)


def _rwchunk_kernel(r_ref, ld_ref, k_ref, v_ref, al_ref, be_ref, y_ref, st_ref, z_ref, *, nb):
    c = RW_CHUNK
    ci = pl.program_id(1)

    @pl.when(ci == 0)
    def _():
        z_ref[...] = jnp.zeros_like(z_ref)

    ti = lax.broadcasted_iota(I32, (c, c), 0)
    si = lax.broadcasted_iota(I32, (c, c), 1)
    low_incl = si <= ti
    low_strict = si < ti
    diag = si == ti
    tri = jnp.where(low_incl, 1.0, 0.0).astype(BF16)
    eye = jnp.where(diag, 1.0, 0.0)
    head0 = lax.broadcasted_iota(I32, (1, LANES), 1) < HEAD_DIM
    hsels = (head0, jnp.logical_not(head0))
    same_head = (ti < HEAD_DIM) == (si < HEAD_DIM)
    nt = (((1,), (1,)), ((), ()))
    batches = range(nb)
    chains = [(b, h) for b in batches for h in range(2)]

    cum = [_xdot(tri, ld_ref[b]) for b in batches]
    tot = [cm[c - 1:c, :] for cm in cum]
    e_neg = [jnp.exp(-cm) for cm in cum]
    at = [al_ref[b] * jnp.exp(cum[b] - ld_ref[b]) for b in batches]
    rt = [r_ref[b] * jnp.exp(cum[b]) for b in batches]
    rhs_t = [jnp.concatenate([be_ref[b] * e_neg[b], k_ref[b] * e_neg[b]], axis=0).astype(BF16) for b in batches]
    vb = [v_ref[b].astype(BF16) for b in batches]
    at_h = [jnp.where(hsels[h], at[b], 0.0) for b, h in chains]
    rt_h = [jnp.where(hsels[h], rt[b], 0.0) for b, h in chains]
    a4 = [lax.dot_general(jnp.concatenate([a, r], axis=0).astype(BF16), rhs_t[b], nt, preferred_element_type=F32)
          for (b, h), a, r in zip(chains, at_h, rt_h)]
    a_ab = [jnp.where(low_strict, m[:c, :c], 0.0) for m in a4]
    a_ak = [jnp.where(low_strict, m[:c, c:], 0.0).astype(BF16) for m in a4]
    a_r = [jnp.concatenate([jnp.where(low_incl, m[c:, :c], 0.0), jnp.where(low_incl, m[c:, c:], 0.0)],
                           axis=1).astype(BF16) for m in a4]
    pw = [m.astype(BF16) for m in a_ab]
    inv = [eye + m for m in a_ab]
    for _ in range(int(np.log2(c)) - 1):
        pw = [jnp.dot(m, m, preferred_element_type=F32).astype(BF16) for m in pw]
        inv = [i + jnp.dot(i.astype(BF16), m, preferred_element_type=F32) for i, m in zip(inv, pw)]
    akv = [jnp.dot(m, vb[b], preferred_element_type=F32) for (b, h), m in zip(chains, a_ak)]
    x = [_dot(i, jnp.concatenate([kv, a], axis=1)) for i, kv, a in zip(inv, akv, at_h)]
    u0_h = [m[:, :LANES] for m in x]
    at2_h = [m[:, LANES:] for m in x]
    y0_h = [jnp.dot(ar, jnp.concatenate([u0, v_ref[b]], axis=0).astype(BF16), preferred_element_type=F32)
            for (b, h), ar, u0 in zip(chains, a_r, u0_h)]
    rt2_h = [r + jnp.dot(ar[:, :c], a2.astype(BF16), preferred_element_type=F32)
             for r, ar, a2 in zip(rt_h, a_r, at2_h)]
    z = [z_ref[b] for b in batches]
    uy = [_dot(jnp.concatenate([at2_h[2 * b] + at2_h[2 * b + 1], rt2_h[2 * b] + rt2_h[2 * b + 1]], axis=0), z[b])
          for b in batches]
    u = [uy[b][:c] + jnp.where(head0, u0_h[2 * b], u0_h[2 * b + 1]) for b in batches]
    for b in batches:
        y_ref[b] = uy[b][c:] + jnp.where(head0, y0_h[2 * b], y0_h[2 * b + 1])
    e_end = [jnp.exp(tot[b] - cum[b]) for b in batches]
    lhs_t = [jnp.concatenate([be_ref[b] * e_end[b], k_ref[b] * e_end[b]], axis=0) for b in batches]
    zadd = [_dot(lhs_t[b].T, jnp.concatenate([u[b], v_ref[b]], axis=0)) for b in batches]
    for b in batches:
        dcol = jnp.sum(jnp.where(diag, jnp.broadcast_to(jnp.exp(tot[b]), (c, c)), 0.0), axis=1, keepdims=True)
        z_ref[b] = dcol * z[b] + jnp.where(same_head, zadd[b], 0.0)

    @pl.when(ci == pl.num_programs(1) - 1)
    def _():
        for b in batches:
            s = z_ref[b].T
            st_ref[b, 0] = s[:HEAD_DIM, :HEAD_DIM]
            st_ref[b, 1] = s[HEAD_DIM:, HEAD_DIM:]


def _rwchunk(r, ld, k, v, al, be):
    nb, t, rwid = r.shape
    c = RW_CHUNK
    assert t % c == 0 and c == LANES
    npair = rwid // LANES
    seq = pl.BlockSpec((nb, c, LANES), lambda hp, ci: (0, ci, hp))
    return pl.pallas_call(
        functools.partial(_rwchunk_kernel, nb=nb),
        grid=(npair, t // c),
        in_specs=[seq] * 6,
        out_specs=[seq, pl.BlockSpec((nb, 2, HEAD_DIM, HEAD_DIM), lambda hp, ci: (0, hp, 0, 0))],
        out_shape=[jax.ShapeDtypeStruct((nb, t, rwid), F32),
                   jax.ShapeDtypeStruct((nb, 2 * npair, HEAD_DIM, HEAD_DIM), F32)],
        scratch_shapes=[pltpu.VMEM((nb, LANES, LANES), F32)],
        compiler_params=_cparams("parallel", "arbitrary"),
        name="rwchunk",
    )(r, ld, k, v, al, be)


def _col(row, width):
    return jnp.broadcast_to(row, (LANES, width)).T[:, 0:1]


def _row(col, width):
    return jnp.broadcast_to(col, (width, LANES)).T[0:1, :]


def _rwstep_kernel(r_ref, ld_ref, k_ref, v_ref, al_ref, be_ref, s_ref, y_ref, so_ref, *, bb, nh):
    rwid = nh * HEAD_DIM

    def one(b, carry):
        r, dcy, k, al, be = (ref[b] for ref in (r_ref, ld_ref, k_ref, al_ref, be_ref))
        dcy = jnp.exp(dcy)
        v_col = _col(v_ref[b], rwid)
        ys = []
        for h in range(nh):
            hs = slice(h * HEAD_DIM, (h + 1) * HEAD_DIM)
            s = s_ref[b, h]
            sa = jnp.sum(s * al[:, hs], axis=1, keepdims=True)
            s = s * dcy[:, hs] + sa * be[:, hs] + v_col[hs, :] * k[:, hs]
            so_ref[b, h] = s
            ys.append(jnp.sum(s * r[:, hs], axis=1, keepdims=True))
        y_ref[b] = _row(jnp.concatenate(ys, axis=0), rwid)
        return carry

    lax.fori_loop(0, bb, one, 0)


def _rwstep(r, ld, k, v, al, be, state):
    nb, _, rwid = r.shape
    nh = state.shape[1]
    bb = 8
    row = pl.BlockSpec((bb, 1, rwid), lambda i: (i, 0, 0))
    st = pl.BlockSpec((bb, nh, HEAD_DIM, HEAD_DIM), lambda i: (i, 0, 0, 0))
    return pl.pallas_call(
        functools.partial(_rwstep_kernel, bb=bb, nh=nh),
        grid=(nb // bb,),
        in_specs=[row] * 6 + [st],
        out_specs=[row, st],
        out_shape=[jax.ShapeDtypeStruct((nb, 1, rwid), F32), jax.ShapeDtypeStruct(state.shape, F32)],
        compiler_params=_cparams("parallel"),
        name="rwstep",
    )(r, ld, k, v, al, be, state)


def _outproj_kernel(x_ref, at_ref, y_ref, r_ref, k_ref, v_ref, g_ref, gt_ref, sh_ref, sc_ref, g2_ref,
                    wt_ref, wb_ref, rk_ref, lw_ref, lb_ref, bdm_ref, bd1_ref, rwt_ref, rb_ref, *rest):
    x1_ref, h2_ref, lg_ref = rest[-3:]
    y = y_ref[...]
    bdm = bdm_ref[...]
    mean = _dot_x(y, bdm)
    yc = y - mean
    var = _dot_x(yc * yc, bdm)
    yn = yc * lax.rsqrt(var + GN_EPS) * lw_ref[...] + lb_ref[...]
    v = v_ref[...]
    bonus = _dot_x(r_ref[...] * k_ref[...] * rk_ref[...], bd1_ref[...]) * v
    rw = (yn + bonus) * g_ref[...]
    mix = _mm(at_ref[...], wt_ref[...]) + _mm(rw, wb_ref[...])
    x1 = x_ref[...] + gt_ref[...] * mix
    x1_ref[...] = x1
    ms = jnp.mean(x1 * x1, axis=-1, keepdims=True)
    h2 = x1 * lax.rsqrt(ms + NORM_EPS) * g2_ref[...] * (1.0 + sc_ref[...]) + sh_ref[...]
    h2_ref[...] = h2
    lg_ref[...] = _dot3_nt(rwt_ref[...], h2) + rb_ref[...]


def _outproj(x3, attn, y, r, k, v, g, mod3, g2, wt_bf, wb_bf, r_k, ln_w, ln_b, bd_mean, bd_ones, rwt, rb, tm,
             tok_rows, tok_row0=0, tok_buf=None):
    nb, t, d = x3.shape
    rmod = mod3.shape[1]
    aw = attn.shape[2]
    ne = rwt.shape[0]
    nt = t // tm
    assert tok_row0 % tm == 0
    grid = (nb, nt)
    row = lambda b, i: (b, i, 0)
    const = lambda b, i: (0, 0)
    mod_spec = lambda s: pl.BlockSpec((None, rmod, d), (lambda b, i: (b, 0, s)) if rmod == 1 else (lambda b, i: (b, i, s)))
    half = pl.BlockSpec((None, tm, aw), row)
    vec = pl.BlockSpec((1, aw), const)
    aliased = [] if tok_buf is None else [tok_buf]
    return pl.pallas_call(
        _outproj_kernel,
        grid=grid,
        in_specs=[pl.BlockSpec((None, tm, d), row)] + [half] * 6 + [mod_spec(2), mod_spec(3), mod_spec(4),
                  pl.BlockSpec((1, d), const), pl.BlockSpec((aw, d), const), pl.BlockSpec((aw, d), const),
                  vec, vec, vec, pl.BlockSpec((aw, aw), const), pl.BlockSpec((aw, aw), const),
                  pl.BlockSpec((ne, d), const), pl.BlockSpec((ne, 1), const)]
                 + [pl.BlockSpec(memory_space=pl.ANY)] * len(aliased),
        out_specs=[pl.BlockSpec((None, tm, d), row),
                   pl.BlockSpec((tm, d), lambda b, i: (tok_row0 // tm + b * nt + i, 0)),
                   pl.BlockSpec((ne, tm), lambda b, i: (0, b * nt + i))],
        out_shape=[jax.ShapeDtypeStruct((nb, t, d), F32), jax.ShapeDtypeStruct((tok_rows, d), F32),
                   jax.ShapeDtypeStruct((ne, nb * t), F32)],
        input_output_aliases={20: 1} if aliased else {},
        compiler_params=_cparams("parallel", "parallel"),
        name="outproj",
    )(x3, attn, y, r, k, v, g, mod3, mod3, mod3, g2, wt_bf, wb_bf, r_k, ln_w, ln_b, bd_mean, bd_ones, rwt, rb,
      *aliased)


def _route_kernel(lg_ref, idx_ref, wt_ref, rank_ref, cnt_ref, carry_ref, *, n_valid):
    i = pl.program_id(0)
    ne, tn = lg_ref.shape

    @pl.when(i == 0)
    def _():
        carry_ref[...] = jnp.zeros_like(carry_ref)

    lg = lg_ref[...]
    eidx = lax.broadcasted_iota(I32, (ne, tn), 0).astype(F32)
    tok = i * tn + lax.broadcasted_iota(I32, (1, tn), 1)
    live = tok < n_valid
    vals, hots = [], []
    for _ in range(TOP_K):
        mx = jnp.max(lg, axis=0, keepdims=True)
        pick = jnp.min(jnp.where(lg == mx, eidx, float(ne)), axis=0, keepdims=True)
        hot = eidx == pick
        vals.append(mx)
        hots.append(hot)
        idx_ref[pl.ds(len(vals) - 1, 1), :] = pick.astype(I32)
        lg = jnp.where(hot, -jnp.inf, lg)
    ex = [jnp.exp(vv - vals[0]) for vv in vals]
    den = ex[0] + ex[1] + ex[2] + ex[3]
    for kk in range(TOP_K):
        wt_ref[pl.ds(kk, 1), :] = ex[kk] / den
    hot_all = jnp.zeros((ne, tn), F32)
    for hot in hots:
        hot_all = hot_all + jnp.where(hot & live, 1.0, 0.0)
    ri = lax.broadcasted_iota(I32, (tn, tn), 0)
    cj = lax.broadcasted_iota(I32, (tn, tn), 1)
    upper = jnp.where(ri <= cj, 1.0, 0.0).astype(BF16)
    incl = jnp.dot(hot_all.astype(BF16), upper, preferred_element_type=F32)
    before = carry_ref[:, 0:1] + incl - hot_all
    for kk, hot in enumerate(hots):
        rank_ref[pl.ds(kk, 1), :] = jnp.sum(jnp.where(hot, before, 0.0), axis=0, keepdims=True).astype(I32)
    carry_ref[...] = carry_ref[...] + jnp.sum(hot_all, axis=1, keepdims=True)
    cnt_ref[...] = carry_ref[...]


def _route(logits_t, n_valid):
    ne, npad = logits_t.shape
    tn = ROUTE_TILE
    tile = lambda rows: pl.BlockSpec((rows, tn), lambda i: (0, i))
    return pl.pallas_call(
        functools.partial(_route_kernel, n_valid=n_valid),
        grid=(npad // tn,),
        in_specs=[tile(ne)],
        out_specs=[tile(TOP_K), tile(TOP_K), tile(TOP_K), pl.BlockSpec((ne, LANES), lambda i: (0, 0))],
        out_shape=[jax.ShapeDtypeStruct((TOP_K, npad), I32), jax.ShapeDtypeStruct((TOP_K, npad), F32),
                   jax.ShapeDtypeStruct((TOP_K, npad), I32), jax.ShapeDtypeStruct((ne, LANES), F32)],
        scratch_shapes=[pltpu.VMEM((ne, LANES), F32)],
        compiler_params=_cparams("arbitrary"),
        name="route",
    )(logits_t)


def _dest_kernel(idx_ref, rank_ref, start_ref, dest_ref, *, n_valid, trash):
    i = pl.program_id(0)
    ne = start_ref.shape[0]
    tn = idx_ref.shape[1]
    eidx = lax.broadcasted_iota(I32, (ne, tn), 0)
    tok = i * tn + lax.broadcasted_iota(I32, (1, tn), 1)
    start = start_ref[:, 0:1]
    for kk in range(TOP_K):
        base = jnp.sum(jnp.where(eidx == idx_ref[pl.ds(kk, 1), :], start, 0.0), axis=0, keepdims=True)
        dest_ref[pl.ds(kk, 1), :] = jnp.where(tok < n_valid, base.astype(I32) + rank_ref[pl.ds(kk, 1), :], trash)


def _dest(idx, rank, start, n_valid, trash):
    _, npad = idx.shape
    ne = start.shape[0]
    tn = ROUTE_TILE
    tile = pl.BlockSpec((TOP_K, tn), lambda i: (0, i))
    return pl.pallas_call(
        functools.partial(_dest_kernel, n_valid=n_valid, trash=trash),
        grid=(npad // tn,),
        in_specs=[tile, tile, pl.BlockSpec((ne, LANES), lambda i: (0, 0))],
        out_specs=tile,
        out_shape=jax.ShapeDtypeStruct((TOP_K, npad), I32),
        compiler_params=_cparams("parallel"),
        name="dest",
    )(idx, rank, jnp.broadcast_to(start.astype(F32)[:, None], (ne, LANES)))


def _slotmap_kernel(start_ref, count_ref, dest_ref, map_ref, *, ne, trash_code):
    i = pl.program_id(0)

    @pl.when(i == 0)
    def _():
        def fill(s, carry):
            map_ref[s] = trash_code
            return carry

        for e in range(ne):
            first = start_ref[e] + count_ref[e]
            last = start_ref[e + 1] if e + 1 < ne else map_ref.shape[0]
            lax.fori_loop(first, last, fill, 0)

    def put(j, carry):
        for kk in range(TOP_K):
            map_ref[dest_ref[0, kk, j]] = (i * TOK_TILE + j) * TOP_K + kk
        return carry

    lax.fori_loop(0, TOK_TILE, put, 0, unroll=8)


def _slotmap(start, counts, dest3, n_slots, trash_code):
    n_tiles = dest3.shape[0]
    grid_spec = pltpu.PrefetchScalarGridSpec(
        num_scalar_prefetch=2,
        grid=(n_tiles,),
        in_specs=[pl.BlockSpec((1, TOP_K, TOK_TILE), lambda i, st, ct: (i, 0, 0), memory_space=pltpu.SMEM)],
        out_specs=pl.BlockSpec(memory_space=pltpu.SMEM),
    )
    return pl.pallas_call(
        functools.partial(_slotmap_kernel, ne=start.shape[0], trash_code=trash_code),
        grid_spec=grid_spec,
        out_shape=jax.ShapeDtypeStruct((n_slots,), I32),
        compiler_params=_cparams("arbitrary"),
        name="slotmap",
    )(start, counts, dest3)


def _decode_kernel(code_ref, tok_ref, row_ref, *, trash_code, y_stride):
    code = code_ref[...]
    nblk, blk = code.shape
    k_bits = TOP_K.bit_length() - 1
    assert TOP_K == 1 << k_bits
    blk_i = lax.broadcasted_iota(I32, (nblk, blk), 0)
    buf = jnp.where(blk_i == nblk - 1, N_OBUF - 1, lax.rem(blk_i, N_OBUF))
    trash_row = TOP_K * y_stride + buf * blk + lax.broadcasted_iota(I32, (nblk, blk), 1)
    pad = code >= trash_code
    tok = code >> k_bits
    tok_ref[...] = jnp.where(pad, 0, tok)
    row_ref[...] = jnp.where(pad, trash_row, (code & (TOP_K - 1)) * y_stride + tok)


def _decode(codes, trash_code, y_stride):
    shp = jax.ShapeDtypeStruct(codes.shape, I32)
    return pl.pallas_call(
        functools.partial(_decode_kernel, trash_code=trash_code, y_stride=y_stride),
        out_shape=[shp, shp],
        compiler_params=pltpu.CompilerParams(vmem_limit_bytes=VMEM_LIMIT),
        name="decode",
    )(codes)


N_XBUF = 2
N_OBUF = 3


def _moe_kernel(be_ref, nact_ref, t0_ref, t1_ref, t2_ref, rprev_ref, rcur_ref, h_ref, wgu_ref, bgu_ref, wd_ref,
                bd_ref, y_ref, xbuf, xb_ref, obuf, gsem, ssem, wgu_bf, wd_bf, *, dff):
    i = pl.program_id(0)
    nact = nact_ref[0]
    blk = MOE_BLOCK

    def gather(t_ref, j, slot):
        return pltpu.make_async_copy(h_ref.at[pl.ds(t_ref[0, 0, j], 1), :], xbuf.at[slot, pl.ds(j, 1), :],
                                     gsem.at[slot])

    def scatter(r_ref, j, slot):
        return pltpu.make_async_copy(obuf.at[slot, pl.ds(j, 1), :], y_ref.at[pl.ds(r_ref[0, 0, j], 1), :],
                                     ssem.at[slot])

    def wait_gathers(slot):
        pltpu.make_async_copy(h_ref.at[pl.ds(0, blk), :], xbuf.at[slot], gsem.at[slot]).wait()

    def wait_scatters(slot):
        pltpu.make_async_copy(obuf.at[slot], y_ref.at[pl.ds(0, blk), :], ssem.at[slot]).wait()

    @pl.when(i == 0)
    def _():
        obuf[...] = jnp.zeros_like(obuf)
        for j in range(blk):
            gather(t0_ref, j, 0).start(priority=j % 2)
        for j in range(blk):
            gather(t1_ref, j, 1).start(priority=j % 2)

    @pl.when(i < nact)
    def _():
        wait_gathers(i % N_XBUF)

        @pl.when(i >= N_OBUF - 1)
        def _():
            wait_scatters(i % N_OBUF)

        @pl.when((i == 0) | (be_ref[i] != be_ref[jnp.maximum(i - 1, 0)]))
        def _():
            wgu_bf[...] = wgu_ref[...].astype(BF16)
            wd_bf[...] = wd_ref[...].astype(BF16)

        xb_ref[...] = xbuf[i % N_XBUF].astype(BF16)
        for j in range(blk):
            gather(t2_ref, j, i % N_XBUF).start(priority=j % 2)
        for j in range(blk):
            scatter(rprev_ref, j, (i + N_OBUF - 1) % N_OBUF).start(priority=j % 2)
        gu = jnp.dot(xb_ref[...], wgu_bf[...], preferred_element_type=F32) + bgu_ref[...]
        gate = jnp.minimum(gu[:, :dff], SWIGLU_LIMIT)
        up = jnp.clip(gu[:, dff:], -SWIGLU_LIMIT, SWIGLU_LIMIT)
        act = (up + 1.0) * gate * _sigmoid(gate * SWIGLU_ALPHA)
        obuf[i % N_OBUF] = jnp.dot(act.astype(BF16), wd_bf[...], preferred_element_type=F32) + bd_ref[...]

        @pl.when(i == nact - 1)
        def _():
            for j in range(blk):
                scatter(rcur_ref, j, i % N_OBUF).start(priority=j % 2)
            wait_gathers(0)
            wait_gathers(1)
            wait_scatters(i % N_OBUF)
            wait_scatters((i + N_OBUF - 1) % N_OBUF)

            @pl.when(i >= 1)
            def _():
                wait_scatters((i + N_OBUF - 2) % N_OBUF)


def _moe(blk_e, nact, slot_map, tok_buf, w_gu, b_gu, w_down, b_down, n_blocks, y_stride):
    ne, d, dff2 = w_gu.shape
    dff = dff2 // 2
    blk = MOE_BLOCK
    trash_code = TOP_K * y_stride
    codes = jnp.concatenate([slot_map[:n_blocks * blk].reshape(n_blocks, blk), jnp.full((1, blk), trash_code, I32)])
    tok_tab, row_tab = (a.reshape(n_blocks + 1, 1, blk) for a in _decode(codes, trash_code, y_stride))
    smem = lambda f: pl.BlockSpec((1, 1, blk), f, memory_space=pltpu.SMEM)
    ahead = lambda n: (lambda i, be, na: (jnp.minimum(i + n, na[0] - 1), 0, 0))
    grid_spec = pltpu.PrefetchScalarGridSpec(
        num_scalar_prefetch=2,
        grid=(n_blocks,),
        in_specs=[smem(ahead(0)), smem(ahead(1)), smem(ahead(2)),
                  smem(lambda i, be, na: (jnp.where(i == 0, n_blocks, i - 1), 0, 0)),
                  smem(lambda i, be, na: (i, 0, 0)),
                  pl.BlockSpec(memory_space=pl.ANY),
                  pl.BlockSpec((None, d, dff2), lambda i, be, na: (be[i], 0, 0)),
                  pl.BlockSpec((None, 1, dff2), lambda i, be, na: (be[i], 0, 0)),
                  pl.BlockSpec((None, dff, d), lambda i, be, na: (be[i], 0, 0)),
                  pl.BlockSpec((None, 1, d), lambda i, be, na: (be[i], 0, 0))],
        out_specs=pl.BlockSpec(memory_space=pl.ANY),
        scratch_shapes=[pltpu.VMEM((N_XBUF, blk, d), F32), pltpu.VMEM((blk, d), BF16),
                        pltpu.VMEM((N_OBUF, blk, d), F32),
                        pltpu.SemaphoreType.DMA((N_XBUF,)), pltpu.SemaphoreType.DMA((N_OBUF,)),
                        pltpu.VMEM((d, dff2), BF16), pltpu.VMEM((dff, d), BF16)],
    )
    return pl.pallas_call(
        functools.partial(_moe_kernel, dff=dff),
        grid_spec=grid_spec,
        out_shape=jax.ShapeDtypeStruct((TOP_K * y_stride + N_OBUF * blk, d), F32),
        compiler_params=_cparams("arbitrary"),
        name="moe",
    )(blk_e, nact, tok_tab, tok_tab, tok_tab, row_tab, row_tab, tok_buf, w_gu, b_gu.reshape(ne, 1, dff2), w_down,
      b_down.reshape(ne, 1, d))


def _combine_kernel(y0_ref, y1_ref, y2_ref, y3_ref, w_ref, x1_ref, gt_ref, o_ref, *, rows):
    w = w_ref[...]
    y = w[:rows, 0:1] * y0_ref[...]
    for kk, y_ref in enumerate((y1_ref, y2_ref, y3_ref), start=1):
        y = y + w[:rows, kk:kk + 1] * y_ref[...]
    o_ref[...] = x1_ref[...] + gt_ref[...] * y


def _combine(ys, wt3, x1, mod3, tok_rows, tile0, rows):
    nb, t, d = x1.shape
    rmod = mod3.shape[1]
    nt = t // rows
    row0 = tile0 * TOK_TILE
    assert tok_rows % rows == 0 and row0 % rows == 0
    row = lambda b, i: (b, i, 0)
    gate = pl.BlockSpec((None, rmod, d), (lambda b, i: (b, 0, 5)) if rmod == 1 else (lambda b, i: (b, i, 5)))
    expert_out = lambda kk: pl.BlockSpec((rows, d), lambda b, i: ((kk * tok_rows + row0) // rows + b * nt + i, 0))
    return pl.pallas_call(
        functools.partial(_combine_kernel, rows=rows),
        grid=(nb, nt),
        in_specs=[expert_out(kk) for kk in range(TOP_K)]
                 + [pl.BlockSpec((None, TOK_TILE, TOP_K), lambda b, i: (tile0 + (b * nt + i) * rows // TOK_TILE, 0, 0)),
                    pl.BlockSpec((None, rows, d), row), gate],
        out_specs=pl.BlockSpec((None, rows, d), row),
        out_shape=jax.ShapeDtypeStruct((nb, t, d), F32),
        compiler_params=_cparams("parallel", "parallel"),
        name="combine",
    )(ys, ys, ys, ys, wt3, x1, mod3)


def _rope_tables(pos, n_heads):
    half = HEAD_DIM // 2
    inv_freq = 1.0 / (ROPE_THETA ** (jnp.arange(0, HEAD_DIM, 2, dtype=F32) / HEAD_DIM))
    ang = pos.astype(F32)[:, None] * inv_freq[None, :]
    cos, sin = jnp.cos(ang), jnp.sin(ang)
    del half
    return (jnp.tile(jnp.concatenate([cos, cos], axis=-1), (1, n_heads)),
            jnp.tile(jnp.concatenate([-sin, sin], axis=-1), (1, n_heads)))


def _block_diag(width, value):
    h = np.arange(width) // HEAD_DIM
    return jnp.asarray(np.where(h[:, None] == h[None, :], value, 0.0), F32)


def kernel(x_prompt, x_sample, cache_k, cache_v, state_wkv, state_shift, c_prompt, c_sample, w_ada, b_ada, norm1_g, norm2_g, w_in, q_norm_g, k_norm_g, rwkv_mu, rwkv_w0, rwkv_w2, rwkv_a0, rwkv_a2, rwkv_g2, rwkv_k_k, rwkv_k_a, rwkv_r_k, rwkv_ln_w, rwkv_ln_b, w_out, router_w, router_b, moe_w_gu, moe_b_gu, moe_w_down, moe_b_down):
    nbp, t, d = x_prompt.shape
    nbs, ts, _ = x_sample.shape
    depth = w_ada.shape[0]
    assert depth == 1 and ts == 1
    n_heads = cache_k.shape[3]
    aw = n_heads * HEAD_DIM
    rwid = rwkv_w0.shape[1]
    rw_heads = rwid // HEAD_DIM
    rwc = rwkv_mu.shape[1]
    past = cache_k.shape[2]
    ne = router_w.shape[2]
    keep = min(MAX_WINDOW, t)
    lyr = 0

    w_in_bf = w_in[lyr].astype(BF16)
    wt_bf = w_out[lyr][:aw].astype(BF16)
    wb_bf = w_out[lyr][aw:].astype(BF16)
    g1 = norm1_g[lyr].reshape(1, d)
    g2 = norm2_g[lyr].reshape(1, d)
    qg = jnp.tile(q_norm_g[lyr], n_heads).reshape(1, aw)
    kg = jnp.tile(k_norm_g[lyr], n_heads).reshape(1, aw)
    bd_mean_a = _block_diag(aw, 1.0 / HEAD_DIM)
    bd_mean_r = _block_diag(rwid, 1.0 / HEAD_DIM)
    bd_ones_r = _block_diag(rwid, 1.0)
    dl = rwkv_w2.shape[1]
    w2p = jnp.zeros((LANES, rwid), F32).at[:dl].set(rwkv_w2[lyr])
    a2p = jnp.zeros((LANES, rwid), F32).at[dl:dl + rwkv_a2.shape[1]].set(rwkv_a2[lyr])
    vec = lambda a: a[lyr].reshape(1, -1)
    rwt = router_w[lyr].T
    rb = router_b[lyr].reshape(ne, 1)

    rows_c = nbp + nbs
    rows_pad = -(-rows_c // 8) * 8
    c_all = jnp.zeros((rows_pad, d), F32).at[:nbp].set(c_prompt).at[nbp:rows_c].set(c_sample)
    mod = _ada(c_all, w_ada[lyr], b_ada[lyr])
    mod_p = mod[:nbp].reshape(nbp, 1, 6 * d)
    mod_s = mod[nbp:rows_c].reshape(1, nbs, 6 * d)
    xs3 = x_sample.reshape(1, nbs, d)

    cos_p, sin_p = _rope_tables(jnp.arange(t), n_heads)
    cos_s, sin_s = _rope_tables(jnp.full((nbs,), PAST_LEN), n_heads)
    qp, kp, vp, rwp, kt_p, vt_p = _inproj(x_prompt, mod_p, g1, w_in_bf, cos_p, sin_p, qg, kg, bd_mean_a,
                                          ROW_TILE, keep=keep)
    qs, ks, vs, rws = _inproj(xs3, mod_s, g1, w_in[lyr], cos_s, sin_s, qg, kg, bd_mean_a, nbs)

    attn_p = _attn_prompt(qp, kp, vp)
    as_rows = lambda a: a.reshape(nbs, 1, -1)
    cache_t = lambda cch: jnp.transpose(cch[lyr], (0, 2, 3, 1)).reshape(nbs, aw, past)
    attn_s = _attn_sample(as_rows(qs), as_rows(ks), as_rows(vs), cache_t(cache_k), cache_t(cache_v))

    rw_args = (vec(rwkv_mu), vec(rwkv_w0), w2p, vec(rwkv_a0), a2p, rwkv_g2[lyr], vec(rwkv_k_k), vec(rwkv_k_a), bd_ones_r)
    pre_p = _rwprep(rwp, jnp.zeros((nbp, 1, rwc), F32), *rw_args, tm=ROW_TILE, whole_prev=False)
    pre_s = _rwprep(rws, state_shift[lyr].reshape(1, nbs, rwc), *rw_args, tm=nbs, whole_prev=True)
    r_p, ld_p, k_p, v_p, al_p, be_p, g_p = pre_p
    y_p, wkv_p = _rwchunk(r_p, ld_p, k_p, v_p, al_p, be_p)
    r_s, ld_s, k_s, v_s, al_s, be_s, g_s = pre_s
    y_s, wkv_s = _rwstep(*(as_rows(a) for a in (r_s, ld_s, k_s, v_s, al_s, be_s)), state_wkv[lyr])
    y_s = y_s.reshape(1, nbs, rwid)

    op_args = (rwkv_r_k[lyr].reshape(1, rwid), vec(rwkv_ln_w), vec(rwkv_ln_b), bd_mean_r, bd_ones_r, rwt, rb)
    n_p = nbp * t
    n_valid = n_p + nbs
    assert n_p % ROUTE_TILE == 0 and ROUTE_TILE % TOK_TILE == 0 and n_p % nbs == 0
    npad = -(-n_valid // ROUTE_TILE) * ROUTE_TILE
    x1_p, tok_buf, lg_p = _outproj(x_prompt, attn_p, y_p, r_p, k_p, v_p, g_p, mod_p, g2, wt_bf, wb_bf, *op_args,
                                   tm=ROW_TILE, tok_rows=npad)
    x1_s, tok_buf, lg_s = _outproj(xs3, attn_s.reshape(1, nbs, aw), y_s, r_s, k_s, v_s, g_s, mod_s, g2,
                                   w_out[lyr][:aw], w_out[lyr][aw:], *op_args, tm=nbs, tok_rows=npad,
                                   tok_row0=n_p, tok_buf=tok_buf)

    logits_t = jnp.zeros((ne, npad), F32).at[:, :n_p].set(lg_p).at[:, n_p:n_valid].set(lg_s)
    idx, wts, rank, cnt = _route(logits_t, n_valid)
    counts = cnt[:, 0].astype(I32)
    blk = MOE_BLOCK
    n_blocks = -(-(n_valid * TOP_K + ne * (blk - 1)) // blk)
    padded = (counts + blk - 1) // blk * blk
    pad_end = jnp.cumsum(padded)
    start = pad_end - padded
    blk_e = jnp.minimum(jnp.sum(pad_end[None, :] <= (jnp.arange(n_blocks) * blk)[:, None], axis=1), ne - 1).astype(I32)
    nact = (pad_end[-1] // blk).reshape(1).astype(I32)
    trash = n_blocks * blk
    dest = _dest(idx, rank, start.astype(I32), n_valid, trash)
    n_tiles = npad // TOK_TILE
    dest3 = dest.reshape(TOP_K, n_tiles, TOK_TILE).transpose(1, 0, 2)
    wt3 = wts.reshape(TOP_K, n_tiles, TOK_TILE).transpose(1, 2, 0)
    slot_map = _slotmap(start.astype(I32), counts, dest3, trash + 8, npad * TOP_K)
    ys = _moe(blk_e, nact, slot_map, tok_buf, moe_w_gu[lyr], moe_b_gu[lyr], moe_w_down[lyr], moe_b_down[lyr],
              n_blocks, npad)
    y_prompt = _combine(ys, wt3, x1_p, mod_p, npad, 0, TOK_TILE)
    y_sample = _combine(ys, wt3, x1_s, mod_s, npad, n_p // TOK_TILE, nbs)

    kept = lambda a: jnp.transpose(a.reshape(nbp, n_heads, HEAD_DIM, keep), (0, 3, 1, 2))[None]
    return (y_prompt, y_sample.reshape(nbs, ts, d), kept(kt_p), kept(vt_p), wkv_p[None], rwp[:, t - 1][None],
            ks.reshape(nbs, ts, n_heads, HEAD_DIM)[None], vs.reshape(nbs, ts, n_heads, HEAD_DIM)[None],
            wkv_s[None], rws.reshape(nbs, rwc)[None])
```

```python
import functools

import numpy as np
import jax
import jax.numpy as jnp
from jax import lax
from jax.experimental import pallas as pl
from jax.experimental.pallas import tpu as pltpu

F32 = jnp.float32
BF16 = jnp.bfloat16
I32 = jnp.int32

HEAD_DIM = 64
LANES = 128
SUBLANES = 8
DILATED_PATTERNS = ((128, 1), (512, 4), (2048, 16))
WINDOW_STEPS = 128
MAX_WINDOW = 2048
PAST_LEN = 16384
ROPE_THETA = 10000.0
NORM_EPS = 1e-6
GN_EPS = 64e-5
TOP_K = 4
SWIGLU_ALPHA = 1.702
SWIGLU_LIMIT = 7.0
RW_CHUNK = 128
MOE_BLOCK = 512
ATTN_UNITS = 4
ROW_TILE = 512
TOK_TILE = 256
ROUTE_TILE = 512
VMEM_LIMIT = 56 * 1024 * 1024
NEG_BIG = -1e30


def _cparams(*sem):
    return pltpu.CompilerParams(dimension_semantics=sem, vmem_limit_bytes=VMEM_LIMIT)


def _dot(a, b):
    return jnp.dot(a.astype(BF16), b.astype(BF16), preferred_element_type=F32)


def _dot_nt(a, b):
    return lax.dot_general(a.astype(BF16), b.astype(BF16), (((1,), (1,)), ((), ())),
                           preferred_element_type=F32)


def _split2(a):
    hi = a.astype(BF16)
    lo = (a - hi.astype(F32)).astype(BF16)
    return hi, lo


def _split3(a):
    hi = a.astype(BF16)
    r1 = a - hi.astype(F32)
    mid = r1.astype(BF16)
    lo = (r1 - mid.astype(F32)).astype(BF16)
    return hi, mid, lo


def _dot_x(a, e):
    e = e.astype(BF16)
    hi, mid, lo = _split3(a)
    return (jnp.dot(hi, e, preferred_element_type=F32) + jnp.dot(mid, e, preferred_element_type=F32)
            + jnp.dot(lo, e, preferred_element_type=F32))


def _xdot(e, a):
    e = e.astype(BF16)
    hi, mid, lo = _split3(a)
    return (jnp.dot(e, hi, preferred_element_type=F32) + jnp.dot(e, mid, preferred_element_type=F32)
            + jnp.dot(e, lo, preferred_element_type=F32))


def _dot3(a, b):
    ah, al = _split2(a)
    bh, bl = _split2(b)
    return (jnp.dot(ah, bh, preferred_element_type=F32) + jnp.dot(ah, bl, preferred_element_type=F32)
            + jnp.dot(al, bh, preferred_element_type=F32))


def _dot3_nt(a, b):
    ah, al = _split2(a)
    bh, bl = _split2(b)
    dn = (((1,), (1,)), ((), ()))
    return (lax.dot_general(ah, bh, dn, preferred_element_type=F32)
            + lax.dot_general(ah, bl, dn, preferred_element_type=F32)
            + lax.dot_general(al, bh, dn, preferred_element_type=F32))


def _mm(a, w):
    if w.dtype == BF16:
        return jnp.dot(a.astype(BF16), w, preferred_element_type=F32)
    return _dot3(a, w)


def _store_token_tiles(ref, val, lead=()):
    n = val.shape[0]
    for s in range(SUBLANES):
        ref[(*lead, pl.ds(s, n, stride=SUBLANES), slice(None))] = val[:, s * LANES:(s + 1) * LANES]


def _load_token_tiles(ref, n, lead=()):
    return jnp.concatenate([ref[(*lead, pl.ds(s, n, stride=SUBLANES), slice(None))] for s in range(SUBLANES)],
                           axis=1)


def _sigmoid(x):
    return 1.0 / (1.0 + jnp.exp(-x))


def _ada_kernel(c_ref, w_ref, b_ref, o_ref):
    c = c_ref[...]
    o_ref[...] = _dot3(c * _sigmoid(c), w_ref[...]) + b_ref[...]


def _ada(c_all, w_ada, b_ada):
    rows, d = c_all.shape
    n = w_ada.shape[1]
    tn = n // 4
    return pl.pallas_call(
        _ada_kernel,
        grid=(n // tn,),
        in_specs=[pl.BlockSpec((rows, d), lambda j: (0, 0)),
                  pl.BlockSpec((d, tn), lambda j: (0, j)),
                  pl.BlockSpec((1, tn), lambda j: (0, j))],
        out_specs=pl.BlockSpec((rows, tn), lambda j: (0, j)),
        out_shape=jax.ShapeDtypeStruct((rows, n), F32),
        compiler_params=_cparams("arbitrary"),
        name="ada",
    )(c_all, w_ada, b_ada.reshape(1, n))


def _inproj_kernel(x_ref, sh_ref, sc_ref, g_ref, w_ref, cos_ref, sin_ref, qg_ref, kg_ref, bd_ref,
                   q_ref, k_ref, v_ref, rw_ref, *maybe_kv_t, aw, first_kept):
    x = x_ref[...]
    ms = jnp.mean(x * x, axis=-1, keepdims=True)
    h = x * lax.rsqrt(ms + NORM_EPS) * g_ref[...] * (1.0 + sc_ref[...]) + sh_ref[...]
    proj = _mm(h, w_ref[...])
    cos = cos_ref[...]
    sin = sin_ref[...]
    lane = lax.broadcasted_iota(I32, (1, aw), 1)
    first_half = (lane % HEAD_DIM) < (HEAD_DIM // 2)
    bd = bd_ref[...]

    def norm_rope(t, g):
        tn = t * lax.rsqrt((_dot if w_ref.dtype == BF16 else _dot_x)(t * t, bd) + NORM_EPS) * g
        rot = jnp.where(first_half, pltpu.roll(tn, aw - HEAD_DIM // 2, 1), pltpu.roll(tn, HEAD_DIM // 2, 1))
        return tn * cos + rot * sin

    scale = 1.0 / np.sqrt(HEAD_DIM).astype(np.float32)
    q_ref[...] = norm_rope(proj[:, :aw], qg_ref[...]) * scale
    k = norm_rope(proj[:, aw:2 * aw], kg_ref[...])
    v = proj[:, 2 * aw:3 * aw]
    k_ref[...] = k
    v_ref[...] = v
    rw_ref[...] = proj[:, 3 * aw:]
    if maybe_kv_t:
        kt_ref, vt_ref = maybe_kv_t
        kept = pl.program_id(1) >= first_kept

        @pl.when(kept)
        def _():
            kt_ref[...] = k.T
            vt_ref[...] = v.T

        @pl.when(jnp.logical_not(kept))
        def _():
            kt_ref[...] = jnp.zeros_like(kt_ref)
            vt_ref[...] = jnp.zeros_like(vt_ref)


def _inproj(x3, mod3, g1, w_in_bf, cos_t, sin_t, qg, kg, bd_mean, tm, keep=0):
    nb, t, d = x3.shape
    r = mod3.shape[1]
    ncol = w_in_bf.shape[1]
    aw = cos_t.shape[1]
    rwc = ncol - 3 * aw
    grid = (nb, t // tm)
    row = lambda b, i: (b, i, 0)
    const = lambda b, i: (0, 0)
    mod_spec = lambda s: pl.BlockSpec((None, r, d), (lambda b, i: (b, 0, s)) if r == 1 else (lambda b, i: (b, i, s)))
    out_specs = [pl.BlockSpec((None, tm, aw), row)] * 3 + [pl.BlockSpec((None, tm, rwc), row)]
    out_shape = [jax.ShapeDtypeStruct((nb, t, aw), F32)] * 3 + [jax.ShapeDtypeStruct((nb, t, rwc), F32)]
    first_kept = (t - keep) // tm
    if keep:
        assert keep % tm == 0 and (t - keep) % tm == 0
        kept_spec = pl.BlockSpec((None, aw, tm), lambda b, i: (b, 0, jnp.maximum(i - first_kept, 0)))
        out_specs += [kept_spec, kept_spec]
        out_shape += [jax.ShapeDtypeStruct((nb, aw, keep), F32)] * 2
    outs = pl.pallas_call(
        functools.partial(_inproj_kernel, aw=aw, first_kept=first_kept),
        grid=grid,
        in_specs=[pl.BlockSpec((None, tm, d), row), mod_spec(0), mod_spec(1),
                  pl.BlockSpec((1, d), const), pl.BlockSpec((d, ncol), const),
                  pl.BlockSpec((tm, aw), lambda b, i: (i, 0)), pl.BlockSpec((tm, aw), lambda b, i: (i, 0)),
                  pl.BlockSpec((1, aw), const), pl.BlockSpec((1, aw), const), pl.BlockSpec((aw, aw), const)],
        out_specs=out_specs,
        out_shape=out_shape,
        compiler_params=_cparams("parallel", "arbitrary"),
        name="inproj",
    )(x3, mod3, mod3, g1, w_in_bf, cos_t, sin_t, qg, kg, bd_mean)
    return outs


def _attn_prompt_kernel(q_ref, k_ref, v_ref, o_ref, kp_ref, vp_ref, m_ref, l_ref, acc_ref, *, t, pad):
    nq = WINDOW_STEPS
    nk = 2 * WINDOW_STEPS
    kp_ref[pl.ds(0, pad), :] = jnp.zeros((pad, LANES), F32)
    vp_ref[pl.ds(0, pad), :] = jnp.zeros((pad, LANES), F32)
    kp_ref[pl.ds(pad, t), :] = k_ref[...]
    vp_ref[pl.ds(pad, t), :] = v_ref[...]
    head0 = lax.broadcasted_iota(I32, (1, LANES), 1) < HEAD_DIM
    qi = lax.broadcasted_iota(I32, (nq, nk), 0)
    kj = lax.broadcasted_iota(I32, (nq, nk), 1)
    steps_back = qi + nq - kj
    band = (steps_back >= 0) & (steps_back <= WINDOW_STEPS)
    has_past = kj >= nq

    hsels = (head0, jnp.logical_not(head0))
    nt = (((1,), (1,)), ((), ()))

    for p, (_, d) in enumerate(DILATED_PATTERNS):
        def units(g, carry, p=p, d=d):
            rows_q, kb, vb, valid, qh = [], [], [], [], []
            for j in range(ATTN_UNITS):
                u = g * ATTN_UNITS + j
                res = u % d
                blk = u // d
                q_start = res + d * nq * blk
                k_start = pad + q_start - d * nq
                if d == 1:
                    rows_q.append(pl.ds(q_start, nq))
                    rows_k = pl.ds(k_start, nk)
                else:
                    rows_q.append(pl.ds(q_start, nq, stride=d))
                    rows_k = pl.ds(k_start, nk, stride=d)
                q = q_ref[rows_q[j], :]
                kb.append(kp_ref[rows_k, :].astype(BF16))
                vb.append(vp_ref[rows_k, :].astype(BF16))
                valid.append(band & (has_past | (blk > 0)))
                qh.append([jnp.where(hsel, q, 0.0).astype(BF16) for hsel in hsels])
            chains = [(j, h) for j in range(ATTN_UNITS) for h in range(2)]
            s = [lax.dot_general(qh[j][h], kb[j], nt, preferred_element_type=F32) for j, h in chains]
            s = [jnp.where(valid[j], sc, NEG_BIG) for (j, h), sc in zip(chains, s)]
            mx = [jnp.max(sc, axis=-1, keepdims=True) for sc in s]
            e = [jnp.exp(sc - m) for sc, m in zip(s, mx)]
            den = [jnp.sum(ec, axis=-1, keepdims=True) for ec in e]
            o = [jnp.dot(ec.astype(BF16), vb[j], preferred_element_type=F32) for (j, h), ec in zip(chains, e)]
            for j in range(ATTN_UNITS):
                m_ref[p, rows_q[j], :] = jnp.where(head0, mx[2 * j], mx[2 * j + 1])
                l_ref[p, rows_q[j], :] = jnp.where(head0, den[2 * j], den[2 * j + 1])
                acc_ref[p, rows_q[j], :] = jnp.where(head0, o[2 * j], o[2 * j + 1])
            return carry

        lax.fori_loop(0, t // nq // ATTN_UNITS, units, 0)

    rows = 256

    def merge(i, carry):
        sl = pl.ds(pl.multiple_of(i * rows, rows), rows)
        m0, m1, m2 = m_ref[0, sl, :], m_ref[1, sl, :], m_ref[2, sl, :]
        mm = jnp.maximum(jnp.maximum(m0, m1), m2)
        w0, w1, w2 = jnp.exp(m0 - mm), jnp.exp(m1 - mm), jnp.exp(m2 - mm)
        num = w0 * acc_ref[0, sl, :] + w1 * acc_ref[1, sl, :] + w2 * acc_ref[2, sl, :]
        den = w0 * l_ref[0, sl, :] + w1 * l_ref[1, sl, :] + w2 * l_ref[2, sl, :]
        o_ref[sl, :] = num / den
        return carry

    lax.fori_loop(0, t // rows, merge, 0)


def _attn_prompt(q, k, v):
    nb, t, aw = q.shape
    pad = MAX_WINDOW
    assert t % MAX_WINDOW == 0
    spec = pl.BlockSpec((None, t, LANES), lambda b, hp: (b, 0, hp))
    return pl.pallas_call(
        functools.partial(_attn_prompt_kernel, t=t, pad=pad),
        grid=(nb, aw // LANES),
        in_specs=[spec, spec, spec],
        out_specs=spec,
        out_shape=jax.ShapeDtypeStruct((nb, t, aw), F32),
        scratch_shapes=[pltpu.VMEM((pad + t, LANES), F32), pltpu.VMEM((pad + t, LANES), F32),
                        pltpu.VMEM((3, t, LANES), F32), pltpu.VMEM((3, t, LANES), F32),
                        pltpu.VMEM((3, t, LANES), F32)],
        compiler_params=_cparams("parallel", "parallel"),
        name="attn_prompt",
    )(q, k, v)


def _attn_sample_kernel(q_ref, kn_ref, vn_ref, kt_ref, vt_ref, o_ref, *, nh, w):
    aw = nh * HEAD_DIM
    dist = w - lax.broadcasted_iota(I32, (1, w), 1)
    mult = jnp.zeros((1, w), F32)
    for win, d in DILATED_PATTERNS:
        mult = mult + jnp.where((dist % d == 0) & (dist <= win), 1.0, 0.0)
    n_pat = float(len(DILATED_PATTERNS))
    q_col = _col(q_ref[...], aw)
    kn_col = _col(kn_ref[...], aw)
    vn_col = _col(vn_ref[...], aw)
    outs = []
    for h in range(nh):
        hs = pl.ds(h * HEAD_DIM, HEAD_DIM)
        qh = q_col[h * HEAD_DIM:(h + 1) * HEAD_DIM]
        s = jnp.sum(kt_ref[hs, :] * qh, axis=0, keepdims=True)
        s_self = jnp.sum(qh * kn_col[h * HEAD_DIM:(h + 1) * HEAD_DIM], axis=0, keepdims=True)
        s = jnp.where(mult > 0.0, s, NEG_BIG)
        mx = jnp.maximum(jnp.max(s, axis=1, keepdims=True), s_self)
        pr = mult * jnp.exp(s - mx)
        p_self = n_pat * jnp.exp(s_self - mx)
        den = jnp.sum(pr, axis=1, keepdims=True) + p_self
        num = jnp.sum(vt_ref[hs, :] * pr, axis=1, keepdims=True) + p_self * vn_col[h * HEAD_DIM:(h + 1) * HEAD_DIM]
        outs.append(num / den)
    o_ref[...] = _row(jnp.concatenate(outs, axis=0), aw)


def _attn_sample(q, kn, vn, cache_kt, cache_vt):
    nb, _, aw = q.shape
    w = cache_kt.shape[2]
    assert w == MAX_WINDOW
    row = pl.BlockSpec((None, 1, aw), lambda i: (i, 0, 0))
    mat = pl.BlockSpec((None, aw, w), lambda i: (i, 0, 0))
    return pl.pallas_call(
        functools.partial(_attn_sample_kernel, nh=aw // HEAD_DIM, w=w),
        grid=(nb,),
        in_specs=[row, row, row, mat, mat],
        out_specs=row,
        out_shape=jax.ShapeDtypeStruct((nb, 1, aw), F32),
        compiler_params=_cparams("parallel"),
        name="attn_sample",
    )(q, kn, vn, cache_kt, cache_vt)


def _rwprep_kernel(p_ref, prev_ref, first_ref, mu_ref, w0_ref, w2_ref, a0_ref, a2_ref, g2_ref, kk_ref, ka_ref,
                   bd_ref, r_ref, ld_ref, k_ref, v_ref, al_ref, be_ref, g_ref, *, rwid, whole_prev):
    p = p_ref[...]
    if whole_prev:
        prev = first_ref[...]
    else:
        tm = p.shape[0]
        before = jnp.where(pl.program_id(1) == 0, first_ref[...], prev_ref[7:8, :])
        rowi = lax.broadcasted_iota(I32, (tm, 1), 0)
        prev = jnp.where(rowi == 0, before, pltpu.roll(p, 1, 0))
    xs = p + mu_ref[...] * (prev - p)
    r = xs[:, :rwid]
    k = xs[:, rwid:2 * rwid]
    v = xs[:, 2 * rwid:3 * rwid]
    xwa = xs[:, 3 * rwid:3 * rwid + LANES]
    xg = xs[:, 3 * rwid + LANES:]
    z = w0_ref[...] + _dot3(jnp.tanh(xwa), w2_ref[...])
    softplus_neg = jnp.maximum(-z, 0.0) + jnp.log(1.0 + jnp.exp(-jnp.abs(z)))
    w = -softplus_neg - 0.5
    a = _sigmoid(a0_ref[...] + _dot3(xwa, a2_ref[...]))
    g = _dot3(_sigmoid(xg), g2_ref[...])
    kk = k * kk_ref[...]
    norm = jnp.sqrt(_dot_x(kk * kk, bd_ref[...]))
    kk = kk / jnp.maximum(norm, 1e-12)
    r_ref[...] = r
    ld_ref[...] = -jnp.exp(w)
    k_ref[...] = k * (1.0 + (a - 1.0) * ka_ref[...])
    v_ref[...] = v
    al_ref[...] = -kk
    be_ref[...] = kk * a
    g_ref[...] = g


def _rwprep(rw, first, mu, w0, w2p, a0, a2p, g2, k_k, k_a, bd_ones, tm, whole_prev):
    nb, t, rwc = rw.shape
    rwid = w0.shape[1]
    grid = (nb, t // tm)
    row = lambda b, i: (b, i, 0)
    const = lambda b, i: (0, 0)
    if whole_prev:
        prev_spec = pl.BlockSpec((None, tm, rwc), row)
        first_spec = pl.BlockSpec((None, tm, rwc), row)
    else:
        prev_spec = pl.BlockSpec((None, 8, rwc), lambda b, i: (b, jnp.maximum(i * (tm // 8) - 1, 0), 0))
        first_spec = pl.BlockSpec((None, 1, rwc), lambda b, i: (b, 0, 0))
    vec = pl.BlockSpec((1, rwid), const)
    out = pl.BlockSpec((None, tm, rwid), row)
    return pl.pallas_call(
        functools.partial(_rwprep_kernel, rwid=rwid, whole_prev=whole_prev),
        grid=grid,
        in_specs=[pl.BlockSpec((None, tm, rwc), row), prev_spec, first_spec,
                  pl.BlockSpec((1, rwc), const), vec, pl.BlockSpec((LANES, rwid), const),
                  vec, pl.BlockSpec((LANES, rwid), const), pl.BlockSpec((LANES, rwid), const), vec, vec,
                  pl.BlockSpec((rwid, rwid), const)],
        out_specs=[out] * 7,
        out_shape=[jax.ShapeDtypeStruct((nb, t, rwid), F32)] * 7,
        compiler_params=_cparams("parallel", "parallel"),
        name="rwprep",
    )(rw, rw, first, mu, w0, w2p, a0, a2p, g2, k_k, k_a, bd_ones)


def _rwchunk_kernel(r_ref, ld_ref, k_ref, v_ref, al_ref, be_ref, y_ref, st_ref, z_ref, *, nb):
    c = RW_CHUNK
    ci = pl.program_id(1)

    @pl.when(ci == 0)
    def _():
        z_ref[...] = jnp.zeros_like(z_ref)

    ti = lax.broadcasted_iota(I32, (c, c), 0)
    si = lax.broadcasted_iota(I32, (c, c), 1)
    low_incl = si <= ti
    low_strict = si < ti
    diag = si == ti
    tri = jnp.where(low_incl, 1.0, 0.0).astype(BF16)
    eye = jnp.where(diag, 1.0, 0.0)
    head0 = lax.broadcasted_iota(I32, (1, LANES), 1) < HEAD_DIM
    hsels = (head0, jnp.logical_not(head0))
    same_head = (ti < HEAD_DIM) == (si < HEAD_DIM)
    nt = (((1,), (1,)), ((), ()))
    batches = range(nb)
    chains = [(b, h) for b in batches for h in range(2)]

    cum = [_xdot(tri, ld_ref[b]) for b in batches]
    tot = [cm[c - 1:c, :] for cm in cum]
    e_neg = [jnp.exp(-cm) for cm in cum]
    at = [al_ref[b] * jnp.exp(cum[b] - ld_ref[b]) for b in batches]
    rt = [r_ref[b] * jnp.exp(cum[b]) for b in batches]
    rhs_t = [jnp.concatenate([be_ref[b] * e_neg[b], k_ref[b] * e_neg[b]], axis=0).astype(BF16) for b in batches]
    vb = [v_ref[b].astype(BF16) for b in batches]
    at_h = [jnp.where(hsels[h], at[b], 0.0) for b, h in chains]
    rt_h = [jnp.where(hsels[h], rt[b], 0.0) for b, h in chains]
    a4 = [lax.dot_general(jnp.concatenate([a, r], axis=0).astype(BF16), rhs_t[b], nt, preferred_element_type=F32)
          for (b, h), a, r in zip(chains, at_h, rt_h)]
    a_ab = [jnp.where(low_strict, m[:c, :c], 0.0) for m in a4]
    a_ak = [jnp.where(low_strict, m[:c, c:], 0.0).astype(BF16) for m in a4]
    a_r = [jnp.concatenate([jnp.where(low_incl, m[c:, :c], 0.0), jnp.where(low_incl, m[c:, c:], 0.0)],
                           axis=1).astype(BF16) for m in a4]
    pw = [m.astype(BF16) for m in a_ab]
    inv = [eye + m for m in a_ab]
    for _ in range(int(np.log2(c)) - 1):
        pw = [jnp.dot(m, m, preferred_element_type=F32).astype(BF16) for m in pw]
        inv = [i + jnp.dot(i.astype(BF16), m, preferred_element_type=F32) for i, m in zip(inv, pw)]
    akv = [jnp.dot(m, vb[b], preferred_element_type=F32) for (b, h), m in zip(chains, a_ak)]
    x = [_dot(i, jnp.concatenate([kv, a], axis=1)) for i, kv, a in zip(inv, akv, at_h)]
    u0_h = [m[:, :LANES] for m in x]
    at2_h = [m[:, LANES:] for m in x]
    y0_h = [jnp.dot(ar, jnp.concatenate([u0, v_ref[b]], axis=0).astype(BF16), preferred_element_type=F32)
            for (b, h), ar, u0 in zip(chains, a_r, u0_h)]
    rt2_h = [r + jnp.dot(ar[:, :c], a2.astype(BF16), preferred_element_type=F32)
             for r, ar, a2 in zip(rt_h, a_r, at2_h)]
    z = [z_ref[b] for b in batches]
    uy = [_dot(jnp.concatenate([at2_h[2 * b] + at2_h[2 * b + 1], rt2_h[2 * b] + rt2_h[2 * b + 1]], axis=0), z[b])
          for b in batches]
    u = [uy[b][:c] + jnp.where(head0, u0_h[2 * b], u0_h[2 * b + 1]) for b in batches]
    for b in batches:
        y_ref[b] = uy[b][c:] + jnp.where(head0, y0_h[2 * b], y0_h[2 * b + 1])
    e_end = [jnp.exp(tot[b] - cum[b]) for b in batches]
    lhs_t = [jnp.concatenate([be_ref[b] * e_end[b], k_ref[b] * e_end[b]], axis=0) for b in batches]
    zadd = [_dot(lhs_t[b].T, jnp.concatenate([u[b], v_ref[b]], axis=0)) for b in batches]
    for b in batches:
        dcol = jnp.sum(jnp.where(diag, jnp.broadcast_to(jnp.exp(tot[b]), (c, c)), 0.0), axis=1, keepdims=True)
        z_ref[b] = dcol * z[b] + jnp.where(same_head, zadd[b], 0.0)

    @pl.when(ci == pl.num_programs(1) - 1)
    def _():
        for b in batches:
            s = z_ref[b].T
            st_ref[b, 0] = s[:HEAD_DIM, :HEAD_DIM]
            st_ref[b, 1] = s[HEAD_DIM:, HEAD_DIM:]


def _rwchunk(r, ld, k, v, al, be):
    nb, t, rwid = r.shape
    c = RW_CHUNK
    assert t % c == 0 and c == LANES
    npair = rwid // LANES
    seq = pl.BlockSpec((nb, c, LANES), lambda hp, ci: (0, ci, hp))
    return pl.pallas_call(
        functools.partial(_rwchunk_kernel, nb=nb),
        grid=(npair, t // c),
        in_specs=[seq] * 6,
        out_specs=[seq, pl.BlockSpec((nb, 2, HEAD_DIM, HEAD_DIM), lambda hp, ci: (0, hp, 0, 0))],
        out_shape=[jax.ShapeDtypeStruct((nb, t, rwid), F32),
                   jax.ShapeDtypeStruct((nb, 2 * npair, HEAD_DIM, HEAD_DIM), F32)],
        scratch_shapes=[pltpu.VMEM((nb, LANES, LANES), F32)],
        compiler_params=_cparams("parallel", "arbitrary"),
        name="rwchunk",
    )(r, ld, k, v, al, be)


def _col(row, width):
    return jnp.broadcast_to(row, (LANES, width)).T[:, 0:1]


def _row(col, width):
    return jnp.broadcast_to(col, (width, LANES)).T[0:1, :]


def _rwstep_kernel(r_ref, ld_ref, k_ref, v_ref, al_ref, be_ref, s_ref, y_ref, so_ref, *, bb, nh):
    rwid = nh * HEAD_DIM

    def one(b, carry):
        r, dcy, k, al, be = (ref[b] for ref in (r_ref, ld_ref, k_ref, al_ref, be_ref))
        dcy = jnp.exp(dcy)
        v_col = _col(v_ref[b], rwid)
        ys = []
        for h in range(nh):
            hs = slice(h * HEAD_DIM, (h + 1) * HEAD_DIM)
            s = s_ref[b, h]
            sa = jnp.sum(s * al[:, hs], axis=1, keepdims=True)
            s = s * dcy[:, hs] + sa * be[:, hs] + v_col[hs, :] * k[:, hs]
            so_ref[b, h] = s
            ys.append(jnp.sum(s * r[:, hs], axis=1, keepdims=True))
        y_ref[b] = _row(jnp.concatenate(ys, axis=0), rwid)
        return carry

    lax.fori_loop(0, bb, one, 0)


def _rwstep(r, ld, k, v, al, be, state):
    nb, _, rwid = r.shape
    nh = state.shape[1]
    bb = 8
    row = pl.BlockSpec((bb, 1, rwid), lambda i: (i, 0, 0))
    st = pl.BlockSpec((bb, nh, HEAD_DIM, HEAD_DIM), lambda i: (i, 0, 0, 0))
    return pl.pallas_call(
        functools.partial(_rwstep_kernel, bb=bb, nh=nh),
        grid=(nb // bb,),
        in_specs=[row] * 6 + [st],
        out_specs=[row, st],
        out_shape=[jax.ShapeDtypeStruct((nb, 1, rwid), F32), jax.ShapeDtypeStruct(state.shape, F32)],
        compiler_params=_cparams("parallel"),
        name="rwstep",
    )(r, ld, k, v, al, be, state)


def _outproj_kernel(x_ref, at_ref, y_ref, r_ref, k_ref, v_ref, g_ref, gt_ref, sh_ref, sc_ref, g2_ref,
                    wt_ref, wb_ref, rk_ref, lw_ref, lb_ref, bdm_ref, bd1_ref, rwt_ref, rb_ref, *rest):
    x1_ref, h2_ref, lg_ref = rest[-3:]
    y = y_ref[...]
    bdm = bdm_ref[...]
    mean = _dot_x(y, bdm)
    yc = y - mean
    var = _dot_x(yc * yc, bdm)
    yn = yc * lax.rsqrt(var + GN_EPS) * lw_ref[...] + lb_ref[...]
    v = v_ref[...]
    bonus = _dot_x(r_ref[...] * k_ref[...] * rk_ref[...], bd1_ref[...]) * v
    rw = (yn + bonus) * g_ref[...]
    mix = _mm(at_ref[...], wt_ref[...]) + _mm(rw, wb_ref[...])
    x1 = x_ref[...] + gt_ref[...] * mix
    x1_ref[...] = x1
    ms = jnp.mean(x1 * x1, axis=-1, keepdims=True)
    h2 = x1 * lax.rsqrt(ms + NORM_EPS) * g2_ref[...] * (1.0 + sc_ref[...]) + sh_ref[...]
    _store_token_tiles(h2_ref, h2)
    lg_ref[...] = _dot3_nt(rwt_ref[...], h2) + rb_ref[...]


def _outproj(x3, attn, y, r, k, v, g, mod3, g2, wt_bf, wb_bf, r_k, ln_w, ln_b, bd_mean, bd_ones, rwt, rb, tm,
             tok_rows, tok_row0=0, tok_buf=None):
    nb, t, d = x3.shape
    rmod = mod3.shape[1]
    aw = attn.shape[2]
    ne = rwt.shape[0]
    nt = t // tm
    assert tok_row0 % tm == 0
    grid = (nb, nt)
    row = lambda b, i: (b, i, 0)
    const = lambda b, i: (0, 0)
    mod_spec = lambda s: pl.BlockSpec((None, rmod, d), (lambda b, i: (b, 0, s)) if rmod == 1 else (lambda b, i: (b, i, s)))
    half = pl.BlockSpec((None, tm, aw), row)
    vec = pl.BlockSpec((1, aw), const)
    aliased = [] if tok_buf is None else [tok_buf]
    return pl.pallas_call(
        _outproj_kernel,
        grid=grid,
        in_specs=[pl.BlockSpec((None, tm, d), row)] + [half] * 6 + [mod_spec(2), mod_spec(3), mod_spec(4),
                  pl.BlockSpec((1, d), const), pl.BlockSpec((aw, d), const), pl.BlockSpec((aw, d), const),
                  vec, vec, vec, pl.BlockSpec((aw, aw), const), pl.BlockSpec((aw, aw), const),
                  pl.BlockSpec((ne, d), const), pl.BlockSpec((ne, 1), const)]
                 + [pl.BlockSpec(memory_space=pl.ANY)] * len(aliased),
        out_specs=[pl.BlockSpec((None, tm, d), row),
                   pl.BlockSpec((tm * SUBLANES, LANES), lambda b, i: (tok_row0 // tm + b * nt + i, 0)),
                   pl.BlockSpec((ne, tm), lambda b, i: (0, b * nt + i))],
        out_shape=[jax.ShapeDtypeStruct((nb, t, d), F32), jax.ShapeDtypeStruct((tok_rows * SUBLANES, LANES), F32),
                   jax.ShapeDtypeStruct((ne, nb * t), F32)],
        input_output_aliases={20: 1} if aliased else {},
        compiler_params=_cparams("parallel", "parallel"),
        name="outproj",
    )(x3, attn, y, r, k, v, g, mod3, mod3, mod3, g2, wt_bf, wb_bf, r_k, ln_w, ln_b, bd_mean, bd_ones, rwt, rb,
      *aliased)


def _route_kernel(lg_ref, idx_ref, wt_ref, rank_ref, cnt_ref, carry_ref, *, n_valid):
    i = pl.program_id(0)
    ne, tn = lg_ref.shape

    @pl.when(i == 0)
    def _():
        carry_ref[...] = jnp.zeros_like(carry_ref)

    lg = lg_ref[...]
    eidx = lax.broadcasted_iota(I32, (ne, tn), 0).astype(F32)
    tok = i * tn + lax.broadcasted_iota(I32, (1, tn), 1)
    live = tok < n_valid
    vals, hots = [], []
    for _ in range(TOP_K):
        mx = jnp.max(lg, axis=0, keepdims=True)
        pick = jnp.min(jnp.where(lg == mx, eidx, float(ne)), axis=0, keepdims=True)
        hot = eidx == pick
        vals.append(mx)
        hots.append(hot)
        idx_ref[pl.ds(len(vals) - 1, 1), :] = pick.astype(I32)
        lg = jnp.where(hot, -jnp.inf, lg)
    ex = [jnp.exp(vv - vals[0]) for vv in vals]
    den = ex[0] + ex[1] + ex[2] + ex[3]
    for kk in range(TOP_K):
        wt_ref[pl.ds(kk, 1), :] = ex[kk] / den
    hot_all = jnp.zeros((ne, tn), F32)
    for hot in hots:
        hot_all = hot_all + jnp.where(hot & live, 1.0, 0.0)
    ri = lax.broadcasted_iota(I32, (tn, tn), 0)
    cj = lax.broadcasted_iota(I32, (tn, tn), 1)
    upper = jnp.where(ri <= cj, 1.0, 0.0).astype(BF16)
    incl = jnp.dot(hot_all.astype(BF16), upper, preferred_element_type=F32)
    before = carry_ref[:, 0:1] + incl - hot_all
    for kk, hot in enumerate(hots):
        rank_ref[pl.ds(kk, 1), :] = jnp.sum(jnp.where(hot, before, 0.0), axis=0, keepdims=True).astype(I32)
    carry_ref[...] = carry_ref[...] + jnp.sum(hot_all, axis=1, keepdims=True)
    cnt_ref[...] = carry_ref[...]


def _route(logits_t, n_valid):
    ne, npad = logits_t.shape
    tn = ROUTE_TILE
    tile = lambda rows: pl.BlockSpec((rows, tn), lambda i: (0, i))
    return pl.pallas_call(
        functools.partial(_route_kernel, n_valid=n_valid),
        grid=(npad // tn,),
        in_specs=[tile(ne)],
        out_specs=[tile(TOP_K), tile(TOP_K), tile(TOP_K), pl.BlockSpec((ne, LANES), lambda i: (0, 0))],
        out_shape=[jax.ShapeDtypeStruct((TOP_K, npad), I32), jax.ShapeDtypeStruct((TOP_K, npad), F32),
                   jax.ShapeDtypeStruct((TOP_K, npad), I32), jax.ShapeDtypeStruct((ne, LANES), F32)],
        scratch_shapes=[pltpu.VMEM((ne, LANES), F32)],
        compiler_params=_cparams("arbitrary"),
        name="route",
    )(logits_t)


def _dest_kernel(idx_ref, rank_ref, start_ref, dest_ref, *, n_valid, trash):
    i = pl.program_id(0)
    ne = start_ref.shape[0]
    tn = idx_ref.shape[1]
    eidx = lax.broadcasted_iota(I32, (ne, tn), 0)
    tok = i * tn + lax.broadcasted_iota(I32, (1, tn), 1)
    start = start_ref[:, 0:1]
    for kk in range(TOP_K):
        base = jnp.sum(jnp.where(eidx == idx_ref[pl.ds(kk, 1), :], start, 0.0), axis=0, keepdims=True)
        dest_ref[pl.ds(kk, 1), :] = jnp.where(tok < n_valid, base.astype(I32) + rank_ref[pl.ds(kk, 1), :], trash)


def _dest(idx, rank, start, n_valid, trash):
    _, npad = idx.shape
    ne = start.shape[0]
    tn = ROUTE_TILE
    tile = pl.BlockSpec((TOP_K, tn), lambda i: (0, i))
    return pl.pallas_call(
        functools.partial(_dest_kernel, n_valid=n_valid, trash=trash),
        grid=(npad // tn,),
        in_specs=[tile, tile, pl.BlockSpec((ne, LANES), lambda i: (0, 0))],
        out_specs=tile,
        out_shape=jax.ShapeDtypeStruct((TOP_K, npad), I32),
        compiler_params=_cparams("parallel"),
        name="dest",
    )(idx, rank, jnp.broadcast_to(start.astype(F32)[:, None], (ne, LANES)))


def _slotmap_kernel(start_ref, count_ref, dest_ref, map_ref, *, ne, trash_code):
    i = pl.program_id(0)

    @pl.when(i == 0)
    def _():
        def fill(s, carry):
            map_ref[s] = trash_code
            return carry

        for e in range(ne):
            first = start_ref[e] + count_ref[e]
            last = start_ref[e + 1] if e + 1 < ne else map_ref.shape[0]
            lax.fori_loop(first, last, fill, 0)

    def put(j, carry):
        for kk in range(TOP_K):
            map_ref[dest_ref[0, kk, j]] = (i * TOK_TILE + j) * TOP_K + kk
        return carry

    lax.fori_loop(0, TOK_TILE, put, 0, unroll=8)


def _slotmap(start, counts, dest3, n_slots, trash_code):
    n_tiles = dest3.shape[0]
    grid_spec = pltpu.PrefetchScalarGridSpec(
        num_scalar_prefetch=2,
        grid=(n_tiles,),
        in_specs=[pl.BlockSpec((1, TOP_K, TOK_TILE), lambda i, st, ct: (i, 0, 0), memory_space=pltpu.SMEM)],
        out_specs=pl.BlockSpec(memory_space=pltpu.SMEM),
    )
    return pl.pallas_call(
        functools.partial(_slotmap_kernel, ne=start.shape[0], trash_code=trash_code),
        grid_spec=grid_spec,
        out_shape=jax.ShapeDtypeStruct((n_slots,), I32),
        compiler_params=_cparams("arbitrary"),
        name="slotmap",
    )(start, counts, dest3)


def _decode_kernel(code_ref, tok_ref, row_ref, *, trash_code, y_stride):
    code = code_ref[...]
    nblk, blk = code.shape
    k_bits = TOP_K.bit_length() - 1
    assert TOP_K == 1 << k_bits
    blk_i = lax.broadcasted_iota(I32, (nblk, blk), 0)
    buf = jnp.where(blk_i == nblk - 1, N_OBUF - 1, lax.rem(blk_i, N_OBUF))
    trash_row = TOP_K * y_stride + buf * blk + lax.broadcasted_iota(I32, (nblk, blk), 1)
    pad = code >= trash_code
    tok = code >> k_bits
    tok_ref[...] = jnp.where(pad, 0, tok) * SUBLANES
    row_ref[...] = jnp.where(pad, trash_row, (code & (TOP_K - 1)) * y_stride + tok) * SUBLANES


def _decode(codes, trash_code, y_stride):
    shp = jax.ShapeDtypeStruct(codes.shape, I32)
    return pl.pallas_call(
        functools.partial(_decode_kernel, trash_code=trash_code, y_stride=y_stride),
        out_shape=[shp, shp],
        compiler_params=pltpu.CompilerParams(vmem_limit_bytes=VMEM_LIMIT),
        name="decode",
    )(codes)


N_XBUF = 2
N_OBUF = 3


def _moe_kernel(be_ref, nact_ref, t0_ref, t1_ref, t2_ref, rprev_ref, rcur_ref, h_ref, wgu_ref, bgu_ref, wd_ref,
                bd_ref, y_ref, xbuf, xb_ref, obuf, gsem, ssem, wgu_bf, wd_bf, *, dff):
    i = pl.program_id(0)
    nact = nact_ref[0]
    blk = MOE_BLOCK
    tile = lambda first: pl.ds(first if isinstance(first, int) else pl.multiple_of(first, SUBLANES), SUBLANES)

    def gather(t_ref, j, slot):
        return pltpu.make_async_copy(h_ref.at[tile(t_ref[0, 0, j]), :], xbuf.at[slot, tile(j * SUBLANES), :],
                                     gsem.at[slot])

    def scatter(r_ref, j, slot):
        return pltpu.make_async_copy(obuf.at[slot, tile(j * SUBLANES), :], y_ref.at[tile(r_ref[0, 0, j]), :],
                                     ssem.at[slot])

    def wait_gathers(slot):
        pltpu.make_async_copy(h_ref.at[pl.ds(0, blk * SUBLANES), :], xbuf.at[slot], gsem.at[slot]).wait()

    def wait_scatters(slot):
        pltpu.make_async_copy(obuf.at[slot], y_ref.at[pl.ds(0, blk * SUBLANES), :], ssem.at[slot]).wait()

    @pl.when(i == 0)
    def _():
        obuf[...] = jnp.zeros_like(obuf)
        for j in range(blk):
            gather(t0_ref, j, 0).start(priority=j % 2)
        for j in range(blk):
            gather(t1_ref, j, 1).start(priority=j % 2)

    @pl.when(i < nact)
    def _():
        wait_gathers(i % N_XBUF)

        @pl.when(i >= N_OBUF - 1)
        def _():
            wait_scatters(i % N_OBUF)

        @pl.when((i == 0) | (be_ref[i] != be_ref[jnp.maximum(i - 1, 0)]))
        def _():
            wgu_bf[...] = wgu_ref[...].astype(BF16)
            wd_bf[...] = wd_ref[...].astype(BF16)

        xb_ref[...] = _load_token_tiles(xbuf, blk, lead=(i % N_XBUF,)).astype(BF16)
        for j in range(blk):
            gather(t2_ref, j, i % N_XBUF).start(priority=j % 2)
        for j in range(blk):
            scatter(rprev_ref, j, (i + N_OBUF - 1) % N_OBUF).start(priority=j % 2)
        gu = jnp.dot(xb_ref[...], wgu_bf[...], preferred_element_type=F32) + bgu_ref[...]
        gate = jnp.minimum(gu[:, :dff], SWIGLU_LIMIT)
        up = jnp.clip(gu[:, dff:], -SWIGLU_LIMIT, SWIGLU_LIMIT)
        act = (up + 1.0) * gate * _sigmoid(gate * SWIGLU_ALPHA)
        _store_token_tiles(obuf, jnp.dot(act.astype(BF16), wd_bf[...], preferred_element_type=F32) + bd_ref[...],
                           lead=(i % N_OBUF,))

        @pl.when(i == nact - 1)
        def _():
            for j in range(blk):
                scatter(rcur_ref, j, i % N_OBUF).start(priority=j % 2)
            wait_gathers(0)
            wait_gathers(1)
            wait_scatters(i % N_OBUF)
            wait_scatters((i + N_OBUF - 1) % N_OBUF)

            @pl.when(i >= 1)
            def _():
                wait_scatters((i + N_OBUF - 2) % N_OBUF)


def _moe(blk_e, nact, slot_map, tok_buf, w_gu, b_gu, w_down, b_down, n_blocks, y_stride):
    ne, d, dff2 = w_gu.shape
    dff = dff2 // 2
    blk = MOE_BLOCK
    trash_code = TOP_K * y_stride
    codes = jnp.concatenate([slot_map[:n_blocks * blk].reshape(n_blocks, blk), jnp.full((1, blk), trash_code, I32)])
    tok_tab, row_tab = (a.reshape(n_blocks + 1, 1, blk) for a in _decode(codes, trash_code, y_stride))
    smem = lambda f: pl.BlockSpec((1, 1, blk), f, memory_space=pltpu.SMEM)
    ahead = lambda n: (lambda i, be, na: (jnp.minimum(i + n, na[0] - 1), 0, 0))
    grid_spec = pltpu.PrefetchScalarGridSpec(
        num_scalar_prefetch=2,
        grid=(n_blocks,),
        in_specs=[smem(ahead(0)), smem(ahead(1)), smem(ahead(2)),
                  smem(lambda i, be, na: (jnp.where(i == 0, n_blocks, i - 1), 0, 0)),
                  smem(lambda i, be, na: (i, 0, 0)),
                  pl.BlockSpec(memory_space=pl.ANY),
                  pl.BlockSpec((None, d, dff2), lambda i, be, na: (be[i], 0, 0)),
                  pl.BlockSpec((None, 1, dff2), lambda i, be, na: (be[i], 0, 0)),
                  pl.BlockSpec((None, dff, d), lambda i, be, na: (be[i], 0, 0)),
                  pl.BlockSpec((None, 1, d), lambda i, be, na: (be[i], 0, 0))],
        out_specs=pl.BlockSpec(memory_space=pl.ANY),
        scratch_shapes=[pltpu.VMEM((N_XBUF, blk * SUBLANES, LANES), F32), pltpu.VMEM((blk, d), BF16),
                        pltpu.VMEM((N_OBUF, blk * SUBLANES, LANES), F32),
                        pltpu.SemaphoreType.DMA((N_XBUF,)), pltpu.SemaphoreType.DMA((N_OBUF,)),
                        pltpu.VMEM((d, dff2), BF16), pltpu.VMEM((dff, d), BF16)],
    )
    return pl.pallas_call(
        functools.partial(_moe_kernel, dff=dff),
        grid_spec=grid_spec,
        out_shape=jax.ShapeDtypeStruct(((TOP_K * y_stride + N_OBUF * blk) * SUBLANES, LANES), F32),
        compiler_params=_cparams("arbitrary"),
        name="moe",
    )(blk_e, nact, tok_tab, tok_tab, tok_tab, row_tab, row_tab, tok_buf, w_gu, b_gu.reshape(ne, 1, dff2), w_down,
      b_down.reshape(ne, 1, d))


def _combine_kernel(y0_ref, y1_ref, y2_ref, y3_ref, w_ref, x1_ref, gt_ref, o_ref, *, rows):
    w = w_ref[...]
    y = w[:rows, 0:1] * _load_token_tiles(y0_ref, rows)
    for kk, y_ref in enumerate((y1_ref, y2_ref, y3_ref), start=1):
        y = y + w[:rows, kk:kk + 1] * _load_token_tiles(y_ref, rows)
    o_ref[...] = x1_ref[...] + gt_ref[...] * y


def _combine(ys, wt3, x1, mod3, tok_rows, tile0, rows):
    nb, t, d = x1.shape
    rmod = mod3.shape[1]
    nt = t // rows
    row0 = tile0 * TOK_TILE
    assert tok_rows % rows == 0 and row0 % rows == 0
    row = lambda b, i: (b, i, 0)
    gate = pl.BlockSpec((None, rmod, d), (lambda b, i: (b, 0, 5)) if rmod == 1 else (lambda b, i: (b, i, 5)))
    assert d == SUBLANES * LANES
    expert_out = lambda kk: pl.BlockSpec((rows * SUBLANES, LANES),
                                         lambda b, i: ((kk * tok_rows + row0) // rows + b * nt + i, 0))
    return pl.pallas_call(
        functools.partial(_combine_kernel, rows=rows),
        grid=(nb, nt),
        in_specs=[expert_out(kk) for kk in range(TOP_K)]
                 + [pl.BlockSpec((None, TOK_TILE, TOP_K), lambda b, i: (tile0 + (b * nt + i) * rows // TOK_TILE, 0, 0)),
                    pl.BlockSpec((None, rows, d), row), gate],
        out_specs=pl.BlockSpec((None, rows, d), row),
        out_shape=jax.ShapeDtypeStruct((nb, t, d), F32),
        compiler_params=_cparams("parallel", "parallel"),
        name="combine",
    )(ys, ys, ys, ys, wt3, x1, mod3)


def _rope_tables(pos, n_heads):
    half = HEAD_DIM // 2
    inv_freq = 1.0 / (ROPE_THETA ** (jnp.arange(0, HEAD_DIM, 2, dtype=F32) / HEAD_DIM))
    ang = pos.astype(F32)[:, None] * inv_freq[None, :]
    cos, sin = jnp.cos(ang), jnp.sin(ang)
    del half
    return (jnp.tile(jnp.concatenate([cos, cos], axis=-1), (1, n_heads)),
            jnp.tile(jnp.concatenate([-sin, sin], axis=-1), (1, n_heads)))


def _block_diag(width, value):
    h = np.arange(width) // HEAD_DIM
    return jnp.asarray(np.where(h[:, None] == h[None, :], value, 0.0), F32)


def kernel(x_prompt, x_sample, cache_k, cache_v, state_wkv, state_shift, c_prompt, c_sample, w_ada, b_ada, norm1_g, norm2_g, w_in, q_norm_g, k_norm_g, rwkv_mu, rwkv_w0, rwkv_w2, rwkv_a0, rwkv_a2, rwkv_g2, rwkv_k_k, rwkv_k_a, rwkv_r_k, rwkv_ln_w, rwkv_ln_b, w_out, router_w, router_b, moe_w_gu, moe_b_gu, moe_w_down, moe_b_down):
    nbp, t, d = x_prompt.shape
    nbs, ts, _ = x_sample.shape
    depth = w_ada.shape[0]
    assert depth == 1 and ts == 1
    n_heads = cache_k.shape[3]
    aw = n_heads * HEAD_DIM
    rwid = rwkv_w0.shape[1]
    rw_heads = rwid // HEAD_DIM
    rwc = rwkv_mu.shape[1]
    past = cache_k.shape[2]
    ne = router_w.shape[2]
    keep = min(MAX_WINDOW, t)
    lyr = 0

    w_in_bf = w_in[lyr].astype(BF16)
    wt_bf = w_out[lyr][:aw].astype(BF16)
    wb_bf = w_out[lyr][aw:].astype(BF16)
    g1 = norm1_g[lyr].reshape(1, d)
    g2 = norm2_g[lyr].reshape(1, d)
    qg = jnp.tile(q_norm_g[lyr], n_heads).reshape(1, aw)
    kg = jnp.tile(k_norm_g[lyr], n_heads).reshape(1, aw)
    bd_mean_a = _block_diag(aw, 1.0 / HEAD_DIM)
    bd_mean_r = _block_diag(rwid, 1.0 / HEAD_DIM)
    bd_ones_r = _block_diag(rwid, 1.0)
    dl = rwkv_w2.shape[1]
    w2p = jnp.zeros((LANES, rwid), F32).at[:dl].set(rwkv_w2[lyr])
    a2p = jnp.zeros((LANES, rwid), F32).at[dl:dl + rwkv_a2.shape[1]].set(rwkv_a2[lyr])
    vec = lambda a: a[lyr].reshape(1, -1)
    rwt = router_w[lyr].T
    rb = router_b[lyr].reshape(ne, 1)

    rows_c = nbp + nbs
    rows_pad = -(-rows_c // 8) * 8
    c_all = jnp.zeros((rows_pad, d), F32).at[:nbp].set(c_prompt).at[nbp:rows_c].set(c_sample)
    mod = _ada(c_all, w_ada[lyr], b_ada[lyr])
    mod_p = mod[:nbp].reshape(nbp, 1, 6 * d)
    mod_s = mod[nbp:rows_c].reshape(1, nbs, 6 * d)
    xs3 = x_sample.reshape(1, nbs, d)

    cos_p, sin_p = _rope_tables(jnp.arange(t), n_heads)
    cos_s, sin_s = _rope_tables(jnp.full((nbs,), PAST_LEN), n_heads)
    qp, kp, vp, rwp, kt_p, vt_p = _inproj(x_prompt, mod_p, g1, w_in_bf, cos_p, sin_p, qg, kg, bd_mean_a,
                                          ROW_TILE, keep=keep)
    qs, ks, vs, rws = _inproj(xs3, mod_s, g1, w_in[lyr], cos_s, sin_s, qg, kg, bd_mean_a, nbs)

    attn_p = _attn_prompt(qp, kp, vp)
    as_rows = lambda a: a.reshape(nbs, 1, -1)
    cache_t = lambda cch: jnp.transpose(cch[lyr], (0, 2, 3, 1)).reshape(nbs, aw, past)
    attn_s = _attn_sample(as_rows(qs), as_rows(ks), as_rows(vs), cache_t(cache_k), cache_t(cache_v))

    rw_args = (vec(rwkv_mu), vec(rwkv_w0), w2p, vec(rwkv_a0), a2p, rwkv_g2[lyr], vec(rwkv_k_k), vec(rwkv_k_a), bd_ones_r)
    pre_p = _rwprep(rwp, jnp.zeros((nbp, 1, rwc), F32), *rw_args, tm=ROW_TILE, whole_prev=False)
    pre_s = _rwprep(rws, state_shift[lyr].reshape(1, nbs, rwc), *rw_args, tm=nbs, whole_prev=True)
    r_p, ld_p, k_p, v_p, al_p, be_p, g_p = pre_p
    y_p, wkv_p = _rwchunk(r_p, ld_p, k_p, v_p, al_p, be_p)
    r_s, ld_s, k_s, v_s, al_s, be_s, g_s = pre_s
    y_s, wkv_s = _rwstep(*(as_rows(a) for a in (r_s, ld_s, k_s, v_s, al_s, be_s)), state_wkv[lyr])
    y_s = y_s.reshape(1, nbs, rwid)

    op_args = (rwkv_r_k[lyr].reshape(1, rwid), vec(rwkv_ln_w), vec(rwkv_ln_b), bd_mean_r, bd_ones_r, rwt, rb)
    n_p = nbp * t
    n_valid = n_p + nbs
    assert n_p % ROUTE_TILE == 0 and ROUTE_TILE % TOK_TILE == 0 and n_p % nbs == 0
    npad = -(-n_valid // ROUTE_TILE) * ROUTE_TILE
    x1_p, tok_buf, lg_p = _outproj(x_prompt, attn_p, y_p, r_p, k_p, v_p, g_p, mod_p, g2, wt_bf, wb_bf, *op_args,
                                   tm=ROW_TILE, tok_rows=npad)
    x1_s, tok_buf, lg_s = _outproj(xs3, attn_s.reshape(1, nbs, aw), y_s, r_s, k_s, v_s, g_s, mod_s, g2,
                                   w_out[lyr][:aw], w_out[lyr][aw:], *op_args, tm=nbs, tok_rows=npad,
                                   tok_row0=n_p, tok_buf=tok_buf)

    logits_t = jnp.zeros((ne, npad), F32).at[:, :n_p].set(lg_p).at[:, n_p:n_valid].set(lg_s)
    idx, wts, rank, cnt = _route(logits_t, n_valid)
    counts = cnt[:, 0].astype(I32)
    blk = MOE_BLOCK
    n_blocks = -(-(n_valid * TOP_K + ne * (blk - 1)) // blk)
    padded = (counts + blk - 1) // blk * blk
    pad_end = jnp.cumsum(padded)
    start = pad_end - padded
    blk_e = jnp.minimum(jnp.sum(pad_end[None, :] <= (jnp.arange(n_blocks) * blk)[:, None], axis=1), ne - 1).astype(I32)
    nact = (pad_end[-1] // blk).reshape(1).astype(I32)
    trash = n_blocks * blk
    dest = _dest(idx, rank, start.astype(I32), n_valid, trash)
    n_tiles = npad // TOK_TILE
    dest3 = dest.reshape(TOP_K, n_tiles, TOK_TILE).transpose(1, 0, 2)
    wt3 = wts.reshape(TOP_K, n_tiles, TOK_TILE).transpose(1, 2, 0)
    slot_map = _slotmap(start.astype(I32), counts, dest3, trash + 8, npad * TOP_K)
    ys = _moe(blk_e, nact, slot_map, tok_buf, moe_w_gu[lyr], moe_b_gu[lyr], moe_w_down[lyr], moe_b_down[lyr],
              n_blocks, npad)
    y_prompt = _combine(ys, wt3, x1_p, mod_p, npad, 0, TOK_TILE)
    y_sample = _combine(ys, wt3, x1_s, mod_s, npad, n_p // TOK_TILE, nbs)

    kept = lambda a: jnp.transpose(a.reshape(nbp, n_heads, HEAD_DIM, keep), (0, 3, 1, 2))[None]
    return (y_prompt, y_sample.reshape(nbs, ts, d), kept(kt_p), kept(vt_p), wkv_p[None], rwp[:, t - 1][None],
            ks.reshape(nbs, ts, n_heads, HEAD_DIM)[None], vs.reshape(nbs, ts, n_heads, HEAD_DIM)[None],
            wkv_s[None], rws.reshape(nbs, rwc)[None])
```

```python
import functools

import numpy as np
import jax
import jax.numpy as jnp
from jax import lax
from jax.experimental import pallas as pl
from jax.experimental.pallas import tpu as pltpu

F32 = jnp.float32
BF16 = jnp.bfloat16
I32 = jnp.int32

HEAD_DIM = 64
LANES = 128
SUBLANES = 8
DILATED_PATTERNS = ((128, 1), (512, 4), (2048, 16))
WINDOW_STEPS = 128
MAX_WINDOW = 2048
PAST_LEN = 16384
ROPE_THETA = 10000.0
NORM_EPS = 1e-6
GN_EPS = 64e-5
TOP_K = 4
SWIGLU_ALPHA = 1.702
SWIGLU_LIMIT = 7.0
RW_CHUNK = 128
MOE_BLOCK = 256
MOE_CHUNK = 16
SLOT_ROWS = 256
ATTN_UNITS = 4
ROW_TILE = 512
ROUTE_TILE = 512
VMEM_LIMIT = 56 * 1024 * 1024
NEG_BIG = -1e30


def _cparams(*sem):
    return pltpu.CompilerParams(dimension_semantics=sem, vmem_limit_bytes=VMEM_LIMIT)


def _dot(a, b):
    return jnp.dot(a.astype(BF16), b.astype(BF16), preferred_element_type=F32)


def _split2(a):
    hi = a.astype(BF16)
    lo = (a - hi.astype(F32)).astype(BF16)
    return hi, lo


def _split3(a):
    hi = a.astype(BF16)
    r1 = a - hi.astype(F32)
    mid = r1.astype(BF16)
    lo = (r1 - mid.astype(F32)).astype(BF16)
    return hi, mid, lo


def _dot_x(a, e):
    e = e.astype(BF16)
    hi, mid, lo = _split3(a)
    return (jnp.dot(hi, e, preferred_element_type=F32) + jnp.dot(mid, e, preferred_element_type=F32)
            + jnp.dot(lo, e, preferred_element_type=F32))


def _xdot(e, a):
    e = e.astype(BF16)
    hi, mid, lo = _split3(a)
    return (jnp.dot(e, hi, preferred_element_type=F32) + jnp.dot(e, mid, preferred_element_type=F32)
            + jnp.dot(e, lo, preferred_element_type=F32))


def _dot3(a, b):
    ah, al = _split2(a)
    bh, bl = _split2(b)
    return (jnp.dot(ah, bh, preferred_element_type=F32) + jnp.dot(ah, bl, preferred_element_type=F32)
            + jnp.dot(al, bh, preferred_element_type=F32))


def _dot3_nt(a, b):
    ah, al = _split2(a)
    bh, bl = _split2(b)
    dn = (((1,), (1,)), ((), ()))
    return (lax.dot_general(ah, bh, dn, preferred_element_type=F32)
            + lax.dot_general(ah, bl, dn, preferred_element_type=F32)
            + lax.dot_general(al, bh, dn, preferred_element_type=F32))


def _mm(a, w):
    if w.dtype == BF16:
        return jnp.dot(a.astype(BF16), w, preferred_element_type=F32)
    return _dot3(a, w)


def _sigmoid(x):
    return 1.0 / (1.0 + jnp.exp(-x))


def _ada_kernel(c_ref, w_ref, b_ref, o_ref):
    c = c_ref[...]
    o_ref[...] = _dot3(c * _sigmoid(c), w_ref[...]) + b_ref[...]


def _ada(c_all, w_ada, b_ada):
    rows, d = c_all.shape
    n = w_ada.shape[1]
    tn = n // 4
    return pl.pallas_call(
        _ada_kernel,
        grid=(n // tn,),
        in_specs=[pl.BlockSpec((rows, d), lambda j: (0, 0)),
                  pl.BlockSpec((d, tn), lambda j: (0, j)),
                  pl.BlockSpec((1, tn), lambda j: (0, j))],
        out_specs=pl.BlockSpec((rows, tn), lambda j: (0, j)),
        out_shape=jax.ShapeDtypeStruct((rows, n), F32),
        compiler_params=_cparams("arbitrary"),
        name="ada",
    )(c_all, w_ada, b_ada.reshape(1, n))


def _inproj_kernel(x_ref, sh_ref, sc_ref, g_ref, w_ref, cos_ref, sin_ref, qg_ref, kg_ref, bd_ref,
                   q_ref, k_ref, v_ref, rw_ref, *maybe_kv_t, aw, first_kept):
    x = x_ref[...]
    ms = jnp.mean(x * x, axis=-1, keepdims=True)
    h = x * lax.rsqrt(ms + NORM_EPS) * g_ref[...] * (1.0 + sc_ref[...]) + sh_ref[...]
    proj = _mm(h, w_ref[...])
    cos = cos_ref[...]
    sin = sin_ref[...]
    lane = lax.broadcasted_iota(I32, (1, aw), 1)
    first_half = (lane % HEAD_DIM) < (HEAD_DIM // 2)
    bd = bd_ref[...]

    def norm_rope(t, g):
        tn = t * lax.rsqrt((_dot if w_ref.dtype == BF16 else _dot_x)(t * t, bd) + NORM_EPS) * g
        rot = jnp.where(first_half, pltpu.roll(tn, aw - HEAD_DIM // 2, 1), pltpu.roll(tn, HEAD_DIM // 2, 1))
        return tn * cos + rot * sin

    scale = 1.0 / np.sqrt(HEAD_DIM).astype(np.float32)
    q_ref[...] = norm_rope(proj[:, :aw], qg_ref[...]) * scale
    k = norm_rope(proj[:, aw:2 * aw], kg_ref[...])
    v = proj[:, 2 * aw:3 * aw]
    k_ref[...] = k
    v_ref[...] = v
    rw_ref[...] = proj[:, 3 * aw:]
    if maybe_kv_t:
        kt_ref, vt_ref = maybe_kv_t
        kept = pl.program_id(1) >= first_kept

        @pl.when(kept)
        def _():
            kt_ref[...] = k.T
            vt_ref[...] = v.T

        @pl.when(jnp.logical_not(kept))
        def _():
            kt_ref[...] = jnp.zeros_like(kt_ref)
            vt_ref[...] = jnp.zeros_like(vt_ref)


def _inproj(x3, mod3, g1, w_in_bf, cos_t, sin_t, qg, kg, bd_mean, tm, keep=0):
    nb, t, d = x3.shape
    r = mod3.shape[1]
    ncol = w_in_bf.shape[1]
    aw = cos_t.shape[1]
    rwc = ncol - 3 * aw
    grid = (nb, t // tm)
    row = lambda b, i: (b, i, 0)
    const = lambda b, i: (0, 0)
    mod_spec = lambda s: pl.BlockSpec((None, r, d), (lambda b, i: (b, 0, s)) if r == 1 else (lambda b, i: (b, i, s)))
    out_specs = [pl.BlockSpec((None, tm, aw), row)] * 3 + [pl.BlockSpec((None, tm, rwc), row)]
    out_shape = [jax.ShapeDtypeStruct((nb, t, aw), F32)] * 3 + [jax.ShapeDtypeStruct((nb, t, rwc), F32)]
    first_kept = (t - keep) // tm
    if keep:
        assert keep % tm == 0 and (t - keep) % tm == 0
        kept_spec = pl.BlockSpec((None, aw, tm), lambda b, i: (b, 0, jnp.maximum(i - first_kept, 0)))
        out_specs += [kept_spec, kept_spec]
        out_shape += [jax.ShapeDtypeStruct((nb, aw, keep), F32)] * 2
    outs = pl.pallas_call(
        functools.partial(_inproj_kernel, aw=aw, first_kept=first_kept),
        grid=grid,
        in_specs=[pl.BlockSpec((None, tm, d), row), mod_spec(0), mod_spec(1),
                  pl.BlockSpec((1, d), const), pl.BlockSpec((d, ncol), const),
                  pl.BlockSpec((tm, aw), lambda b, i: (i, 0)), pl.BlockSpec((tm, aw), lambda b, i: (i, 0)),
                  pl.BlockSpec((1, aw), const), pl.BlockSpec((1, aw), const), pl.BlockSpec((aw, aw), const)],
        out_specs=out_specs,
        out_shape=out_shape,
        compiler_params=_cparams("parallel", "arbitrary"),
        name="inproj",
    )(x3, mod3, mod3, g1, w_in_bf, cos_t, sin_t, qg, kg, bd_mean)
    return outs


def _attn_prompt_kernel(q_ref, k_ref, v_ref, o_ref, kp_ref, vp_ref, m_ref, l_ref, acc_ref, *, t, pad):
    nq = WINDOW_STEPS
    nk = 2 * WINDOW_STEPS
    kp_ref[pl.ds(0, pad), :] = jnp.zeros((pad, LANES), F32)
    vp_ref[pl.ds(0, pad), :] = jnp.zeros((pad, LANES), F32)
    kp_ref[pl.ds(pad, t), :] = k_ref[...]
    vp_ref[pl.ds(pad, t), :] = v_ref[...]
    head0 = lax.broadcasted_iota(I32, (1, LANES), 1) < HEAD_DIM
    qi = lax.broadcasted_iota(I32, (nq, nk), 0)
    kj = lax.broadcasted_iota(I32, (nq, nk), 1)
    steps_back = qi + nq - kj
    band = (steps_back >= 0) & (steps_back <= WINDOW_STEPS)
    has_past = kj >= nq

    hsels = (head0, jnp.logical_not(head0))
    nt = (((1,), (1,)), ((), ()))

    for p, (_, d) in enumerate(DILATED_PATTERNS):
        def units(g, carry, p=p, d=d):
            rows_q, kb, vb, valid, qh = [], [], [], [], []
            for j in range(ATTN_UNITS):
                u = g * ATTN_UNITS + j
                res = u % d
                blk = u // d
                q_start = res + d * nq * blk
                k_start = pad + q_start - d * nq
                if d == 1:
                    rows_q.append(pl.ds(q_start, nq))
                    rows_k = pl.ds(k_start, nk)
                else:
                    rows_q.append(pl.ds(q_start, nq, stride=d))
                    rows_k = pl.ds(k_start, nk, stride=d)
                q = q_ref[rows_q[j], :]
                kb.append(kp_ref[rows_k, :].astype(BF16))
                vb.append(vp_ref[rows_k, :].astype(BF16))
                valid.append(band & (has_past | (blk > 0)))
                qh.append([jnp.where(hsel, q, 0.0).astype(BF16) for hsel in hsels])
            chains = [(j, h) for j in range(ATTN_UNITS) for h in range(2)]
            s = [lax.dot_general(qh[j][h], kb[j], nt, preferred_element_type=F32) for j, h in chains]
            s = [jnp.where(valid[j], sc, NEG_BIG) for (j, h), sc in zip(chains, s)]
            mx = [jnp.max(sc, axis=-1, keepdims=True) for sc in s]
            e = [jnp.exp(sc - m) for sc, m in zip(s, mx)]
            den = [jnp.sum(ec, axis=-1, keepdims=True) for ec in e]
            o = [jnp.dot(ec.astype(BF16), vb[j], preferred_element_type=F32) for (j, h), ec in zip(chains, e)]
            for j in range(ATTN_UNITS):
                m_ref[p, rows_q[j], :] = jnp.where(head0, mx[2 * j], mx[2 * j + 1])
                l_ref[p, rows_q[j], :] = jnp.where(head0, den[2 * j], den[2 * j + 1])
                acc_ref[p, rows_q[j], :] = jnp.where(head0, o[2 * j], o[2 * j + 1])
            return carry

        lax.fori_loop(0, t // nq // ATTN_UNITS, units, 0)

    rows = 256

    def merge(i, carry):
        sl = pl.ds(pl.multiple_of(i * rows, rows), rows)
        m0, m1, m2 = m_ref[0, sl, :], m_ref[1, sl, :], m_ref[2, sl, :]
        mm = jnp.maximum(jnp.maximum(m0, m1), m2)
        w0, w1, w2 = jnp.exp(m0 - mm), jnp.exp(m1 - mm), jnp.exp(m2 - mm)
        num = w0 * acc_ref[0, sl, :] + w1 * acc_ref[1, sl, :] + w2 * acc_ref[2, sl, :]
        den = w0 * l_ref[0, sl, :] + w1 * l_ref[1, sl, :] + w2 * l_ref[2, sl, :]
        o_ref[sl, :] = num / den
        return carry

    lax.fori_loop(0, t // rows, merge, 0)


def _attn_prompt(q, k, v):
    nb, t, aw = q.shape
    pad = MAX_WINDOW
    assert t % MAX_WINDOW == 0
    spec = pl.BlockSpec((None, t, LANES), lambda b, hp: (b, 0, hp))
    return pl.pallas_call(
        functools.partial(_attn_prompt_kernel, t=t, pad=pad),
        grid=(nb, aw // LANES),
        in_specs=[spec, spec, spec],
        out_specs=spec,
        out_shape=jax.ShapeDtypeStruct((nb, t, aw), F32),
        scratch_shapes=[pltpu.VMEM((pad + t, LANES), F32), pltpu.VMEM((pad + t, LANES), F32),
                        pltpu.VMEM((3, t, LANES), F32), pltpu.VMEM((3, t, LANES), F32),
                        pltpu.VMEM((3, t, LANES), F32)],
        compiler_params=_cparams("parallel", "parallel"),
        name="attn_prompt",
    )(q, k, v)


def _attn_sample_kernel(q_ref, kn_ref, vn_ref, kt_ref, vt_ref, o_ref, *, nh, w):
    aw = nh * HEAD_DIM
    dist = w - lax.broadcasted_iota(I32, (1, w), 1)
    mult = jnp.zeros((1, w), F32)
    for win, d in DILATED_PATTERNS:
        mult = mult + jnp.where((dist % d == 0) & (dist <= win), 1.0, 0.0)
    n_pat = float(len(DILATED_PATTERNS))
    q_col = _col(q_ref[...], aw)
    kn_col = _col(kn_ref[...], aw)
    vn_col = _col(vn_ref[...], aw)
    heads = range(nh)
    hs = [pl.ds(h * HEAD_DIM, HEAD_DIM) for h in heads]
    cut = lambda col, h: col[h * HEAD_DIM:(h + 1) * HEAD_DIM]
    s = [jnp.sum(kt_ref[hs[h], :] * cut(q_col, h), axis=0, keepdims=True) for h in heads]
    s_self = [jnp.sum(cut(q_col, h) * cut(kn_col, h), axis=0, keepdims=True) for h in heads]
    s = [jnp.where(mult > 0.0, sh, NEG_BIG) for sh in s]
    mx = [jnp.maximum(jnp.max(sh, axis=1, keepdims=True), ss) for sh, ss in zip(s, s_self)]
    pr = [mult * jnp.exp(sh - m) for sh, m in zip(s, mx)]
    p_self = [n_pat * jnp.exp(ss - m) for ss, m in zip(s_self, mx)]
    den = [jnp.sum(p, axis=1, keepdims=True) + ps for p, ps in zip(pr, p_self)]
    num = [jnp.sum(vt_ref[hs[h], :] * pr[h], axis=1, keepdims=True) + p_self[h] * cut(vn_col, h) for h in heads]
    outs = [n / dn for n, dn in zip(num, den)]
    o_ref[...] = _row(jnp.concatenate(outs, axis=0), aw)


def _attn_sample(q, kn, vn, cache_kt, cache_vt):
    nb, _, aw = q.shape
    w = cache_kt.shape[2]
    assert w == MAX_WINDOW
    row = pl.BlockSpec((None, 1, aw), lambda i: (i, 0, 0))
    mat = pl.BlockSpec((None, aw, w), lambda i: (i, 0, 0))
    return pl.pallas_call(
        functools.partial(_attn_sample_kernel, nh=aw // HEAD_DIM, w=w),
        grid=(nb,),
        in_specs=[row, row, row, mat, mat],
        out_specs=row,
        out_shape=jax.ShapeDtypeStruct((nb, 1, aw), F32),
        compiler_params=_cparams("parallel"),
        name="attn_sample",
    )(q, kn, vn, cache_kt, cache_vt)


def _rwprep_kernel(p_ref, prev_ref, first_ref, mu_ref, w0_ref, w2_ref, a0_ref, a2_ref, g2_ref, kk_ref, ka_ref,
                   bd_ref, r_ref, ld_ref, k_ref, v_ref, al_ref, be_ref, g_ref, *, rwid, whole_prev):
    p = p_ref[...]
    if whole_prev:
        prev = first_ref[...]
    else:
        tm = p.shape[0]
        before = jnp.where(pl.program_id(1) == 0, first_ref[...], prev_ref[7:8, :])
        rowi = lax.broadcasted_iota(I32, (tm, 1), 0)
        prev = jnp.where(rowi == 0, before, pltpu.roll(p, 1, 0))
    xs = p + mu_ref[...] * (prev - p)
    r = xs[:, :rwid]
    k = xs[:, rwid:2 * rwid]
    v = xs[:, 2 * rwid:3 * rwid]
    xwa = xs[:, 3 * rwid:3 * rwid + LANES]
    xg = xs[:, 3 * rwid + LANES:]
    z = w0_ref[...] + _dot3(jnp.tanh(xwa), w2_ref[...])
    softplus_neg = jnp.maximum(-z, 0.0) + jnp.log(1.0 + jnp.exp(-jnp.abs(z)))
    w = -softplus_neg - 0.5
    a = _sigmoid(a0_ref[...] + _dot3(xwa, a2_ref[...]))
    g = _dot3(_sigmoid(xg), g2_ref[...])
    kk = k * kk_ref[...]
    norm = jnp.sqrt(_dot_x(kk * kk, bd_ref[...]))
    kk = kk / jnp.maximum(norm, 1e-12)
    r_ref[...] = r
    ld_ref[...] = -jnp.exp(w)
    k_ref[...] = k * (1.0 + (a - 1.0) * ka_ref[...])
    v_ref[...] = v
    al_ref[...] = -kk
    be_ref[...] = kk * a
    g_ref[...] = g


def _rwprep(rw, first, mu, w0, w2p, a0, a2p, g2, k_k, k_a, bd_ones, tm, whole_prev):
    nb, t, rwc = rw.shape
    rwid = w0.shape[1]
    grid = (nb, t // tm)
    row = lambda b, i: (b, i, 0)
    const = lambda b, i: (0, 0)
    if whole_prev:
        prev_spec = pl.BlockSpec((None, tm, rwc), row)
        first_spec = pl.BlockSpec((None, tm, rwc), row)
    else:
        prev_spec = pl.BlockSpec((None, 8, rwc), lambda b, i: (b, jnp.maximum(i * (tm // 8) - 1, 0), 0))
        first_spec = pl.BlockSpec((None, 1, rwc), lambda b, i: (b, 0, 0))
    vec = pl.BlockSpec((1, rwid), const)
    out = pl.BlockSpec((None, tm, rwid), row)
    return pl.pallas_call(
        functools.partial(_rwprep_kernel, rwid=rwid, whole_prev=whole_prev),
        grid=grid,
        in_specs=[pl.BlockSpec((None, tm, rwc), row), prev_spec, first_spec,
                  pl.BlockSpec((1, rwc), const), vec, pl.BlockSpec((LANES, rwid), const),
                  vec, pl.BlockSpec((LANES, rwid), const), pl.BlockSpec((LANES, rwid), const), vec, vec,
                  pl.BlockSpec((rwid, rwid), const)],
        out_specs=[out] * 7,
        out_shape=[jax.ShapeDtypeStruct((nb, t, rwid), F32)] * 7,
        compiler_params=_cparams("parallel", "parallel"),
        name="rwprep",
    )(rw, rw, first, mu, w0, w2p, a0, a2p, g2, k_k, k_a, bd_ones)


def _rwchunk_kernel(r_ref, ld_ref, k_ref, v_ref, al_ref, be_ref, y_ref, st_ref, z_ref, *, nb):
    c = RW_CHUNK
    ci = pl.program_id(1)

    @pl.when(ci == 0)
    def _():
        z_ref[...] = jnp.zeros_like(z_ref)

    ti = lax.broadcasted_iota(I32, (c, c), 0)
    si = lax.broadcasted_iota(I32, (c, c), 1)
    low_incl = si <= ti
    low_strict = si < ti
    diag = si == ti
    tri = jnp.where(low_incl, 1.0, 0.0).astype(BF16)
    eye = jnp.where(diag, 1.0, 0.0)
    head0 = lax.broadcasted_iota(I32, (1, LANES), 1) < HEAD_DIM
    hsels = (head0, jnp.logical_not(head0))
    same_head = (ti < HEAD_DIM) == (si < HEAD_DIM)
    nt = (((1,), (1,)), ((), ()))
    batches = range(nb)
    chains = [(b, h) for b in batches for h in range(2)]

    cum = [_xdot(tri, ld_ref[b]) for b in batches]
    tot = [cm[c - 1:c, :] for cm in cum]
    e_neg = [jnp.exp(-cm) for cm in cum]
    at = [al_ref[b] * jnp.exp(cum[b] - ld_ref[b]) for b in batches]
    rt = [r_ref[b] * jnp.exp(cum[b]) for b in batches]
    rhs_t = [jnp.concatenate([be_ref[b] * e_neg[b], k_ref[b] * e_neg[b]], axis=0).astype(BF16) for b in batches]
    vb = [v_ref[b].astype(BF16) for b in batches]
    at_h = [jnp.where(hsels[h], at[b], 0.0) for b, h in chains]
    rt_h = [jnp.where(hsels[h], rt[b], 0.0) for b, h in chains]
    a4 = [lax.dot_general(jnp.concatenate([a, r], axis=0).astype(BF16), rhs_t[b], nt, preferred_element_type=F32)
          for (b, h), a, r in zip(chains, at_h, rt_h)]
    a_ab = [jnp.where(low_strict, m[:c, :c], 0.0) for m in a4]
    a_ak = [jnp.where(low_strict, m[:c, c:], 0.0).astype(BF16) for m in a4]
    a_r = [jnp.concatenate([jnp.where(low_incl, m[c:, :c], 0.0), jnp.where(low_incl, m[c:, c:], 0.0)],
                           axis=1).astype(BF16) for m in a4]
    pw = [m.astype(BF16) for m in a_ab]
    inv = [eye + m for m in a_ab]
    for _ in range(int(np.log2(c)) - 1):
        pw = [jnp.dot(m, m, preferred_element_type=F32).astype(BF16) for m in pw]
        inv = [i + jnp.dot(i.astype(BF16), m, preferred_element_type=F32) for i, m in zip(inv, pw)]
    akv = [jnp.dot(m, vb[b], preferred_element_type=F32) for (b, h), m in zip(chains, a_ak)]
    x = [_dot(i, jnp.concatenate([kv, a], axis=1)) for i, kv, a in zip(inv, akv, at_h)]
    u0_h = [m[:, :LANES] for m in x]
    at2_h = [m[:, LANES:] for m in x]
    y0_h = [jnp.dot(ar, jnp.concatenate([u0, v_ref[b]], axis=0).astype(BF16), preferred_element_type=F32)
            for (b, h), ar, u0 in zip(chains, a_r, u0_h)]
    rt2_h = [r + jnp.dot(ar[:, :c], a2.astype(BF16), preferred_element_type=F32)
             for r, ar, a2 in zip(rt_h, a_r, at2_h)]
    z = [z_ref[b] for b in batches]
    uy = [_dot(jnp.concatenate([at2_h[2 * b] + at2_h[2 * b + 1], rt2_h[2 * b] + rt2_h[2 * b + 1]], axis=0), z[b])
          for b in batches]
    u = [uy[b][:c] + jnp.where(head0, u0_h[2 * b], u0_h[2 * b + 1]) for b in batches]
    for b in batches:
        y_ref[b] = uy[b][c:] + jnp.where(head0, y0_h[2 * b], y0_h[2 * b + 1])
    e_end = [jnp.exp(tot[b] - cum[b]) for b in batches]
    lhs_t = [jnp.concatenate([be_ref[b] * e_end[b], k_ref[b] * e_end[b]], axis=0) for b in batches]
    zadd = [_dot(lhs_t[b].T, jnp.concatenate([u[b], v_ref[b]], axis=0)) for b in batches]
    for b in batches:
        dcol = jnp.sum(jnp.where(diag, jnp.broadcast_to(jnp.exp(tot[b]), (c, c)), 0.0), axis=1, keepdims=True)
        z_ref[b] = dcol * z[b] + jnp.where(same_head, zadd[b], 0.0)

    @pl.when(ci == pl.num_programs(1) - 1)
    def _():
        for b in batches:
            s = z_ref[b].T
            st_ref[b, 0] = s[:HEAD_DIM, :HEAD_DIM]
            st_ref[b, 1] = s[HEAD_DIM:, HEAD_DIM:]


def _rwchunk(r, ld, k, v, al, be):
    nb, t, rwid = r.shape
    c = RW_CHUNK
    assert t % c == 0 and c == LANES
    npair = rwid // LANES
    seq = pl.BlockSpec((nb, c, LANES), lambda hp, ci: (0, ci, hp))
    return pl.pallas_call(
        functools.partial(_rwchunk_kernel, nb=nb),
        grid=(npair, t // c),
        in_specs=[seq] * 6,
        out_specs=[seq, pl.BlockSpec((nb, 2, HEAD_DIM, HEAD_DIM), lambda hp, ci: (0, hp, 0, 0))],
        out_shape=[jax.ShapeDtypeStruct((nb, t, rwid), F32),
                   jax.ShapeDtypeStruct((nb, 2 * npair, HEAD_DIM, HEAD_DIM), F32)],
        scratch_shapes=[pltpu.VMEM((nb, LANES, LANES), F32)],
        compiler_params=_cparams("parallel", "arbitrary"),
        name="rwchunk",
    )(r, ld, k, v, al, be)


def _col(row, width):
    return jnp.broadcast_to(row, (LANES, width)).T[:, 0:1]


def _row(col, width):
    return jnp.broadcast_to(col, (width, LANES)).T[0:1, :]


def _rwstep_kernel(r_ref, ld_ref, k_ref, v_ref, al_ref, be_ref, s_ref, y_ref, so_ref, *, bb, nh):
    rwid = nh * HEAD_DIM

    def one(b, carry):
        r, dcy, k, al, be = (ref[b] for ref in (r_ref, ld_ref, k_ref, al_ref, be_ref))
        dcy = jnp.exp(dcy)
        v_col = _col(v_ref[b], rwid)
        heads = range(nh)
        hs = [slice(h * HEAD_DIM, (h + 1) * HEAD_DIM) for h in heads]
        st = [s_ref[b, h] for h in heads]
        sa = [jnp.sum(st[h] * al[:, hs[h]], axis=1, keepdims=True) for h in heads]
        st = [st[h] * dcy[:, hs[h]] + sa[h] * be[:, hs[h]] + v_col[hs[h], :] * k[:, hs[h]] for h in heads]
        for h in heads:
            so_ref[b, h] = st[h]
        ys = [jnp.sum(st[h] * r[:, hs[h]], axis=1, keepdims=True) for h in heads]
        y_ref[b] = _row(jnp.concatenate(ys, axis=0), rwid)
        return carry

    lax.fori_loop(0, bb, one, 0)


def _rwstep(r, ld, k, v, al, be, state):
    nb, _, rwid = r.shape
    nh = state.shape[1]
    bb = 8
    row = pl.BlockSpec((bb, 1, rwid), lambda i: (i, 0, 0))
    st = pl.BlockSpec((bb, nh, HEAD_DIM, HEAD_DIM), lambda i: (i, 0, 0, 0))
    return pl.pallas_call(
        functools.partial(_rwstep_kernel, bb=bb, nh=nh),
        grid=(nb // bb,),
        in_specs=[row] * 6 + [st],
        out_specs=[row, st],
        out_shape=[jax.ShapeDtypeStruct((nb, 1, rwid), F32), jax.ShapeDtypeStruct(state.shape, F32)],
        compiler_params=_cparams("parallel"),
        name="rwstep",
    )(r, ld, k, v, al, be, state)


def _outproj_kernel(x_ref, at_ref, y_ref, r_ref, k_ref, v_ref, g_ref, gt_ref, sh_ref, sc_ref, g2_ref,
                    wt_ref, wb_ref, rk_ref, lw_ref, lb_ref, bdm_ref, bd1_ref, rwt_ref, rb_ref,
                    x1_ref, h2_ref, lg_ref):
    y = y_ref[...]
    bdm = bdm_ref[...]
    mean = _dot_x(y, bdm)
    yc = y - mean
    var = _dot_x(yc * yc, bdm)
    yn = yc * lax.rsqrt(var + GN_EPS) * lw_ref[...] + lb_ref[...]
    v = v_ref[...]
    bonus = _dot_x(r_ref[...] * k_ref[...] * rk_ref[...], bd1_ref[...]) * v
    rw = (yn + bonus) * g_ref[...]
    mix = _mm(at_ref[...], wt_ref[...]) + _mm(rw, wb_ref[...])
    x1 = x_ref[...] + gt_ref[...] * mix
    x1_ref[...] = x1
    ms = jnp.mean(x1 * x1, axis=-1, keepdims=True)
    h2 = x1 * lax.rsqrt(ms + NORM_EPS) * g2_ref[...] * (1.0 + sc_ref[...]) + sh_ref[...]
    h2_ref[...] = h2.astype(h2_ref.dtype)
    lg_ref[...] = _dot3_nt(rwt_ref[...], h2) + rb_ref[...]


def _outproj(x3, attn, y, r, k, v, g, mod3, g2, wt_bf, wb_bf, r_k, ln_w, ln_b, bd_mean, bd_ones, rwt, rb, tm):
    nb, t, d = x3.shape
    rmod = mod3.shape[1]
    aw = attn.shape[2]
    ne = rwt.shape[0]
    nt = t // tm
    grid = (nb, nt)
    row = lambda b, i: (b, i, 0)
    const = lambda b, i: (0, 0)
    mod_spec = lambda s: pl.BlockSpec((None, rmod, d), (lambda b, i: (b, 0, s)) if rmod == 1 else (lambda b, i: (b, i, s)))
    half = pl.BlockSpec((None, tm, aw), row)
    vec = pl.BlockSpec((1, aw), const)
    return pl.pallas_call(
        _outproj_kernel,
        grid=grid,
        in_specs=[pl.BlockSpec((None, tm, d), row)] + [half] * 6 + [mod_spec(2), mod_spec(3), mod_spec(4),
                  pl.BlockSpec((1, d), const), pl.BlockSpec((aw, d), const), pl.BlockSpec((aw, d), const),
                  vec, vec, vec, pl.BlockSpec((aw, aw), const), pl.BlockSpec((aw, aw), const),
                  pl.BlockSpec((ne, d), const), pl.BlockSpec((ne, 1), const)],
        out_specs=[pl.BlockSpec((None, tm, d), row), pl.BlockSpec((None, tm, d), row),
                   pl.BlockSpec((ne, tm), lambda b, i: (0, b * nt + i))],
        out_shape=[jax.ShapeDtypeStruct((nb, t, d), F32), jax.ShapeDtypeStruct((nb, t, d), BF16),
                   jax.ShapeDtypeStruct((ne, nb * t), F32)],
        compiler_params=_cparams("parallel", "parallel"),
        name="outproj",
    )(x3, attn, y, r, k, v, g, mod3, mod3, mod3, g2, wt_bf, wb_bf, r_k, ln_w, ln_b, bd_mean, bd_ones, rwt, rb)


def _route_kernel(lg_ref, pos_ref, wt_ref, cnt_ref, *, n_valid):
    i = pl.program_id(0)
    ne, tn = lg_ref.shape
    lg = lg_ref[...]
    eidx = lax.broadcasted_iota(I32, (ne, tn), 0).astype(F32)
    tok = i * tn + lax.broadcasted_iota(I32, (1, tn), 1)
    live = tok < n_valid
    vals, hots = [], []
    for _ in range(TOP_K):
        mx = jnp.max(lg, axis=0, keepdims=True)
        pick = jnp.min(jnp.where(lg == mx, eidx, float(ne)), axis=0, keepdims=True)
        hot = eidx == pick
        vals.append(mx)
        hots.append(hot)
        lg = jnp.where(hot, -jnp.inf, lg)
    ex = [jnp.exp(vv - vals[0]) for vv in vals]
    den = ex[0] + ex[1] + ex[2] + ex[3]
    for kk in range(TOP_K):
        wt_ref[pl.ds(kk, 1), :] = ex[kk] / den
    hot_all = jnp.zeros((ne, tn), F32)
    for hot in hots:
        hot_all = hot_all + jnp.where(hot & live, 1.0, 0.0)
    ri = lax.broadcasted_iota(I32, (tn, tn), 0)
    cj = lax.broadcasted_iota(I32, (tn, tn), 1)
    upper = jnp.where(ri <= cj, 1.0, 0.0).astype(BF16)
    before = jnp.dot(hot_all.astype(BF16), upper, preferred_element_type=F32) - hot_all
    cnt = jnp.sum(hot_all, axis=1, keepdims=True)
    padded = jnp.ceil(cnt / MOE_CHUNK) * MOE_CHUNK
    er = lax.broadcasted_iota(I32, (ne, ne), 0)
    ec = lax.broadcasted_iota(I32, (ne, ne), 1)
    lower_strict = jnp.where(ec < er, 1.0, 0.0)
    off = _xdot(lower_strict, jnp.broadcast_to(padded, (ne, LANES)))[:, 0:1]
    for kk, hot in enumerate(hots):
        pos = jnp.sum(jnp.where(hot, off + before, 0.0), axis=0, keepdims=True)
        pos_ref[pl.ds(kk, 1), :] = jnp.where(live, pos, -1.0).astype(I32)
    cnt_ref[...] = jnp.broadcast_to(cnt, (ne, LANES))


def _route(logits_t, n_valid):
    ne, npad = logits_t.shape
    tn = ROUTE_TILE
    nt = npad // tn
    tile = lambda rows: pl.BlockSpec((rows, tn), lambda i: (0, i))
    pos, wts, cnt = pl.pallas_call(
        functools.partial(_route_kernel, n_valid=n_valid),
        grid=(nt,),
        in_specs=[tile(ne)],
        out_specs=[tile(TOP_K), tile(TOP_K), pl.BlockSpec((None, ne, LANES), lambda i: (i, 0, 0))],
        out_shape=[jax.ShapeDtypeStruct((TOP_K, npad), I32), jax.ShapeDtypeStruct((TOP_K, npad), F32),
                   jax.ShapeDtypeStruct((nt, ne, LANES), F32)],
        compiler_params=_cparams("parallel"),
        name="route",
    )(logits_t)
    return pos, wts, cnt[:, :, 0].astype(I32)


def _slot_capacity(ne):
    return -(-(TOP_K * ROUTE_TILE + ne * (MOE_CHUNK - 1)) // SLOT_ROWS) * SLOT_ROWS


def _dispatch_kernel(pos_ref, wt_ref, h_ref, *rest, n_slots):
    xs_ref, ws_ref = rest[-2:]
    tn = h_ref.shape[0]
    h = h_ref[...]
    pos = [pos_ref[pl.ds(kk, 1), :] for kk in range(TOP_K)]
    wts = [wt_ref[pl.ds(kk, 1), :] for kk in range(TOP_K)]

    def rows(c, carry):
        r0 = pl.multiple_of(c * SLOT_ROWS, SLOT_ROWS)
        slot = r0 + lax.broadcasted_iota(I32, (SLOT_ROWS, tn), 0)
        p = jnp.zeros((SLOT_ROWS, tn), F32)
        pw = jnp.zeros((SLOT_ROWS, tn), F32)
        for kk in range(TOP_K):
            hit = slot == pos[kk]
            p = p + jnp.where(hit, 1.0, 0.0)
            pw = pw + jnp.where(hit, wts[kk], 0.0)
        xs_ref[pl.ds(r0, SLOT_ROWS), :] = jnp.dot(p.astype(BF16), h, preferred_element_type=F32).astype(BF16)
        ws_ref[pl.ds(r0, SLOT_ROWS), :] = jnp.broadcast_to(jnp.sum(pw, axis=1, keepdims=True), (SLOT_ROWS, LANES))
        return carry

    lax.fori_loop(0, n_slots // SLOT_ROWS, rows, 0)


def _dispatch(pos, wts, h, bufs, tile0, n_slots, n_rows):
    n, d = h.shape
    tn = ROUTE_TILE
    bufs = () if bufs is None else tuple(bufs)
    return pl.pallas_call(
        functools.partial(_dispatch_kernel, n_slots=n_slots),
        grid=(n // tn,),
        in_specs=[pl.BlockSpec((TOP_K, tn), lambda i: (0, tile0 + i)), pl.BlockSpec((TOP_K, tn), lambda i: (0, tile0 + i)),
                  pl.BlockSpec((tn, d), lambda i: (i, 0))] + [pl.BlockSpec(memory_space=pl.ANY)] * len(bufs),
        out_specs=[pl.BlockSpec((n_slots, d), lambda i: (tile0 + i, 0)),
                   pl.BlockSpec((n_slots, LANES), lambda i: (tile0 + i, 0))],
        out_shape=[jax.ShapeDtypeStruct((n_rows, d), BF16), jax.ShapeDtypeStruct((n_rows, LANES), F32)],
        input_output_aliases={3: 0, 4: 1} if bufs else {},
        compiler_params=_cparams("parallel"),
        name="dispatch",
    )(pos, wts, h, *bufs)


def _chunk_tables(cnt, n_slots, n_blocks):
    nt, ne = cnt.shape
    per_blk = MOE_BLOCK // MOE_CHUNK
    nch = -(-cnt // MOE_CHUNK)
    seg0 = (jnp.cumsum(nch, axis=1) - nch) + (jnp.arange(nt) * (n_slots // MOE_CHUNK))[:, None]
    cum_t = jnp.cumsum(nch, axis=0)
    total = cum_t[-1]
    blocks = -(-total // per_blk)
    blk_end = jnp.cumsum(blocks)
    blk_e = jnp.minimum(jnp.sum(blk_end[None, :] <= jnp.arange(n_blocks)[:, None], axis=1), ne - 1).astype(I32)
    nact = blk_end[-1].reshape(1).astype(I32)
    n = jnp.arange(n_blocks * per_blk)
    e_n = blk_e[n // per_blk]
    q = n - (blk_end - blocks)[e_n] * per_blk
    live = (q < total[e_n]) & (n < nact[0] * per_blk)
    cum_e = cum_t.T[e_n]
    t_n = jnp.minimum(jnp.sum(cum_e <= q[:, None], axis=1), nt - 1)
    m = q - (cum_t - nch)[t_n, e_n]
    src = jnp.where(live, seg0[t_n, e_n] + m, 0).astype(I32)
    spare = nt * (n_slots // MOE_CHUNK) + (n % (N_OBUF * per_blk))
    dst = jnp.where(live, src, spare).astype(I32)
    return blk_e, nact, src, dst


N_OBUF = 2


def _moe_kernel(be_ref, nact_ref, src_ref, dst_ref, xs_ref, ws_ref, wgu_ref, bgu_ref, wd_ref, bd_ref, out_in_ref,
                out_ref, xbuf, wbuf, obuf, gsem, ssem, wgu_bf, wd_bf, *, dff):
    del out_in_ref
    i = pl.program_id(0)
    nact = nact_ref[0]
    per_blk = MOE_BLOCK // MOE_CHUNK
    rows = lambda c: pl.ds(pl.multiple_of(c * MOE_CHUNK, MOE_CHUNK), MOE_CHUNK)

    def gathers(blk, slot):
        copies = []
        for m in range(per_blk):
            c = src_ref[blk * per_blk + m]
            copies.append(pltpu.make_async_copy(xs_ref.at[rows(c), :], xbuf.at[slot, rows(m), :], gsem.at[slot]))
            copies.append(pltpu.make_async_copy(ws_ref.at[rows(c), :], wbuf.at[slot, rows(m), :], gsem.at[slot]))
        return copies

    def scatters(blk, slot):
        return [pltpu.make_async_copy(obuf.at[slot, rows(m), :], out_ref.at[rows(dst_ref[blk * per_blk + m]), :],
                                      ssem.at[slot]) for m in range(per_blk)]

    @pl.when(i == 0)
    def _():
        for cp in gathers(0, 0):
            cp.start()

    @pl.when(i < nact)
    def _():
        slot = i % 2
        for cp in gathers(i, slot):
            cp.wait()

        @pl.when(i + 1 < nact)
        def _():
            for cp in gathers(i + 1, 1 - slot):
                cp.start()

        @pl.when((i == 0) | (be_ref[i] != be_ref[jnp.maximum(i - 1, 0)]))
        def _():
            wgu_bf[...] = wgu_ref[...].astype(BF16)
            wd_bf[...] = wd_ref[...].astype(BF16)

        gu = jnp.dot(xbuf[slot], wgu_bf[...], preferred_element_type=F32) + bgu_ref[...]
        gate = jnp.minimum(gu[:, :dff], SWIGLU_LIMIT)
        up = jnp.clip(gu[:, dff:], -SWIGLU_LIMIT, SWIGLU_LIMIT)
        act = (up + 1.0) * gate * _sigmoid(gate * SWIGLU_ALPHA)
        res = jnp.dot(act.astype(BF16), wd_bf[...], preferred_element_type=F32) + bd_ref[...]
        obuf[slot] = (res * wbuf[slot][:, 0:1]).astype(BF16)

        @pl.when(i >= 1)
        def _():
            for cp in scatters(i - 1, 1 - slot):
                cp.wait()

        for cp in scatters(i, slot):
            cp.start()

        @pl.when(i == nact - 1)
        def _():
            for cp in scatters(i, slot):
                cp.wait()


def _moe(blk_e, nact, src, dst, xs, ws, w_gu, b_gu, w_down, b_down, n_blocks):
    ne, d, dff2 = w_gu.shape
    dff = dff2 // 2
    blk = MOE_BLOCK
    grid_spec = pltpu.PrefetchScalarGridSpec(
        num_scalar_prefetch=4,
        grid=(n_blocks,),
        in_specs=[pl.BlockSpec(memory_space=pl.ANY), pl.BlockSpec(memory_space=pl.ANY),
                  pl.BlockSpec((None, d, dff2), lambda i, be, *_: (be[i], 0, 0)),
                  pl.BlockSpec((None, 1, dff2), lambda i, be, *_: (be[i], 0, 0)),
                  pl.BlockSpec((None, dff, d), lambda i, be, *_: (be[i], 0, 0)),
                  pl.BlockSpec((None, 1, d), lambda i, be, *_: (be[i], 0, 0)),
                  pl.BlockSpec(memory_space=pl.ANY)],
        out_specs=pl.BlockSpec(memory_space=pl.ANY),
        scratch_shapes=[pltpu.VMEM((2, blk, d), BF16), pltpu.VMEM((2, blk, LANES), F32),
                        pltpu.VMEM((N_OBUF, blk, d), BF16),
                        pltpu.SemaphoreType.DMA((2,)), pltpu.SemaphoreType.DMA((N_OBUF,)),
                        pltpu.VMEM((d, dff2), BF16), pltpu.VMEM((dff, d), BF16)],
    )
    return pl.pallas_call(
        functools.partial(_moe_kernel, dff=dff),
        grid_spec=grid_spec,
        out_shape=jax.ShapeDtypeStruct(xs.shape, BF16),
        input_output_aliases={10: 0},
        compiler_params=_cparams("arbitrary"),
        name="moe",
    )(blk_e, nact, src, dst, xs, ws, w_gu, b_gu.reshape(ne, 1, dff2), w_down, b_down.reshape(ne, 1, d),
      jnp.zeros(xs.shape, BF16))


def _combine_kernel(pos_ref, ys_ref, x1_ref, gt_ref, o_ref, acc_ref, *, n_slots):
    tn = x1_ref.shape[0]
    pos = [pos_ref[:, kk:kk + 1] for kk in range(TOP_K)]
    acc_ref[...] = jnp.zeros_like(acc_ref)

    def cols(c, carry):
        c0 = pl.multiple_of(c * SLOT_ROWS, SLOT_ROWS)
        slot = c0 + lax.broadcasted_iota(I32, (tn, SLOT_ROWS), 1)
        p = jnp.zeros((tn, SLOT_ROWS), F32)
        for kk in range(TOP_K):
            p = p + jnp.where(slot == pos[kk], 1.0, 0.0)
        acc_ref[...] += jnp.dot(p.astype(BF16), ys_ref[pl.ds(c0, SLOT_ROWS), :], preferred_element_type=F32)
        return carry

    lax.fori_loop(0, n_slots // SLOT_ROWS, cols, 0)
    o_ref[...] = x1_ref[...] + gt_ref[...] * acc_ref[...]


def _combine(pos_t, ys, x1, mod3, tile0, rows, n_slots):
    nb, t, d = x1.shape
    rmod = mod3.shape[1]
    nt = t // rows
    per_tile = ROUTE_TILE // rows
    row = lambda b, i: (b, i, 0)
    gate = pl.BlockSpec((None, rmod, d), (lambda b, i: (b, 0, 5)) if rmod == 1 else (lambda b, i: (b, i, 5)))
    return pl.pallas_call(
        functools.partial(_combine_kernel, n_slots=n_slots),
        grid=(nb, nt),
        in_specs=[pl.BlockSpec((rows, TOP_K), lambda b, i: (tile0 * per_tile + b * nt + i, 0)),
                  pl.BlockSpec((n_slots, d), lambda b, i: (tile0 + (b * nt + i) // per_tile, 0)),
                  pl.BlockSpec((None, rows, d), row), gate],
        out_specs=pl.BlockSpec((None, rows, d), row),
        out_shape=jax.ShapeDtypeStruct((nb, t, d), F32),
        scratch_shapes=[pltpu.VMEM((rows, d), F32)],
        compiler_params=_cparams("parallel", "parallel"),
        name="combine",
    )(pos_t, ys, x1, mod3)


def _rope_tables(pos, n_heads):
    half = HEAD_DIM // 2
    inv_freq = 1.0 / (ROPE_THETA ** (jnp.arange(0, HEAD_DIM, 2, dtype=F32) / HEAD_DIM))
    ang = pos.astype(F32)[:, None] * inv_freq[None, :]
    cos, sin = jnp.cos(ang), jnp.sin(ang)
    del half
    return (jnp.tile(jnp.concatenate([cos, cos], axis=-1), (1, n_heads)),
            jnp.tile(jnp.concatenate([-sin, sin], axis=-1), (1, n_heads)))


def _block_diag(width, value):
    h = np.arange(width) // HEAD_DIM
    return jnp.asarray(np.where(h[:, None] == h[None, :], value, 0.0), F32)


def kernel(x_prompt, x_sample, cache_k, cache_v, state_wkv, state_shift, c_prompt, c_sample, w_ada, b_ada, norm1_g, norm2_g, w_in, q_norm_g, k_norm_g, rwkv_mu, rwkv_w0, rwkv_w2, rwkv_a0, rwkv_a2, rwkv_g2, rwkv_k_k, rwkv_k_a, rwkv_r_k, rwkv_ln_w, rwkv_ln_b, w_out, router_w, router_b, moe_w_gu, moe_b_gu, moe_w_down, moe_b_down):
    nbp, t, d = x_prompt.shape
    nbs, ts, _ = x_sample.shape
    depth = w_ada.shape[0]
    assert depth == 1 and ts == 1
    n_heads = cache_k.shape[3]
    aw = n_heads * HEAD_DIM
    rwid = rwkv_w0.shape[1]
    rwc = rwkv_mu.shape[1]
    past = cache_k.shape[2]
    ne = router_w.shape[2]
    keep = min(MAX_WINDOW, t)
    lyr = 0

    w_in_bf = w_in[lyr].astype(BF16)
    wt_bf = w_out[lyr][:aw].astype(BF16)
    wb_bf = w_out[lyr][aw:].astype(BF16)
    g1 = norm1_g[lyr].reshape(1, d)
    g2 = norm2_g[lyr].reshape(1, d)
    qg = jnp.tile(q_norm_g[lyr], n_heads).reshape(1, aw)
    kg = jnp.tile(k_norm_g[lyr], n_heads).reshape(1, aw)
    bd_mean_a = _block_diag(aw, 1.0 / HEAD_DIM)
    bd_mean_r = _block_diag(rwid, 1.0 / HEAD_DIM)
    bd_ones_r = _block_diag(rwid, 1.0)
    dl = rwkv_w2.shape[1]
    w2p = jnp.zeros((LANES, rwid), F32).at[:dl].set(rwkv_w2[lyr])
    a2p = jnp.zeros((LANES, rwid), F32).at[dl:dl + rwkv_a2.shape[1]].set(rwkv_a2[lyr])
    vec = lambda a: a[lyr].reshape(1, -1)
    rwt = router_w[lyr].T
    rb = router_b[lyr].reshape(ne, 1)

    rows_c = nbp + nbs
    rows_pad = -(-rows_c // 8) * 8
    c_all = jnp.zeros((rows_pad, d), F32).at[:nbp].set(c_prompt).at[nbp:rows_c].set(c_sample)
    mod = _ada(c_all, w_ada[lyr], b_ada[lyr])
    mod_p = mod[:nbp].reshape(nbp, 1, 6 * d)
    mod_s = mod[nbp:rows_c].reshape(1, nbs, 6 * d)
    xs3 = x_sample.reshape(1, nbs, d)

    cos_p, sin_p = _rope_tables(jnp.arange(t), n_heads)
    cos_s, sin_s = _rope_tables(jnp.full((nbs,), PAST_LEN), n_heads)
    qp, kp, vp, rwp, kt_p, vt_p = _inproj(x_prompt, mod_p, g1, w_in_bf, cos_p, sin_p, qg, kg, bd_mean_a,
                                          ROW_TILE, keep=keep)
    qs, ks, vs, rws = _inproj(xs3, mod_s, g1, w_in[lyr], cos_s, sin_s, qg, kg, bd_mean_a, nbs)

    attn_p = _attn_prompt(qp, kp, vp)
    as_rows = lambda a: a.reshape(nbs, 1, -1)
    cache_t = lambda cch: jnp.transpose(cch[lyr], (0, 2, 3, 1)).reshape(nbs, aw, past)
    attn_s = _attn_sample(as_rows(qs), as_rows(ks), as_rows(vs), cache_t(cache_k), cache_t(cache_v))

    rw_args = (vec(rwkv_mu), vec(rwkv_w0), w2p, vec(rwkv_a0), a2p, rwkv_g2[lyr], vec(rwkv_k_k), vec(rwkv_k_a), bd_ones_r)
    pre_p = _rwprep(rwp, jnp.zeros((nbp, 1, rwc), F32), *rw_args, tm=ROW_TILE, whole_prev=False)
    pre_s = _rwprep(rws, state_shift[lyr].reshape(1, nbs, rwc), *rw_args, tm=nbs, whole_prev=True)
    r_p, ld_p, k_p, v_p, al_p, be_p, g_p = pre_p
    y_p, wkv_p = _rwchunk(r_p, ld_p, k_p, v_p, al_p, be_p)
    r_s, ld_s, k_s, v_s, al_s, be_s, g_s = pre_s
    y_s, wkv_s = _rwstep(*(as_rows(a) for a in (r_s, ld_s, k_s, v_s, al_s, be_s)), state_wkv[lyr])
    y_s = y_s.reshape(1, nbs, rwid)

    op_args = (rwkv_r_k[lyr].reshape(1, rwid), vec(rwkv_ln_w), vec(rwkv_ln_b), bd_mean_r, bd_ones_r, rwt, rb)
    x1_p, h2_p, lg_p = _outproj(x_prompt, attn_p, y_p, r_p, k_p, v_p, g_p, mod_p, g2, wt_bf, wb_bf, *op_args,
                                tm=ROW_TILE)
    x1_s, h2_s, lg_s = _outproj(xs3, attn_s.reshape(1, nbs, aw), y_s, r_s, k_s, v_s, g_s, mod_s, g2,
                                w_out[lyr][:aw], w_out[lyr][aw:], *op_args, tm=nbs)

    n_p = nbp * t
    n_valid = n_p + nbs
    assert n_p % ROUTE_TILE == 0 and nbs <= ROUTE_TILE and ROUTE_TILE % nbs == 0 and d == SUBLANES * LANES
    npad = n_p + ROUTE_TILE
    n_tiles = npad // ROUTE_TILE
    logits_t = jnp.zeros((ne, npad), F32).at[:, :n_p].set(lg_p).at[:, n_p:n_valid].set(lg_s)
    pos, wts, cnt = _route(logits_t, n_valid)
    n_slots = _slot_capacity(ne)
    per_blk = MOE_BLOCK // MOE_CHUNK
    n_blocks = -(-(n_valid * TOP_K // MOE_CHUNK + n_tiles * ne) // per_blk) + ne
    blk_e, nact, src, dst = _chunk_tables(cnt, n_slots, n_blocks)
    n_rows = n_tiles * n_slots + N_OBUF * MOE_BLOCK
    bufs = _dispatch(pos, wts, h2_p.reshape(n_p, d), None, 0, n_slots, n_rows)
    h_last = jnp.zeros((ROUTE_TILE, d), BF16).at[:nbs].set(h2_s[0])
    xs, ws = _dispatch(pos, wts, h_last, bufs, n_p // ROUTE_TILE, n_slots, n_rows)
    ys = _moe(blk_e, nact, src, dst, xs, ws, moe_w_gu[lyr], moe_b_gu[lyr], moe_w_down[lyr], moe_b_down[lyr], n_blocks)
    pos_t = pos.T
    y_prompt = _combine(pos_t, ys, x1_p, mod_p, 0, ROUTE_TILE, n_slots)
    y_sample = _combine(pos_t, ys, x1_s, mod_s, n_p // ROUTE_TILE, nbs, n_slots)

    kept = lambda a: jnp.transpose(a.reshape(nbp, n_heads, HEAD_DIM, keep), (0, 3, 1, 2))[None]
    return (y_prompt, y_sample.reshape(nbs, ts, d), kept(kt_p), kept(vt_p), wkv_p[None], rwp[:, t - 1][None],
            ks.reshape(nbs, ts, n_heads, HEAD_DIM)[None], vs.reshape(nbs, ts, n_heads, HEAD_DIM)[None],
            wkv_s[None], rws.reshape(nbs, rwc)[None])
```

```python
import functools

import numpy as np
import jax
import jax.numpy as jnp
from jax import lax
from jax.experimental import pallas as pl
from jax.experimental.pallas import tpu as pltpu

F32 = jnp.float32
BF16 = jnp.bfloat16
I32 = jnp.int32

HEAD_DIM = 64
LANES = 128
SUBLANES = 8
DILATED_PATTERNS = ((128, 1), (512, 4), (2048, 16))
WINDOW_STEPS = 128
MAX_WINDOW = 2048
PAST_LEN = 16384
ROPE_THETA = 10000.0
NORM_EPS = 1e-6
GN_EPS = 64e-5
TOP_K = 4
SWIGLU_ALPHA = 1.702
SWIGLU_LIMIT = 7.0
RW_CHUNK = 128
MOE_BLOCK = 512
MOE_CHUNK = 16
SLOT_ROWS = 256
ATTN_UNITS = 4
ROW_TILE = 512
ROUTE_TILE = 512
VMEM_LIMIT = 56 * 1024 * 1024
NEG_BIG = -1e30


def _cparams(*sem):
    return pltpu.CompilerParams(dimension_semantics=sem, vmem_limit_bytes=VMEM_LIMIT)


def _dot(a, b):
    return jnp.dot(a.astype(BF16), b.astype(BF16), preferred_element_type=F32)


def _split2(a):
    hi = a.astype(BF16)
    lo = (a - hi.astype(F32)).astype(BF16)
    return hi, lo


def _split3(a):
    hi = a.astype(BF16)
    r1 = a - hi.astype(F32)
    mid = r1.astype(BF16)
    lo = (r1 - mid.astype(F32)).astype(BF16)
    return hi, mid, lo


def _dot_x(a, e):
    e = e.astype(BF16)
    hi, mid, lo = _split3(a)
    return (jnp.dot(hi, e, preferred_element_type=F32) + jnp.dot(mid, e, preferred_element_type=F32)
            + jnp.dot(lo, e, preferred_element_type=F32))


def _xdot(e, a):
    e = e.astype(BF16)
    hi, mid, lo = _split3(a)
    return (jnp.dot(e, hi, preferred_element_type=F32) + jnp.dot(e, mid, preferred_element_type=F32)
            + jnp.dot(e, lo, preferred_element_type=F32))


def _dot3(a, b):
    ah, al = _split2(a)
    bh, bl = _split2(b)
    return (jnp.dot(ah, bh, preferred_element_type=F32) + jnp.dot(ah, bl, preferred_element_type=F32)
            + jnp.dot(al, bh, preferred_element_type=F32))


def _dot3_nt(a, b):
    ah, al = _split2(a)
    bh, bl = _split2(b)
    dn = (((1,), (1,)), ((), ()))
    return (lax.dot_general(ah, bh, dn, preferred_element_type=F32)
            + lax.dot_general(ah, bl, dn, preferred_element_type=F32)
            + lax.dot_general(al, bh, dn, preferred_element_type=F32))


def _mm(a, w):
    if w.dtype == BF16:
        return jnp.dot(a.astype(BF16), w, preferred_element_type=F32)
    return _dot3(a, w)


def _sigmoid(x):
    return 1.0 / (1.0 + jnp.exp(-x))


def _ada_kernel(c_ref, w_ref, b_ref, o_ref):
    c = c_ref[...]
    o_ref[...] = _dot3(c * _sigmoid(c), w_ref[...]) + b_ref[...]


def _ada(c_all, w_ada, b_ada):
    rows, d = c_all.shape
    n = w_ada.shape[1]
    tn = n // 4
    return pl.pallas_call(
        _ada_kernel,
        grid=(n // tn,),
        in_specs=[pl.BlockSpec((rows, d), lambda j: (0, 0)),
                  pl.BlockSpec((d, tn), lambda j: (0, j)),
                  pl.BlockSpec((1, tn), lambda j: (0, j))],
        out_specs=pl.BlockSpec((rows, tn), lambda j: (0, j)),
        out_shape=jax.ShapeDtypeStruct((rows, n), F32),
        compiler_params=_cparams("arbitrary"),
        name="ada",
    )(c_all, w_ada, b_ada.reshape(1, n))


def _inproj_kernel(x_ref, sh_ref, sc_ref, g_ref, w_ref, cos_ref, sin_ref, qg_ref, kg_ref, bd_ref,
                   q_ref, k_ref, v_ref, rw_ref, *maybe_kv_t, aw, first_kept):
    x = x_ref[...]
    ms = jnp.mean(x * x, axis=-1, keepdims=True)
    h = x * lax.rsqrt(ms + NORM_EPS) * g_ref[...] * (1.0 + sc_ref[...]) + sh_ref[...]
    proj = _mm(h, w_ref[...])
    cos = cos_ref[...]
    sin = sin_ref[...]
    lane = lax.broadcasted_iota(I32, (1, aw), 1)
    first_half = (lane % HEAD_DIM) < (HEAD_DIM // 2)
    bd = bd_ref[...]

    def norm_rope(t, g):
        tn = t * lax.rsqrt((_dot if w_ref.dtype == BF16 else _dot_x)(t * t, bd) + NORM_EPS) * g
        rot = jnp.where(first_half, pltpu.roll(tn, aw - HEAD_DIM // 2, 1), pltpu.roll(tn, HEAD_DIM // 2, 1))
        return tn * cos + rot * sin

    scale = 1.0 / np.sqrt(HEAD_DIM).astype(np.float32)
    q_ref[...] = norm_rope(proj[:, :aw], qg_ref[...]) * scale
    k = norm_rope(proj[:, aw:2 * aw], kg_ref[...])
    v = proj[:, 2 * aw:3 * aw]
    k_ref[...] = k
    v_ref[...] = v
    rw_ref[...] = proj[:, 3 * aw:]
    if maybe_kv_t:
        kt_ref, vt_ref = maybe_kv_t
        kept = pl.program_id(1) >= first_kept

        @pl.when(kept)
        def _():
            kt_ref[...] = k.T
            vt_ref[...] = v.T

        @pl.when(jnp.logical_not(kept))
        def _():
            kt_ref[...] = jnp.zeros_like(kt_ref)
            vt_ref[...] = jnp.zeros_like(vt_ref)


def _inproj(x3, mod3, g1, w_in_bf, cos_t, sin_t, qg, kg, bd_mean, tm, keep=0):
    nb, t, d = x3.shape
    r = mod3.shape[1]
    ncol = w_in_bf.shape[1]
    aw = cos_t.shape[1]
    rwc = ncol - 3 * aw
    grid = (nb, t // tm)
    row = lambda b, i: (b, i, 0)
    const = lambda b, i: (0, 0)
    mod_spec = lambda s: pl.BlockSpec((None, r, d), (lambda b, i: (b, 0, s)) if r == 1 else (lambda b, i: (b, i, s)))
    out_specs = [pl.BlockSpec((None, tm, aw), row)] * 3 + [pl.BlockSpec((None, tm, rwc), row)]
    out_shape = [jax.ShapeDtypeStruct((nb, t, aw), F32)] * 3 + [jax.ShapeDtypeStruct((nb, t, rwc), F32)]
    first_kept = (t - keep) // tm
    if keep:
        assert keep % tm == 0 and (t - keep) % tm == 0
        kept_spec = pl.BlockSpec((None, aw, tm), lambda b, i: (b, 0, jnp.maximum(i - first_kept, 0)))
        out_specs += [kept_spec, kept_spec]
        out_shape += [jax.ShapeDtypeStruct((nb, aw, keep), F32)] * 2
    outs = pl.pallas_call(
        functools.partial(_inproj_kernel, aw=aw, first_kept=first_kept),
        grid=grid,
        in_specs=[pl.BlockSpec((None, tm, d), row), mod_spec(0), mod_spec(1),
                  pl.BlockSpec((1, d), const), pl.BlockSpec((d, ncol), const),
                  pl.BlockSpec((tm, aw), lambda b, i: (i, 0)), pl.BlockSpec((tm, aw), lambda b, i: (i, 0)),
                  pl.BlockSpec((1, aw), const), pl.BlockSpec((1, aw), const), pl.BlockSpec((aw, aw), const)],
        out_specs=out_specs,
        out_shape=out_shape,
        compiler_params=_cparams("parallel", "arbitrary"),
        name="inproj",
    )(x3, mod3, mod3, g1, w_in_bf, cos_t, sin_t, qg, kg, bd_mean)
    return outs


def _attn_prompt_kernel(q_ref, k_ref, v_ref, o_ref, kp_ref, vp_ref, m_ref, l_ref, acc_ref, *, t, pad):
    nq = WINDOW_STEPS
    nk = 2 * WINDOW_STEPS
    kp_ref[pl.ds(0, pad), :] = jnp.zeros((pad, LANES), F32)
    vp_ref[pl.ds(0, pad), :] = jnp.zeros((pad, LANES), F32)
    kp_ref[pl.ds(pad, t), :] = k_ref[...]
    vp_ref[pl.ds(pad, t), :] = v_ref[...]
    head0 = lax.broadcasted_iota(I32, (1, LANES), 1) < HEAD_DIM
    qi = lax.broadcasted_iota(I32, (nq, nk), 0)
    kj = lax.broadcasted_iota(I32, (nq, nk), 1)
    steps_back = qi + nq - kj
    band = (steps_back >= 0) & (steps_back <= WINDOW_STEPS)
    has_past = kj >= nq

    hsels = (head0, jnp.logical_not(head0))
    nt = (((1,), (1,)), ((), ()))

    for p, (_, d) in enumerate(DILATED_PATTERNS):
        def units(g, carry, p=p, d=d):
            rows_q, kb, vb, valid, qh = [], [], [], [], []
            for j in range(ATTN_UNITS):
                u = g * ATTN_UNITS + j
                res = u % d
                blk = u // d
                q_start = res + d * nq * blk
                k_start = pad + q_start - d * nq
                if d == 1:
                    rows_q.append(pl.ds(q_start, nq))
                    rows_k = pl.ds(k_start, nk)
                else:
                    rows_q.append(pl.ds(q_start, nq, stride=d))
                    rows_k = pl.ds(k_start, nk, stride=d)
                q = q_ref[rows_q[j], :]
                kb.append(kp_ref[rows_k, :].astype(BF16))
                vb.append(vp_ref[rows_k, :].astype(BF16))
                valid.append(band & (has_past | (blk > 0)))
                qh.append([jnp.where(hsel, q, 0.0).astype(BF16) for hsel in hsels])
            chains = [(j, h) for j in range(ATTN_UNITS) for h in range(2)]
            s = [lax.dot_general(qh[j][h], kb[j], nt, preferred_element_type=F32) for j, h in chains]
            s = [jnp.where(valid[j], sc, NEG_BIG) for (j, h), sc in zip(chains, s)]
            mx = [jnp.max(sc, axis=-1, keepdims=True) for sc in s]
            e = [jnp.exp(sc - m) for sc, m in zip(s, mx)]
            den = [jnp.sum(ec, axis=-1, keepdims=True) for ec in e]
            o = [jnp.dot(ec.astype(BF16), vb[j], preferred_element_type=F32) for (j, h), ec in zip(chains, e)]
            for j in range(ATTN_UNITS):
                m_ref[p, rows_q[j], :] = jnp.where(head0, mx[2 * j], mx[2 * j + 1])
                l_ref[p, rows_q[j], :] = jnp.where(head0, den[2 * j], den[2 * j + 1])
                acc_ref[p, rows_q[j], :] = jnp.where(head0, o[2 * j], o[2 * j + 1])
            return carry

        lax.fori_loop(0, t // nq // ATTN_UNITS, units, 0)

    rows = 256

    def merge(i, carry):
        sl = pl.ds(pl.multiple_of(i * rows, rows), rows)
        m0, m1, m2 = m_ref[0, sl, :], m_ref[1, sl, :], m_ref[2, sl, :]
        mm = jnp.maximum(jnp.maximum(m0, m1), m2)
        w0, w1, w2 = jnp.exp(m0 - mm), jnp.exp(m1 - mm), jnp.exp(m2 - mm)
        num = w0 * acc_ref[0, sl, :] + w1 * acc_ref[1, sl, :] + w2 * acc_ref[2, sl, :]
        den = w0 * l_ref[0, sl, :] + w1 * l_ref[1, sl, :] + w2 * l_ref[2, sl, :]
        o_ref[sl, :] = num / den
        return carry

    lax.fori_loop(0, t // rows, merge, 0)


def _attn_prompt(q, k, v):
    nb, t, aw = q.shape
    pad = MAX_WINDOW
    assert t % MAX_WINDOW == 0
    spec = pl.BlockSpec((None, t, LANES), lambda b, hp: (b, 0, hp))
    return pl.pallas_call(
        functools.partial(_attn_prompt_kernel, t=t, pad=pad),
        grid=(nb, aw // LANES),
        in_specs=[spec, spec, spec],
        out_specs=spec,
        out_shape=jax.ShapeDtypeStruct((nb, t, aw), F32),
        scratch_shapes=[pltpu.VMEM((pad + t, LANES), F32), pltpu.VMEM((pad + t, LANES), F32),
                        pltpu.VMEM((3, t, LANES), F32), pltpu.VMEM((3, t, LANES), F32),
                        pltpu.VMEM((3, t, LANES), F32)],
        compiler_params=_cparams("parallel", "parallel"),
        name="attn_prompt",
    )(q, k, v)


def _attn_sample_kernel(q_ref, kn_ref, vn_ref, kt_ref, vt_ref, o_ref, *, nh, w):
    aw = nh * HEAD_DIM
    dist = w - lax.broadcasted_iota(I32, (1, w), 1)
    mult = jnp.zeros((1, w), F32)
    for win, d in DILATED_PATTERNS:
        mult = mult + jnp.where((dist % d == 0) & (dist <= win), 1.0, 0.0)
    n_pat = float(len(DILATED_PATTERNS))
    q_col = _col(q_ref[...], aw)
    kn_col = _col(kn_ref[...], aw)
    vn_col = _col(vn_ref[...], aw)
    heads = range(nh)
    hs = [pl.ds(h * HEAD_DIM, HEAD_DIM) for h in heads]
    cut = lambda col, h: col[h * HEAD_DIM:(h + 1) * HEAD_DIM]
    s = [jnp.sum(kt_ref[hs[h], :] * cut(q_col, h), axis=0, keepdims=True) for h in heads]
    s_self = [jnp.sum(cut(q_col, h) * cut(kn_col, h), axis=0, keepdims=True) for h in heads]
    s = [jnp.where(mult > 0.0, sh, NEG_BIG) for sh in s]
    mx = [jnp.maximum(jnp.max(sh, axis=1, keepdims=True), ss) for sh, ss in zip(s, s_self)]
    pr = [mult * jnp.exp(sh - m) for sh, m in zip(s, mx)]
    p_self = [n_pat * jnp.exp(ss - m) for ss, m in zip(s_self, mx)]
    den = [jnp.sum(p, axis=1, keepdims=True) + ps for p, ps in zip(pr, p_self)]
    num = [jnp.sum(vt_ref[hs[h], :] * pr[h], axis=1, keepdims=True) + p_self[h] * cut(vn_col, h) for h in heads]
    outs = [n / dn for n, dn in zip(num, den)]
    o_ref[...] = _row(jnp.concatenate(outs, axis=0), aw)


def _attn_sample(q, kn, vn, cache_kt, cache_vt):
    nb, _, aw = q.shape
    w = cache_kt.shape[2]
    assert w == MAX_WINDOW
    row = pl.BlockSpec((None, 1, aw), lambda i: (i, 0, 0))
    mat = pl.BlockSpec((None, aw, w), lambda i: (i, 0, 0))
    return pl.pallas_call(
        functools.partial(_attn_sample_kernel, nh=aw // HEAD_DIM, w=w),
        grid=(nb,),
        in_specs=[row, row, row, mat, mat],
        out_specs=row,
        out_shape=jax.ShapeDtypeStruct((nb, 1, aw), F32),
        compiler_params=_cparams("parallel"),
        name="attn_sample",
    )(q, kn, vn, cache_kt, cache_vt)


def _rwprep_kernel(p_ref, prev_ref, first_ref, mu_ref, w0_ref, w2_ref, a0_ref, a2_ref, g2_ref, kk_ref, ka_ref,
                   bd_ref, r_ref, ld_ref, k_ref, v_ref, al_ref, be_ref, g_ref, *, rwid, whole_prev):
    p = p_ref[...]
    if whole_prev:
        prev = first_ref[...]
    else:
        tm = p.shape[0]
        before = jnp.where(pl.program_id(1) == 0, first_ref[...], prev_ref[7:8, :])
        rowi = lax.broadcasted_iota(I32, (tm, 1), 0)
        prev = jnp.where(rowi == 0, before, pltpu.roll(p, 1, 0))
    xs = p + mu_ref[...] * (prev - p)
    r = xs[:, :rwid]
    k = xs[:, rwid:2 * rwid]
    v = xs[:, 2 * rwid:3 * rwid]
    xwa = xs[:, 3 * rwid:3 * rwid + LANES]
    xg = xs[:, 3 * rwid + LANES:]
    z = w0_ref[...] + _dot3(jnp.tanh(xwa), w2_ref[...])
    softplus_neg = jnp.maximum(-z, 0.0) + jnp.log(1.0 + jnp.exp(-jnp.abs(z)))
    w = -softplus_neg - 0.5
    a = _sigmoid(a0_ref[...] + _dot3(xwa, a2_ref[...]))
    g = _dot3(_sigmoid(xg), g2_ref[...])
    kk = k * kk_ref[...]
    norm = jnp.sqrt(_dot_x(kk * kk, bd_ref[...]))
    kk = kk / jnp.maximum(norm, 1e-12)
    r_ref[...] = r
    ld_ref[...] = -jnp.exp(w)
    k_ref[...] = k * (1.0 + (a - 1.0) * ka_ref[...])
    v_ref[...] = v
    al_ref[...] = -kk
    be_ref[...] = kk * a
    g_ref[...] = g


def _rwprep(rw, first, mu, w0, w2p, a0, a2p, g2, k_k, k_a, bd_ones, tm, whole_prev):
    nb, t, rwc = rw.shape
    rwid = w0.shape[1]
    grid = (nb, t // tm)
    row = lambda b, i: (b, i, 0)
    const = lambda b, i: (0, 0)
    if whole_prev:
        prev_spec = pl.BlockSpec((None, tm, rwc), row)
        first_spec = pl.BlockSpec((None, tm, rwc), row)
    else:
        prev_spec = pl.BlockSpec((None, 8, rwc), lambda b, i: (b, jnp.maximum(i * (tm // 8) - 1, 0), 0))
        first_spec = pl.BlockSpec((None, 1, rwc), lambda b, i: (b, 0, 0))
    vec = pl.BlockSpec((1, rwid), const)
    out = pl.BlockSpec((None, tm, rwid), row)
    return pl.pallas_call(
        functools.partial(_rwprep_kernel, rwid=rwid, whole_prev=whole_prev),
        grid=grid,
        in_specs=[pl.BlockSpec((None, tm, rwc), row), prev_spec, first_spec,
                  pl.BlockSpec((1, rwc), const), vec, pl.BlockSpec((LANES, rwid), const),
                  vec, pl.BlockSpec((LANES, rwid), const), pl.BlockSpec((LANES, rwid), const), vec, vec,
                  pl.BlockSpec((rwid, rwid), const)],
        out_specs=[out] * 7,
        out_shape=[jax.ShapeDtypeStruct((nb, t, rwid), F32)] * 7,
        compiler_params=_cparams("parallel", "parallel"),
        name="rwprep",
    )(rw, rw, first, mu, w0, w2p, a0, a2p, g2, k_k, k_a, bd_ones)


def _rwchunk_kernel(r_ref, ld_ref, k_ref, v_ref, al_ref, be_ref, y_ref, st_ref, z_ref, *, nb):
    c = RW_CHUNK
    ci = pl.program_id(1)

    @pl.when(ci == 0)
    def _():
        z_ref[...] = jnp.zeros_like(z_ref)

    ti = lax.broadcasted_iota(I32, (c, c), 0)
    si = lax.broadcasted_iota(I32, (c, c), 1)
    low_incl = si <= ti
    low_strict = si < ti
    diag = si == ti
    tri = jnp.where(low_incl, 1.0, 0.0).astype(BF16)
    eye = jnp.where(diag, 1.0, 0.0)
    head0 = lax.broadcasted_iota(I32, (1, LANES), 1) < HEAD_DIM
    hsels = (head0, jnp.logical_not(head0))
    same_head = (ti < HEAD_DIM) == (si < HEAD_DIM)
    nt = (((1,), (1,)), ((), ()))
    batches = range(nb)
    chains = [(b, h) for b in batches for h in range(2)]

    cum = [_xdot(tri, ld_ref[b]) for b in batches]
    tot = [cm[c - 1:c, :] for cm in cum]
    e_neg = [jnp.exp(-cm) for cm in cum]
    at = [al_ref[b] * jnp.exp(cum[b] - ld_ref[b]) for b in batches]
    rt = [r_ref[b] * jnp.exp(cum[b]) for b in batches]
    rhs_t = [jnp.concatenate([be_ref[b] * e_neg[b], k_ref[b] * e_neg[b]], axis=0).astype(BF16) for b in batches]
    vb = [v_ref[b].astype(BF16) for b in batches]
    at_h = [jnp.where(hsels[h], at[b], 0.0) for b, h in chains]
    rt_h = [jnp.where(hsels[h], rt[b], 0.0) for b, h in chains]
    a4 = [lax.dot_general(jnp.concatenate([a, r], axis=0).astype(BF16), rhs_t[b], nt, preferred_element_type=F32)
          for (b, h), a, r in zip(chains, at_h, rt_h)]
    a_ab = [jnp.where(low_strict, m[:c, :c], 0.0) for m in a4]
    a_ak = [jnp.where(low_strict, m[:c, c:], 0.0).astype(BF16) for m in a4]
    a_r = [jnp.concatenate([jnp.where(low_incl, m[c:, :c], 0.0), jnp.where(low_incl, m[c:, c:], 0.0)],
                           axis=1).astype(BF16) for m in a4]
    pw = [m.astype(BF16) for m in a_ab]
    inv = [eye + m for m in a_ab]
    for _ in range(int(np.log2(c)) - 1):
        pw = [jnp.dot(m, m, preferred_element_type=F32).astype(BF16) for m in pw]
        inv = [i + jnp.dot(i.astype(BF16), m, preferred_element_type=F32) for i, m in zip(inv, pw)]
    akv = [jnp.dot(m, vb[b], preferred_element_type=F32) for (b, h), m in zip(chains, a_ak)]
    x = [_dot(i, jnp.concatenate([kv, a], axis=1)) for i, kv, a in zip(inv, akv, at_h)]
    u0_h = [m[:, :LANES] for m in x]
    at2_h = [m[:, LANES:] for m in x]
    y0_h = [jnp.dot(ar, jnp.concatenate([u0, v_ref[b]], axis=0).astype(BF16), preferred_element_type=F32)
            for (b, h), ar, u0 in zip(chains, a_r, u0_h)]
    rt2_h = [r + jnp.dot(ar[:, :c], a2.astype(BF16), preferred_element_type=F32)
             for r, ar, a2 in zip(rt_h, a_r, at2_h)]
    z = [z_ref[b] for b in batches]
    uy = [_dot(jnp.concatenate([at2_h[2 * b] + at2_h[2 * b + 1], rt2_h[2 * b] + rt2_h[2 * b + 1]], axis=0), z[b])
          for b in batches]
    u = [uy[b][:c] + jnp.where(head0, u0_h[2 * b], u0_h[2 * b + 1]) for b in batches]
    for b in batches:
        y_ref[b] = uy[b][c:] + jnp.where(head0, y0_h[2 * b], y0_h[2 * b + 1])
    e_end = [jnp.exp(tot[b] - cum[b]) for b in batches]
    lhs_t = [jnp.concatenate([be_ref[b] * e_end[b], k_ref[b] * e_end[b]], axis=0) for b in batches]
    zadd = [_dot(lhs_t[b].T, jnp.concatenate([u[b], v_ref[b]], axis=0)) for b in batches]
    for b in batches:
        dcol = jnp.sum(jnp.where(diag, jnp.broadcast_to(jnp.exp(tot[b]), (c, c)), 0.0), axis=1, keepdims=True)
        z_ref[b] = dcol * z[b] + jnp.where(same_head, zadd[b], 0.0)

    @pl.when(ci == pl.num_programs(1) - 1)
    def _():
        for b in batches:
            s = z_ref[b].T
            st_ref[b, 0] = s[:HEAD_DIM, :HEAD_DIM]
            st_ref[b, 1] = s[HEAD_DIM:, HEAD_DIM:]


def _rwchunk(r, ld, k, v, al, be):
    nb, t, rwid = r.shape
    c = RW_CHUNK
    assert t % c == 0 and c == LANES
    npair = rwid // LANES
    seq = pl.BlockSpec((nb, c, LANES), lambda hp, ci: (0, ci, hp))
    return pl.pallas_call(
        functools.partial(_rwchunk_kernel, nb=nb),
        grid=(npair, t // c),
        in_specs=[seq] * 6,
        out_specs=[seq, pl.BlockSpec((nb, 2, HEAD_DIM, HEAD_DIM), lambda hp, ci: (0, hp, 0, 0))],
        out_shape=[jax.ShapeDtypeStruct((nb, t, rwid), F32),
                   jax.ShapeDtypeStruct((nb, 2 * npair, HEAD_DIM, HEAD_DIM), F32)],
        scratch_shapes=[pltpu.VMEM((nb, LANES, LANES), F32)],
        compiler_params=_cparams("parallel", "arbitrary"),
        name="rwchunk",
    )(r, ld, k, v, al, be)


def _col(row, width):
    return jnp.broadcast_to(row, (LANES, width)).T[:, 0:1]


def _row(col, width):
    return jnp.broadcast_to(col, (width, LANES)).T[0:1, :]


def _rwstep_kernel(r_ref, ld_ref, k_ref, v_ref, al_ref, be_ref, s_ref, y_ref, so_ref, *, bb, nh):
    rwid = nh * HEAD_DIM

    def one(b, carry):
        r, dcy, k, al, be = (ref[b] for ref in (r_ref, ld_ref, k_ref, al_ref, be_ref))
        dcy = jnp.exp(dcy)
        v_col = _col(v_ref[b], rwid)
        heads = range(nh)
        hs = [slice(h * HEAD_DIM, (h + 1) * HEAD_DIM) for h in heads]
        st = [s_ref[b, h] for h in heads]
        sa = [jnp.sum(st[h] * al[:, hs[h]], axis=1, keepdims=True) for h in heads]
        st = [st[h] * dcy[:, hs[h]] + sa[h] * be[:, hs[h]] + v_col[hs[h], :] * k[:, hs[h]] for h in heads]
        for h in heads:
            so_ref[b, h] = st[h]
        ys = [jnp.sum(st[h] * r[:, hs[h]], axis=1, keepdims=True) for h in heads]
        y_ref[b] = _row(jnp.concatenate(ys, axis=0), rwid)
        return carry

    lax.fori_loop(0, bb, one, 0)


def _rwstep(r, ld, k, v, al, be, state):
    nb, _, rwid = r.shape
    nh = state.shape[1]
    bb = 8
    row = pl.BlockSpec((bb, 1, rwid), lambda i: (i, 0, 0))
    st = pl.BlockSpec((bb, nh, HEAD_DIM, HEAD_DIM), lambda i: (i, 0, 0, 0))
    return pl.pallas_call(
        functools.partial(_rwstep_kernel, bb=bb, nh=nh),
        grid=(nb // bb,),
        in_specs=[row] * 6 + [st],
        out_specs=[row, st],
        out_shape=[jax.ShapeDtypeStruct((nb, 1, rwid), F32), jax.ShapeDtypeStruct(state.shape, F32)],
        compiler_params=_cparams("parallel"),
        name="rwstep",
    )(r, ld, k, v, al, be, state)


def _outproj_kernel(x_ref, at_ref, y_ref, r_ref, k_ref, v_ref, g_ref, gt_ref, sh_ref, sc_ref, g2_ref,
                    wt_ref, wb_ref, rk_ref, lw_ref, lb_ref, bdm_ref, bd1_ref, rwt_ref, rb_ref,
                    x1_ref, h2_ref, lg_ref):
    y = y_ref[...]
    bdm = bdm_ref[...]
    mean = _dot_x(y, bdm)
    yc = y - mean
    var = _dot_x(yc * yc, bdm)
    yn = yc * lax.rsqrt(var + GN_EPS) * lw_ref[...] + lb_ref[...]
    v = v_ref[...]
    bonus = _dot_x(r_ref[...] * k_ref[...] * rk_ref[...], bd1_ref[...]) * v
    rw = (yn + bonus) * g_ref[...]
    mix = _mm(at_ref[...], wt_ref[...]) + _mm(rw, wb_ref[...])
    x1 = x_ref[...] + gt_ref[...] * mix
    x1_ref[...] = x1
    ms = jnp.mean(x1 * x1, axis=-1, keepdims=True)
    h2 = x1 * lax.rsqrt(ms + NORM_EPS) * g2_ref[...] * (1.0 + sc_ref[...]) + sh_ref[...]
    h2_ref[...] = h2.astype(h2_ref.dtype)
    lg_ref[...] = _dot3_nt(rwt_ref[...], h2) + rb_ref[...]


def _outproj(x3, attn, y, r, k, v, g, mod3, g2, wt_bf, wb_bf, r_k, ln_w, ln_b, bd_mean, bd_ones, rwt, rb, tm):
    nb, t, d = x3.shape
    rmod = mod3.shape[1]
    aw = attn.shape[2]
    ne = rwt.shape[0]
    nt = t // tm
    grid = (nb, nt)
    row = lambda b, i: (b, i, 0)
    const = lambda b, i: (0, 0)
    mod_spec = lambda s: pl.BlockSpec((None, rmod, d), (lambda b, i: (b, 0, s)) if rmod == 1 else (lambda b, i: (b, i, s)))
    half = pl.BlockSpec((None, tm, aw), row)
    vec = pl.BlockSpec((1, aw), const)
    return pl.pallas_call(
        _outproj_kernel,
        grid=grid,
        in_specs=[pl.BlockSpec((None, tm, d), row)] + [half] * 6 + [mod_spec(2), mod_spec(3), mod_spec(4),
                  pl.BlockSpec((1, d), const), pl.BlockSpec((aw, d), const), pl.BlockSpec((aw, d), const),
                  vec, vec, vec, pl.BlockSpec((aw, aw), const), pl.BlockSpec((aw, aw), const),
                  pl.BlockSpec((ne, d), const), pl.BlockSpec((ne, 1), const)],
        out_specs=[pl.BlockSpec((None, tm, d), row), pl.BlockSpec((None, tm, d), row),
                   pl.BlockSpec((ne, tm), lambda b, i: (0, b * nt + i))],
        out_shape=[jax.ShapeDtypeStruct((nb, t, d), F32), jax.ShapeDtypeStruct((nb, t, d), BF16),
                   jax.ShapeDtypeStruct((ne, nb * t), F32)],
        compiler_params=_cparams("parallel", "parallel"),
        name="outproj",
    )(x3, attn, y, r, k, v, g, mod3, mod3, mod3, g2, wt_bf, wb_bf, r_k, ln_w, ln_b, bd_mean, bd_ones, rwt, rb)


def _route_kernel(lg_ref, pos_ref, wt_ref, cnt_ref, *, n_valid):
    i = pl.program_id(0)
    ne, tn = lg_ref.shape
    lg = lg_ref[...]
    eidx = lax.broadcasted_iota(I32, (ne, tn), 0).astype(F32)
    tok = i * tn + lax.broadcasted_iota(I32, (1, tn), 1)
    live = tok < n_valid
    vals, hots = [], []
    for _ in range(TOP_K):
        mx = jnp.max(lg, axis=0, keepdims=True)
        pick = jnp.min(jnp.where(lg == mx, eidx, float(ne)), axis=0, keepdims=True)
        hot = eidx == pick
        vals.append(mx)
        hots.append(hot)
        lg = jnp.where(hot, -jnp.inf, lg)
    ex = [jnp.exp(vv - vals[0]) for vv in vals]
    den = ex[0] + ex[1] + ex[2] + ex[3]
    for kk in range(TOP_K):
        wt_ref[pl.ds(kk, 1), :] = ex[kk] / den
    hot_all = jnp.zeros((ne, tn), F32)
    for hot in hots:
        hot_all = hot_all + jnp.where(hot & live, 1.0, 0.0)
    ri = lax.broadcasted_iota(I32, (tn, tn), 0)
    cj = lax.broadcasted_iota(I32, (tn, tn), 1)
    upper = jnp.where(ri <= cj, 1.0, 0.0).astype(BF16)
    before = jnp.dot(hot_all.astype(BF16), upper, preferred_element_type=F32) - hot_all
    cnt = jnp.sum(hot_all, axis=1, keepdims=True)
    padded = jnp.ceil(cnt / MOE_CHUNK) * MOE_CHUNK
    er = lax.broadcasted_iota(I32, (ne, ne), 0)
    ec = lax.broadcasted_iota(I32, (ne, ne), 1)
    lower_strict = jnp.where(ec < er, 1.0, 0.0)
    off = _xdot(lower_strict, jnp.broadcast_to(padded, (ne, LANES)))[:, 0:1]
    for kk, hot in enumerate(hots):
        pos = jnp.sum(jnp.where(hot, off + before, 0.0), axis=0, keepdims=True)
        pos_ref[pl.ds(kk, 1), :] = jnp.where(live, pos, -1.0).astype(I32)
    cnt_ref[...] = jnp.broadcast_to(cnt, (ne, LANES))


def _route(logits_t, n_valid):
    ne, npad = logits_t.shape
    tn = ROUTE_TILE
    nt = npad // tn
    tile = lambda rows: pl.BlockSpec((rows, tn), lambda i: (0, i))
    pos, wts, cnt = pl.pallas_call(
        functools.partial(_route_kernel, n_valid=n_valid),
        grid=(nt,),
        in_specs=[tile(ne)],
        out_specs=[tile(TOP_K), tile(TOP_K), pl.BlockSpec((None, ne, LANES), lambda i: (i, 0, 0))],
        out_shape=[jax.ShapeDtypeStruct((TOP_K, npad), I32), jax.ShapeDtypeStruct((TOP_K, npad), F32),
                   jax.ShapeDtypeStruct((nt, ne, LANES), F32)],
        compiler_params=_cparams("parallel"),
        name="route",
    )(logits_t)
    return pos, wts, cnt[:, :, 0].astype(I32)


def _slot_capacity(ne):
    return -(-(TOP_K * ROUTE_TILE + ne * (MOE_CHUNK - 1)) // SLOT_ROWS) * SLOT_ROWS


def _dispatch_kernel(pos_ref, wt_ref, h_ref, *rest, n_slots):
    xs_ref, ws_ref = rest[-2:]
    tn = h_ref.shape[0]
    h = h_ref[...]
    pos = [pos_ref[pl.ds(kk, 1), :] for kk in range(TOP_K)]
    wts = [wt_ref[pl.ds(kk, 1), :] for kk in range(TOP_K)]

    def rows(c, carry):
        r0 = pl.multiple_of(c * SLOT_ROWS, SLOT_ROWS)
        slot = r0 + lax.broadcasted_iota(I32, (SLOT_ROWS, tn), 0)
        pw = jnp.zeros((SLOT_ROWS, tn), F32)
        for kk in range(TOP_K):
            pw = pw + jnp.where(slot == pos[kk], wts[kk], 0.0)
        p = jnp.where(pw > 0.0, 1.0, 0.0)
        xs_ref[pl.ds(r0, SLOT_ROWS), :] = jnp.dot(p.astype(BF16), h, preferred_element_type=F32).astype(BF16)
        ws_ref[pl.ds(r0, SLOT_ROWS), :] = jnp.broadcast_to(jnp.sum(pw, axis=1, keepdims=True), (SLOT_ROWS, LANES))
        return carry

    lax.fori_loop(0, n_slots // SLOT_ROWS, rows, 0)


def _dispatch(pos, wts, h, bufs, tile0, n_slots, n_rows):
    n, d = h.shape
    tn = ROUTE_TILE
    bufs = () if bufs is None else tuple(bufs)
    return pl.pallas_call(
        functools.partial(_dispatch_kernel, n_slots=n_slots),
        grid=(n // tn,),
        in_specs=[pl.BlockSpec((TOP_K, tn), lambda i: (0, tile0 + i)), pl.BlockSpec((TOP_K, tn), lambda i: (0, tile0 + i)),
                  pl.BlockSpec((tn, d), lambda i: (i, 0))] + [pl.BlockSpec(memory_space=pl.ANY)] * len(bufs),
        out_specs=[pl.BlockSpec((n_slots, d), lambda i: (tile0 + i, 0)),
                   pl.BlockSpec((n_slots, LANES), lambda i: (tile0 + i, 0))],
        out_shape=[jax.ShapeDtypeStruct((n_rows, d), BF16), jax.ShapeDtypeStruct((n_rows, LANES), F32)],
        input_output_aliases={3: 0, 4: 1} if bufs else {},
        compiler_params=_cparams("parallel"),
        name="dispatch",
    )(pos, wts, h, *bufs)


def _chunk_tables(cnt, n_slots, n_blocks):
    nt, ne = cnt.shape
    per_blk = MOE_BLOCK // MOE_CHUNK
    nch = -(-cnt // MOE_CHUNK)
    seg0 = (jnp.cumsum(nch, axis=1) - nch) + (jnp.arange(nt) * (n_slots // MOE_CHUNK))[:, None]
    cum_t = jnp.cumsum(nch, axis=0)
    total = cum_t[-1]
    blocks = -(-total // per_blk)
    blk_end = jnp.cumsum(blocks)
    blk = jnp.arange(n_blocks)
    blk_e = jnp.minimum(jnp.sum(blk_end[None, :] <= blk[:, None], axis=1), ne - 1).astype(I32)
    nact = blk_end[-1].reshape(1).astype(I32)
    pick_e = blk_e[:, None] == jnp.arange(ne)[None, :]
    of_e = lambda tab: jnp.sum(jnp.where(pick_e, tab[None, :], 0), axis=1)
    col_e = lambda tab: jnp.sum(jnp.where(pick_e[:, None, :], tab[None, :, :], 0), axis=2)
    q = ((blk - of_e(blk_end - blocks)) * per_blk)[:, None] + jnp.arange(per_blk)[None, :]
    live = (q < of_e(total)[:, None]) & (blk < nact[0])[:, None]
    t_n = jnp.minimum(jnp.sum(col_e(cum_t)[:, None, :] <= q[:, :, None], axis=2), nt - 1)
    pick_t = t_n[:, :, None] == jnp.arange(nt)[None, None, :]
    of_t = lambda tab: jnp.sum(jnp.where(pick_t, tab[:, None, :], 0), axis=2)
    src = jnp.where(live, of_t(col_e(seg0)) + q - of_t(col_e(cum_t - nch)), 0)
    n = blk[:, None] * per_blk + jnp.arange(per_blk)[None, :]
    spare = nt * (n_slots // MOE_CHUNK) + n % (N_OBUF * per_blk)
    dst = jnp.where(live, src, spare)
    return blk_e, nact, src.reshape(-1).astype(I32), dst.reshape(-1).astype(I32)


N_OBUF = 2


def _moe_kernel(be_ref, nact_ref, src_ref, dst_ref, xs_ref, ws_ref, wgu_ref, bgu_ref, wd_ref, bd_ref, out_in_ref,
                out_ref, xbuf, wbuf, obuf, gsem, ssem, wgu_bf, wd_bf, *, dff):
    del out_in_ref
    i = pl.program_id(0)
    nact = nact_ref[0]
    per_blk = MOE_BLOCK // MOE_CHUNK
    rows = lambda c: pl.ds(pl.multiple_of(c * MOE_CHUNK, MOE_CHUNK), MOE_CHUNK)

    def gathers(blk, slot):
        copies = []
        for m in range(per_blk):
            c = src_ref[blk * per_blk + m]
            copies.append(pltpu.make_async_copy(xs_ref.at[rows(c), :], xbuf.at[slot, rows(m), :], gsem.at[slot]))
            copies.append(pltpu.make_async_copy(ws_ref.at[rows(c), :], wbuf.at[slot, rows(m), :], gsem.at[slot]))
        return copies

    def scatters(blk, slot):
        return [pltpu.make_async_copy(obuf.at[slot, rows(m), :], out_ref.at[rows(dst_ref[blk * per_blk + m]), :],
                                      ssem.at[slot]) for m in range(per_blk)]

    @pl.when(i == 0)
    def _():
        for cp in gathers(0, 0):
            cp.start()

    @pl.when(i < nact)
    def _():
        slot = i % 2
        for cp in gathers(i, slot):
            cp.wait()

        @pl.when(i + 1 < nact)
        def _():
            for cp in gathers(i + 1, 1 - slot):
                cp.start()

        @pl.when((i == 0) | (be_ref[i] != be_ref[jnp.maximum(i - 1, 0)]))
        def _():
            wgu_bf[...] = wgu_ref[...].astype(BF16)
            wd_bf[...] = wd_ref[...].astype(BF16)

        gu = jnp.dot(xbuf[slot], wgu_bf[...], preferred_element_type=F32) + bgu_ref[...]
        gate = jnp.minimum(gu[:, :dff], SWIGLU_LIMIT)
        up = jnp.clip(gu[:, dff:], -SWIGLU_LIMIT, SWIGLU_LIMIT)
        act = (up + 1.0) * gate * _sigmoid(gate * SWIGLU_ALPHA)
        res = jnp.dot(act.astype(BF16), wd_bf[...], preferred_element_type=F32) + bd_ref[...]
        obuf[slot] = (res * wbuf[slot][:, 0:1]).astype(BF16)

        @pl.when(i >= 1)
        def _():
            for cp in scatters(i - 1, 1 - slot):
                cp.wait()

        for cp in scatters(i, slot):
            cp.start()

        @pl.when(i == nact - 1)
        def _():
            for cp in scatters(i, slot):
                cp.wait()


def _moe(blk_e, nact, src, dst, xs, ws, w_gu, b_gu, w_down, b_down, n_blocks):
    ne, d, dff2 = w_gu.shape
    dff = dff2 // 2
    blk = MOE_BLOCK
    grid_spec = pltpu.PrefetchScalarGridSpec(
        num_scalar_prefetch=4,
        grid=(n_blocks,),
        in_specs=[pl.BlockSpec(memory_space=pl.ANY), pl.BlockSpec(memory_space=pl.ANY),
                  pl.BlockSpec((None, d, dff2), lambda i, be, *_: (be[i], 0, 0)),
                  pl.BlockSpec((None, 1, dff2), lambda i, be, *_: (be[i], 0, 0)),
                  pl.BlockSpec((None, dff, d), lambda i, be, *_: (be[i], 0, 0)),
                  pl.BlockSpec((None, 1, d), lambda i, be, *_: (be[i], 0, 0)),
                  pl.BlockSpec(memory_space=pl.ANY)],
        out_specs=pl.BlockSpec(memory_space=pl.ANY),
        scratch_shapes=[pltpu.VMEM((2, blk, d), BF16), pltpu.VMEM((2, blk, LANES), F32),
                        pltpu.VMEM((N_OBUF, blk, d), BF16),
                        pltpu.SemaphoreType.DMA((2,)), pltpu.SemaphoreType.DMA((N_OBUF,)),
                        pltpu.VMEM((d, dff2), BF16), pltpu.VMEM((dff, d), BF16)],
    )
    return pl.pallas_call(
        functools.partial(_moe_kernel, dff=dff),
        grid_spec=grid_spec,
        out_shape=jax.ShapeDtypeStruct(xs.shape, BF16),
        input_output_aliases={10: 0},
        compiler_params=_cparams("arbitrary"),
        name="moe",
    )(blk_e, nact, src, dst, xs, ws, w_gu, b_gu.reshape(ne, 1, dff2), w_down, b_down.reshape(ne, 1, d),
      jnp.zeros(xs.shape, BF16))


def _combine_kernel(pos_ref, ys_ref, x1_ref, gt_ref, o_ref, p_ref, *, n_slots):
    tn = x1_ref.shape[0]
    lane = lax.broadcasted_iota(I32, (tn, SLOT_ROWS), 1)
    pos = [jnp.broadcast_to(pos_ref[:, kk:kk + 1], (tn, SLOT_ROWS)) - lane for kk in range(TOP_K)]

    def cols(c, carry):
        c0 = pl.multiple_of(c * SLOT_ROWS, SLOT_ROWS)
        p = jnp.zeros((tn, SLOT_ROWS), F32)
        for kk in range(TOP_K):
            p = p + jnp.where(pos[kk] == c0, 1.0, 0.0)
        p_ref[:, pl.ds(c0, SLOT_ROWS)] = p.astype(BF16)
        return carry

    lax.fori_loop(0, n_slots // SLOT_ROWS, cols, 0)
    o_ref[...] = x1_ref[...] + gt_ref[...] * jnp.dot(p_ref[...], ys_ref[...], preferred_element_type=F32)


def _combine(pos_t, ys, x1, mod3, tile0, rows, n_slots):
    nb, t, d = x1.shape
    rmod = mod3.shape[1]
    nt = t // rows
    per_tile = ROUTE_TILE // rows
    row = lambda b, i: (b, i, 0)
    gate = pl.BlockSpec((None, rmod, d), (lambda b, i: (b, 0, 5)) if rmod == 1 else (lambda b, i: (b, i, 5)))
    return pl.pallas_call(
        functools.partial(_combine_kernel, n_slots=n_slots),
        grid=(nb, nt),
        in_specs=[pl.BlockSpec((rows, TOP_K), lambda b, i: (tile0 * per_tile + b * nt + i, 0)),
                  pl.BlockSpec((n_slots, d), lambda b, i: (tile0 + (b * nt + i) // per_tile, 0)),
                  pl.BlockSpec((None, rows, d), row), gate],
        out_specs=pl.BlockSpec((None, rows, d), row),
        out_shape=jax.ShapeDtypeStruct((nb, t, d), F32),
        scratch_shapes=[pltpu.VMEM((rows, n_slots), BF16)],
        compiler_params=_cparams("parallel", "parallel"),
        name="combine",
    )(pos_t, ys, x1, mod3)


def _rope_tables(pos, n_heads):
    half = HEAD_DIM // 2
    inv_freq = 1.0 / (ROPE_THETA ** (jnp.arange(0, HEAD_DIM, 2, dtype=F32) / HEAD_DIM))
    ang = pos.astype(F32)[:, None] * inv_freq[None, :]
    cos, sin = jnp.cos(ang), jnp.sin(ang)
    del half
    return (jnp.tile(jnp.concatenate([cos, cos], axis=-1), (1, n_heads)),
            jnp.tile(jnp.concatenate([-sin, sin], axis=-1), (1, n_heads)))


def _block_diag(width, value):
    h = np.arange(width) // HEAD_DIM
    return jnp.asarray(np.where(h[:, None] == h[None, :], value, 0.0), F32)


def kernel(x_prompt, x_sample, cache_k, cache_v, state_wkv, state_shift, c_prompt, c_sample, w_ada, b_ada, norm1_g, norm2_g, w_in, q_norm_g, k_norm_g, rwkv_mu, rwkv_w0, rwkv_w2, rwkv_a0, rwkv_a2, rwkv_g2, rwkv_k_k, rwkv_k_a, rwkv_r_k, rwkv_ln_w, rwkv_ln_b, w_out, router_w, router_b, moe_w_gu, moe_b_gu, moe_w_down, moe_b_down):
    nbp, t, d = x_prompt.shape
    nbs, ts, _ = x_sample.shape
    depth = w_ada.shape[0]
    assert depth == 1 and ts == 1
    n_heads = cache_k.shape[3]
    aw = n_heads * HEAD_DIM
    rwid = rwkv_w0.shape[1]
    rwc = rwkv_mu.shape[1]
    past = cache_k.shape[2]
    ne = router_w.shape[2]
    keep = min(MAX_WINDOW, t)
    lyr = 0

    w_in_bf = w_in[lyr].astype(BF16)
    wt_bf = w_out[lyr][:aw].astype(BF16)
    wb_bf = w_out[lyr][aw:].astype(BF16)
    g1 = norm1_g[lyr].reshape(1, d)
    g2 = norm2_g[lyr].reshape(1, d)
    qg = jnp.tile(q_norm_g[lyr], n_heads).reshape(1, aw)
    kg = jnp.tile(k_norm_g[lyr], n_heads).reshape(1, aw)
    bd_mean_a = _block_diag(aw, 1.0 / HEAD_DIM)
    bd_mean_r = _block_diag(rwid, 1.0 / HEAD_DIM)
    bd_ones_r = _block_diag(rwid, 1.0)
    dl = rwkv_w2.shape[1]
    w2p = jnp.zeros((LANES, rwid), F32).at[:dl].set(rwkv_w2[lyr])
    a2p = jnp.zeros((LANES, rwid), F32).at[dl:dl + rwkv_a2.shape[1]].set(rwkv_a2[lyr])
    vec = lambda a: a[lyr].reshape(1, -1)
    rwt = router_w[lyr].T
    rb = router_b[lyr].reshape(ne, 1)

    rows_c = nbp + nbs
    rows_pad = -(-rows_c // 8) * 8
    c_all = jnp.zeros((rows_pad, d), F32).at[:nbp].set(c_prompt).at[nbp:rows_c].set(c_sample)
    mod = _ada(c_all, w_ada[lyr], b_ada[lyr])
    mod_p = mod[:nbp].reshape(nbp, 1, 6 * d)
    mod_s = mod[nbp:rows_c].reshape(1, nbs, 6 * d)
    xs3 = x_sample.reshape(1, nbs, d)

    cos_p, sin_p = _rope_tables(jnp.arange(t), n_heads)
    cos_s, sin_s = _rope_tables(jnp.full((nbs,), PAST_LEN), n_heads)
    qp, kp, vp, rwp, kt_p, vt_p = _inproj(x_prompt, mod_p, g1, w_in_bf, cos_p, sin_p, qg, kg, bd_mean_a,
                                          ROW_TILE, keep=keep)
    qs, ks, vs, rws = _inproj(xs3, mod_s, g1, w_in[lyr], cos_s, sin_s, qg, kg, bd_mean_a, nbs)

    attn_p = _attn_prompt(qp, kp, vp)
    as_rows = lambda a: a.reshape(nbs, 1, -1)
    cache_t = lambda cch: jnp.transpose(cch[lyr], (0, 2, 3, 1)).reshape(nbs, aw, past)
    attn_s = _attn_sample(as_rows(qs), as_rows(ks), as_rows(vs), cache_t(cache_k), cache_t(cache_v))

    rw_args = (vec(rwkv_mu), vec(rwkv_w0), w2p, vec(rwkv_a0), a2p, rwkv_g2[lyr], vec(rwkv_k_k), vec(rwkv_k_a), bd_ones_r)
    pre_p = _rwprep(rwp, jnp.zeros((nbp, 1, rwc), F32), *rw_args, tm=ROW_TILE, whole_prev=False)
    pre_s = _rwprep(rws, state_shift[lyr].reshape(1, nbs, rwc), *rw_args, tm=nbs, whole_prev=True)
    r_p, ld_p, k_p, v_p, al_p, be_p, g_p = pre_p
    y_p, wkv_p = _rwchunk(r_p, ld_p, k_p, v_p, al_p, be_p)
    r_s, ld_s, k_s, v_s, al_s, be_s, g_s = pre_s
    y_s, wkv_s = _rwstep(*(as_rows(a) for a in (r_s, ld_s, k_s, v_s, al_s, be_s)), state_wkv[lyr])
    y_s = y_s.reshape(1, nbs, rwid)

    op_args = (rwkv_r_k[lyr].reshape(1, rwid), vec(rwkv_ln_w), vec(rwkv_ln_b), bd_mean_r, bd_ones_r, rwt, rb)
    x1_p, h2_p, lg_p = _outproj(x_prompt, attn_p, y_p, r_p, k_p, v_p, g_p, mod_p, g2, wt_bf, wb_bf, *op_args,
                                tm=ROW_TILE)
    x1_s, h2_s, lg_s = _outproj(xs3, attn_s.reshape(1, nbs, aw), y_s, r_s, k_s, v_s, g_s, mod_s, g2,
                                w_out[lyr][:aw], w_out[lyr][aw:], *op_args, tm=nbs)

    n_p = nbp * t
    n_valid = n_p + nbs
    assert n_p % ROUTE_TILE == 0 and nbs <= ROUTE_TILE and ROUTE_TILE % nbs == 0 and d == SUBLANES * LANES
    npad = n_p + ROUTE_TILE
    n_tiles = npad // ROUTE_TILE
    logits_t = jnp.zeros((ne, npad), F32).at[:, :n_p].set(lg_p).at[:, n_p:n_valid].set(lg_s)
    pos, wts, cnt = _route(logits_t, n_valid)
    n_slots = _slot_capacity(ne)
    per_blk = MOE_BLOCK // MOE_CHUNK
    n_blocks = -(-(n_valid * TOP_K // MOE_CHUNK + n_tiles * ne) // per_blk) + ne
    blk_e, nact, src, dst = _chunk_tables(cnt, n_slots, n_blocks)
    n_rows = n_tiles * n_slots + N_OBUF * MOE_BLOCK
    bufs = _dispatch(pos, wts, h2_p.reshape(n_p, d), None, 0, n_slots, n_rows)
    h_last = jnp.zeros((ROUTE_TILE, d), BF16).at[:nbs].set(h2_s[0])
    xs, ws = _dispatch(pos, wts, h_last, bufs, n_p // ROUTE_TILE, n_slots, n_rows)
    ys = _moe(blk_e, nact, src, dst, xs, ws, moe_w_gu[lyr], moe_b_gu[lyr], moe_w_down[lyr], moe_b_down[lyr], n_blocks)
    pos_t = pos.T
    y_prompt = _combine(pos_t, ys, x1_p, mod_p, 0, ROUTE_TILE, n_slots)
    y_sample = _combine(pos_t, ys, x1_s, mod_s, n_p // ROUTE_TILE, nbs, n_slots)

    kept = lambda a: jnp.transpose(a.reshape(nbp, n_heads, HEAD_DIM, keep), (0, 3, 1, 2))[None]
    return (y_prompt, y_sample.reshape(nbs, ts, d), kept(kt_p), kept(vt_p), wkv_p[None], rwp[:, t - 1][None],
            ks.reshape(nbs, ts, n_heads, HEAD_DIM)[None], vs.reshape(nbs, ts, n_heads, HEAD_DIM)[None],
            wkv_s[None], rws.reshape(nbs, rwc)[None])
```

```python
import functools

import numpy as np
import jax
import jax.numpy as jnp
from jax import lax
from jax.experimental import pallas as pl
from jax.experimental.pallas import tpu as pltpu

F32 = jnp.float32
BF16 = jnp.bfloat16
I32 = jnp.int32

HEAD_DIM = 64
LANES = 128
SUBLANES = 8
DILATED_PATTERNS = ((128, 1), (512, 4), (2048, 16))
WINDOW_STEPS = 128
MAX_WINDOW = 2048
PAST_LEN = 16384
ROPE_THETA = 10000.0
NORM_EPS = 1e-6
GN_EPS = 64e-5
TOP_K = 4
SWIGLU_ALPHA = 1.702
SWIGLU_LIMIT = 7.0
RW_CHUNK = 128
MOE_BLOCK = 512
MOE_CHUNK = 16
SLOT_ROWS = 256
ATTN_UNITS = 4
ROW_TILE = 512
ROUTE_TILE = 512
VMEM_LIMIT = 56 * 1024 * 1024
NEG_BIG = -1e30


def _cparams(*sem):
    return pltpu.CompilerParams(dimension_semantics=sem, vmem_limit_bytes=VMEM_LIMIT)


def _dot(a, b):
    return jnp.dot(a.astype(BF16), b.astype(BF16), preferred_element_type=F32)


def _split2(a):
    hi = a.astype(BF16)
    lo = (a - hi.astype(F32)).astype(BF16)
    return hi, lo


def _split3(a):
    hi = a.astype(BF16)
    r1 = a - hi.astype(F32)
    mid = r1.astype(BF16)
    lo = (r1 - mid.astype(F32)).astype(BF16)
    return hi, mid, lo


def _dot_x(a, e):
    e = e.astype(BF16)
    hi, mid, lo = _split3(a)
    return (jnp.dot(hi, e, preferred_element_type=F32) + jnp.dot(mid, e, preferred_element_type=F32)
            + jnp.dot(lo, e, preferred_element_type=F32))


def _xdot(e, a):
    e = e.astype(BF16)
    hi, mid, lo = _split3(a)
    return (jnp.dot(e, hi, preferred_element_type=F32) + jnp.dot(e, mid, preferred_element_type=F32)
            + jnp.dot(e, lo, preferred_element_type=F32))


def _dot3(a, b):
    ah, al = _split2(a)
    bh, bl = _split2(b)
    return (jnp.dot(ah, bh, preferred_element_type=F32) + jnp.dot(ah, bl, preferred_element_type=F32)
            + jnp.dot(al, bh, preferred_element_type=F32))


def _dot3_nt(a, b):
    ah, al = _split2(a)
    bh, bl = _split2(b)
    dn = (((1,), (1,)), ((), ()))
    return (lax.dot_general(ah, bh, dn, preferred_element_type=F32)
            + lax.dot_general(ah, bl, dn, preferred_element_type=F32)
            + lax.dot_general(al, bh, dn, preferred_element_type=F32))


def _mm(a, w):
    if w.dtype == BF16:
        return jnp.dot(a.astype(BF16), w, preferred_element_type=F32)
    return _dot3(a, w)


def _sigmoid(x):
    return 1.0 / (1.0 + jnp.exp(-x))


def _ada_kernel(c_ref, w_ref, b_ref, o_ref):
    c = c_ref[...]
    o_ref[...] = _dot3(c * _sigmoid(c), w_ref[...]) + b_ref[...]


def _ada(c_all, w_ada, b_ada):
    rows, d = c_all.shape
    n = w_ada.shape[1]
    tn = n // 4
    return pl.pallas_call(
        _ada_kernel,
        grid=(n // tn,),
        in_specs=[pl.BlockSpec((rows, d), lambda j: (0, 0)),
                  pl.BlockSpec((d, tn), lambda j: (0, j)),
                  pl.BlockSpec((1, tn), lambda j: (0, j))],
        out_specs=pl.BlockSpec((rows, tn), lambda j: (0, j)),
        out_shape=jax.ShapeDtypeStruct((rows, n), F32),
        compiler_params=_cparams("arbitrary"),
        name="ada",
    )(c_all, w_ada, b_ada.reshape(1, n))


def _inproj_kernel(x_ref, sh_ref, sc_ref, g_ref, w_ref, cos_ref, sin_ref, qg_ref, kg_ref, bd_ref,
                   q_ref, k_ref, v_ref, rw_ref, *maybe_kv_t, aw, first_kept):
    x = x_ref[...]
    ms = jnp.mean(x * x, axis=-1, keepdims=True)
    h = x * lax.rsqrt(ms + NORM_EPS) * g_ref[...] * (1.0 + sc_ref[...]) + sh_ref[...]
    proj = _mm(h, w_ref[...])
    cos = cos_ref[...]
    sin = sin_ref[...]
    lane = lax.broadcasted_iota(I32, (1, aw), 1)
    first_half = (lane % HEAD_DIM) < (HEAD_DIM // 2)
    bd = bd_ref[...]

    def norm_rope(t, g):
        tn = t * lax.rsqrt((_dot if w_ref.dtype == BF16 else _dot_x)(t * t, bd) + NORM_EPS) * g
        rot = jnp.where(first_half, pltpu.roll(tn, aw - HEAD_DIM // 2, 1), pltpu.roll(tn, HEAD_DIM // 2, 1))
        return tn * cos + rot * sin

    scale = 1.0 / np.sqrt(HEAD_DIM).astype(np.float32)
    q_ref[...] = norm_rope(proj[:, :aw], qg_ref[...]) * scale
    k = norm_rope(proj[:, aw:2 * aw], kg_ref[...])
    v = proj[:, 2 * aw:3 * aw]
    k_ref[...] = k
    v_ref[...] = v
    rw_ref[...] = proj[:, 3 * aw:]
    if maybe_kv_t:
        kt_ref, vt_ref = maybe_kv_t
        kept = pl.program_id(1) >= first_kept

        @pl.when(kept)
        def _():
            kt_ref[...] = k.T
            vt_ref[...] = v.T

        @pl.when(jnp.logical_not(kept))
        def _():
            kt_ref[...] = jnp.zeros_like(kt_ref)
            vt_ref[...] = jnp.zeros_like(vt_ref)


def _inproj(x3, mod3, g1, w_in_bf, cos_t, sin_t, qg, kg, bd_mean, tm, keep=0):
    nb, t, d = x3.shape
    r = mod3.shape[1]
    ncol = w_in_bf.shape[1]
    aw = cos_t.shape[1]
    rwc = ncol - 3 * aw
    grid = (nb, t // tm)
    row = lambda b, i: (b, i, 0)
    const = lambda b, i: (0, 0)
    mod_spec = lambda s: pl.BlockSpec((None, r, d), (lambda b, i: (b, 0, s)) if r == 1 else (lambda b, i: (b, i, s)))
    out_specs = [pl.BlockSpec((None, tm, aw), row)] * 3 + [pl.BlockSpec((None, tm, rwc), row)]
    out_shape = [jax.ShapeDtypeStruct((nb, t, aw), F32)] * 3 + [jax.ShapeDtypeStruct((nb, t, rwc), F32)]
    first_kept = (t - keep) // tm
    if keep:
        assert keep % tm == 0 and (t - keep) % tm == 0
        kept_spec = pl.BlockSpec((None, aw, tm), lambda b, i: (b, 0, jnp.maximum(i - first_kept, 0)))
        out_specs += [kept_spec, kept_spec]
        out_shape += [jax.ShapeDtypeStruct((nb, aw, keep), F32)] * 2
    outs = pl.pallas_call(
        functools.partial(_inproj_kernel, aw=aw, first_kept=first_kept),
        grid=grid,
        in_specs=[pl.BlockSpec((None, tm, d), row), mod_spec(0), mod_spec(1),
                  pl.BlockSpec((1, d), const), pl.BlockSpec((d, ncol), const),
                  pl.BlockSpec((tm, aw), lambda b, i: (i, 0)), pl.BlockSpec((tm, aw), lambda b, i: (i, 0)),
                  pl.BlockSpec((1, aw), const), pl.BlockSpec((1, aw), const), pl.BlockSpec((aw, aw), const)],
        out_specs=out_specs,
        out_shape=out_shape,
        compiler_params=_cparams("parallel", "arbitrary"),
        name="inproj",
    )(x3, mod3, mod3, g1, w_in_bf, cos_t, sin_t, qg, kg, bd_mean)
    return outs


def _attn_prompt_kernel(q_ref, k_ref, v_ref, o_ref, kp_ref, vp_ref, m_ref, l_ref, acc_ref, *, t, pad):
    nq = WINDOW_STEPS
    nk = 2 * WINDOW_STEPS
    kp_ref[pl.ds(0, pad), :] = jnp.zeros((pad, LANES), F32)
    vp_ref[pl.ds(0, pad), :] = jnp.zeros((pad, LANES), F32)
    kp_ref[pl.ds(pad, t), :] = k_ref[...]
    vp_ref[pl.ds(pad, t), :] = v_ref[...]
    head0 = lax.broadcasted_iota(I32, (1, LANES), 1) < HEAD_DIM
    qi = lax.broadcasted_iota(I32, (nq, nk), 0)
    kj = lax.broadcasted_iota(I32, (nq, nk), 1)
    steps_back = qi + nq - kj
    band = (steps_back >= 0) & (steps_back <= WINDOW_STEPS)
    has_past = kj >= nq

    hsels = (head0, jnp.logical_not(head0))
    nt = (((1,), (1,)), ((), ()))

    for p, (_, d) in enumerate(DILATED_PATTERNS):
        def units(g, carry, p=p, d=d):
            rows_q, kb, vb, valid, qh = [], [], [], [], []
            for j in range(ATTN_UNITS):
                u = g * ATTN_UNITS + j
                res = u % d
                blk = u // d
                q_start = res + d * nq * blk
                k_start = pad + q_start - d * nq
                if d == 1:
                    rows_q.append(pl.ds(q_start, nq))
                    rows_k = pl.ds(k_start, nk)
                else:
                    rows_q.append(pl.ds(q_start, nq, stride=d))
                    rows_k = pl.ds(k_start, nk, stride=d)
                q = q_ref[rows_q[j], :]
                kb.append(kp_ref[rows_k, :].astype(BF16))
                vb.append(vp_ref[rows_k, :].astype(BF16))
                valid.append(band & (has_past | (blk > 0)))
                qh.append([jnp.where(hsel, q, 0.0).astype(BF16) for hsel in hsels])
            chains = [(j, h) for j in range(ATTN_UNITS) for h in range(2)]
            s = [lax.dot_general(qh[j][h], kb[j], nt, preferred_element_type=F32) for j, h in chains]
            s = [jnp.where(valid[j], sc, NEG_BIG) for (j, h), sc in zip(chains, s)]
            mx = [jnp.max(sc, axis=-1, keepdims=True) for sc in s]
            e = [jnp.exp(sc - m) for sc, m in zip(s, mx)]
            den = [jnp.sum(ec, axis=-1, keepdims=True) for ec in e]
            o = [jnp.dot(ec.astype(BF16), vb[j], preferred_element_type=F32) for (j, h), ec in zip(chains, e)]
            for j in range(ATTN_UNITS):
                m_ref[p, rows_q[j], :] = jnp.where(head0, mx[2 * j], mx[2 * j + 1])
                l_ref[p, rows_q[j], :] = jnp.where(head0, den[2 * j], den[2 * j + 1])
                acc_ref[p, rows_q[j], :] = jnp.where(head0, o[2 * j], o[2 * j + 1])
            return carry

        lax.fori_loop(0, t // nq // ATTN_UNITS, units, 0)

    rows = 256

    def merge(i, carry):
        sl = pl.ds(pl.multiple_of(i * rows, rows), rows)
        m0, m1, m2 = m_ref[0, sl, :], m_ref[1, sl, :], m_ref[2, sl, :]
        mm = jnp.maximum(jnp.maximum(m0, m1), m2)
        w0, w1, w2 = jnp.exp(m0 - mm), jnp.exp(m1 - mm), jnp.exp(m2 - mm)
        num = w0 * acc_ref[0, sl, :] + w1 * acc_ref[1, sl, :] + w2 * acc_ref[2, sl, :]
        den = w0 * l_ref[0, sl, :] + w1 * l_ref[1, sl, :] + w2 * l_ref[2, sl, :]
        o_ref[sl, :] = num / den
        return carry

    lax.fori_loop(0, t // rows, merge, 0)


def _attn_prompt(q, k, v):
    nb, t, aw = q.shape
    pad = MAX_WINDOW
    assert t % MAX_WINDOW == 0
    spec = pl.BlockSpec((None, t, LANES), lambda b, hp: (b, 0, hp))
    return pl.pallas_call(
        functools.partial(_attn_prompt_kernel, t=t, pad=pad),
        grid=(nb, aw // LANES),
        in_specs=[spec, spec, spec],
        out_specs=spec,
        out_shape=jax.ShapeDtypeStruct((nb, t, aw), F32),
        scratch_shapes=[pltpu.VMEM((pad + t, LANES), F32), pltpu.VMEM((pad + t, LANES), F32),
                        pltpu.VMEM((3, t, LANES), F32), pltpu.VMEM((3, t, LANES), F32),
                        pltpu.VMEM((3, t, LANES), F32)],
        compiler_params=_cparams("parallel", "parallel"),
        name="attn_prompt",
    )(q, k, v)


def _attn_sample_kernel(q_ref, kn_ref, vn_ref, kt_ref, vt_ref, o_ref, *, nh, w):
    aw = nh * HEAD_DIM
    dist = w - lax.broadcasted_iota(I32, (1, w), 1)
    mult = jnp.zeros((1, w), F32)
    for win, d in DILATED_PATTERNS:
        mult = mult + jnp.where((dist % d == 0) & (dist <= win), 1.0, 0.0)
    n_pat = float(len(DILATED_PATTERNS))
    q_col = _col(q_ref[...], aw)
    kn_col = _col(kn_ref[...], aw)
    vn_col = _col(vn_ref[...], aw)
    heads = range(nh)
    hs = [pl.ds(h * HEAD_DIM, HEAD_DIM) for h in heads]
    cut = lambda col, h: col[h * HEAD_DIM:(h + 1) * HEAD_DIM]
    s = [jnp.sum(kt_ref[hs[h], :] * cut(q_col, h), axis=0, keepdims=True) for h in heads]
    s_self = [jnp.sum(cut(q_col, h) * cut(kn_col, h), axis=0, keepdims=True) for h in heads]
    s = [jnp.where(mult > 0.0, sh, NEG_BIG) for sh in s]
    mx = [jnp.maximum(jnp.max(sh, axis=1, keepdims=True), ss) for sh, ss in zip(s, s_self)]
    pr = [mult * jnp.exp(sh - m) for sh, m in zip(s, mx)]
    p_self = [n_pat * jnp.exp(ss - m) for ss, m in zip(s_self, mx)]
    den = [jnp.sum(p, axis=1, keepdims=True) + ps for p, ps in zip(pr, p_self)]
    num = [jnp.sum(vt_ref[hs[h], :] * pr[h], axis=1, keepdims=True) + p_self[h] * cut(vn_col, h) for h in heads]
    outs = [n / dn for n, dn in zip(num, den)]
    o_ref[...] = _row(jnp.concatenate(outs, axis=0), aw)


def _attn_sample(q, kn, vn, cache_kt, cache_vt):
    nb, _, aw = q.shape
    w = cache_kt.shape[2]
    assert w == MAX_WINDOW
    row = pl.BlockSpec((None, 1, aw), lambda i: (i, 0, 0))
    mat = pl.BlockSpec((None, aw, w), lambda i: (i, 0, 0))
    return pl.pallas_call(
        functools.partial(_attn_sample_kernel, nh=aw // HEAD_DIM, w=w),
        grid=(nb,),
        in_specs=[row, row, row, mat, mat],
        out_specs=row,
        out_shape=jax.ShapeDtypeStruct((nb, 1, aw), F32),
        compiler_params=_cparams("parallel"),
        name="attn_sample",
    )(q, kn, vn, cache_kt, cache_vt)


def _rwprep_kernel(p_ref, prev_ref, first_ref, mu_ref, w0_ref, w2_ref, a0_ref, a2_ref, g2_ref, kk_ref, ka_ref,
                   bd_ref, r_ref, ld_ref, k_ref, v_ref, al_ref, be_ref, g_ref, *, rwid, whole_prev):
    p = p_ref[...]
    if whole_prev:
        prev = first_ref[...]
    else:
        tm = p.shape[0]
        before = jnp.where(pl.program_id(1) == 0, first_ref[...], prev_ref[7:8, :])
        rowi = lax.broadcasted_iota(I32, (tm, 1), 0)
        prev = jnp.where(rowi == 0, before, pltpu.roll(p, 1, 0))
    xs = p + mu_ref[...] * (prev - p)
    r = xs[:, :rwid]
    k = xs[:, rwid:2 * rwid]
    v = xs[:, 2 * rwid:3 * rwid]
    xwa = xs[:, 3 * rwid:3 * rwid + LANES]
    xg = xs[:, 3 * rwid + LANES:]
    z = w0_ref[...] + _dot3(jnp.tanh(xwa), w2_ref[...])
    softplus_neg = jnp.maximum(-z, 0.0) + jnp.log(1.0 + jnp.exp(-jnp.abs(z)))
    w = -softplus_neg - 0.5
    a = _sigmoid(a0_ref[...] + _dot3(xwa, a2_ref[...]))
    g = _dot3(_sigmoid(xg), g2_ref[...])
    kk = k * kk_ref[...]
    norm = jnp.sqrt(_dot_x(kk * kk, bd_ref[...]))
    kk = kk / jnp.maximum(norm, 1e-12)
    r_ref[...] = r
    ld_ref[...] = -jnp.exp(w)
    k_ref[...] = k * (1.0 + (a - 1.0) * ka_ref[...])
    v_ref[...] = v
    al_ref[...] = -kk
    be_ref[...] = kk * a
    g_ref[...] = g


def _rwprep(rw, first, mu, w0, w2p, a0, a2p, g2, k_k, k_a, bd_ones, tm, whole_prev):
    nb, t, rwc = rw.shape
    rwid = w0.shape[1]
    grid = (nb, t // tm)
    row = lambda b, i: (b, i, 0)
    const = lambda b, i: (0, 0)
    if whole_prev:
        prev_spec = pl.BlockSpec((None, tm, rwc), row)
        first_spec = pl.BlockSpec((None, tm, rwc), row)
    else:
        prev_spec = pl.BlockSpec((None, 8, rwc), lambda b, i: (b, jnp.maximum(i * (tm // 8) - 1, 0), 0))
        first_spec = pl.BlockSpec((None, 1, rwc), lambda b, i: (b, 0, 0))
    vec = pl.BlockSpec((1, rwid), const)
    out = pl.BlockSpec((None, tm, rwid), row)
    return pl.pallas_call(
        functools.partial(_rwprep_kernel, rwid=rwid, whole_prev=whole_prev),
        grid=grid,
        in_specs=[pl.BlockSpec((None, tm, rwc), row), prev_spec, first_spec,
                  pl.BlockSpec((1, rwc), const), vec, pl.BlockSpec((LANES, rwid), const),
                  vec, pl.BlockSpec((LANES, rwid), const), pl.BlockSpec((LANES, rwid), const), vec, vec,
                  pl.BlockSpec((rwid, rwid), const)],
        out_specs=[out] * 7,
        out_shape=[jax.ShapeDtypeStruct((nb, t, rwid), F32)] * 7,
        compiler_params=_cparams("parallel", "parallel"),
        name="rwprep",
    )(rw, rw, first, mu, w0, w2p, a0, a2p, g2, k_k, k_a, bd_ones)


def _rwchunk_kernel(r_ref, ld_ref, k_ref, v_ref, al_ref, be_ref, y_ref, st_ref, z_ref, *, nb):
    c = RW_CHUNK
    ci = pl.program_id(1)

    @pl.when(ci == 0)
    def _():
        z_ref[...] = jnp.zeros_like(z_ref)

    ti = lax.broadcasted_iota(I32, (c, c), 0)
    si = lax.broadcasted_iota(I32, (c, c), 1)
    low_incl = si <= ti
    low_strict = si < ti
    diag = si == ti
    tri = jnp.where(low_incl, 1.0, 0.0).astype(BF16)
    eye = jnp.where(diag, 1.0, 0.0)
    head0 = lax.broadcasted_iota(I32, (1, LANES), 1) < HEAD_DIM
    hsels = (head0, jnp.logical_not(head0))
    same_head = (ti < HEAD_DIM) == (si < HEAD_DIM)
    nt = (((1,), (1,)), ((), ()))
    batches = range(nb)
    chains = [(b, h) for b in batches for h in range(2)]

    cum = [_xdot(tri, ld_ref[b]) for b in batches]
    tot = [cm[c - 1:c, :] for cm in cum]
    e_neg = [jnp.exp(-cm) for cm in cum]
    at = [al_ref[b] * jnp.exp(cum[b] - ld_ref[b]) for b in batches]
    rt = [r_ref[b] * jnp.exp(cum[b]) for b in batches]
    rhs_t = [jnp.concatenate([be_ref[b] * e_neg[b], k_ref[b] * e_neg[b]], axis=0).astype(BF16) for b in batches]
    vb = [v_ref[b].astype(BF16) for b in batches]
    at_h = [jnp.where(hsels[h], at[b], 0.0) for b, h in chains]
    rt_h = [jnp.where(hsels[h], rt[b], 0.0) for b, h in chains]
    a4 = [lax.dot_general(jnp.concatenate([a, r], axis=0).astype(BF16), rhs_t[b], nt, preferred_element_type=F32)
          for (b, h), a, r in zip(chains, at_h, rt_h)]
    a_ab = [jnp.where(low_strict, m[:c, :c], 0.0) for m in a4]
    a_ak = [jnp.where(low_strict, m[:c, c:], 0.0).astype(BF16) for m in a4]
    a_r = [jnp.concatenate([jnp.where(low_incl, m[c:, :c], 0.0), jnp.where(low_incl, m[c:, c:], 0.0)],
                           axis=1).astype(BF16) for m in a4]
    pw = [m.astype(BF16) for m in a_ab]
    inv = [eye + m for m in a_ab]
    for _ in range(int(np.log2(c)) - 1):
        pw = [jnp.dot(m, m, preferred_element_type=F32).astype(BF16) for m in pw]
        inv = [i + jnp.dot(i.astype(BF16), m, preferred_element_type=F32) for i, m in zip(inv, pw)]
    akv = [jnp.dot(m, vb[b], preferred_element_type=F32) for (b, h), m in zip(chains, a_ak)]
    x = [_dot(i, jnp.concatenate([kv, a], axis=1)) for i, kv, a in zip(inv, akv, at_h)]
    u0_h = [m[:, :LANES] for m in x]
    at2_h = [m[:, LANES:] for m in x]
    y0_h = [jnp.dot(ar, jnp.concatenate([u0, v_ref[b]], axis=0).astype(BF16), preferred_element_type=F32)
            for (b, h), ar, u0 in zip(chains, a_r, u0_h)]
    rt2_h = [r + jnp.dot(ar[:, :c], a2.astype(BF16), preferred_element_type=F32)
             for r, ar, a2 in zip(rt_h, a_r, at2_h)]
    z = [z_ref[b] for b in batches]
    uy = [_dot(jnp.concatenate([at2_h[2 * b] + at2_h[2 * b + 1], rt2_h[2 * b] + rt2_h[2 * b + 1]], axis=0), z[b])
          for b in batches]
    u = [uy[b][:c] + jnp.where(head0, u0_h[2 * b], u0_h[2 * b + 1]) for b in batches]
    for b in batches:
        y_ref[b] = uy[b][c:] + jnp.where(head0, y0_h[2 * b], y0_h[2 * b + 1])
    e_end = [jnp.exp(tot[b] - cum[b]) for b in batches]
    lhs_t = [jnp.concatenate([be_ref[b] * e_end[b], k_ref[b] * e_end[b]], axis=0) for b in batches]
    zadd = [_dot(lhs_t[b].T, jnp.concatenate([u[b], v_ref[b]], axis=0)) for b in batches]
    for b in batches:
        dcol = jnp.sum(jnp.where(diag, jnp.broadcast_to(jnp.exp(tot[b]), (c, c)), 0.0), axis=1, keepdims=True)
        z_ref[b] = dcol * z[b] + jnp.where(same_head, zadd[b], 0.0)

    @pl.when(ci == pl.num_programs(1) - 1)
    def _():
        for b in batches:
            s = z_ref[b].T
            st_ref[b, 0] = s[:HEAD_DIM, :HEAD_DIM]
            st_ref[b, 1] = s[HEAD_DIM:, HEAD_DIM:]


def _rwchunk(r, ld, k, v, al, be):
    nb, t, rwid = r.shape
    c = RW_CHUNK
    assert t % c == 0 and c == LANES
    npair = rwid // LANES
    seq = pl.BlockSpec((nb, c, LANES), lambda hp, ci: (0, ci, hp))
    return pl.pallas_call(
        functools.partial(_rwchunk_kernel, nb=nb),
        grid=(npair, t // c),
        in_specs=[seq] * 6,
        out_specs=[seq, pl.BlockSpec((nb, 2, HEAD_DIM, HEAD_DIM), lambda hp, ci: (0, hp, 0, 0))],
        out_shape=[jax.ShapeDtypeStruct((nb, t, rwid), F32),
                   jax.ShapeDtypeStruct((nb, 2 * npair, HEAD_DIM, HEAD_DIM), F32)],
        scratch_shapes=[pltpu.VMEM((nb, LANES, LANES), F32)],
        compiler_params=_cparams("parallel", "arbitrary"),
        name="rwchunk",
    )(r, ld, k, v, al, be)


def _col(row, width):
    return jnp.broadcast_to(row, (LANES, width)).T[:, 0:1]


def _row(col, width):
    return jnp.broadcast_to(col, (width, LANES)).T[0:1, :]


def _rwstep_kernel(r_ref, ld_ref, k_ref, v_ref, al_ref, be_ref, s_ref, y_ref, so_ref, *, bb, nh):
    rwid = nh * HEAD_DIM

    def one(b, carry):
        r, dcy, k, al, be = (ref[b] for ref in (r_ref, ld_ref, k_ref, al_ref, be_ref))
        dcy = jnp.exp(dcy)
        v_col = _col(v_ref[b], rwid)
        heads = range(nh)
        hs = [slice(h * HEAD_DIM, (h + 1) * HEAD_DIM) for h in heads]
        st = [s_ref[b, h] for h in heads]
        sa = [jnp.sum(st[h] * al[:, hs[h]], axis=1, keepdims=True) for h in heads]
        st = [st[h] * dcy[:, hs[h]] + sa[h] * be[:, hs[h]] + v_col[hs[h], :] * k[:, hs[h]] for h in heads]
        for h in heads:
            so_ref[b, h] = st[h]
        ys = [jnp.sum(st[h] * r[:, hs[h]], axis=1, keepdims=True) for h in heads]
        y_ref[b] = _row(jnp.concatenate(ys, axis=0), rwid)
        return carry

    lax.fori_loop(0, bb, one, 0)


def _rwstep(r, ld, k, v, al, be, state):
    nb, _, rwid = r.shape
    nh = state.shape[1]
    bb = 8
    row = pl.BlockSpec((bb, 1, rwid), lambda i: (i, 0, 0))
    st = pl.BlockSpec((bb, nh, HEAD_DIM, HEAD_DIM), lambda i: (i, 0, 0, 0))
    return pl.pallas_call(
        functools.partial(_rwstep_kernel, bb=bb, nh=nh),
        grid=(nb // bb,),
        in_specs=[row] * 6 + [st],
        out_specs=[row, st],
        out_shape=[jax.ShapeDtypeStruct((nb, 1, rwid), F32), jax.ShapeDtypeStruct(state.shape, F32)],
        compiler_params=_cparams("parallel"),
        name="rwstep",
    )(r, ld, k, v, al, be, state)


def _outproj_kernel(x_ref, at_ref, y_ref, r_ref, k_ref, v_ref, g_ref, gt_ref, sh_ref, sc_ref, g2_ref,
                    wt_ref, wb_ref, rk_ref, lw_ref, lb_ref, bdm_ref, bd1_ref, rwt_ref, rb_ref, *rest, n_slots):
    x1_ref, pos_ref, wt4_ref, cnt_ref, xs_ref, ws_ref, hb_ref = rest[-7:]
    y = y_ref[...]
    bdm = bdm_ref[...]
    mean = _dot_x(y, bdm)
    yc = y - mean
    var = _dot_x(yc * yc, bdm)
    yn = yc * lax.rsqrt(var + GN_EPS) * lw_ref[...] + lb_ref[...]
    v = v_ref[...]
    bonus = _dot_x(r_ref[...] * k_ref[...] * rk_ref[...], bd1_ref[...]) * v
    rw = (yn + bonus) * g_ref[...]
    mix = _mm(at_ref[...], wt_ref[...]) + _mm(rw, wb_ref[...])
    x1 = x_ref[...] + gt_ref[...] * mix
    x1_ref[...] = x1
    ms = jnp.mean(x1 * x1, axis=-1, keepdims=True)
    h2 = x1 * lax.rsqrt(ms + NORM_EPS) * g2_ref[...] * (1.0 + sc_ref[...]) + sh_ref[...]
    pos, wts, cnt = _route_tile(_dot3_nt(rwt_ref[...], h2) + rb_ref[...])
    for kk in range(TOP_K):
        pos_ref[pl.ds(kk, 1), :] = pos[kk]
        wt4_ref[pl.ds(kk, 1), :] = wts[kk]
    cnt_ref[...] = jnp.broadcast_to(cnt, cnt_ref.shape)
    hb_ref[...] = h2.astype(BF16)
    _sort_tile(pos, wts, hb_ref, xs_ref, ws_ref, n_slots)


def _outproj(x3, attn, y, r, k, v, g, mod3, g2, wt_bf, wb_bf, r_k, ln_w, ln_b, bd_mean, bd_ones, rwt, rb, tm,
             n_slots, n_rows, tile0=0, bufs=None):
    nb, t, d = x3.shape
    rmod = mod3.shape[1]
    aw = attn.shape[2]
    ne = rwt.shape[0]
    nt = t // tm
    grid = (nb, nt)
    row = lambda b, i: (b, i, 0)
    const = lambda b, i: (0, 0)
    tile = lambda b, i: (tile0 + b * nt + i, 0)
    mod_spec = lambda s: pl.BlockSpec((None, rmod, d), (lambda b, i: (b, 0, s)) if rmod == 1 else (lambda b, i: (b, i, s)))
    half = pl.BlockSpec((None, tm, aw), row)
    vec = pl.BlockSpec((1, aw), const)
    tok4 = pl.BlockSpec((TOP_K, tm), lambda b, i: (0, b * nt + i))
    bufs = () if bufs is None else tuple(bufs)
    n_in = 20
    return pl.pallas_call(
        functools.partial(_outproj_kernel, n_slots=n_slots),
        grid=grid,
        in_specs=[pl.BlockSpec((None, tm, d), row)] + [half] * 6 + [mod_spec(2), mod_spec(3), mod_spec(4),
                  pl.BlockSpec((1, d), const), pl.BlockSpec((aw, d), const), pl.BlockSpec((aw, d), const),
                  vec, vec, vec, pl.BlockSpec((aw, aw), const), pl.BlockSpec((aw, aw), const),
                  pl.BlockSpec((ne, d), const), pl.BlockSpec((ne, 1), const)]
                 + [pl.BlockSpec(memory_space=pl.ANY)] * len(bufs),
        out_specs=[pl.BlockSpec((None, tm, d), row), tok4, tok4,
                   pl.BlockSpec((None, ne, LANES), lambda b, i: (b * nt + i, 0, 0)),
                   pl.BlockSpec((n_slots, d), tile), pl.BlockSpec((n_slots, LANES), tile)],
        out_shape=[jax.ShapeDtypeStruct((nb, t, d), F32), jax.ShapeDtypeStruct((TOP_K, nb * t), I32),
                   jax.ShapeDtypeStruct((TOP_K, nb * t), F32), jax.ShapeDtypeStruct((nb * nt, ne, LANES), F32),
                   jax.ShapeDtypeStruct((n_rows, d), BF16), jax.ShapeDtypeStruct((n_rows, LANES), F32)],
        scratch_shapes=[pltpu.VMEM((tm, d), BF16)],
        input_output_aliases={n_in: 4, n_in + 1: 5} if bufs else {},
        compiler_params=_cparams("parallel", "parallel"),
        name="outproj",
    )(x3, attn, y, r, k, v, g, mod3, mod3, mod3, g2, wt_bf, wb_bf, r_k, ln_w, ln_b, bd_mean, bd_ones, rwt, rb, *bufs)


def _route_tile(lg):
    ne, tn = lg.shape
    eidx = lax.broadcasted_iota(I32, (ne, tn), 0).astype(F32)
    vals, hots = [], []
    for _ in range(TOP_K):
        mx = jnp.max(lg, axis=0, keepdims=True)
        pick = jnp.min(jnp.where(lg == mx, eidx, float(ne)), axis=0, keepdims=True)
        hot = eidx == pick
        vals.append(mx)
        hots.append(hot)
        lg = jnp.where(hot, -jnp.inf, lg)
    ex = [jnp.exp(vv - vals[0]) for vv in vals]
    den = ex[0] + ex[1] + ex[2] + ex[3]
    wts = [e / den for e in ex]
    hot_all = jnp.zeros((ne, tn), F32)
    for hot in hots:
        hot_all = hot_all + jnp.where(hot, 1.0, 0.0)
    ri = lax.broadcasted_iota(I32, (tn, tn), 0)
    cj = lax.broadcasted_iota(I32, (tn, tn), 1)
    upper = jnp.where(ri <= cj, 1.0, 0.0).astype(BF16)
    before = jnp.dot(hot_all.astype(BF16), upper, preferred_element_type=F32) - hot_all
    cnt = jnp.sum(hot_all, axis=1, keepdims=True)
    padded = jnp.ceil(cnt / MOE_CHUNK) * MOE_CHUNK
    er = lax.broadcasted_iota(I32, (ne, ne), 0)
    ec = lax.broadcasted_iota(I32, (ne, ne), 1)
    lower_strict = jnp.where(ec < er, 1.0, 0.0)
    off = _xdot(lower_strict, jnp.broadcast_to(padded, (ne, LANES)))[:, 0:1]
    pos = [jnp.sum(jnp.where(hot, off + before, 0.0), axis=0, keepdims=True).astype(I32) for hot in hots]
    return pos, wts, cnt


def _slot_capacity(ne):
    return -(-(TOP_K * ROUTE_TILE + ne * (MOE_CHUNK - 1)) // SLOT_ROWS) * SLOT_ROWS


def _sort_tile(pos, wts, h_ref, xs_ref, ws_ref, n_slots):
    tn = h_ref.shape[0]

    def rows(c, carry):
        h = h_ref[...]
        r0 = pl.multiple_of(c * SLOT_ROWS, SLOT_ROWS)
        slot = r0 + lax.broadcasted_iota(I32, (SLOT_ROWS, tn), 0)
        pw = jnp.zeros((SLOT_ROWS, tn), F32)
        for kk in range(TOP_K):
            pw = pw + jnp.where(slot == pos[kk], wts[kk], 0.0)
        p = jnp.where(pw > 0.0, 1.0, 0.0)
        xs_ref[pl.ds(r0, SLOT_ROWS), :] = jnp.dot(p.astype(BF16), h, preferred_element_type=F32).astype(BF16)
        ws_ref[pl.ds(r0, SLOT_ROWS), :] = jnp.broadcast_to(jnp.sum(pw, axis=1, keepdims=True), (SLOT_ROWS, LANES))
        return carry

    lax.fori_loop(0, n_slots // SLOT_ROWS, rows, 0)


def _chunk_tables(cnt, n_slots, n_blocks):
    nt, ne = cnt.shape
    per_blk = MOE_BLOCK // MOE_CHUNK
    nch = -(-cnt // MOE_CHUNK)
    seg0 = (jnp.cumsum(nch, axis=1) - nch) + (jnp.arange(nt) * (n_slots // MOE_CHUNK))[:, None]
    cum_t = jnp.cumsum(nch, axis=0)
    total = cum_t[-1]
    blocks = -(-total // per_blk)
    blk_end = jnp.cumsum(blocks)
    blk = jnp.arange(n_blocks)
    blk_e = jnp.minimum(jnp.sum(blk_end[None, :] <= blk[:, None], axis=1), ne - 1).astype(I32)
    nact = blk_end[-1].reshape(1).astype(I32)
    pick_e = blk_e[:, None] == jnp.arange(ne)[None, :]
    of_e = lambda tab: jnp.sum(jnp.where(pick_e, tab[None, :], 0), axis=1)
    col_e = lambda tab: jnp.sum(jnp.where(pick_e[:, None, :], tab[None, :, :], 0), axis=2)
    q = ((blk - of_e(blk_end - blocks)) * per_blk)[:, None] + jnp.arange(per_blk)[None, :]
    live = (q < of_e(total)[:, None]) & (blk < nact[0])[:, None]
    t_n = jnp.minimum(jnp.sum(col_e(cum_t)[:, None, :] <= q[:, :, None], axis=2), nt - 1)
    pick_t = t_n[:, :, None] == jnp.arange(nt)[None, None, :]
    of_t = lambda tab: jnp.sum(jnp.where(pick_t, tab[:, None, :], 0), axis=2)
    src = jnp.where(live, of_t(col_e(seg0)) + q - of_t(col_e(cum_t - nch)), 0)
    n = blk[:, None] * per_blk + jnp.arange(per_blk)[None, :]
    spare = nt * (n_slots // MOE_CHUNK) + n % (N_OBUF * per_blk)
    dst = jnp.where(live, src, spare)
    return blk_e, nact, src.reshape(-1).astype(I32), dst.reshape(-1).astype(I32)


N_OBUF = 2


def _moe_kernel(be_ref, nact_ref, src_ref, dst_ref, xs_ref, ws_ref, wgu_ref, bgu_ref, wd_ref, bd_ref, out_in_ref,
                out_ref, xbuf, wbuf, obuf, gsem, ssem, wgu_bf, wd_bf, *, dff):
    del out_in_ref
    i = pl.program_id(0)
    nact = nact_ref[0]
    per_blk = MOE_BLOCK // MOE_CHUNK
    rows = lambda c: pl.ds(pl.multiple_of(c * MOE_CHUNK, MOE_CHUNK), MOE_CHUNK)

    def gathers(blk, slot):
        copies = []
        for m in range(per_blk):
            c = src_ref[blk * per_blk + m]
            copies.append(pltpu.make_async_copy(xs_ref.at[rows(c), :], xbuf.at[slot, rows(m), :], gsem.at[slot]))
            copies.append(pltpu.make_async_copy(ws_ref.at[rows(c), :], wbuf.at[slot, rows(m), :], gsem.at[slot]))
        return copies

    def scatters(blk, slot):
        return [pltpu.make_async_copy(obuf.at[slot, rows(m), :], out_ref.at[rows(dst_ref[blk * per_blk + m]), :],
                                      ssem.at[slot]) for m in range(per_blk)]

    @pl.when(i == 0)
    def _():
        for cp in gathers(0, 0):
            cp.start()

    @pl.when(i < nact)
    def _():
        slot = i % 2
        for cp in gathers(i, slot):
            cp.wait()

        @pl.when(i + 1 < nact)
        def _():
            for cp in gathers(i + 1, 1 - slot):
                cp.start()

        @pl.when((i == 0) | (be_ref[i] != be_ref[jnp.maximum(i - 1, 0)]))
        def _():
            wgu_bf[...] = wgu_ref[...].astype(BF16)
            wd_bf[...] = wd_ref[...].astype(BF16)

        gu = jnp.dot(xbuf[slot], wgu_bf[...], preferred_element_type=F32) + bgu_ref[...]
        gate = jnp.minimum(gu[:, :dff], SWIGLU_LIMIT)
        up = jnp.clip(gu[:, dff:], -SWIGLU_LIMIT, SWIGLU_LIMIT)
        act = (up + 1.0) * gate * _sigmoid(gate * SWIGLU_ALPHA)
        res = jnp.dot(act.astype(BF16), wd_bf[...], preferred_element_type=F32) + bd_ref[...]
        obuf[slot] = (res * wbuf[slot][:, 0:1]).astype(BF16)

        @pl.when(i >= 1)
        def _():
            for cp in scatters(i - 1, 1 - slot):
                cp.wait()

        for cp in scatters(i, slot):
            cp.start()

        @pl.when(i == nact - 1)
        def _():
            for cp in scatters(i, slot):
                cp.wait()


def _moe(blk_e, nact, src, dst, xs, ws, w_gu, b_gu, w_down, b_down, n_blocks):
    ne, d, dff2 = w_gu.shape
    dff = dff2 // 2
    blk = MOE_BLOCK
    grid_spec = pltpu.PrefetchScalarGridSpec(
        num_scalar_prefetch=4,
        grid=(n_blocks,),
        in_specs=[pl.BlockSpec(memory_space=pl.ANY), pl.BlockSpec(memory_space=pl.ANY),
                  pl.BlockSpec((None, d, dff2), lambda i, be, *_: (be[i], 0, 0)),
                  pl.BlockSpec((None, 1, dff2), lambda i, be, *_: (be[i], 0, 0)),
                  pl.BlockSpec((None, dff, d), lambda i, be, *_: (be[i], 0, 0)),
                  pl.BlockSpec((None, 1, d), lambda i, be, *_: (be[i], 0, 0)),
                  pl.BlockSpec(memory_space=pl.ANY)],
        out_specs=pl.BlockSpec(memory_space=pl.ANY),
        scratch_shapes=[pltpu.VMEM((2, blk, d), BF16), pltpu.VMEM((2, blk, LANES), F32),
                        pltpu.VMEM((N_OBUF, blk, d), BF16),
                        pltpu.SemaphoreType.DMA((2,)), pltpu.SemaphoreType.DMA((N_OBUF,)),
                        pltpu.VMEM((d, dff2), BF16), pltpu.VMEM((dff, d), BF16)],
    )
    return pl.pallas_call(
        functools.partial(_moe_kernel, dff=dff),
        grid_spec=grid_spec,
        out_shape=jax.ShapeDtypeStruct(xs.shape, BF16),
        input_output_aliases={10: 0},
        compiler_params=_cparams("arbitrary"),
        name="moe",
    )(blk_e, nact, src, dst, xs, ws, w_gu, b_gu.reshape(ne, 1, dff2), w_down, b_down.reshape(ne, 1, d),
      jnp.zeros(xs.shape, BF16))


def _combine_kernel(pos_ref, ys_ref, x1_ref, gt_ref, o_ref, p_ref, *, n_slots):
    tn = x1_ref.shape[0]
    lane = lax.broadcasted_iota(I32, (tn, SLOT_ROWS), 1)
    pos = [jnp.broadcast_to(pos_ref[:, kk:kk + 1], (tn, SLOT_ROWS)) - lane for kk in range(TOP_K)]

    def cols(c, carry):
        c0 = pl.multiple_of(c * SLOT_ROWS, SLOT_ROWS)
        p = jnp.zeros((tn, SLOT_ROWS), F32)
        for kk in range(TOP_K):
            p = p + jnp.where(pos[kk] == c0, 1.0, 0.0)
        p_ref[:, pl.ds(c0, SLOT_ROWS)] = p.astype(BF16)
        return carry

    lax.fori_loop(0, n_slots // SLOT_ROWS, cols, 0)
    o_ref[...] = x1_ref[...] + gt_ref[...] * jnp.dot(p_ref[...], ys_ref[...], preferred_element_type=F32)


def _combine(pos_t, ys, x1, mod3, tile0, rows, n_slots):
    nb, t, d = x1.shape
    rmod = mod3.shape[1]
    nt = t // rows
    per_tile = ROUTE_TILE // rows
    row = lambda b, i: (b, i, 0)
    gate = pl.BlockSpec((None, rmod, d), (lambda b, i: (b, 0, 5)) if rmod == 1 else (lambda b, i: (b, i, 5)))
    return pl.pallas_call(
        functools.partial(_combine_kernel, n_slots=n_slots),
        grid=(nb, nt),
        in_specs=[pl.BlockSpec((rows, TOP_K), lambda b, i: (b * nt + i, 0)),
                  pl.BlockSpec((n_slots, d), lambda b, i: (tile0 + (b * nt + i) // per_tile, 0)),
                  pl.BlockSpec((None, rows, d), row), gate],
        out_specs=pl.BlockSpec((None, rows, d), row),
        out_shape=jax.ShapeDtypeStruct((nb, t, d), F32),
        scratch_shapes=[pltpu.VMEM((rows, n_slots), BF16)],
        compiler_params=_cparams("parallel", "parallel"),
        name="combine",
    )(pos_t, ys, x1, mod3)


def _rope_tables(pos, n_heads):
    half = HEAD_DIM // 2
    inv_freq = 1.0 / (ROPE_THETA ** (jnp.arange(0, HEAD_DIM, 2, dtype=F32) / HEAD_DIM))
    ang = pos.astype(F32)[:, None] * inv_freq[None, :]
    cos, sin = jnp.cos(ang), jnp.sin(ang)
    del half
    return (jnp.tile(jnp.concatenate([cos, cos], axis=-1), (1, n_heads)),
            jnp.tile(jnp.concatenate([-sin, sin], axis=-1), (1, n_heads)))


def _block_diag(width, value):
    h = np.arange(width) // HEAD_DIM
    return jnp.asarray(np.where(h[:, None] == h[None, :], value, 0.0), F32)


def kernel(x_prompt, x_sample, cache_k, cache_v, state_wkv, state_shift, c_prompt, c_sample, w_ada, b_ada, norm1_g, norm2_g, w_in, q_norm_g, k_norm_g, rwkv_mu, rwkv_w0, rwkv_w2, rwkv_a0, rwkv_a2, rwkv_g2, rwkv_k_k, rwkv_k_a, rwkv_r_k, rwkv_ln_w, rwkv_ln_b, w_out, router_w, router_b, moe_w_gu, moe_b_gu, moe_w_down, moe_b_down):
    nbp, t, d = x_prompt.shape
    nbs, ts, _ = x_sample.shape
    depth = w_ada.shape[0]
    assert depth == 1 and ts == 1
    n_heads = cache_k.shape[3]
    aw = n_heads * HEAD_DIM
    rwid = rwkv_w0.shape[1]
    rwc = rwkv_mu.shape[1]
    past = cache_k.shape[2]
    ne = router_w.shape[2]
    keep = min(MAX_WINDOW, t)
    lyr = 0

    w_in_bf = w_in[lyr].astype(BF16)
    wt_bf = w_out[lyr][:aw].astype(BF16)
    wb_bf = w_out[lyr][aw:].astype(BF16)
    g1 = norm1_g[lyr].reshape(1, d)
    g2 = norm2_g[lyr].reshape(1, d)
    qg = jnp.tile(q_norm_g[lyr], n_heads).reshape(1, aw)
    kg = jnp.tile(k_norm_g[lyr], n_heads).reshape(1, aw)
    bd_mean_a = _block_diag(aw, 1.0 / HEAD_DIM)
    bd_mean_r = _block_diag(rwid, 1.0 / HEAD_DIM)
    bd_ones_r = _block_diag(rwid, 1.0)
    dl = rwkv_w2.shape[1]
    w2p = jnp.zeros((LANES, rwid), F32).at[:dl].set(rwkv_w2[lyr])
    a2p = jnp.zeros((LANES, rwid), F32).at[dl:dl + rwkv_a2.shape[1]].set(rwkv_a2[lyr])
    vec = lambda a: a[lyr].reshape(1, -1)
    rwt = router_w[lyr].T
    rb = router_b[lyr].reshape(ne, 1)

    rows_c = nbp + nbs
    rows_pad = -(-rows_c // 8) * 8
    c_all = jnp.zeros((rows_pad, d), F32).at[:nbp].set(c_prompt).at[nbp:rows_c].set(c_sample)
    mod = _ada(c_all, w_ada[lyr], b_ada[lyr])
    mod_p = mod[:nbp].reshape(nbp, 1, 6 * d)
    mod_s = mod[nbp:rows_c].reshape(1, nbs, 6 * d)
    xs3 = x_sample.reshape(1, nbs, d)

    cos_p, sin_p = _rope_tables(jnp.arange(t), n_heads)
    cos_s, sin_s = _rope_tables(jnp.full((nbs,), PAST_LEN), n_heads)
    qp, kp, vp, rwp, kt_p, vt_p = _inproj(x_prompt, mod_p, g1, w_in_bf, cos_p, sin_p, qg, kg, bd_mean_a,
                                          ROW_TILE, keep=keep)
    qs, ks, vs, rws = _inproj(xs3, mod_s, g1, w_in[lyr], cos_s, sin_s, qg, kg, bd_mean_a, nbs)

    attn_p = _attn_prompt(qp, kp, vp)
    as_rows = lambda a: a.reshape(nbs, 1, -1)
    cache_t = lambda cch: jnp.transpose(cch[lyr], (0, 2, 3, 1)).reshape(nbs, aw, past)
    attn_s = _attn_sample(as_rows(qs), as_rows(ks), as_rows(vs), cache_t(cache_k), cache_t(cache_v))

    rw_args = (vec(rwkv_mu), vec(rwkv_w0), w2p, vec(rwkv_a0), a2p, rwkv_g2[lyr], vec(rwkv_k_k), vec(rwkv_k_a), bd_ones_r)
    pre_p = _rwprep(rwp, jnp.zeros((nbp, 1, rwc), F32), *rw_args, tm=ROW_TILE, whole_prev=False)
    pre_s = _rwprep(rws, state_shift[lyr].reshape(1, nbs, rwc), *rw_args, tm=nbs, whole_prev=True)
    r_p, ld_p, k_p, v_p, al_p, be_p, g_p = pre_p
    y_p, wkv_p = _rwchunk(r_p, ld_p, k_p, v_p, al_p, be_p)
    r_s, ld_s, k_s, v_s, al_s, be_s, g_s = pre_s
    y_s, wkv_s = _rwstep(*(as_rows(a) for a in (r_s, ld_s, k_s, v_s, al_s, be_s)), state_wkv[lyr])
    y_s = y_s.reshape(1, nbs, rwid)

    n_p = nbp * t
    n_valid = n_p + nbs
    assert ROW_TILE == ROUTE_TILE and n_p % ROUTE_TILE == 0 and nbs <= ROUTE_TILE and d == SUBLANES * LANES
    n_tiles = n_p // ROUTE_TILE + 1
    n_slots = _slot_capacity(ne)
    n_rows = n_tiles * n_slots + N_OBUF * MOE_BLOCK
    op_args = (rwkv_r_k[lyr].reshape(1, rwid), vec(rwkv_ln_w), vec(rwkv_ln_b), bd_mean_r, bd_ones_r, rwt, rb)
    x1_p, pos_p, _, cnt_p, *bufs = _outproj(x_prompt, attn_p, y_p, r_p, k_p, v_p, g_p, mod_p, g2, wt_bf, wb_bf,
                                            *op_args, tm=ROW_TILE, n_slots=n_slots, n_rows=n_rows)
    x1_s, pos_s, _, cnt_s, xs, ws = _outproj(xs3, attn_s.reshape(1, nbs, aw), y_s, r_s, k_s, v_s, g_s, mod_s, g2,
                                             w_out[lyr][:aw], w_out[lyr][aw:], *op_args, tm=nbs, n_slots=n_slots,
                                             n_rows=n_rows, tile0=n_p // ROUTE_TILE, bufs=bufs)

    cnt = jnp.concatenate([cnt_p, cnt_s])[:, :, 0].astype(I32)
    per_blk = MOE_BLOCK // MOE_CHUNK
    n_blocks = -(-(n_valid * TOP_K // MOE_CHUNK + n_tiles * ne) // per_blk) + ne
    blk_e, nact, src, dst = _chunk_tables(cnt, n_slots, n_blocks)
    ys = _moe(blk_e, nact, src, dst, xs, ws, moe_w_gu[lyr], moe_b_gu[lyr], moe_w_down[lyr], moe_b_down[lyr], n_blocks)
    y_prompt = _combine(pos_p.T, ys, x1_p, mod_p, 0, ROUTE_TILE, n_slots)
    y_sample = _combine(pos_s.T, ys, x1_s, mod_s, n_p // ROUTE_TILE, nbs, n_slots)

    kept = lambda a: jnp.transpose(a.reshape(nbp, n_heads, HEAD_DIM, keep), (0, 3, 1, 2))[None]
    return (y_prompt, y_sample.reshape(nbs, ts, d), kept(kt_p), kept(vt_p), wkv_p[None], rwp[:, t - 1][None],
            ks.reshape(nbs, ts, n_heads, HEAD_DIM)[None], vs.reshape(nbs, ts, n_heads, HEAD_DIM)[None],
            wkv_s[None], rws.reshape(nbs, rwc)[None])
```

```python
import functools

import numpy as np
import jax
import jax.numpy as jnp
from jax import lax
from jax.experimental import pallas as pl
from jax.experimental.pallas import tpu as pltpu

F32 = jnp.float32
BF16 = jnp.bfloat16
I32 = jnp.int32

HEAD_DIM = 64
LANES = 128
SUBLANES = 8
DILATED_PATTERNS = ((128, 1), (512, 4), (2048, 16))
WINDOW_STEPS = 128
MAX_WINDOW = 2048
PAST_LEN = 16384
ROPE_THETA = 10000.0
NORM_EPS = 1e-6
GN_EPS = 64e-5
TOP_K = 4
SWIGLU_ALPHA = 1.702
SWIGLU_LIMIT = 7.0
RW_CHUNK = 128
RW_PAIRS = 2
MOE_BLOCK = 512
MOE_CHUNK = 16
SLOT_ROWS = 256
ATTN_UNITS = 4
ROW_TILE = 512
ROUTE_TILE = 512
VMEM_LIMIT = 56 * 1024 * 1024
NEG_BIG = -1e30


def _cparams(*sem):
    return pltpu.CompilerParams(dimension_semantics=sem, vmem_limit_bytes=VMEM_LIMIT)


def _dot(a, b):
    return jnp.dot(a.astype(BF16), b.astype(BF16), preferred_element_type=F32)


def _split2(a):
    hi = a.astype(BF16)
    lo = (a - hi.astype(F32)).astype(BF16)
    return hi, lo


def _split3(a):
    hi = a.astype(BF16)
    r1 = a - hi.astype(F32)
    mid = r1.astype(BF16)
    lo = (r1 - mid.astype(F32)).astype(BF16)
    return hi, mid, lo


def _dot_x(a, e):
    e = e.astype(BF16)
    hi, mid, lo = _split3(a)
    return (jnp.dot(hi, e, preferred_element_type=F32) + jnp.dot(mid, e, preferred_element_type=F32)
            + jnp.dot(lo, e, preferred_element_type=F32))


def _xdot(e, a):
    e = e.astype(BF16)
    hi, mid, lo = _split3(a)
    return (jnp.dot(e, hi, preferred_element_type=F32) + jnp.dot(e, mid, preferred_element_type=F32)
            + jnp.dot(e, lo, preferred_element_type=F32))


def _dot3(a, b):
    ah, al = _split2(a)
    bh, bl = _split2(b)
    return (jnp.dot(ah, bh, preferred_element_type=F32) + jnp.dot(ah, bl, preferred_element_type=F32)
            + jnp.dot(al, bh, preferred_element_type=F32))


def _dot3_nt(a, b):
    ah, al = _split2(a)
    bh, bl = _split2(b)
    dn = (((1,), (1,)), ((), ()))
    return (lax.dot_general(ah, bh, dn, preferred_element_type=F32)
            + lax.dot_general(ah, bl, dn, preferred_element_type=F32)
            + lax.dot_general(al, bh, dn, preferred_element_type=F32))


def _mm(a, w):
    if w.dtype == BF16:
        return jnp.dot(a.astype(BF16), w, preferred_element_type=F32)
    return _dot3(a, w)


def _sigmoid(x):
    return 1.0 / (1.0 + jnp.exp(-x))


def _ada_kernel(c_ref, w_ref, b_ref, o_ref):
    c = c_ref[...]
    o_ref[...] = _dot3(c * _sigmoid(c), w_ref[...]) + b_ref[...]


def _ada(c_all, w_ada, b_ada):
    rows, d = c_all.shape
    n = w_ada.shape[1]
    tn = n // 4
    return pl.pallas_call(
        _ada_kernel,
        grid=(n // tn,),
        in_specs=[pl.BlockSpec((rows, d), lambda j: (0, 0)),
                  pl.BlockSpec((d, tn), lambda j: (0, j)),
                  pl.BlockSpec((1, tn), lambda j: (0, j))],
        out_specs=pl.BlockSpec((rows, tn), lambda j: (0, j)),
        out_shape=jax.ShapeDtypeStruct((rows, n), F32),
        compiler_params=_cparams("arbitrary"),
        name="ada",
    )(c_all, w_ada, b_ada.reshape(1, n))


def _inproj_kernel(x_ref, sh_ref, sc_ref, g_ref, w_ref, cos_ref, sin_ref, qg_ref, kg_ref, bd_ref,
                   q_ref, k_ref, v_ref, rw_ref, *maybe_kv_t, aw, first_kept):
    x = x_ref[...]
    ms = jnp.mean(x * x, axis=-1, keepdims=True)
    h = x * lax.rsqrt(ms + NORM_EPS) * g_ref[...] * (1.0 + sc_ref[...]) + sh_ref[...]
    proj = _mm(h, w_ref[...])
    cos = cos_ref[...]
    sin = sin_ref[...]
    lane = lax.broadcasted_iota(I32, (1, aw), 1)
    first_half = (lane % HEAD_DIM) < (HEAD_DIM // 2)
    bd = bd_ref[...]

    def norm_rope(t, g):
        tn = t * lax.rsqrt((_dot if w_ref.dtype == BF16 else _dot_x)(t * t, bd) + NORM_EPS) * g
        rot = jnp.where(first_half, pltpu.roll(tn, aw - HEAD_DIM // 2, 1), pltpu.roll(tn, HEAD_DIM // 2, 1))
        return tn * cos + rot * sin

    scale = 1.0 / np.sqrt(HEAD_DIM).astype(np.float32)
    q_ref[...] = norm_rope(proj[:, :aw], qg_ref[...]) * scale
    k = norm_rope(proj[:, aw:2 * aw], kg_ref[...])
    v = proj[:, 2 * aw:3 * aw]
    k_ref[...] = k
    v_ref[...] = v
    rw_ref[...] = proj[:, 3 * aw:]
    if maybe_kv_t:
        kt_ref, vt_ref = maybe_kv_t
        kept = pl.program_id(1) >= first_kept

        @pl.when(kept)
        def _():
            kt_ref[...] = k.T
            vt_ref[...] = v.T

        @pl.when(jnp.logical_not(kept))
        def _():
            kt_ref[...] = jnp.zeros_like(kt_ref)
            vt_ref[...] = jnp.zeros_like(vt_ref)


def _inproj(x3, mod3, g1, w_in_bf, cos_t, sin_t, qg, kg, bd_mean, tm, keep=0):
    nb, t, d = x3.shape
    r = mod3.shape[1]
    ncol = w_in_bf.shape[1]
    aw = cos_t.shape[1]
    rwc = ncol - 3 * aw
    grid = (nb, t // tm)
    row = lambda b, i: (b, i, 0)
    const = lambda b, i: (0, 0)
    mod_spec = lambda s: pl.BlockSpec((None, r, d), (lambda b, i: (b, 0, s)) if r == 1 else (lambda b, i: (b, i, s)))
    out_specs = [pl.BlockSpec((None, tm, aw), row)] * 3 + [pl.BlockSpec((None, tm, rwc), row)]
    out_shape = [jax.ShapeDtypeStruct((nb, t, aw), F32)] * 3 + [jax.ShapeDtypeStruct((nb, t, rwc), F32)]
    first_kept = (t - keep) // tm
    if keep:
        assert keep % tm == 0 and (t - keep) % tm == 0
        kept_spec = pl.BlockSpec((None, aw, tm), lambda b, i: (b, 0, jnp.maximum(i - first_kept, 0)))
        out_specs += [kept_spec, kept_spec]
        out_shape += [jax.ShapeDtypeStruct((nb, aw, keep), F32)] * 2
    outs = pl.pallas_call(
        functools.partial(_inproj_kernel, aw=aw, first_kept=first_kept),
        grid=grid,
        in_specs=[pl.BlockSpec((None, tm, d), row), mod_spec(0), mod_spec(1),
                  pl.BlockSpec((1, d), const), pl.BlockSpec((d, ncol), const),
                  pl.BlockSpec((tm, aw), lambda b, i: (i, 0)), pl.BlockSpec((tm, aw), lambda b, i: (i, 0)),
                  pl.BlockSpec((1, aw), const), pl.BlockSpec((1, aw), const), pl.BlockSpec((aw, aw), const)],
        out_specs=out_specs,
        out_shape=out_shape,
        compiler_params=_cparams("parallel", "arbitrary"),
        name="inproj",
    )(x3, mod3, mod3, g1, w_in_bf, cos_t, sin_t, qg, kg, bd_mean)
    return outs


def _attn_prompt_kernel(q_ref, k_ref, v_ref, o_ref, kp_ref, vp_ref, m_ref, l_ref, acc_ref, *, t, pad):
    nq = WINDOW_STEPS
    nk = 2 * WINDOW_STEPS
    kp_ref[pl.ds(0, pad), :] = jnp.zeros((pad, LANES), F32)
    vp_ref[pl.ds(0, pad), :] = jnp.zeros((pad, LANES), F32)
    kp_ref[pl.ds(pad, t), :] = k_ref[...]
    vp_ref[pl.ds(pad, t), :] = v_ref[...]
    head0 = lax.broadcasted_iota(I32, (1, LANES), 1) < HEAD_DIM
    qi = lax.broadcasted_iota(I32, (nq, nk), 0)
    kj = lax.broadcasted_iota(I32, (nq, nk), 1)
    steps_back = qi + nq - kj
    band = (steps_back >= 0) & (steps_back <= WINDOW_STEPS)
    has_past = kj >= nq

    hsels = (head0, jnp.logical_not(head0))
    nt = (((1,), (1,)), ((), ()))

    for p, (_, d) in enumerate(DILATED_PATTERNS):
        def units(g, carry, p=p, d=d):
            rows_q, kb, vb, valid, qh = [], [], [], [], []
            for j in range(ATTN_UNITS):
                u = g * ATTN_UNITS + j
                res = u % d
                blk = u // d
                q_start = res + d * nq * blk
                k_start = pad + q_start - d * nq
                if d == 1:
                    rows_q.append(pl.ds(q_start, nq))
                    rows_k = pl.ds(k_start, nk)
                else:
                    rows_q.append(pl.ds(q_start, nq, stride=d))
                    rows_k = pl.ds(k_start, nk, stride=d)
                q = q_ref[rows_q[j], :]
                kb.append(kp_ref[rows_k, :].astype(BF16))
                vb.append(vp_ref[rows_k, :].astype(BF16))
                valid.append(band & (has_past | (blk > 0)))
                qh.append([jnp.where(hsel, q, 0.0).astype(BF16) for hsel in hsels])
            chains = [(j, h) for j in range(ATTN_UNITS) for h in range(2)]
            s = [lax.dot_general(qh[j][h], kb[j], nt, preferred_element_type=F32) for j, h in chains]
            s = [jnp.where(valid[j], sc, NEG_BIG) for (j, h), sc in zip(chains, s)]
            mx = [jnp.max(sc, axis=-1, keepdims=True) for sc in s]
            e = [jnp.exp(sc - m) for sc, m in zip(s, mx)]
            den = [jnp.sum(ec, axis=-1, keepdims=True) for ec in e]
            o = [jnp.dot(ec.astype(BF16), vb[j], preferred_element_type=F32) for (j, h), ec in zip(chains, e)]
            for j in range(ATTN_UNITS):
                m_ref[p, rows_q[j], :] = jnp.where(head0, mx[2 * j], mx[2 * j + 1])
                l_ref[p, rows_q[j], :] = jnp.where(head0, den[2 * j], den[2 * j + 1])
                acc_ref[p, rows_q[j], :] = jnp.where(head0, o[2 * j], o[2 * j + 1])
            return carry

        lax.fori_loop(0, t // nq // ATTN_UNITS, units, 0)

    rows = 256

    def merge(i, carry):
        sl = pl.ds(pl.multiple_of(i * rows, rows), rows)
        m0, m1, m2 = m_ref[0, sl, :], m_ref[1, sl, :], m_ref[2, sl, :]
        mm = jnp.maximum(jnp.maximum(m0, m1), m2)
        w0, w1, w2 = jnp.exp(m0 - mm), jnp.exp(m1 - mm), jnp.exp(m2 - mm)
        num = w0 * acc_ref[0, sl, :] + w1 * acc_ref[1, sl, :] + w2 * acc_ref[2, sl, :]
        den = w0 * l_ref[0, sl, :] + w1 * l_ref[1, sl, :] + w2 * l_ref[2, sl, :]
        o_ref[sl, :] = num / den
        return carry

    lax.fori_loop(0, t // rows, merge, 0)


def _attn_prompt(q, k, v):
    nb, t, aw = q.shape
    pad = MAX_WINDOW
    assert t % MAX_WINDOW == 0
    spec = pl.BlockSpec((None, t, LANES), lambda b, hp: (b, 0, hp))
    return pl.pallas_call(
        functools.partial(_attn_prompt_kernel, t=t, pad=pad),
        grid=(nb, aw // LANES),
        in_specs=[spec, spec, spec],
        out_specs=spec,
        out_shape=jax.ShapeDtypeStruct((nb, t, aw), F32),
        scratch_shapes=[pltpu.VMEM((pad + t, LANES), F32), pltpu.VMEM((pad + t, LANES), F32),
                        pltpu.VMEM((3, t, LANES), F32), pltpu.VMEM((3, t, LANES), F32),
                        pltpu.VMEM((3, t, LANES), F32)],
        compiler_params=_cparams("parallel", "parallel"),
        name="attn_prompt",
    )(q, k, v)


def _attn_sample_kernel(q_ref, kn_ref, vn_ref, kt_ref, vt_ref, o_ref, *, nh, w):
    aw = nh * HEAD_DIM
    dist = w - lax.broadcasted_iota(I32, (1, w), 1)
    mult = jnp.zeros((1, w), F32)
    for win, d in DILATED_PATTERNS:
        mult = mult + jnp.where((dist % d == 0) & (dist <= win), 1.0, 0.0)
    n_pat = float(len(DILATED_PATTERNS))
    q_col = _col(q_ref[...], aw)
    kn_col = _col(kn_ref[...], aw)
    vn_col = _col(vn_ref[...], aw)
    heads = range(nh)
    hs = [pl.ds(h * HEAD_DIM, HEAD_DIM) for h in heads]
    cut = lambda col, h: col[h * HEAD_DIM:(h + 1) * HEAD_DIM]
    s = [jnp.sum(kt_ref[hs[h], :] * cut(q_col, h), axis=0, keepdims=True) for h in heads]
    s_self = [jnp.sum(cut(q_col, h) * cut(kn_col, h), axis=0, keepdims=True) for h in heads]
    s = [jnp.where(mult > 0.0, sh, NEG_BIG) for sh in s]
    mx = [jnp.maximum(jnp.max(sh, axis=1, keepdims=True), ss) for sh, ss in zip(s, s_self)]
    pr = [mult * jnp.exp(sh - m) for sh, m in zip(s, mx)]
    p_self = [n_pat * jnp.exp(ss - m) for ss, m in zip(s_self, mx)]
    den = [jnp.sum(p, axis=1, keepdims=True) + ps for p, ps in zip(pr, p_self)]
    num = [jnp.sum(vt_ref[hs[h], :] * pr[h], axis=1, keepdims=True) + p_self[h] * cut(vn_col, h) for h in heads]
    outs = [n / dn for n, dn in zip(num, den)]
    o_ref[...] = _row(jnp.concatenate(outs, axis=0), aw)


def _attn_sample(q, kn, vn, cache_kt, cache_vt):
    nb, _, aw = q.shape
    w = cache_kt.shape[2]
    assert w == MAX_WINDOW
    row = pl.BlockSpec((None, 1, aw), lambda i: (i, 0, 0))
    mat = pl.BlockSpec((None, aw, w), lambda i: (i, 0, 0))
    return pl.pallas_call(
        functools.partial(_attn_sample_kernel, nh=aw // HEAD_DIM, w=w),
        grid=(nb,),
        in_specs=[row, row, row, mat, mat],
        out_specs=row,
        out_shape=jax.ShapeDtypeStruct((nb, 1, aw), F32),
        compiler_params=_cparams("parallel"),
        name="attn_sample",
    )(q, kn, vn, cache_kt, cache_vt)


def _rwprep_kernel(p_ref, prev_ref, first_ref, mu_ref, w0_ref, w2_ref, a0_ref, a2_ref, g2_ref, kk_ref, ka_ref,
                   bd_ref, r_ref, ld_ref, k_ref, v_ref, al_ref, be_ref, g_ref, *, rwid, whole_prev):
    p = p_ref[...]
    if whole_prev:
        prev = first_ref[...]
    else:
        tm = p.shape[0]
        before = jnp.where(pl.program_id(1) == 0, first_ref[...], prev_ref[7:8, :])
        rowi = lax.broadcasted_iota(I32, (tm, 1), 0)
        prev = jnp.where(rowi == 0, before, pltpu.roll(p, 1, 0))
    xs = p + mu_ref[...] * (prev - p)
    r = xs[:, :rwid]
    k = xs[:, rwid:2 * rwid]
    v = xs[:, 2 * rwid:3 * rwid]
    xwa = xs[:, 3 * rwid:3 * rwid + LANES]
    xg = xs[:, 3 * rwid + LANES:]
    z = w0_ref[...] + _dot3(jnp.tanh(xwa), w2_ref[...])
    softplus_neg = jnp.maximum(-z, 0.0) + jnp.log(1.0 + jnp.exp(-jnp.abs(z)))
    w = -softplus_neg - 0.5
    a = _sigmoid(a0_ref[...] + _dot3(xwa, a2_ref[...]))
    g = _dot3(_sigmoid(xg), g2_ref[...])
    kk = k * kk_ref[...]
    norm = jnp.sqrt(_dot_x(kk * kk, bd_ref[...]))
    kk = kk / jnp.maximum(norm, 1e-12)
    r_ref[...] = r
    ld_ref[...] = -jnp.exp(w)
    k_ref[...] = k * (1.0 + (a - 1.0) * ka_ref[...])
    v_ref[...] = v
    al_ref[...] = -kk
    be_ref[...] = kk * a
    g_ref[...] = g


def _rwprep(rw, first, mu, w0, w2p, a0, a2p, g2, k_k, k_a, bd_ones, tm, whole_prev):
    nb, t, rwc = rw.shape
    rwid = w0.shape[1]
    grid = (nb, t // tm)
    row = lambda b, i: (b, i, 0)
    const = lambda b, i: (0, 0)
    if whole_prev:
        prev_spec = pl.BlockSpec((None, tm, rwc), row)
        first_spec = pl.BlockSpec((None, tm, rwc), row)
    else:
        prev_spec = pl.BlockSpec((None, 8, rwc), lambda b, i: (b, jnp.maximum(i * (tm // 8) - 1, 0), 0))
        first_spec = pl.BlockSpec((None, 1, rwc), lambda b, i: (b, 0, 0))
    vec = pl.BlockSpec((1, rwid), const)
    out = pl.BlockSpec((None, tm, rwid), row)
    return pl.pallas_call(
        functools.partial(_rwprep_kernel, rwid=rwid, whole_prev=whole_prev),
        grid=grid,
        in_specs=[pl.BlockSpec((None, tm, rwc), row), prev_spec, first_spec,
                  pl.BlockSpec((1, rwc), const), vec, pl.BlockSpec((LANES, rwid), const),
                  vec, pl.BlockSpec((LANES, rwid), const), pl.BlockSpec((LANES, rwid), const), vec, vec,
                  pl.BlockSpec((rwid, rwid), const)],
        out_specs=[out] * 7,
        out_shape=[jax.ShapeDtypeStruct((nb, t, rwid), F32)] * 7,
        compiler_params=_cparams("parallel", "parallel"),
        name="rwprep",
    )(rw, rw, first, mu, w0, w2p, a0, a2p, g2, k_k, k_a, bd_ones)


def _rwchunk_kernel(r_ref, ld_ref, k_ref, v_ref, al_ref, be_ref, y_ref, st_ref, z_ref, *, nb, npb):
    c = RW_CHUNK
    ci = pl.program_id(1)

    @pl.when(ci == 0)
    def _():
        z_ref[...] = jnp.zeros_like(z_ref)

    ti = lax.broadcasted_iota(I32, (c, c), 0)
    si = lax.broadcasted_iota(I32, (c, c), 1)
    low_incl = si <= ti
    low_strict = si < ti
    diag = si == ti
    tri = jnp.where(low_incl, 1.0, 0.0).astype(BF16)
    eye = jnp.where(diag, 1.0, 0.0)
    head0 = lax.broadcasted_iota(I32, (1, LANES), 1) < HEAD_DIM
    hsels = (head0, jnp.logical_not(head0))
    same_head = (ti < HEAD_DIM) == (si < HEAD_DIM)
    nt = (((1,), (1,)), ((), ()))
    batches = range(nb * npb)
    chains = [(b, h) for b in batches for h in range(2)]
    lanes_of = lambda b: pl.ds((b // nb) * LANES, LANES)
    r_v, ld_v, k_v, v_v, al_v, be_v = ([ref[b % nb, :, lanes_of(b)] for b in batches]
                                       for ref in (r_ref, ld_ref, k_ref, v_ref, al_ref, be_ref))

    cum = [_xdot(tri, ld_v[b]) for b in batches]
    tot = [cm[c - 1:c, :] for cm in cum]
    e_neg = [jnp.exp(-cm) for cm in cum]
    at = [al_v[b] * jnp.exp(cum[b] - ld_v[b]) for b in batches]
    rt = [r_v[b] * jnp.exp(cum[b]) for b in batches]
    rhs_t = [jnp.concatenate([be_v[b] * e_neg[b], k_v[b] * e_neg[b]], axis=0).astype(BF16) for b in batches]
    vb = [v_v[b].astype(BF16) for b in batches]
    at_h = [jnp.where(hsels[h], at[b], 0.0) for b, h in chains]
    rt_h = [jnp.where(hsels[h], rt[b], 0.0) for b, h in chains]
    a4 = [lax.dot_general(jnp.concatenate([a, r], axis=0).astype(BF16), rhs_t[b], nt, preferred_element_type=F32)
          for (b, h), a, r in zip(chains, at_h, rt_h)]
    a_ab = [jnp.where(low_strict, m[:c, :c], 0.0) for m in a4]
    a_ak = [jnp.where(low_strict, m[:c, c:], 0.0).astype(BF16) for m in a4]
    a_r = [jnp.concatenate([jnp.where(low_incl, m[c:, :c], 0.0), jnp.where(low_incl, m[c:, c:], 0.0)],
                           axis=1).astype(BF16) for m in a4]
    levels = int(np.log2(c))
    pw = [m.astype(BF16) for m in a_ab]
    pw = [jnp.dot(m, m, preferred_element_type=F32).astype(BF16) for m in pw]
    inv = [eye + m for m in a_ab]
    for _ in range(1, levels - 1):
        x2 = [jnp.dot(m, jnp.concatenate([m, i.astype(BF16)], axis=1), preferred_element_type=F32)
              for m, i in zip(pw, inv)]
        pw = [m[:, :c].astype(BF16) for m in x2]
        inv = [i + m[:, c:] for i, m in zip(inv, x2)]
    inv = [i + jnp.dot(m, i.astype(BF16), preferred_element_type=F32) for i, m in zip(inv, pw)]
    akv = [jnp.dot(m, vb[b], preferred_element_type=F32) for (b, h), m in zip(chains, a_ak)]
    x = [_dot(i, jnp.concatenate([kv, a], axis=1)) for i, kv, a in zip(inv, akv, at_h)]
    u0_h = [m[:, :LANES] for m in x]
    at2_h = [m[:, LANES:] for m in x]
    y0_h = [jnp.dot(ar, jnp.concatenate([u0, v_v[b]], axis=0).astype(BF16), preferred_element_type=F32)
            for (b, h), ar, u0 in zip(chains, a_r, u0_h)]
    rt2_h = [r + jnp.dot(ar[:, :c], a2.astype(BF16), preferred_element_type=F32)
             for r, ar, a2 in zip(rt_h, a_r, at2_h)]
    z = [z_ref[b] for b in batches]
    uy = [_dot(jnp.concatenate([at2_h[2 * b] + at2_h[2 * b + 1], rt2_h[2 * b] + rt2_h[2 * b + 1]], axis=0), z[b])
          for b in batches]
    u = [uy[b][:c] + jnp.where(head0, u0_h[2 * b], u0_h[2 * b + 1]) for b in batches]
    for b in batches:
        y_ref[b % nb, :, lanes_of(b)] = uy[b][c:] + jnp.where(head0, y0_h[2 * b], y0_h[2 * b + 1])
    e_end = [jnp.exp(tot[b] - cum[b]) for b in batches]
    lhs_t = [jnp.concatenate([be_v[b] * e_end[b], k_v[b] * e_end[b]], axis=0) for b in batches]
    zadd = [_dot(lhs_t[b].T, jnp.concatenate([u[b], v_v[b]], axis=0)) for b in batches]
    for b in batches:
        dcol = jnp.sum(jnp.where(diag, jnp.broadcast_to(jnp.exp(tot[b]), (c, c)), 0.0), axis=1, keepdims=True)
        z_ref[b] = dcol * z[b] + jnp.where(same_head, zadd[b], 0.0)

    @pl.when(ci == pl.num_programs(1) - 1)
    def _():
        for b in batches:
            s = z_ref[b].T
            st_ref[b % nb, 2 * (b // nb)] = s[:HEAD_DIM, :HEAD_DIM]
            st_ref[b % nb, 2 * (b // nb) + 1] = s[HEAD_DIM:, HEAD_DIM:]


def _rwchunk(r, ld, k, v, al, be):
    nb, t, rwid = r.shape
    c = RW_CHUNK
    npb = RW_PAIRS
    assert t % c == 0 and c == LANES and rwid % (npb * LANES) == 0
    ngrp = rwid // (npb * LANES)
    seq = pl.BlockSpec((nb, c, npb * LANES), lambda hp, ci: (0, ci, hp))
    return pl.pallas_call(
        functools.partial(_rwchunk_kernel, nb=nb, npb=npb),
        grid=(ngrp, t // c),
        in_specs=[seq] * 6,
        out_specs=[seq, pl.BlockSpec((nb, 2 * npb, HEAD_DIM, HEAD_DIM), lambda hp, ci: (0, hp, 0, 0))],
        out_shape=[jax.ShapeDtypeStruct((nb, t, rwid), F32),
                   jax.ShapeDtypeStruct((nb, 2 * npb * ngrp, HEAD_DIM, HEAD_DIM), F32)],
        scratch_shapes=[pltpu.VMEM((nb * npb, LANES, LANES), F32)],
        compiler_params=_cparams("parallel", "arbitrary"),
        name="rwchunk",
    )(r, ld, k, v, al, be)


def _col(row, width):
    return jnp.broadcast_to(row, (LANES, width)).T[:, 0:1]


def _row(col, width):
    return jnp.broadcast_to(col, (width, LANES)).T[0:1, :]


def _rwstep_kernel(r_ref, ld_ref, k_ref, v_ref, al_ref, be_ref, s_ref, y_ref, so_ref, *, bb, nh):
    rwid = nh * HEAD_DIM

    def one(b, carry):
        r, dcy, k, al, be = (ref[b] for ref in (r_ref, ld_ref, k_ref, al_ref, be_ref))
        dcy = jnp.exp(dcy)
        v_col = _col(v_ref[b], rwid)
        heads = range(nh)
        hs = [slice(h * HEAD_DIM, (h + 1) * HEAD_DIM) for h in heads]
        st = [s_ref[b, h] for h in heads]
        sa = [jnp.sum(st[h] * al[:, hs[h]], axis=1, keepdims=True) for h in heads]
        st = [st[h] * dcy[:, hs[h]] + sa[h] * be[:, hs[h]] + v_col[hs[h], :] * k[:, hs[h]] for h in heads]
        for h in heads:
            so_ref[b, h] = st[h]
        ys = [jnp.sum(st[h] * r[:, hs[h]], axis=1, keepdims=True) for h in heads]
        y_ref[b] = _row(jnp.concatenate(ys, axis=0), rwid)
        return carry

    lax.fori_loop(0, bb, one, 0)


def _rwstep(r, ld, k, v, al, be, state):
    nb, _, rwid = r.shape
    nh = state.shape[1]
    bb = 8
    row = pl.BlockSpec((bb, 1, rwid), lambda i: (i, 0, 0))
    st = pl.BlockSpec((bb, nh, HEAD_DIM, HEAD_DIM), lambda i: (i, 0, 0, 0))
    return pl.pallas_call(
        functools.partial(_rwstep_kernel, bb=bb, nh=nh),
        grid=(nb // bb,),
        in_specs=[row] * 6 + [st],
        out_specs=[row, st],
        out_shape=[jax.ShapeDtypeStruct((nb, 1, rwid), F32), jax.ShapeDtypeStruct(state.shape, F32)],
        compiler_params=_cparams("parallel"),
        name="rwstep",
    )(r, ld, k, v, al, be, state)


def _outproj_kernel(x_ref, at_ref, y_ref, r_ref, k_ref, v_ref, g_ref, gt_ref, sh_ref, sc_ref, g2_ref,
                    wt_ref, wb_ref, rk_ref, lw_ref, lb_ref, bdm_ref, bd1_ref, rwt_ref, rb_ref, *rest, n_slots):
    x1_ref, pos_ref, wt4_ref, cnt_ref, xs_ref, ws_ref, hb_ref = rest[-7:]
    y = y_ref[...]
    bdm = bdm_ref[...]
    mean = _dot_x(y, bdm)
    yc = y - mean
    var = _dot_x(yc * yc, bdm)
    yn = yc * lax.rsqrt(var + GN_EPS) * lw_ref[...] + lb_ref[...]
    v = v_ref[...]
    bonus = _dot_x(r_ref[...] * k_ref[...] * rk_ref[...], bd1_ref[...]) * v
    rw = (yn + bonus) * g_ref[...]
    mix = _mm(at_ref[...], wt_ref[...]) + _mm(rw, wb_ref[...])
    x1 = x_ref[...] + gt_ref[...] * mix
    x1_ref[...] = x1
    ms = jnp.mean(x1 * x1, axis=-1, keepdims=True)
    h2 = x1 * lax.rsqrt(ms + NORM_EPS) * g2_ref[...] * (1.0 + sc_ref[...]) + sh_ref[...]
    pos, wts, cnt = _route_tile(_dot3_nt(rwt_ref[...], h2) + rb_ref[...])
    for kk in range(TOP_K):
        pos_ref[pl.ds(kk, 1), :] = pos[kk]
        wt4_ref[pl.ds(kk, 1), :] = wts[kk]
    cnt_ref[...] = jnp.broadcast_to(cnt, cnt_ref.shape)
    hb_ref[...] = h2.astype(BF16)
    _sort_tile(pos, wts, hb_ref, xs_ref, ws_ref, n_slots)


def _outproj(x3, attn, y, r, k, v, g, mod3, g2, wt_bf, wb_bf, r_k, ln_w, ln_b, bd_mean, bd_ones, rwt, rb, tm,
             n_slots, n_rows, tile0=0, bufs=None):
    nb, t, d = x3.shape
    rmod = mod3.shape[1]
    aw = attn.shape[2]
    ne = rwt.shape[0]
    nt = t // tm
    grid = (nb, nt)
    row = lambda b, i: (b, i, 0)
    const = lambda b, i: (0, 0)
    tile = lambda b, i: (tile0 + b * nt + i, 0)
    mod_spec = lambda s: pl.BlockSpec((None, rmod, d), (lambda b, i: (b, 0, s)) if rmod == 1 else (lambda b, i: (b, i, s)))
    half = pl.BlockSpec((None, tm, aw), row)
    vec = pl.BlockSpec((1, aw), const)
    tok4 = pl.BlockSpec((TOP_K, tm), lambda b, i: (0, b * nt + i))
    bufs = () if bufs is None else tuple(bufs)
    n_in = 20
    return pl.pallas_call(
        functools.partial(_outproj_kernel, n_slots=n_slots),
        grid=grid,
        in_specs=[pl.BlockSpec((None, tm, d), row)] + [half] * 6 + [mod_spec(2), mod_spec(3), mod_spec(4),
                  pl.BlockSpec((1, d), const), pl.BlockSpec((aw, d), const), pl.BlockSpec((aw, d), const),
                  vec, vec, vec, pl.BlockSpec((aw, aw), const), pl.BlockSpec((aw, aw), const),
                  pl.BlockSpec((ne, d), const), pl.BlockSpec((ne, 1), const)]
                 + [pl.BlockSpec(memory_space=pl.ANY)] * len(bufs),
        out_specs=[pl.BlockSpec((None, tm, d), row), tok4, tok4,
                   pl.BlockSpec((None, ne, LANES), lambda b, i: (b * nt + i, 0, 0)),
                   pl.BlockSpec((n_slots, d), tile), pl.BlockSpec((n_slots, LANES), tile)],
        out_shape=[jax.ShapeDtypeStruct((nb, t, d), F32), jax.ShapeDtypeStruct((TOP_K, nb * t), I32),
                   jax.ShapeDtypeStruct((TOP_K, nb * t), F32), jax.ShapeDtypeStruct((nb * nt, ne, LANES), F32),
                   jax.ShapeDtypeStruct((n_rows, d), BF16), jax.ShapeDtypeStruct((n_rows, LANES), F32)],
        scratch_shapes=[pltpu.VMEM((tm, d), BF16)],
        input_output_aliases={n_in: 4, n_in + 1: 5} if bufs else {},
        compiler_params=_cparams("parallel", "parallel"),
        name="outproj",
    )(x3, attn, y, r, k, v, g, mod3, mod3, mod3, g2, wt_bf, wb_bf, r_k, ln_w, ln_b, bd_mean, bd_ones, rwt, rb, *bufs)


def _route_tile(lg):
    ne, tn = lg.shape
    eidx = lax.broadcasted_iota(I32, (ne, tn), 0).astype(F32)
    vals, hots = [], []
    for _ in range(TOP_K):
        mx = jnp.max(lg, axis=0, keepdims=True)
        pick = jnp.min(jnp.where(lg == mx, eidx, float(ne)), axis=0, keepdims=True)
        hot = eidx == pick
        vals.append(mx)
        hots.append(hot)
        lg = jnp.where(hot, -jnp.inf, lg)
    ex = [jnp.exp(vv - vals[0]) for vv in vals]
    den = ex[0] + ex[1] + ex[2] + ex[3]
    wts = [e / den for e in ex]
    hot_all = jnp.zeros((ne, tn), F32)
    for hot in hots:
        hot_all = hot_all + jnp.where(hot, 1.0, 0.0)
    ri = lax.broadcasted_iota(I32, (tn, tn), 0)
    cj = lax.broadcasted_iota(I32, (tn, tn), 1)
    upper = jnp.where(ri <= cj, 1.0, 0.0).astype(BF16)
    before = jnp.dot(hot_all.astype(BF16), upper, preferred_element_type=F32) - hot_all
    cnt = jnp.sum(hot_all, axis=1, keepdims=True)
    padded = jnp.ceil(cnt / MOE_CHUNK) * MOE_CHUNK
    er = lax.broadcasted_iota(I32, (ne, ne), 0)
    ec = lax.broadcasted_iota(I32, (ne, ne), 1)
    lower_strict = jnp.where(ec < er, 1.0, 0.0)
    off = _xdot(lower_strict, jnp.broadcast_to(padded, (ne, LANES)))[:, 0:1]
    pos = [jnp.sum(jnp.where(hot, off + before, 0.0), axis=0, keepdims=True).astype(I32) for hot in hots]
    return pos, wts, cnt


def _slot_capacity(ne):
    return -(-(TOP_K * ROUTE_TILE + ne * (MOE_CHUNK - 1)) // SLOT_ROWS) * SLOT_ROWS


def _sort_tile(pos, wts, h_ref, xs_ref, ws_ref, n_slots):
    tn = h_ref.shape[0]

    def rows(c, carry):
        h = h_ref[...]
        r0 = pl.multiple_of(c * SLOT_ROWS, SLOT_ROWS)
        slot = r0 + lax.broadcasted_iota(I32, (SLOT_ROWS, tn), 0)
        pw = jnp.zeros((SLOT_ROWS, tn), F32)
        for kk in range(TOP_K):
            pw = pw + jnp.where(slot == pos[kk], wts[kk], 0.0)
        p = jnp.where(pw > 0.0, 1.0, 0.0)
        xs_ref[pl.ds(r0, SLOT_ROWS), :] = jnp.dot(p.astype(BF16), h, preferred_element_type=F32).astype(BF16)
        ws_ref[pl.ds(r0, SLOT_ROWS), :] = jnp.broadcast_to(jnp.sum(pw, axis=1, keepdims=True), (SLOT_ROWS, LANES))
        return carry

    lax.fori_loop(0, n_slots // SLOT_ROWS, rows, 0)


def _chunk_tables(cnt, n_slots, n_blocks):
    nt, ne = cnt.shape
    per_blk = MOE_BLOCK // MOE_CHUNK
    nch = -(-cnt // MOE_CHUNK)
    seg0 = (jnp.cumsum(nch, axis=1) - nch) + (jnp.arange(nt) * (n_slots // MOE_CHUNK))[:, None]
    cum_t = jnp.cumsum(nch, axis=0)
    total = cum_t[-1]
    blocks = -(-total // per_blk)
    blk_end = jnp.cumsum(blocks)
    blk = jnp.arange(n_blocks)
    blk_e = jnp.minimum(jnp.sum(blk_end[None, :] <= blk[:, None], axis=1), ne - 1).astype(I32)
    nact = blk_end[-1].reshape(1).astype(I32)
    pick_e = blk_e[:, None] == jnp.arange(ne)[None, :]
    of_e = lambda tab: jnp.sum(jnp.where(pick_e, tab[None, :], 0), axis=1)
    col_e = lambda tab: jnp.sum(jnp.where(pick_e[:, None, :], tab[None, :, :], 0), axis=2)
    q = ((blk - of_e(blk_end - blocks)) * per_blk)[:, None] + jnp.arange(per_blk)[None, :]
    live = (q < of_e(total)[:, None]) & (blk < nact[0])[:, None]
    t_n = jnp.minimum(jnp.sum(col_e(cum_t)[:, None, :] <= q[:, :, None], axis=2), nt - 1)
    pick_t = t_n[:, :, None] == jnp.arange(nt)[None, None, :]
    of_t = lambda tab: jnp.sum(jnp.where(pick_t, tab[:, None, :], 0), axis=2)
    src = jnp.where(live, of_t(col_e(seg0)) + q - of_t(col_e(cum_t - nch)), 0)
    n = blk[:, None] * per_blk + jnp.arange(per_blk)[None, :]
    spare = nt * (n_slots // MOE_CHUNK) + n % (N_OBUF * per_blk)
    dst = jnp.where(live, src, spare)
    return blk_e, nact, src.reshape(-1).astype(I32), dst.reshape(-1).astype(I32)


N_OBUF = 2


def _moe_kernel(be_ref, nact_ref, src_ref, dst_ref, xs_ref, ws_ref, wgu_ref, bgu_ref, wd_ref, bd_ref, out_in_ref,
                out_ref, xbuf, wbuf, obuf, gsem, ssem, wgu_bf, wd_bf, *, dff):
    del out_in_ref
    i = pl.program_id(0)
    nact = nact_ref[0]
    per_blk = MOE_BLOCK // MOE_CHUNK
    rows = lambda c: pl.ds(pl.multiple_of(c * MOE_CHUNK, MOE_CHUNK), MOE_CHUNK)

    def gathers(blk, slot):
        copies = []
        for m in range(per_blk):
            c = src_ref[blk * per_blk + m]
            copies.append(pltpu.make_async_copy(xs_ref.at[rows(c), :], xbuf.at[slot, rows(m), :], gsem.at[slot]))
            copies.append(pltpu.make_async_copy(ws_ref.at[rows(c), :], wbuf.at[slot, rows(m), :], gsem.at[slot]))
        return copies

    def scatters(blk, slot):
        return [pltpu.make_async_copy(obuf.at[slot, rows(m), :], out_ref.at[rows(dst_ref[blk * per_blk + m]), :],
                                      ssem.at[slot]) for m in range(per_blk)]

    @pl.when(i == 0)
    def _():
        for cp in gathers(0, 0):
            cp.start()

    @pl.when(i < nact)
    def _():
        slot = i % 2
        for cp in gathers(i, slot):
            cp.wait()

        @pl.when(i + 1 < nact)
        def _():
            for cp in gathers(i + 1, 1 - slot):
                cp.start()

        @pl.when((i == 0) | (be_ref[i] != be_ref[jnp.maximum(i - 1, 0)]))
        def _():
            wgu_bf[...] = wgu_ref[...].astype(BF16)
            wd_bf[...] = wd_ref[...].astype(BF16)

        gu = jnp.dot(xbuf[slot], wgu_bf[...], preferred_element_type=F32) + bgu_ref[...]
        gate = jnp.minimum(gu[:, :dff], SWIGLU_LIMIT)
        up = jnp.clip(gu[:, dff:], -SWIGLU_LIMIT, SWIGLU_LIMIT)
        act = (up + 1.0) * gate * _sigmoid(gate * SWIGLU_ALPHA)
        res = jnp.dot(act.astype(BF16), wd_bf[...], preferred_element_type=F32) + bd_ref[...]
        obuf[slot] = (res * wbuf[slot][:, 0:1]).astype(BF16)

        @pl.when(i >= 1)
        def _():
            for cp in scatters(i - 1, 1 - slot):
                cp.wait()

        for cp in scatters(i, slot):
            cp.start()

        @pl.when(i == nact - 1)
        def _():
            for cp in scatters(i, slot):
                cp.wait()


def _moe(blk_e, nact, src, dst, xs, ws, w_gu, b_gu, w_down, b_down, n_blocks):
    ne, d, dff2 = w_gu.shape
    dff = dff2 // 2
    blk = MOE_BLOCK
    grid_spec = pltpu.PrefetchScalarGridSpec(
        num_scalar_prefetch=4,
        grid=(n_blocks,),
        in_specs=[pl.BlockSpec(memory_space=pl.ANY), pl.BlockSpec(memory_space=pl.ANY),
                  pl.BlockSpec((None, d, dff2), lambda i, be, *_: (be[i], 0, 0)),
                  pl.BlockSpec((None, 1, dff2), lambda i, be, *_: (be[i], 0, 0)),
                  pl.BlockSpec((None, dff, d), lambda i, be, *_: (be[i], 0, 0)),
                  pl.BlockSpec((None, 1, d), lambda i, be, *_: (be[i], 0, 0)),
                  pl.BlockSpec(memory_space=pl.ANY)],
        out_specs=pl.BlockSpec(memory_space=pl.ANY),
        scratch_shapes=[pltpu.VMEM((2, blk, d), BF16), pltpu.VMEM((2, blk, LANES), F32),
                        pltpu.VMEM((N_OBUF, blk, d), BF16),
                        pltpu.SemaphoreType.DMA((2,)), pltpu.SemaphoreType.DMA((N_OBUF,)),
                        pltpu.VMEM((d, dff2), BF16), pltpu.VMEM((dff, d), BF16)],
    )
    return pl.pallas_call(
        functools.partial(_moe_kernel, dff=dff),
        grid_spec=grid_spec,
        out_shape=jax.ShapeDtypeStruct(xs.shape, BF16),
        input_output_aliases={10: 0},
        compiler_params=_cparams("arbitrary"),
        name="moe",
    )(blk_e, nact, src, dst, xs, ws, w_gu, b_gu.reshape(ne, 1, dff2), w_down, b_down.reshape(ne, 1, d),
      jnp.zeros(xs.shape, BF16))


def _combine_kernel(pos_ref, ys_ref, x1_ref, gt_ref, o_ref, p_ref, *, n_slots):
    tn = x1_ref.shape[0]
    lane = lax.broadcasted_iota(I32, (tn, SLOT_ROWS), 1)
    pos = [jnp.broadcast_to(pos_ref[:, kk:kk + 1], (tn, SLOT_ROWS)) - lane for kk in range(TOP_K)]

    def cols(c, carry):
        c0 = pl.multiple_of(c * SLOT_ROWS, SLOT_ROWS)
        p = jnp.zeros((tn, SLOT_ROWS), F32)
        for kk in range(TOP_K):
            p = p + jnp.where(pos[kk] == c0, 1.0, 0.0)
        p_ref[:, pl.ds(c0, SLOT_ROWS)] = p.astype(BF16)
        return carry

    lax.fori_loop(0, n_slots // SLOT_ROWS, cols, 0)
    o_ref[...] = x1_ref[...] + gt_ref[...] * jnp.dot(p_ref[...], ys_ref[...], preferred_element_type=F32)


def _combine(pos_t, ys, x1, mod3, tile0, rows, n_slots):
    nb, t, d = x1.shape
    rmod = mod3.shape[1]
    nt = t // rows
    per_tile = ROUTE_TILE // rows
    row = lambda b, i: (b, i, 0)
    gate = pl.BlockSpec((None, rmod, d), (lambda b, i: (b, 0, 5)) if rmod == 1 else (lambda b, i: (b, i, 5)))
    return pl.pallas_call(
        functools.partial(_combine_kernel, n_slots=n_slots),
        grid=(nb, nt),
        in_specs=[pl.BlockSpec((rows, TOP_K), lambda b, i: (b * nt + i, 0)),
                  pl.BlockSpec((n_slots, d), lambda b, i: (tile0 + (b * nt + i) // per_tile, 0)),
                  pl.BlockSpec((None, rows, d), row), gate],
        out_specs=pl.BlockSpec((None, rows, d), row),
        out_shape=jax.ShapeDtypeStruct((nb, t, d), F32),
        scratch_shapes=[pltpu.VMEM((rows, n_slots), BF16)],
        compiler_params=_cparams("parallel", "parallel"),
        name="combine",
    )(pos_t, ys, x1, mod3)


def _rope_tables(pos, n_heads):
    half = HEAD_DIM // 2
    inv_freq = 1.0 / (ROPE_THETA ** (jnp.arange(0, HEAD_DIM, 2, dtype=F32) / HEAD_DIM))
    ang = pos.astype(F32)[:, None] * inv_freq[None, :]
    cos, sin = jnp.cos(ang), jnp.sin(ang)
    del half
    return (jnp.tile(jnp.concatenate([cos, cos], axis=-1), (1, n_heads)),
            jnp.tile(jnp.concatenate([-sin, sin], axis=-1), (1, n_heads)))


def _block_diag(width, value):
    h = np.arange(width) // HEAD_DIM
    return jnp.asarray(np.where(h[:, None] == h[None, :], value, 0.0), F32)


def kernel(x_prompt, x_sample, cache_k, cache_v, state_wkv, state_shift, c_prompt, c_sample, w_ada, b_ada, norm1_g, norm2_g, w_in, q_norm_g, k_norm_g, rwkv_mu, rwkv_w0, rwkv_w2, rwkv_a0, rwkv_a2, rwkv_g2, rwkv_k_k, rwkv_k_a, rwkv_r_k, rwkv_ln_w, rwkv_ln_b, w_out, router_w, router_b, moe_w_gu, moe_b_gu, moe_w_down, moe_b_down):
    nbp, t, d = x_prompt.shape
    nbs, ts, _ = x_sample.shape
    depth = w_ada.shape[0]
    assert depth == 1 and ts == 1
    n_heads = cache_k.shape[3]
    aw = n_heads * HEAD_DIM
    rwid = rwkv_w0.shape[1]
    rwc = rwkv_mu.shape[1]
    past = cache_k.shape[2]
    ne = router_w.shape[2]
    keep = min(MAX_WINDOW, t)
    lyr = 0

    w_in_bf = w_in[lyr].astype(BF16)
    wt_bf = w_out[lyr][:aw].astype(BF16)
    wb_bf = w_out[lyr][aw:].astype(BF16)
    g1 = norm1_g[lyr].reshape(1, d)
    g2 = norm2_g[lyr].reshape(1, d)
    qg = jnp.tile(q_norm_g[lyr], n_heads).reshape(1, aw)
    kg = jnp.tile(k_norm_g[lyr], n_heads).reshape(1, aw)
    bd_mean_a = _block_diag(aw, 1.0 / HEAD_DIM)
    bd_mean_r = _block_diag(rwid, 1.0 / HEAD_DIM)
    bd_ones_r = _block_diag(rwid, 1.0)
    dl = rwkv_w2.shape[1]
    w2p = jnp.zeros((LANES, rwid), F32).at[:dl].set(rwkv_w2[lyr])
    a2p = jnp.zeros((LANES, rwid), F32).at[dl:dl + rwkv_a2.shape[1]].set(rwkv_a2[lyr])
    vec = lambda a: a[lyr].reshape(1, -1)
    rwt = router_w[lyr].T
    rb = router_b[lyr].reshape(ne, 1)

    rows_c = nbp + nbs
    rows_pad = -(-rows_c // 8) * 8
    c_all = jnp.zeros((rows_pad, d), F32).at[:nbp].set(c_prompt).at[nbp:rows_c].set(c_sample)
    mod = _ada(c_all, w_ada[lyr], b_ada[lyr])
    mod_p = mod[:nbp].reshape(nbp, 1, 6 * d)
    mod_s = mod[nbp:rows_c].reshape(1, nbs, 6 * d)
    xs3 = x_sample.reshape(1, nbs, d)

    cos_p, sin_p = _rope_tables(jnp.arange(t), n_heads)
    cos_s, sin_s = _rope_tables(jnp.full((nbs,), PAST_LEN), n_heads)
    qp, kp, vp, rwp, kt_p, vt_p = _inproj(x_prompt, mod_p, g1, w_in_bf, cos_p, sin_p, qg, kg, bd_mean_a,
                                          ROW_TILE, keep=keep)
    qs, ks, vs, rws = _inproj(xs3, mod_s, g1, w_in[lyr], cos_s, sin_s, qg, kg, bd_mean_a, nbs)

    attn_p = _attn_prompt(qp, kp, vp)
    as_rows = lambda a: a.reshape(nbs, 1, -1)
    cache_t = lambda cch: jnp.transpose(cch[lyr], (0, 2, 3, 1)).reshape(nbs, aw, past)
    attn_s = _attn_sample(as_rows(qs), as_rows(ks), as_rows(vs), cache_t(cache_k), cache_t(cache_v))

    rw_args = (vec(rwkv_mu), vec(rwkv_w0), w2p, vec(rwkv_a0), a2p, rwkv_g2[lyr], vec(rwkv_k_k), vec(rwkv_k_a), bd_ones_r)
    pre_p = _rwprep(rwp, jnp.zeros((nbp, 1, rwc), F32), *rw_args, tm=ROW_TILE, whole_prev=False)
    pre_s = _rwprep(rws, state_shift[lyr].reshape(1, nbs, rwc), *rw_args, tm=nbs, whole_prev=True)
    r_p, ld_p, k_p, v_p, al_p, be_p, g_p = pre_p
    y_p, wkv_p = _rwchunk(r_p, ld_p, k_p, v_p, al_p, be_p)
    r_s, ld_s, k_s, v_s, al_s, be_s, g_s = pre_s
    y_s, wkv_s = _rwstep(*(as_rows(a) for a in (r_s, ld_s, k_s, v_s, al_s, be_s)), state_wkv[lyr])
    y_s = y_s.reshape(1, nbs, rwid)

    n_p = nbp * t
    n_valid = n_p + nbs
    assert ROW_TILE == ROUTE_TILE and n_p % ROUTE_TILE == 0 and nbs <= ROUTE_TILE and d == SUBLANES * LANES
    n_tiles = n_p // ROUTE_TILE + 1
    n_slots = _slot_capacity(ne)
    n_rows = n_tiles * n_slots + N_OBUF * MOE_BLOCK
    op_args = (rwkv_r_k[lyr].reshape(1, rwid), vec(rwkv_ln_w), vec(rwkv_ln_b), bd_mean_r, bd_ones_r, rwt, rb)
    x1_p, pos_p, _, cnt_p, *bufs = _outproj(x_prompt, attn_p, y_p, r_p, k_p, v_p, g_p, mod_p, g2, wt_bf, wb_bf,
                                            *op_args, tm=ROW_TILE, n_slots=n_slots, n_rows=n_rows)
    x1_s, pos_s, _, cnt_s, xs, ws = _outproj(xs3, attn_s.reshape(1, nbs, aw), y_s, r_s, k_s, v_s, g_s, mod_s, g2,
                                             w_out[lyr][:aw], w_out[lyr][aw:], *op_args, tm=nbs, n_slots=n_slots,
                                             n_rows=n_rows, tile0=n_p // ROUTE_TILE, bufs=bufs)

    cnt = jnp.concatenate([cnt_p, cnt_s])[:, :, 0].astype(I32)
    per_blk = MOE_BLOCK // MOE_CHUNK
    n_blocks = -(-(n_valid * TOP_K // MOE_CHUNK + n_tiles * ne) // per_blk) + ne
    blk_e, nact, src, dst = _chunk_tables(cnt, n_slots, n_blocks)
    ys = _moe(blk_e, nact, src, dst, xs, ws, moe_w_gu[lyr], moe_b_gu[lyr], moe_w_down[lyr], moe_b_down[lyr], n_blocks)
    y_prompt = _combine(pos_p.T, ys, x1_p, mod_p, 0, ROUTE_TILE, n_slots)
    y_sample = _combine(pos_s.T, ys, x1_s, mod_s, n_p // ROUTE_TILE, nbs, n_slots)

    kept = lambda a: jnp.transpose(a.reshape(nbp, n_heads, HEAD_DIM, keep), (0, 3, 1, 2))[None]
    return (y_prompt, y_sample.reshape(nbs, ts, d), kept(kt_p), kept(vt_p), wkv_p[None], rwp[:, t - 1][None],
            ks.reshape(nbs, ts, n_heads, HEAD_DIM)[None], vs.reshape(nbs, ts, n_heads, HEAD_DIM)[None],
            wkv_s[None], rws.reshape(nbs, rwc)[None])
```

```python
import functools

import numpy as np
import jax
import jax.numpy as jnp
from jax import lax
from jax.experimental import pallas as pl
from jax.experimental.pallas import tpu as pltpu

F32 = jnp.float32
BF16 = jnp.bfloat16
I32 = jnp.int32

HEAD_DIM = 64
LANES = 128
SUBLANES = 8
DILATED_PATTERNS = ((128, 1), (512, 4), (2048, 16))
WINDOW_STEPS = 128
MAX_WINDOW = 2048
PAST_LEN = 16384
ROPE_THETA = 10000.0
NORM_EPS = 1e-6
GN_EPS = 64e-5
TOP_K = 4
SWIGLU_ALPHA = 1.702
SWIGLU_LIMIT = 7.0
RW_CHUNK = 128
RW_PAIRS = 2
MOE_BLOCK = 512
MOE_CHUNK = 16
SLOT_ROWS = 256
ATTN_UNITS = 8
ROW_TILE = 512
ROUTE_TILE = 512
VMEM_LIMIT = 56 * 1024 * 1024
NEG_BIG = -1e30


def _cparams(*sem):
    return pltpu.CompilerParams(dimension_semantics=sem, vmem_limit_bytes=VMEM_LIMIT)


def _dot(a, b):
    return jnp.dot(a.astype(BF16), b.astype(BF16), preferred_element_type=F32)


def _split2(a):
    hi = a.astype(BF16)
    lo = (a - hi.astype(F32)).astype(BF16)
    return hi, lo


def _split3(a):
    hi = a.astype(BF16)
    r1 = a - hi.astype(F32)
    mid = r1.astype(BF16)
    lo = (r1 - mid.astype(F32)).astype(BF16)
    return hi, mid, lo


def _dot_x(a, e):
    e = e.astype(BF16)
    hi, mid, lo = _split3(a)
    return (jnp.dot(hi, e, preferred_element_type=F32) + jnp.dot(mid, e, preferred_element_type=F32)
            + jnp.dot(lo, e, preferred_element_type=F32))


def _xdot(e, a):
    e = e.astype(BF16)
    hi, mid, lo = _split3(a)
    return (jnp.dot(e, hi, preferred_element_type=F32) + jnp.dot(e, mid, preferred_element_type=F32)
            + jnp.dot(e, lo, preferred_element_type=F32))


def _dot3(a, b):
    ah, al = _split2(a)
    bh, bl = _split2(b)
    return (jnp.dot(ah, bh, preferred_element_type=F32) + jnp.dot(ah, bl, preferred_element_type=F32)
            + jnp.dot(al, bh, preferred_element_type=F32))


def _dot3_nt(a, b):
    ah, al = _split2(a)
    bh, bl = _split2(b)
    dn = (((1,), (1,)), ((), ()))
    return (lax.dot_general(ah, bh, dn, preferred_element_type=F32)
            + lax.dot_general(ah, bl, dn, preferred_element_type=F32)
            + lax.dot_general(al, bh, dn, preferred_element_type=F32))


def _mm(a, w):
    if w.dtype == BF16:
        return jnp.dot(a.astype(BF16), w, preferred_element_type=F32)
    return _dot3(a, w)


def _sigmoid(x):
    return 1.0 / (1.0 + jnp.exp(-x))


def _ada_kernel(c_ref, w_ref, b_ref, o_ref):
    c = c_ref[...]
    o_ref[...] = _dot3(c * _sigmoid(c), w_ref[...]) + b_ref[...]


def _ada(c_all, w_ada, b_ada):
    rows, d = c_all.shape
    n = w_ada.shape[1]
    tn = n // 4
    return pl.pallas_call(
        _ada_kernel,
        grid=(n // tn,),
        in_specs=[pl.BlockSpec((rows, d), lambda j: (0, 0)),
                  pl.BlockSpec((d, tn), lambda j: (0, j)),
                  pl.BlockSpec((1, tn), lambda j: (0, j))],
        out_specs=pl.BlockSpec((rows, tn), lambda j: (0, j)),
        out_shape=jax.ShapeDtypeStruct((rows, n), F32),
        compiler_params=_cparams("arbitrary"),
        name="ada",
    )(c_all, w_ada, b_ada.reshape(1, n))


def _inproj_kernel(x_ref, sh_ref, sc_ref, g_ref, w_ref, cos_ref, sin_ref, qg_ref, kg_ref, bd_ref,
                   q_ref, k_ref, v_ref, rw_ref, *maybe_kv_t, aw, first_kept):
    x = x_ref[...]
    ms = jnp.mean(x * x, axis=-1, keepdims=True)
    h = x * lax.rsqrt(ms + NORM_EPS) * g_ref[...] * (1.0 + sc_ref[...]) + sh_ref[...]
    proj = _mm(h, w_ref[...])
    cos = cos_ref[...]
    sin = sin_ref[...]
    lane = lax.broadcasted_iota(I32, (1, aw), 1)
    first_half = (lane % HEAD_DIM) < (HEAD_DIM // 2)
    bd = bd_ref[...]

    def norm_rope(t, g):
        tn = t * lax.rsqrt((_dot if w_ref.dtype == BF16 else _dot_x)(t * t, bd) + NORM_EPS) * g
        rot = jnp.where(first_half, pltpu.roll(tn, aw - HEAD_DIM // 2, 1), pltpu.roll(tn, HEAD_DIM // 2, 1))
        return tn * cos + rot * sin

    scale = 1.0 / np.sqrt(HEAD_DIM).astype(np.float32)
    q_ref[...] = norm_rope(proj[:, :aw], qg_ref[...]) * scale
    k = norm_rope(proj[:, aw:2 * aw], kg_ref[...])
    v = proj[:, 2 * aw:3 * aw]
    k_ref[...] = k
    v_ref[...] = v
    rw_ref[...] = proj[:, 3 * aw:]
    if maybe_kv_t:
        kt_ref, vt_ref = maybe_kv_t
        kept = pl.program_id(1) >= first_kept

        @pl.when(kept)
        def _():
            kt_ref[...] = k.T
            vt_ref[...] = v.T

        @pl.when(jnp.logical_not(kept))
        def _():
            kt_ref[...] = jnp.zeros_like(kt_ref)
            vt_ref[...] = jnp.zeros_like(vt_ref)


def _inproj(x3, mod3, g1, w_in_bf, cos_t, sin_t, qg, kg, bd_mean, tm, keep=0):
    nb, t, d = x3.shape
    r = mod3.shape[1]
    ncol = w_in_bf.shape[1]
    aw = cos_t.shape[1]
    rwc = ncol - 3 * aw
    grid = (nb, t // tm)
    row = lambda b, i: (b, i, 0)
    const = lambda b, i: (0, 0)
    mod_spec = lambda s: pl.BlockSpec((None, r, d), (lambda b, i: (b, 0, s)) if r == 1 else (lambda b, i: (b, i, s)))
    out_specs = [pl.BlockSpec((None, tm, aw), row)] * 3 + [pl.BlockSpec((None, tm, rwc), row)]
    out_shape = [jax.ShapeDtypeStruct((nb, t, aw), F32)] * 3 + [jax.ShapeDtypeStruct((nb, t, rwc), F32)]
    first_kept = (t - keep) // tm
    if keep:
        assert keep % tm == 0 and (t - keep) % tm == 0
        kept_spec = pl.BlockSpec((None, aw, tm), lambda b, i: (b, 0, jnp.maximum(i - first_kept, 0)))
        out_specs += [kept_spec, kept_spec]
        out_shape += [jax.ShapeDtypeStruct((nb, aw, keep), F32)] * 2
    outs = pl.pallas_call(
        functools.partial(_inproj_kernel, aw=aw, first_kept=first_kept),
        grid=grid,
        in_specs=[pl.BlockSpec((None, tm, d), row), mod_spec(0), mod_spec(1),
                  pl.BlockSpec((1, d), const), pl.BlockSpec((d, ncol), const),
                  pl.BlockSpec((tm, aw), lambda b, i: (i, 0)), pl.BlockSpec((tm, aw), lambda b, i: (i, 0)),
                  pl.BlockSpec((1, aw), const), pl.BlockSpec((1, aw), const), pl.BlockSpec((aw, aw), const)],
        out_specs=out_specs,
        out_shape=out_shape,
        compiler_params=_cparams("parallel", "arbitrary"),
        name="inproj",
    )(x3, mod3, mod3, g1, w_in_bf, cos_t, sin_t, qg, kg, bd_mean)
    return outs


def _attn_prompt_kernel(q_ref, k_ref, v_ref, o_ref, kp_ref, vp_ref, m_ref, l_ref, acc_ref, *, t, pad):
    nq = WINDOW_STEPS
    nk = 2 * WINDOW_STEPS
    kp_ref[pl.ds(0, pad), :] = jnp.zeros((pad, LANES), F32)
    vp_ref[pl.ds(0, pad), :] = jnp.zeros((pad, LANES), F32)
    kp_ref[pl.ds(pad, t), :] = k_ref[...]
    vp_ref[pl.ds(pad, t), :] = v_ref[...]
    head0 = lax.broadcasted_iota(I32, (1, LANES), 1) < HEAD_DIM
    qi = lax.broadcasted_iota(I32, (nq, nk), 0)
    kj = lax.broadcasted_iota(I32, (nq, nk), 1)
    steps_back = qi + nq - kj
    band = (steps_back >= 0) & (steps_back <= WINDOW_STEPS)
    has_past = kj >= nq

    hsels = (head0, jnp.logical_not(head0))
    nt = (((1,), (1,)), ((), ()))

    for p, (_, d) in enumerate(DILATED_PATTERNS):
        def units(g, carry, p=p, d=d):
            rows_q, kb, vb, valid, qh = [], [], [], [], []
            for j in range(ATTN_UNITS):
                u = g * ATTN_UNITS + j
                res = u % d
                blk = u // d
                q_start = res + d * nq * blk
                k_start = pad + q_start - d * nq
                if d == 1:
                    rows_q.append(pl.ds(q_start, nq))
                    rows_k = pl.ds(k_start, nk)
                else:
                    rows_q.append(pl.ds(q_start, nq, stride=d))
                    rows_k = pl.ds(k_start, nk, stride=d)
                q = q_ref[rows_q[j], :]
                kb.append(kp_ref[rows_k, :].astype(BF16))
                vb.append(vp_ref[rows_k, :].astype(BF16))
                valid.append(band & (has_past | (blk > 0)))
                qh.append([jnp.where(hsel, q, 0.0).astype(BF16) for hsel in hsels])
            chains = [(j, h) for j in range(ATTN_UNITS) for h in range(2)]
            s = [lax.dot_general(qh[j][h], kb[j], nt, preferred_element_type=F32) for j, h in chains]
            s = [jnp.where(valid[j], sc, NEG_BIG) for (j, h), sc in zip(chains, s)]
            mx = [jnp.max(sc, axis=-1, keepdims=True) for sc in s]
            e = [jnp.exp(sc - m) for sc, m in zip(s, mx)]
            den = [jnp.sum(ec, axis=-1, keepdims=True) for ec in e]
            o = [jnp.dot(ec.astype(BF16), vb[j], preferred_element_type=F32) for (j, h), ec in zip(chains, e)]
            for j in range(ATTN_UNITS):
                m_ref[p, rows_q[j], :] = jnp.where(head0, mx[2 * j], mx[2 * j + 1])
                l_ref[p, rows_q[j], :] = jnp.where(head0, den[2 * j], den[2 * j + 1])
                acc_ref[p, rows_q[j], :] = jnp.where(head0, o[2 * j], o[2 * j + 1])
            return carry

        lax.fori_loop(0, t // nq // ATTN_UNITS, units, 0)

    rows = 256

    def merge(i, carry):
        sl = pl.ds(pl.multiple_of(i * rows, rows), rows)
        m0, m1, m2 = m_ref[0, sl, :], m_ref[1, sl, :], m_ref[2, sl, :]
        mm = jnp.maximum(jnp.maximum(m0, m1), m2)
        w0, w1, w2 = jnp.exp(m0 - mm), jnp.exp(m1 - mm), jnp.exp(m2 - mm)
        num = w0 * acc_ref[0, sl, :] + w1 * acc_ref[1, sl, :] + w2 * acc_ref[2, sl, :]
        den = w0 * l_ref[0, sl, :] + w1 * l_ref[1, sl, :] + w2 * l_ref[2, sl, :]
        o_ref[sl, :] = num / den
        return carry

    lax.fori_loop(0, t // rows, merge, 0)


def _attn_prompt(q, k, v):
    nb, t, aw = q.shape
    pad = MAX_WINDOW
    assert t % MAX_WINDOW == 0
    spec = pl.BlockSpec((None, t, LANES), lambda b, hp: (b, 0, hp))
    return pl.pallas_call(
        functools.partial(_attn_prompt_kernel, t=t, pad=pad),
        grid=(nb, aw // LANES),
        in_specs=[spec, spec, spec],
        out_specs=spec,
        out_shape=jax.ShapeDtypeStruct((nb, t, aw), F32),
        scratch_shapes=[pltpu.VMEM((pad + t, LANES), F32), pltpu.VMEM((pad + t, LANES), F32),
                        pltpu.VMEM((3, t, LANES), F32), pltpu.VMEM((3, t, LANES), F32),
                        pltpu.VMEM((3, t, LANES), F32)],
        compiler_params=_cparams("parallel", "parallel"),
        name="attn_prompt",
    )(q, k, v)


def _attn_sample_kernel(q_ref, kn_ref, vn_ref, kt_ref, vt_ref, o_ref, *, nh, w):
    aw = nh * HEAD_DIM
    dist = w - lax.broadcasted_iota(I32, (1, w), 1)
    mult = jnp.zeros((1, w), F32)
    for win, d in DILATED_PATTERNS:
        mult = mult + jnp.where((dist % d == 0) & (dist <= win), 1.0, 0.0)
    n_pat = float(len(DILATED_PATTERNS))
    q_col = _col(q_ref[...], aw)
    kn_col = _col(kn_ref[...], aw)
    vn_col = _col(vn_ref[...], aw)
    heads = range(nh)
    hs = [pl.ds(h * HEAD_DIM, HEAD_DIM) for h in heads]
    cut = lambda col, h: col[h * HEAD_DIM:(h + 1) * HEAD_DIM]
    s = [jnp.sum(kt_ref[hs[h], :] * cut(q_col, h), axis=0, keepdims=True) for h in heads]
    s_self = [jnp.sum(cut(q_col, h) * cut(kn_col, h), axis=0, keepdims=True) for h in heads]
    s = [jnp.where(mult > 0.0, sh, NEG_BIG) for sh in s]
    mx = [jnp.maximum(jnp.max(sh, axis=1, keepdims=True), ss) for sh, ss in zip(s, s_self)]
    pr = [mult * jnp.exp(sh - m) for sh, m in zip(s, mx)]
    p_self = [n_pat * jnp.exp(ss - m) for ss, m in zip(s_self, mx)]
    den = [jnp.sum(p, axis=1, keepdims=True) + ps for p, ps in zip(pr, p_self)]
    num = [jnp.sum(vt_ref[hs[h], :] * pr[h], axis=1, keepdims=True) + p_self[h] * cut(vn_col, h) for h in heads]
    outs = [n / dn for n, dn in zip(num, den)]
    o_ref[...] = _row(jnp.concatenate(outs, axis=0), aw)


def _attn_sample(q, kn, vn, cache_kt, cache_vt):
    nb, _, aw = q.shape
    w = cache_kt.shape[2]
    assert w == MAX_WINDOW
    row = pl.BlockSpec((None, 1, aw), lambda i: (i, 0, 0))
    mat = pl.BlockSpec((None, aw, w), lambda i: (i, 0, 0))
    return pl.pallas_call(
        functools.partial(_attn_sample_kernel, nh=aw // HEAD_DIM, w=w),
        grid=(nb,),
        in_specs=[row, row, row, mat, mat],
        out_specs=row,
        out_shape=jax.ShapeDtypeStruct((nb, 1, aw), F32),
        compiler_params=_cparams("parallel"),
        name="attn_sample",
    )(q, kn, vn, cache_kt, cache_vt)


def _rwprep_kernel(p_ref, prev_ref, first_ref, mu_ref, w0_ref, w2_ref, a0_ref, a2_ref, g2_ref, kk_ref, ka_ref,
                   bd_ref, r_ref, ld_ref, k_ref, v_ref, al_ref, be_ref, g_ref, *, rwid, whole_prev):
    p = p_ref[...]
    if whole_prev:
        prev = first_ref[...]
    else:
        tm = p.shape[0]
        before = jnp.where(pl.program_id(1) == 0, first_ref[...], prev_ref[7:8, :])
        rowi = lax.broadcasted_iota(I32, (tm, 1), 0)
        prev = jnp.where(rowi == 0, before, pltpu.roll(p, 1, 0))
    xs = p + mu_ref[...] * (prev - p)
    r = xs[:, :rwid]
    k = xs[:, rwid:2 * rwid]
    v = xs[:, 2 * rwid:3 * rwid]
    xwa = xs[:, 3 * rwid:3 * rwid + LANES]
    xg = xs[:, 3 * rwid + LANES:]
    z = w0_ref[...] + _dot3(jnp.tanh(xwa), w2_ref[...])
    softplus_neg = jnp.maximum(-z, 0.0) + jnp.log(1.0 + jnp.exp(-jnp.abs(z)))
    w = -softplus_neg - 0.5
    a = _sigmoid(a0_ref[...] + _dot3(xwa, a2_ref[...]))
    g = _dot3(_sigmoid(xg), g2_ref[...])
    kk = k * kk_ref[...]
    norm = jnp.sqrt(_dot_x(kk * kk, bd_ref[...]))
    kk = kk / jnp.maximum(norm, 1e-12)
    r_ref[...] = r
    ld_ref[...] = -jnp.exp(w)
    k_ref[...] = k * (1.0 + (a - 1.0) * ka_ref[...])
    v_ref[...] = v
    al_ref[...] = -kk
    be_ref[...] = kk * a
    g_ref[...] = g


def _rwprep(rw, first, mu, w0, w2p, a0, a2p, g2, k_k, k_a, bd_ones, tm, whole_prev):
    nb, t, rwc = rw.shape
    rwid = w0.shape[1]
    grid = (nb, t // tm)
    row = lambda b, i: (b, i, 0)
    const = lambda b, i: (0, 0)
    if whole_prev:
        prev_spec = pl.BlockSpec((None, tm, rwc), row)
        first_spec = pl.BlockSpec((None, tm, rwc), row)
    else:
        prev_spec = pl.BlockSpec((None, 8, rwc), lambda b, i: (b, jnp.maximum(i * (tm // 8) - 1, 0), 0))
        first_spec = pl.BlockSpec((None, 1, rwc), lambda b, i: (b, 0, 0))
    vec = pl.BlockSpec((1, rwid), const)
    out = pl.BlockSpec((None, tm, rwid), row)
    return pl.pallas_call(
        functools.partial(_rwprep_kernel, rwid=rwid, whole_prev=whole_prev),
        grid=grid,
        in_specs=[pl.BlockSpec((None, tm, rwc), row), prev_spec, first_spec,
                  pl.BlockSpec((1, rwc), const), vec, pl.BlockSpec((LANES, rwid), const),
                  vec, pl.BlockSpec((LANES, rwid), const), pl.BlockSpec((LANES, rwid), const), vec, vec,
                  pl.BlockSpec((rwid, rwid), const)],
        out_specs=[out] * 7,
        out_shape=[jax.ShapeDtypeStruct((nb, t, rwid), F32)] * 7,
        compiler_params=_cparams("parallel", "parallel"),
        name="rwprep",
    )(rw, rw, first, mu, w0, w2p, a0, a2p, g2, k_k, k_a, bd_ones)


def _rwchunk_kernel(r_ref, ld_ref, k_ref, v_ref, al_ref, be_ref, y_ref, st_ref, z_ref, *, nb, npb):
    c = RW_CHUNK
    ci = pl.program_id(1)

    @pl.when(ci == 0)
    def _():
        z_ref[...] = jnp.zeros_like(z_ref)

    ti = lax.broadcasted_iota(I32, (c, c), 0)
    si = lax.broadcasted_iota(I32, (c, c), 1)
    low_incl = si <= ti
    low_strict = si < ti
    diag = si == ti
    tri = jnp.where(low_incl, 1.0, 0.0).astype(BF16)
    eye = jnp.where(diag, 1.0, 0.0)
    head0 = lax.broadcasted_iota(I32, (1, LANES), 1) < HEAD_DIM
    hsels = (head0, jnp.logical_not(head0))
    same_head = (ti < HEAD_DIM) == (si < HEAD_DIM)
    nt = (((1,), (1,)), ((), ()))
    batches = range(nb * npb)
    chains = [(b, h) for b in batches for h in range(2)]
    lanes_of = lambda b: pl.ds((b // nb) * LANES, LANES)
    r_v, ld_v, k_v, v_v, al_v, be_v = ([ref[b % nb, :, lanes_of(b)] for b in batches]
                                       for ref in (r_ref, ld_ref, k_ref, v_ref, al_ref, be_ref))

    cum = [_xdot(tri, ld_v[b]) for b in batches]
    tot = [cm[c - 1:c, :] for cm in cum]
    e_neg = [jnp.exp(-cm) for cm in cum]
    at = [al_v[b] * jnp.exp(cum[b] - ld_v[b]) for b in batches]
    rt = [r_v[b] * jnp.exp(cum[b]) for b in batches]
    rhs_t = [jnp.concatenate([be_v[b] * e_neg[b], k_v[b] * e_neg[b]], axis=0).astype(BF16) for b in batches]
    vb = [v_v[b].astype(BF16) for b in batches]
    at_h = [jnp.where(hsels[h], at[b], 0.0) for b, h in chains]
    rt_h = [jnp.where(hsels[h], rt[b], 0.0) for b, h in chains]
    a4 = [lax.dot_general(jnp.concatenate([a, r], axis=0).astype(BF16), rhs_t[b], nt, preferred_element_type=F32)
          for (b, h), a, r in zip(chains, at_h, rt_h)]
    a_ab = [jnp.where(low_strict, m[:c, :c], 0.0) for m in a4]
    a_ak = [jnp.where(low_strict, m[:c, c:], 0.0).astype(BF16) for m in a4]
    a_r = [jnp.concatenate([jnp.where(low_incl, m[c:, :c], 0.0), jnp.where(low_incl, m[c:, c:], 0.0)],
                           axis=1).astype(BF16) for m in a4]
    levels = int(np.log2(c))
    pw = [m.astype(BF16) for m in a_ab]
    pw = [jnp.dot(m, m, preferred_element_type=F32).astype(BF16) for m in pw]
    inv = [eye + m for m in a_ab]
    for _ in range(1, levels - 1):
        x2 = [jnp.dot(m, jnp.concatenate([m, i.astype(BF16)], axis=1), preferred_element_type=F32)
              for m, i in zip(pw, inv)]
        pw = [m[:, :c].astype(BF16) for m in x2]
        inv = [i + m[:, c:] for i, m in zip(inv, x2)]
    inv = [i + jnp.dot(m, i.astype(BF16), preferred_element_type=F32) for i, m in zip(inv, pw)]
    akv = [jnp.dot(m, vb[b], preferred_element_type=F32) for (b, h), m in zip(chains, a_ak)]
    x = [_dot(i, jnp.concatenate([kv, a], axis=1)) for i, kv, a in zip(inv, akv, at_h)]
    u0_h = [m[:, :LANES] for m in x]
    at2_h = [m[:, LANES:] for m in x]
    y0_h = [jnp.dot(ar, jnp.concatenate([u0, v_v[b]], axis=0).astype(BF16), preferred_element_type=F32)
            for (b, h), ar, u0 in zip(chains, a_r, u0_h)]
    rt2_h = [r + jnp.dot(ar[:, :c], a2.astype(BF16), preferred_element_type=F32)
             for r, ar, a2 in zip(rt_h, a_r, at2_h)]
    z = [z_ref[b] for b in batches]
    uy = [_dot(jnp.concatenate([at2_h[2 * b] + at2_h[2 * b + 1], rt2_h[2 * b] + rt2_h[2 * b + 1]], axis=0), z[b])
          for b in batches]
    u = [uy[b][:c] + jnp.where(head0, u0_h[2 * b], u0_h[2 * b + 1]) for b in batches]
    for b in batches:
        y_ref[b % nb, :, lanes_of(b)] = uy[b][c:] + jnp.where(head0, y0_h[2 * b], y0_h[2 * b + 1])
    e_end = [jnp.exp(tot[b] - cum[b]) for b in batches]
    lhs_t = [jnp.concatenate([be_v[b] * e_end[b], k_v[b] * e_end[b]], axis=0) for b in batches]
    zadd = [_dot(lhs_t[b].T, jnp.concatenate([u[b], v_v[b]], axis=0)) for b in batches]
    for b in batches:
        dcol = jnp.sum(jnp.where(diag, jnp.broadcast_to(jnp.exp(tot[b]), (c, c)), 0.0), axis=1, keepdims=True)
        z_ref[b] = dcol * z[b] + jnp.where(same_head, zadd[b], 0.0)

    @pl.when(ci == pl.num_programs(1) - 1)
    def _():
        for b in batches:
            s = z_ref[b].T
            st_ref[b % nb, 2 * (b // nb)] = s[:HEAD_DIM, :HEAD_DIM]
            st_ref[b % nb, 2 * (b // nb) + 1] = s[HEAD_DIM:, HEAD_DIM:]


def _rwchunk(r, ld, k, v, al, be):
    nb, t, rwid = r.shape
    c = RW_CHUNK
    npb = RW_PAIRS
    assert t % c == 0 and c == LANES and rwid % (npb * LANES) == 0
    ngrp = rwid // (npb * LANES)
    seq = pl.BlockSpec((nb, c, npb * LANES), lambda hp, ci: (0, ci, hp))
    return pl.pallas_call(
        functools.partial(_rwchunk_kernel, nb=nb, npb=npb),
        grid=(ngrp, t // c),
        in_specs=[seq] * 6,
        out_specs=[seq, pl.BlockSpec((nb, 2 * npb, HEAD_DIM, HEAD_DIM), lambda hp, ci: (0, hp, 0, 0))],
        out_shape=[jax.ShapeDtypeStruct((nb, t, rwid), F32),
                   jax.ShapeDtypeStruct((nb, 2 * npb * ngrp, HEAD_DIM, HEAD_DIM), F32)],
        scratch_shapes=[pltpu.VMEM((nb * npb, LANES, LANES), F32)],
        compiler_params=_cparams("parallel", "arbitrary"),
        name="rwchunk",
    )(r, ld, k, v, al, be)


def _col(row, width):
    return jnp.broadcast_to(row, (LANES, width)).T[:, 0:1]


def _row(col, width):
    return jnp.broadcast_to(col, (width, LANES)).T[0:1, :]


def _rwstep_kernel(r_ref, ld_ref, k_ref, v_ref, al_ref, be_ref, s_ref, y_ref, so_ref, *, bb, nh):
    rwid = nh * HEAD_DIM

    def one(b, carry):
        r, dcy, k, al, be = (ref[b] for ref in (r_ref, ld_ref, k_ref, al_ref, be_ref))
        dcy = jnp.exp(dcy)
        v_col = _col(v_ref[b], rwid)
        heads = range(nh)
        hs = [slice(h * HEAD_DIM, (h + 1) * HEAD_DIM) for h in heads]
        st = [s_ref[b, h] for h in heads]
        sa = [jnp.sum(st[h] * al[:, hs[h]], axis=1, keepdims=True) for h in heads]
        st = [st[h] * dcy[:, hs[h]] + sa[h] * be[:, hs[h]] + v_col[hs[h], :] * k[:, hs[h]] for h in heads]
        for h in heads:
            so_ref[b, h] = st[h]
        ys = [jnp.sum(st[h] * r[:, hs[h]], axis=1, keepdims=True) for h in heads]
        y_ref[b] = _row(jnp.concatenate(ys, axis=0), rwid)
        return carry

    lax.fori_loop(0, bb, one, 0)


def _rwstep(r, ld, k, v, al, be, state):
    nb, _, rwid = r.shape
    nh = state.shape[1]
    bb = 8
    row = pl.BlockSpec((bb, 1, rwid), lambda i: (i, 0, 0))
    st = pl.BlockSpec((bb, nh, HEAD_DIM, HEAD_DIM), lambda i: (i, 0, 0, 0))
    return pl.pallas_call(
        functools.partial(_rwstep_kernel, bb=bb, nh=nh),
        grid=(nb // bb,),
        in_specs=[row] * 6 + [st],
        out_specs=[row, st],
        out_shape=[jax.ShapeDtypeStruct((nb, 1, rwid), F32), jax.ShapeDtypeStruct(state.shape, F32)],
        compiler_params=_cparams("parallel"),
        name="rwstep",
    )(r, ld, k, v, al, be, state)


def _outproj_kernel(x_ref, at_ref, y_ref, r_ref, k_ref, v_ref, g_ref, gt_ref, sh_ref, sc_ref, g2_ref,
                    wt_ref, wb_ref, rk_ref, lw_ref, lb_ref, bdm_ref, bd1_ref, rwt_ref, rb_ref, *rest, n_slots):
    x1_ref, pos_ref, wt4_ref, cnt_ref, xs_ref, ws_ref, hb_ref = rest[-7:]
    y = y_ref[...]
    bdm = bdm_ref[...]
    mean = _dot_x(y, bdm)
    yc = y - mean
    var = _dot_x(yc * yc, bdm)
    yn = yc * lax.rsqrt(var + GN_EPS) * lw_ref[...] + lb_ref[...]
    v = v_ref[...]
    bonus = _dot_x(r_ref[...] * k_ref[...] * rk_ref[...], bd1_ref[...]) * v
    rw = (yn + bonus) * g_ref[...]
    mix = _mm(at_ref[...], wt_ref[...]) + _mm(rw, wb_ref[...])
    x1 = x_ref[...] + gt_ref[...] * mix
    x1_ref[...] = x1
    ms = jnp.mean(x1 * x1, axis=-1, keepdims=True)
    h2 = x1 * lax.rsqrt(ms + NORM_EPS) * g2_ref[...] * (1.0 + sc_ref[...]) + sh_ref[...]
    pos, wts, cnt = _route_tile(_dot3_nt(rwt_ref[...], h2) + rb_ref[...])
    for kk in range(TOP_K):
        pos_ref[pl.ds(kk, 1), :] = pos[kk]
        wt4_ref[pl.ds(kk, 1), :] = wts[kk]
    cnt_ref[...] = jnp.broadcast_to(cnt, cnt_ref.shape)
    hb_ref[...] = h2.astype(BF16)
    _sort_tile(pos, wts, hb_ref, xs_ref, ws_ref, n_slots)


def _outproj(x3, attn, y, r, k, v, g, mod3, g2, wt_bf, wb_bf, r_k, ln_w, ln_b, bd_mean, bd_ones, rwt, rb, tm,
             n_slots, n_rows, tile0=0, bufs=None):
    nb, t, d = x3.shape
    rmod = mod3.shape[1]
    aw = attn.shape[2]
    ne = rwt.shape[0]
    nt = t // tm
    grid = (nb, nt)
    row = lambda b, i: (b, i, 0)
    const = lambda b, i: (0, 0)
    tile = lambda b, i: (tile0 + b * nt + i, 0)
    mod_spec = lambda s: pl.BlockSpec((None, rmod, d), (lambda b, i: (b, 0, s)) if rmod == 1 else (lambda b, i: (b, i, s)))
    half = pl.BlockSpec((None, tm, aw), row)
    vec = pl.BlockSpec((1, aw), const)
    tok4 = pl.BlockSpec((TOP_K, tm), lambda b, i: (0, b * nt + i))
    bufs = () if bufs is None else tuple(bufs)
    n_in = 20
    return pl.pallas_call(
        functools.partial(_outproj_kernel, n_slots=n_slots),
        grid=grid,
        in_specs=[pl.BlockSpec((None, tm, d), row)] + [half] * 6 + [mod_spec(2), mod_spec(3), mod_spec(4),
                  pl.BlockSpec((1, d), const), pl.BlockSpec((aw, d), const), pl.BlockSpec((aw, d), const),
                  vec, vec, vec, pl.BlockSpec((aw, aw), const), pl.BlockSpec((aw, aw), const),
                  pl.BlockSpec((ne, d), const), pl.BlockSpec((ne, 1), const)]
                 + [pl.BlockSpec(memory_space=pl.ANY)] * len(bufs),
        out_specs=[pl.BlockSpec((None, tm, d), row), tok4, tok4,
                   pl.BlockSpec((None, ne, LANES), lambda b, i: (b * nt + i, 0, 0)),
                   pl.BlockSpec((n_slots, d), tile), pl.BlockSpec((n_slots, LANES), tile)],
        out_shape=[jax.ShapeDtypeStruct((nb, t, d), F32), jax.ShapeDtypeStruct((TOP_K, nb * t), I32),
                   jax.ShapeDtypeStruct((TOP_K, nb * t), F32), jax.ShapeDtypeStruct((nb * nt, ne, LANES), F32),
                   jax.ShapeDtypeStruct((n_rows, d), BF16), jax.ShapeDtypeStruct((n_rows, LANES), F32)],
        scratch_shapes=[pltpu.VMEM((tm, d), BF16)],
        input_output_aliases={n_in: 4, n_in + 1: 5} if bufs else {},
        compiler_params=_cparams("parallel", "parallel"),
        name="outproj",
    )(x3, attn, y, r, k, v, g, mod3, mod3, mod3, g2, wt_bf, wb_bf, r_k, ln_w, ln_b, bd_mean, bd_ones, rwt, rb, *bufs)


def _route_tile(lg):
    ne, tn = lg.shape
    eidx = lax.broadcasted_iota(I32, (ne, tn), 0).astype(F32)
    vals, hots = [], []
    for _ in range(TOP_K):
        mx = jnp.max(lg, axis=0, keepdims=True)
        pick = jnp.min(jnp.where(lg == mx, eidx, float(ne)), axis=0, keepdims=True)
        hot = eidx == pick
        vals.append(mx)
        hots.append(hot)
        lg = jnp.where(hot, -jnp.inf, lg)
    ex = [jnp.exp(vv - vals[0]) for vv in vals]
    den = ex[0] + ex[1] + ex[2] + ex[3]
    wts = [e / den for e in ex]
    hot_all = jnp.zeros((ne, tn), F32)
    for hot in hots:
        hot_all = hot_all + jnp.where(hot, 1.0, 0.0)
    ri = lax.broadcasted_iota(I32, (tn, tn), 0)
    cj = lax.broadcasted_iota(I32, (tn, tn), 1)
    upper = jnp.where(ri <= cj, 1.0, 0.0).astype(BF16)
    before = jnp.dot(hot_all.astype(BF16), upper, preferred_element_type=F32) - hot_all
    cnt = jnp.sum(hot_all, axis=1, keepdims=True)
    padded = jnp.ceil(cnt / MOE_CHUNK) * MOE_CHUNK
    er = lax.broadcasted_iota(I32, (ne, ne), 0)
    ec = lax.broadcasted_iota(I32, (ne, ne), 1)
    lower_strict = jnp.where(ec < er, 1.0, 0.0)
    off = _xdot(lower_strict, jnp.broadcast_to(padded, (ne, LANES)))[:, 0:1]
    pos = [jnp.sum(jnp.where(hot, off + before, 0.0), axis=0, keepdims=True).astype(I32) for hot in hots]
    return pos, wts, cnt


def _slot_capacity(ne):
    return -(-(TOP_K * ROUTE_TILE + ne * (MOE_CHUNK - 1)) // SLOT_ROWS) * SLOT_ROWS


def _sort_tile(pos, wts, h_ref, xs_ref, ws_ref, n_slots):
    tn = h_ref.shape[0]

    def rows(c, carry):
        h = h_ref[...]
        r0 = pl.multiple_of(c * SLOT_ROWS, SLOT_ROWS)
        slot = r0 + lax.broadcasted_iota(I32, (SLOT_ROWS, tn), 0)
        pw = jnp.zeros((SLOT_ROWS, tn), F32)
        for kk in range(TOP_K):
            pw = pw + jnp.where(slot == pos[kk], wts[kk], 0.0)
        p = jnp.where(pw > 0.0, 1.0, 0.0)
        xs_ref[pl.ds(r0, SLOT_ROWS), :] = jnp.dot(p.astype(BF16), h, preferred_element_type=F32).astype(BF16)
        ws_ref[pl.ds(r0, SLOT_ROWS), :] = jnp.broadcast_to(jnp.sum(pw, axis=1, keepdims=True), (SLOT_ROWS, LANES))
        return carry

    lax.fori_loop(0, n_slots // SLOT_ROWS, rows, 0)


def _chunk_tables(cnt, n_slots, n_blocks):
    nt, ne = cnt.shape
    per_blk = MOE_BLOCK // MOE_CHUNK
    nch = -(-cnt // MOE_CHUNK)
    seg0 = (jnp.cumsum(nch, axis=1) - nch) + (jnp.arange(nt) * (n_slots // MOE_CHUNK))[:, None]
    cum_t = jnp.cumsum(nch, axis=0)
    total = cum_t[-1]
    blocks = -(-total // per_blk)
    blk_end = jnp.cumsum(blocks)
    blk = jnp.arange(n_blocks)
    blk_e = jnp.minimum(jnp.sum(blk_end[None, :] <= blk[:, None], axis=1), ne - 1).astype(I32)
    nact = blk_end[-1].reshape(1).astype(I32)
    pick_e = blk_e[:, None] == jnp.arange(ne)[None, :]
    of_e = lambda tab: jnp.sum(jnp.where(pick_e, tab[None, :], 0), axis=1)
    col_e = lambda tab: jnp.sum(jnp.where(pick_e[:, None, :], tab[None, :, :], 0), axis=2)
    q = ((blk - of_e(blk_end - blocks)) * per_blk)[:, None] + jnp.arange(per_blk)[None, :]
    live = (q < of_e(total)[:, None]) & (blk < nact[0])[:, None]
    t_n = jnp.minimum(jnp.sum(col_e(cum_t)[:, None, :] <= q[:, :, None], axis=2), nt - 1)
    pick_t = t_n[:, :, None] == jnp.arange(nt)[None, None, :]
    of_t = lambda tab: jnp.sum(jnp.where(pick_t, tab[:, None, :], 0), axis=2)
    src = jnp.where(live, of_t(col_e(seg0)) + q - of_t(col_e(cum_t - nch)), 0)
    n = blk[:, None] * per_blk + jnp.arange(per_blk)[None, :]
    spare = nt * (n_slots // MOE_CHUNK) + n % (N_OBUF * per_blk)
    dst = jnp.where(live, src, spare)
    return blk_e, nact, src.reshape(-1).astype(I32), dst.reshape(-1).astype(I32)


N_OBUF = 2


def _moe_kernel(be_ref, nact_ref, src_ref, dst_ref, xs_ref, ws_ref, wgu_ref, bgu_ref, wd_ref, bd_ref,
                out_ref, xbuf, wbuf, obuf, gsem, ssem, wgu_bf, wd_bf, *, dff):
    i = pl.program_id(0)
    nact = nact_ref[0]
    per_blk = MOE_BLOCK // MOE_CHUNK
    rows = lambda c: pl.ds(pl.multiple_of(c * MOE_CHUNK, MOE_CHUNK), MOE_CHUNK)

    def gathers(blk, slot):
        copies = []
        for m in range(per_blk):
            c = src_ref[blk * per_blk + m]
            copies.append(pltpu.make_async_copy(xs_ref.at[rows(c), :], xbuf.at[slot, rows(m), :], gsem.at[slot]))
            copies.append(pltpu.make_async_copy(ws_ref.at[rows(c), :], wbuf.at[slot, rows(m), :], gsem.at[slot]))
        return copies

    def scatters(blk, slot):
        return [pltpu.make_async_copy(obuf.at[slot, rows(m), :], out_ref.at[rows(dst_ref[blk * per_blk + m]), :],
                                      ssem.at[slot]) for m in range(per_blk)]

    @pl.when(i == 0)
    def _():
        for cp in gathers(0, 0):
            cp.start()

    @pl.when(i < nact)
    def _():
        slot = i % 2
        for cp in gathers(i, slot):
            cp.wait()

        @pl.when(i + 1 < nact)
        def _():
            for cp in gathers(i + 1, 1 - slot):
                cp.start()

        @pl.when((i == 0) | (be_ref[i] != be_ref[jnp.maximum(i - 1, 0)]))
        def _():
            wgu_bf[...] = wgu_ref[...].astype(BF16)
            wd_bf[...] = wd_ref[...].astype(BF16)

        gu = jnp.dot(xbuf[slot], wgu_bf[...], preferred_element_type=F32) + bgu_ref[...]
        gate = jnp.minimum(gu[:, :dff], SWIGLU_LIMIT)
        up = jnp.clip(gu[:, dff:], -SWIGLU_LIMIT, SWIGLU_LIMIT)
        act = (up + 1.0) * gate * _sigmoid(gate * SWIGLU_ALPHA)
        res = jnp.dot(act.astype(BF16), wd_bf[...], preferred_element_type=F32) + bd_ref[...]
        obuf[slot] = (res * wbuf[slot][:, 0:1]).astype(BF16)

        @pl.when(i >= 1)
        def _():
            for cp in scatters(i - 1, 1 - slot):
                cp.wait()

        for cp in scatters(i, slot):
            cp.start()

        @pl.when(i == nact - 1)
        def _():
            for cp in scatters(i, slot):
                cp.wait()


def _moe(blk_e, nact, src, dst, xs, ws, w_gu, b_gu, w_down, b_down, n_blocks):
    ne, d, dff2 = w_gu.shape
    dff = dff2 // 2
    blk = MOE_BLOCK
    grid_spec = pltpu.PrefetchScalarGridSpec(
        num_scalar_prefetch=4,
        grid=(n_blocks,),
        in_specs=[pl.BlockSpec(memory_space=pl.ANY), pl.BlockSpec(memory_space=pl.ANY),
                  pl.BlockSpec((None, d, dff2), lambda i, be, *_: (be[i], 0, 0)),
                  pl.BlockSpec((None, 1, dff2), lambda i, be, *_: (be[i], 0, 0)),
                  pl.BlockSpec((None, dff, d), lambda i, be, *_: (be[i], 0, 0)),
                  pl.BlockSpec((None, 1, d), lambda i, be, *_: (be[i], 0, 0))],
        out_specs=pl.BlockSpec(memory_space=pl.ANY),
        scratch_shapes=[pltpu.VMEM((2, blk, d), BF16), pltpu.VMEM((2, blk, LANES), F32),
                        pltpu.VMEM((N_OBUF, blk, d), BF16),
                        pltpu.SemaphoreType.DMA((2,)), pltpu.SemaphoreType.DMA((N_OBUF,)),
                        pltpu.VMEM((d, dff2), BF16), pltpu.VMEM((dff, d), BF16)],
    )
    return pl.pallas_call(
        functools.partial(_moe_kernel, dff=dff),
        grid_spec=grid_spec,
        out_shape=jax.ShapeDtypeStruct(xs.shape, BF16),
        compiler_params=_cparams("arbitrary"),
        name="moe",
    )(blk_e, nact, src, dst, xs, ws, w_gu, b_gu.reshape(ne, 1, dff2), w_down, b_down.reshape(ne, 1, d))


def _combine_kernel(pos_ref, used_ref, ys_ref, x1_ref, gt_ref, o_ref, p_ref, *, n_slots):
    tn = x1_ref.shape[0]
    filled = lax.broadcasted_iota(I32, (n_slots, 1), 0) < used_ref[0:1, 0:1]
    ys = jnp.where(filled, ys_ref[...], jnp.zeros((), BF16))
    lane = lax.broadcasted_iota(I32, (tn, SLOT_ROWS), 1)
    pos = [jnp.broadcast_to(pos_ref[:, kk:kk + 1], (tn, SLOT_ROWS)) - lane for kk in range(TOP_K)]

    def cols(c, carry):
        c0 = pl.multiple_of(c * SLOT_ROWS, SLOT_ROWS)
        p = jnp.zeros((tn, SLOT_ROWS), F32)
        for kk in range(TOP_K):
            p = p + jnp.where(pos[kk] == c0, 1.0, 0.0)
        p_ref[:, pl.ds(c0, SLOT_ROWS)] = p.astype(BF16)
        return carry

    lax.fori_loop(0, n_slots // SLOT_ROWS, cols, 0)
    o_ref[...] = x1_ref[...] + gt_ref[...] * jnp.dot(p_ref[...], ys, preferred_element_type=F32)


def _combine(pos_t, used, ys, x1, mod3, tile0, rows, n_slots):
    nb, t, d = x1.shape
    rmod = mod3.shape[1]
    nt = t // rows
    per_tile = ROUTE_TILE // rows
    row = lambda b, i: (b, i, 0)
    gate = pl.BlockSpec((None, rmod, d), (lambda b, i: (b, 0, 5)) if rmod == 1 else (lambda b, i: (b, i, 5)))
    return pl.pallas_call(
        functools.partial(_combine_kernel, n_slots=n_slots),
        grid=(nb, nt),
        in_specs=[pl.BlockSpec((rows, TOP_K), lambda b, i: (b * nt + i, 0)),
                  pl.BlockSpec((None, SUBLANES, LANES), lambda b, i: (tile0 + (b * nt + i) // per_tile, 0, 0)),
                  pl.BlockSpec((n_slots, d), lambda b, i: (tile0 + (b * nt + i) // per_tile, 0)),
                  pl.BlockSpec((None, rows, d), row), gate],
        out_specs=pl.BlockSpec((None, rows, d), row),
        out_shape=jax.ShapeDtypeStruct((nb, t, d), F32),
        scratch_shapes=[pltpu.VMEM((rows, n_slots), BF16)],
        compiler_params=_cparams("parallel", "parallel"),
        name="combine",
    )(pos_t, used, ys, x1, mod3)


def _rope_tables(pos, n_heads):
    half = HEAD_DIM // 2
    inv_freq = 1.0 / (ROPE_THETA ** (jnp.arange(0, HEAD_DIM, 2, dtype=F32) / HEAD_DIM))
    ang = pos.astype(F32)[:, None] * inv_freq[None, :]
    cos, sin = jnp.cos(ang), jnp.sin(ang)
    del half
    return (jnp.tile(jnp.concatenate([cos, cos], axis=-1), (1, n_heads)),
            jnp.tile(jnp.concatenate([-sin, sin], axis=-1), (1, n_heads)))


def _block_diag(width, value):
    h = np.arange(width) // HEAD_DIM
    return jnp.asarray(np.where(h[:, None] == h[None, :], value, 0.0), F32)


def kernel(x_prompt, x_sample, cache_k, cache_v, state_wkv, state_shift, c_prompt, c_sample, w_ada, b_ada, norm1_g, norm2_g, w_in, q_norm_g, k_norm_g, rwkv_mu, rwkv_w0, rwkv_w2, rwkv_a0, rwkv_a2, rwkv_g2, rwkv_k_k, rwkv_k_a, rwkv_r_k, rwkv_ln_w, rwkv_ln_b, w_out, router_w, router_b, moe_w_gu, moe_b_gu, moe_w_down, moe_b_down):
    nbp, t, d = x_prompt.shape
    nbs, ts, _ = x_sample.shape
    depth = w_ada.shape[0]
    assert depth == 1 and ts == 1
    n_heads = cache_k.shape[3]
    aw = n_heads * HEAD_DIM
    rwid = rwkv_w0.shape[1]
    rwc = rwkv_mu.shape[1]
    past = cache_k.shape[2]
    ne = router_w.shape[2]
    keep = min(MAX_WINDOW, t)
    lyr = 0

    w_in_bf = w_in[lyr].astype(BF16)
    wt_bf = w_out[lyr][:aw].astype(BF16)
    wb_bf = w_out[lyr][aw:].astype(BF16)
    g1 = norm1_g[lyr].reshape(1, d)
    g2 = norm2_g[lyr].reshape(1, d)
    qg = jnp.tile(q_norm_g[lyr], n_heads).reshape(1, aw)
    kg = jnp.tile(k_norm_g[lyr], n_heads).reshape(1, aw)
    bd_mean_a = _block_diag(aw, 1.0 / HEAD_DIM)
    bd_mean_r = _block_diag(rwid, 1.0 / HEAD_DIM)
    bd_ones_r = _block_diag(rwid, 1.0)
    dl = rwkv_w2.shape[1]
    w2p = jnp.zeros((LANES, rwid), F32).at[:dl].set(rwkv_w2[lyr])
    a2p = jnp.zeros((LANES, rwid), F32).at[dl:dl + rwkv_a2.shape[1]].set(rwkv_a2[lyr])
    vec = lambda a: a[lyr].reshape(1, -1)
    rwt = router_w[lyr].T
    rb = router_b[lyr].reshape(ne, 1)

    rows_c = nbp + nbs
    rows_pad = -(-rows_c // 8) * 8
    c_all = jnp.zeros((rows_pad, d), F32).at[:nbp].set(c_prompt).at[nbp:rows_c].set(c_sample)
    mod = _ada(c_all, w_ada[lyr], b_ada[lyr])
    mod_p = mod[:nbp].reshape(nbp, 1, 6 * d)
    mod_s = mod[nbp:rows_c].reshape(1, nbs, 6 * d)
    xs3 = x_sample.reshape(1, nbs, d)

    cos_p, sin_p = _rope_tables(jnp.arange(t), n_heads)
    cos_s, sin_s = _rope_tables(jnp.full((nbs,), PAST_LEN), n_heads)
    qp, kp, vp, rwp, kt_p, vt_p = _inproj(x_prompt, mod_p, g1, w_in_bf, cos_p, sin_p, qg, kg, bd_mean_a,
                                          ROW_TILE, keep=keep)
    qs, ks, vs, rws = _inproj(xs3, mod_s, g1, w_in[lyr], cos_s, sin_s, qg, kg, bd_mean_a, nbs)

    attn_p = _attn_prompt(qp, kp, vp)
    as_rows = lambda a: a.reshape(nbs, 1, -1)
    cache_t = lambda cch: jnp.transpose(cch[lyr], (0, 2, 3, 1)).reshape(nbs, aw, past)
    attn_s = _attn_sample(as_rows(qs), as_rows(ks), as_rows(vs), cache_t(cache_k), cache_t(cache_v))

    rw_args = (vec(rwkv_mu), vec(rwkv_w0), w2p, vec(rwkv_a0), a2p, rwkv_g2[lyr], vec(rwkv_k_k), vec(rwkv_k_a), bd_ones_r)
    pre_p = _rwprep(rwp, jnp.zeros((nbp, 1, rwc), F32), *rw_args, tm=ROW_TILE, whole_prev=False)
    pre_s = _rwprep(rws, state_shift[lyr].reshape(1, nbs, rwc), *rw_args, tm=nbs, whole_prev=True)
    r_p, ld_p, k_p, v_p, al_p, be_p, g_p = pre_p
    y_p, wkv_p = _rwchunk(r_p, ld_p, k_p, v_p, al_p, be_p)
    r_s, ld_s, k_s, v_s, al_s, be_s, g_s = pre_s
    y_s, wkv_s = _rwstep(*(as_rows(a) for a in (r_s, ld_s, k_s, v_s, al_s, be_s)), state_wkv[lyr])
    y_s = y_s.reshape(1, nbs, rwid)

    n_p = nbp * t
    n_valid = n_p + nbs
    assert ROW_TILE == ROUTE_TILE and n_p % ROUTE_TILE == 0 and nbs <= ROUTE_TILE and d == SUBLANES * LANES
    n_tiles = n_p // ROUTE_TILE + 1
    n_slots = _slot_capacity(ne)
    n_rows = n_tiles * n_slots + N_OBUF * MOE_BLOCK
    op_args = (rwkv_r_k[lyr].reshape(1, rwid), vec(rwkv_ln_w), vec(rwkv_ln_b), bd_mean_r, bd_ones_r, rwt, rb)
    x1_p, pos_p, _, cnt_p, *bufs = _outproj(x_prompt, attn_p, y_p, r_p, k_p, v_p, g_p, mod_p, g2, wt_bf, wb_bf,
                                            *op_args, tm=ROW_TILE, n_slots=n_slots, n_rows=n_rows)
    x1_s, pos_s, _, cnt_s, xs, ws = _outproj(xs3, attn_s.reshape(1, nbs, aw), y_s, r_s, k_s, v_s, g_s, mod_s, g2,
                                             w_out[lyr][:aw], w_out[lyr][aw:], *op_args, tm=nbs, n_slots=n_slots,
                                             n_rows=n_rows, tile0=n_p // ROUTE_TILE, bufs=bufs)

    cnt = jnp.concatenate([cnt_p, cnt_s])[:, :, 0].astype(I32)
    per_blk = MOE_BLOCK // MOE_CHUNK
    n_blocks = -(-(n_valid * TOP_K // MOE_CHUNK + n_tiles * ne) // per_blk) + ne
    blk_e, nact, src, dst = _chunk_tables(cnt, n_slots, n_blocks)
    ys = _moe(blk_e, nact, src, dst, xs, ws, moe_w_gu[lyr], moe_b_gu[lyr], moe_w_down[lyr], moe_b_down[lyr], n_blocks)
    used = jnp.sum(-(-cnt // MOE_CHUNK) * MOE_CHUNK, axis=1)
    used = jnp.broadcast_to(used[:, None, None], (n_tiles, SUBLANES, LANES))
    y_prompt = _combine(pos_p.T, used, ys, x1_p, mod_p, 0, ROUTE_TILE, n_slots)
    y_sample = _combine(pos_s.T, used, ys, x1_s, mod_s, n_p // ROUTE_TILE, nbs, n_slots)

    kept = lambda a: jnp.transpose(a.reshape(nbp, n_heads, HEAD_DIM, keep), (0, 3, 1, 2))[None]
    return (y_prompt, y_sample.reshape(nbs, ts, d), kept(kt_p), kept(vt_p), wkv_p[None], rwp[:, t - 1][None],
            ks.reshape(nbs, ts, n_heads, HEAD_DIM)[None], vs.reshape(nbs, ts, n_heads, HEAD_DIM)[None],
            wkv_s[None], rws.reshape(nbs, rwc)[None])
```

```python
import functools

import numpy as np
import jax
import jax.numpy as jnp
from jax import lax
from jax.experimental import pallas as pl
from jax.experimental.pallas import tpu as pltpu

F32 = jnp.float32
BF16 = jnp.bfloat16
I32 = jnp.int32

HEAD_DIM = 64
LANES = 128
SUBLANES = 8
DILATED_PATTERNS = ((128, 1), (512, 4), (2048, 16))
WINDOW_STEPS = 128
MAX_WINDOW = 2048
PAST_LEN = 16384
ROPE_THETA = 10000.0
NORM_EPS = 1e-6
GN_EPS = 64e-5
TOP_K = 4
SWIGLU_ALPHA = 1.702
SWIGLU_LIMIT = 7.0
RW_CHUNK = 128
RW_PAIRS = 2
MOE_BLOCK = 512
MOE_CHUNK = 16
SLOT_ROWS = 256
ATTN_UNITS = 8
ROW_TILE = 512
ROUTE_TILE = 512
VMEM_LIMIT = 56 * 1024 * 1024
NEG_BIG = -1e30


def _cparams(*sem):
    return pltpu.CompilerParams(dimension_semantics=sem, vmem_limit_bytes=VMEM_LIMIT)


def _dot(a, b):
    return jnp.dot(a.astype(BF16), b.astype(BF16), preferred_element_type=F32)


def _split2(a):
    hi = a.astype(BF16)
    lo = (a - hi.astype(F32)).astype(BF16)
    return hi, lo


def _split3(a):
    hi = a.astype(BF16)
    r1 = a - hi.astype(F32)
    mid = r1.astype(BF16)
    lo = (r1 - mid.astype(F32)).astype(BF16)
    return hi, mid, lo


def _dot_x(a, e):
    e = e.astype(BF16)
    hi, mid, lo = _split3(a)
    return (jnp.dot(hi, e, preferred_element_type=F32) + jnp.dot(mid, e, preferred_element_type=F32)
            + jnp.dot(lo, e, preferred_element_type=F32))


def _xdot(e, a):
    e = e.astype(BF16)
    hi, mid, lo = _split3(a)
    return (jnp.dot(e, hi, preferred_element_type=F32) + jnp.dot(e, mid, preferred_element_type=F32)
            + jnp.dot(e, lo, preferred_element_type=F32))


def _dot3(a, b):
    ah, al = _split2(a)
    bh, bl = _split2(b)
    return (jnp.dot(ah, bh, preferred_element_type=F32) + jnp.dot(ah, bl, preferred_element_type=F32)
            + jnp.dot(al, bh, preferred_element_type=F32))


def _dot3_nt(a, b):
    ah, al = _split2(a)
    bh, bl = _split2(b)
    dn = (((1,), (1,)), ((), ()))
    return (lax.dot_general(ah, bh, dn, preferred_element_type=F32)
            + lax.dot_general(ah, bl, dn, preferred_element_type=F32)
            + lax.dot_general(al, bh, dn, preferred_element_type=F32))


def _mm(a, w):
    if w.dtype == BF16:
        return jnp.dot(a.astype(BF16), w, preferred_element_type=F32)
    return _dot3(a, w)


def _sigmoid(x):
    return 1.0 / (1.0 + jnp.exp(-x))


def _ada_kernel(c_ref, w_ref, b_ref, o_ref):
    c = c_ref[...]
    o_ref[...] = _dot3(c * _sigmoid(c), w_ref[...]) + b_ref[...]


def _ada(c_all, w_ada, b_ada):
    rows, d = c_all.shape
    n = w_ada.shape[1]
    tn = n // 4
    return pl.pallas_call(
        _ada_kernel,
        grid=(n // tn,),
        in_specs=[pl.BlockSpec((rows, d), lambda j: (0, 0)),
                  pl.BlockSpec((d, tn), lambda j: (0, j)),
                  pl.BlockSpec((1, tn), lambda j: (0, j))],
        out_specs=pl.BlockSpec((rows, tn), lambda j: (0, j)),
        out_shape=jax.ShapeDtypeStruct((rows, n), F32),
        compiler_params=_cparams("arbitrary"),
        name="ada",
    )(c_all, w_ada, b_ada.reshape(1, n))


N_RW_PARAMS = 9
N_RW_FEATS = 7


def _inproj_kernel(x_ref, sh_ref, sc_ref, g_ref, w_ref, cos_ref, sin_ref, qg_ref, kg_ref, bd_ref, *rest,
                   aw, first_kept, n_kept, rw_fused):
    n_in = 1 + N_RW_PARAMS if rw_fused else 0
    rw_in, outs = rest[:n_in], rest[n_in:]
    q_ref, k_ref, v_ref = outs[:3]
    n_rw_out = N_RW_FEATS + 1 if rw_fused else 1
    rw_out = outs[3:3 + n_rw_out]
    maybe_kv_t = outs[3 + n_rw_out:3 + n_rw_out + n_kept]
    x = x_ref[...]
    ms = jnp.mean(x * x, axis=-1, keepdims=True)
    h = x * lax.rsqrt(ms + NORM_EPS) * g_ref[...] * (1.0 + sc_ref[...]) + sh_ref[...]
    proj = _mm(h, w_ref[...])
    cos = cos_ref[...]
    sin = sin_ref[...]
    lane = lax.broadcasted_iota(I32, (1, aw), 1)
    first_half = (lane % HEAD_DIM) < (HEAD_DIM // 2)
    bd = bd_ref[...]

    def norm_rope(t, g):
        tn = t * lax.rsqrt((_dot if w_ref.dtype == BF16 else _dot_x)(t * t, bd) + NORM_EPS) * g
        rot = jnp.where(first_half, pltpu.roll(tn, aw - HEAD_DIM // 2, 1), pltpu.roll(tn, HEAD_DIM // 2, 1))
        return tn * cos + rot * sin

    scale = 1.0 / np.sqrt(HEAD_DIM).astype(np.float32)
    q_ref[...] = norm_rope(proj[:, :aw], qg_ref[...]) * scale
    k = norm_rope(proj[:, aw:2 * aw], kg_ref[...])
    v = proj[:, 2 * aw:3 * aw]
    k_ref[...] = k
    v_ref[...] = v
    rw = proj[:, 3 * aw:]
    if rw_fused:
        carry_ref = outs[-1]
        tail = rw[rw.shape[0] - SUBLANES:, :]
        before = jnp.where(pl.program_id(1) == 0, rw_in[0][...], carry_ref[SUBLANES - 1:SUBLANES, :])
        for ref, val in zip(rw_out, _rw_features(rw, _shifted(rw, before), *rw_in[1:])):
            ref[...] = val
        carry_ref[...] = tail
        rw_out[-1][...] = tail
    else:
        rw_out[0][...] = rw
    if maybe_kv_t:
        kt_ref, vt_ref = maybe_kv_t
        kept = pl.program_id(1) >= first_kept

        @pl.when(kept)
        def _():
            kt_ref[...] = k.T
            vt_ref[...] = v.T

        @pl.when(jnp.logical_not(kept))
        def _():
            kt_ref[...] = jnp.zeros_like(kt_ref)
            vt_ref[...] = jnp.zeros_like(vt_ref)


def _inproj(x3, mod3, g1, w_in_bf, cos_t, sin_t, qg, kg, bd_mean, tm, keep=0, rw=None):
    nb, t, d = x3.shape
    r = mod3.shape[1]
    ncol = w_in_bf.shape[1]
    aw = cos_t.shape[1]
    rwc = ncol - 3 * aw
    grid = (nb, t // tm)
    row = lambda b, i: (b, i, 0)
    const = lambda b, i: (0, 0)
    once = dict(pipeline_mode=pl.Buffered(1)) if rw else {}
    mod_spec = lambda s: pl.BlockSpec((None, r, d), (lambda b, i: (b, 0, s)) if r == 1 else (lambda b, i: (b, i, s)))
    out_specs = [pl.BlockSpec((None, tm, aw), row)] * 3
    out_shape = [jax.ShapeDtypeStruct((nb, t, aw), F32)] * 3
    rw_in, rw_specs, scratch = (), [], []
    if rw:
        rw_in = tuple(rw)
        rwid = rw_in[2].shape[1]
        rw_specs = [pl.BlockSpec((None, 1, rwc), lambda b, i: (b, 0, 0))] + [
            pl.BlockSpec(a.shape, const, **once) for a in rw_in[1:]]
        out_specs += [pl.BlockSpec((None, tm, rwid), row)] * N_RW_FEATS + [
            pl.BlockSpec((None, SUBLANES, rwc), lambda b, i: (b, 0, 0))]
        out_shape += [jax.ShapeDtypeStruct((nb, t, rwid), F32)] * N_RW_FEATS + [
            jax.ShapeDtypeStruct((nb, SUBLANES, rwc), F32)]
        scratch = [pltpu.VMEM((SUBLANES, rwc), F32)]
    else:
        out_specs += [pl.BlockSpec((None, tm, rwc), row)]
        out_shape += [jax.ShapeDtypeStruct((nb, t, rwc), F32)]
    first_kept = (t - keep) // tm
    if keep:
        assert keep % tm == 0 and (t - keep) % tm == 0
        kept_spec = pl.BlockSpec((None, aw, tm), lambda b, i: (b, 0, jnp.maximum(i - first_kept, 0)))
        out_specs += [kept_spec, kept_spec]
        out_shape += [jax.ShapeDtypeStruct((nb, aw, keep), F32)] * 2
    outs = pl.pallas_call(
        functools.partial(_inproj_kernel, aw=aw, first_kept=first_kept, n_kept=2 if keep else 0, rw_fused=bool(rw)),
        grid=grid,
        in_specs=[pl.BlockSpec((None, tm, d), row), mod_spec(0), mod_spec(1),
                  pl.BlockSpec((1, d), const), pl.BlockSpec((d, ncol), const, **once),
                  pl.BlockSpec((tm, aw), lambda b, i: (i, 0)), pl.BlockSpec((tm, aw), lambda b, i: (i, 0)),
                  pl.BlockSpec((1, aw), const), pl.BlockSpec((1, aw), const),
                  pl.BlockSpec((aw, aw), const, **once)] + rw_specs,
        out_specs=out_specs,
        out_shape=out_shape,
        scratch_shapes=scratch,
        compiler_params=_cparams("parallel", "arbitrary"),
        name="inproj",
    )(x3, mod3, mod3, g1, w_in_bf, cos_t, sin_t, qg, kg, bd_mean, *rw_in)
    return outs


def _attn_prompt_kernel(q_ref, k_ref, v_ref, o_ref, kp_ref, vp_ref, m_ref, l_ref, acc_ref, *, t, pad):
    nq = WINDOW_STEPS
    nk = 2 * WINDOW_STEPS
    kp_ref[pl.ds(0, pad), :] = jnp.zeros((pad, LANES), F32)
    vp_ref[pl.ds(0, pad), :] = jnp.zeros((pad, LANES), F32)
    kp_ref[pl.ds(pad, t), :] = k_ref[...]
    vp_ref[pl.ds(pad, t), :] = v_ref[...]
    head0 = lax.broadcasted_iota(I32, (1, LANES), 1) < HEAD_DIM
    qi = lax.broadcasted_iota(I32, (nq, nk), 0)
    kj = lax.broadcasted_iota(I32, (nq, nk), 1)
    steps_back = qi + nq - kj
    band = (steps_back >= 0) & (steps_back <= WINDOW_STEPS)
    has_past = kj >= nq

    hsels = (head0, jnp.logical_not(head0))
    nt = (((1,), (1,)), ((), ()))

    for p, (_, d) in enumerate(DILATED_PATTERNS):
        def units(g, carry, p=p, d=d):
            rows_q, kb, vb, valid, qh = [], [], [], [], []
            for j in range(ATTN_UNITS):
                u = g * ATTN_UNITS + j
                res = u % d
                blk = u // d
                q_start = res + d * nq * blk
                k_start = pad + q_start - d * nq
                if d == 1:
                    rows_q.append(pl.ds(q_start, nq))
                    rows_k = pl.ds(k_start, nk)
                else:
                    rows_q.append(pl.ds(q_start, nq, stride=d))
                    rows_k = pl.ds(k_start, nk, stride=d)
                q = q_ref[rows_q[j], :]
                kb.append(kp_ref[rows_k, :].astype(BF16))
                vb.append(vp_ref[rows_k, :].astype(BF16))
                valid.append(band & (has_past | (blk > 0)))
                qh.append([jnp.where(hsel, q, 0.0).astype(BF16) for hsel in hsels])
            chains = [(j, h) for j in range(ATTN_UNITS) for h in range(2)]
            s = [lax.dot_general(qh[j][h], kb[j], nt, preferred_element_type=F32) for j, h in chains]
            s = [jnp.where(valid[j], sc, NEG_BIG) for (j, h), sc in zip(chains, s)]
            mx = [jnp.max(sc, axis=-1, keepdims=True) for sc in s]
            e = [jnp.exp(sc - m) for sc, m in zip(s, mx)]
            den = [jnp.sum(ec, axis=-1, keepdims=True) for ec in e]
            o = [jnp.dot(ec.astype(BF16), vb[j], preferred_element_type=F32) for (j, h), ec in zip(chains, e)]
            for j in range(ATTN_UNITS):
                m_ref[p, rows_q[j], :] = jnp.where(head0, mx[2 * j], mx[2 * j + 1])
                l_ref[p, rows_q[j], :] = jnp.where(head0, den[2 * j], den[2 * j + 1])
                acc_ref[p, rows_q[j], :] = jnp.where(head0, o[2 * j], o[2 * j + 1])
            return carry

        lax.fori_loop(0, t // nq // ATTN_UNITS, units, 0)

    rows = 256

    def merge(i, carry):
        sl = pl.ds(pl.multiple_of(i * rows, rows), rows)
        m0, m1, m2 = m_ref[0, sl, :], m_ref[1, sl, :], m_ref[2, sl, :]
        mm = jnp.maximum(jnp.maximum(m0, m1), m2)
        w0, w1, w2 = jnp.exp(m0 - mm), jnp.exp(m1 - mm), jnp.exp(m2 - mm)
        num = w0 * acc_ref[0, sl, :] + w1 * acc_ref[1, sl, :] + w2 * acc_ref[2, sl, :]
        den = w0 * l_ref[0, sl, :] + w1 * l_ref[1, sl, :] + w2 * l_ref[2, sl, :]
        o_ref[sl, :] = num / den
        return carry

    lax.fori_loop(0, t // rows, merge, 0)


def _attn_prompt(q, k, v):
    nb, t, aw = q.shape
    pad = MAX_WINDOW
    assert t % MAX_WINDOW == 0
    spec = pl.BlockSpec((None, t, LANES), lambda b, hp: (b, 0, hp))
    return pl.pallas_call(
        functools.partial(_attn_prompt_kernel, t=t, pad=pad),
        grid=(nb, aw // LANES),
        in_specs=[spec, spec, spec],
        out_specs=spec,
        out_shape=jax.ShapeDtypeStruct((nb, t, aw), F32),
        scratch_shapes=[pltpu.VMEM((pad + t, LANES), F32), pltpu.VMEM((pad + t, LANES), F32),
                        pltpu.VMEM((3, t, LANES), F32), pltpu.VMEM((3, t, LANES), F32),
                        pltpu.VMEM((3, t, LANES), F32)],
        compiler_params=_cparams("parallel", "parallel"),
        name="attn_prompt",
    )(q, k, v)


def _attn_sample_kernel(q_ref, kn_ref, vn_ref, kt_ref, vt_ref, o_ref, *, nh, w):
    aw = nh * HEAD_DIM
    dist = w - lax.broadcasted_iota(I32, (1, w), 1)
    mult = jnp.zeros((1, w), F32)
    for win, d in DILATED_PATTERNS:
        mult = mult + jnp.where((dist % d == 0) & (dist <= win), 1.0, 0.0)
    n_pat = float(len(DILATED_PATTERNS))
    q_col = _col(q_ref[...], aw)
    kn_col = _col(kn_ref[...], aw)
    vn_col = _col(vn_ref[...], aw)
    heads = range(nh)
    hs = [pl.ds(h * HEAD_DIM, HEAD_DIM) for h in heads]
    cut = lambda col, h: col[h * HEAD_DIM:(h + 1) * HEAD_DIM]
    s = [jnp.sum(kt_ref[hs[h], :] * cut(q_col, h), axis=0, keepdims=True) for h in heads]
    s_self = [jnp.sum(cut(q_col, h) * cut(kn_col, h), axis=0, keepdims=True) for h in heads]
    s = [jnp.where(mult > 0.0, sh, NEG_BIG) for sh in s]
    mx = [jnp.maximum(jnp.max(sh, axis=1, keepdims=True), ss) for sh, ss in zip(s, s_self)]
    pr = [mult * jnp.exp(sh - m) for sh, m in zip(s, mx)]
    p_self = [n_pat * jnp.exp(ss - m) for ss, m in zip(s_self, mx)]
    den = [jnp.sum(p, axis=1, keepdims=True) + ps for p, ps in zip(pr, p_self)]
    num = [jnp.sum(vt_ref[hs[h], :] * pr[h], axis=1, keepdims=True) + p_self[h] * cut(vn_col, h) for h in heads]
    outs = [n / dn for n, dn in zip(num, den)]
    o_ref[...] = _row(jnp.concatenate(outs, axis=0), aw)


def _attn_sample(q, kn, vn, cache_kt, cache_vt):
    nb, _, aw = q.shape
    w = cache_kt.shape[2]
    assert w == MAX_WINDOW
    row = pl.BlockSpec((None, 1, aw), lambda i: (i, 0, 0))
    mat = pl.BlockSpec((None, aw, w), lambda i: (i, 0, 0))
    return pl.pallas_call(
        functools.partial(_attn_sample_kernel, nh=aw // HEAD_DIM, w=w),
        grid=(nb,),
        in_specs=[row, row, row, mat, mat],
        out_specs=row,
        out_shape=jax.ShapeDtypeStruct((nb, 1, aw), F32),
        compiler_params=_cparams("parallel"),
        name="attn_sample",
    )(q, kn, vn, cache_kt, cache_vt)


def _rw_features(p, prev, mu_ref, w0_ref, w2_ref, a0_ref, a2_ref, g2_ref, kk_ref, ka_ref, bd_ref):
    rwid = w0_ref.shape[1]
    xs = p + mu_ref[...] * (prev - p)
    r = xs[:, :rwid]
    k = xs[:, rwid:2 * rwid]
    v = xs[:, 2 * rwid:3 * rwid]
    xwa = xs[:, 3 * rwid:3 * rwid + LANES]
    xg = xs[:, 3 * rwid + LANES:]
    z = w0_ref[...] + _dot3(jnp.tanh(xwa), w2_ref[...])
    softplus_neg = jnp.maximum(-z, 0.0) + jnp.log(1.0 + jnp.exp(-jnp.abs(z)))
    w = -softplus_neg - 0.5
    a = _sigmoid(a0_ref[...] + _dot3(xwa, a2_ref[...]))
    g = _dot3(_sigmoid(xg), g2_ref[...])
    kk = k * kk_ref[...]
    norm = jnp.sqrt(_dot_x(kk * kk, bd_ref[...]))
    kk = kk / jnp.maximum(norm, 1e-12)
    return r, -jnp.exp(w), k * (1.0 + (a - 1.0) * ka_ref[...]), v, -kk, kk * a, g


def _shifted(p, before):
    rowi = lax.broadcasted_iota(I32, (p.shape[0], 1), 0)
    return jnp.where(rowi == 0, before, pltpu.roll(p, 1, 0))


def _rwprep_kernel(p_ref, prev_ref, first_ref, *refs, whole_prev):
    params, outs = refs[:9], refs[9:]
    p = p_ref[...]
    if whole_prev:
        prev = first_ref[...]
    else:
        prev = _shifted(p, jnp.where(pl.program_id(1) == 0, first_ref[...], prev_ref[7:8, :]))
    for ref, val in zip(outs, _rw_features(p, prev, *params)):
        ref[...] = val


def _rwprep(rw, first, mu, w0, w2p, a0, a2p, g2, k_k, k_a, bd_ones, tm, whole_prev):
    nb, t, rwc = rw.shape
    rwid = w0.shape[1]
    grid = (nb, t // tm)
    row = lambda b, i: (b, i, 0)
    const = lambda b, i: (0, 0)
    if whole_prev:
        prev_spec = pl.BlockSpec((None, tm, rwc), row)
        first_spec = pl.BlockSpec((None, tm, rwc), row)
    else:
        prev_spec = pl.BlockSpec((None, 8, rwc), lambda b, i: (b, jnp.maximum(i * (tm // 8) - 1, 0), 0))
        first_spec = pl.BlockSpec((None, 1, rwc), lambda b, i: (b, 0, 0))
    vec = pl.BlockSpec((1, rwid), const)
    out = pl.BlockSpec((None, tm, rwid), row)
    return pl.pallas_call(
        functools.partial(_rwprep_kernel, whole_prev=whole_prev),
        grid=grid,
        in_specs=[pl.BlockSpec((None, tm, rwc), row), prev_spec, first_spec,
                  pl.BlockSpec((1, rwc), const), vec, pl.BlockSpec((LANES, rwid), const),
                  vec, pl.BlockSpec((LANES, rwid), const), pl.BlockSpec((LANES, rwid), const), vec, vec,
                  pl.BlockSpec((rwid, rwid), const)],
        out_specs=[out] * 7,
        out_shape=[jax.ShapeDtypeStruct((nb, t, rwid), F32)] * 7,
        compiler_params=_cparams("parallel", "parallel"),
        name="rwprep",
    )(rw, rw, first, mu, w0, w2p, a0, a2p, g2, k_k, k_a, bd_ones)


def _rwchunk_kernel(r_ref, ld_ref, k_ref, v_ref, al_ref, be_ref, y_ref, st_ref, z_ref, *, nb, npb):
    c = RW_CHUNK
    ci = pl.program_id(1)

    @pl.when(ci == 0)
    def _():
        z_ref[...] = jnp.zeros_like(z_ref)

    ti = lax.broadcasted_iota(I32, (c, c), 0)
    si = lax.broadcasted_iota(I32, (c, c), 1)
    low_incl = si <= ti
    low_strict = si < ti
    diag = si == ti
    tri = jnp.where(low_incl, 1.0, 0.0).astype(BF16)
    eye = jnp.where(diag, 1.0, 0.0)
    head0 = lax.broadcasted_iota(I32, (1, LANES), 1) < HEAD_DIM
    hsels = (head0, jnp.logical_not(head0))
    same_head = (ti < HEAD_DIM) == (si < HEAD_DIM)
    nt = (((1,), (1,)), ((), ()))
    batches = range(nb * npb)
    chains = [(b, h) for b in batches for h in range(2)]
    lanes_of = lambda b: pl.ds((b // nb) * LANES, LANES)
    r_v, ld_v, k_v, v_v, al_v, be_v = ([ref[b % nb, :, lanes_of(b)] for b in batches]
                                       for ref in (r_ref, ld_ref, k_ref, v_ref, al_ref, be_ref))

    cum = [_xdot(tri, ld_v[b]) for b in batches]
    tot = [cm[c - 1:c, :] for cm in cum]
    e_neg = [jnp.exp(-cm) for cm in cum]
    at = [al_v[b] * jnp.exp(cum[b] - ld_v[b]) for b in batches]
    rt = [r_v[b] * jnp.exp(cum[b]) for b in batches]
    rhs_t = [jnp.concatenate([be_v[b] * e_neg[b], k_v[b] * e_neg[b]], axis=0).astype(BF16) for b in batches]
    vb = [v_v[b].astype(BF16) for b in batches]
    at_h = [jnp.where(hsels[h], at[b], 0.0) for b, h in chains]
    rt_h = [jnp.where(hsels[h], rt[b], 0.0) for b, h in chains]
    a4 = [lax.dot_general(jnp.concatenate([a, r], axis=0).astype(BF16), rhs_t[b], nt, preferred_element_type=F32)
          for (b, h), a, r in zip(chains, at_h, rt_h)]
    a_ab = [jnp.where(low_strict, m[:c, :c], 0.0) for m in a4]
    a_ak = [jnp.where(low_strict, m[:c, c:], 0.0).astype(BF16) for m in a4]
    a_r = [jnp.concatenate([jnp.where(low_incl, m[c:, :c], 0.0), jnp.where(low_incl, m[c:, c:], 0.0)],
                           axis=1).astype(BF16) for m in a4]
    levels = int(np.log2(c))
    pw = [m.astype(BF16) for m in a_ab]
    pw = [jnp.dot(m, m, preferred_element_type=F32).astype(BF16) for m in pw]
    inv = [eye + m for m in a_ab]
    for _ in range(1, levels - 1):
        x2 = [jnp.dot(m, jnp.concatenate([m, i.astype(BF16)], axis=1), preferred_element_type=F32)
              for m, i in zip(pw, inv)]
        pw = [m[:, :c].astype(BF16) for m in x2]
        inv = [i + m[:, c:] for i, m in zip(inv, x2)]
    inv = [i + jnp.dot(m, i.astype(BF16), preferred_element_type=F32) for i, m in zip(inv, pw)]
    akv = [jnp.dot(m, vb[b], preferred_element_type=F32) for (b, h), m in zip(chains, a_ak)]
    x = [_dot(i, jnp.concatenate([kv, a], axis=1)) for i, kv, a in zip(inv, akv, at_h)]
    u0_h = [m[:, :LANES] for m in x]
    at2_h = [m[:, LANES:] for m in x]
    y0_h = [jnp.dot(ar, jnp.concatenate([u0, v_v[b]], axis=0).astype(BF16), preferred_element_type=F32)
            for (b, h), ar, u0 in zip(chains, a_r, u0_h)]
    rt2_h = [r + jnp.dot(ar[:, :c], a2.astype(BF16), preferred_element_type=F32)
             for r, ar, a2 in zip(rt_h, a_r, at2_h)]
    z = [z_ref[b] for b in batches]
    uy = [_dot(jnp.concatenate([at2_h[2 * b] + at2_h[2 * b + 1], rt2_h[2 * b] + rt2_h[2 * b + 1]], axis=0), z[b])
          for b in batches]
    u = [uy[b][:c] + jnp.where(head0, u0_h[2 * b], u0_h[2 * b + 1]) for b in batches]
    for b in batches:
        y_ref[b % nb, :, lanes_of(b)] = uy[b][c:] + jnp.where(head0, y0_h[2 * b], y0_h[2 * b + 1])
    e_end = [jnp.exp(tot[b] - cum[b]) for b in batches]
    lhs_t = [jnp.concatenate([be_v[b] * e_end[b], k_v[b] * e_end[b]], axis=0) for b in batches]
    zadd = [_dot(lhs_t[b].T, jnp.concatenate([u[b], v_v[b]], axis=0)) for b in batches]
    for b in batches:
        dcol = jnp.sum(jnp.where(diag, jnp.broadcast_to(jnp.exp(tot[b]), (c, c)), 0.0), axis=1, keepdims=True)
        z_ref[b] = dcol * z[b] + jnp.where(same_head, zadd[b], 0.0)

    @pl.when(ci == pl.num_programs(1) - 1)
    def _():
        for b in batches:
            s = z_ref[b].T
            st_ref[b % nb, 2 * (b // nb)] = s[:HEAD_DIM, :HEAD_DIM]
            st_ref[b % nb, 2 * (b // nb) + 1] = s[HEAD_DIM:, HEAD_DIM:]


def _rwchunk(r, ld, k, v, al, be):
    nb, t, rwid = r.shape
    c = RW_CHUNK
    npb = RW_PAIRS
    assert t % c == 0 and c == LANES and rwid % (npb * LANES) == 0
    ngrp = rwid // (npb * LANES)
    seq = pl.BlockSpec((nb, c, npb * LANES), lambda hp, ci: (0, ci, hp))
    return pl.pallas_call(
        functools.partial(_rwchunk_kernel, nb=nb, npb=npb),
        grid=(ngrp, t // c),
        in_specs=[seq] * 6,
        out_specs=[seq, pl.BlockSpec((nb, 2 * npb, HEAD_DIM, HEAD_DIM), lambda hp, ci: (0, hp, 0, 0))],
        out_shape=[jax.ShapeDtypeStruct((nb, t, rwid), F32),
                   jax.ShapeDtypeStruct((nb, 2 * npb * ngrp, HEAD_DIM, HEAD_DIM), F32)],
        scratch_shapes=[pltpu.VMEM((nb * npb, LANES, LANES), F32)],
        compiler_params=_cparams("parallel", "arbitrary"),
        name="rwchunk",
    )(r, ld, k, v, al, be)


def _col(row, width):
    return jnp.broadcast_to(row, (LANES, width)).T[:, 0:1]


def _row(col, width):
    return jnp.broadcast_to(col, (width, LANES)).T[0:1, :]


def _rwstep_kernel(r_ref, ld_ref, k_ref, v_ref, al_ref, be_ref, s_ref, y_ref, so_ref, *, bb, nh):
    rwid = nh * HEAD_DIM

    def one(b, carry):
        r, dcy, k, al, be = (ref[b] for ref in (r_ref, ld_ref, k_ref, al_ref, be_ref))
        dcy = jnp.exp(dcy)
        v_col = _col(v_ref[b], rwid)
        heads = range(nh)
        hs = [slice(h * HEAD_DIM, (h + 1) * HEAD_DIM) for h in heads]
        st = [s_ref[b, h] for h in heads]
        sa = [jnp.sum(st[h] * al[:, hs[h]], axis=1, keepdims=True) for h in heads]
        st = [st[h] * dcy[:, hs[h]] + sa[h] * be[:, hs[h]] + v_col[hs[h], :] * k[:, hs[h]] for h in heads]
        for h in heads:
            so_ref[b, h] = st[h]
        ys = [jnp.sum(st[h] * r[:, hs[h]], axis=1, keepdims=True) for h in heads]
        y_ref[b] = _row(jnp.concatenate(ys, axis=0), rwid)
        return carry

    lax.fori_loop(0, bb, one, 0)


def _rwstep(r, ld, k, v, al, be, state):
    nb, _, rwid = r.shape
    nh = state.shape[1]
    bb = 8
    row = pl.BlockSpec((bb, 1, rwid), lambda i: (i, 0, 0))
    st = pl.BlockSpec((bb, nh, HEAD_DIM, HEAD_DIM), lambda i: (i, 0, 0, 0))
    return pl.pallas_call(
        functools.partial(_rwstep_kernel, bb=bb, nh=nh),
        grid=(nb // bb,),
        in_specs=[row] * 6 + [st],
        out_specs=[row, st],
        out_shape=[jax.ShapeDtypeStruct((nb, 1, rwid), F32), jax.ShapeDtypeStruct(state.shape, F32)],
        compiler_params=_cparams("parallel"),
        name="rwstep",
    )(r, ld, k, v, al, be, state)


def _outproj_kernel(x_ref, at_ref, y_ref, r_ref, k_ref, v_ref, g_ref, gt_ref, sh_ref, sc_ref, g2_ref,
                    wt_ref, wb_ref, rk_ref, lw_ref, lb_ref, bdm_ref, bd1_ref, rwt_ref, rb_ref, *rest, n_slots):
    x1_ref, pos_ref, wt4_ref, cnt_ref, xs_ref, ws_ref, hb_ref = rest[-7:]
    y = y_ref[...]
    bdm = bdm_ref[...]
    mean = _dot_x(y, bdm)
    yc = y - mean
    var = _dot_x(yc * yc, bdm)
    yn = yc * lax.rsqrt(var + GN_EPS) * lw_ref[...] + lb_ref[...]
    v = v_ref[...]
    bonus = _dot_x(r_ref[...] * k_ref[...] * rk_ref[...], bd1_ref[...]) * v
    rw = (yn + bonus) * g_ref[...]
    mix = _mm(at_ref[...], wt_ref[...]) + _mm(rw, wb_ref[...])
    x1 = x_ref[...] + gt_ref[...] * mix
    x1_ref[...] = x1
    ms = jnp.mean(x1 * x1, axis=-1, keepdims=True)
    h2 = x1 * lax.rsqrt(ms + NORM_EPS) * g2_ref[...] * (1.0 + sc_ref[...]) + sh_ref[...]
    pos, wts, cnt = _route_tile(_dot3_nt(rwt_ref[...], h2) + rb_ref[...])
    for kk in range(TOP_K):
        pos_ref[pl.ds(kk, 1), :] = pos[kk]
        wt4_ref[pl.ds(kk, 1), :] = wts[kk]
    cnt_ref[...] = jnp.broadcast_to(cnt, cnt_ref.shape)
    hb_ref[...] = h2.astype(BF16)
    _sort_tile(pos, wts, hb_ref, xs_ref, ws_ref, n_slots)


def _outproj(x3, attn, y, r, k, v, g, mod3, g2, wt_bf, wb_bf, r_k, ln_w, ln_b, bd_mean, bd_ones, rwt, rb, tm,
             n_slots, n_rows, tile0=0, bufs=None):
    nb, t, d = x3.shape
    rmod = mod3.shape[1]
    aw = attn.shape[2]
    ne = rwt.shape[0]
    nt = t // tm
    grid = (nb, nt)
    row = lambda b, i: (b, i, 0)
    const = lambda b, i: (0, 0)
    tile = lambda b, i: (tile0 + b * nt + i, 0)
    mod_spec = lambda s: pl.BlockSpec((None, rmod, d), (lambda b, i: (b, 0, s)) if rmod == 1 else (lambda b, i: (b, i, s)))
    half = pl.BlockSpec((None, tm, aw), row)
    vec = pl.BlockSpec((1, aw), const)
    tok4 = pl.BlockSpec((TOP_K, tm), lambda b, i: (0, b * nt + i))
    bufs = () if bufs is None else tuple(bufs)
    n_in = 20
    return pl.pallas_call(
        functools.partial(_outproj_kernel, n_slots=n_slots),
        grid=grid,
        in_specs=[pl.BlockSpec((None, tm, d), row)] + [half] * 6 + [mod_spec(2), mod_spec(3), mod_spec(4),
                  pl.BlockSpec((1, d), const), pl.BlockSpec((aw, d), const), pl.BlockSpec((aw, d), const),
                  vec, vec, vec, pl.BlockSpec((aw, aw), const), pl.BlockSpec((aw, aw), const),
                  pl.BlockSpec((ne, d), const), pl.BlockSpec((ne, 1), const)]
                 + [pl.BlockSpec(memory_space=pl.ANY)] * len(bufs),
        out_specs=[pl.BlockSpec((None, tm, d), row), tok4, tok4,
                   pl.BlockSpec((None, ne, LANES), lambda b, i: (b * nt + i, 0, 0)),
                   pl.BlockSpec((n_slots, d), tile), pl.BlockSpec((n_slots, LANES), tile)],
        out_shape=[jax.ShapeDtypeStruct((nb, t, d), F32), jax.ShapeDtypeStruct((TOP_K, nb * t), I32),
                   jax.ShapeDtypeStruct((TOP_K, nb * t), F32), jax.ShapeDtypeStruct((nb * nt, ne, LANES), F32),
                   jax.ShapeDtypeStruct((n_rows, d), BF16), jax.ShapeDtypeStruct((n_rows, LANES), F32)],
        scratch_shapes=[pltpu.VMEM((tm, d), BF16)],
        input_output_aliases={n_in: 4, n_in + 1: 5} if bufs else {},
        compiler_params=_cparams("parallel", "parallel"),
        name="outproj",
    )(x3, attn, y, r, k, v, g, mod3, mod3, mod3, g2, wt_bf, wb_bf, r_k, ln_w, ln_b, bd_mean, bd_ones, rwt, rb, *bufs)


def _route_tile(lg):
    ne, tn = lg.shape
    eidx = lax.broadcasted_iota(I32, (ne, tn), 0).astype(F32)
    vals, hots = [], []
    for _ in range(TOP_K):
        mx = jnp.max(lg, axis=0, keepdims=True)
        pick = jnp.min(jnp.where(lg == mx, eidx, float(ne)), axis=0, keepdims=True)
        hot = eidx == pick
        vals.append(mx)
        hots.append(hot)
        lg = jnp.where(hot, -jnp.inf, lg)
    ex = [jnp.exp(vv - vals[0]) for vv in vals]
    den = ex[0] + ex[1] + ex[2] + ex[3]
    wts = [e / den for e in ex]
    hot_all = jnp.zeros((ne, tn), F32)
    for hot in hots:
        hot_all = hot_all + jnp.where(hot, 1.0, 0.0)
    ri = lax.broadcasted_iota(I32, (tn, tn), 0)
    cj = lax.broadcasted_iota(I32, (tn, tn), 1)
    upper = jnp.where(ri <= cj, 1.0, 0.0).astype(BF16)
    before = jnp.dot(hot_all.astype(BF16), upper, preferred_element_type=F32) - hot_all
    cnt = jnp.sum(hot_all, axis=1, keepdims=True)
    padded = jnp.ceil(cnt / MOE_CHUNK) * MOE_CHUNK
    er = lax.broadcasted_iota(I32, (ne, ne), 0)
    ec = lax.broadcasted_iota(I32, (ne, ne), 1)
    lower_strict = jnp.where(ec < er, 1.0, 0.0)
    off = _xdot(lower_strict, jnp.broadcast_to(padded, (ne, LANES)))[:, 0:1]
    pos = [jnp.sum(jnp.where(hot, off + before, 0.0), axis=0, keepdims=True).astype(I32) for hot in hots]
    return pos, wts, cnt


def _slot_capacity(ne):
    return -(-(TOP_K * ROUTE_TILE + ne * (MOE_CHUNK - 1)) // SLOT_ROWS) * SLOT_ROWS


def _sort_tile(pos, wts, h_ref, xs_ref, ws_ref, n_slots):
    tn = h_ref.shape[0]

    def rows(c, carry):
        h = h_ref[...]
        r0 = pl.multiple_of(c * SLOT_ROWS, SLOT_ROWS)
        slot = r0 + lax.broadcasted_iota(I32, (SLOT_ROWS, tn), 0)
        pw = jnp.zeros((SLOT_ROWS, tn), F32)
        for kk in range(TOP_K):
            pw = pw + jnp.where(slot == pos[kk], wts[kk], 0.0)
        p = jnp.where(pw > 0.0, 1.0, 0.0)
        xs_ref[pl.ds(r0, SLOT_ROWS), :] = jnp.dot(p.astype(BF16), h, preferred_element_type=F32).astype(BF16)
        ws_ref[pl.ds(r0, SLOT_ROWS), :] = jnp.broadcast_to(jnp.sum(pw, axis=1, keepdims=True), (SLOT_ROWS, LANES))
        return carry

    lax.fori_loop(0, n_slots // SLOT_ROWS, rows, 0)


def _chunk_tables(cnt, n_slots, n_blocks):
    nt, ne = cnt.shape
    per_blk = MOE_BLOCK // MOE_CHUNK
    nch = -(-cnt // MOE_CHUNK)
    seg0 = (jnp.cumsum(nch, axis=1) - nch) + (jnp.arange(nt) * (n_slots // MOE_CHUNK))[:, None]
    cum_t = jnp.cumsum(nch, axis=0)
    total = cum_t[-1]
    blocks = -(-total // per_blk)
    blk_end = jnp.cumsum(blocks)
    blk = jnp.arange(n_blocks)
    blk_e = jnp.minimum(jnp.sum(blk_end[None, :] <= blk[:, None], axis=1), ne - 1).astype(I32)
    nact = blk_end[-1].reshape(1).astype(I32)
    pick_e = blk_e[:, None] == jnp.arange(ne)[None, :]
    of_e = lambda tab: jnp.sum(jnp.where(pick_e, tab[None, :], 0), axis=1)
    col_e = lambda tab: jnp.sum(jnp.where(pick_e[:, None, :], tab[None, :, :], 0), axis=2)
    q = ((blk - of_e(blk_end - blocks)) * per_blk)[:, None] + jnp.arange(per_blk)[None, :]
    live = (q < of_e(total)[:, None]) & (blk < nact[0])[:, None]
    t_n = jnp.minimum(jnp.sum(col_e(cum_t)[:, None, :] <= q[:, :, None], axis=2), nt - 1)
    pick_t = t_n[:, :, None] == jnp.arange(nt)[None, None, :]
    of_t = lambda tab: jnp.sum(jnp.where(pick_t, tab[:, None, :], 0), axis=2)
    src = jnp.where(live, of_t(col_e(seg0)) + q - of_t(col_e(cum_t - nch)), 0)
    n = blk[:, None] * per_blk + jnp.arange(per_blk)[None, :]
    spare = nt * (n_slots // MOE_CHUNK) + n % (N_OBUF * per_blk)
    dst = jnp.where(live, src, spare)
    return blk_e, nact, src.reshape(-1).astype(I32), dst.reshape(-1).astype(I32)


N_OBUF = 2


def _moe_kernel(be_ref, nact_ref, src_ref, dst_ref, xs_ref, ws_ref, wgu_ref, bgu_ref, wd_ref, bd_ref,
                out_ref, xbuf, wbuf, obuf, gsem, ssem, wgu_bf, wd_bf, *, dff):
    i = pl.program_id(0)
    nact = nact_ref[0]
    per_blk = MOE_BLOCK // MOE_CHUNK
    rows = lambda c: pl.ds(pl.multiple_of(c * MOE_CHUNK, MOE_CHUNK), MOE_CHUNK)

    def gathers(blk, slot):
        copies = []
        for m in range(per_blk):
            c = src_ref[blk * per_blk + m]
            copies.append(pltpu.make_async_copy(xs_ref.at[rows(c), :], xbuf.at[slot, rows(m), :], gsem.at[slot]))
            copies.append(pltpu.make_async_copy(ws_ref.at[rows(c), :], wbuf.at[slot, rows(m), :], gsem.at[slot]))
        return copies

    def scatters(blk, slot):
        return [pltpu.make_async_copy(obuf.at[slot, rows(m), :], out_ref.at[rows(dst_ref[blk * per_blk + m]), :],
                                      ssem.at[slot]) for m in range(per_blk)]

    @pl.when(i == 0)
    def _():
        for cp in gathers(0, 0):
            cp.start()

    @pl.when(i < nact)
    def _():
        slot = i % 2
        for cp in gathers(i, slot):
            cp.wait()

        @pl.when(i + 1 < nact)
        def _():
            for cp in gathers(i + 1, 1 - slot):
                cp.start()

        @pl.when((i == 0) | (be_ref[i] != be_ref[jnp.maximum(i - 1, 0)]))
        def _():
            wgu_bf[...] = wgu_ref[...].astype(BF16)
            wd_bf[...] = wd_ref[...].astype(BF16)

        gu = jnp.dot(xbuf[slot], wgu_bf[...], preferred_element_type=F32) + bgu_ref[...]
        gate = jnp.minimum(gu[:, :dff], SWIGLU_LIMIT)
        up = jnp.clip(gu[:, dff:], -SWIGLU_LIMIT, SWIGLU_LIMIT)
        act = (up + 1.0) * gate * _sigmoid(gate * SWIGLU_ALPHA)
        res = jnp.dot(act.astype(BF16), wd_bf[...], preferred_element_type=F32) + bd_ref[...]
        obuf[slot] = (res * wbuf[slot][:, 0:1]).astype(BF16)

        @pl.when(i >= 1)
        def _():
            for cp in scatters(i - 1, 1 - slot):
                cp.wait()

        for cp in scatters(i, slot):
            cp.start()

        @pl.when(i == nact - 1)
        def _():
            for cp in scatters(i, slot):
                cp.wait()


def _moe(blk_e, nact, src, dst, xs, ws, w_gu, b_gu, w_down, b_down, n_blocks):
    ne, d, dff2 = w_gu.shape
    dff = dff2 // 2
    blk = MOE_BLOCK
    grid_spec = pltpu.PrefetchScalarGridSpec(
        num_scalar_prefetch=4,
        grid=(n_blocks,),
        in_specs=[pl.BlockSpec(memory_space=pl.ANY), pl.BlockSpec(memory_space=pl.ANY),
                  pl.BlockSpec((None, d, dff2), lambda i, be, *_: (be[i], 0, 0)),
                  pl.BlockSpec((None, 1, dff2), lambda i, be, *_: (be[i], 0, 0)),
                  pl.BlockSpec((None, dff, d), lambda i, be, *_: (be[i], 0, 0)),
                  pl.BlockSpec((None, 1, d), lambda i, be, *_: (be[i], 0, 0))],
        out_specs=pl.BlockSpec(memory_space=pl.ANY),
        scratch_shapes=[pltpu.VMEM((2, blk, d), BF16), pltpu.VMEM((2, blk, LANES), F32),
                        pltpu.VMEM((N_OBUF, blk, d), BF16),
                        pltpu.SemaphoreType.DMA((2,)), pltpu.SemaphoreType.DMA((N_OBUF,)),
                        pltpu.VMEM((d, dff2), BF16), pltpu.VMEM((dff, d), BF16)],
    )
    return pl.pallas_call(
        functools.partial(_moe_kernel, dff=dff),
        grid_spec=grid_spec,
        out_shape=jax.ShapeDtypeStruct(xs.shape, BF16),
        compiler_params=_cparams("arbitrary"),
        name="moe",
    )(blk_e, nact, src, dst, xs, ws, w_gu, b_gu.reshape(ne, 1, dff2), w_down, b_down.reshape(ne, 1, d))


def _combine_kernel(pos_ref, used_ref, ys_ref, x1_ref, gt_ref, o_ref, p_ref, *, n_slots):
    tn = x1_ref.shape[0]
    filled = lax.broadcasted_iota(I32, (n_slots, 1), 0) < used_ref[0:1, 0:1]
    ys = jnp.where(filled, ys_ref[...], jnp.zeros((), BF16))
    lane = lax.broadcasted_iota(I32, (tn, SLOT_ROWS), 1)
    pos = [jnp.broadcast_to(pos_ref[:, kk:kk + 1], (tn, SLOT_ROWS)) - lane for kk in range(TOP_K)]

    def cols(c, carry):
        c0 = pl.multiple_of(c * SLOT_ROWS, SLOT_ROWS)
        p = jnp.zeros((tn, SLOT_ROWS), F32)
        for kk in range(TOP_K):
            p = p + jnp.where(pos[kk] == c0, 1.0, 0.0)
        p_ref[:, pl.ds(c0, SLOT_ROWS)] = p.astype(BF16)
        return carry

    lax.fori_loop(0, n_slots // SLOT_ROWS, cols, 0)
    o_ref[...] = x1_ref[...] + gt_ref[...] * jnp.dot(p_ref[...], ys, preferred_element_type=F32)


def _combine(pos_t, used, ys, x1, mod3, tile0, rows, n_slots):
    nb, t, d = x1.shape
    rmod = mod3.shape[1]
    nt = t // rows
    per_tile = ROUTE_TILE // rows
    row = lambda b, i: (b, i, 0)
    gate = pl.BlockSpec((None, rmod, d), (lambda b, i: (b, 0, 5)) if rmod == 1 else (lambda b, i: (b, i, 5)))
    return pl.pallas_call(
        functools.partial(_combine_kernel, n_slots=n_slots),
        grid=(nb, nt),
        in_specs=[pl.BlockSpec((rows, TOP_K), lambda b, i: (b * nt + i, 0)),
                  pl.BlockSpec((None, SUBLANES, LANES), lambda b, i: (tile0 + (b * nt + i) // per_tile, 0, 0)),
                  pl.BlockSpec((n_slots, d), lambda b, i: (tile0 + (b * nt + i) // per_tile, 0)),
                  pl.BlockSpec((None, rows, d), row), gate],
        out_specs=pl.BlockSpec((None, rows, d), row),
        out_shape=jax.ShapeDtypeStruct((nb, t, d), F32),
        scratch_shapes=[pltpu.VMEM((rows, n_slots), BF16)],
        compiler_params=_cparams("parallel", "parallel"),
        name="combine",
    )(pos_t, used, ys, x1, mod3)


def _rope_tables(pos, n_heads):
    half = HEAD_DIM // 2
    inv_freq = 1.0 / (ROPE_THETA ** (jnp.arange(0, HEAD_DIM, 2, dtype=F32) / HEAD_DIM))
    ang = pos.astype(F32)[:, None] * inv_freq[None, :]
    cos, sin = jnp.cos(ang), jnp.sin(ang)
    del half
    return (jnp.tile(jnp.concatenate([cos, cos], axis=-1), (1, n_heads)),
            jnp.tile(jnp.concatenate([-sin, sin], axis=-1), (1, n_heads)))


def _block_diag(width, value):
    h = np.arange(width) // HEAD_DIM
    return jnp.asarray(np.where(h[:, None] == h[None, :], value, 0.0), F32)


def kernel(x_prompt, x_sample, cache_k, cache_v, state_wkv, state_shift, c_prompt, c_sample, w_ada, b_ada, norm1_g, norm2_g, w_in, q_norm_g, k_norm_g, rwkv_mu, rwkv_w0, rwkv_w2, rwkv_a0, rwkv_a2, rwkv_g2, rwkv_k_k, rwkv_k_a, rwkv_r_k, rwkv_ln_w, rwkv_ln_b, w_out, router_w, router_b, moe_w_gu, moe_b_gu, moe_w_down, moe_b_down):
    nbp, t, d = x_prompt.shape
    nbs, ts, _ = x_sample.shape
    depth = w_ada.shape[0]
    assert depth == 1 and ts == 1
    n_heads = cache_k.shape[3]
    aw = n_heads * HEAD_DIM
    rwid = rwkv_w0.shape[1]
    rwc = rwkv_mu.shape[1]
    past = cache_k.shape[2]
    ne = router_w.shape[2]
    keep = min(MAX_WINDOW, t)
    lyr = 0

    w_in_bf = w_in[lyr].astype(BF16)
    wt_bf = w_out[lyr][:aw].astype(BF16)
    wb_bf = w_out[lyr][aw:].astype(BF16)
    g1 = norm1_g[lyr].reshape(1, d)
    g2 = norm2_g[lyr].reshape(1, d)
    qg = jnp.tile(q_norm_g[lyr], n_heads).reshape(1, aw)
    kg = jnp.tile(k_norm_g[lyr], n_heads).reshape(1, aw)
    bd_mean_a = _block_diag(aw, 1.0 / HEAD_DIM)
    bd_mean_r = _block_diag(rwid, 1.0 / HEAD_DIM)
    bd_ones_r = _block_diag(rwid, 1.0)
    dl = rwkv_w2.shape[1]
    w2p = jnp.zeros((LANES, rwid), F32).at[:dl].set(rwkv_w2[lyr])
    a2p = jnp.zeros((LANES, rwid), F32).at[dl:dl + rwkv_a2.shape[1]].set(rwkv_a2[lyr])
    vec = lambda a: a[lyr].reshape(1, -1)
    rwt = router_w[lyr].T
    rb = router_b[lyr].reshape(ne, 1)

    rows_c = nbp + nbs
    rows_pad = -(-rows_c // 8) * 8
    c_all = jnp.zeros((rows_pad, d), F32).at[:nbp].set(c_prompt).at[nbp:rows_c].set(c_sample)
    mod = _ada(c_all, w_ada[lyr], b_ada[lyr])
    mod_p = mod[:nbp].reshape(nbp, 1, 6 * d)
    mod_s = mod[nbp:rows_c].reshape(1, nbs, 6 * d)
    xs3 = x_sample.reshape(1, nbs, d)

    cos_p, sin_p = _rope_tables(jnp.arange(t), n_heads)
    cos_s, sin_s = _rope_tables(jnp.full((nbs,), PAST_LEN), n_heads)
    rw_args = (vec(rwkv_mu), vec(rwkv_w0), w2p, vec(rwkv_a0), a2p, rwkv_g2[lyr], vec(rwkv_k_k), vec(rwkv_k_a), bd_ones_r)
    qp, kp, vp, *pre_p, tail_p, kt_p, vt_p = _inproj(x_prompt, mod_p, g1, w_in_bf, cos_p, sin_p, qg, kg, bd_mean_a,
                                                     ROW_TILE, keep=keep, rw=(jnp.zeros((nbp, 1, rwc), F32), *rw_args))
    qs, ks, vs, rws = _inproj(xs3, mod_s, g1, w_in[lyr], cos_s, sin_s, qg, kg, bd_mean_a, nbs)

    attn_p = _attn_prompt(qp, kp, vp)
    as_rows = lambda a: a.reshape(nbs, 1, -1)
    cache_t = lambda cch: jnp.transpose(cch[lyr], (0, 2, 3, 1)).reshape(nbs, aw, past)
    attn_s = _attn_sample(as_rows(qs), as_rows(ks), as_rows(vs), cache_t(cache_k), cache_t(cache_v))

    pre_s = _rwprep(rws, state_shift[lyr].reshape(1, nbs, rwc), *rw_args, tm=nbs, whole_prev=True)
    r_p, ld_p, k_p, v_p, al_p, be_p, g_p = pre_p
    y_p, wkv_p = _rwchunk(r_p, ld_p, k_p, v_p, al_p, be_p)
    r_s, ld_s, k_s, v_s, al_s, be_s, g_s = pre_s
    y_s, wkv_s = _rwstep(*(as_rows(a) for a in (r_s, ld_s, k_s, v_s, al_s, be_s)), state_wkv[lyr])
    y_s = y_s.reshape(1, nbs, rwid)

    n_p = nbp * t
    n_valid = n_p + nbs
    assert ROW_TILE == ROUTE_TILE and n_p % ROUTE_TILE == 0 and nbs <= ROUTE_TILE and d == SUBLANES * LANES
    n_tiles = n_p // ROUTE_TILE + 1
    n_slots = _slot_capacity(ne)
    n_rows = n_tiles * n_slots + N_OBUF * MOE_BLOCK
    op_args = (rwkv_r_k[lyr].reshape(1, rwid), vec(rwkv_ln_w), vec(rwkv_ln_b), bd_mean_r, bd_ones_r, rwt, rb)
    x1_p, pos_p, _, cnt_p, *bufs = _outproj(x_prompt, attn_p, y_p, r_p, k_p, v_p, g_p, mod_p, g2, wt_bf, wb_bf,
                                            *op_args, tm=ROW_TILE, n_slots=n_slots, n_rows=n_rows)
    x1_s, pos_s, _, cnt_s, xs, ws = _outproj(xs3, attn_s.reshape(1, nbs, aw), y_s, r_s, k_s, v_s, g_s, mod_s, g2,
                                             w_out[lyr][:aw], w_out[lyr][aw:], *op_args, tm=nbs, n_slots=n_slots,
                                             n_rows=n_rows, tile0=n_p // ROUTE_TILE, bufs=bufs)

    cnt = jnp.concatenate([cnt_p, cnt_s])[:, :, 0].astype(I32)
    per_blk = MOE_BLOCK // MOE_CHUNK
    n_blocks = -(-(n_valid * TOP_K // MOE_CHUNK + n_tiles * ne) // per_blk) + ne
    blk_e, nact, src, dst = _chunk_tables(cnt, n_slots, n_blocks)
    ys = _moe(blk_e, nact, src, dst, xs, ws, moe_w_gu[lyr], moe_b_gu[lyr], moe_w_down[lyr], moe_b_down[lyr], n_blocks)
    used = jnp.sum(-(-cnt // MOE_CHUNK) * MOE_CHUNK, axis=1)
    used = jnp.broadcast_to(used[:, None, None], (n_tiles, SUBLANES, LANES))
    y_prompt = _combine(pos_p.T, used, ys, x1_p, mod_p, 0, ROUTE_TILE, n_slots)
    y_sample = _combine(pos_s.T, used, ys, x1_s, mod_s, n_p // ROUTE_TILE, nbs, n_slots)

    kept = lambda a: jnp.transpose(a.reshape(nbp, n_heads, HEAD_DIM, keep), (0, 3, 1, 2))[None]
    return (y_prompt, y_sample.reshape(nbs, ts, d), kept(kt_p), kept(vt_p), wkv_p[None], tail_p[:, SUBLANES - 1][None],
            ks.reshape(nbs, ts, n_heads, HEAD_DIM)[None], vs.reshape(nbs, ts, n_heads, HEAD_DIM)[None],
            wkv_s[None], rws.reshape(nbs, rwc)[None])
```

```python
import functools

import numpy as np
import jax
import jax.numpy as jnp
from jax import lax
from jax.experimental import pallas as pl
from jax.experimental.pallas import tpu as pltpu

F32 = jnp.float32
BF16 = jnp.bfloat16
I32 = jnp.int32

HEAD_DIM = 64
LANES = 128
SUBLANES = 8
DILATED_PATTERNS = ((128, 1), (512, 4), (2048, 16))
WINDOW_STEPS = 128
MAX_WINDOW = 2048
PAST_LEN = 16384
ROPE_THETA = 10000.0
NORM_EPS = 1e-6
GN_EPS = 64e-5
TOP_K = 4
SWIGLU_ALPHA = 1.702
SWIGLU_LIMIT = 7.0
RW_CHUNK = 128
RW_PAIRS = 2
MOE_BLOCK = 512
MOE_CHUNK = 16
SLOT_ROWS = 256
ATTN_UNITS = 8
ROW_TILE = 512
ROUTE_TILE = 512
VMEM_LIMIT = 56 * 1024 * 1024
NEG_BIG = -1e30


def _cparams(*sem):
    return pltpu.CompilerParams(dimension_semantics=sem, vmem_limit_bytes=VMEM_LIMIT)


def _dot(a, b):
    return jnp.dot(a.astype(BF16), b.astype(BF16), preferred_element_type=F32)


def _split2(a):
    hi = a.astype(BF16)
    lo = (a - hi.astype(F32)).astype(BF16)
    return hi, lo


def _split3(a):
    hi = a.astype(BF16)
    r1 = a - hi.astype(F32)
    mid = r1.astype(BF16)
    lo = (r1 - mid.astype(F32)).astype(BF16)
    return hi, mid, lo


def _dot_x(a, e):
    e = e.astype(BF16)
    hi, lo = _split2(a)
    return jnp.dot(hi, e, preferred_element_type=F32) + jnp.dot(lo, e, preferred_element_type=F32)


def _xdot(e, a):
    e = e.astype(BF16)
    hi, mid, lo = _split3(a)
    return (jnp.dot(e, hi, preferred_element_type=F32) + jnp.dot(e, mid, preferred_element_type=F32)
            + jnp.dot(e, lo, preferred_element_type=F32))


def _dot3(a, b):
    ah, al = _split2(a)
    bh, bl = _split2(b)
    return (jnp.dot(ah, bh, preferred_element_type=F32) + jnp.dot(ah, bl, preferred_element_type=F32)
            + jnp.dot(al, bh, preferred_element_type=F32))


def _dot3_nt(a, b):
    ah, al = _split2(a)
    bh, bl = _split2(b)
    dn = (((1,), (1,)), ((), ()))
    return (lax.dot_general(ah, bh, dn, preferred_element_type=F32)
            + lax.dot_general(ah, bl, dn, preferred_element_type=F32)
            + lax.dot_general(al, bh, dn, preferred_element_type=F32))


def _mm(a, w):
    if w.dtype == BF16:
        return jnp.dot(a.astype(BF16), w, preferred_element_type=F32)
    return _dot3(a, w)


def _sigmoid(x):
    return 1.0 / (1.0 + jnp.exp(-x))


def _ada_kernel(c_ref, w_ref, b_ref, o_ref):
    c = c_ref[...]
    o_ref[...] = _dot3(c * _sigmoid(c), w_ref[...]) + b_ref[...]


def _ada(c_all, w_ada, b_ada):
    rows, d = c_all.shape
    n = w_ada.shape[1]
    tn = n // 4
    return pl.pallas_call(
        _ada_kernel,
        grid=(n // tn,),
        in_specs=[pl.BlockSpec((rows, d), lambda j: (0, 0)),
                  pl.BlockSpec((d, tn), lambda j: (0, j)),
                  pl.BlockSpec((1, tn), lambda j: (0, j))],
        out_specs=pl.BlockSpec((rows, tn), lambda j: (0, j)),
        out_shape=jax.ShapeDtypeStruct((rows, n), F32),
        compiler_params=_cparams("arbitrary"),
        name="ada",
    )(c_all, w_ada, b_ada.reshape(1, n))


N_RW_PARAMS = 9
N_RW_FEATS = 7


def _inproj_kernel(x_ref, sh_ref, sc_ref, g_ref, w_ref, cos_ref, sin_ref, qg_ref, kg_ref, bd_ref, *rest,
                   aw, first_kept, n_kept, rw_fused):
    n_in = 1 + N_RW_PARAMS if rw_fused else 0
    rw_in, outs = rest[:n_in], rest[n_in:]
    q_ref, k_ref, v_ref = outs[:3]
    n_rw_out = N_RW_FEATS + 1 if rw_fused else 1
    rw_out = outs[3:3 + n_rw_out]
    maybe_kv_t = outs[3 + n_rw_out:3 + n_rw_out + n_kept]
    x = x_ref[...]
    ms = jnp.mean(x * x, axis=-1, keepdims=True)
    h = x * lax.rsqrt(ms + NORM_EPS) * g_ref[...] * (1.0 + sc_ref[...]) + sh_ref[...]
    proj = _mm(h, w_ref[...])
    cos = cos_ref[...]
    sin = sin_ref[...]
    lane = lax.broadcasted_iota(I32, (1, aw), 1)
    first_half = (lane % HEAD_DIM) < (HEAD_DIM // 2)
    bd = bd_ref[...]

    def norm_rope(t, g):
        tn = t * lax.rsqrt((_dot if w_ref.dtype == BF16 else _dot_x)(t * t, bd) + NORM_EPS) * g
        rot = jnp.where(first_half, pltpu.roll(tn, aw - HEAD_DIM // 2, 1), pltpu.roll(tn, HEAD_DIM // 2, 1))
        return tn * cos + rot * sin

    scale = 1.0 / np.sqrt(HEAD_DIM).astype(np.float32)
    q_ref[...] = norm_rope(proj[:, :aw], qg_ref[...]) * scale
    k = norm_rope(proj[:, aw:2 * aw], kg_ref[...])
    v = proj[:, 2 * aw:3 * aw]
    k_ref[...] = k
    v_ref[...] = v
    rw = proj[:, 3 * aw:]
    if rw_fused:
        carry_ref = outs[-1]
        tail = rw[rw.shape[0] - SUBLANES:, :]
        before = jnp.where(pl.program_id(1) == 0, rw_in[0][...], carry_ref[SUBLANES - 1:SUBLANES, :])
        for ref, val in zip(rw_out, _rw_features(rw, _shifted(rw, before), *rw_in[1:])):
            ref[...] = val
        carry_ref[...] = tail
        rw_out[-1][...] = tail
    else:
        rw_out[0][...] = rw
    if maybe_kv_t:
        kt_ref, vt_ref = maybe_kv_t
        kept = pl.program_id(1) >= first_kept

        @pl.when(kept)
        def _():
            kt_ref[...] = k.T
            vt_ref[...] = v.T

        @pl.when(jnp.logical_not(kept))
        def _():
            kt_ref[...] = jnp.zeros_like(kt_ref)
            vt_ref[...] = jnp.zeros_like(vt_ref)


def _inproj(x3, mod3, g1, w_in_bf, cos_t, sin_t, qg, kg, bd_mean, tm, keep=0, rw=None):
    nb, t, d = x3.shape
    r = mod3.shape[1]
    ncol = w_in_bf.shape[1]
    aw = cos_t.shape[1]
    rwc = ncol - 3 * aw
    grid = (nb, t // tm)
    row = lambda b, i: (b, i, 0)
    const = lambda b, i: (0, 0)
    once = dict(pipeline_mode=pl.Buffered(1)) if rw else {}
    mod_spec = lambda s: pl.BlockSpec((None, r, d), (lambda b, i: (b, 0, s)) if r == 1 else (lambda b, i: (b, i, s)))
    out_specs = [pl.BlockSpec((None, tm, aw), row)] * 3
    out_shape = [jax.ShapeDtypeStruct((nb, t, aw), F32)] * 3
    rw_in, rw_specs, scratch = (), [], []
    if rw:
        rw_in = tuple(rw)
        rwid = rw_in[2].shape[1]
        rw_specs = [pl.BlockSpec((None, 1, rwc), lambda b, i: (b, 0, 0))] + [
            pl.BlockSpec(a.shape, const, **once) for a in rw_in[1:]]
        out_specs += [pl.BlockSpec((None, tm, rwid), row)] * N_RW_FEATS + [
            pl.BlockSpec((None, SUBLANES, rwc), lambda b, i: (b, 0, 0))]
        out_shape += [jax.ShapeDtypeStruct((nb, t, rwid), F32)] * N_RW_FEATS + [
            jax.ShapeDtypeStruct((nb, SUBLANES, rwc), F32)]
        scratch = [pltpu.VMEM((SUBLANES, rwc), F32)]
    else:
        out_specs += [pl.BlockSpec((None, tm, rwc), row)]
        out_shape += [jax.ShapeDtypeStruct((nb, t, rwc), F32)]
    first_kept = (t - keep) // tm
    if keep:
        assert keep % tm == 0 and (t - keep) % tm == 0
        kept_spec = pl.BlockSpec((None, aw, tm), lambda b, i: (b, 0, jnp.maximum(i - first_kept, 0)))
        out_specs += [kept_spec, kept_spec]
        out_shape += [jax.ShapeDtypeStruct((nb, aw, keep), F32)] * 2
    outs = pl.pallas_call(
        functools.partial(_inproj_kernel, aw=aw, first_kept=first_kept, n_kept=2 if keep else 0, rw_fused=bool(rw)),
        grid=grid,
        in_specs=[pl.BlockSpec((None, tm, d), row), mod_spec(0), mod_spec(1),
                  pl.BlockSpec((1, d), const), pl.BlockSpec((d, ncol), const, **once),
                  pl.BlockSpec((tm, aw), lambda b, i: (i, 0)), pl.BlockSpec((tm, aw), lambda b, i: (i, 0)),
                  pl.BlockSpec((1, aw), const), pl.BlockSpec((1, aw), const),
                  pl.BlockSpec((aw, aw), const, **once)] + rw_specs,
        out_specs=out_specs,
        out_shape=out_shape,
        scratch_shapes=scratch,
        compiler_params=_cparams("parallel", "arbitrary"),
        name="inproj",
    )(x3, mod3, mod3, g1, w_in_bf, cos_t, sin_t, qg, kg, bd_mean, *rw_in)
    return outs


def _attn_prompt_kernel(q_ref, k_ref, v_ref, o_ref, kp_ref, vp_ref, m_ref, l_ref, acc_ref, *, t, pad):
    nq = WINDOW_STEPS
    nk = 2 * WINDOW_STEPS
    kp_ref[pl.ds(0, pad), :] = jnp.zeros((pad, LANES), F32)
    vp_ref[pl.ds(0, pad), :] = jnp.zeros((pad, LANES), F32)
    kp_ref[pl.ds(pad, t), :] = k_ref[...]
    vp_ref[pl.ds(pad, t), :] = v_ref[...]
    head0 = lax.broadcasted_iota(I32, (1, LANES), 1) < HEAD_DIM
    qi = lax.broadcasted_iota(I32, (nq, nk), 0)
    kj = lax.broadcasted_iota(I32, (nq, nk), 1)
    steps_back = qi + nq - kj
    band = (steps_back >= 0) & (steps_back <= WINDOW_STEPS)
    has_past = kj >= nq

    hsels = (head0, jnp.logical_not(head0))
    nt = (((1,), (1,)), ((), ()))

    for p, (_, d) in enumerate(DILATED_PATTERNS):
        def units(g, carry, p=p, d=d):
            rows_q, kb, vb, valid, qh = [], [], [], [], []
            for j in range(ATTN_UNITS):
                u = g * ATTN_UNITS + j
                res = u % d
                blk = u // d
                q_start = res + d * nq * blk
                k_start = pad + q_start - d * nq
                if d == 1:
                    rows_q.append(pl.ds(q_start, nq))
                    rows_k = pl.ds(k_start, nk)
                else:
                    rows_q.append(pl.ds(q_start, nq, stride=d))
                    rows_k = pl.ds(k_start, nk, stride=d)
                q = q_ref[rows_q[j], :]
                kb.append(kp_ref[rows_k, :].astype(BF16))
                vb.append(vp_ref[rows_k, :].astype(BF16))
                valid.append(band & (has_past | (blk > 0)))
                qh.append([jnp.where(hsel, q, 0.0).astype(BF16) for hsel in hsels])
            chains = [(j, h) for j in range(ATTN_UNITS) for h in range(2)]
            s = [lax.dot_general(qh[j][h], kb[j], nt, preferred_element_type=F32) for j, h in chains]
            s = [jnp.where(valid[j], sc, NEG_BIG) for (j, h), sc in zip(chains, s)]
            mx = [jnp.max(sc, axis=-1, keepdims=True) for sc in s]
            e = [jnp.exp(sc - m) for sc, m in zip(s, mx)]
            den = [jnp.sum(ec, axis=-1, keepdims=True) for ec in e]
            o = [jnp.dot(ec.astype(BF16), vb[j], preferred_element_type=F32) for (j, h), ec in zip(chains, e)]
            for j in range(ATTN_UNITS):
                m_ref[p, rows_q[j], :] = jnp.where(head0, mx[2 * j], mx[2 * j + 1])
                l_ref[p, rows_q[j], :] = jnp.where(head0, den[2 * j], den[2 * j + 1])
                acc_ref[p, rows_q[j], :] = jnp.where(head0, o[2 * j], o[2 * j + 1])
            return carry

        lax.fori_loop(0, t // nq // ATTN_UNITS, units, 0)

    rows = 256

    def merge(i, carry):
        sl = pl.ds(pl.multiple_of(i * rows, rows), rows)
        m0, m1, m2 = m_ref[0, sl, :], m_ref[1, sl, :], m_ref[2, sl, :]
        mm = jnp.maximum(jnp.maximum(m0, m1), m2)
        w0, w1, w2 = jnp.exp(m0 - mm), jnp.exp(m1 - mm), jnp.exp(m2 - mm)
        num = w0 * acc_ref[0, sl, :] + w1 * acc_ref[1, sl, :] + w2 * acc_ref[2, sl, :]
        den = w0 * l_ref[0, sl, :] + w1 * l_ref[1, sl, :] + w2 * l_ref[2, sl, :]
        o_ref[sl, :] = num / den
        return carry

    lax.fori_loop(0, t // rows, merge, 0)


def _attn_prompt(q, k, v):
    nb, t, aw = q.shape
    pad = MAX_WINDOW
    assert t % MAX_WINDOW == 0
    spec = pl.BlockSpec((None, t, LANES), lambda b, hp: (b, 0, hp))
    return pl.pallas_call(
        functools.partial(_attn_prompt_kernel, t=t, pad=pad),
        grid=(nb, aw // LANES),
        in_specs=[spec, spec, spec],
        out_specs=spec,
        out_shape=jax.ShapeDtypeStruct((nb, t, aw), F32),
        scratch_shapes=[pltpu.VMEM((pad + t, LANES), F32), pltpu.VMEM((pad + t, LANES), F32),
                        pltpu.VMEM((3, t, LANES), F32), pltpu.VMEM((3, t, LANES), F32),
                        pltpu.VMEM((3, t, LANES), F32)],
        compiler_params=_cparams("parallel", "parallel"),
        name="attn_prompt",
    )(q, k, v)


def _attn_sample_kernel(q_ref, kn_ref, vn_ref, kt_ref, vt_ref, o_ref, *, nh, w):
    aw = nh * HEAD_DIM
    dist = w - lax.broadcasted_iota(I32, (1, w), 1)
    mult = jnp.zeros((1, w), F32)
    for win, d in DILATED_PATTERNS:
        mult = mult + jnp.where((dist % d == 0) & (dist <= win), 1.0, 0.0)
    n_pat = float(len(DILATED_PATTERNS))
    q_col = _col(q_ref[...], aw)
    kn_col = _col(kn_ref[...], aw)
    vn_col = _col(vn_ref[...], aw)
    heads = range(nh)
    hs = [pl.ds(h * HEAD_DIM, HEAD_DIM) for h in heads]
    cut = lambda col, h: col[h * HEAD_DIM:(h + 1) * HEAD_DIM]
    s = [jnp.sum(kt_ref[hs[h], :] * cut(q_col, h), axis=0, keepdims=True) for h in heads]
    s_self = [jnp.sum(cut(q_col, h) * cut(kn_col, h), axis=0, keepdims=True) for h in heads]
    s = [jnp.where(mult > 0.0, sh, NEG_BIG) for sh in s]
    mx = [jnp.maximum(jnp.max(sh, axis=1, keepdims=True), ss) for sh, ss in zip(s, s_self)]
    pr = [mult * jnp.exp(sh - m) for sh, m in zip(s, mx)]
    p_self = [n_pat * jnp.exp(ss - m) for ss, m in zip(s_self, mx)]
    den = [jnp.sum(p, axis=1, keepdims=True) + ps for p, ps in zip(pr, p_self)]
    num = [jnp.sum(vt_ref[hs[h], :] * pr[h], axis=1, keepdims=True) + p_self[h] * cut(vn_col, h) for h in heads]
    outs = [n / dn for n, dn in zip(num, den)]
    o_ref[...] = _row(jnp.concatenate(outs, axis=0), aw)


def _attn_sample(q, kn, vn, cache_kt, cache_vt):
    nb, _, aw = q.shape
    w = cache_kt.shape[2]
    assert w == MAX_WINDOW
    row = pl.BlockSpec((None, 1, aw), lambda i: (i, 0, 0))
    mat = pl.BlockSpec((None, aw, w), lambda i: (i, 0, 0))
    return pl.pallas_call(
        functools.partial(_attn_sample_kernel, nh=aw // HEAD_DIM, w=w),
        grid=(nb,),
        in_specs=[row, row, row, mat, mat],
        out_specs=row,
        out_shape=jax.ShapeDtypeStruct((nb, 1, aw), F32),
        compiler_params=_cparams("parallel"),
        name="attn_sample",
    )(q, kn, vn, cache_kt, cache_vt)


def _rw_features(p, prev, mu_ref, w0_ref, w2_ref, a0_ref, a2_ref, g2_ref, kk_ref, ka_ref, bd_ref):
    rwid = w0_ref.shape[1]
    xs = p + mu_ref[...] * (prev - p)
    r = xs[:, :rwid]
    k = xs[:, rwid:2 * rwid]
    v = xs[:, 2 * rwid:3 * rwid]
    xwa = xs[:, 3 * rwid:3 * rwid + LANES]
    xg = xs[:, 3 * rwid + LANES:]
    z = w0_ref[...] + _dot3(jnp.tanh(xwa), w2_ref[...])
    softplus_neg = jnp.maximum(-z, 0.0) + jnp.log(1.0 + jnp.exp(-jnp.abs(z)))
    w = -softplus_neg - 0.5
    a = _sigmoid(a0_ref[...] + _dot3(xwa, a2_ref[...]))
    g = _dot3(_sigmoid(xg), g2_ref[...])
    kk = k * kk_ref[...]
    norm = jnp.sqrt(_dot_x(kk * kk, bd_ref[...]))
    kk = kk / jnp.maximum(norm, 1e-12)
    return r, -jnp.exp(w), k * (1.0 + (a - 1.0) * ka_ref[...]), v, -kk, kk * a, g


def _shifted(p, before):
    rowi = lax.broadcasted_iota(I32, (p.shape[0], 1), 0)
    return jnp.where(rowi == 0, before, pltpu.roll(p, 1, 0))


def _rwprep_kernel(p_ref, prev_ref, first_ref, *refs, whole_prev):
    params, outs = refs[:9], refs[9:]
    p = p_ref[...]
    if whole_prev:
        prev = first_ref[...]
    else:
        prev = _shifted(p, jnp.where(pl.program_id(1) == 0, first_ref[...], prev_ref[7:8, :]))
    for ref, val in zip(outs, _rw_features(p, prev, *params)):
        ref[...] = val


def _rwprep(rw, first, mu, w0, w2p, a0, a2p, g2, k_k, k_a, bd_ones, tm, whole_prev):
    nb, t, rwc = rw.shape
    rwid = w0.shape[1]
    grid = (nb, t // tm)
    row = lambda b, i: (b, i, 0)
    const = lambda b, i: (0, 0)
    if whole_prev:
        prev_spec = pl.BlockSpec((None, tm, rwc), row)
        first_spec = pl.BlockSpec((None, tm, rwc), row)
    else:
        prev_spec = pl.BlockSpec((None, 8, rwc), lambda b, i: (b, jnp.maximum(i * (tm // 8) - 1, 0), 0))
        first_spec = pl.BlockSpec((None, 1, rwc), lambda b, i: (b, 0, 0))
    vec = pl.BlockSpec((1, rwid), const)
    out = pl.BlockSpec((None, tm, rwid), row)
    return pl.pallas_call(
        functools.partial(_rwprep_kernel, whole_prev=whole_prev),
        grid=grid,
        in_specs=[pl.BlockSpec((None, tm, rwc), row), prev_spec, first_spec,
                  pl.BlockSpec((1, rwc), const), vec, pl.BlockSpec((LANES, rwid), const),
                  vec, pl.BlockSpec((LANES, rwid), const), pl.BlockSpec((LANES, rwid), const), vec, vec,
                  pl.BlockSpec((rwid, rwid), const)],
        out_specs=[out] * 7,
        out_shape=[jax.ShapeDtypeStruct((nb, t, rwid), F32)] * 7,
        compiler_params=_cparams("parallel", "parallel"),
        name="rwprep",
    )(rw, rw, first, mu, w0, w2p, a0, a2p, g2, k_k, k_a, bd_ones)


def _rwchunk_kernel(r_ref, ld_ref, k_ref, v_ref, al_ref, be_ref, y_ref, st_ref, z_ref, *, nb, npb):
    c = RW_CHUNK
    ci = pl.program_id(1)

    @pl.when(ci == 0)
    def _():
        z_ref[...] = jnp.zeros_like(z_ref)

    ti = lax.broadcasted_iota(I32, (c, c), 0)
    si = lax.broadcasted_iota(I32, (c, c), 1)
    low_incl = si <= ti
    low_strict = si < ti
    diag = si == ti
    tri = jnp.where(low_incl, 1.0, 0.0).astype(BF16)
    eye = jnp.where(diag, 1.0, 0.0)
    head0 = lax.broadcasted_iota(I32, (1, LANES), 1) < HEAD_DIM
    hsels = (head0, jnp.logical_not(head0))
    same_head = (ti < HEAD_DIM) == (si < HEAD_DIM)
    nt = (((1,), (1,)), ((), ()))
    batches = range(nb * npb)
    chains = [(b, h) for b in batches for h in range(2)]
    lanes_of = lambda b: pl.ds((b // nb) * LANES, LANES)
    r_v, ld_v, k_v, v_v, al_v, be_v = ([ref[b % nb, :, lanes_of(b)] for b in batches]
                                       for ref in (r_ref, ld_ref, k_ref, v_ref, al_ref, be_ref))

    cum = [_xdot(tri, ld_v[b]) for b in batches]
    tot = [cm[c - 1:c, :] for cm in cum]
    e_neg = [jnp.exp(-cm) for cm in cum]
    at = [al_v[b] * jnp.exp(cum[b] - ld_v[b]) for b in batches]
    rt = [r_v[b] * jnp.exp(cum[b]) for b in batches]
    rhs_t = [jnp.concatenate([be_v[b] * e_neg[b], k_v[b] * e_neg[b]], axis=0).astype(BF16) for b in batches]
    vb = [v_v[b].astype(BF16) for b in batches]
    at_h = [jnp.where(hsels[h], at[b], 0.0) for b, h in chains]
    rt_h = [jnp.where(hsels[h], rt[b], 0.0) for b, h in chains]
    a4 = [lax.dot_general(jnp.concatenate([a, r], axis=0).astype(BF16), rhs_t[b], nt, preferred_element_type=F32)
          for (b, h), a, r in zip(chains, at_h, rt_h)]
    a_ab = [jnp.where(low_strict, m[:c, :c], 0.0) for m in a4]
    a_ak = [jnp.where(low_strict, m[:c, c:], 0.0).astype(BF16) for m in a4]
    a_r = [jnp.concatenate([jnp.where(low_incl, m[c:, :c], 0.0), jnp.where(low_incl, m[c:, c:], 0.0)],
                           axis=1).astype(BF16) for m in a4]
    levels = int(np.log2(c))
    pw = [m.astype(BF16) for m in a_ab]
    pw = [jnp.dot(m, m, preferred_element_type=F32).astype(BF16) for m in pw]
    inv = [eye + m for m in a_ab]
    for _ in range(1, levels - 1):
        x2 = [jnp.dot(m, jnp.concatenate([m, i.astype(BF16)], axis=1), preferred_element_type=F32)
              for m, i in zip(pw, inv)]
        pw = [m[:, :c].astype(BF16) for m in x2]
        inv = [i + m[:, c:] for i, m in zip(inv, x2)]
    inv = [i + jnp.dot(m, i.astype(BF16), preferred_element_type=F32) for i, m in zip(inv, pw)]
    akv = [jnp.dot(m, vb[b], preferred_element_type=F32) for (b, h), m in zip(chains, a_ak)]
    x = [_dot(i, jnp.concatenate([kv, a], axis=1)) for i, kv, a in zip(inv, akv, at_h)]
    u0_h = [m[:, :LANES] for m in x]
    at2_h = [m[:, LANES:] for m in x]
    y0_h = [jnp.dot(ar, jnp.concatenate([u0, v_v[b]], axis=0).astype(BF16), preferred_element_type=F32)
            for (b, h), ar, u0 in zip(chains, a_r, u0_h)]
    rt2_h = [r + jnp.dot(ar[:, :c], a2.astype(BF16), preferred_element_type=F32)
             for r, ar, a2 in zip(rt_h, a_r, at2_h)]
    z = [z_ref[b] for b in batches]
    uy = [_dot(jnp.concatenate([at2_h[2 * b] + at2_h[2 * b + 1], rt2_h[2 * b] + rt2_h[2 * b + 1]], axis=0), z[b])
          for b in batches]
    u = [uy[b][:c] + jnp.where(head0, u0_h[2 * b], u0_h[2 * b + 1]) for b in batches]
    for b in batches:
        y_ref[b % nb, :, lanes_of(b)] = uy[b][c:] + jnp.where(head0, y0_h[2 * b], y0_h[2 * b + 1])
    e_end = [jnp.exp(tot[b] - cum[b]) for b in batches]
    lhs_t = [jnp.concatenate([be_v[b] * e_end[b], k_v[b] * e_end[b]], axis=0) for b in batches]
    zadd = [_dot(lhs_t[b].T, jnp.concatenate([u[b], v_v[b]], axis=0)) for b in batches]
    for b in batches:
        dcol = jnp.sum(jnp.where(diag, jnp.broadcast_to(jnp.exp(tot[b]), (c, c)), 0.0), axis=1, keepdims=True)
        z_ref[b] = dcol * z[b] + jnp.where(same_head, zadd[b], 0.0)

    @pl.when(ci == pl.num_programs(1) - 1)
    def _():
        for b in batches:
            s = z_ref[b].T
            st_ref[b % nb, 2 * (b // nb)] = s[:HEAD_DIM, :HEAD_DIM]
            st_ref[b % nb, 2 * (b // nb) + 1] = s[HEAD_DIM:, HEAD_DIM:]


def _rwchunk(r, ld, k, v, al, be):
    nb, t, rwid = r.shape
    c = RW_CHUNK
    npb = RW_PAIRS
    assert t % c == 0 and c == LANES and rwid % (npb * LANES) == 0
    ngrp = rwid // (npb * LANES)
    seq = pl.BlockSpec((nb, c, npb * LANES), lambda hp, ci: (0, ci, hp))
    return pl.pallas_call(
        functools.partial(_rwchunk_kernel, nb=nb, npb=npb),
        grid=(ngrp, t // c),
        in_specs=[seq] * 6,
        out_specs=[seq, pl.BlockSpec((nb, 2 * npb, HEAD_DIM, HEAD_DIM), lambda hp, ci: (0, hp, 0, 0))],
        out_shape=[jax.ShapeDtypeStruct((nb, t, rwid), F32),
                   jax.ShapeDtypeStruct((nb, 2 * npb * ngrp, HEAD_DIM, HEAD_DIM), F32)],
        scratch_shapes=[pltpu.VMEM((nb * npb, LANES, LANES), F32)],
        compiler_params=_cparams("parallel", "arbitrary"),
        name="rwchunk",
    )(r, ld, k, v, al, be)


def _col(row, width):
    return jnp.broadcast_to(row, (LANES, width)).T[:, 0:1]


def _row(col, width):
    return jnp.broadcast_to(col, (width, LANES)).T[0:1, :]


def _rwstep_kernel(r_ref, ld_ref, k_ref, v_ref, al_ref, be_ref, s_ref, y_ref, so_ref, *, bb, nh):
    rwid = nh * HEAD_DIM

    def one(b, carry):
        r, dcy, k, al, be = (ref[b] for ref in (r_ref, ld_ref, k_ref, al_ref, be_ref))
        dcy = jnp.exp(dcy)
        v_col = _col(v_ref[b], rwid)
        heads = range(nh)
        hs = [slice(h * HEAD_DIM, (h + 1) * HEAD_DIM) for h in heads]
        st = [s_ref[b, h] for h in heads]
        sa = [jnp.sum(st[h] * al[:, hs[h]], axis=1, keepdims=True) for h in heads]
        st = [st[h] * dcy[:, hs[h]] + sa[h] * be[:, hs[h]] + v_col[hs[h], :] * k[:, hs[h]] for h in heads]
        for h in heads:
            so_ref[b, h] = st[h]
        ys = [jnp.sum(st[h] * r[:, hs[h]], axis=1, keepdims=True) for h in heads]
        y_ref[b] = _row(jnp.concatenate(ys, axis=0), rwid)
        return carry

    lax.fori_loop(0, bb, one, 0, unroll=2)


def _rwstep(r, ld, k, v, al, be, state):
    nb, _, rwid = r.shape
    nh = state.shape[1]
    bb = 8
    row = pl.BlockSpec((bb, 1, rwid), lambda i: (i, 0, 0))
    st = pl.BlockSpec((bb, nh, HEAD_DIM, HEAD_DIM), lambda i: (i, 0, 0, 0))
    return pl.pallas_call(
        functools.partial(_rwstep_kernel, bb=bb, nh=nh),
        grid=(nb // bb,),
        in_specs=[row] * 6 + [st],
        out_specs=[row, st],
        out_shape=[jax.ShapeDtypeStruct((nb, 1, rwid), F32), jax.ShapeDtypeStruct(state.shape, F32)],
        compiler_params=_cparams("parallel"),
        name="rwstep",
    )(r, ld, k, v, al, be, state)


def _outproj_kernel(x_ref, at_ref, y_ref, r_ref, k_ref, v_ref, g_ref, gt_ref, sh_ref, sc_ref, g2_ref,
                    wt_ref, wb_ref, rk_ref, lw_ref, lb_ref, bdm_ref, bd1_ref, rwt_ref, rb_ref, *rest, n_slots):
    x1_ref, pos_ref, wt4_ref, cnt_ref, xs_ref, ws_ref, hb_ref = rest[-7:]
    y = y_ref[...]
    bdm = bdm_ref[...]
    mean = _dot_x(y, bdm)
    yc = y - mean
    var = _dot_x(yc * yc, bdm)
    yn = yc * lax.rsqrt(var + GN_EPS) * lw_ref[...] + lb_ref[...]
    v = v_ref[...]
    bonus = _dot_x(r_ref[...] * k_ref[...] * rk_ref[...], bd1_ref[...]) * v
    rw = (yn + bonus) * g_ref[...]
    mix = _mm(at_ref[...], wt_ref[...]) + _mm(rw, wb_ref[...])
    x1 = x_ref[...] + gt_ref[...] * mix
    x1_ref[...] = x1
    ms = jnp.mean(x1 * x1, axis=-1, keepdims=True)
    h2 = x1 * lax.rsqrt(ms + NORM_EPS) * g2_ref[...] * (1.0 + sc_ref[...]) + sh_ref[...]
    pos, wts, cnt = _route_tile(_dot3_nt(rwt_ref[...], h2) + rb_ref[...])
    for kk in range(TOP_K):
        pos_ref[pl.ds(kk, 1), :] = pos[kk]
        wt4_ref[pl.ds(kk, 1), :] = wts[kk]
    cnt_ref[...] = jnp.broadcast_to(cnt, cnt_ref.shape)
    hb_ref[...] = h2.astype(BF16)
    used = jnp.sum(jnp.ceil(cnt / MOE_CHUNK) * MOE_CHUNK).astype(I32)
    _sort_tile(pos, wts, hb_ref, xs_ref, ws_ref, (used + SLOT_ROWS - 1) // SLOT_ROWS)


def _outproj(x3, attn, y, r, k, v, g, mod3, g2, wt_bf, wb_bf, r_k, ln_w, ln_b, bd_mean, bd_ones, rwt, rb, tm,
             n_slots, n_rows, tile0=0, bufs=None):
    nb, t, d = x3.shape
    rmod = mod3.shape[1]
    aw = attn.shape[2]
    ne = rwt.shape[0]
    nt = t // tm
    grid = (nb, nt)
    row = lambda b, i: (b, i, 0)
    const = lambda b, i: (0, 0)
    tile = lambda b, i: (tile0 + b * nt + i, 0)
    mod_spec = lambda s: pl.BlockSpec((None, rmod, d), (lambda b, i: (b, 0, s)) if rmod == 1 else (lambda b, i: (b, i, s)))
    half = pl.BlockSpec((None, tm, aw), row)
    vec = pl.BlockSpec((1, aw), const)
    tok4 = pl.BlockSpec((TOP_K, tm), lambda b, i: (0, b * nt + i))
    bufs = () if bufs is None else tuple(bufs)
    n_in = 20
    return pl.pallas_call(
        functools.partial(_outproj_kernel, n_slots=n_slots),
        grid=grid,
        in_specs=[pl.BlockSpec((None, tm, d), row)] + [half] * 6 + [mod_spec(2), mod_spec(3), mod_spec(4),
                  pl.BlockSpec((1, d), const), pl.BlockSpec((aw, d), const), pl.BlockSpec((aw, d), const),
                  vec, vec, vec, pl.BlockSpec((aw, aw), const), pl.BlockSpec((aw, aw), const),
                  pl.BlockSpec((ne, d), const), pl.BlockSpec((ne, 1), const)]
                 + [pl.BlockSpec(memory_space=pl.ANY)] * len(bufs),
        out_specs=[pl.BlockSpec((None, tm, d), row), tok4, tok4,
                   pl.BlockSpec((None, ne, LANES), lambda b, i: (b * nt + i, 0, 0)),
                   pl.BlockSpec((n_slots, d), tile), pl.BlockSpec((n_slots, LANES), tile)],
        out_shape=[jax.ShapeDtypeStruct((nb, t, d), F32), jax.ShapeDtypeStruct((TOP_K, nb * t), I32),
                   jax.ShapeDtypeStruct((TOP_K, nb * t), F32), jax.ShapeDtypeStruct((nb * nt, ne, LANES), F32),
                   jax.ShapeDtypeStruct((n_rows, d), BF16), jax.ShapeDtypeStruct((n_rows, LANES), F32)],
        scratch_shapes=[pltpu.VMEM((tm, d), BF16)],
        input_output_aliases={n_in: 4, n_in + 1: 5} if bufs else {},
        compiler_params=_cparams("parallel", "parallel"),
        name="outproj",
    )(x3, attn, y, r, k, v, g, mod3, mod3, mod3, g2, wt_bf, wb_bf, r_k, ln_w, ln_b, bd_mean, bd_ones, rwt, rb, *bufs)


def _route_tile(lg):
    ne, tn = lg.shape
    eidx = lax.broadcasted_iota(I32, (ne, tn), 0).astype(F32)
    vals, hots = [], []
    for _ in range(TOP_K):
        mx = jnp.max(lg, axis=0, keepdims=True)
        pick = jnp.min(jnp.where(lg == mx, eidx, float(ne)), axis=0, keepdims=True)
        hot = eidx == pick
        vals.append(mx)
        hots.append(hot)
        lg = jnp.where(hot, -jnp.inf, lg)
    ex = [jnp.exp(vv - vals[0]) for vv in vals]
    den = ex[0] + ex[1] + ex[2] + ex[3]
    wts = [e / den for e in ex]
    hot_all = jnp.zeros((ne, tn), F32)
    for hot in hots:
        hot_all = hot_all + jnp.where(hot, 1.0, 0.0)
    ri = lax.broadcasted_iota(I32, (tn, tn), 0)
    cj = lax.broadcasted_iota(I32, (tn, tn), 1)
    upper = jnp.where(ri <= cj, 1.0, 0.0).astype(BF16)
    before = jnp.dot(hot_all.astype(BF16), upper, preferred_element_type=F32) - hot_all
    cnt = jnp.sum(hot_all, axis=1, keepdims=True)
    padded = jnp.ceil(cnt / MOE_CHUNK) * MOE_CHUNK
    er = lax.broadcasted_iota(I32, (ne, ne), 0)
    ec = lax.broadcasted_iota(I32, (ne, ne), 1)
    lower_strict = jnp.where(ec < er, 1.0, 0.0)
    off = _xdot(lower_strict, jnp.broadcast_to(padded, (ne, LANES)))[:, 0:1]
    pos = [jnp.sum(jnp.where(hot, off + before, 0.0), axis=0, keepdims=True).astype(I32) for hot in hots]
    return pos, wts, cnt


def _slot_capacity(ne):
    return -(-(TOP_K * ROUTE_TILE + ne * (MOE_CHUNK - 1)) // SLOT_ROWS) * SLOT_ROWS


def _sort_tile(pos, wts, h_ref, xs_ref, ws_ref, n_steps):
    tn = h_ref.shape[0]

    def rows(c, carry):
        h = h_ref[...]
        r0 = pl.multiple_of(c * SLOT_ROWS, SLOT_ROWS)
        slot = r0 + lax.broadcasted_iota(I32, (SLOT_ROWS, tn), 0)
        pw = jnp.zeros((SLOT_ROWS, tn), F32)
        for kk in range(TOP_K):
            pw = pw + jnp.where(slot == pos[kk], wts[kk], 0.0)
        p = jnp.where(pw > 0.0, 1.0, 0.0)
        xs_ref[pl.ds(r0, SLOT_ROWS), :] = jnp.dot(p.astype(BF16), h, preferred_element_type=F32).astype(BF16)
        ws_ref[pl.ds(r0, SLOT_ROWS), :] = jnp.broadcast_to(jnp.sum(pw, axis=1, keepdims=True), (SLOT_ROWS, LANES))
        return carry

    lax.fori_loop(0, n_steps, rows, 0)


def _chunk_tables(cnt, n_slots, n_blocks):
    nt, ne = cnt.shape
    per_blk = MOE_BLOCK // MOE_CHUNK
    nch = -(-cnt // MOE_CHUNK)
    seg0 = (jnp.cumsum(nch, axis=1) - nch) + (jnp.arange(nt) * (n_slots // MOE_CHUNK))[:, None]
    cum_t = jnp.cumsum(nch, axis=0)
    total = cum_t[-1]
    blocks = -(-total // per_blk)
    blk_end = jnp.cumsum(blocks)
    blk = jnp.arange(n_blocks)
    blk_e = jnp.minimum(jnp.sum(blk_end[None, :] <= blk[:, None], axis=1), ne - 1).astype(I32)
    nact = blk_end[-1].reshape(1).astype(I32)
    pick_e = blk_e[:, None] == jnp.arange(ne)[None, :]
    of_e = lambda tab: jnp.sum(jnp.where(pick_e, tab[None, :], 0), axis=1)
    col_e = lambda tab: jnp.sum(jnp.where(pick_e[:, None, :], tab[None, :, :], 0), axis=2)
    q = ((blk - of_e(blk_end - blocks)) * per_blk)[:, None] + jnp.arange(per_blk)[None, :]
    live = (q < of_e(total)[:, None]) & (blk < nact[0])[:, None]
    t_n = jnp.minimum(jnp.sum(col_e(cum_t)[:, None, :] <= q[:, :, None], axis=2), nt - 1)
    pick_t = t_n[:, :, None] == jnp.arange(nt)[None, None, :]
    of_t = lambda tab: jnp.sum(jnp.where(pick_t, tab[:, None, :], 0), axis=2)
    src = jnp.where(live, of_t(col_e(seg0)) + q - of_t(col_e(cum_t - nch)), 0)
    n = blk[:, None] * per_blk + jnp.arange(per_blk)[None, :]
    spare = nt * (n_slots // MOE_CHUNK) + n % (N_OBUF * per_blk)
    dst = jnp.where(live, src, spare)
    return blk_e, nact, src.reshape(-1).astype(I32), dst.reshape(-1).astype(I32)


N_OBUF = 2


def _moe_kernel(be_ref, nact_ref, src_ref, dst_ref, xs_ref, ws_ref, wgu_ref, bgu_ref, wd_ref, bd_ref,
                out_ref, xbuf, wbuf, obuf, gsem, ssem, wgu_bf, wd_bf, *, dff):
    i = pl.program_id(0)
    nact = nact_ref[0]
    per_blk = MOE_BLOCK // MOE_CHUNK
    rows = lambda c: pl.ds(pl.multiple_of(c * MOE_CHUNK, MOE_CHUNK), MOE_CHUNK)

    def gathers(blk, slot):
        copies = []
        for m in range(per_blk):
            c = src_ref[blk * per_blk + m]
            copies.append(pltpu.make_async_copy(xs_ref.at[rows(c), :], xbuf.at[slot, rows(m), :], gsem.at[slot]))
            copies.append(pltpu.make_async_copy(ws_ref.at[rows(c), :], wbuf.at[slot, rows(m), :], gsem.at[slot]))
        return copies

    def scatters(blk, slot):
        return [pltpu.make_async_copy(obuf.at[slot, rows(m), :], out_ref.at[rows(dst_ref[blk * per_blk + m]), :],
                                      ssem.at[slot]) for m in range(per_blk)]

    @pl.when(i == 0)
    def _():
        for cp in gathers(0, 0):
            cp.start()

    @pl.when(i < nact)
    def _():
        slot = i % 2
        for cp in gathers(i, slot):
            cp.wait()

        @pl.when(i + 1 < nact)
        def _():
            for cp in gathers(i + 1, 1 - slot):
                cp.start()

        @pl.when((i == 0) | (be_ref[i] != be_ref[jnp.maximum(i - 1, 0)]))
        def _():
            wgu_bf[...] = wgu_ref[...].astype(BF16)
            wd_bf[...] = wd_ref[...].astype(BF16)

        gu = jnp.dot(xbuf[slot], wgu_bf[...], preferred_element_type=F32) + bgu_ref[...]
        gate = jnp.minimum(gu[:, :dff], SWIGLU_LIMIT)
        up = jnp.clip(gu[:, dff:], -SWIGLU_LIMIT, SWIGLU_LIMIT)
        act = (up + 1.0) * gate * _sigmoid(gate * SWIGLU_ALPHA)
        res = jnp.dot(act.astype(BF16), wd_bf[...], preferred_element_type=F32) + bd_ref[...]
        obuf[slot] = (res * wbuf[slot][:, 0:1]).astype(BF16)

        @pl.when(i >= 1)
        def _():
            for cp in scatters(i - 1, 1 - slot):
                cp.wait()

        for cp in scatters(i, slot):
            cp.start()

        @pl.when(i == nact - 1)
        def _():
            for cp in scatters(i, slot):
                cp.wait()


def _moe(blk_e, nact, src, dst, xs, ws, w_gu, b_gu, w_down, b_down, n_blocks):
    ne, d, dff2 = w_gu.shape
    dff = dff2 // 2
    blk = MOE_BLOCK
    grid_spec = pltpu.PrefetchScalarGridSpec(
        num_scalar_prefetch=4,
        grid=(n_blocks,),
        in_specs=[pl.BlockSpec(memory_space=pl.ANY), pl.BlockSpec(memory_space=pl.ANY),
                  pl.BlockSpec((None, d, dff2), lambda i, be, *_: (be[i], 0, 0)),
                  pl.BlockSpec((None, 1, dff2), lambda i, be, *_: (be[i], 0, 0)),
                  pl.BlockSpec((None, dff, d), lambda i, be, *_: (be[i], 0, 0)),
                  pl.BlockSpec((None, 1, d), lambda i, be, *_: (be[i], 0, 0))],
        out_specs=pl.BlockSpec(memory_space=pl.ANY),
        scratch_shapes=[pltpu.VMEM((2, blk, d), BF16), pltpu.VMEM((2, blk, LANES), F32),
                        pltpu.VMEM((N_OBUF, blk, d), BF16),
                        pltpu.SemaphoreType.DMA((2,)), pltpu.SemaphoreType.DMA((N_OBUF,)),
                        pltpu.VMEM((d, dff2), BF16), pltpu.VMEM((dff, d), BF16)],
    )
    return pl.pallas_call(
        functools.partial(_moe_kernel, dff=dff),
        grid_spec=grid_spec,
        out_shape=jax.ShapeDtypeStruct(xs.shape, BF16),
        compiler_params=_cparams("arbitrary"),
        name="moe",
    )(blk_e, nact, src, dst, xs, ws, w_gu, b_gu.reshape(ne, 1, dff2), w_down, b_down.reshape(ne, 1, d))


def _combine_kernel(pos_ref, used_ref, ys_ref, x1_ref, gt_ref, o_ref, p_ref, *, n_slots):
    tn = x1_ref.shape[0]
    filled = lax.broadcasted_iota(I32, (n_slots, 1), 0) < used_ref[0:1, 0:1]
    ys = jnp.where(filled, ys_ref[...], jnp.zeros((), BF16))
    lane = lax.broadcasted_iota(I32, (tn, SLOT_ROWS), 1)
    pos = [jnp.broadcast_to(pos_ref[:, kk:kk + 1], (tn, SLOT_ROWS)) - lane for kk in range(TOP_K)]

    def cols(c, carry):
        c0 = pl.multiple_of(c * SLOT_ROWS, SLOT_ROWS)
        p = jnp.zeros((tn, SLOT_ROWS), F32)
        for kk in range(TOP_K):
            p = p + jnp.where(pos[kk] == c0, 1.0, 0.0)
        p_ref[:, pl.ds(c0, SLOT_ROWS)] = p.astype(BF16)
        return carry

    lax.fori_loop(0, n_slots // SLOT_ROWS, cols, 0)
    o_ref[...] = x1_ref[...] + gt_ref[...] * jnp.dot(p_ref[...], ys, preferred_element_type=F32)


def _combine(pos_t, used, ys, x1, mod3, tile0, rows, n_slots):
    nb, t, d = x1.shape
    rmod = mod3.shape[1]
    nt = t // rows
    per_tile = ROUTE_TILE // rows
    row = lambda b, i: (b, i, 0)
    gate = pl.BlockSpec((None, rmod, d), (lambda b, i: (b, 0, 5)) if rmod == 1 else (lambda b, i: (b, i, 5)))
    return pl.pallas_call(
        functools.partial(_combine_kernel, n_slots=n_slots),
        grid=(nb, nt),
        in_specs=[pl.BlockSpec((rows, TOP_K), lambda b, i: (b * nt + i, 0)),
                  pl.BlockSpec((None, SUBLANES, LANES), lambda b, i: (tile0 + (b * nt + i) // per_tile, 0, 0)),
                  pl.BlockSpec((n_slots, d), lambda b, i: (tile0 + (b * nt + i) // per_tile, 0)),
                  pl.BlockSpec((None, rows, d), row), gate],
        out_specs=pl.BlockSpec((None, rows, d), row),
        out_shape=jax.ShapeDtypeStruct((nb, t, d), F32),
        scratch_shapes=[pltpu.VMEM((rows, n_slots), BF16)],
        compiler_params=_cparams("parallel", "parallel"),
        name="combine",
    )(pos_t, used, ys, x1, mod3)


def _rope_tables(pos, n_heads):
    half = HEAD_DIM // 2
    inv_freq = 1.0 / (ROPE_THETA ** (jnp.arange(0, HEAD_DIM, 2, dtype=F32) / HEAD_DIM))
    ang = pos.astype(F32)[:, None] * inv_freq[None, :]
    cos, sin = jnp.cos(ang), jnp.sin(ang)
    del half
    return (jnp.tile(jnp.concatenate([cos, cos], axis=-1), (1, n_heads)),
            jnp.tile(jnp.concatenate([-sin, sin], axis=-1), (1, n_heads)))


def _block_diag(width, value):
    h = np.arange(width) // HEAD_DIM
    return jnp.asarray(np.where(h[:, None] == h[None, :], value, 0.0), F32)


def kernel(x_prompt, x_sample, cache_k, cache_v, state_wkv, state_shift, c_prompt, c_sample, w_ada, b_ada, norm1_g, norm2_g, w_in, q_norm_g, k_norm_g, rwkv_mu, rwkv_w0, rwkv_w2, rwkv_a0, rwkv_a2, rwkv_g2, rwkv_k_k, rwkv_k_a, rwkv_r_k, rwkv_ln_w, rwkv_ln_b, w_out, router_w, router_b, moe_w_gu, moe_b_gu, moe_w_down, moe_b_down):
    nbp, t, d = x_prompt.shape
    nbs, ts, _ = x_sample.shape
    depth = w_ada.shape[0]
    assert depth == 1 and ts == 1
    n_heads = cache_k.shape[3]
    aw = n_heads * HEAD_DIM
    rwid = rwkv_w0.shape[1]
    rwc = rwkv_mu.shape[1]
    past = cache_k.shape[2]
    ne = router_w.shape[2]
    keep = min(MAX_WINDOW, t)
    lyr = 0

    w_in_bf = w_in[lyr].astype(BF16)
    wt_bf = w_out[lyr][:aw].astype(BF16)
    wb_bf = w_out[lyr][aw:].astype(BF16)
    g1 = norm1_g[lyr].reshape(1, d)
    g2 = norm2_g[lyr].reshape(1, d)
    qg = jnp.tile(q_norm_g[lyr], n_heads).reshape(1, aw)
    kg = jnp.tile(k_norm_g[lyr], n_heads).reshape(1, aw)
    bd_mean_a = _block_diag(aw, 1.0 / HEAD_DIM)
    bd_mean_r = _block_diag(rwid, 1.0 / HEAD_DIM)
    bd_ones_r = _block_diag(rwid, 1.0)
    dl = rwkv_w2.shape[1]
    w2p = jnp.zeros((LANES, rwid), F32).at[:dl].set(rwkv_w2[lyr])
    a2p = jnp.zeros((LANES, rwid), F32).at[dl:dl + rwkv_a2.shape[1]].set(rwkv_a2[lyr])
    vec = lambda a: a[lyr].reshape(1, -1)
    rwt = router_w[lyr].T
    rb = router_b[lyr].reshape(ne, 1)

    rows_c = nbp + nbs
    rows_pad = -(-rows_c // 8) * 8
    c_all = jnp.zeros((rows_pad, d), F32).at[:nbp].set(c_prompt).at[nbp:rows_c].set(c_sample)
    mod = _ada(c_all, w_ada[lyr], b_ada[lyr])
    mod_p = mod[:nbp].reshape(nbp, 1, 6 * d)
    mod_s = mod[nbp:rows_c].reshape(1, nbs, 6 * d)
    xs3 = x_sample.reshape(1, nbs, d)

    cos_p, sin_p = _rope_tables(jnp.arange(t), n_heads)
    cos_s, sin_s = _rope_tables(jnp.full((nbs,), PAST_LEN), n_heads)
    rw_args = (vec(rwkv_mu), vec(rwkv_w0), w2p, vec(rwkv_a0), a2p, rwkv_g2[lyr], vec(rwkv_k_k), vec(rwkv_k_a), bd_ones_r)
    qp, kp, vp, *pre_p, tail_p, kt_p, vt_p = _inproj(x_prompt, mod_p, g1, w_in_bf, cos_p, sin_p, qg, kg, bd_mean_a,
                                                     ROW_TILE, keep=keep, rw=(jnp.zeros((nbp, 1, rwc), F32), *rw_args))
    qs, ks, vs, rws = _inproj(xs3, mod_s, g1, w_in[lyr], cos_s, sin_s, qg, kg, bd_mean_a, nbs)

    attn_p = _attn_prompt(qp, kp, vp)
    as_rows = lambda a: a.reshape(nbs, 1, -1)
    cache_t = lambda cch: jnp.transpose(cch[lyr], (0, 2, 3, 1)).reshape(nbs, aw, past)
    attn_s = _attn_sample(as_rows(qs), as_rows(ks), as_rows(vs), cache_t(cache_k), cache_t(cache_v))

    pre_s = _rwprep(rws, state_shift[lyr].reshape(1, nbs, rwc), *rw_args, tm=nbs, whole_prev=True)
    r_p, ld_p, k_p, v_p, al_p, be_p, g_p = pre_p
    y_p, wkv_p = _rwchunk(r_p, ld_p, k_p, v_p, al_p, be_p)
    r_s, ld_s, k_s, v_s, al_s, be_s, g_s = pre_s
    y_s, wkv_s = _rwstep(*(as_rows(a) for a in (r_s, ld_s, k_s, v_s, al_s, be_s)), state_wkv[lyr])
    y_s = y_s.reshape(1, nbs, rwid)

    n_p = nbp * t
    n_valid = n_p + nbs
    assert ROW_TILE == ROUTE_TILE and n_p % ROUTE_TILE == 0 and nbs <= ROUTE_TILE and d == SUBLANES * LANES
    n_tiles = n_p // ROUTE_TILE + 1
    n_slots = _slot_capacity(ne)
    n_rows = n_tiles * n_slots + N_OBUF * MOE_BLOCK
    op_args = (rwkv_r_k[lyr].reshape(1, rwid), vec(rwkv_ln_w), vec(rwkv_ln_b), bd_mean_r, bd_ones_r, rwt, rb)
    x1_p, pos_p, _, cnt_p, *bufs = _outproj(x_prompt, attn_p, y_p, r_p, k_p, v_p, g_p, mod_p, g2, wt_bf, wb_bf,
                                            *op_args, tm=ROW_TILE, n_slots=n_slots, n_rows=n_rows)
    x1_s, pos_s, _, cnt_s, xs, ws = _outproj(xs3, attn_s.reshape(1, nbs, aw), y_s, r_s, k_s, v_s, g_s, mod_s, g2,
                                             w_out[lyr][:aw], w_out[lyr][aw:], *op_args, tm=nbs, n_slots=n_slots,
                                             n_rows=n_rows, tile0=n_p // ROUTE_TILE, bufs=bufs)

    cnt = jnp.concatenate([cnt_p, cnt_s])[:, :, 0].astype(I32)
    per_blk = MOE_BLOCK // MOE_CHUNK
    n_blocks = -(-(n_valid * TOP_K // MOE_CHUNK + n_tiles * ne) // per_blk) + ne
    blk_e, nact, src, dst = _chunk_tables(cnt, n_slots, n_blocks)
    ys = _moe(blk_e, nact, src, dst, xs, ws, moe_w_gu[lyr], moe_b_gu[lyr], moe_w_down[lyr], moe_b_down[lyr], n_blocks)
    used = jnp.sum(-(-cnt // MOE_CHUNK) * MOE_CHUNK, axis=1)
    used = jnp.broadcast_to(used[:, None, None], (n_tiles, SUBLANES, LANES))
    y_prompt = _combine(pos_p.T, used, ys, x1_p, mod_p, 0, ROUTE_TILE, n_slots)
    y_sample = _combine(pos_s.T, used, ys, x1_s, mod_s, n_p // ROUTE_TILE, nbs, n_slots)

    kept = lambda a: jnp.transpose(a.reshape(nbp, n_heads, HEAD_DIM, keep), (0, 3, 1, 2))[None]
    return (y_prompt, y_sample.reshape(nbs, ts, d), kept(kt_p), kept(vt_p), wkv_p[None], tail_p[:, SUBLANES - 1][None],
            ks.reshape(nbs, ts, n_heads, HEAD_DIM)[None], vs.reshape(nbs, ts, n_heads, HEAD_DIM)[None],
            wkv_s[None], rws.reshape(nbs, rwc)[None])
```

```python
import functools

import numpy as np
import jax
import jax.numpy as jnp
from jax import lax
from jax.experimental import pallas as pl
from jax.experimental.pallas import tpu as pltpu

F32 = jnp.float32
BF16 = jnp.bfloat16
I32 = jnp.int32

HEAD_DIM = 64
LANES = 128
SUBLANES = 8
DILATED_PATTERNS = ((128, 1), (512, 4), (2048, 16))
WINDOW_STEPS = 128
MAX_WINDOW = 2048
PAST_LEN = 16384
ROPE_THETA = 10000.0
NORM_EPS = 1e-6
GN_EPS = 64e-5
TOP_K = 4
SWIGLU_ALPHA = 1.702
SWIGLU_LIMIT = 7.0
RW_CHUNK = 128
RW_PAIRS = 2
MOE_BLOCK = 512
MOE_CHUNK = 16
SLOT_ROWS = 256
ATTN_UNITS = 8
ROW_TILE = 512
ROUTE_TILE = 512
VMEM_LIMIT = 56 * 1024 * 1024
NEG_BIG = -1e30


def _cparams(*sem):
    return pltpu.CompilerParams(dimension_semantics=sem, vmem_limit_bytes=VMEM_LIMIT)


def _dot(a, b):
    return jnp.dot(a.astype(BF16), b.astype(BF16), preferred_element_type=F32)


def _split2(a):
    hi = a.astype(BF16)
    lo = (a - hi.astype(F32)).astype(BF16)
    return hi, lo


def _split3(a):
    hi = a.astype(BF16)
    r1 = a - hi.astype(F32)
    mid = r1.astype(BF16)
    lo = (r1 - mid.astype(F32)).astype(BF16)
    return hi, mid, lo


def _dot_x(a, e):
    e = e.astype(BF16)
    hi, lo = _split2(a)
    return jnp.dot(hi, e, preferred_element_type=F32) + jnp.dot(lo, e, preferred_element_type=F32)


def _xdot(e, a):
    e = e.astype(BF16)
    hi, mid, lo = _split3(a)
    return (jnp.dot(e, hi, preferred_element_type=F32) + jnp.dot(e, mid, preferred_element_type=F32)
            + jnp.dot(e, lo, preferred_element_type=F32))


def _dot3(a, b):
    ah, al = _split2(a)
    bh, bl = _split2(b)
    return (jnp.dot(ah, bh, preferred_element_type=F32) + jnp.dot(ah, bl, preferred_element_type=F32)
            + jnp.dot(al, bh, preferred_element_type=F32))


def _dot3_nt(a, b):
    ah, al = _split2(a)
    bh, bl = _split2(b)
    dn = (((1,), (1,)), ((), ()))
    return (lax.dot_general(ah, bh, dn, preferred_element_type=F32)
            + lax.dot_general(ah, bl, dn, preferred_element_type=F32)
            + lax.dot_general(al, bh, dn, preferred_element_type=F32))


def _mm(a, w):
    if w.dtype == BF16:
        return jnp.dot(a.astype(BF16), w, preferred_element_type=F32)
    return _dot3(a, w)


def _sigmoid(x):
    return 1.0 / (1.0 + jnp.exp(-x))


def _ada_kernel(c_ref, w_ref, b_ref, o_ref):
    c = c_ref[...]
    o_ref[...] = _dot3(c * _sigmoid(c), w_ref[...]) + b_ref[...]


def _ada(c_all, w_ada, b_ada):
    rows, d = c_all.shape
    n = w_ada.shape[1]
    tn = n // 4
    return pl.pallas_call(
        _ada_kernel,
        grid=(n // tn,),
        in_specs=[pl.BlockSpec((rows, d), lambda j: (0, 0)),
                  pl.BlockSpec((d, tn), lambda j: (0, j)),
                  pl.BlockSpec((1, tn), lambda j: (0, j))],
        out_specs=pl.BlockSpec((rows, tn), lambda j: (0, j)),
        out_shape=jax.ShapeDtypeStruct((rows, n), F32),
        compiler_params=_cparams("arbitrary"),
        name="ada",
    )(c_all, w_ada, b_ada.reshape(1, n))


N_RW_PARAMS = 9
N_RW_FEATS = 7


def _inproj_kernel(x_ref, sh_ref, sc_ref, g_ref, w_ref, cos_ref, sin_ref, qg_ref, kg_ref, bd_ref, *rest,
                   aw, first_kept, n_kept, rw_fused):
    n_in = 1 + N_RW_PARAMS if rw_fused else 0
    rw_in, outs = rest[:n_in], rest[n_in:]
    q_ref, k_ref, v_ref = outs[:3]
    n_rw_out = N_RW_FEATS + 1 if rw_fused else 1
    rw_out = outs[3:3 + n_rw_out]
    maybe_kv_t = outs[3 + n_rw_out:3 + n_rw_out + n_kept]
    x = x_ref[...]
    ms = jnp.mean(x * x, axis=-1, keepdims=True)
    h = x * lax.rsqrt(ms + NORM_EPS) * g_ref[...] * (1.0 + sc_ref[...]) + sh_ref[...]
    proj = _mm(h, w_ref[...])
    cos = cos_ref[...]
    sin = sin_ref[...]
    lane = lax.broadcasted_iota(I32, (1, aw), 1)
    first_half = (lane % HEAD_DIM) < (HEAD_DIM // 2)
    bd = bd_ref[...]

    def norm_rope(t, g):
        tn = t * lax.rsqrt((_dot if w_ref.dtype == BF16 else _dot_x)(t * t, bd) + NORM_EPS) * g
        rot = jnp.where(first_half, pltpu.roll(tn, aw - HEAD_DIM // 2, 1), pltpu.roll(tn, HEAD_DIM // 2, 1))
        return tn * cos + rot * sin

    scale = 1.0 / np.sqrt(HEAD_DIM).astype(np.float32)
    q_ref[...] = norm_rope(proj[:, :aw], qg_ref[...]) * scale
    k = norm_rope(proj[:, aw:2 * aw], kg_ref[...])
    v = proj[:, 2 * aw:3 * aw]
    k_ref[...] = k
    v_ref[...] = v
    rw = proj[:, 3 * aw:]
    if rw_fused:
        carry_ref = outs[-1]
        tail = rw[rw.shape[0] - SUBLANES:, :]
        before = jnp.where(pl.program_id(1) == 0, rw_in[0][...], carry_ref[SUBLANES - 1:SUBLANES, :])
        for ref, val in zip(rw_out, _rw_features(rw, _shifted(rw, before), *rw_in[1:])):
            ref[...] = val
        carry_ref[...] = tail
        rw_out[-1][...] = tail
    else:
        rw_out[0][...] = rw
    if maybe_kv_t:
        kt_ref, vt_ref = maybe_kv_t
        kept = pl.program_id(1) >= first_kept

        @pl.when(kept)
        def _():
            kt_ref[...] = k.T
            vt_ref[...] = v.T

        @pl.when(jnp.logical_not(kept))
        def _():
            kt_ref[...] = jnp.zeros_like(kt_ref)
            vt_ref[...] = jnp.zeros_like(vt_ref)


def _inproj(x3, mod3, g1, w_in_bf, cos_t, sin_t, qg, kg, bd_mean, tm, keep=0, rw=None):
    nb, t, d = x3.shape
    r = mod3.shape[1]
    ncol = w_in_bf.shape[1]
    aw = cos_t.shape[1]
    rwc = ncol - 3 * aw
    grid = (nb, t // tm)
    row = lambda b, i: (b, i, 0)
    const = lambda b, i: (0, 0)
    once = dict(pipeline_mode=pl.Buffered(1)) if rw else {}
    mod_spec = lambda s: pl.BlockSpec((None, r, d), (lambda b, i: (b, 0, s)) if r == 1 else (lambda b, i: (b, i, s)))
    out_specs = [pl.BlockSpec((None, tm, aw), row)] * 3
    out_shape = [jax.ShapeDtypeStruct((nb, t, aw), F32)] * 3
    rw_in, rw_specs, scratch = (), [], []
    if rw:
        rw_in = tuple(rw)
        rwid = rw_in[2].shape[1]
        rw_specs = [pl.BlockSpec((None, 1, rwc), lambda b, i: (b, 0, 0))] + [
            pl.BlockSpec(a.shape, const, **once) for a in rw_in[1:]]
        out_specs += [pl.BlockSpec((None, tm, rwid), row)] * N_RW_FEATS + [
            pl.BlockSpec((None, SUBLANES, rwc), lambda b, i: (b, 0, 0))]
        out_shape += [jax.ShapeDtypeStruct((nb, t, rwid), F32)] * N_RW_FEATS + [
            jax.ShapeDtypeStruct((nb, SUBLANES, rwc), F32)]
        scratch = [pltpu.VMEM((SUBLANES, rwc), F32)]
    else:
        out_specs += [pl.BlockSpec((None, tm, rwc), row)]
        out_shape += [jax.ShapeDtypeStruct((nb, t, rwc), F32)]
    first_kept = (t - keep) // tm
    if keep:
        assert keep % tm == 0 and (t - keep) % tm == 0
        kept_spec = pl.BlockSpec((None, aw, tm), lambda b, i: (b, 0, jnp.maximum(i - first_kept, 0)))
        out_specs += [kept_spec, kept_spec]
        out_shape += [jax.ShapeDtypeStruct((nb, aw, keep), F32)] * 2
    outs = pl.pallas_call(
        functools.partial(_inproj_kernel, aw=aw, first_kept=first_kept, n_kept=2 if keep else 0, rw_fused=bool(rw)),
        grid=grid,
        in_specs=[pl.BlockSpec((None, tm, d), row), mod_spec(0), mod_spec(1),
                  pl.BlockSpec((1, d), const), pl.BlockSpec((d, ncol), const, **once),
                  pl.BlockSpec((tm, aw), lambda b, i: (i, 0)), pl.BlockSpec((tm, aw), lambda b, i: (i, 0)),
                  pl.BlockSpec((1, aw), const), pl.BlockSpec((1, aw), const),
                  pl.BlockSpec((aw, aw), const, **once)] + rw_specs,
        out_specs=out_specs,
        out_shape=out_shape,
        scratch_shapes=scratch,
        compiler_params=_cparams("parallel", "arbitrary"),
        name="inproj",
    )(x3, mod3, mod3, g1, w_in_bf, cos_t, sin_t, qg, kg, bd_mean, *rw_in)
    return outs


def _attn_prompt_kernel(q_ref, k_ref, v_ref, o_ref, kp_ref, vp_ref, m_ref, l_ref, acc_ref, *, t, pad):
    nq = WINDOW_STEPS
    nk = 2 * WINDOW_STEPS
    kp_ref[pl.ds(0, pad), :] = jnp.zeros((pad, LANES), F32)
    vp_ref[pl.ds(0, pad), :] = jnp.zeros((pad, LANES), F32)
    kp_ref[pl.ds(pad, t), :] = k_ref[...]
    vp_ref[pl.ds(pad, t), :] = v_ref[...]
    head0 = lax.broadcasted_iota(I32, (1, LANES), 1) < HEAD_DIM
    qi = lax.broadcasted_iota(I32, (nq, nk), 0)
    kj = lax.broadcasted_iota(I32, (nq, nk), 1)
    steps_back = qi + nq - kj
    band = (steps_back >= 0) & (steps_back <= WINDOW_STEPS)
    has_past = kj >= nq

    hsels = (head0, jnp.logical_not(head0))
    nt = (((1,), (1,)), ((), ()))

    for p, (_, d) in enumerate(DILATED_PATTERNS):
        def units(g, carry, p=p, d=d):
            rows_q, kb, vb, valid, qh = [], [], [], [], []
            for j in range(ATTN_UNITS):
                u = g * ATTN_UNITS + j
                res = u % d
                blk = u // d
                q_start = res + d * nq * blk
                k_start = pad + q_start - d * nq
                if d == 1:
                    rows_q.append(pl.ds(q_start, nq))
                    rows_k = pl.ds(k_start, nk)
                else:
                    rows_q.append(pl.ds(q_start, nq, stride=d))
                    rows_k = pl.ds(k_start, nk, stride=d)
                q = q_ref[rows_q[j], :]
                kb.append(kp_ref[rows_k, :].astype(BF16))
                vb.append(vp_ref[rows_k, :].astype(BF16))
                valid.append(band & (has_past | (blk > 0)))
                qh.append([jnp.where(hsel, q, 0.0).astype(BF16) for hsel in hsels])
            chains = [(j, h) for j in range(ATTN_UNITS) for h in range(2)]
            s = [lax.dot_general(qh[j][h], kb[j], nt, preferred_element_type=F32) for j, h in chains]
            s = [jnp.where(valid[j], sc, NEG_BIG) for (j, h), sc in zip(chains, s)]
            mx = [jnp.max(sc, axis=-1, keepdims=True) for sc in s]
            e = [jnp.exp(sc - m) for sc, m in zip(s, mx)]
            den = [jnp.sum(ec, axis=-1, keepdims=True) for ec in e]
            o = [jnp.dot(ec.astype(BF16), vb[j], preferred_element_type=F32) for (j, h), ec in zip(chains, e)]
            for j in range(ATTN_UNITS):
                m_ref[p, rows_q[j], :] = jnp.where(head0, mx[2 * j], mx[2 * j + 1])
                l_ref[p, rows_q[j], :] = jnp.where(head0, den[2 * j], den[2 * j + 1])
                acc_ref[p, rows_q[j], :] = jnp.where(head0, o[2 * j], o[2 * j + 1])
            return carry

        lax.fori_loop(0, t // nq // ATTN_UNITS, units, 0)

    rows = 256

    def merge(i, carry):
        sl = pl.ds(pl.multiple_of(i * rows, rows), rows)
        m0, m1, m2 = m_ref[0, sl, :], m_ref[1, sl, :], m_ref[2, sl, :]
        mm = jnp.maximum(jnp.maximum(m0, m1), m2)
        w0, w1, w2 = jnp.exp(m0 - mm), jnp.exp(m1 - mm), jnp.exp(m2 - mm)
        num = w0 * acc_ref[0, sl, :] + w1 * acc_ref[1, sl, :] + w2 * acc_ref[2, sl, :]
        den = w0 * l_ref[0, sl, :] + w1 * l_ref[1, sl, :] + w2 * l_ref[2, sl, :]
        o_ref[sl, :] = num / den
        return carry

    lax.fori_loop(0, t // rows, merge, 0)


def _attn_prompt(q, k, v):
    nb, t, aw = q.shape
    pad = MAX_WINDOW
    assert t % MAX_WINDOW == 0
    spec = pl.BlockSpec((None, t, LANES), lambda b, hp: (b, 0, hp))
    return pl.pallas_call(
        functools.partial(_attn_prompt_kernel, t=t, pad=pad),
        grid=(nb, aw // LANES),
        in_specs=[spec, spec, spec],
        out_specs=spec,
        out_shape=jax.ShapeDtypeStruct((nb, t, aw), F32),
        scratch_shapes=[pltpu.VMEM((pad + t, LANES), F32), pltpu.VMEM((pad + t, LANES), F32),
                        pltpu.VMEM((3, t, LANES), F32), pltpu.VMEM((3, t, LANES), F32),
                        pltpu.VMEM((3, t, LANES), F32)],
        compiler_params=_cparams("parallel", "parallel"),
        name="attn_prompt",
    )(q, k, v)


def _attn_sample_kernel(q_ref, kn_ref, vn_ref, kt_ref, vt_ref, o_ref, *, nh, w):
    aw = nh * HEAD_DIM
    dist = w - lax.broadcasted_iota(I32, (1, w), 1)
    mult = jnp.zeros((1, w), F32)
    for win, d in DILATED_PATTERNS:
        mult = mult + jnp.where((dist % d == 0) & (dist <= win), 1.0, 0.0)
    n_pat = float(len(DILATED_PATTERNS))
    q_col = _col(q_ref[...], aw)
    kn_col = _col(kn_ref[...], aw)
    vn_col = _col(vn_ref[...], aw)
    heads = range(nh)
    hs = [pl.ds(h * HEAD_DIM, HEAD_DIM) for h in heads]
    cut = lambda col, h: col[h * HEAD_DIM:(h + 1) * HEAD_DIM]
    s = [jnp.sum(kt_ref[hs[h], :] * cut(q_col, h), axis=0, keepdims=True) for h in heads]
    s_self = [jnp.sum(cut(q_col, h) * cut(kn_col, h), axis=0, keepdims=True) for h in heads]
    s = [jnp.where(mult > 0.0, sh, NEG_BIG) for sh in s]
    mx = [jnp.maximum(jnp.max(sh, axis=1, keepdims=True), ss) for sh, ss in zip(s, s_self)]
    pr = [mult * jnp.exp(sh - m) for sh, m in zip(s, mx)]
    p_self = [n_pat * jnp.exp(ss - m) for ss, m in zip(s_self, mx)]
    den = [jnp.sum(p, axis=1, keepdims=True) + ps for p, ps in zip(pr, p_self)]
    num = [jnp.sum(vt_ref[hs[h], :] * pr[h], axis=1, keepdims=True) + p_self[h] * cut(vn_col, h) for h in heads]
    outs = [n / dn for n, dn in zip(num, den)]
    o_ref[...] = _row(jnp.concatenate(outs, axis=0), aw)


def _attn_sample(q, kn, vn, cache_kt, cache_vt):
    nb, _, aw = q.shape
    w = cache_kt.shape[2]
    assert w == MAX_WINDOW
    row = pl.BlockSpec((None, 1, aw), lambda i: (i, 0, 0))
    mat = pl.BlockSpec((None, aw, w), lambda i: (i, 0, 0))
    return pl.pallas_call(
        functools.partial(_attn_sample_kernel, nh=aw // HEAD_DIM, w=w),
        grid=(nb,),
        in_specs=[row, row, row, mat, mat],
        out_specs=row,
        out_shape=jax.ShapeDtypeStruct((nb, 1, aw), F32),
        compiler_params=_cparams("parallel"),
        name="attn_sample",
    )(q, kn, vn, cache_kt, cache_vt)


def _rw_features(p, prev, mu_ref, w0_ref, w2_ref, a0_ref, a2_ref, g2_ref, kk_ref, ka_ref, bd_ref):
    rwid = w0_ref.shape[1]
    xs = p + mu_ref[...] * (prev - p)
    r = xs[:, :rwid]
    k = xs[:, rwid:2 * rwid]
    v = xs[:, 2 * rwid:3 * rwid]
    xwa = xs[:, 3 * rwid:3 * rwid + LANES]
    xg = xs[:, 3 * rwid + LANES:]
    z = w0_ref[...] + _mm(jnp.tanh(xwa), w2_ref[...])
    softplus_neg = jnp.maximum(-z, 0.0) + jnp.log(1.0 + jnp.exp(-jnp.abs(z)))
    w = -softplus_neg - 0.5
    a = _sigmoid(a0_ref[...] + _mm(xwa, a2_ref[...]))
    g = _mm(_sigmoid(xg), g2_ref[...])
    kk = k * kk_ref[...]
    norm = jnp.sqrt(_dot_x(kk * kk, bd_ref[...]))
    kk = kk / jnp.maximum(norm, 1e-12)
    return r, -jnp.exp(w), k * (1.0 + (a - 1.0) * ka_ref[...]), v, -kk, kk * a, g


def _shifted(p, before):
    rowi = lax.broadcasted_iota(I32, (p.shape[0], 1), 0)
    return jnp.where(rowi == 0, before, pltpu.roll(p, 1, 0))


def _rwprep_kernel(p_ref, prev_ref, first_ref, *refs, whole_prev):
    params, outs = refs[:9], refs[9:]
    p = p_ref[...]
    if whole_prev:
        prev = first_ref[...]
    else:
        prev = _shifted(p, jnp.where(pl.program_id(1) == 0, first_ref[...], prev_ref[7:8, :]))
    for ref, val in zip(outs, _rw_features(p, prev, *params)):
        ref[...] = val


def _rwprep(rw, first, mu, w0, w2p, a0, a2p, g2, k_k, k_a, bd_ones, tm, whole_prev):
    nb, t, rwc = rw.shape
    rwid = w0.shape[1]
    grid = (nb, t // tm)
    row = lambda b, i: (b, i, 0)
    const = lambda b, i: (0, 0)
    if whole_prev:
        prev_spec = pl.BlockSpec((None, tm, rwc), row)
        first_spec = pl.BlockSpec((None, tm, rwc), row)
    else:
        prev_spec = pl.BlockSpec((None, 8, rwc), lambda b, i: (b, jnp.maximum(i * (tm // 8) - 1, 0), 0))
        first_spec = pl.BlockSpec((None, 1, rwc), lambda b, i: (b, 0, 0))
    vec = pl.BlockSpec((1, rwid), const)
    out = pl.BlockSpec((None, tm, rwid), row)
    return pl.pallas_call(
        functools.partial(_rwprep_kernel, whole_prev=whole_prev),
        grid=grid,
        in_specs=[pl.BlockSpec((None, tm, rwc), row), prev_spec, first_spec,
                  pl.BlockSpec((1, rwc), const), vec, pl.BlockSpec((LANES, rwid), const),
                  vec, pl.BlockSpec((LANES, rwid), const), pl.BlockSpec((LANES, rwid), const), vec, vec,
                  pl.BlockSpec((rwid, rwid), const)],
        out_specs=[out] * 7,
        out_shape=[jax.ShapeDtypeStruct((nb, t, rwid), F32)] * 7,
        compiler_params=_cparams("parallel", "parallel"),
        name="rwprep",
    )(rw, rw, first, mu, w0, w2p, a0, a2p, g2, k_k, k_a, bd_ones)


def _rwchunk_kernel(r_ref, ld_ref, k_ref, v_ref, al_ref, be_ref, y_ref, st_ref, z_ref, *, nb, npb):
    c = RW_CHUNK
    ci = pl.program_id(1)

    @pl.when(ci == 0)
    def _():
        z_ref[...] = jnp.zeros_like(z_ref)

    ti = lax.broadcasted_iota(I32, (c, c), 0)
    si = lax.broadcasted_iota(I32, (c, c), 1)
    low_incl = si <= ti
    low_strict = si < ti
    diag = si == ti
    tri = jnp.where(low_incl, 1.0, 0.0).astype(BF16)
    eye = jnp.where(diag, 1.0, 0.0)
    head0 = lax.broadcasted_iota(I32, (1, LANES), 1) < HEAD_DIM
    hsels = (head0, jnp.logical_not(head0))
    same_head = (ti < HEAD_DIM) == (si < HEAD_DIM)
    nt = (((1,), (1,)), ((), ()))
    batches = range(nb * npb)
    chains = [(b, h) for b in batches for h in range(2)]
    lanes_of = lambda b: pl.ds((b // nb) * LANES, LANES)
    r_v, ld_v, k_v, v_v, al_v, be_v = ([ref[b % nb, :, lanes_of(b)] for b in batches]
                                       for ref in (r_ref, ld_ref, k_ref, v_ref, al_ref, be_ref))

    cum = [_xdot(tri, ld_v[b]) for b in batches]
    tot = [cm[c - 1:c, :] for cm in cum]
    e_neg = [jnp.exp(-cm) for cm in cum]
    at = [al_v[b] * jnp.exp(cum[b] - ld_v[b]) for b in batches]
    rt = [r_v[b] * jnp.exp(cum[b]) for b in batches]
    rhs_t = [jnp.concatenate([be_v[b] * e_neg[b], k_v[b] * e_neg[b]], axis=0).astype(BF16) for b in batches]
    vb = [v_v[b].astype(BF16) for b in batches]
    at_h = [jnp.where(hsels[h], at[b], 0.0) for b, h in chains]
    rt_h = [jnp.where(hsels[h], rt[b], 0.0) for b, h in chains]
    a4 = [lax.dot_general(jnp.concatenate([a, r], axis=0).astype(BF16), rhs_t[b], nt, preferred_element_type=F32)
          for (b, h), a, r in zip(chains, at_h, rt_h)]
    a_ab = [jnp.where(low_strict, m[:c, :c], 0.0) for m in a4]
    a_ak = [jnp.where(low_strict, m[:c, c:], 0.0).astype(BF16) for m in a4]
    a_r = [jnp.concatenate([jnp.where(low_incl, m[c:, :c], 0.0), jnp.where(low_incl, m[c:, c:], 0.0)],
                           axis=1).astype(BF16) for m in a4]
    levels = int(np.log2(c))
    pw = [m.astype(BF16) for m in a_ab]
    pw = [jnp.dot(m, m, preferred_element_type=F32).astype(BF16) for m in pw]
    inv = [eye + m for m in a_ab]
    for _ in range(1, levels - 1):
        x2 = [jnp.dot(m, jnp.concatenate([m, i.astype(BF16)], axis=1), preferred_element_type=F32)
              for m, i in zip(pw, inv)]
        pw = [m[:, :c].astype(BF16) for m in x2]
        inv = [i + m[:, c:] for i, m in zip(inv, x2)]
    inv = [i + jnp.dot(m, i.astype(BF16), preferred_element_type=F32) for i, m in zip(inv, pw)]
    akv = [jnp.dot(m, vb[b], preferred_element_type=F32) for (b, h), m in zip(chains, a_ak)]
    x = [_dot(i, jnp.concatenate([kv, a], axis=1)) for i, kv, a in zip(inv, akv, at_h)]
    u0_h = [m[:, :LANES] for m in x]
    at2_h = [m[:, LANES:] for m in x]
    y0_h = [jnp.dot(ar, jnp.concatenate([u0, v_v[b]], axis=0).astype(BF16), preferred_element_type=F32)
            for (b, h), ar, u0 in zip(chains, a_r, u0_h)]
    rt2_h = [r + jnp.dot(ar[:, :c], a2.astype(BF16), preferred_element_type=F32)
             for r, ar, a2 in zip(rt_h, a_r, at2_h)]
    z = [z_ref[b] for b in batches]
    uy = [_dot(jnp.concatenate([at2_h[2 * b] + at2_h[2 * b + 1], rt2_h[2 * b] + rt2_h[2 * b + 1]], axis=0), z[b])
          for b in batches]
    u = [uy[b][:c] + jnp.where(head0, u0_h[2 * b], u0_h[2 * b + 1]) for b in batches]
    for b in batches:
        y_ref[b % nb, :, lanes_of(b)] = uy[b][c:] + jnp.where(head0, y0_h[2 * b], y0_h[2 * b + 1])
    e_end = [jnp.exp(tot[b] - cum[b]) for b in batches]
    lhs_t = [jnp.concatenate([be_v[b] * e_end[b], k_v[b] * e_end[b]], axis=0) for b in batches]
    zadd = [_dot(lhs_t[b].T, jnp.concatenate([u[b], v_v[b]], axis=0)) for b in batches]
    for b in batches:
        dcol = jnp.sum(jnp.where(diag, jnp.broadcast_to(jnp.exp(tot[b]), (c, c)), 0.0), axis=1, keepdims=True)
        z_ref[b] = dcol * z[b] + jnp.where(same_head, zadd[b], 0.0)

    @pl.when(ci == pl.num_programs(1) - 1)
    def _():
        for b in batches:
            s = z_ref[b].T
            st_ref[b % nb, 2 * (b // nb)] = s[:HEAD_DIM, :HEAD_DIM]
            st_ref[b % nb, 2 * (b // nb) + 1] = s[HEAD_DIM:, HEAD_DIM:]


def _rwchunk(r, ld, k, v, al, be):
    nb, t, rwid = r.shape
    c = RW_CHUNK
    npb = RW_PAIRS
    assert t % c == 0 and c == LANES and rwid % (npb * LANES) == 0
    ngrp = rwid // (npb * LANES)
    seq = pl.BlockSpec((nb, c, npb * LANES), lambda hp, ci: (0, ci, hp))
    return pl.pallas_call(
        functools.partial(_rwchunk_kernel, nb=nb, npb=npb),
        grid=(ngrp, t // c),
        in_specs=[seq] * 6,
        out_specs=[seq, pl.BlockSpec((nb, 2 * npb, HEAD_DIM, HEAD_DIM), lambda hp, ci: (0, hp, 0, 0))],
        out_shape=[jax.ShapeDtypeStruct((nb, t, rwid), F32),
                   jax.ShapeDtypeStruct((nb, 2 * npb * ngrp, HEAD_DIM, HEAD_DIM), F32)],
        scratch_shapes=[pltpu.VMEM((nb * npb, LANES, LANES), F32)],
        compiler_params=_cparams("parallel", "arbitrary"),
        name="rwchunk",
    )(r, ld, k, v, al, be)


def _col(row, width):
    return jnp.broadcast_to(row, (LANES, width)).T[:, 0:1]


def _row(col, width):
    return jnp.broadcast_to(col, (width, LANES)).T[0:1, :]


def _rwstep_kernel(r_ref, ld_ref, k_ref, v_ref, al_ref, be_ref, s_ref, y_ref, so_ref, *, bb, nh):
    rwid = nh * HEAD_DIM

    def one(b, carry):
        r, dcy, k, al, be = (ref[b] for ref in (r_ref, ld_ref, k_ref, al_ref, be_ref))
        dcy = jnp.exp(dcy)
        v_col = _col(v_ref[b], rwid)
        heads = range(nh)
        hs = [slice(h * HEAD_DIM, (h + 1) * HEAD_DIM) for h in heads]
        st = [s_ref[b, h] for h in heads]
        sa = [jnp.sum(st[h] * al[:, hs[h]], axis=1, keepdims=True) for h in heads]
        st = [st[h] * dcy[:, hs[h]] + sa[h] * be[:, hs[h]] + v_col[hs[h], :] * k[:, hs[h]] for h in heads]
        for h in heads:
            so_ref[b, h] = st[h]
        ys = [jnp.sum(st[h] * r[:, hs[h]], axis=1, keepdims=True) for h in heads]
        y_ref[b] = _row(jnp.concatenate(ys, axis=0), rwid)
        return carry

    lax.fori_loop(0, bb, one, 0, unroll=2)


def _rwstep(r, ld, k, v, al, be, state):
    nb, _, rwid = r.shape
    nh = state.shape[1]
    bb = 8
    row = pl.BlockSpec((bb, 1, rwid), lambda i: (i, 0, 0))
    st = pl.BlockSpec((bb, nh, HEAD_DIM, HEAD_DIM), lambda i: (i, 0, 0, 0))
    return pl.pallas_call(
        functools.partial(_rwstep_kernel, bb=bb, nh=nh),
        grid=(nb // bb,),
        in_specs=[row] * 6 + [st],
        out_specs=[row, st],
        out_shape=[jax.ShapeDtypeStruct((nb, 1, rwid), F32), jax.ShapeDtypeStruct(state.shape, F32)],
        compiler_params=_cparams("parallel"),
        name="rwstep",
    )(r, ld, k, v, al, be, state)


def _outproj_kernel(x_ref, at_ref, y_ref, r_ref, k_ref, v_ref, g_ref, gt_ref, sh_ref, sc_ref, g2_ref,
                    wt_ref, wb_ref, rk_ref, lw_ref, lb_ref, bdm_ref, bd1_ref, rwt_ref, rb_ref, *rest, n_slots):
    x1_ref, pos_ref, wt4_ref, cnt_ref, xs_ref, ws_ref, hb_ref = rest[-7:]
    y = y_ref[...]
    bdm = bdm_ref[...]
    mean = _dot_x(y, bdm)
    yc = y - mean
    var = _dot_x(yc * yc, bdm)
    yn = yc * lax.rsqrt(var + GN_EPS) * lw_ref[...] + lb_ref[...]
    v = v_ref[...]
    bonus = _dot_x(r_ref[...] * k_ref[...] * rk_ref[...], bd1_ref[...]) * v
    rw = (yn + bonus) * g_ref[...]
    mix = _mm(at_ref[...], wt_ref[...]) + _mm(rw, wb_ref[...])
    x1 = x_ref[...] + gt_ref[...] * mix
    x1_ref[...] = x1
    ms = jnp.mean(x1 * x1, axis=-1, keepdims=True)
    h2 = x1 * lax.rsqrt(ms + NORM_EPS) * g2_ref[...] * (1.0 + sc_ref[...]) + sh_ref[...]
    pos, wts, cnt = _route_tile(_dot3_nt(rwt_ref[...], h2) + rb_ref[...])
    for kk in range(TOP_K):
        pos_ref[pl.ds(kk, 1), :] = pos[kk]
        wt4_ref[pl.ds(kk, 1), :] = wts[kk]
    cnt_ref[...] = jnp.broadcast_to(cnt, cnt_ref.shape)
    hb_ref[...] = h2.astype(BF16)
    used = jnp.sum(jnp.ceil(cnt / MOE_CHUNK) * MOE_CHUNK).astype(I32)
    _sort_tile(pos, wts, hb_ref, xs_ref, ws_ref, (used + SLOT_ROWS - 1) // SLOT_ROWS)


def _outproj(x3, attn, y, r, k, v, g, mod3, g2, wt_bf, wb_bf, r_k, ln_w, ln_b, bd_mean, bd_ones, rwt, rb, tm,
             n_slots, n_rows, tile0=0, bufs=None):
    nb, t, d = x3.shape
    rmod = mod3.shape[1]
    aw = attn.shape[2]
    ne = rwt.shape[0]
    nt = t // tm
    grid = (nb, nt)
    row = lambda b, i: (b, i, 0)
    const = lambda b, i: (0, 0)
    tile = lambda b, i: (tile0 + b * nt + i, 0)
    mod_spec = lambda s: pl.BlockSpec((None, rmod, d), (lambda b, i: (b, 0, s)) if rmod == 1 else (lambda b, i: (b, i, s)))
    half = pl.BlockSpec((None, tm, aw), row)
    vec = pl.BlockSpec((1, aw), const)
    tok4 = pl.BlockSpec((TOP_K, tm), lambda b, i: (0, b * nt + i))
    bufs = () if bufs is None else tuple(bufs)
    n_in = 20
    return pl.pallas_call(
        functools.partial(_outproj_kernel, n_slots=n_slots),
        grid=grid,
        in_specs=[pl.BlockSpec((None, tm, d), row)] + [half] * 6 + [mod_spec(2), mod_spec(3), mod_spec(4),
                  pl.BlockSpec((1, d), const), pl.BlockSpec((aw, d), const), pl.BlockSpec((aw, d), const),
                  vec, vec, vec, pl.BlockSpec((aw, aw), const), pl.BlockSpec((aw, aw), const),
                  pl.BlockSpec((ne, d), const), pl.BlockSpec((ne, 1), const)]
                 + [pl.BlockSpec(memory_space=pl.ANY)] * len(bufs),
        out_specs=[pl.BlockSpec((None, tm, d), row), tok4, tok4,
                   pl.BlockSpec((None, ne, LANES), lambda b, i: (b * nt + i, 0, 0)),
                   pl.BlockSpec((n_slots, d), tile), pl.BlockSpec((n_slots, LANES), tile)],
        out_shape=[jax.ShapeDtypeStruct((nb, t, d), F32), jax.ShapeDtypeStruct((TOP_K, nb * t), I32),
                   jax.ShapeDtypeStruct((TOP_K, nb * t), F32), jax.ShapeDtypeStruct((nb * nt, ne, LANES), F32),
                   jax.ShapeDtypeStruct((n_rows, d), BF16), jax.ShapeDtypeStruct((n_rows, LANES), F32)],
        scratch_shapes=[pltpu.VMEM((tm, d), BF16)],
        input_output_aliases={n_in: 4, n_in + 1: 5} if bufs else {},
        compiler_params=_cparams("parallel", "parallel"),
        name="outproj",
    )(x3, attn, y, r, k, v, g, mod3, mod3, mod3, g2, wt_bf, wb_bf, r_k, ln_w, ln_b, bd_mean, bd_ones, rwt, rb, *bufs)


def _route_tile(lg):
    ne, tn = lg.shape
    eidx = lax.broadcasted_iota(I32, (ne, tn), 0).astype(F32)
    vals, hots = [], []
    for _ in range(TOP_K):
        mx = jnp.max(lg, axis=0, keepdims=True)
        pick = jnp.min(jnp.where(lg == mx, eidx, float(ne)), axis=0, keepdims=True)
        hot = eidx == pick
        vals.append(mx)
        hots.append(hot)
        lg = jnp.where(hot, -jnp.inf, lg)
    ex = [jnp.exp(vv - vals[0]) for vv in vals]
    den = ex[0] + ex[1] + ex[2] + ex[3]
    wts = [e / den for e in ex]
    hot_all = jnp.zeros((ne, tn), F32)
    for hot in hots:
        hot_all = hot_all + jnp.where(hot, 1.0, 0.0)
    ri = lax.broadcasted_iota(I32, (tn, tn), 0)
    cj = lax.broadcasted_iota(I32, (tn, tn), 1)
    upper = jnp.where(ri <= cj, 1.0, 0.0).astype(BF16)
    before = jnp.dot(hot_all.astype(BF16), upper, preferred_element_type=F32) - hot_all
    cnt = jnp.sum(hot_all, axis=1, keepdims=True)
    padded = jnp.ceil(cnt / MOE_CHUNK) * MOE_CHUNK
    er = lax.broadcasted_iota(I32, (ne, ne), 0)
    ec = lax.broadcasted_iota(I32, (ne, ne), 1)
    lower_strict = jnp.where(ec < er, 1.0, 0.0)
    off = _xdot(lower_strict, jnp.broadcast_to(padded, (ne, LANES)))[:, 0:1]
    pos = [jnp.sum(jnp.where(hot, off + before, 0.0), axis=0, keepdims=True).astype(I32) for hot in hots]
    return pos, wts, cnt


def _slot_capacity(ne):
    return -(-(TOP_K * ROUTE_TILE + ne * (MOE_CHUNK - 1)) // SLOT_ROWS) * SLOT_ROWS


def _sort_tile(pos, wts, h_ref, xs_ref, ws_ref, n_steps):
    tn = h_ref.shape[0]

    def rows(c, carry):
        h = h_ref[...]
        r0 = pl.multiple_of(c * SLOT_ROWS, SLOT_ROWS)
        slot = r0 + lax.broadcasted_iota(I32, (SLOT_ROWS, tn), 0)
        pw = jnp.zeros((SLOT_ROWS, tn), F32)
        for kk in range(TOP_K):
            pw = pw + jnp.where(slot == pos[kk], wts[kk], 0.0)
        p = jnp.where(pw > 0.0, 1.0, 0.0)
        xs_ref[pl.ds(r0, SLOT_ROWS), :] = jnp.dot(p.astype(BF16), h, preferred_element_type=F32).astype(BF16)
        ws_ref[pl.ds(r0, SLOT_ROWS), :] = jnp.broadcast_to(jnp.sum(pw, axis=1, keepdims=True), (SLOT_ROWS, LANES))
        return carry

    lax.fori_loop(0, n_steps, rows, 0)


def _chunk_tables(cnt, n_slots, n_blocks):
    nt, ne = cnt.shape
    per_blk = MOE_BLOCK // MOE_CHUNK
    nch = -(-cnt // MOE_CHUNK)
    seg0 = (jnp.cumsum(nch, axis=1) - nch) + (jnp.arange(nt) * (n_slots // MOE_CHUNK))[:, None]
    cum_t = jnp.cumsum(nch, axis=0)
    total = cum_t[-1]
    blocks = -(-total // per_blk)
    blk_end = jnp.cumsum(blocks)
    blk = jnp.arange(n_blocks)
    blk_e = jnp.minimum(jnp.sum(blk_end[None, :] <= blk[:, None], axis=1), ne - 1).astype(I32)
    nact = blk_end[-1].reshape(1).astype(I32)
    pick_e = blk_e[:, None] == jnp.arange(ne)[None, :]
    of_e = lambda tab: jnp.sum(jnp.where(pick_e, tab[None, :], 0), axis=1)
    col_e = lambda tab: jnp.sum(jnp.where(pick_e[:, None, :], tab[None, :, :], 0), axis=2)
    q = ((blk - of_e(blk_end - blocks)) * per_blk)[:, None] + jnp.arange(per_blk)[None, :]
    live = (q < of_e(total)[:, None]) & (blk < nact[0])[:, None]
    t_n = jnp.minimum(jnp.sum(col_e(cum_t)[:, None, :] <= q[:, :, None], axis=2), nt - 1)
    pick_t = t_n[:, :, None] == jnp.arange(nt)[None, None, :]
    of_t = lambda tab: jnp.sum(jnp.where(pick_t, tab[:, None, :], 0), axis=2)
    src = jnp.where(live, of_t(col_e(seg0)) + q - of_t(col_e(cum_t - nch)), 0)
    n = blk[:, None] * per_blk + jnp.arange(per_blk)[None, :]
    spare = nt * (n_slots // MOE_CHUNK) + n % (N_OBUF * per_blk)
    dst = jnp.where(live, src, spare)
    return blk_e, nact, src.reshape(-1).astype(I32), dst.reshape(-1).astype(I32)


N_OBUF = 2


def _moe_kernel(be_ref, nact_ref, src_ref, dst_ref, xs_ref, ws_ref, wgu_ref, bgu_ref, wd_ref, bd_ref,
                out_ref, xbuf, wbuf, obuf, gsem, ssem, wgu_bf, wd_bf, *, dff):
    i = pl.program_id(0)
    nact = nact_ref[0]
    per_blk = MOE_BLOCK // MOE_CHUNK
    rows = lambda c: pl.ds(pl.multiple_of(c * MOE_CHUNK, MOE_CHUNK), MOE_CHUNK)

    def gathers(blk, slot):
        copies = []
        for m in range(per_blk):
            c = src_ref[blk * per_blk + m]
            copies.append(pltpu.make_async_copy(xs_ref.at[rows(c), :], xbuf.at[slot, rows(m), :], gsem.at[slot]))
            copies.append(pltpu.make_async_copy(ws_ref.at[rows(c), :], wbuf.at[slot, rows(m), :], gsem.at[slot]))
        return copies

    def scatters(blk, slot):
        return [pltpu.make_async_copy(obuf.at[slot, rows(m), :], out_ref.at[rows(dst_ref[blk * per_blk + m]), :],
                                      ssem.at[slot]) for m in range(per_blk)]

    @pl.when(i == 0)
    def _():
        for cp in gathers(0, 0):
            cp.start()

    @pl.when(i < nact)
    def _():
        slot = i % 2
        for cp in gathers(i, slot):
            cp.wait()

        @pl.when(i + 1 < nact)
        def _():
            for cp in gathers(i + 1, 1 - slot):
                cp.start()

        @pl.when((i == 0) | (be_ref[i] != be_ref[jnp.maximum(i - 1, 0)]))
        def _():
            wgu_bf[...] = wgu_ref[...].astype(BF16)
            wd_bf[...] = wd_ref[...].astype(BF16)

        gu = jnp.dot(xbuf[slot], wgu_bf[...], preferred_element_type=F32) + bgu_ref[...]
        gate = jnp.minimum(gu[:, :dff], SWIGLU_LIMIT)
        up = jnp.clip(gu[:, dff:], -SWIGLU_LIMIT, SWIGLU_LIMIT)
        act = (up + 1.0) * gate * _sigmoid(gate * SWIGLU_ALPHA)
        res = jnp.dot(act.astype(BF16), wd_bf[...], preferred_element_type=F32) + bd_ref[...]
        obuf[slot] = (res * wbuf[slot][:, 0:1]).astype(BF16)

        @pl.when(i >= 1)
        def _():
            for cp in scatters(i - 1, 1 - slot):
                cp.wait()

        for cp in scatters(i, slot):
            cp.start()

        @pl.when(i == nact - 1)
        def _():
            for cp in scatters(i, slot):
                cp.wait()


def _moe(blk_e, nact, src, dst, xs, ws, w_gu, b_gu, w_down, b_down, n_blocks):
    ne, d, dff2 = w_gu.shape
    dff = dff2 // 2
    blk = MOE_BLOCK
    grid_spec = pltpu.PrefetchScalarGridSpec(
        num_scalar_prefetch=4,
        grid=(n_blocks,),
        in_specs=[pl.BlockSpec(memory_space=pl.ANY), pl.BlockSpec(memory_space=pl.ANY),
                  pl.BlockSpec((None, d, dff2), lambda i, be, *_: (be[i], 0, 0)),
                  pl.BlockSpec((None, 1, dff2), lambda i, be, *_: (be[i], 0, 0)),
                  pl.BlockSpec((None, dff, d), lambda i, be, *_: (be[i], 0, 0)),
                  pl.BlockSpec((None, 1, d), lambda i, be, *_: (be[i], 0, 0))],
        out_specs=pl.BlockSpec(memory_space=pl.ANY),
        scratch_shapes=[pltpu.VMEM((2, blk, d), BF16), pltpu.VMEM((2, blk, LANES), F32),
                        pltpu.VMEM((N_OBUF, blk, d), BF16),
                        pltpu.SemaphoreType.DMA((2,)), pltpu.SemaphoreType.DMA((N_OBUF,)),
                        pltpu.VMEM((d, dff2), BF16), pltpu.VMEM((dff, d), BF16)],
    )
    return pl.pallas_call(
        functools.partial(_moe_kernel, dff=dff),
        grid_spec=grid_spec,
        out_shape=jax.ShapeDtypeStruct(xs.shape, BF16),
        compiler_params=_cparams("arbitrary"),
        name="moe",
    )(blk_e, nact, src, dst, xs, ws, w_gu, b_gu.reshape(ne, 1, dff2), w_down, b_down.reshape(ne, 1, d))


def _combine_kernel(pos_ref, used_ref, ys_ref, x1_ref, gt_ref, o_ref, p_ref, *, n_slots):
    tn = x1_ref.shape[0]
    filled = lax.broadcasted_iota(I32, (n_slots, 1), 0) < used_ref[0:1, 0:1]
    ys = jnp.where(filled, ys_ref[...], jnp.zeros((), BF16))
    lane = lax.broadcasted_iota(I32, (tn, SLOT_ROWS), 1)
    pos = [jnp.broadcast_to(pos_ref[:, kk:kk + 1], (tn, SLOT_ROWS)) - lane for kk in range(TOP_K)]

    def cols(c, carry):
        c0 = pl.multiple_of(c * SLOT_ROWS, SLOT_ROWS)
        p = jnp.zeros((tn, SLOT_ROWS), F32)
        for kk in range(TOP_K):
            p = p + jnp.where(pos[kk] == c0, 1.0, 0.0)
        p_ref[:, pl.ds(c0, SLOT_ROWS)] = p.astype(BF16)
        return carry

    lax.fori_loop(0, n_slots // SLOT_ROWS, cols, 0)
    o_ref[...] = x1_ref[...] + gt_ref[...] * jnp.dot(p_ref[...], ys, preferred_element_type=F32)


def _combine(pos_t, used, ys, x1, mod3, tile0, rows, n_slots):
    nb, t, d = x1.shape
    rmod = mod3.shape[1]
    nt = t // rows
    per_tile = ROUTE_TILE // rows
    row = lambda b, i: (b, i, 0)
    gate = pl.BlockSpec((None, rmod, d), (lambda b, i: (b, 0, 5)) if rmod == 1 else (lambda b, i: (b, i, 5)))
    return pl.pallas_call(
        functools.partial(_combine_kernel, n_slots=n_slots),
        grid=(nb, nt),
        in_specs=[pl.BlockSpec((rows, TOP_K), lambda b, i: (b * nt + i, 0)),
                  pl.BlockSpec((None, SUBLANES, LANES), lambda b, i: (tile0 + (b * nt + i) // per_tile, 0, 0)),
                  pl.BlockSpec((n_slots, d), lambda b, i: (tile0 + (b * nt + i) // per_tile, 0)),
                  pl.BlockSpec((None, rows, d), row), gate],
        out_specs=pl.BlockSpec((None, rows, d), row),
        out_shape=jax.ShapeDtypeStruct((nb, t, d), F32),
        scratch_shapes=[pltpu.VMEM((rows, n_slots), BF16)],
        compiler_params=_cparams("parallel", "parallel"),
        name="combine",
    )(pos_t, used, ys, x1, mod3)


def _rope_tables(pos, n_heads):
    half = HEAD_DIM // 2
    inv_freq = 1.0 / (ROPE_THETA ** (jnp.arange(0, HEAD_DIM, 2, dtype=F32) / HEAD_DIM))
    ang = pos.astype(F32)[:, None] * inv_freq[None, :]
    cos, sin = jnp.cos(ang), jnp.sin(ang)
    del half
    return (jnp.tile(jnp.concatenate([cos, cos], axis=-1), (1, n_heads)),
            jnp.tile(jnp.concatenate([-sin, sin], axis=-1), (1, n_heads)))


def _block_diag(width, value):
    h = np.arange(width) // HEAD_DIM
    return jnp.asarray(np.where(h[:, None] == h[None, :], value, 0.0), F32)


def kernel(x_prompt, x_sample, cache_k, cache_v, state_wkv, state_shift, c_prompt, c_sample, w_ada, b_ada, norm1_g, norm2_g, w_in, q_norm_g, k_norm_g, rwkv_mu, rwkv_w0, rwkv_w2, rwkv_a0, rwkv_a2, rwkv_g2, rwkv_k_k, rwkv_k_a, rwkv_r_k, rwkv_ln_w, rwkv_ln_b, w_out, router_w, router_b, moe_w_gu, moe_b_gu, moe_w_down, moe_b_down):
    nbp, t, d = x_prompt.shape
    nbs, ts, _ = x_sample.shape
    depth = w_ada.shape[0]
    assert depth == 1 and ts == 1
    n_heads = cache_k.shape[3]
    aw = n_heads * HEAD_DIM
    rwid = rwkv_w0.shape[1]
    rwc = rwkv_mu.shape[1]
    past = cache_k.shape[2]
    ne = router_w.shape[2]
    keep = min(MAX_WINDOW, t)
    lyr = 0

    w_in_bf = w_in[lyr].astype(BF16)
    wt_bf = w_out[lyr][:aw].astype(BF16)
    wb_bf = w_out[lyr][aw:].astype(BF16)
    g1 = norm1_g[lyr].reshape(1, d)
    g2 = norm2_g[lyr].reshape(1, d)
    qg = jnp.tile(q_norm_g[lyr], n_heads).reshape(1, aw)
    kg = jnp.tile(k_norm_g[lyr], n_heads).reshape(1, aw)
    bd_mean_a = _block_diag(aw, 1.0 / HEAD_DIM)
    bd_mean_r = _block_diag(rwid, 1.0 / HEAD_DIM)
    bd_ones_r = _block_diag(rwid, 1.0)
    dl = rwkv_w2.shape[1]
    w2p = jnp.zeros((LANES, rwid), F32).at[:dl].set(rwkv_w2[lyr])
    a2p = jnp.zeros((LANES, rwid), F32).at[dl:dl + rwkv_a2.shape[1]].set(rwkv_a2[lyr])
    vec = lambda a: a[lyr].reshape(1, -1)
    rwt = router_w[lyr].T
    rb = router_b[lyr].reshape(ne, 1)

    rows_c = nbp + nbs
    rows_pad = -(-rows_c // 8) * 8
    c_all = jnp.zeros((rows_pad, d), F32).at[:nbp].set(c_prompt).at[nbp:rows_c].set(c_sample)
    mod = _ada(c_all, w_ada[lyr], b_ada[lyr])
    mod_p = mod[:nbp].reshape(nbp, 1, 6 * d)
    mod_s = mod[nbp:rows_c].reshape(1, nbs, 6 * d)
    xs3 = x_sample.reshape(1, nbs, d)

    cos_p, sin_p = _rope_tables(jnp.arange(t), n_heads)
    cos_s, sin_s = _rope_tables(jnp.full((nbs,), PAST_LEN), n_heads)
    rw_args = (vec(rwkv_mu), vec(rwkv_w0), w2p, vec(rwkv_a0), a2p, rwkv_g2[lyr], vec(rwkv_k_k), vec(rwkv_k_a), bd_ones_r)
    rw_args_p = tuple(a.astype(BF16) if i in (2, 4, 5) else a for i, a in enumerate(rw_args))
    qp, kp, vp, *pre_p, tail_p, kt_p, vt_p = _inproj(x_prompt, mod_p, g1, w_in_bf, cos_p, sin_p, qg, kg, bd_mean_a,
                                                     ROW_TILE, keep=keep, rw=(jnp.zeros((nbp, 1, rwc), F32), *rw_args_p))
    qs, ks, vs, rws = _inproj(xs3, mod_s, g1, w_in[lyr], cos_s, sin_s, qg, kg, bd_mean_a, nbs)

    attn_p = _attn_prompt(qp, kp, vp)
    as_rows = lambda a: a.reshape(nbs, 1, -1)
    cache_t = lambda cch: jnp.transpose(cch[lyr], (0, 2, 3, 1)).reshape(nbs, aw, past)
    attn_s = _attn_sample(as_rows(qs), as_rows(ks), as_rows(vs), cache_t(cache_k), cache_t(cache_v))

    pre_s = _rwprep(rws, state_shift[lyr].reshape(1, nbs, rwc), *rw_args, tm=nbs, whole_prev=True)
    r_p, ld_p, k_p, v_p, al_p, be_p, g_p = pre_p
    y_p, wkv_p = _rwchunk(r_p, ld_p, k_p, v_p, al_p, be_p)
    r_s, ld_s, k_s, v_s, al_s, be_s, g_s = pre_s
    y_s, wkv_s = _rwstep(*(as_rows(a) for a in (r_s, ld_s, k_s, v_s, al_s, be_s)), state_wkv[lyr])
    y_s = y_s.reshape(1, nbs, rwid)

    n_p = nbp * t
    n_valid = n_p + nbs
    assert ROW_TILE == ROUTE_TILE and n_p % ROUTE_TILE == 0 and nbs <= ROUTE_TILE and d == SUBLANES * LANES
    n_tiles = n_p // ROUTE_TILE + 1
    n_slots = _slot_capacity(ne)
    n_rows = n_tiles * n_slots + N_OBUF * MOE_BLOCK
    op_args = (rwkv_r_k[lyr].reshape(1, rwid), vec(rwkv_ln_w), vec(rwkv_ln_b), bd_mean_r, bd_ones_r, rwt, rb)
    x1_p, pos_p, _, cnt_p, *bufs = _outproj(x_prompt, attn_p, y_p, r_p, k_p, v_p, g_p, mod_p, g2, wt_bf, wb_bf,
                                            *op_args, tm=ROW_TILE, n_slots=n_slots, n_rows=n_rows)
    x1_s, pos_s, _, cnt_s, xs, ws = _outproj(xs3, attn_s.reshape(1, nbs, aw), y_s, r_s, k_s, v_s, g_s, mod_s, g2,
                                             w_out[lyr][:aw], w_out[lyr][aw:], *op_args, tm=nbs, n_slots=n_slots,
                                             n_rows=n_rows, tile0=n_p // ROUTE_TILE, bufs=bufs)

    cnt = jnp.concatenate([cnt_p, cnt_s])[:, :, 0].astype(I32)
    per_blk = MOE_BLOCK // MOE_CHUNK
    n_blocks = -(-(n_valid * TOP_K // MOE_CHUNK + n_tiles * ne) // per_blk) + ne
    blk_e, nact, src, dst = _chunk_tables(cnt, n_slots, n_blocks)
    ys = _moe(blk_e, nact, src, dst, xs, ws, moe_w_gu[lyr], moe_b_gu[lyr], moe_w_down[lyr], moe_b_down[lyr], n_blocks)
    used = jnp.sum(-(-cnt // MOE_CHUNK) * MOE_CHUNK, axis=1)
    used = jnp.broadcast_to(used[:, None, None], (n_tiles, SUBLANES, LANES))
    y_prompt = _combine(pos_p.T, used, ys, x1_p, mod_p, 0, ROUTE_TILE, n_slots)
    y_sample = _combine(pos_s.T, used, ys, x1_s, mod_s, n_p // ROUTE_TILE, nbs, n_slots)

    kept = lambda a: jnp.transpose(a.reshape(nbp, n_heads, HEAD_DIM, keep), (0, 3, 1, 2))[None]
    return (y_prompt, y_sample.reshape(nbs, ts, d), kept(kt_p), kept(vt_p), wkv_p[None], tail_p[:, SUBLANES - 1][None],
            ks.reshape(nbs, ts, n_heads, HEAD_DIM)[None], vs.reshape(nbs, ts, n_heads, HEAD_DIM)[None],
            wkv_s[None], rws.reshape(nbs, rwc)[None])
```

```python
import functools

import numpy as np
import jax
import jax.numpy as jnp
from jax import lax
from jax.experimental import pallas as pl
from jax.experimental.pallas import tpu as pltpu

F32 = jnp.float32
BF16 = jnp.bfloat16
I32 = jnp.int32

HEAD_DIM = 64
LANES = 128
SUBLANES = 8
DILATED_PATTERNS = ((128, 1), (512, 4), (2048, 16))
WINDOW_STEPS = 128
MAX_WINDOW = 2048
PAST_LEN = 16384
ROPE_THETA = 10000.0
NORM_EPS = 1e-6
GN_EPS = 64e-5
TOP_K = 4
SWIGLU_ALPHA = 1.702
SWIGLU_LIMIT = 7.0
RW_CHUNK = 128
RW_PAIRS = 2
MOE_BLOCK = 512
MOE_CHUNK = 16
SLOT_ROWS = 256
ATTN_UNITS = 8
ROW_TILE = 512
ROUTE_TILE = 256
VMEM_LIMIT = 56 * 1024 * 1024
NEG_BIG = -1e30


def _cparams(*sem):
    return pltpu.CompilerParams(dimension_semantics=sem, vmem_limit_bytes=VMEM_LIMIT)


def _dot(a, b):
    return jnp.dot(a.astype(BF16), b.astype(BF16), preferred_element_type=F32)


def _split2(a):
    hi = a.astype(BF16)
    lo = (a - hi.astype(F32)).astype(BF16)
    return hi, lo


def _split3(a):
    hi = a.astype(BF16)
    r1 = a - hi.astype(F32)
    mid = r1.astype(BF16)
    lo = (r1 - mid.astype(F32)).astype(BF16)
    return hi, mid, lo


def _dot_x(a, e):
    e = e.astype(BF16)
    hi, lo = _split2(a)
    return jnp.dot(hi, e, preferred_element_type=F32) + jnp.dot(lo, e, preferred_element_type=F32)


def _xdot(e, a):
    e = e.astype(BF16)
    hi, mid, lo = _split3(a)
    return (jnp.dot(e, hi, preferred_element_type=F32) + jnp.dot(e, mid, preferred_element_type=F32)
            + jnp.dot(e, lo, preferred_element_type=F32))


def _dot3(a, b):
    ah, al = _split2(a)
    bh, bl = _split2(b)
    return (jnp.dot(ah, bh, preferred_element_type=F32) + jnp.dot(ah, bl, preferred_element_type=F32)
            + jnp.dot(al, bh, preferred_element_type=F32))


def _dot3_nt(a, b):
    ah, al = _split2(a)
    bh, bl = _split2(b)
    dn = (((1,), (1,)), ((), ()))
    return (lax.dot_general(ah, bh, dn, preferred_element_type=F32)
            + lax.dot_general(ah, bl, dn, preferred_element_type=F32)
            + lax.dot_general(al, bh, dn, preferred_element_type=F32))


def _mm(a, w):
    if w.dtype == BF16:
        return jnp.dot(a.astype(BF16), w, preferred_element_type=F32)
    return _dot3(a, w)


def _sigmoid(x):
    return 1.0 / (1.0 + jnp.exp(-x))


def _ada_kernel(c_ref, w_ref, b_ref, o_ref):
    c = c_ref[...]
    o_ref[...] = _dot3(c * _sigmoid(c), w_ref[...]) + b_ref[...]


def _ada(c_all, w_ada, b_ada):
    rows, d = c_all.shape
    n = w_ada.shape[1]
    tn = n // 4
    return pl.pallas_call(
        _ada_kernel,
        grid=(n // tn,),
        in_specs=[pl.BlockSpec((rows, d), lambda j: (0, 0)),
                  pl.BlockSpec((d, tn), lambda j: (0, j)),
                  pl.BlockSpec((1, tn), lambda j: (0, j))],
        out_specs=pl.BlockSpec((rows, tn), lambda j: (0, j)),
        out_shape=jax.ShapeDtypeStruct((rows, n), F32),
        compiler_params=_cparams("arbitrary"),
        name="ada",
    )(c_all, w_ada, b_ada.reshape(1, n))


N_RW_PARAMS = 9
N_RW_FEATS = 7


def _inproj_kernel(x_ref, sh_ref, sc_ref, g_ref, w_ref, cos_ref, sin_ref, qg_ref, kg_ref, bd_ref, *rest,
                   aw, first_kept, n_kept, rw_fused):
    n_in = 1 + N_RW_PARAMS if rw_fused else 0
    rw_in, outs = rest[:n_in], rest[n_in:]
    q_ref, k_ref, v_ref = outs[:3]
    n_rw_out = N_RW_FEATS + 1 if rw_fused else 1
    rw_out = outs[3:3 + n_rw_out]
    maybe_kv_t = outs[3 + n_rw_out:3 + n_rw_out + n_kept]
    x = x_ref[...]
    ms = jnp.mean(x * x, axis=-1, keepdims=True)
    h = x * lax.rsqrt(ms + NORM_EPS) * g_ref[...] * (1.0 + sc_ref[...]) + sh_ref[...]
    proj = _mm(h, w_ref[...])
    cos = cos_ref[...]
    sin = sin_ref[...]
    lane = lax.broadcasted_iota(I32, (1, aw), 1)
    first_half = (lane % HEAD_DIM) < (HEAD_DIM // 2)
    bd = bd_ref[...]

    def norm_rope(t, g):
        tn = t * lax.rsqrt((_dot if w_ref.dtype == BF16 else _dot_x)(t * t, bd) + NORM_EPS) * g
        rot = jnp.where(first_half, pltpu.roll(tn, aw - HEAD_DIM // 2, 1), pltpu.roll(tn, HEAD_DIM // 2, 1))
        return tn * cos + rot * sin

    scale = 1.0 / np.sqrt(HEAD_DIM).astype(np.float32)
    q_ref[...] = norm_rope(proj[:, :aw], qg_ref[...]) * scale
    k = norm_rope(proj[:, aw:2 * aw], kg_ref[...])
    v = proj[:, 2 * aw:3 * aw]
    k_ref[...] = k
    v_ref[...] = v
    rw = proj[:, 3 * aw:]
    if rw_fused:
        carry_ref = outs[-1]
        tail = rw[rw.shape[0] - SUBLANES:, :]
        before = jnp.where(pl.program_id(1) == 0, rw_in[0][...], carry_ref[SUBLANES - 1:SUBLANES, :])
        for ref, val in zip(rw_out, _rw_features(rw, _shifted(rw, before), *rw_in[1:])):
            ref[...] = val
        carry_ref[...] = tail
        rw_out[-1][...] = tail
    else:
        rw_out[0][...] = rw
    if maybe_kv_t:
        kt_ref, vt_ref = maybe_kv_t
        kept = pl.program_id(1) >= first_kept

        @pl.when(kept)
        def _():
            kt_ref[...] = k.T
            vt_ref[...] = v.T

        @pl.when(jnp.logical_not(kept))
        def _():
            kt_ref[...] = jnp.zeros_like(kt_ref)
            vt_ref[...] = jnp.zeros_like(vt_ref)


def _inproj(x3, mod3, g1, w_in_bf, cos_t, sin_t, qg, kg, bd_mean, tm, keep=0, rw=None):
    nb, t, d = x3.shape
    r = mod3.shape[1]
    ncol = w_in_bf.shape[1]
    aw = cos_t.shape[1]
    rwc = ncol - 3 * aw
    grid = (nb, t // tm)
    row = lambda b, i: (b, i, 0)
    const = lambda b, i: (0, 0)
    once = dict(pipeline_mode=pl.Buffered(1)) if rw else {}
    mod_spec = lambda s: pl.BlockSpec((None, r, d), (lambda b, i: (b, 0, s)) if r == 1 else (lambda b, i: (b, i, s)))
    out_specs = [pl.BlockSpec((None, tm, aw), row)] * 3
    out_shape = [jax.ShapeDtypeStruct((nb, t, aw), F32)] * 3
    rw_in, rw_specs, scratch = (), [], []
    if rw:
        rw_in = tuple(rw)
        rwid = rw_in[2].shape[1]
        rw_specs = [pl.BlockSpec((None, 1, rwc), lambda b, i: (b, 0, 0))] + [
            pl.BlockSpec(a.shape, const, **once) for a in rw_in[1:]]
        out_specs += [pl.BlockSpec((None, tm, rwid), row)] * N_RW_FEATS + [
            pl.BlockSpec((None, SUBLANES, rwc), lambda b, i: (b, 0, 0))]
        out_shape += [jax.ShapeDtypeStruct((nb, t, rwid), F32)] * N_RW_FEATS + [
            jax.ShapeDtypeStruct((nb, SUBLANES, rwc), F32)]
        scratch = [pltpu.VMEM((SUBLANES, rwc), F32)]
    else:
        out_specs += [pl.BlockSpec((None, tm, rwc), row)]
        out_shape += [jax.ShapeDtypeStruct((nb, t, rwc), F32)]
    first_kept = (t - keep) // tm
    if keep:
        assert keep % tm == 0 and (t - keep) % tm == 0
        kept_spec = pl.BlockSpec((None, aw, tm), lambda b, i: (b, 0, jnp.maximum(i - first_kept, 0)))
        out_specs += [kept_spec, kept_spec]
        out_shape += [jax.ShapeDtypeStruct((nb, aw, keep), F32)] * 2
    outs = pl.pallas_call(
        functools.partial(_inproj_kernel, aw=aw, first_kept=first_kept, n_kept=2 if keep else 0, rw_fused=bool(rw)),
        grid=grid,
        in_specs=[pl.BlockSpec((None, tm, d), row), mod_spec(0), mod_spec(1),
                  pl.BlockSpec((1, d), const), pl.BlockSpec((d, ncol), const, **once),
                  pl.BlockSpec((tm, aw), lambda b, i: (i, 0)), pl.BlockSpec((tm, aw), lambda b, i: (i, 0)),
                  pl.BlockSpec((1, aw), const), pl.BlockSpec((1, aw), const),
                  pl.BlockSpec((aw, aw), const, **once)] + rw_specs,
        out_specs=out_specs,
        out_shape=out_shape,
        scratch_shapes=scratch,
        compiler_params=_cparams("parallel", "arbitrary"),
        name="inproj",
    )(x3, mod3, mod3, g1, w_in_bf, cos_t, sin_t, qg, kg, bd_mean, *rw_in)
    return outs


def _attn_prompt_kernel(q_ref, k_ref, v_ref, o_ref, kp_ref, vp_ref, m_ref, l_ref, acc_ref, *, t, pad):
    nq = WINDOW_STEPS
    nk = 2 * WINDOW_STEPS
    kp_ref[pl.ds(0, pad), :] = jnp.zeros((pad, LANES), F32)
    vp_ref[pl.ds(0, pad), :] = jnp.zeros((pad, LANES), F32)
    kp_ref[pl.ds(pad, t), :] = k_ref[...]
    vp_ref[pl.ds(pad, t), :] = v_ref[...]
    head0 = lax.broadcasted_iota(I32, (1, LANES), 1) < HEAD_DIM
    qi = lax.broadcasted_iota(I32, (nq, nk), 0)
    kj = lax.broadcasted_iota(I32, (nq, nk), 1)
    steps_back = qi + nq - kj
    band = (steps_back >= 0) & (steps_back <= WINDOW_STEPS)
    has_past = kj >= nq

    hsels = (head0, jnp.logical_not(head0))
    nt = (((1,), (1,)), ((), ()))

    for p, (_, d) in enumerate(DILATED_PATTERNS):
        def units(g, carry, p=p, d=d):
            rows_q, kb, vb, valid, qh = [], [], [], [], []
            for j in range(ATTN_UNITS):
                u = g * ATTN_UNITS + j
                res = u % d
                blk = u // d
                q_start = res + d * nq * blk
                k_start = pad + q_start - d * nq
                if d == 1:
                    rows_q.append(pl.ds(q_start, nq))
                    rows_k = pl.ds(k_start, nk)
                else:
                    rows_q.append(pl.ds(q_start, nq, stride=d))
                    rows_k = pl.ds(k_start, nk, stride=d)
                q = q_ref[rows_q[j], :]
                kb.append(kp_ref[rows_k, :].astype(BF16))
                vb.append(vp_ref[rows_k, :].astype(BF16))
                valid.append(band & (has_past | (blk > 0)))
                qh.append([jnp.where(hsel, q, 0.0).astype(BF16) for hsel in hsels])
            chains = [(j, h) for j in range(ATTN_UNITS) for h in range(2)]
            s = [lax.dot_general(qh[j][h], kb[j], nt, preferred_element_type=F32) for j, h in chains]
            s = [jnp.where(valid[j], sc, NEG_BIG) for (j, h), sc in zip(chains, s)]
            mx = [jnp.max(sc, axis=-1, keepdims=True) for sc in s]
            e = [jnp.exp(sc - m) for sc, m in zip(s, mx)]
            den = [jnp.sum(ec, axis=-1, keepdims=True) for ec in e]
            o = [jnp.dot(ec.astype(BF16), vb[j], preferred_element_type=F32) for (j, h), ec in zip(chains, e)]
            for j in range(ATTN_UNITS):
                m_ref[p, rows_q[j], :] = jnp.where(head0, mx[2 * j], mx[2 * j + 1])
                l_ref[p, rows_q[j], :] = jnp.where(head0, den[2 * j], den[2 * j + 1])
                acc_ref[p, rows_q[j], :] = jnp.where(head0, o[2 * j], o[2 * j + 1])
            return carry

        lax.fori_loop(0, t // nq // ATTN_UNITS, units, 0)

    rows = 256

    def merge(i, carry):
        sl = pl.ds(pl.multiple_of(i * rows, rows), rows)
        m0, m1, m2 = m_ref[0, sl, :], m_ref[1, sl, :], m_ref[2, sl, :]
        mm = jnp.maximum(jnp.maximum(m0, m1), m2)
        w0, w1, w2 = jnp.exp(m0 - mm), jnp.exp(m1 - mm), jnp.exp(m2 - mm)
        num = w0 * acc_ref[0, sl, :] + w1 * acc_ref[1, sl, :] + w2 * acc_ref[2, sl, :]
        den = w0 * l_ref[0, sl, :] + w1 * l_ref[1, sl, :] + w2 * l_ref[2, sl, :]
        o_ref[sl, :] = num / den
        return carry

    lax.fori_loop(0, t // rows, merge, 0)


def _attn_prompt(q, k, v):
    nb, t, aw = q.shape
    pad = MAX_WINDOW
    assert t % MAX_WINDOW == 0
    spec = pl.BlockSpec((None, t, LANES), lambda b, hp: (b, 0, hp))
    return pl.pallas_call(
        functools.partial(_attn_prompt_kernel, t=t, pad=pad),
        grid=(nb, aw // LANES),
        in_specs=[spec, spec, spec],
        out_specs=spec,
        out_shape=jax.ShapeDtypeStruct((nb, t, aw), F32),
        scratch_shapes=[pltpu.VMEM((pad + t, LANES), F32), pltpu.VMEM((pad + t, LANES), F32),
                        pltpu.VMEM((3, t, LANES), F32), pltpu.VMEM((3, t, LANES), F32),
                        pltpu.VMEM((3, t, LANES), F32)],
        compiler_params=_cparams("parallel", "parallel"),
        name="attn_prompt",
    )(q, k, v)


def _attn_sample_kernel(q_ref, kn_ref, vn_ref, kt_ref, vt_ref, o_ref, *, nh, w):
    aw = nh * HEAD_DIM
    dist = w - lax.broadcasted_iota(I32, (1, w), 1)
    mult = jnp.zeros((1, w), F32)
    for win, d in DILATED_PATTERNS:
        mult = mult + jnp.where((dist % d == 0) & (dist <= win), 1.0, 0.0)
    n_pat = float(len(DILATED_PATTERNS))
    q_col = _col(q_ref[...], aw)
    kn_col = _col(kn_ref[...], aw)
    vn_col = _col(vn_ref[...], aw)
    heads = range(nh)
    hs = [pl.ds(h * HEAD_DIM, HEAD_DIM) for h in heads]
    cut = lambda col, h: col[h * HEAD_DIM:(h + 1) * HEAD_DIM]
    s = [jnp.sum(kt_ref[hs[h], :] * cut(q_col, h), axis=0, keepdims=True) for h in heads]
    s_self = [jnp.sum(cut(q_col, h) * cut(kn_col, h), axis=0, keepdims=True) for h in heads]
    s = [jnp.where(mult > 0.0, sh, NEG_BIG) for sh in s]
    mx = [jnp.maximum(jnp.max(sh, axis=1, keepdims=True), ss) for sh, ss in zip(s, s_self)]
    pr = [mult * jnp.exp(sh - m) for sh, m in zip(s, mx)]
    p_self = [n_pat * jnp.exp(ss - m) for ss, m in zip(s_self, mx)]
    den = [jnp.sum(p, axis=1, keepdims=True) + ps for p, ps in zip(pr, p_self)]
    num = [jnp.sum(vt_ref[hs[h], :] * pr[h], axis=1, keepdims=True) + p_self[h] * cut(vn_col, h) for h in heads]
    outs = [n / dn for n, dn in zip(num, den)]
    o_ref[...] = _row(jnp.concatenate(outs, axis=0), aw)


def _attn_sample(q, kn, vn, cache_kt, cache_vt):
    nb, _, aw = q.shape
    w = cache_kt.shape[2]
    assert w == MAX_WINDOW
    row = pl.BlockSpec((None, 1, aw), lambda i: (i, 0, 0))
    mat = pl.BlockSpec((None, aw, w), lambda i: (i, 0, 0))
    return pl.pallas_call(
        functools.partial(_attn_sample_kernel, nh=aw // HEAD_DIM, w=w),
        grid=(nb,),
        in_specs=[row, row, row, mat, mat],
        out_specs=row,
        out_shape=jax.ShapeDtypeStruct((nb, 1, aw), F32),
        compiler_params=_cparams("parallel"),
        name="attn_sample",
    )(q, kn, vn, cache_kt, cache_vt)


def _rw_features(p, prev, mu_ref, w0_ref, w2_ref, a0_ref, a2_ref, g2_ref, kk_ref, ka_ref, bd_ref):
    rwid = w0_ref.shape[1]
    xs = p + mu_ref[...] * (prev - p)
    r = xs[:, :rwid]
    k = xs[:, rwid:2 * rwid]
    v = xs[:, 2 * rwid:3 * rwid]
    xwa = xs[:, 3 * rwid:3 * rwid + LANES]
    xg = xs[:, 3 * rwid + LANES:]
    z = w0_ref[...] + _mm(jnp.tanh(xwa), w2_ref[...])
    softplus_neg = jnp.maximum(-z, 0.0) + jnp.log(1.0 + jnp.exp(-jnp.abs(z)))
    w = -softplus_neg - 0.5
    a = _sigmoid(a0_ref[...] + _mm(xwa, a2_ref[...]))
    g = _mm(_sigmoid(xg), g2_ref[...])
    kk = k * kk_ref[...]
    norm = jnp.sqrt(_dot_x(kk * kk, bd_ref[...]))
    kk = kk / jnp.maximum(norm, 1e-12)
    return r, -jnp.exp(w), k * (1.0 + (a - 1.0) * ka_ref[...]), v, -kk, kk * a, g


def _shifted(p, before):
    rowi = lax.broadcasted_iota(I32, (p.shape[0], 1), 0)
    return jnp.where(rowi == 0, before, pltpu.roll(p, 1, 0))


def _rwprep_kernel(p_ref, prev_ref, first_ref, *refs, whole_prev):
    params, outs = refs[:9], refs[9:]
    p = p_ref[...]
    if whole_prev:
        prev = first_ref[...]
    else:
        prev = _shifted(p, jnp.where(pl.program_id(1) == 0, first_ref[...], prev_ref[7:8, :]))
    for ref, val in zip(outs, _rw_features(p, prev, *params)):
        ref[...] = val


def _rwprep(rw, first, mu, w0, w2p, a0, a2p, g2, k_k, k_a, bd_ones, tm, whole_prev):
    nb, t, rwc = rw.shape
    rwid = w0.shape[1]
    grid = (nb, t // tm)
    row = lambda b, i: (b, i, 0)
    const = lambda b, i: (0, 0)
    if whole_prev:
        prev_spec = pl.BlockSpec((None, tm, rwc), row)
        first_spec = pl.BlockSpec((None, tm, rwc), row)
    else:
        prev_spec = pl.BlockSpec((None, 8, rwc), lambda b, i: (b, jnp.maximum(i * (tm // 8) - 1, 0), 0))
        first_spec = pl.BlockSpec((None, 1, rwc), lambda b, i: (b, 0, 0))
    vec = pl.BlockSpec((1, rwid), const)
    out = pl.BlockSpec((None, tm, rwid), row)
    return pl.pallas_call(
        functools.partial(_rwprep_kernel, whole_prev=whole_prev),
        grid=grid,
        in_specs=[pl.BlockSpec((None, tm, rwc), row), prev_spec, first_spec,
                  pl.BlockSpec((1, rwc), const), vec, pl.BlockSpec((LANES, rwid), const),
                  vec, pl.BlockSpec((LANES, rwid), const), pl.BlockSpec((LANES, rwid), const), vec, vec,
                  pl.BlockSpec((rwid, rwid), const)],
        out_specs=[out] * 7,
        out_shape=[jax.ShapeDtypeStruct((nb, t, rwid), F32)] * 7,
        compiler_params=_cparams("parallel", "parallel"),
        name="rwprep",
    )(rw, rw, first, mu, w0, w2p, a0, a2p, g2, k_k, k_a, bd_ones)


def _rwchunk_kernel(r_ref, ld_ref, k_ref, v_ref, al_ref, be_ref, y_ref, st_ref, z_ref, *, nb, npb):
    c = RW_CHUNK
    ci = pl.program_id(1)

    @pl.when(ci == 0)
    def _():
        z_ref[...] = jnp.zeros_like(z_ref)

    ti = lax.broadcasted_iota(I32, (c, c), 0)
    si = lax.broadcasted_iota(I32, (c, c), 1)
    low_incl = si <= ti
    low_strict = si < ti
    diag = si == ti
    tri = jnp.where(low_incl, 1.0, 0.0).astype(BF16)
    eye = jnp.where(diag, 1.0, 0.0)
    head0 = lax.broadcasted_iota(I32, (1, LANES), 1) < HEAD_DIM
    hsels = (head0, jnp.logical_not(head0))
    same_head = (ti < HEAD_DIM) == (si < HEAD_DIM)
    nt = (((1,), (1,)), ((), ()))
    batches = range(nb * npb)
    chains = [(b, h) for b in batches for h in range(2)]
    lanes_of = lambda b: pl.ds((b // nb) * LANES, LANES)
    r_v, ld_v, k_v, v_v, al_v, be_v = ([ref[b % nb, :, lanes_of(b)] for b in batches]
                                       for ref in (r_ref, ld_ref, k_ref, v_ref, al_ref, be_ref))

    cum = [_xdot(tri, ld_v[b]) for b in batches]
    tot = [cm[c - 1:c, :] for cm in cum]
    e_neg = [jnp.exp(-cm) for cm in cum]
    at = [al_v[b] * jnp.exp(cum[b] - ld_v[b]) for b in batches]
    rt = [r_v[b] * jnp.exp(cum[b]) for b in batches]
    rhs_t = [jnp.concatenate([be_v[b] * e_neg[b], k_v[b] * e_neg[b]], axis=0).astype(BF16) for b in batches]
    vb = [v_v[b].astype(BF16) for b in batches]
    at_h = [jnp.where(hsels[h], at[b], 0.0) for b, h in chains]
    rt_h = [jnp.where(hsels[h], rt[b], 0.0) for b, h in chains]
    a4 = [lax.dot_general(jnp.concatenate([a, r], axis=0).astype(BF16), rhs_t[b], nt, preferred_element_type=F32)
          for (b, h), a, r in zip(chains, at_h, rt_h)]
    a_ab = [jnp.where(low_strict, m[:c, :c], 0.0) for m in a4]
    a_ak = [jnp.where(low_strict, m[:c, c:], 0.0).astype(BF16) for m in a4]
    a_r = [jnp.concatenate([jnp.where(low_incl, m[c:, :c], 0.0), jnp.where(low_incl, m[c:, c:], 0.0)],
                           axis=1).astype(BF16) for m in a4]
    levels = int(np.log2(c))
    pw = [m.astype(BF16) for m in a_ab]
    pw = [jnp.dot(m, m, preferred_element_type=F32).astype(BF16) for m in pw]
    inv = [eye + m for m in a_ab]
    for _ in range(1, levels - 1):
        x2 = [jnp.dot(m, jnp.concatenate([m, i.astype(BF16)], axis=1), preferred_element_type=F32)
              for m, i in zip(pw, inv)]
        pw = [m[:, :c].astype(BF16) for m in x2]
        inv = [i + m[:, c:] for i, m in zip(inv, x2)]
    inv = [i + jnp.dot(m, i.astype(BF16), preferred_element_type=F32) for i, m in zip(inv, pw)]
    akv = [jnp.dot(m, vb[b], preferred_element_type=F32) for (b, h), m in zip(chains, a_ak)]
    x = [_dot(i, jnp.concatenate([kv, a], axis=1)) for i, kv, a in zip(inv, akv, at_h)]
    u0_h = [m[:, :LANES] for m in x]
    at2_h = [m[:, LANES:] for m in x]
    y0_h = [jnp.dot(ar, jnp.concatenate([u0, v_v[b]], axis=0).astype(BF16), preferred_element_type=F32)
            for (b, h), ar, u0 in zip(chains, a_r, u0_h)]
    rt2_h = [r + jnp.dot(ar[:, :c], a2.astype(BF16), preferred_element_type=F32)
             for r, ar, a2 in zip(rt_h, a_r, at2_h)]
    z = [z_ref[b] for b in batches]
    uy = [_dot(jnp.concatenate([at2_h[2 * b] + at2_h[2 * b + 1], rt2_h[2 * b] + rt2_h[2 * b + 1]], axis=0), z[b])
          for b in batches]
    u = [uy[b][:c] + jnp.where(head0, u0_h[2 * b], u0_h[2 * b + 1]) for b in batches]
    for b in batches:
        y_ref[b % nb, :, lanes_of(b)] = uy[b][c:] + jnp.where(head0, y0_h[2 * b], y0_h[2 * b + 1])
    e_end = [jnp.exp(tot[b] - cum[b]) for b in batches]
    lhs_t = [jnp.concatenate([be_v[b] * e_end[b], k_v[b] * e_end[b]], axis=0) for b in batches]
    zadd = [_dot(lhs_t[b].T, jnp.concatenate([u[b], v_v[b]], axis=0)) for b in batches]
    for b in batches:
        dcol = jnp.sum(jnp.where(diag, jnp.broadcast_to(jnp.exp(tot[b]), (c, c)), 0.0), axis=1, keepdims=True)
        z_ref[b] = dcol * z[b] + jnp.where(same_head, zadd[b], 0.0)

    @pl.when(ci == pl.num_programs(1) - 1)
    def _():
        for b in batches:
            s = z_ref[b].T
            st_ref[b % nb, 2 * (b // nb)] = s[:HEAD_DIM, :HEAD_DIM]
            st_ref[b % nb, 2 * (b // nb) + 1] = s[HEAD_DIM:, HEAD_DIM:]


def _rwchunk(r, ld, k, v, al, be):
    nb, t, rwid = r.shape
    c = RW_CHUNK
    npb = RW_PAIRS
    assert t % c == 0 and c == LANES and rwid % (npb * LANES) == 0
    ngrp = rwid // (npb * LANES)
    seq = pl.BlockSpec((nb, c, npb * LANES), lambda hp, ci: (0, ci, hp))
    return pl.pallas_call(
        functools.partial(_rwchunk_kernel, nb=nb, npb=npb),
        grid=(ngrp, t // c),
        in_specs=[seq] * 6,
        out_specs=[seq, pl.BlockSpec((nb, 2 * npb, HEAD_DIM, HEAD_DIM), lambda hp, ci: (0, hp, 0, 0))],
        out_shape=[jax.ShapeDtypeStruct((nb, t, rwid), F32),
                   jax.ShapeDtypeStruct((nb, 2 * npb * ngrp, HEAD_DIM, HEAD_DIM), F32)],
        scratch_shapes=[pltpu.VMEM((nb * npb, LANES, LANES), F32)],
        compiler_params=_cparams("parallel", "arbitrary"),
        name="rwchunk",
    )(r, ld, k, v, al, be)


def _col(row, width):
    return jnp.broadcast_to(row, (LANES, width)).T[:, 0:1]


def _row(col, width):
    return jnp.broadcast_to(col, (width, LANES)).T[0:1, :]


def _rwstep_kernel(r_ref, ld_ref, k_ref, v_ref, al_ref, be_ref, s_ref, y_ref, so_ref, *, bb, nh):
    rwid = nh * HEAD_DIM

    def one(b, carry):
        r, dcy, k, al, be = (ref[b] for ref in (r_ref, ld_ref, k_ref, al_ref, be_ref))
        dcy = jnp.exp(dcy)
        v_col = _col(v_ref[b], rwid)
        heads = range(nh)
        hs = [slice(h * HEAD_DIM, (h + 1) * HEAD_DIM) for h in heads]
        st = [s_ref[b, h] for h in heads]
        sa = [jnp.sum(st[h] * al[:, hs[h]], axis=1, keepdims=True) for h in heads]
        st = [st[h] * dcy[:, hs[h]] + sa[h] * be[:, hs[h]] + v_col[hs[h], :] * k[:, hs[h]] for h in heads]
        for h in heads:
            so_ref[b, h] = st[h]
        ys = [jnp.sum(st[h] * r[:, hs[h]], axis=1, keepdims=True) for h in heads]
        y_ref[b] = _row(jnp.concatenate(ys, axis=0), rwid)
        return carry

    lax.fori_loop(0, bb, one, 0, unroll=2)


def _rwstep(r, ld, k, v, al, be, state):
    nb, _, rwid = r.shape
    nh = state.shape[1]
    bb = 8
    row = pl.BlockSpec((bb, 1, rwid), lambda i: (i, 0, 0))
    st = pl.BlockSpec((bb, nh, HEAD_DIM, HEAD_DIM), lambda i: (i, 0, 0, 0))
    return pl.pallas_call(
        functools.partial(_rwstep_kernel, bb=bb, nh=nh),
        grid=(nb // bb,),
        in_specs=[row] * 6 + [st],
        out_specs=[row, st],
        out_shape=[jax.ShapeDtypeStruct((nb, 1, rwid), F32), jax.ShapeDtypeStruct(state.shape, F32)],
        compiler_params=_cparams("parallel"),
        name="rwstep",
    )(r, ld, k, v, al, be, state)


def _outproj_kernel(x_ref, at_ref, y_ref, r_ref, k_ref, v_ref, g_ref, gt_ref, sh_ref, sc_ref, g2_ref,
                    wt_ref, wb_ref, rk_ref, lw_ref, lb_ref, bdm_ref, bd1_ref, rwt_ref, rb_ref, *rest, n_slots):
    x1_ref, pos_ref, wt4_ref, cnt_ref, xs_ref, ws_ref, hb_ref = rest[-7:]
    y = y_ref[...]
    bdm = bdm_ref[...]
    mean = _dot_x(y, bdm)
    yc = y - mean
    var = _dot_x(yc * yc, bdm)
    yn = yc * lax.rsqrt(var + GN_EPS) * lw_ref[...] + lb_ref[...]
    v = v_ref[...]
    bonus = _dot_x(r_ref[...] * k_ref[...] * rk_ref[...], bd1_ref[...]) * v
    rw = (yn + bonus) * g_ref[...]
    mix = _mm(at_ref[...], wt_ref[...]) + _mm(rw, wb_ref[...])
    x1 = x_ref[...] + gt_ref[...] * mix
    x1_ref[...] = x1
    ms = jnp.mean(x1 * x1, axis=-1, keepdims=True)
    h2 = x1 * lax.rsqrt(ms + NORM_EPS) * g2_ref[...] * (1.0 + sc_ref[...]) + sh_ref[...]
    pos, wts, cnt = _route_tile(_dot3_nt(rwt_ref[...], h2) + rb_ref[...])
    for kk in range(TOP_K):
        pos_ref[pl.ds(kk, 1), :] = pos[kk]
        wt4_ref[pl.ds(kk, 1), :] = wts[kk]
    cnt_ref[...] = jnp.broadcast_to(cnt, cnt_ref.shape)
    hb_ref[...] = h2.astype(BF16)
    used = jnp.sum(jnp.ceil(cnt / MOE_CHUNK) * MOE_CHUNK).astype(I32)
    _sort_tile(pos, wts, hb_ref, xs_ref, ws_ref, (used + SLOT_ROWS - 1) // SLOT_ROWS)


def _outproj(x3, attn, y, r, k, v, g, mod3, g2, wt_bf, wb_bf, r_k, ln_w, ln_b, bd_mean, bd_ones, rwt, rb, tm,
             n_slots, n_rows, tile0=0, bufs=None):
    nb, t, d = x3.shape
    rmod = mod3.shape[1]
    aw = attn.shape[2]
    ne = rwt.shape[0]
    nt = t // tm
    grid = (nb, nt)
    row = lambda b, i: (b, i, 0)
    const = lambda b, i: (0, 0)
    tile = lambda b, i: (tile0 + b * nt + i, 0)
    mod_spec = lambda s: pl.BlockSpec((None, rmod, d), (lambda b, i: (b, 0, s)) if rmod == 1 else (lambda b, i: (b, i, s)))
    half = pl.BlockSpec((None, tm, aw), row)
    vec = pl.BlockSpec((1, aw), const)
    tok4 = pl.BlockSpec((TOP_K, tm), lambda b, i: (0, b * nt + i))
    bufs = () if bufs is None else tuple(bufs)
    n_in = 20
    return pl.pallas_call(
        functools.partial(_outproj_kernel, n_slots=n_slots),
        grid=grid,
        in_specs=[pl.BlockSpec((None, tm, d), row)] + [half] * 6 + [mod_spec(2), mod_spec(3), mod_spec(4),
                  pl.BlockSpec((1, d), const), pl.BlockSpec((aw, d), const), pl.BlockSpec((aw, d), const),
                  vec, vec, vec, pl.BlockSpec((aw, aw), const), pl.BlockSpec((aw, aw), const),
                  pl.BlockSpec((ne, d), const), pl.BlockSpec((ne, 1), const)]
                 + [pl.BlockSpec(memory_space=pl.ANY)] * len(bufs),
        out_specs=[pl.BlockSpec((None, tm, d), row), tok4, tok4,
                   pl.BlockSpec((None, ne, LANES), lambda b, i: (b * nt + i, 0, 0)),
                   pl.BlockSpec((n_slots, d), tile), pl.BlockSpec((n_slots, LANES), tile)],
        out_shape=[jax.ShapeDtypeStruct((nb, t, d), F32), jax.ShapeDtypeStruct((TOP_K, nb * t), I32),
                   jax.ShapeDtypeStruct((TOP_K, nb * t), F32), jax.ShapeDtypeStruct((nb * nt, ne, LANES), F32),
                   jax.ShapeDtypeStruct((n_rows, d), BF16), jax.ShapeDtypeStruct((n_rows, LANES), F32)],
        scratch_shapes=[pltpu.VMEM((tm, d), BF16)],
        input_output_aliases={n_in: 4, n_in + 1: 5} if bufs else {},
        compiler_params=_cparams("parallel", "parallel"),
        name="outproj",
    )(x3, attn, y, r, k, v, g, mod3, mod3, mod3, g2, wt_bf, wb_bf, r_k, ln_w, ln_b, bd_mean, bd_ones, rwt, rb, *bufs)


def _route_tile(lg):
    ne, tn = lg.shape
    eidx = lax.broadcasted_iota(I32, (ne, tn), 0).astype(F32)
    vals, hots = [], []
    for _ in range(TOP_K):
        mx = jnp.max(lg, axis=0, keepdims=True)
        pick = jnp.min(jnp.where(lg == mx, eidx, float(ne)), axis=0, keepdims=True)
        hot = eidx == pick
        vals.append(mx)
        hots.append(hot)
        lg = jnp.where(hot, -jnp.inf, lg)
    ex = [jnp.exp(vv - vals[0]) for vv in vals]
    den = ex[0] + ex[1] + ex[2] + ex[3]
    wts = [e / den for e in ex]
    hot_all = jnp.zeros((ne, tn), F32)
    for hot in hots:
        hot_all = hot_all + jnp.where(hot, 1.0, 0.0)
    ri = lax.broadcasted_iota(I32, (tn, tn), 0)
    cj = lax.broadcasted_iota(I32, (tn, tn), 1)
    upper = jnp.where(ri <= cj, 1.0, 0.0).astype(BF16)
    before = jnp.dot(hot_all.astype(BF16), upper, preferred_element_type=F32) - hot_all
    cnt = jnp.sum(hot_all, axis=1, keepdims=True)
    padded = jnp.ceil(cnt / MOE_CHUNK) * MOE_CHUNK
    er = lax.broadcasted_iota(I32, (ne, ne), 0)
    ec = lax.broadcasted_iota(I32, (ne, ne), 1)
    lower_strict = jnp.where(ec < er, 1.0, 0.0)
    off = _xdot(lower_strict, jnp.broadcast_to(padded, (ne, LANES)))[:, 0:1]
    pos = [jnp.sum(jnp.where(hot, off + before, 0.0), axis=0, keepdims=True).astype(I32) for hot in hots]
    return pos, wts, cnt


def _slot_capacity(ne):
    return -(-(TOP_K * ROUTE_TILE + ne * (MOE_CHUNK - 1)) // SLOT_ROWS) * SLOT_ROWS


def _sort_tile(pos, wts, h_ref, xs_ref, ws_ref, n_steps):
    tn = h_ref.shape[0]

    def rows(c, carry):
        h = h_ref[...]
        r0 = pl.multiple_of(c * SLOT_ROWS, SLOT_ROWS)
        slot = r0 + lax.broadcasted_iota(I32, (SLOT_ROWS, tn), 0)
        pw = jnp.zeros((SLOT_ROWS, tn), F32)
        for kk in range(TOP_K):
            pw = pw + jnp.where(slot == pos[kk], wts[kk], 0.0)
        p = jnp.where(pw > 0.0, 1.0, 0.0)
        xs_ref[pl.ds(r0, SLOT_ROWS), :] = jnp.dot(p.astype(BF16), h, preferred_element_type=F32).astype(BF16)
        ws_ref[pl.ds(r0, SLOT_ROWS), :] = jnp.broadcast_to(jnp.sum(pw, axis=1, keepdims=True), (SLOT_ROWS, LANES))
        return carry

    lax.fori_loop(0, n_steps, rows, 0)


def _chunk_tables(cnt, n_slots, n_blocks):
    nt, ne = cnt.shape
    per_blk = MOE_BLOCK // MOE_CHUNK
    nch = -(-cnt // MOE_CHUNK)
    seg0 = (jnp.cumsum(nch, axis=1) - nch) + (jnp.arange(nt) * (n_slots // MOE_CHUNK))[:, None]
    cum_t = jnp.cumsum(nch, axis=0)
    total = cum_t[-1]
    blocks = -(-total // per_blk)
    blk_end = jnp.cumsum(blocks)
    blk = jnp.arange(n_blocks)
    blk_e = jnp.minimum(jnp.sum(blk_end[None, :] <= blk[:, None], axis=1), ne - 1).astype(I32)
    nact = blk_end[-1].reshape(1).astype(I32)
    pick_e = blk_e[:, None] == jnp.arange(ne)[None, :]
    of_e = lambda tab: jnp.sum(jnp.where(pick_e, tab[None, :], 0), axis=1)
    col_e = lambda tab: jnp.sum(jnp.where(pick_e[:, None, :], tab[None, :, :], 0), axis=2)
    q = ((blk - of_e(blk_end - blocks)) * per_blk)[:, None] + jnp.arange(per_blk)[None, :]
    live = (q < of_e(total)[:, None]) & (blk < nact[0])[:, None]
    t_n = jnp.minimum(jnp.sum(col_e(cum_t)[:, None, :] <= q[:, :, None], axis=2), nt - 1)
    pick_t = t_n[:, :, None] == jnp.arange(nt)[None, None, :]
    of_t = lambda tab: jnp.sum(jnp.where(pick_t, tab[:, None, :], 0), axis=2)
    src = jnp.where(live, of_t(col_e(seg0)) + q - of_t(col_e(cum_t - nch)), 0)
    n = blk[:, None] * per_blk + jnp.arange(per_blk)[None, :]
    spare = nt * (n_slots // MOE_CHUNK) + n % (N_OBUF * per_blk)
    dst = jnp.where(live, src, spare)
    return blk_e, nact, src.reshape(-1).astype(I32), dst.reshape(-1).astype(I32)


N_OBUF = 2


def _moe_kernel(be_ref, nact_ref, src_ref, dst_ref, xs_ref, ws_ref, wgu_ref, bgu_ref, wd_ref, bd_ref,
                out_ref, xbuf, wbuf, obuf, gsem, ssem, wgu_bf, wd_bf, *, dff):
    i = pl.program_id(0)
    nact = nact_ref[0]
    per_blk = MOE_BLOCK // MOE_CHUNK
    rows = lambda c: pl.ds(pl.multiple_of(c * MOE_CHUNK, MOE_CHUNK), MOE_CHUNK)

    def gathers(blk, slot):
        copies = []
        for m in range(per_blk):
            c = src_ref[blk * per_blk + m]
            copies.append(pltpu.make_async_copy(xs_ref.at[rows(c), :], xbuf.at[slot, rows(m), :], gsem.at[slot]))
            copies.append(pltpu.make_async_copy(ws_ref.at[rows(c), :], wbuf.at[slot, rows(m), :], gsem.at[slot]))
        return copies

    def scatters(blk, slot):
        return [pltpu.make_async_copy(obuf.at[slot, rows(m), :], out_ref.at[rows(dst_ref[blk * per_blk + m]), :],
                                      ssem.at[slot]) for m in range(per_blk)]

    @pl.when(i == 0)
    def _():
        for cp in gathers(0, 0):
            cp.start()

    @pl.when(i < nact)
    def _():
        slot = i % 2
        for cp in gathers(i, slot):
            cp.wait()

        @pl.when(i + 1 < nact)
        def _():
            for cp in gathers(i + 1, 1 - slot):
                cp.start()

        @pl.when((i == 0) | (be_ref[i] != be_ref[jnp.maximum(i - 1, 0)]))
        def _():
            wgu_bf[...] = wgu_ref[...].astype(BF16)
            wd_bf[...] = wd_ref[...].astype(BF16)

        gu = jnp.dot(xbuf[slot], wgu_bf[...], preferred_element_type=F32) + bgu_ref[...]
        gate = jnp.minimum(gu[:, :dff], SWIGLU_LIMIT)
        up = jnp.clip(gu[:, dff:], -SWIGLU_LIMIT, SWIGLU_LIMIT)
        act = (up + 1.0) * gate * _sigmoid(gate * SWIGLU_ALPHA)
        res = jnp.dot(act.astype(BF16), wd_bf[...], preferred_element_type=F32) + bd_ref[...]
        obuf[slot] = (res * wbuf[slot][:, 0:1]).astype(BF16)

        @pl.when(i >= 1)
        def _():
            for cp in scatters(i - 1, 1 - slot):
                cp.wait()

        for cp in scatters(i, slot):
            cp.start()

        @pl.when(i == nact - 1)
        def _():
            for cp in scatters(i, slot):
                cp.wait()


def _moe(blk_e, nact, src, dst, xs, ws, w_gu, b_gu, w_down, b_down, n_blocks):
    ne, d, dff2 = w_gu.shape
    dff = dff2 // 2
    blk = MOE_BLOCK
    grid_spec = pltpu.PrefetchScalarGridSpec(
        num_scalar_prefetch=4,
        grid=(n_blocks,),
        in_specs=[pl.BlockSpec(memory_space=pl.ANY), pl.BlockSpec(memory_space=pl.ANY),
                  pl.BlockSpec((None, d, dff2), lambda i, be, *_: (be[i], 0, 0)),
                  pl.BlockSpec((None, 1, dff2), lambda i, be, *_: (be[i], 0, 0)),
                  pl.BlockSpec((None, dff, d), lambda i, be, *_: (be[i], 0, 0)),
                  pl.BlockSpec((None, 1, d), lambda i, be, *_: (be[i], 0, 0))],
        out_specs=pl.BlockSpec(memory_space=pl.ANY),
        scratch_shapes=[pltpu.VMEM((2, blk, d), BF16), pltpu.VMEM((2, blk, LANES), F32),
                        pltpu.VMEM((N_OBUF, blk, d), BF16),
                        pltpu.SemaphoreType.DMA((2,)), pltpu.SemaphoreType.DMA((N_OBUF,)),
                        pltpu.VMEM((d, dff2), BF16), pltpu.VMEM((dff, d), BF16)],
    )
    return pl.pallas_call(
        functools.partial(_moe_kernel, dff=dff),
        grid_spec=grid_spec,
        out_shape=jax.ShapeDtypeStruct(xs.shape, BF16),
        compiler_params=_cparams("arbitrary"),
        name="moe",
    )(blk_e, nact, src, dst, xs, ws, w_gu, b_gu.reshape(ne, 1, dff2), w_down, b_down.reshape(ne, 1, d))


def _combine_kernel(pos_ref, used_ref, ys_ref, x1_ref, gt_ref, o_ref, p_ref, *, n_slots):
    tn = x1_ref.shape[0]
    filled = lax.broadcasted_iota(I32, (n_slots, 1), 0) < used_ref[0:1, 0:1]
    ys = jnp.where(filled, ys_ref[...], jnp.zeros((), BF16))
    lane = lax.broadcasted_iota(I32, (tn, SLOT_ROWS), 1)
    pos = [jnp.broadcast_to(pos_ref[:, kk:kk + 1], (tn, SLOT_ROWS)) - lane for kk in range(TOP_K)]

    def cols(c, carry):
        c0 = pl.multiple_of(c * SLOT_ROWS, SLOT_ROWS)
        p = jnp.zeros((tn, SLOT_ROWS), F32)
        for kk in range(TOP_K):
            p = p + jnp.where(pos[kk] == c0, 1.0, 0.0)
        p_ref[:, pl.ds(c0, SLOT_ROWS)] = p.astype(BF16)
        return carry

    lax.fori_loop(0, n_slots // SLOT_ROWS, cols, 0)
    o_ref[...] = x1_ref[...] + gt_ref[...] * jnp.dot(p_ref[...], ys, preferred_element_type=F32)


def _combine(pos_t, used, ys, x1, mod3, tile0, rows, n_slots):
    nb, t, d = x1.shape
    rmod = mod3.shape[1]
    nt = t // rows
    per_tile = ROUTE_TILE // rows
    row = lambda b, i: (b, i, 0)
    gate = pl.BlockSpec((None, rmod, d), (lambda b, i: (b, 0, 5)) if rmod == 1 else (lambda b, i: (b, i, 5)))
    return pl.pallas_call(
        functools.partial(_combine_kernel, n_slots=n_slots),
        grid=(nb, nt),
        in_specs=[pl.BlockSpec((rows, TOP_K), lambda b, i: (b * nt + i, 0)),
                  pl.BlockSpec((None, SUBLANES, LANES), lambda b, i: (tile0 + (b * nt + i) // per_tile, 0, 0)),
                  pl.BlockSpec((n_slots, d), lambda b, i: (tile0 + (b * nt + i) // per_tile, 0)),
                  pl.BlockSpec((None, rows, d), row), gate],
        out_specs=pl.BlockSpec((None, rows, d), row),
        out_shape=jax.ShapeDtypeStruct((nb, t, d), F32),
        scratch_shapes=[pltpu.VMEM((rows, n_slots), BF16)],
        compiler_params=_cparams("parallel", "parallel"),
        name="combine",
    )(pos_t, used, ys, x1, mod3)


def _rope_tables(pos, n_heads):
    half = HEAD_DIM // 2
    inv_freq = 1.0 / (ROPE_THETA ** (jnp.arange(0, HEAD_DIM, 2, dtype=F32) / HEAD_DIM))
    ang = pos.astype(F32)[:, None] * inv_freq[None, :]
    cos, sin = jnp.cos(ang), jnp.sin(ang)
    del half
    return (jnp.tile(jnp.concatenate([cos, cos], axis=-1), (1, n_heads)),
            jnp.tile(jnp.concatenate([-sin, sin], axis=-1), (1, n_heads)))


def _block_diag(width, value):
    h = np.arange(width) // HEAD_DIM
    return jnp.asarray(np.where(h[:, None] == h[None, :], value, 0.0), F32)


def kernel(x_prompt, x_sample, cache_k, cache_v, state_wkv, state_shift, c_prompt, c_sample, w_ada, b_ada, norm1_g, norm2_g, w_in, q_norm_g, k_norm_g, rwkv_mu, rwkv_w0, rwkv_w2, rwkv_a0, rwkv_a2, rwkv_g2, rwkv_k_k, rwkv_k_a, rwkv_r_k, rwkv_ln_w, rwkv_ln_b, w_out, router_w, router_b, moe_w_gu, moe_b_gu, moe_w_down, moe_b_down):
    nbp, t, d = x_prompt.shape
    nbs, ts, _ = x_sample.shape
    depth = w_ada.shape[0]
    assert depth == 1 and ts == 1
    n_heads = cache_k.shape[3]
    aw = n_heads * HEAD_DIM
    rwid = rwkv_w0.shape[1]
    rwc = rwkv_mu.shape[1]
    past = cache_k.shape[2]
    ne = router_w.shape[2]
    keep = min(MAX_WINDOW, t)
    lyr = 0

    w_in_bf = w_in[lyr].astype(BF16)
    wt_bf = w_out[lyr][:aw].astype(BF16)
    wb_bf = w_out[lyr][aw:].astype(BF16)
    g1 = norm1_g[lyr].reshape(1, d)
    g2 = norm2_g[lyr].reshape(1, d)
    qg = jnp.tile(q_norm_g[lyr], n_heads).reshape(1, aw)
    kg = jnp.tile(k_norm_g[lyr], n_heads).reshape(1, aw)
    bd_mean_a = _block_diag(aw, 1.0 / HEAD_DIM)
    bd_mean_r = _block_diag(rwid, 1.0 / HEAD_DIM)
    bd_ones_r = _block_diag(rwid, 1.0)
    dl = rwkv_w2.shape[1]
    w2p = jnp.zeros((LANES, rwid), F32).at[:dl].set(rwkv_w2[lyr])
    a2p = jnp.zeros((LANES, rwid), F32).at[dl:dl + rwkv_a2.shape[1]].set(rwkv_a2[lyr])
    vec = lambda a: a[lyr].reshape(1, -1)
    rwt = router_w[lyr].T
    rb = router_b[lyr].reshape(ne, 1)

    rows_c = nbp + nbs
    rows_pad = -(-rows_c // 8) * 8
    c_all = jnp.zeros((rows_pad, d), F32).at[:nbp].set(c_prompt).at[nbp:rows_c].set(c_sample)
    mod = _ada(c_all, w_ada[lyr], b_ada[lyr])
    mod_p = mod[:nbp].reshape(nbp, 1, 6 * d)
    mod_s = mod[nbp:rows_c].reshape(1, nbs, 6 * d)
    xs3 = x_sample.reshape(1, nbs, d)

    cos_p, sin_p = _rope_tables(jnp.arange(t), n_heads)
    cos_s, sin_s = _rope_tables(jnp.full((nbs,), PAST_LEN), n_heads)
    rw_args = (vec(rwkv_mu), vec(rwkv_w0), w2p, vec(rwkv_a0), a2p, rwkv_g2[lyr], vec(rwkv_k_k), vec(rwkv_k_a), bd_ones_r)
    rw_args_p = tuple(a.astype(BF16) if i in (2, 4, 5) else a for i, a in enumerate(rw_args))
    qp, kp, vp, *pre_p, tail_p, kt_p, vt_p = _inproj(x_prompt, mod_p, g1, w_in_bf, cos_p, sin_p, qg, kg, bd_mean_a,
                                                     ROW_TILE, keep=keep, rw=(jnp.zeros((nbp, 1, rwc), F32), *rw_args_p))
    qs, ks, vs, rws = _inproj(xs3, mod_s, g1, w_in[lyr], cos_s, sin_s, qg, kg, bd_mean_a, nbs)

    attn_p = _attn_prompt(qp, kp, vp)
    as_rows = lambda a: a.reshape(nbs, 1, -1)
    cache_t = lambda cch: jnp.transpose(cch[lyr], (0, 2, 3, 1)).reshape(nbs, aw, past)
    attn_s = _attn_sample(as_rows(qs), as_rows(ks), as_rows(vs), cache_t(cache_k), cache_t(cache_v))

    pre_s = _rwprep(rws, state_shift[lyr].reshape(1, nbs, rwc), *rw_args, tm=nbs, whole_prev=True)
    r_p, ld_p, k_p, v_p, al_p, be_p, g_p = pre_p
    y_p, wkv_p = _rwchunk(r_p, ld_p, k_p, v_p, al_p, be_p)
    r_s, ld_s, k_s, v_s, al_s, be_s, g_s = pre_s
    y_s, wkv_s = _rwstep(*(as_rows(a) for a in (r_s, ld_s, k_s, v_s, al_s, be_s)), state_wkv[lyr])
    y_s = y_s.reshape(1, nbs, rwid)

    n_p = nbp * t
    n_valid = n_p + nbs
    assert t % ROUTE_TILE == 0 and nbs <= ROUTE_TILE and ROUTE_TILE % nbs == 0 and d == SUBLANES * LANES
    n_tiles = n_p // ROUTE_TILE + 1
    n_slots = _slot_capacity(ne)
    n_rows = n_tiles * n_slots + N_OBUF * MOE_BLOCK
    op_args = (rwkv_r_k[lyr].reshape(1, rwid), vec(rwkv_ln_w), vec(rwkv_ln_b), bd_mean_r, bd_ones_r, rwt, rb)
    x1_p, pos_p, _, cnt_p, *bufs = _outproj(x_prompt, attn_p, y_p, r_p, k_p, v_p, g_p, mod_p, g2, wt_bf, wb_bf,
                                            *op_args, tm=ROUTE_TILE, n_slots=n_slots, n_rows=n_rows)
    x1_s, pos_s, _, cnt_s, xs, ws = _outproj(xs3, attn_s.reshape(1, nbs, aw), y_s, r_s, k_s, v_s, g_s, mod_s, g2,
                                             w_out[lyr][:aw], w_out[lyr][aw:], *op_args, tm=nbs, n_slots=n_slots,
                                             n_rows=n_rows, tile0=n_p // ROUTE_TILE, bufs=bufs)

    cnt = jnp.concatenate([cnt_p, cnt_s])[:, :, 0].astype(I32)
    per_blk = MOE_BLOCK // MOE_CHUNK
    n_blocks = -(-(n_valid * TOP_K // MOE_CHUNK + n_tiles * ne) // per_blk) + ne
    blk_e, nact, src, dst = _chunk_tables(cnt, n_slots, n_blocks)
    ys = _moe(blk_e, nact, src, dst, xs, ws, moe_w_gu[lyr], moe_b_gu[lyr], moe_w_down[lyr], moe_b_down[lyr], n_blocks)
    used = jnp.sum(-(-cnt // MOE_CHUNK) * MOE_CHUNK, axis=1)
    used = jnp.broadcast_to(used[:, None, None], (n_tiles, SUBLANES, LANES))
    y_prompt = _combine(pos_p.T, used, ys, x1_p, mod_p, 0, ROUTE_TILE, n_slots)
    y_sample = _combine(pos_s.T, used, ys, x1_s, mod_s, n_p // ROUTE_TILE, nbs, n_slots)

    kept = lambda a: jnp.transpose(a.reshape(nbp, n_heads, HEAD_DIM, keep), (0, 3, 1, 2))[None]
    return (y_prompt, y_sample.reshape(nbs, ts, d), kept(kt_p), kept(vt_p), wkv_p[None], tail_p[:, SUBLANES - 1][None],
            ks.reshape(nbs, ts, n_heads, HEAD_DIM)[None], vs.reshape(nbs, ts, n_heads, HEAD_DIM)[None],
            wkv_s[None], rws.reshape(nbs, rwc)[None])
```

```python
import functools

import numpy as np
import jax
import jax.numpy as jnp
from jax import lax
from jax.experimental import pallas as pl
from jax.experimental.pallas import tpu as pltpu

F32 = jnp.float32
BF16 = jnp.bfloat16
I32 = jnp.int32

HEAD_DIM = 64
LANES = 128
SUBLANES = 8
DILATED_PATTERNS = ((128, 1), (512, 4), (2048, 16))
WINDOW_STEPS = 128
MAX_WINDOW = 2048
PAST_LEN = 16384
ROPE_THETA = 10000.0
NORM_EPS = 1e-6
GN_EPS = 64e-5
TOP_K = 4
SWIGLU_ALPHA = 1.702
SWIGLU_LIMIT = 7.0
RW_CHUNK = 128
RW_PAIRS = 2
MOE_BLOCK = 512
MOE_CHUNK = 16
SLOT_ROWS = 256
ATTN_UNITS = 8
ROW_TILE = 512
ROUTE_TILE = 512
VMEM_LIMIT = 56 * 1024 * 1024
NEG_BIG = -1e30


def _cparams(*sem):
    return pltpu.CompilerParams(dimension_semantics=sem, vmem_limit_bytes=VMEM_LIMIT)


def _dot(a, b):
    return jnp.dot(a.astype(BF16), b.astype(BF16), preferred_element_type=F32)


def _split2(a):
    hi = a.astype(BF16)
    lo = (a - hi.astype(F32)).astype(BF16)
    return hi, lo


def _split3(a):
    hi = a.astype(BF16)
    r1 = a - hi.astype(F32)
    mid = r1.astype(BF16)
    lo = (r1 - mid.astype(F32)).astype(BF16)
    return hi, mid, lo


def _dot_x(a, e):
    e = e.astype(BF16)
    hi, lo = _split2(a)
    return jnp.dot(hi, e, preferred_element_type=F32) + jnp.dot(lo, e, preferred_element_type=F32)


def _xdot(e, a):
    e = e.astype(BF16)
    hi, mid, lo = _split3(a)
    return (jnp.dot(e, hi, preferred_element_type=F32) + jnp.dot(e, mid, preferred_element_type=F32)
            + jnp.dot(e, lo, preferred_element_type=F32))


def _dot3(a, b):
    ah, al = _split2(a)
    bh, bl = _split2(b)
    return (jnp.dot(ah, bh, preferred_element_type=F32) + jnp.dot(ah, bl, preferred_element_type=F32)
            + jnp.dot(al, bh, preferred_element_type=F32))


def _dot3_nt(a, b):
    ah, al = _split2(a)
    bh, bl = _split2(b)
    dn = (((1,), (1,)), ((), ()))
    return (lax.dot_general(ah, bh, dn, preferred_element_type=F32)
            + lax.dot_general(ah, bl, dn, preferred_element_type=F32)
            + lax.dot_general(al, bh, dn, preferred_element_type=F32))


def _mm(a, w):
    if w.dtype == BF16:
        return jnp.dot(a.astype(BF16), w, preferred_element_type=F32)
    return _dot3(a, w)


def _sigmoid(x):
    return 1.0 / (1.0 + jnp.exp(-x))


def _ada_kernel(c_ref, w_ref, b_ref, o_ref):
    c = c_ref[...]
    o_ref[...] = _dot3(c * _sigmoid(c), w_ref[...]) + b_ref[...]


def _ada(c_all, w_ada, b_ada):
    rows, d = c_all.shape
    n = w_ada.shape[1]
    tn = n // 4
    return pl.pallas_call(
        _ada_kernel,
        grid=(n // tn,),
        in_specs=[pl.BlockSpec((rows, d), lambda j: (0, 0)),
                  pl.BlockSpec((d, tn), lambda j: (0, j)),
                  pl.BlockSpec((1, tn), lambda j: (0, j))],
        out_specs=pl.BlockSpec((rows, tn), lambda j: (0, j)),
        out_shape=jax.ShapeDtypeStruct((rows, n), F32),
        compiler_params=_cparams("arbitrary"),
        name="ada",
    )(c_all, w_ada, b_ada.reshape(1, n))


N_RW_PARAMS = 9
N_RW_FEATS = 7


def _inproj_kernel(x_ref, sh_ref, sc_ref, g_ref, w_ref, cos_ref, sin_ref, qg_ref, kg_ref, bd_ref, *rest,
                   aw, first_kept, n_kept, rw_fused):
    n_in = 1 + N_RW_PARAMS if rw_fused else 0
    rw_in, outs = rest[:n_in], rest[n_in:]
    q_ref, k_ref, v_ref = outs[:3]
    n_rw_out = N_RW_FEATS + 1 if rw_fused else 1
    rw_out = outs[3:3 + n_rw_out]
    maybe_kv_t = outs[3 + n_rw_out:3 + n_rw_out + n_kept]
    x = x_ref[...]
    ms = jnp.mean(x * x, axis=-1, keepdims=True)
    h = x * lax.rsqrt(ms + NORM_EPS) * g_ref[...] * (1.0 + sc_ref[...]) + sh_ref[...]
    proj = _mm(h, w_ref[...])
    cos = cos_ref[...]
    sin = sin_ref[...]
    lane = lax.broadcasted_iota(I32, (1, aw), 1)
    first_half = (lane % HEAD_DIM) < (HEAD_DIM // 2)
    bd = bd_ref[...]

    def norm_rope(t, g):
        tn = t * lax.rsqrt((_dot if w_ref.dtype == BF16 else _dot_x)(t * t, bd) + NORM_EPS) * g
        rot = jnp.where(first_half, pltpu.roll(tn, aw - HEAD_DIM // 2, 1), pltpu.roll(tn, HEAD_DIM // 2, 1))
        return tn * cos + rot * sin

    scale = 1.0 / np.sqrt(HEAD_DIM).astype(np.float32)
    q_ref[...] = norm_rope(proj[:, :aw], qg_ref[...]) * scale
    k = norm_rope(proj[:, aw:2 * aw], kg_ref[...])
    v = proj[:, 2 * aw:3 * aw]
    k_ref[...] = k
    v_ref[...] = v
    rw = proj[:, 3 * aw:]
    if rw_fused:
        carry_ref = outs[-1]
        tail = rw[rw.shape[0] - SUBLANES:, :]
        before = jnp.where(pl.program_id(1) == 0, rw_in[0][...], carry_ref[SUBLANES - 1:SUBLANES, :])
        for ref, val in zip(rw_out, _rw_features(rw, _shifted(rw, before), *rw_in[1:])):
            ref[...] = val
        carry_ref[...] = tail
        rw_out[-1][...] = tail
    else:
        rw_out[0][...] = rw
    if maybe_kv_t:
        kt_ref, vt_ref = maybe_kv_t
        kept = pl.program_id(1) >= first_kept

        @pl.when(kept)
        def _():
            kt_ref[...] = k.T
            vt_ref[...] = v.T

        @pl.when(jnp.logical_not(kept))
        def _():
            kt_ref[...] = jnp.zeros_like(kt_ref)
            vt_ref[...] = jnp.zeros_like(vt_ref)


def _inproj(x3, mod3, g1, w_in_bf, cos_t, sin_t, qg, kg, bd_mean, tm, keep=0, rw=None):
    nb, t, d = x3.shape
    r = mod3.shape[1]
    ncol = w_in_bf.shape[1]
    aw = cos_t.shape[1]
    rwc = ncol - 3 * aw
    grid = (nb, t // tm)
    row = lambda b, i: (b, i, 0)
    const = lambda b, i: (0, 0)
    once = dict(pipeline_mode=pl.Buffered(1)) if rw else {}
    mod_spec = lambda s: pl.BlockSpec((None, r, d), (lambda b, i: (b, 0, s)) if r == 1 else (lambda b, i: (b, i, s)))
    out_specs = [pl.BlockSpec((None, tm, aw), row)] * 3
    out_shape = [jax.ShapeDtypeStruct((nb, t, aw), F32)] * 3
    rw_in, rw_specs, scratch = (), [], []
    if rw:
        rw_in = tuple(rw)
        rwid = rw_in[2].shape[1]
        rw_specs = [pl.BlockSpec((None, 1, rwc), lambda b, i: (b, 0, 0))] + [
            pl.BlockSpec(a.shape, const, **once) for a in rw_in[1:]]
        out_specs += [pl.BlockSpec((None, tm, rwid), row)] * N_RW_FEATS + [
            pl.BlockSpec((None, SUBLANES, rwc), lambda b, i: (b, 0, 0))]
        out_shape += [jax.ShapeDtypeStruct((nb, t, rwid), F32)] * N_RW_FEATS + [
            jax.ShapeDtypeStruct((nb, SUBLANES, rwc), F32)]
        scratch = [pltpu.VMEM((SUBLANES, rwc), F32)]
    else:
        out_specs += [pl.BlockSpec((None, tm, rwc), row)]
        out_shape += [jax.ShapeDtypeStruct((nb, t, rwc), F32)]
    first_kept = (t - keep) // tm
    if keep:
        assert keep % tm == 0 and (t - keep) % tm == 0
        kept_spec = pl.BlockSpec((None, aw, tm), lambda b, i: (b, 0, jnp.maximum(i - first_kept, 0)))
        out_specs += [kept_spec, kept_spec]
        out_shape += [jax.ShapeDtypeStruct((nb, aw, keep), F32)] * 2
    outs = pl.pallas_call(
        functools.partial(_inproj_kernel, aw=aw, first_kept=first_kept, n_kept=2 if keep else 0, rw_fused=bool(rw)),
        grid=grid,
        in_specs=[pl.BlockSpec((None, tm, d), row), mod_spec(0), mod_spec(1),
                  pl.BlockSpec((1, d), const), pl.BlockSpec((d, ncol), const, **once),
                  pl.BlockSpec((tm, aw), lambda b, i: (i, 0)), pl.BlockSpec((tm, aw), lambda b, i: (i, 0)),
                  pl.BlockSpec((1, aw), const), pl.BlockSpec((1, aw), const),
                  pl.BlockSpec((aw, aw), const, **once)] + rw_specs,
        out_specs=out_specs,
        out_shape=out_shape,
        scratch_shapes=scratch,
        compiler_params=_cparams("parallel", "arbitrary"),
        name="inproj",
    )(x3, mod3, mod3, g1, w_in_bf, cos_t, sin_t, qg, kg, bd_mean, *rw_in)
    return outs


def _attn_prompt_kernel(q_ref, k_ref, v_ref, o_ref, kp_ref, vp_ref, m_ref, l_ref, acc_ref, *, t, pad):
    nq = WINDOW_STEPS
    nk = 2 * WINDOW_STEPS
    kp_ref[pl.ds(0, pad), :] = jnp.zeros((pad, LANES), F32)
    vp_ref[pl.ds(0, pad), :] = jnp.zeros((pad, LANES), F32)
    kp_ref[pl.ds(pad, t), :] = k_ref[...]
    vp_ref[pl.ds(pad, t), :] = v_ref[...]
    head0 = lax.broadcasted_iota(I32, (1, LANES), 1) < HEAD_DIM
    qi = lax.broadcasted_iota(I32, (nq, nk), 0)
    kj = lax.broadcasted_iota(I32, (nq, nk), 1)
    steps_back = qi + nq - kj
    band = (steps_back >= 0) & (steps_back <= WINDOW_STEPS)
    has_past = kj >= nq

    hsels = (head0, jnp.logical_not(head0))
    nt = (((1,), (1,)), ((), ()))

    for p, (_, d) in enumerate(DILATED_PATTERNS):
        def units(g, carry, p=p, d=d):
            rows_q, kb, vb, valid, qh = [], [], [], [], []
            for j in range(ATTN_UNITS):
                u = g * ATTN_UNITS + j
                res = u % d
                blk = u // d
                q_start = res + d * nq * blk
                k_start = pad + q_start - d * nq
                if d == 1:
                    rows_q.append(pl.ds(q_start, nq))
                    rows_k = pl.ds(k_start, nk)
                else:
                    rows_q.append(pl.ds(q_start, nq, stride=d))
                    rows_k = pl.ds(k_start, nk, stride=d)
                q = q_ref[rows_q[j], :]
                kb.append(kp_ref[rows_k, :].astype(BF16))
                vb.append(vp_ref[rows_k, :].astype(BF16))
                valid.append(band & (has_past | (blk > 0)))
                qh.append([jnp.where(hsel, q, 0.0).astype(BF16) for hsel in hsels])
            chains = [(j, h) for j in range(ATTN_UNITS) for h in range(2)]
            s = [lax.dot_general(qh[j][h], kb[j], nt, preferred_element_type=F32) for j, h in chains]
            s = [jnp.where(valid[j], sc, NEG_BIG) for (j, h), sc in zip(chains, s)]
            mx = [jnp.max(sc, axis=-1, keepdims=True) for sc in s]
            e = [jnp.exp(sc - m) for sc, m in zip(s, mx)]
            den = [jnp.sum(ec, axis=-1, keepdims=True) for ec in e]
            o = [jnp.dot(ec.astype(BF16), vb[j], preferred_element_type=F32) for (j, h), ec in zip(chains, e)]
            for j in range(ATTN_UNITS):
                m_ref[p, rows_q[j], :] = jnp.where(head0, mx[2 * j], mx[2 * j + 1])
                l_ref[p, rows_q[j], :] = jnp.where(head0, den[2 * j], den[2 * j + 1])
                acc_ref[p, rows_q[j], :] = jnp.where(head0, o[2 * j], o[2 * j + 1])
            return carry

        lax.fori_loop(0, t // nq // ATTN_UNITS, units, 0)

    rows = 256

    def merge(i, carry):
        sl = pl.ds(pl.multiple_of(i * rows, rows), rows)
        m0, m1, m2 = m_ref[0, sl, :], m_ref[1, sl, :], m_ref[2, sl, :]
        mm = jnp.maximum(jnp.maximum(m0, m1), m2)
        w0, w1, w2 = jnp.exp(m0 - mm), jnp.exp(m1 - mm), jnp.exp(m2 - mm)
        num = w0 * acc_ref[0, sl, :] + w1 * acc_ref[1, sl, :] + w2 * acc_ref[2, sl, :]
        den = w0 * l_ref[0, sl, :] + w1 * l_ref[1, sl, :] + w2 * l_ref[2, sl, :]
        o_ref[sl, :] = num / den
        return carry

    lax.fori_loop(0, t // rows, merge, 0)


def _attn_prompt(q, k, v):
    nb, t, aw = q.shape
    pad = MAX_WINDOW
    assert t % MAX_WINDOW == 0
    spec = pl.BlockSpec((None, t, LANES), lambda b, hp: (b, 0, hp))
    return pl.pallas_call(
        functools.partial(_attn_prompt_kernel, t=t, pad=pad),
        grid=(nb, aw // LANES),
        in_specs=[spec, spec, spec],
        out_specs=spec,
        out_shape=jax.ShapeDtypeStruct((nb, t, aw), F32),
        scratch_shapes=[pltpu.VMEM((pad + t, LANES), F32), pltpu.VMEM((pad + t, LANES), F32),
                        pltpu.VMEM((3, t, LANES), F32), pltpu.VMEM((3, t, LANES), F32),
                        pltpu.VMEM((3, t, LANES), F32)],
        compiler_params=_cparams("parallel", "parallel"),
        name="attn_prompt",
    )(q, k, v)


def _attn_sample_kernel(q_ref, kn_ref, vn_ref, kt_ref, vt_ref, o_ref, *, nh, w):
    aw = nh * HEAD_DIM
    dist = w - lax.broadcasted_iota(I32, (1, w), 1)
    mult = jnp.zeros((1, w), F32)
    for win, d in DILATED_PATTERNS:
        mult = mult + jnp.where((dist % d == 0) & (dist <= win), 1.0, 0.0)
    n_pat = float(len(DILATED_PATTERNS))
    q_col = _col(q_ref[...], aw)
    kn_col = _col(kn_ref[...], aw)
    vn_col = _col(vn_ref[...], aw)
    heads = range(nh)
    hs = [pl.ds(h * HEAD_DIM, HEAD_DIM) for h in heads]
    cut = lambda col, h: col[h * HEAD_DIM:(h + 1) * HEAD_DIM]
    s = [jnp.sum(kt_ref[hs[h], :] * cut(q_col, h), axis=0, keepdims=True) for h in heads]
    s_self = [jnp.sum(cut(q_col, h) * cut(kn_col, h), axis=0, keepdims=True) for h in heads]
    s = [jnp.where(mult > 0.0, sh, NEG_BIG) for sh in s]
    mx = [jnp.maximum(jnp.max(sh, axis=1, keepdims=True), ss) for sh, ss in zip(s, s_self)]
    pr = [mult * jnp.exp(sh - m) for sh, m in zip(s, mx)]
    p_self = [n_pat * jnp.exp(ss - m) for ss, m in zip(s_self, mx)]
    den = [jnp.sum(p, axis=1, keepdims=True) + ps for p, ps in zip(pr, p_self)]
    num = [jnp.sum(vt_ref[hs[h], :] * pr[h], axis=1, keepdims=True) + p_self[h] * cut(vn_col, h) for h in heads]
    outs = [n / dn for n, dn in zip(num, den)]
    o_ref[...] = _row(jnp.concatenate(outs, axis=0), aw)


def _attn_sample(q, kn, vn, cache_kt, cache_vt):
    nb, _, aw = q.shape
    w = cache_kt.shape[2]
    assert w == MAX_WINDOW
    row = pl.BlockSpec((None, 1, aw), lambda i: (i, 0, 0))
    mat = pl.BlockSpec((None, aw, w), lambda i: (i, 0, 0))
    return pl.pallas_call(
        functools.partial(_attn_sample_kernel, nh=aw // HEAD_DIM, w=w),
        grid=(nb,),
        in_specs=[row, row, row, mat, mat],
        out_specs=row,
        out_shape=jax.ShapeDtypeStruct((nb, 1, aw), F32),
        compiler_params=_cparams("parallel"),
        name="attn_sample",
    )(q, kn, vn, cache_kt, cache_vt)


def _rw_features(p, prev, mu_ref, w0_ref, w2_ref, a0_ref, a2_ref, g2_ref, kk_ref, ka_ref, bd_ref):
    rwid = w0_ref.shape[1]
    xs = p + mu_ref[...] * (prev - p)
    r = xs[:, :rwid]
    k = xs[:, rwid:2 * rwid]
    v = xs[:, 2 * rwid:3 * rwid]
    xwa = xs[:, 3 * rwid:3 * rwid + LANES]
    xg = xs[:, 3 * rwid + LANES:]
    z = w0_ref[...] + _mm(jnp.tanh(xwa), w2_ref[...])
    softplus_neg = jnp.maximum(-z, 0.0) + jnp.log(1.0 + jnp.exp(-jnp.abs(z)))
    w = -softplus_neg - 0.5
    a = _sigmoid(a0_ref[...] + _mm(xwa, a2_ref[...]))
    g = _mm(_sigmoid(xg), g2_ref[...])
    kk = k * kk_ref[...]
    norm = jnp.sqrt(_dot_x(kk * kk, bd_ref[...]))
    kk = kk / jnp.maximum(norm, 1e-12)
    return r, -jnp.exp(w), k * (1.0 + (a - 1.0) * ka_ref[...]), v, -kk, kk * a, g


def _shifted(p, before):
    rowi = lax.broadcasted_iota(I32, (p.shape[0], 1), 0)
    return jnp.where(rowi == 0, before, pltpu.roll(p, 1, 0))


def _rwprep_kernel(p_ref, prev_ref, first_ref, *refs, whole_prev):
    params, outs = refs[:9], refs[9:]
    p = p_ref[...]
    if whole_prev:
        prev = first_ref[...]
    else:
        prev = _shifted(p, jnp.where(pl.program_id(1) == 0, first_ref[...], prev_ref[7:8, :]))
    for ref, val in zip(outs, _rw_features(p, prev, *params)):
        ref[...] = val


def _rwprep(rw, first, mu, w0, w2p, a0, a2p, g2, k_k, k_a, bd_ones, tm, whole_prev):
    nb, t, rwc = rw.shape
    rwid = w0.shape[1]
    grid = (nb, t // tm)
    row = lambda b, i: (b, i, 0)
    const = lambda b, i: (0, 0)
    if whole_prev:
        prev_spec = pl.BlockSpec((None, tm, rwc), row)
        first_spec = pl.BlockSpec((None, tm, rwc), row)
    else:
        prev_spec = pl.BlockSpec((None, 8, rwc), lambda b, i: (b, jnp.maximum(i * (tm // 8) - 1, 0), 0))
        first_spec = pl.BlockSpec((None, 1, rwc), lambda b, i: (b, 0, 0))
    vec = pl.BlockSpec((1, rwid), const)
    out = pl.BlockSpec((None, tm, rwid), row)
    return pl.pallas_call(
        functools.partial(_rwprep_kernel, whole_prev=whole_prev),
        grid=grid,
        in_specs=[pl.BlockSpec((None, tm, rwc), row), prev_spec, first_spec,
                  pl.BlockSpec((1, rwc), const), vec, pl.BlockSpec((LANES, rwid), const),
                  vec, pl.BlockSpec((LANES, rwid), const), pl.BlockSpec((LANES, rwid), const), vec, vec,
                  pl.BlockSpec((rwid, rwid), const)],
        out_specs=[out] * 7,
        out_shape=[jax.ShapeDtypeStruct((nb, t, rwid), F32)] * 7,
        compiler_params=_cparams("parallel", "parallel"),
        name="rwprep",
    )(rw, rw, first, mu, w0, w2p, a0, a2p, g2, k_k, k_a, bd_ones)


def _rwchunk_kernel(r_ref, ld_ref, k_ref, v_ref, al_ref, be_ref, y_ref, st_ref, z_ref, *, nb, npb):
    c = RW_CHUNK
    ci = pl.program_id(1)

    @pl.when(ci == 0)
    def _():
        z_ref[...] = jnp.zeros_like(z_ref)

    ti = lax.broadcasted_iota(I32, (c, c), 0)
    si = lax.broadcasted_iota(I32, (c, c), 1)
    low_incl = si <= ti
    low_strict = si < ti
    diag = si == ti
    tri = jnp.where(low_incl, 1.0, 0.0).astype(BF16)
    eye = jnp.where(diag, 1.0, 0.0)
    head0 = lax.broadcasted_iota(I32, (1, LANES), 1) < HEAD_DIM
    hsels = (head0, jnp.logical_not(head0))
    same_head = (ti < HEAD_DIM) == (si < HEAD_DIM)
    nt = (((1,), (1,)), ((), ()))
    batches = range(nb * npb)
    chains = [(b, h) for b in batches for h in range(2)]
    lanes_of = lambda b: pl.ds((b // nb) * LANES, LANES)
    r_v, ld_v, k_v, v_v, al_v, be_v = ([ref[b % nb, :, lanes_of(b)] for b in batches]
                                       for ref in (r_ref, ld_ref, k_ref, v_ref, al_ref, be_ref))

    cum = [_xdot(tri, ld_v[b]) for b in batches]
    tot = [cm[c - 1:c, :] for cm in cum]
    e_neg = [jnp.exp(-cm) for cm in cum]
    at = [al_v[b] * jnp.exp(cum[b] - ld_v[b]) for b in batches]
    rt = [r_v[b] * jnp.exp(cum[b]) for b in batches]
    rhs_t = [jnp.concatenate([be_v[b] * e_neg[b], k_v[b] * e_neg[b]], axis=0).astype(BF16) for b in batches]
    vb = [v_v[b].astype(BF16) for b in batches]
    at_h = [jnp.where(hsels[h], at[b], 0.0) for b, h in chains]
    rt_h = [jnp.where(hsels[h], rt[b], 0.0) for b, h in chains]
    a4 = [lax.dot_general(jnp.concatenate([a, r], axis=0).astype(BF16), rhs_t[b], nt, preferred_element_type=F32)
          for (b, h), a, r in zip(chains, at_h, rt_h)]
    a_ab = [jnp.where(low_strict, m[:c, :c], 0.0) for m in a4]
    a_ak = [jnp.where(low_strict, m[:c, c:], 0.0).astype(BF16) for m in a4]
    a_r = [jnp.concatenate([jnp.where(low_incl, m[c:, :c], 0.0), jnp.where(low_incl, m[c:, c:], 0.0)],
                           axis=1).astype(BF16) for m in a4]
    levels = int(np.log2(c))
    pw = [m.astype(BF16) for m in a_ab]
    pw = [jnp.dot(m, m, preferred_element_type=F32).astype(BF16) for m in pw]
    inv = [eye + m for m in a_ab]
    for _ in range(1, levels - 1):
        x2 = [jnp.dot(m, jnp.concatenate([m, i.astype(BF16)], axis=1), preferred_element_type=F32)
              for m, i in zip(pw, inv)]
        pw = [m[:, :c].astype(BF16) for m in x2]
        inv = [i + m[:, c:] for i, m in zip(inv, x2)]
    inv = [i + jnp.dot(m, i.astype(BF16), preferred_element_type=F32) for i, m in zip(inv, pw)]
    akv = [jnp.dot(m, vb[b], preferred_element_type=F32) for (b, h), m in zip(chains, a_ak)]
    x = [_dot(i, jnp.concatenate([kv, a], axis=1)) for i, kv, a in zip(inv, akv, at_h)]
    u0_h = [m[:, :LANES] for m in x]
    at2_h = [m[:, LANES:] for m in x]
    y0_h = [jnp.dot(ar, jnp.concatenate([u0, v_v[b]], axis=0).astype(BF16), preferred_element_type=F32)
            for (b, h), ar, u0 in zip(chains, a_r, u0_h)]
    rt2_h = [r + jnp.dot(ar[:, :c], a2.astype(BF16), preferred_element_type=F32)
             for r, ar, a2 in zip(rt_h, a_r, at2_h)]
    z = [z_ref[b] for b in batches]
    uy = [_dot(jnp.concatenate([at2_h[2 * b] + at2_h[2 * b + 1], rt2_h[2 * b] + rt2_h[2 * b + 1]], axis=0), z[b])
          for b in batches]
    u = [uy[b][:c] + jnp.where(head0, u0_h[2 * b], u0_h[2 * b + 1]) for b in batches]
    for b in batches:
        y_ref[b % nb, :, lanes_of(b)] = uy[b][c:] + jnp.where(head0, y0_h[2 * b], y0_h[2 * b + 1])
    e_end = [jnp.exp(tot[b] - cum[b]) for b in batches]
    lhs_t = [jnp.concatenate([be_v[b] * e_end[b], k_v[b] * e_end[b]], axis=0) for b in batches]
    zadd = [_dot(lhs_t[b].T, jnp.concatenate([u[b], v_v[b]], axis=0)) for b in batches]
    for b in batches:
        dcol = jnp.sum(jnp.where(diag, jnp.broadcast_to(jnp.exp(tot[b]), (c, c)), 0.0), axis=1, keepdims=True)
        z_ref[b] = dcol * z[b] + jnp.where(same_head, zadd[b], 0.0)

    @pl.when(ci == pl.num_programs(1) - 1)
    def _():
        for b in batches:
            s = z_ref[b].T
            st_ref[b % nb, 2 * (b // nb)] = s[:HEAD_DIM, :HEAD_DIM]
            st_ref[b % nb, 2 * (b // nb) + 1] = s[HEAD_DIM:, HEAD_DIM:]


def _rwchunk(r, ld, k, v, al, be):
    nb, t, rwid = r.shape
    c = RW_CHUNK
    npb = RW_PAIRS
    assert t % c == 0 and c == LANES and rwid % (npb * LANES) == 0
    ngrp = rwid // (npb * LANES)
    seq = pl.BlockSpec((nb, c, npb * LANES), lambda hp, ci: (0, ci, hp))
    return pl.pallas_call(
        functools.partial(_rwchunk_kernel, nb=nb, npb=npb),
        grid=(ngrp, t // c),
        in_specs=[seq] * 6,
        out_specs=[seq, pl.BlockSpec((nb, 2 * npb, HEAD_DIM, HEAD_DIM), lambda hp, ci: (0, hp, 0, 0))],
        out_shape=[jax.ShapeDtypeStruct((nb, t, rwid), F32),
                   jax.ShapeDtypeStruct((nb, 2 * npb * ngrp, HEAD_DIM, HEAD_DIM), F32)],
        scratch_shapes=[pltpu.VMEM((nb * npb, LANES, LANES), F32)],
        compiler_params=_cparams("parallel", "arbitrary"),
        name="rwchunk",
    )(r, ld, k, v, al, be)


def _col(row, width):
    return jnp.broadcast_to(row, (LANES, width)).T[:, 0:1]


def _row(col, width):
    return jnp.broadcast_to(col, (width, LANES)).T[0:1, :]


def _rwstep_kernel(r_ref, ld_ref, k_ref, v_ref, al_ref, be_ref, s_ref, y_ref, so_ref, *, bb, nh):
    rwid = nh * HEAD_DIM

    def one(b, carry):
        r, dcy, k, al, be = (ref[b] for ref in (r_ref, ld_ref, k_ref, al_ref, be_ref))
        dcy = jnp.exp(dcy)
        v_col = _col(v_ref[b], rwid)
        heads = range(nh)
        hs = [slice(h * HEAD_DIM, (h + 1) * HEAD_DIM) for h in heads]
        st = [s_ref[b, h] for h in heads]
        sa = [jnp.sum(st[h] * al[:, hs[h]], axis=1, keepdims=True) for h in heads]
        st = [st[h] * dcy[:, hs[h]] + sa[h] * be[:, hs[h]] + v_col[hs[h], :] * k[:, hs[h]] for h in heads]
        for h in heads:
            so_ref[b, h] = st[h]
        ys = [jnp.sum(st[h] * r[:, hs[h]], axis=1, keepdims=True) for h in heads]
        y_ref[b] = _row(jnp.concatenate(ys, axis=0), rwid)
        return carry

    lax.fori_loop(0, bb, one, 0, unroll=2)


def _rwstep(r, ld, k, v, al, be, state):
    nb, _, rwid = r.shape
    nh = state.shape[1]
    bb = 8
    row = pl.BlockSpec((bb, 1, rwid), lambda i: (i, 0, 0))
    st = pl.BlockSpec((bb, nh, HEAD_DIM, HEAD_DIM), lambda i: (i, 0, 0, 0))
    return pl.pallas_call(
        functools.partial(_rwstep_kernel, bb=bb, nh=nh),
        grid=(nb // bb,),
        in_specs=[row] * 6 + [st],
        out_specs=[row, st],
        out_shape=[jax.ShapeDtypeStruct((nb, 1, rwid), F32), jax.ShapeDtypeStruct(state.shape, F32)],
        compiler_params=_cparams("parallel"),
        name="rwstep",
    )(r, ld, k, v, al, be, state)


def _outproj_kernel(x_ref, at_ref, y_ref, r_ref, k_ref, v_ref, g_ref, gt_ref, sh_ref, sc_ref, g2_ref,
                    wt_ref, wb_ref, rk_ref, lw_ref, lb_ref, bdm_ref, bd1_ref, rwt_ref, rb_ref, *rest, n_slots):
    x1_ref, pos_ref, wt4_ref, cnt_ref, xs_ref, ws_ref, hb_ref = rest[-7:]
    y = y_ref[...]
    bdm = bdm_ref[...]
    mean = _dot_x(y, bdm)
    yc = y - mean
    var = _dot_x(yc * yc, bdm)
    yn = yc * lax.rsqrt(var + GN_EPS) * lw_ref[...] + lb_ref[...]
    v = v_ref[...]
    bonus = _dot_x(r_ref[...] * k_ref[...] * rk_ref[...], bd1_ref[...]) * v
    rw = (yn + bonus) * g_ref[...]
    mix = _mm(at_ref[...], wt_ref[...]) + _mm(rw, wb_ref[...])
    x1 = x_ref[...] + gt_ref[...] * mix
    x1_ref[...] = x1
    ms = jnp.mean(x1 * x1, axis=-1, keepdims=True)
    h2 = x1 * lax.rsqrt(ms + NORM_EPS) * g2_ref[...] * (1.0 + sc_ref[...]) + sh_ref[...]
    pos, wts, cnt = _route_tile(_dot3_nt(rwt_ref[...], h2) + rb_ref[...])
    for kk in range(TOP_K):
        pos_ref[pl.ds(kk, 1), :] = pos[kk]
        wt4_ref[pl.ds(kk, 1), :] = wts[kk]
    cnt_ref[...] = jnp.broadcast_to(cnt, cnt_ref.shape)
    hb_ref[...] = h2.astype(BF16)
    used = jnp.sum(jnp.ceil(cnt / MOE_CHUNK) * MOE_CHUNK).astype(I32)
    _sort_tile(pos, wts, hb_ref, xs_ref, ws_ref, (used + SLOT_ROWS - 1) // SLOT_ROWS)


def _outproj(x3, attn, y, r, k, v, g, mod3, g2, wt_bf, wb_bf, r_k, ln_w, ln_b, bd_mean, bd_ones, rwt, rb, tm,
             n_slots, n_rows, tile0=0, bufs=None):
    nb, t, d = x3.shape
    rmod = mod3.shape[1]
    aw = attn.shape[2]
    ne = rwt.shape[0]
    nt = t // tm
    grid = (nb, nt)
    row = lambda b, i: (b, i, 0)
    const = lambda b, i: (0, 0)
    tile = lambda b, i: (tile0 + b * nt + i, 0)
    mod_spec = lambda s: pl.BlockSpec((None, rmod, d), (lambda b, i: (b, 0, s)) if rmod == 1 else (lambda b, i: (b, i, s)))
    half = pl.BlockSpec((None, tm, aw), row)
    vec = pl.BlockSpec((1, aw), const)
    tok4 = pl.BlockSpec((TOP_K, tm), lambda b, i: (0, b * nt + i))
    bufs = () if bufs is None else tuple(bufs)
    n_in = 20
    return pl.pallas_call(
        functools.partial(_outproj_kernel, n_slots=n_slots),
        grid=grid,
        in_specs=[pl.BlockSpec((None, tm, d), row)] + [half] * 6 + [mod_spec(2), mod_spec(3), mod_spec(4),
                  pl.BlockSpec((1, d), const), pl.BlockSpec((aw, d), const), pl.BlockSpec((aw, d), const),
                  vec, vec, vec, pl.BlockSpec((aw, aw), const), pl.BlockSpec((aw, aw), const),
                  pl.BlockSpec((ne, d), const), pl.BlockSpec((ne, 1), const)]
                 + [pl.BlockSpec(memory_space=pl.ANY)] * len(bufs),
        out_specs=[pl.BlockSpec((None, tm, d), row), tok4, tok4,
                   pl.BlockSpec((None, ne, LANES), lambda b, i: (b * nt + i, 0, 0)),
                   pl.BlockSpec((n_slots, d), tile), pl.BlockSpec((n_slots, LANES), tile)],
        out_shape=[jax.ShapeDtypeStruct((nb, t, d), F32), jax.ShapeDtypeStruct((TOP_K, nb * t), I32),
                   jax.ShapeDtypeStruct((TOP_K, nb * t), F32), jax.ShapeDtypeStruct((nb * nt, ne, LANES), F32),
                   jax.ShapeDtypeStruct((n_rows, d), BF16), jax.ShapeDtypeStruct((n_rows, LANES), F32)],
        scratch_shapes=[pltpu.VMEM((tm, d), BF16)],
        input_output_aliases={n_in: 4, n_in + 1: 5} if bufs else {},
        compiler_params=_cparams("parallel", "parallel"),
        name="outproj",
    )(x3, attn, y, r, k, v, g, mod3, mod3, mod3, g2, wt_bf, wb_bf, r_k, ln_w, ln_b, bd_mean, bd_ones, rwt, rb, *bufs)


def _route_tile(lg):
    ne, tn = lg.shape
    eidx = lax.broadcasted_iota(I32, (ne, tn), 0).astype(F32)
    vals, hots = [], []
    for _ in range(TOP_K):
        mx = jnp.max(lg, axis=0, keepdims=True)
        pick = jnp.min(jnp.where(lg == mx, eidx, float(ne)), axis=0, keepdims=True)
        hot = eidx == pick
        vals.append(mx)
        hots.append(hot)
        lg = jnp.where(hot, -jnp.inf, lg)
    ex = [jnp.exp(vv - vals[0]) for vv in vals]
    den = ex[0] + ex[1] + ex[2] + ex[3]
    wts = [e / den for e in ex]
    hot_all = jnp.zeros((ne, tn), F32)
    for hot in hots:
        hot_all = hot_all + jnp.where(hot, 1.0, 0.0)
    ri = lax.broadcasted_iota(I32, (tn, tn), 0)
    cj = lax.broadcasted_iota(I32, (tn, tn), 1)
    upper = jnp.where(ri <= cj, 1.0, 0.0).astype(BF16)
    before = jnp.dot(hot_all.astype(BF16), upper, preferred_element_type=F32) - hot_all
    cnt = jnp.sum(hot_all, axis=1, keepdims=True)
    padded = jnp.ceil(cnt / MOE_CHUNK) * MOE_CHUNK
    er = lax.broadcasted_iota(I32, (ne, ne), 0)
    ec = lax.broadcasted_iota(I32, (ne, ne), 1)
    lower_strict = jnp.where(ec < er, 1.0, 0.0)
    off = _xdot(lower_strict, jnp.broadcast_to(padded, (ne, LANES)))[:, 0:1]
    pos = [jnp.sum(jnp.where(hot, off + before, 0.0), axis=0, keepdims=True).astype(I32) for hot in hots]
    return pos, wts, cnt


def _slot_capacity(ne):
    return -(-(TOP_K * ROUTE_TILE + ne * (MOE_CHUNK - 1)) // SLOT_ROWS) * SLOT_ROWS


def _sort_tile(pos, wts, h_ref, xs_ref, ws_ref, n_steps):
    tn = h_ref.shape[0]

    def rows(c, carry):
        h = h_ref[...]
        r0 = pl.multiple_of(c * SLOT_ROWS, SLOT_ROWS)
        slot = r0 + lax.broadcasted_iota(I32, (SLOT_ROWS, tn), 0)
        pw = jnp.zeros((SLOT_ROWS, tn), F32)
        for kk in range(TOP_K):
            pw = pw + jnp.where(slot == pos[kk], wts[kk], 0.0)
        p = jnp.where(pw > 0.0, 1.0, 0.0)
        xs_ref[pl.ds(r0, SLOT_ROWS), :] = jnp.dot(p.astype(BF16), h, preferred_element_type=F32).astype(BF16)
        ws_ref[pl.ds(r0, SLOT_ROWS), :] = jnp.broadcast_to(jnp.sum(pw, axis=1, keepdims=True), (SLOT_ROWS, LANES))
        return carry

    lax.fori_loop(0, n_steps, rows, 0)


def _chunk_tables(cnt, n_slots, n_blocks):
    nt, ne = cnt.shape
    per_blk = MOE_BLOCK // MOE_CHUNK
    nch = -(-cnt // MOE_CHUNK)
    seg0 = (jnp.cumsum(nch, axis=1) - nch) + (jnp.arange(nt) * (n_slots // MOE_CHUNK))[:, None]
    cum_t = jnp.cumsum(nch, axis=0)
    total = cum_t[-1]
    blocks = -(-total // per_blk)
    blk_end = jnp.cumsum(blocks)
    blk = jnp.arange(n_blocks)
    blk_e = jnp.minimum(jnp.sum(blk_end[None, :] <= blk[:, None], axis=1), ne - 1).astype(I32)
    nact = blk_end[-1].reshape(1).astype(I32)
    pick_e = blk_e[:, None] == jnp.arange(ne)[None, :]
    of_e = lambda tab: jnp.sum(jnp.where(pick_e, tab[None, :], 0), axis=1)
    col_e = lambda tab: jnp.sum(jnp.where(pick_e[:, None, :], tab[None, :, :], 0), axis=2)
    q = ((blk - of_e(blk_end - blocks)) * per_blk)[:, None] + jnp.arange(per_blk)[None, :]
    live = (q < of_e(total)[:, None]) & (blk < nact[0])[:, None]
    t_n = jnp.minimum(jnp.sum(col_e(cum_t)[:, None, :] <= q[:, :, None], axis=2), nt - 1)
    pick_t = t_n[:, :, None] == jnp.arange(nt)[None, None, :]
    of_t = lambda tab: jnp.sum(jnp.where(pick_t, tab[:, None, :], 0), axis=2)
    src = jnp.where(live, of_t(col_e(seg0)) + q - of_t(col_e(cum_t - nch)), 0)
    n = blk[:, None] * per_blk + jnp.arange(per_blk)[None, :]
    spare = nt * (n_slots // MOE_CHUNK) + n % (N_OBUF * per_blk)
    dst = jnp.where(live, src, spare)
    half = (jnp.sum(live, axis=1) <= per_blk // 2).astype(I32)
    return blk_e, nact, src.reshape(-1).astype(I32), dst.reshape(-1).astype(I32), half


N_OBUF = 2


def _moe_kernel(be_ref, nact_ref, src_ref, dst_ref, half_ref, xs_ref, ws_ref, wgu_ref, bgu_ref, wd_ref, bd_ref,
                out_ref, xbuf, wbuf, obuf, gsem, ssem, wgu_bf, wd_bf, *, dff):
    i = pl.program_id(0)
    nact = nact_ref[0]
    per_blk = MOE_BLOCK // MOE_CHUNK
    rows = lambda c: pl.ds(pl.multiple_of(c * MOE_CHUNK, MOE_CHUNK), MOE_CHUNK)

    def gathers(blk, slot):
        copies = []
        for m in range(per_blk):
            c = src_ref[blk * per_blk + m]
            copies.append(pltpu.make_async_copy(xs_ref.at[rows(c), :], xbuf.at[slot, rows(m), :], gsem.at[slot]))
            copies.append(pltpu.make_async_copy(ws_ref.at[rows(c), :], wbuf.at[slot, rows(m), :], gsem.at[slot]))
        return copies

    def scatters(blk, slot):
        return [pltpu.make_async_copy(obuf.at[slot, rows(m), :], out_ref.at[rows(dst_ref[blk * per_blk + m]), :],
                                      ssem.at[slot]) for m in range(per_blk)]

    @pl.when(i == 0)
    def _():
        obuf[...] = jnp.zeros_like(obuf)
        for cp in gathers(0, 0):
            cp.start()

    @pl.when(i < nact)
    def _():
        slot = i % 2
        for cp in gathers(i, slot):
            cp.wait()

        @pl.when(i + 1 < nact)
        def _():
            for cp in gathers(i + 1, 1 - slot):
                cp.start()

        @pl.when((i == 0) | (be_ref[i] != be_ref[jnp.maximum(i - 1, 0)]))
        def _():
            wgu_bf[...] = wgu_ref[...].astype(BF16)
            wd_bf[...] = wd_ref[...].astype(BF16)

        def mlp(n):
            gu = jnp.dot(xbuf[slot, :n], wgu_bf[...], preferred_element_type=F32) + bgu_ref[...]
            gate = jnp.minimum(gu[:, :dff], SWIGLU_LIMIT)
            up = jnp.clip(gu[:, dff:], -SWIGLU_LIMIT, SWIGLU_LIMIT)
            act = (up + 1.0) * gate * _sigmoid(gate * SWIGLU_ALPHA)
            res = jnp.dot(act.astype(BF16), wd_bf[...], preferred_element_type=F32) + bd_ref[...]
            obuf[slot, :n] = (res * wbuf[slot, :n, 0:1]).astype(BF16)

        @pl.when(half_ref[i] == 1)
        def _():
            mlp(MOE_BLOCK // 2)

        @pl.when(half_ref[i] == 0)
        def _():
            mlp(MOE_BLOCK)

        @pl.when(i >= 1)
        def _():
            for cp in scatters(i - 1, 1 - slot):
                cp.wait()

        for cp in scatters(i, slot):
            cp.start()

        @pl.when(i == nact - 1)
        def _():
            for cp in scatters(i, slot):
                cp.wait()


def _moe(blk_e, nact, src, dst, half, xs, ws, w_gu, b_gu, w_down, b_down, n_blocks):
    ne, d, dff2 = w_gu.shape
    dff = dff2 // 2
    blk = MOE_BLOCK
    grid_spec = pltpu.PrefetchScalarGridSpec(
        num_scalar_prefetch=5,
        grid=(n_blocks,),
        in_specs=[pl.BlockSpec(memory_space=pl.ANY), pl.BlockSpec(memory_space=pl.ANY),
                  pl.BlockSpec((None, d, dff2), lambda i, be, *_: (be[i], 0, 0)),
                  pl.BlockSpec((None, 1, dff2), lambda i, be, *_: (be[i], 0, 0)),
                  pl.BlockSpec((None, dff, d), lambda i, be, *_: (be[i], 0, 0)),
                  pl.BlockSpec((None, 1, d), lambda i, be, *_: (be[i], 0, 0))],
        out_specs=pl.BlockSpec(memory_space=pl.ANY),
        scratch_shapes=[pltpu.VMEM((2, blk, d), BF16), pltpu.VMEM((2, blk, LANES), F32),
                        pltpu.VMEM((N_OBUF, blk, d), BF16),
                        pltpu.SemaphoreType.DMA((2,)), pltpu.SemaphoreType.DMA((N_OBUF,)),
                        pltpu.VMEM((d, dff2), BF16), pltpu.VMEM((dff, d), BF16)],
    )
    return pl.pallas_call(
        functools.partial(_moe_kernel, dff=dff),
        grid_spec=grid_spec,
        out_shape=jax.ShapeDtypeStruct(xs.shape, BF16),
        compiler_params=_cparams("arbitrary"),
        name="moe",
    )(blk_e, nact, src, dst, half, xs, ws, w_gu, b_gu.reshape(ne, 1, dff2), w_down, b_down.reshape(ne, 1, d))


def _combine_kernel(pos_ref, used_ref, ys_ref, x1_ref, gt_ref, o_ref, p_ref, *, n_slots):
    tn = x1_ref.shape[0]
    filled = lax.broadcasted_iota(I32, (n_slots, 1), 0) < used_ref[0:1, 0:1]
    ys = jnp.where(filled, ys_ref[...], jnp.zeros((), BF16))
    lane = lax.broadcasted_iota(I32, (tn, SLOT_ROWS), 1)
    pos = [jnp.broadcast_to(pos_ref[:, kk:kk + 1], (tn, SLOT_ROWS)) - lane for kk in range(TOP_K)]

    def cols(c, carry):
        c0 = pl.multiple_of(c * SLOT_ROWS, SLOT_ROWS)
        p = jnp.zeros((tn, SLOT_ROWS), F32)
        for kk in range(TOP_K):
            p = p + jnp.where(pos[kk] == c0, 1.0, 0.0)
        p_ref[:, pl.ds(c0, SLOT_ROWS)] = p.astype(BF16)
        return carry

    lax.fori_loop(0, n_slots // SLOT_ROWS, cols, 0)
    o_ref[...] = x1_ref[...] + gt_ref[...] * jnp.dot(p_ref[...], ys, preferred_element_type=F32)


def _combine(pos_t, used, ys, x1, mod3, tile0, rows, n_slots):
    nb, t, d = x1.shape
    rmod = mod3.shape[1]
    nt = t // rows
    per_tile = ROUTE_TILE // rows
    row = lambda b, i: (b, i, 0)
    gate = pl.BlockSpec((None, rmod, d), (lambda b, i: (b, 0, 5)) if rmod == 1 else (lambda b, i: (b, i, 5)))
    return pl.pallas_call(
        functools.partial(_combine_kernel, n_slots=n_slots),
        grid=(nb, nt),
        in_specs=[pl.BlockSpec((rows, TOP_K), lambda b, i: (b * nt + i, 0)),
                  pl.BlockSpec((None, SUBLANES, LANES), lambda b, i: (tile0 + (b * nt + i) // per_tile, 0, 0)),
                  pl.BlockSpec((n_slots, d), lambda b, i: (tile0 + (b * nt + i) // per_tile, 0)),
                  pl.BlockSpec((None, rows, d), row), gate],
        out_specs=pl.BlockSpec((None, rows, d), row),
        out_shape=jax.ShapeDtypeStruct((nb, t, d), F32),
        scratch_shapes=[pltpu.VMEM((rows, n_slots), BF16)],
        compiler_params=_cparams("parallel", "parallel"),
        name="combine",
    )(pos_t, used, ys, x1, mod3)


def _rope_tables(pos, n_heads):
    half = HEAD_DIM // 2
    inv_freq = 1.0 / (ROPE_THETA ** (jnp.arange(0, HEAD_DIM, 2, dtype=F32) / HEAD_DIM))
    ang = pos.astype(F32)[:, None] * inv_freq[None, :]
    cos, sin = jnp.cos(ang), jnp.sin(ang)
    del half
    return (jnp.tile(jnp.concatenate([cos, cos], axis=-1), (1, n_heads)),
            jnp.tile(jnp.concatenate([-sin, sin], axis=-1), (1, n_heads)))


def _block_diag(width, value):
    h = np.arange(width) // HEAD_DIM
    return jnp.asarray(np.where(h[:, None] == h[None, :], value, 0.0), F32)


def kernel(x_prompt, x_sample, cache_k, cache_v, state_wkv, state_shift, c_prompt, c_sample, w_ada, b_ada, norm1_g, norm2_g, w_in, q_norm_g, k_norm_g, rwkv_mu, rwkv_w0, rwkv_w2, rwkv_a0, rwkv_a2, rwkv_g2, rwkv_k_k, rwkv_k_a, rwkv_r_k, rwkv_ln_w, rwkv_ln_b, w_out, router_w, router_b, moe_w_gu, moe_b_gu, moe_w_down, moe_b_down):
    nbp, t, d = x_prompt.shape
    nbs, ts, _ = x_sample.shape
    depth = w_ada.shape[0]
    assert depth == 1 and ts == 1
    n_heads = cache_k.shape[3]
    aw = n_heads * HEAD_DIM
    rwid = rwkv_w0.shape[1]
    rwc = rwkv_mu.shape[1]
    past = cache_k.shape[2]
    ne = router_w.shape[2]
    keep = min(MAX_WINDOW, t)
    lyr = 0

    w_in_bf = w_in[lyr].astype(BF16)
    wt_bf = w_out[lyr][:aw].astype(BF16)
    wb_bf = w_out[lyr][aw:].astype(BF16)
    g1 = norm1_g[lyr].reshape(1, d)
    g2 = norm2_g[lyr].reshape(1, d)
    qg = jnp.tile(q_norm_g[lyr], n_heads).reshape(1, aw)
    kg = jnp.tile(k_norm_g[lyr], n_heads).reshape(1, aw)
    bd_mean_a = _block_diag(aw, 1.0 / HEAD_DIM)
    bd_mean_r = _block_diag(rwid, 1.0 / HEAD_DIM)
    bd_ones_r = _block_diag(rwid, 1.0)
    dl = rwkv_w2.shape[1]
    w2p = jnp.zeros((LANES, rwid), F32).at[:dl].set(rwkv_w2[lyr])
    a2p = jnp.zeros((LANES, rwid), F32).at[dl:dl + rwkv_a2.shape[1]].set(rwkv_a2[lyr])
    vec = lambda a: a[lyr].reshape(1, -1)
    rwt = router_w[lyr].T
    rb = router_b[lyr].reshape(ne, 1)

    rows_c = nbp + nbs
    rows_pad = -(-rows_c // 8) * 8
    c_all = jnp.zeros((rows_pad, d), F32).at[:nbp].set(c_prompt).at[nbp:rows_c].set(c_sample)
    mod = _ada(c_all, w_ada[lyr], b_ada[lyr])
    mod_p = mod[:nbp].reshape(nbp, 1, 6 * d)
    mod_s = mod[nbp:rows_c].reshape(1, nbs, 6 * d)
    xs3 = x_sample.reshape(1, nbs, d)

    cos_p, sin_p = _rope_tables(jnp.arange(t), n_heads)
    cos_s, sin_s = _rope_tables(jnp.full((nbs,), PAST_LEN), n_heads)
    rw_args = (vec(rwkv_mu), vec(rwkv_w0), w2p, vec(rwkv_a0), a2p, rwkv_g2[lyr], vec(rwkv_k_k), vec(rwkv_k_a), bd_ones_r)
    rw_args_p = tuple(a.astype(BF16) if i in (2, 4, 5) else a for i, a in enumerate(rw_args))
    qp, kp, vp, *pre_p, tail_p, kt_p, vt_p = _inproj(x_prompt, mod_p, g1, w_in_bf, cos_p, sin_p, qg, kg, bd_mean_a,
                                                     ROW_TILE, keep=keep, rw=(jnp.zeros((nbp, 1, rwc), F32), *rw_args_p))
    qs, ks, vs, rws = _inproj(xs3, mod_s, g1, w_in[lyr], cos_s, sin_s, qg, kg, bd_mean_a, nbs)

    attn_p = _attn_prompt(qp, kp, vp)
    as_rows = lambda a: a.reshape(nbs, 1, -1)
    cache_t = lambda cch: jnp.transpose(cch[lyr], (0, 2, 3, 1)).reshape(nbs, aw, past)
    attn_s = _attn_sample(as_rows(qs), as_rows(ks), as_rows(vs), cache_t(cache_k), cache_t(cache_v))

    pre_s = _rwprep(rws, state_shift[lyr].reshape(1, nbs, rwc), *rw_args, tm=nbs, whole_prev=True)
    r_p, ld_p, k_p, v_p, al_p, be_p, g_p = pre_p
    y_p, wkv_p = _rwchunk(r_p, ld_p, k_p, v_p, al_p, be_p)
    r_s, ld_s, k_s, v_s, al_s, be_s, g_s = pre_s
    y_s, wkv_s = _rwstep(*(as_rows(a) for a in (r_s, ld_s, k_s, v_s, al_s, be_s)), state_wkv[lyr])
    y_s = y_s.reshape(1, nbs, rwid)

    n_p = nbp * t
    n_valid = n_p + nbs
    assert ROW_TILE == ROUTE_TILE and n_p % ROUTE_TILE == 0 and nbs <= ROUTE_TILE and d == SUBLANES * LANES
    n_tiles = n_p // ROUTE_TILE + 1
    n_slots = _slot_capacity(ne)
    n_rows = n_tiles * n_slots + N_OBUF * MOE_BLOCK
    op_args = (rwkv_r_k[lyr].reshape(1, rwid), vec(rwkv_ln_w), vec(rwkv_ln_b), bd_mean_r, bd_ones_r, rwt, rb)
    x1_p, pos_p, _, cnt_p, *bufs = _outproj(x_prompt, attn_p, y_p, r_p, k_p, v_p, g_p, mod_p, g2, wt_bf, wb_bf,
                                            *op_args, tm=ROW_TILE, n_slots=n_slots, n_rows=n_rows)
    x1_s, pos_s, _, cnt_s, xs, ws = _outproj(xs3, attn_s.reshape(1, nbs, aw), y_s, r_s, k_s, v_s, g_s, mod_s, g2,
                                             w_out[lyr][:aw], w_out[lyr][aw:], *op_args, tm=nbs, n_slots=n_slots,
                                             n_rows=n_rows, tile0=n_p // ROUTE_TILE, bufs=bufs)

    cnt = jnp.concatenate([cnt_p, cnt_s])[:, :, 0].astype(I32)
    per_blk = MOE_BLOCK // MOE_CHUNK
    n_blocks = -(-(n_valid * TOP_K // MOE_CHUNK + n_tiles * ne) // per_blk) + ne
    blk_e, nact, src, dst, half = _chunk_tables(cnt, n_slots, n_blocks)
    ys = _moe(blk_e, nact, src, dst, half, xs, ws, moe_w_gu[lyr], moe_b_gu[lyr], moe_w_down[lyr], moe_b_down[lyr], n_blocks)
    used = jnp.sum(-(-cnt // MOE_CHUNK) * MOE_CHUNK, axis=1)
    used = jnp.broadcast_to(used[:, None, None], (n_tiles, SUBLANES, LANES))
    y_prompt = _combine(pos_p.T, used, ys, x1_p, mod_p, 0, ROUTE_TILE, n_slots)
    y_sample = _combine(pos_s.T, used, ys, x1_s, mod_s, n_p // ROUTE_TILE, nbs, n_slots)

    kept = lambda a: jnp.transpose(a.reshape(nbp, n_heads, HEAD_DIM, keep), (0, 3, 1, 2))[None]
    return (y_prompt, y_sample.reshape(nbs, ts, d), kept(kt_p), kept(vt_p), wkv_p[None], tail_p[:, SUBLANES - 1][None],
            ks.reshape(nbs, ts, n_heads, HEAD_DIM)[None], vs.reshape(nbs, ts, n_heads, HEAD_DIM)[None],
            wkv_s[None], rws.reshape(nbs, rwc)[None])
```

```python
import functools

import numpy as np
import jax
import jax.numpy as jnp
from jax import lax
from jax.experimental import pallas as pl
from jax.experimental.pallas import tpu as pltpu

F32 = jnp.float32
BF16 = jnp.bfloat16
I32 = jnp.int32

HEAD_DIM = 64
LANES = 128
SUBLANES = 8
DILATED_PATTERNS = ((128, 1), (512, 4), (2048, 16))
WINDOW_STEPS = 128
MAX_WINDOW = 2048
PAST_LEN = 16384
ROPE_THETA = 10000.0
NORM_EPS = 1e-6
GN_EPS = 64e-5
TOP_K = 4
SWIGLU_ALPHA = 1.702
SWIGLU_LIMIT = 7.0
RW_CHUNK = 128
RW_PAIRS = 2
MOE_BLOCK = 512
MOE_CHUNK = 16
SLOT_ROWS = 256
ATTN_UNITS = 8
ROW_TILE = 512
ROUTE_TILE = 512
VMEM_LIMIT = 56 * 1024 * 1024
NEG_BIG = -1e30


def _cparams(*sem):
    return pltpu.CompilerParams(dimension_semantics=sem, vmem_limit_bytes=VMEM_LIMIT)


def _dot(a, b):
    return jnp.dot(a.astype(BF16), b.astype(BF16), preferred_element_type=F32)


def _split2(a):
    hi = a.astype(BF16)
    lo = (a - hi.astype(F32)).astype(BF16)
    return hi, lo


def _split3(a):
    hi = a.astype(BF16)
    r1 = a - hi.astype(F32)
    mid = r1.astype(BF16)
    lo = (r1 - mid.astype(F32)).astype(BF16)
    return hi, mid, lo


def _dot_x(a, e):
    e = e.astype(BF16)
    hi, lo = _split2(a)
    return jnp.dot(hi, e, preferred_element_type=F32) + jnp.dot(lo, e, preferred_element_type=F32)


def _xdot(e, a):
    e = e.astype(BF16)
    hi, mid, lo = _split3(a)
    return (jnp.dot(e, hi, preferred_element_type=F32) + jnp.dot(e, mid, preferred_element_type=F32)
            + jnp.dot(e, lo, preferred_element_type=F32))


def _dot3(a, b):
    ah, al = _split2(a)
    bh, bl = _split2(b)
    return (jnp.dot(ah, bh, preferred_element_type=F32) + jnp.dot(ah, bl, preferred_element_type=F32)
            + jnp.dot(al, bh, preferred_element_type=F32))


def _dot3_nt(a, b):
    ah, al = _split2(a)
    bh, bl = _split2(b)
    dn = (((1,), (1,)), ((), ()))
    return (lax.dot_general(ah, bh, dn, preferred_element_type=F32)
            + lax.dot_general(ah, bl, dn, preferred_element_type=F32)
            + lax.dot_general(al, bh, dn, preferred_element_type=F32))


def _mm(a, w):
    if w.dtype == BF16:
        return jnp.dot(a.astype(BF16), w, preferred_element_type=F32)
    return _dot3(a, w)


def _sigmoid(x):
    return 1.0 / (1.0 + jnp.exp(-x))


def _ada_kernel(c_ref, w_ref, b_ref, o_ref):
    c = c_ref[...]
    o_ref[...] = _dot3(c * _sigmoid(c), w_ref[...]) + b_ref[...]


def _ada(c_all, w_ada, b_ada):
    rows, d = c_all.shape
    n = w_ada.shape[1]
    tn = n // 4
    return pl.pallas_call(
        _ada_kernel,
        grid=(n // tn,),
        in_specs=[pl.BlockSpec((rows, d), lambda j: (0, 0)),
                  pl.BlockSpec((d, tn), lambda j: (0, j)),
                  pl.BlockSpec((1, tn), lambda j: (0, j))],
        out_specs=pl.BlockSpec((rows, tn), lambda j: (0, j)),
        out_shape=jax.ShapeDtypeStruct((rows, n), F32),
        compiler_params=_cparams("arbitrary"),
        name="ada",
    )(c_all, w_ada, b_ada.reshape(1, n))


N_RW_PARAMS = 9
N_RW_FEATS = 7


def _inproj_kernel(x_ref, sh_ref, sc_ref, g_ref, w_ref, cos_ref, sin_ref, qg_ref, kg_ref, bd_ref, *rest,
                   aw, first_kept, n_kept, rw_fused):
    n_in = 1 + N_RW_PARAMS if rw_fused else 0
    rw_in, outs = rest[:n_in], rest[n_in:]
    q_ref, k_ref, v_ref = outs[:3]
    n_rw_out = N_RW_FEATS + 1 if rw_fused else 1
    rw_out = outs[3:3 + n_rw_out]
    maybe_kv_t = outs[3 + n_rw_out:3 + n_rw_out + n_kept]
    x = x_ref[...]
    ms = jnp.mean(x * x, axis=-1, keepdims=True)
    h = x * lax.rsqrt(ms + NORM_EPS) * g_ref[...] * (1.0 + sc_ref[...]) + sh_ref[...]
    proj = _mm(h, w_ref[...])
    cos = cos_ref[...]
    sin = sin_ref[...]
    lane = lax.broadcasted_iota(I32, (1, aw), 1)
    first_half = (lane % HEAD_DIM) < (HEAD_DIM // 2)
    bd = bd_ref[...]

    def norm_rope(t, g):
        tn = t * lax.rsqrt((_dot if w_ref.dtype == BF16 else _dot_x)(t * t, bd) + NORM_EPS) * g
        rot = jnp.where(first_half, pltpu.roll(tn, aw - HEAD_DIM // 2, 1), pltpu.roll(tn, HEAD_DIM // 2, 1))
        return tn * cos + rot * sin

    scale = 1.0 / np.sqrt(HEAD_DIM).astype(np.float32)
    q_ref[...] = norm_rope(proj[:, :aw], qg_ref[...]) * scale
    k = norm_rope(proj[:, aw:2 * aw], kg_ref[...])
    v = proj[:, 2 * aw:3 * aw]
    k_ref[...] = k
    v_ref[...] = v
    rw = proj[:, 3 * aw:]
    if rw_fused:
        carry_ref = outs[-1]
        tail = rw[rw.shape[0] - SUBLANES:, :]
        before = jnp.where(pl.program_id(1) == 0, rw_in[0][...], carry_ref[SUBLANES - 1:SUBLANES, :])
        for ref, val in zip(rw_out, _rw_features(rw, _shifted(rw, before), *rw_in[1:])):
            ref[...] = val
        carry_ref[...] = tail
        rw_out[-1][...] = tail
    else:
        rw_out[0][...] = rw
    if maybe_kv_t:
        kt_ref, vt_ref = maybe_kv_t
        kept = pl.program_id(1) >= first_kept

        @pl.when(kept)
        def _():
            kt_ref[...] = k.T
            vt_ref[...] = v.T

        @pl.when(jnp.logical_not(kept))
        def _():
            kt_ref[...] = jnp.zeros_like(kt_ref)
            vt_ref[...] = jnp.zeros_like(vt_ref)


def _inproj(x3, mod3, g1, w_in_bf, cos_t, sin_t, qg, kg, bd_mean, tm, keep=0, rw=None):
    nb, t, d = x3.shape
    r = mod3.shape[1]
    ncol = w_in_bf.shape[1]
    aw = cos_t.shape[1]
    rwc = ncol - 3 * aw
    grid = (nb, t // tm)
    row = lambda b, i: (b, i, 0)
    const = lambda b, i: (0, 0)
    once = dict(pipeline_mode=pl.Buffered(1)) if rw else {}
    mod_spec = lambda s: pl.BlockSpec((None, r, d), (lambda b, i: (b, 0, s)) if r == 1 else (lambda b, i: (b, i, s)))
    out_specs = [pl.BlockSpec((None, tm, aw), row)] * 3
    out_shape = [jax.ShapeDtypeStruct((nb, t, aw), F32)] * 3
    rw_in, rw_specs, scratch = (), [], []
    if rw:
        rw_in = tuple(rw)
        rwid = rw_in[2].shape[1]
        rw_specs = [pl.BlockSpec((None, 1, rwc), lambda b, i: (b, 0, 0))] + [
            pl.BlockSpec(a.shape, const, **once) for a in rw_in[1:]]
        out_specs += [pl.BlockSpec((None, tm, rwid), row)] * N_RW_FEATS + [
            pl.BlockSpec((None, SUBLANES, rwc), lambda b, i: (b, 0, 0))]
        out_shape += [jax.ShapeDtypeStruct((nb, t, rwid), F32)] * N_RW_FEATS + [
            jax.ShapeDtypeStruct((nb, SUBLANES, rwc), F32)]
        scratch = [pltpu.VMEM((SUBLANES, rwc), F32)]
    else:
        out_specs += [pl.BlockSpec((None, tm, rwc), row)]
        out_shape += [jax.ShapeDtypeStruct((nb, t, rwc), F32)]
    first_kept = (t - keep) // tm
    if keep:
        assert keep % tm == 0 and (t - keep) % tm == 0
        kept_spec = pl.BlockSpec((None, aw, tm), lambda b, i: (b, 0, jnp.maximum(i - first_kept, 0)))
        out_specs += [kept_spec, kept_spec]
        out_shape += [jax.ShapeDtypeStruct((nb, aw, keep), F32)] * 2
    outs = pl.pallas_call(
        functools.partial(_inproj_kernel, aw=aw, first_kept=first_kept, n_kept=2 if keep else 0, rw_fused=bool(rw)),
        grid=grid,
        in_specs=[pl.BlockSpec((None, tm, d), row), mod_spec(0), mod_spec(1),
                  pl.BlockSpec((1, d), const), pl.BlockSpec((d, ncol), const, **once),
                  pl.BlockSpec((tm, aw), lambda b, i: (i, 0)), pl.BlockSpec((tm, aw), lambda b, i: (i, 0)),
                  pl.BlockSpec((1, aw), const), pl.BlockSpec((1, aw), const),
                  pl.BlockSpec((aw, aw), const, **once)] + rw_specs,
        out_specs=out_specs,
        out_shape=out_shape,
        scratch_shapes=scratch,
        compiler_params=_cparams("parallel", "arbitrary"),
        name="inproj",
    )(x3, mod3, mod3, g1, w_in_bf, cos_t, sin_t, qg, kg, bd_mean, *rw_in)
    return outs


def _attn_prompt_kernel(q_ref, k_ref, v_ref, o_ref, kp_ref, vp_ref, m_ref, l_ref, acc_ref, *, t, pad):
    nq = WINDOW_STEPS
    nk = 2 * WINDOW_STEPS
    kp_ref[pl.ds(0, pad), :] = jnp.zeros((pad, LANES), F32)
    vp_ref[pl.ds(0, pad), :] = jnp.zeros((pad, LANES), F32)
    kp_ref[pl.ds(pad, t), :] = k_ref[...]
    vp_ref[pl.ds(pad, t), :] = v_ref[...]
    head0 = lax.broadcasted_iota(I32, (1, LANES), 1) < HEAD_DIM
    qi = lax.broadcasted_iota(I32, (nq, nk), 0)
    kj = lax.broadcasted_iota(I32, (nq, nk), 1)
    steps_back = qi + nq - kj
    band = (steps_back >= 0) & (steps_back <= WINDOW_STEPS)
    has_past = kj >= nq

    hsels = (head0, jnp.logical_not(head0))
    nt = (((1,), (1,)), ((), ()))

    for p, (_, d) in enumerate(DILATED_PATTERNS):
        def units(g, carry, p=p, d=d):
            rows_q, kb, vb, valid, qh = [], [], [], [], []
            for j in range(ATTN_UNITS):
                u = g * ATTN_UNITS + j
                res = u % d
                blk = u // d
                q_start = res + d * nq * blk
                k_start = pad + q_start - d * nq
                if d == 1:
                    rows_q.append(pl.ds(q_start, nq))
                    rows_k = pl.ds(k_start, nk)
                else:
                    rows_q.append(pl.ds(q_start, nq, stride=d))
                    rows_k = pl.ds(k_start, nk, stride=d)
                q = q_ref[rows_q[j], :]
                kb.append(kp_ref[rows_k, :].astype(BF16))
                vb.append(vp_ref[rows_k, :].astype(BF16))
                valid.append(band & (has_past | (blk > 0)))
                qh.append([jnp.where(hsel, q, 0.0).astype(BF16) for hsel in hsels])
            chains = [(j, h) for j in range(ATTN_UNITS) for h in range(2)]
            s = [lax.dot_general(qh[j][h], kb[j], nt, preferred_element_type=F32) for j, h in chains]
            s = [jnp.where(valid[j], sc, NEG_BIG) for (j, h), sc in zip(chains, s)]
            mx = [jnp.max(sc, axis=-1, keepdims=True) for sc in s]
            e = [jnp.exp(sc - m) for sc, m in zip(s, mx)]
            den = [jnp.sum(ec, axis=-1, keepdims=True) for ec in e]
            o = [jnp.dot(ec.astype(BF16), vb[j], preferred_element_type=F32) for (j, h), ec in zip(chains, e)]
            for j in range(ATTN_UNITS):
                m_ref[p, rows_q[j], :] = jnp.where(head0, mx[2 * j], mx[2 * j + 1])
                l_ref[p, rows_q[j], :] = jnp.where(head0, den[2 * j], den[2 * j + 1])
                acc_ref[p, rows_q[j], :] = jnp.where(head0, o[2 * j], o[2 * j + 1])
            return carry

        lax.fori_loop(0, t // nq // ATTN_UNITS, units, 0)

    rows = 256

    def merge(i, carry):
        sl = pl.ds(pl.multiple_of(i * rows, rows), rows)
        m0, m1, m2 = m_ref[0, sl, :], m_ref[1, sl, :], m_ref[2, sl, :]
        mm = jnp.maximum(jnp.maximum(m0, m1), m2)
        w0, w1, w2 = jnp.exp(m0 - mm), jnp.exp(m1 - mm), jnp.exp(m2 - mm)
        num = w0 * acc_ref[0, sl, :] + w1 * acc_ref[1, sl, :] + w2 * acc_ref[2, sl, :]
        den = w0 * l_ref[0, sl, :] + w1 * l_ref[1, sl, :] + w2 * l_ref[2, sl, :]
        o_ref[sl, :] = num / den
        return carry

    lax.fori_loop(0, t // rows, merge, 0)


def _attn_prompt(q, k, v):
    nb, t, aw = q.shape
    pad = MAX_WINDOW
    assert t % MAX_WINDOW == 0
    spec = pl.BlockSpec((None, t, LANES), lambda b, hp: (b, 0, hp))
    return pl.pallas_call(
        functools.partial(_attn_prompt_kernel, t=t, pad=pad),
        grid=(nb, aw // LANES),
        in_specs=[spec, spec, spec],
        out_specs=spec,
        out_shape=jax.ShapeDtypeStruct((nb, t, aw), F32),
        scratch_shapes=[pltpu.VMEM((pad + t, LANES), F32), pltpu.VMEM((pad + t, LANES), F32),
                        pltpu.VMEM((3, t, LANES), F32), pltpu.VMEM((3, t, LANES), F32),
                        pltpu.VMEM((3, t, LANES), F32)],
        compiler_params=_cparams("parallel", "parallel"),
        name="attn_prompt",
    )(q, k, v)


def _attn_sample_kernel(q_ref, kn_ref, vn_ref, kt_ref, vt_ref, o_ref, *, nh, w):
    aw = nh * HEAD_DIM
    dist = w - lax.broadcasted_iota(I32, (1, w), 1)
    mult = jnp.zeros((1, w), F32)
    for win, d in DILATED_PATTERNS:
        mult = mult + jnp.where((dist % d == 0) & (dist <= win), 1.0, 0.0)
    n_pat = float(len(DILATED_PATTERNS))
    q_col = _col(q_ref[...], aw)
    kn_col = _col(kn_ref[...], aw)
    vn_col = _col(vn_ref[...], aw)
    heads = range(nh)
    hs = [pl.ds(h * HEAD_DIM, HEAD_DIM) for h in heads]
    cut = lambda col, h: col[h * HEAD_DIM:(h + 1) * HEAD_DIM]
    s = [jnp.sum(kt_ref[hs[h], :] * cut(q_col, h), axis=0, keepdims=True) for h in heads]
    s_self = [jnp.sum(cut(q_col, h) * cut(kn_col, h), axis=0, keepdims=True) for h in heads]
    s = [jnp.where(mult > 0.0, sh, NEG_BIG) for sh in s]
    mx = [jnp.maximum(jnp.max(sh, axis=1, keepdims=True), ss) for sh, ss in zip(s, s_self)]
    pr = [mult * jnp.exp(sh - m) for sh, m in zip(s, mx)]
    p_self = [n_pat * jnp.exp(ss - m) for ss, m in zip(s_self, mx)]
    den = [jnp.sum(p, axis=1, keepdims=True) + ps for p, ps in zip(pr, p_self)]
    num = [jnp.sum(vt_ref[hs[h], :] * pr[h], axis=1, keepdims=True) + p_self[h] * cut(vn_col, h) for h in heads]
    outs = [n / dn for n, dn in zip(num, den)]
    o_ref[...] = _row(jnp.concatenate(outs, axis=0), aw)


def _attn_sample(q, kn, vn, cache_kt, cache_vt):
    nb, _, aw = q.shape
    w = cache_kt.shape[2]
    assert w == MAX_WINDOW
    row = pl.BlockSpec((None, 1, aw), lambda i: (i, 0, 0))
    mat = pl.BlockSpec((None, aw, w), lambda i: (i, 0, 0))
    return pl.pallas_call(
        functools.partial(_attn_sample_kernel, nh=aw // HEAD_DIM, w=w),
        grid=(nb,),
        in_specs=[row, row, row, mat, mat],
        out_specs=row,
        out_shape=jax.ShapeDtypeStruct((nb, 1, aw), F32),
        compiler_params=_cparams("parallel"),
        name="attn_sample",
    )(q, kn, vn, cache_kt, cache_vt)


def _rw_features(p, prev, mu_ref, w0_ref, w2_ref, a0_ref, a2_ref, g2_ref, kk_ref, ka_ref, bd_ref):
    rwid = w0_ref.shape[1]
    xs = p + mu_ref[...] * (prev - p)
    r = xs[:, :rwid]
    k = xs[:, rwid:2 * rwid]
    v = xs[:, 2 * rwid:3 * rwid]
    xwa = xs[:, 3 * rwid:3 * rwid + LANES]
    xg = xs[:, 3 * rwid + LANES:]
    z = w0_ref[...] + _mm(jnp.tanh(xwa), w2_ref[...])
    softplus_neg = jnp.maximum(-z, 0.0) + jnp.log(1.0 + jnp.exp(-jnp.abs(z)))
    w = -softplus_neg - 0.5
    a = _sigmoid(a0_ref[...] + _mm(xwa, a2_ref[...]))
    g = _mm(_sigmoid(xg), g2_ref[...])
    kk = k * kk_ref[...]
    norm = jnp.sqrt(_dot_x(kk * kk, bd_ref[...]))
    kk = kk / jnp.maximum(norm, 1e-12)
    return r, -jnp.exp(w), k * (1.0 + (a - 1.0) * ka_ref[...]), v, -kk, kk * a, g


def _shifted(p, before):
    rowi = lax.broadcasted_iota(I32, (p.shape[0], 1), 0)
    return jnp.where(rowi == 0, before, pltpu.roll(p, 1, 0))


def _rwprep_kernel(p_ref, prev_ref, first_ref, *refs, whole_prev):
    params, outs = refs[:9], refs[9:]
    p = p_ref[...]
    if whole_prev:
        prev = first_ref[...]
    else:
        prev = _shifted(p, jnp.where(pl.program_id(1) == 0, first_ref[...], prev_ref[7:8, :]))
    for ref, val in zip(outs, _rw_features(p, prev, *params)):
        ref[...] = val


def _rwprep(rw, first, mu, w0, w2p, a0, a2p, g2, k_k, k_a, bd_ones, tm, whole_prev):
    nb, t, rwc = rw.shape
    rwid = w0.shape[1]
    grid = (nb, t // tm)
    row = lambda b, i: (b, i, 0)
    const = lambda b, i: (0, 0)
    if whole_prev:
        prev_spec = pl.BlockSpec((None, tm, rwc), row)
        first_spec = pl.BlockSpec((None, tm, rwc), row)
    else:
        prev_spec = pl.BlockSpec((None, 8, rwc), lambda b, i: (b, jnp.maximum(i * (tm // 8) - 1, 0), 0))
        first_spec = pl.BlockSpec((None, 1, rwc), lambda b, i: (b, 0, 0))
    vec = pl.BlockSpec((1, rwid), const)
    out = pl.BlockSpec((None, tm, rwid), row)
    return pl.pallas_call(
        functools.partial(_rwprep_kernel, whole_prev=whole_prev),
        grid=grid,
        in_specs=[pl.BlockSpec((None, tm, rwc), row), prev_spec, first_spec,
                  pl.BlockSpec((1, rwc), const), vec, pl.BlockSpec((LANES, rwid), const),
                  vec, pl.BlockSpec((LANES, rwid), const), pl.BlockSpec((LANES, rwid), const), vec, vec,
                  pl.BlockSpec((rwid, rwid), const)],
        out_specs=[out] * 7,
        out_shape=[jax.ShapeDtypeStruct((nb, t, rwid), F32)] * 7,
        compiler_params=_cparams("parallel", "parallel"),
        name="rwprep",
    )(rw, rw, first, mu, w0, w2p, a0, a2p, g2, k_k, k_a, bd_ones)


def _rwchunk_kernel(r_ref, ld_ref, k_ref, v_ref, al_ref, be_ref, y_ref, st_ref, z_ref, *, nb, npb):
    c = RW_CHUNK
    ci = pl.program_id(1)

    @pl.when(ci == 0)
    def _():
        z_ref[...] = jnp.zeros_like(z_ref)

    ti = lax.broadcasted_iota(I32, (c, c), 0)
    si = lax.broadcasted_iota(I32, (c, c), 1)
    low_incl = si <= ti
    low_strict = si < ti
    diag = si == ti
    tri = jnp.where(low_incl, 1.0, 0.0).astype(BF16)
    eye = jnp.where(diag, 1.0, 0.0)
    head0 = lax.broadcasted_iota(I32, (1, LANES), 1) < HEAD_DIM
    hsels = (head0, jnp.logical_not(head0))
    same_head = (ti < HEAD_DIM) == (si < HEAD_DIM)
    nt = (((1,), (1,)), ((), ()))
    batches = range(nb * npb)
    chains = [(b, h) for b in batches for h in range(2)]
    lanes_of = lambda b: pl.ds((b // nb) * LANES, LANES)
    r_v, ld_v, k_v, v_v, al_v, be_v = ([ref[b % nb, :, lanes_of(b)] for b in batches]
                                       for ref in (r_ref, ld_ref, k_ref, v_ref, al_ref, be_ref))

    cum = [_xdot(tri, ld_v[b]) for b in batches]
    tot = [cm[c - 1:c, :] for cm in cum]
    e_neg = [jnp.exp(-cm) for cm in cum]
    at = [al_v[b] * jnp.exp(cum[b] - ld_v[b]) for b in batches]
    rt = [r_v[b] * jnp.exp(cum[b]) for b in batches]
    rhs_t = [jnp.concatenate([be_v[b] * e_neg[b], k_v[b] * e_neg[b]], axis=0).astype(BF16) for b in batches]
    vb = [v_v[b].astype(BF16) for b in batches]
    at_h = [jnp.where(hsels[h], at[b], 0.0) for b, h in chains]
    rt_h = [jnp.where(hsels[h], rt[b], 0.0) for b, h in chains]
    a4 = [lax.dot_general(jnp.concatenate([a, r], axis=0).astype(BF16), rhs_t[b], nt, preferred_element_type=F32)
          for (b, h), a, r in zip(chains, at_h, rt_h)]
    a_ab = [jnp.where(low_strict, m[:c, :c], 0.0) for m in a4]
    a_ak = [jnp.where(low_strict, m[:c, c:], 0.0).astype(BF16) for m in a4]
    a_r = [jnp.concatenate([jnp.where(low_incl, m[c:, :c], 0.0), jnp.where(low_incl, m[c:, c:], 0.0)],
                           axis=1).astype(BF16) for m in a4]
    levels = int(np.log2(c))
    pw = [m.astype(BF16) for m in a_ab]
    pw = [jnp.dot(m, m, preferred_element_type=F32).astype(BF16) for m in pw]
    inv = [eye + m for m in a_ab]
    for _ in range(1, levels - 1):
        x2 = [jnp.dot(m, jnp.concatenate([m, i.astype(BF16)], axis=1), preferred_element_type=F32)
              for m, i in zip(pw, inv)]
        pw = [m[:, :c].astype(BF16) for m in x2]
        inv = [i + m[:, c:] for i, m in zip(inv, x2)]
    inv = [i + jnp.dot(m, i.astype(BF16), preferred_element_type=F32) for i, m in zip(inv, pw)]
    akv = [jnp.dot(m, vb[b], preferred_element_type=F32) for (b, h), m in zip(chains, a_ak)]
    x = [_dot(i, jnp.concatenate([kv, a], axis=1)) for i, kv, a in zip(inv, akv, at_h)]
    u0_h = [m[:, :LANES] for m in x]
    at2_h = [m[:, LANES:] for m in x]
    y0_h = [jnp.dot(ar, jnp.concatenate([u0, v_v[b]], axis=0).astype(BF16), preferred_element_type=F32)
            for (b, h), ar, u0 in zip(chains, a_r, u0_h)]
    rt2_h = [r + jnp.dot(ar[:, :c], a2.astype(BF16), preferred_element_type=F32)
             for r, ar, a2 in zip(rt_h, a_r, at2_h)]
    z = [z_ref[b] for b in batches]
    uy = [_dot(jnp.concatenate([at2_h[2 * b] + at2_h[2 * b + 1], rt2_h[2 * b] + rt2_h[2 * b + 1]], axis=0), z[b])
          for b in batches]
    u = [uy[b][:c] + jnp.where(head0, u0_h[2 * b], u0_h[2 * b + 1]) for b in batches]
    for b in batches:
        y_ref[b % nb, :, lanes_of(b)] = uy[b][c:] + jnp.where(head0, y0_h[2 * b], y0_h[2 * b + 1])
    e_end = [jnp.exp(tot[b] - cum[b]) for b in batches]
    lhs_t = [jnp.concatenate([be_v[b] * e_end[b], k_v[b] * e_end[b]], axis=0) for b in batches]
    zadd = [_dot(lhs_t[b].T, jnp.concatenate([u[b], v_v[b]], axis=0)) for b in batches]
    for b in batches:
        dcol = jnp.sum(jnp.where(diag, jnp.broadcast_to(jnp.exp(tot[b]), (c, c)), 0.0), axis=1, keepdims=True)
        z_ref[b] = dcol * z[b] + jnp.where(same_head, zadd[b], 0.0)

    @pl.when(ci == pl.num_programs(1) - 1)
    def _():
        for b in batches:
            s = z_ref[b].T
            st_ref[b % nb, 2 * (b // nb)] = s[:HEAD_DIM, :HEAD_DIM]
            st_ref[b % nb, 2 * (b // nb) + 1] = s[HEAD_DIM:, HEAD_DIM:]


def _rwchunk(r, ld, k, v, al, be):
    nb, t, rwid = r.shape
    c = RW_CHUNK
    npb = RW_PAIRS
    assert t % c == 0 and c == LANES and rwid % (npb * LANES) == 0
    ngrp = rwid // (npb * LANES)
    seq = pl.BlockSpec((nb, c, npb * LANES), lambda hp, ci: (0, ci, hp))
    return pl.pallas_call(
        functools.partial(_rwchunk_kernel, nb=nb, npb=npb),
        grid=(ngrp, t // c),
        in_specs=[seq] * 6,
        out_specs=[seq, pl.BlockSpec((nb, 2 * npb, HEAD_DIM, HEAD_DIM), lambda hp, ci: (0, hp, 0, 0))],
        out_shape=[jax.ShapeDtypeStruct((nb, t, rwid), F32),
                   jax.ShapeDtypeStruct((nb, 2 * npb * ngrp, HEAD_DIM, HEAD_DIM), F32)],
        scratch_shapes=[pltpu.VMEM((nb * npb, LANES, LANES), F32)],
        compiler_params=_cparams("parallel", "arbitrary"),
        name="rwchunk",
    )(r, ld, k, v, al, be)


def _col(row, width):
    return jnp.broadcast_to(row, (LANES, width)).T[:, 0:1]


def _row(col, width):
    return jnp.broadcast_to(col, (width, LANES)).T[0:1, :]


def _rwstep_kernel(r_ref, ld_ref, k_ref, v_ref, al_ref, be_ref, s_ref, y_ref, so_ref, *, bb, nh):
    rwid = nh * HEAD_DIM

    def one(b, carry):
        r, dcy, k, al, be = (ref[b] for ref in (r_ref, ld_ref, k_ref, al_ref, be_ref))
        dcy = jnp.exp(dcy)
        v_col = _col(v_ref[b], rwid)
        heads = range(nh)
        hs = [slice(h * HEAD_DIM, (h + 1) * HEAD_DIM) for h in heads]
        st = [s_ref[b, h] for h in heads]
        sa = [jnp.sum(st[h] * al[:, hs[h]], axis=1, keepdims=True) for h in heads]
        st = [st[h] * dcy[:, hs[h]] + sa[h] * be[:, hs[h]] + v_col[hs[h], :] * k[:, hs[h]] for h in heads]
        for h in heads:
            so_ref[b, h] = st[h]
        ys = [jnp.sum(st[h] * r[:, hs[h]], axis=1, keepdims=True) for h in heads]
        y_ref[b] = _row(jnp.concatenate(ys, axis=0), rwid)
        return carry

    lax.fori_loop(0, bb, one, 0, unroll=2)


def _rwstep(r, ld, k, v, al, be, state):
    nb, _, rwid = r.shape
    nh = state.shape[1]
    bb = 8
    row = pl.BlockSpec((bb, 1, rwid), lambda i: (i, 0, 0))
    st = pl.BlockSpec((bb, nh, HEAD_DIM, HEAD_DIM), lambda i: (i, 0, 0, 0))
    return pl.pallas_call(
        functools.partial(_rwstep_kernel, bb=bb, nh=nh),
        grid=(nb // bb,),
        in_specs=[row] * 6 + [st],
        out_specs=[row, st],
        out_shape=[jax.ShapeDtypeStruct((nb, 1, rwid), F32), jax.ShapeDtypeStruct(state.shape, F32)],
        compiler_params=_cparams("parallel"),
        name="rwstep",
    )(r, ld, k, v, al, be, state)


def _outproj_kernel(x_ref, at_ref, y_ref, r_ref, k_ref, v_ref, g_ref, gt_ref, sh_ref, sc_ref, g2_ref,
                    wt_ref, wb_ref, rk_ref, lw_ref, lb_ref, bdm_ref, bd1_ref, rwt_ref, rb_ref, *rest, n_slots):
    x1_ref, pos_ref, wt4_ref, cnt_ref, xs_ref, ws_ref, hb_ref = rest[-7:]
    y = y_ref[...]
    bdm = bdm_ref[...]
    mean = _dot_x(y, bdm)
    yc = y - mean
    var = _dot_x(yc * yc, bdm)
    yn = yc * lax.rsqrt(var + GN_EPS) * lw_ref[...] + lb_ref[...]
    v = v_ref[...]
    bonus = _dot_x(r_ref[...] * k_ref[...] * rk_ref[...], bd1_ref[...]) * v
    rw = (yn + bonus) * g_ref[...]
    mix = _mm(at_ref[...], wt_ref[...]) + _mm(rw, wb_ref[...])
    x1 = x_ref[...] + gt_ref[...] * mix
    x1_ref[...] = x1
    ms = jnp.mean(x1 * x1, axis=-1, keepdims=True)
    h2 = x1 * lax.rsqrt(ms + NORM_EPS) * g2_ref[...] * (1.0 + sc_ref[...]) + sh_ref[...]
    pos, wts, cnt = _route_tile(_dot3_nt(rwt_ref[...], h2) + rb_ref[...])
    for kk in range(TOP_K):
        pos_ref[pl.ds(kk, 1), :] = pos[kk]
        wt4_ref[pl.ds(kk, 1), :] = wts[kk]
    cnt_ref[...] = jnp.broadcast_to(cnt, cnt_ref.shape)
    hb_ref[...] = h2.astype(BF16)
    used = jnp.sum(jnp.ceil(cnt / MOE_CHUNK) * MOE_CHUNK).astype(I32)
    _sort_tile(pos, wts, hb_ref, xs_ref, ws_ref, (used + SLOT_ROWS - 1) // SLOT_ROWS)


def _outproj(x3, attn, y, r, k, v, g, mod3, g2, wt_bf, wb_bf, r_k, ln_w, ln_b, bd_mean, bd_ones, rwt, rb, tm,
             n_slots, n_rows, tile0=0, bufs=None):
    nb, t, d = x3.shape
    rmod = mod3.shape[1]
    aw = attn.shape[2]
    ne = rwt.shape[0]
    nt = t // tm
    grid = (nb, nt)
    row = lambda b, i: (b, i, 0)
    const = lambda b, i: (0, 0)
    tile = lambda b, i: (tile0 + b * nt + i, 0)
    mod_spec = lambda s: pl.BlockSpec((None, rmod, d), (lambda b, i: (b, 0, s)) if rmod == 1 else (lambda b, i: (b, i, s)))
    half = pl.BlockSpec((None, tm, aw), row)
    vec = pl.BlockSpec((1, aw), const)
    tok4 = pl.BlockSpec((TOP_K, tm), lambda b, i: (0, b * nt + i))
    bufs = () if bufs is None else tuple(bufs)
    n_in = 20
    return pl.pallas_call(
        functools.partial(_outproj_kernel, n_slots=n_slots),
        grid=grid,
        in_specs=[pl.BlockSpec((None, tm, d), row)] + [half] * 6 + [mod_spec(2), mod_spec(3), mod_spec(4),
                  pl.BlockSpec((1, d), const), pl.BlockSpec((aw, d), const), pl.BlockSpec((aw, d), const),
                  vec, vec, vec, pl.BlockSpec((aw, aw), const), pl.BlockSpec((aw, aw), const),
                  pl.BlockSpec((ne, d), const), pl.BlockSpec((ne, 1), const)]
                 + [pl.BlockSpec(memory_space=pl.ANY)] * len(bufs),
        out_specs=[pl.BlockSpec((None, tm, d), row), tok4, tok4,
                   pl.BlockSpec((None, ne, LANES), lambda b, i: (b * nt + i, 0, 0)),
                   pl.BlockSpec((n_slots, d), tile), pl.BlockSpec((n_slots, LANES), tile)],
        out_shape=[jax.ShapeDtypeStruct((nb, t, d), F32), jax.ShapeDtypeStruct((TOP_K, nb * t), I32),
                   jax.ShapeDtypeStruct((TOP_K, nb * t), F32), jax.ShapeDtypeStruct((nb * nt, ne, LANES), F32),
                   jax.ShapeDtypeStruct((n_rows, d), BF16), jax.ShapeDtypeStruct((n_rows, LANES), F32)],
        scratch_shapes=[pltpu.VMEM((tm, d), BF16)],
        input_output_aliases={n_in: 4, n_in + 1: 5} if bufs else {},
        compiler_params=_cparams("parallel", "parallel"),
        name="outproj",
    )(x3, attn, y, r, k, v, g, mod3, mod3, mod3, g2, wt_bf, wb_bf, r_k, ln_w, ln_b, bd_mean, bd_ones, rwt, rb, *bufs)


def _route_tile(lg):
    ne, tn = lg.shape
    eidx = lax.broadcasted_iota(I32, (ne, tn), 0).astype(F32)
    vals, hots = [], []
    for _ in range(TOP_K):
        mx = jnp.max(lg, axis=0, keepdims=True)
        pick = jnp.min(jnp.where(lg == mx, eidx, float(ne)), axis=0, keepdims=True)
        hot = eidx == pick
        vals.append(mx)
        hots.append(hot)
        lg = jnp.where(hot, -jnp.inf, lg)
    ex = [jnp.exp(vv - vals[0]) for vv in vals]
    den = ex[0] + ex[1] + ex[2] + ex[3]
    wts = [e / den for e in ex]
    hot_all = jnp.zeros((ne, tn), F32)
    for hot in hots:
        hot_all = hot_all + jnp.where(hot, 1.0, 0.0)
    ri = lax.broadcasted_iota(I32, (tn, tn), 0)
    cj = lax.broadcasted_iota(I32, (tn, tn), 1)
    upper = jnp.where(ri <= cj, 1.0, 0.0).astype(BF16)
    before = jnp.dot(hot_all.astype(BF16), upper, preferred_element_type=F32) - hot_all
    cnt = jnp.sum(hot_all, axis=1, keepdims=True)
    padded = jnp.ceil(cnt / MOE_CHUNK) * MOE_CHUNK
    er = lax.broadcasted_iota(I32, (ne, ne), 0)
    ec = lax.broadcasted_iota(I32, (ne, ne), 1)
    lower_strict = jnp.where(ec < er, 1.0, 0.0)
    off = _xdot(lower_strict, jnp.broadcast_to(padded, (ne, LANES)))[:, 0:1]
    pos = [jnp.sum(jnp.where(hot, off + before, 0.0), axis=0, keepdims=True).astype(I32) for hot in hots]
    return pos, wts, cnt


def _slot_capacity(ne):
    return -(-(TOP_K * ROUTE_TILE + ne * (MOE_CHUNK - 1)) // SLOT_ROWS) * SLOT_ROWS


def _sort_tile(pos, wts, h_ref, xs_ref, ws_ref, n_steps):
    tn = h_ref.shape[0]

    def rows(c, carry):
        h = h_ref[...]
        r0 = pl.multiple_of(c * SLOT_ROWS, SLOT_ROWS)
        slot = r0 + lax.broadcasted_iota(I32, (SLOT_ROWS, tn), 0)
        pw = jnp.zeros((SLOT_ROWS, tn), F32)
        for kk in range(TOP_K):
            pw = pw + jnp.where(slot == pos[kk], wts[kk], 0.0)
        p = jnp.where(pw > 0.0, 1.0, 0.0)
        xs_ref[pl.ds(r0, SLOT_ROWS), :] = jnp.dot(p.astype(BF16), h, preferred_element_type=F32).astype(BF16)
        ws_ref[pl.ds(r0, SLOT_ROWS), :] = jnp.broadcast_to(jnp.sum(pw, axis=1, keepdims=True), (SLOT_ROWS, LANES))
        return carry

    lax.fori_loop(0, n_steps, rows, 0)


def _chunk_tables(cnt, n_slots, n_blocks):
    nt, ne = cnt.shape
    per_blk = MOE_BLOCK // MOE_CHUNK
    nch = -(-cnt // MOE_CHUNK)
    seg0 = (jnp.cumsum(nch, axis=1) - nch) + (jnp.arange(nt) * (n_slots // MOE_CHUNK))[:, None]
    cum_t = jnp.cumsum(nch, axis=0)
    total = cum_t[-1]
    blocks = -(-total // per_blk)
    blk_end = jnp.cumsum(blocks)
    blk = jnp.arange(n_blocks)
    blk_e = jnp.minimum(jnp.sum(blk_end[None, :] <= blk[:, None], axis=1), ne - 1).astype(I32)
    nact = blk_end[-1].reshape(1).astype(I32)
    pick_e = blk_e[:, None] == jnp.arange(ne)[None, :]
    of_e = lambda tab: jnp.sum(jnp.where(pick_e, tab[None, :], 0), axis=1)
    col_e = lambda tab: jnp.sum(jnp.where(pick_e[:, None, :], tab[None, :, :], 0), axis=2)
    q = ((blk - of_e(blk_end - blocks)) * per_blk)[:, None] + jnp.arange(per_blk)[None, :]
    live = (q < of_e(total)[:, None]) & (blk < nact[0])[:, None]
    t_n = jnp.minimum(jnp.sum(col_e(cum_t)[:, None, :] <= q[:, :, None], axis=2), nt - 1)
    pick_t = t_n[:, :, None] == jnp.arange(nt)[None, None, :]
    of_t = lambda tab: jnp.sum(jnp.where(pick_t, tab[:, None, :], 0), axis=2)
    src = jnp.where(live, of_t(col_e(seg0)) + q - of_t(col_e(cum_t - nch)), 0)
    n = blk[:, None] * per_blk + jnp.arange(per_blk)[None, :]
    spare = nt * (n_slots // MOE_CHUNK) + n % (N_OBUF * per_blk)
    dst = jnp.where(live, src, spare)
    return blk_e, nact, src.reshape(-1).astype(I32), dst.reshape(-1).astype(I32)


N_OBUF = 2


def _moe_kernel(be_ref, nact_ref, src_ref, dst_ref, xs_ref, ws_ref, wgu_ref, bgu_ref, wd_ref, bd_ref,
                out_ref, xbuf, wbuf, obuf, gsem, ssem, wgu_bf, wd_bf, *, dff):
    i = pl.program_id(0)
    nact = nact_ref[0]
    per_blk = MOE_BLOCK // MOE_CHUNK
    rows = lambda c: pl.ds(pl.multiple_of(c * MOE_CHUNK, MOE_CHUNK), MOE_CHUNK)

    def gathers(blk, slot):
        copies = []
        for m in range(per_blk):
            c = src_ref[blk * per_blk + m]
            copies.append(pltpu.make_async_copy(xs_ref.at[rows(c), :], xbuf.at[slot, rows(m), :], gsem.at[slot]))
            copies.append(pltpu.make_async_copy(ws_ref.at[rows(c), :], wbuf.at[slot, rows(m), :], gsem.at[slot]))
        return copies

    def scatters(blk, slot):
        return [pltpu.make_async_copy(obuf.at[slot, rows(m), :], out_ref.at[rows(dst_ref[blk * per_blk + m]), :],
                                      ssem.at[slot]) for m in range(per_blk)]

    @pl.when(i == 0)
    def _():
        for cp in gathers(0, 0):
            cp.start()

    @pl.when(i < nact)
    def _():
        slot = i % 2
        for cp in gathers(i, slot):
            cp.wait()

        @pl.when(i + 1 < nact)
        def _():
            for cp in gathers(i + 1, 1 - slot):
                cp.start()

        @pl.when((i == 0) | (be_ref[i] != be_ref[jnp.maximum(i - 1, 0)]))
        def _():
            wgu_bf[...] = wgu_ref[...].astype(BF16)
            wd_bf[...] = wd_ref[...].astype(BF16)

        gu = jnp.dot(xbuf[slot], wgu_bf[...], preferred_element_type=F32) + bgu_ref[...]
        gate = jnp.minimum(gu[:, :dff], SWIGLU_LIMIT)
        up = jnp.clip(gu[:, dff:], -SWIGLU_LIMIT, SWIGLU_LIMIT)
        act = (up + 1.0) * gate * _sigmoid(gate * SWIGLU_ALPHA)
        res = jnp.dot(act.astype(BF16), wd_bf[...], preferred_element_type=F32) + bd_ref[...]
        obuf[slot] = (res * wbuf[slot][:, 0:1]).astype(BF16)

        @pl.when(i >= 1)
        def _():
            for cp in scatters(i - 1, 1 - slot):
                cp.wait()

        for cp in scatters(i, slot):
            cp.start()

        @pl.when(i == nact - 1)
        def _():
            for cp in scatters(i, slot):
                cp.wait()


def _moe(blk_e, nact, src, dst, xs, ws, w_gu, b_gu, w_down, b_down, n_blocks):
    ne, d, dff2 = w_gu.shape
    dff = dff2 // 2
    blk = MOE_BLOCK
    grid_spec = pltpu.PrefetchScalarGridSpec(
        num_scalar_prefetch=4,
        grid=(n_blocks,),
        in_specs=[pl.BlockSpec(memory_space=pl.ANY), pl.BlockSpec(memory_space=pl.ANY),
                  pl.BlockSpec((None, d, dff2), lambda i, be, *_: (be[i], 0, 0)),
                  pl.BlockSpec((None, 1, dff2), lambda i, be, *_: (be[i], 0, 0)),
                  pl.BlockSpec((None, dff, d), lambda i, be, *_: (be[i], 0, 0)),
                  pl.BlockSpec((None, 1, d), lambda i, be, *_: (be[i], 0, 0))],
        out_specs=pl.BlockSpec(memory_space=pl.ANY),
        scratch_shapes=[pltpu.VMEM((2, blk, d), BF16), pltpu.VMEM((2, blk, LANES), F32),
                        pltpu.VMEM((N_OBUF, blk, d), BF16),
                        pltpu.SemaphoreType.DMA((2,)), pltpu.SemaphoreType.DMA((N_OBUF,)),
                        pltpu.VMEM((d, dff2), BF16), pltpu.VMEM((dff, d), BF16)],
    )
    return pl.pallas_call(
        functools.partial(_moe_kernel, dff=dff),
        grid_spec=grid_spec,
        out_shape=jax.ShapeDtypeStruct(xs.shape, BF16),
        compiler_params=_cparams("arbitrary"),
        name="moe",
    )(blk_e, nact, src, dst, xs, ws, w_gu, b_gu.reshape(ne, 1, dff2), w_down, b_down.reshape(ne, 1, d))


def _combine_kernel(pos_ref, used_ref, ys_ref, x1_ref, gt_ref, o_ref, p_ref, *, n_slots, always):
    tn = x1_ref.shape[0]
    used = jnp.max(used_ref[...])
    lane = lax.broadcasted_iota(I32, (tn, SLOT_ROWS), 1)
    pos = [jnp.broadcast_to(pos_ref[:, kk:kk + 1], (tn, SLOT_ROWS)) - lane for kk in range(TOP_K)]

    def cols(c, carry):
        c0 = pl.multiple_of(c * SLOT_ROWS, SLOT_ROWS)
        p = jnp.zeros((tn, SLOT_ROWS), F32)
        for kk in range(TOP_K):
            p = p + jnp.where(pos[kk] == c0, 1.0, 0.0)
        p_ref[:, pl.ds(c0, SLOT_ROWS)] = p.astype(BF16)
        return carry

    def blank(c, carry):
        p_ref[:, pl.ds(pl.multiple_of(c * SLOT_ROWS, SLOT_ROWS), SLOT_ROWS)] = jnp.zeros((tn, SLOT_ROWS), BF16)
        return carry

    n_live = (used + SLOT_ROWS - 1) // SLOT_ROWS
    lax.fori_loop(0, n_live, cols, 0)
    lax.fori_loop(n_live, n_slots // SLOT_ROWS, blank, 0)
    filled = always + lax.broadcasted_iota(I32, (n_slots - always, 1), 0) < used
    y = jnp.dot(p_ref[:, always:], jnp.where(filled, ys_ref[always:, :], jnp.zeros((), BF16)),
                preferred_element_type=F32)
    if always:
        y = y + jnp.dot(p_ref[:, :always], ys_ref[:always, :], preferred_element_type=F32)
    o_ref[...] = x1_ref[...] + gt_ref[...] * y


def _combine(pos_t, used, ys, x1, mod3, tile0, rows, n_slots):
    nb, t, d = x1.shape
    rmod = mod3.shape[1]
    nt = t // rows
    per_tile = ROUTE_TILE // rows
    row = lambda b, i: (b, i, 0)
    gate = pl.BlockSpec((None, rmod, d), (lambda b, i: (b, 0, 5)) if rmod == 1 else (lambda b, i: (b, i, 5)))
    return pl.pallas_call(
        functools.partial(_combine_kernel, n_slots=n_slots,
                          always=min(ROUTE_TILE, nb * t) * TOP_K // SLOT_ROWS * SLOT_ROWS),
        grid=(nb, nt),
        in_specs=[pl.BlockSpec((rows, TOP_K), lambda b, i: (b * nt + i, 0)),
                  pl.BlockSpec((None, SUBLANES, LANES), lambda b, i: (tile0 + (b * nt + i) // per_tile, 0, 0)),
                  pl.BlockSpec((n_slots, d), lambda b, i: (tile0 + (b * nt + i) // per_tile, 0)),
                  pl.BlockSpec((None, rows, d), row), gate],
        out_specs=pl.BlockSpec((None, rows, d), row),
        out_shape=jax.ShapeDtypeStruct((nb, t, d), F32),
        scratch_shapes=[pltpu.VMEM((rows, n_slots), BF16)],
        compiler_params=_cparams("parallel", "parallel"),
        name="combine",
    )(pos_t, used, ys, x1, mod3)


def _rope_tables(pos, n_heads):
    half = HEAD_DIM // 2
    inv_freq = 1.0 / (ROPE_THETA ** (jnp.arange(0, HEAD_DIM, 2, dtype=F32) / HEAD_DIM))
    ang = pos.astype(F32)[:, None] * inv_freq[None, :]
    cos, sin = jnp.cos(ang), jnp.sin(ang)
    del half
    return (jnp.tile(jnp.concatenate([cos, cos], axis=-1), (1, n_heads)),
            jnp.tile(jnp.concatenate([-sin, sin], axis=-1), (1, n_heads)))


def _block_diag(width, value):
    h = np.arange(width) // HEAD_DIM
    return jnp.asarray(np.where(h[:, None] == h[None, :], value, 0.0), F32)


def kernel(x_prompt, x_sample, cache_k, cache_v, state_wkv, state_shift, c_prompt, c_sample, w_ada, b_ada, norm1_g, norm2_g, w_in, q_norm_g, k_norm_g, rwkv_mu, rwkv_w0, rwkv_w2, rwkv_a0, rwkv_a2, rwkv_g2, rwkv_k_k, rwkv_k_a, rwkv_r_k, rwkv_ln_w, rwkv_ln_b, w_out, router_w, router_b, moe_w_gu, moe_b_gu, moe_w_down, moe_b_down):
    nbp, t, d = x_prompt.shape
    nbs, ts, _ = x_sample.shape
    depth = w_ada.shape[0]
    assert depth == 1 and ts == 1
    n_heads = cache_k.shape[3]
    aw = n_heads * HEAD_DIM
    rwid = rwkv_w0.shape[1]
    rwc = rwkv_mu.shape[1]
    past = cache_k.shape[2]
    ne = router_w.shape[2]
    keep = min(MAX_WINDOW, t)
    lyr = 0

    w_in_bf = w_in[lyr].astype(BF16)
    wt_bf = w_out[lyr][:aw].astype(BF16)
    wb_bf = w_out[lyr][aw:].astype(BF16)
    g1 = norm1_g[lyr].reshape(1, d)
    g2 = norm2_g[lyr].reshape(1, d)
    qg = jnp.tile(q_norm_g[lyr], n_heads).reshape(1, aw)
    kg = jnp.tile(k_norm_g[lyr], n_heads).reshape(1, aw)
    bd_mean_a = _block_diag(aw, 1.0 / HEAD_DIM)
    bd_mean_r = _block_diag(rwid, 1.0 / HEAD_DIM)
    bd_ones_r = _block_diag(rwid, 1.0)
    dl = rwkv_w2.shape[1]
    w2p = jnp.zeros((LANES, rwid), F32).at[:dl].set(rwkv_w2[lyr])
    a2p = jnp.zeros((LANES, rwid), F32).at[dl:dl + rwkv_a2.shape[1]].set(rwkv_a2[lyr])
    vec = lambda a: a[lyr].reshape(1, -1)
    rwt = router_w[lyr].T
    rb = router_b[lyr].reshape(ne, 1)

    rows_c = nbp + nbs
    rows_pad = -(-rows_c // 8) * 8
    c_all = jnp.zeros((rows_pad, d), F32).at[:nbp].set(c_prompt).at[nbp:rows_c].set(c_sample)
    mod = _ada(c_all, w_ada[lyr], b_ada[lyr])
    mod_p = mod[:nbp].reshape(nbp, 1, 6 * d)
    mod_s = mod[nbp:rows_c].reshape(1, nbs, 6 * d)
    xs3 = x_sample.reshape(1, nbs, d)

    cos_p, sin_p = _rope_tables(jnp.arange(t), n_heads)
    cos_s, sin_s = _rope_tables(jnp.full((nbs,), PAST_LEN), n_heads)
    rw_args = (vec(rwkv_mu), vec(rwkv_w0), w2p, vec(rwkv_a0), a2p, rwkv_g2[lyr], vec(rwkv_k_k), vec(rwkv_k_a), bd_ones_r)
    rw_args_p = tuple(a.astype(BF16) if i in (2, 4, 5) else a for i, a in enumerate(rw_args))
    qp, kp, vp, *pre_p, tail_p, kt_p, vt_p = _inproj(x_prompt, mod_p, g1, w_in_bf, cos_p, sin_p, qg, kg, bd_mean_a,
                                                     ROW_TILE, keep=keep, rw=(jnp.zeros((nbp, 1, rwc), F32), *rw_args_p))
    qs, ks, vs, rws = _inproj(xs3, mod_s, g1, w_in[lyr], cos_s, sin_s, qg, kg, bd_mean_a, nbs)

    attn_p = _attn_prompt(qp, kp, vp)
    as_rows = lambda a: a.reshape(nbs, 1, -1)
    cache_t = lambda cch: jnp.transpose(cch[lyr], (0, 2, 3, 1)).reshape(nbs, aw, past)
    attn_s = _attn_sample(as_rows(qs), as_rows(ks), as_rows(vs), cache_t(cache_k), cache_t(cache_v))

    pre_s = _rwprep(rws, state_shift[lyr].reshape(1, nbs, rwc), *rw_args, tm=nbs, whole_prev=True)
    r_p, ld_p, k_p, v_p, al_p, be_p, g_p = pre_p
    y_p, wkv_p = _rwchunk(r_p, ld_p, k_p, v_p, al_p, be_p)
    r_s, ld_s, k_s, v_s, al_s, be_s, g_s = pre_s
    y_s, wkv_s = _rwstep(*(as_rows(a) for a in (r_s, ld_s, k_s, v_s, al_s, be_s)), state_wkv[lyr])
    y_s = y_s.reshape(1, nbs, rwid)

    n_p = nbp * t
    n_valid = n_p + nbs
    assert ROW_TILE == ROUTE_TILE and n_p % ROUTE_TILE == 0 and nbs <= ROUTE_TILE and d == SUBLANES * LANES
    n_tiles = n_p // ROUTE_TILE + 1
    n_slots = _slot_capacity(ne)
    n_rows = n_tiles * n_slots + N_OBUF * MOE_BLOCK
    op_args = (rwkv_r_k[lyr].reshape(1, rwid), vec(rwkv_ln_w), vec(rwkv_ln_b), bd_mean_r, bd_ones_r, rwt, rb)
    x1_p, pos_p, _, cnt_p, *bufs = _outproj(x_prompt, attn_p, y_p, r_p, k_p, v_p, g_p, mod_p, g2, wt_bf, wb_bf,
                                            *op_args, tm=ROW_TILE, n_slots=n_slots, n_rows=n_rows)
    x1_s, pos_s, _, cnt_s, xs, ws = _outproj(xs3, attn_s.reshape(1, nbs, aw), y_s, r_s, k_s, v_s, g_s, mod_s, g2,
                                             w_out[lyr][:aw], w_out[lyr][aw:], *op_args, tm=nbs, n_slots=n_slots,
                                             n_rows=n_rows, tile0=n_p // ROUTE_TILE, bufs=bufs)

    cnt = jnp.concatenate([cnt_p, cnt_s])[:, :, 0].astype(I32)
    per_blk = MOE_BLOCK // MOE_CHUNK
    n_blocks = -(-(n_valid * TOP_K // MOE_CHUNK + n_tiles * ne) // per_blk) + ne
    blk_e, nact, src, dst = _chunk_tables(cnt, n_slots, n_blocks)
    ys = _moe(blk_e, nact, src, dst, xs, ws, moe_w_gu[lyr], moe_b_gu[lyr], moe_w_down[lyr], moe_b_down[lyr], n_blocks)
    used = jnp.sum(-(-cnt // MOE_CHUNK) * MOE_CHUNK, axis=1)
    used = jnp.broadcast_to(used[:, None, None], (n_tiles, SUBLANES, LANES))
    y_prompt = _combine(pos_p.T, used, ys, x1_p, mod_p, 0, ROUTE_TILE, n_slots)
    y_sample = _combine(pos_s.T, used, ys, x1_s, mod_s, n_p // ROUTE_TILE, nbs, n_slots)

    kept = lambda a: jnp.transpose(a.reshape(nbp, n_heads, HEAD_DIM, keep), (0, 3, 1, 2))[None]
    return (y_prompt, y_sample.reshape(nbs, ts, d), kept(kt_p), kept(vt_p), wkv_p[None], tail_p[:, SUBLANES - 1][None],
            ks.reshape(nbs, ts, n_heads, HEAD_DIM)[None], vs.reshape(nbs, ts, n_heads, HEAD_DIM)[None],
            wkv_s[None], rws.reshape(nbs, rwc)[None])
```
